```python
import jax, jax.numpy as jnp
from jax import lax
import numpy as np

D_MODEL = 1024
BATCH = 16
SEQ = 2048
DEPTH = 2

D_MIX = D_MODEL
D_CF = D_MIX // 4
D_SC = D_MIX // 4
D_ATT = D_MIX - D_CF - D_SC
HEAD_DIM = 64
N_ATT_HEADS = D_ATT // HEAD_DIM
CF_WIDTH = 31
SC_WIDTH = 3
Q_BLOCK = 128
EPS = 1e-6
SPLITS = (D_CF, D_CF, D_CF, D_SC, D_SC, D_SC, D_SC, D_ATT, D_ATT, D_ATT, D_ATT, N_ATT_HEADS)
N_IN = sum(SPLITS)
SPLIT_IDX = tuple(int(i) for i in np.cumsum(SPLITS)[:-1])

kernel_name = 'hymba_style_conformer_shortconv_fox_hybrid'


def rmsnorm(x, g):
    xf = x.astype(jnp.float32)
    y = xf * lax.rsqrt(jnp.mean(xf * xf, axis=-1, keepdims=True) + EPS)
    return (y * g.astype(jnp.float32)).astype(x.dtype)


def layernorm(x, g, b):
    xf = x.astype(jnp.float32)
    mu = jnp.mean(xf, axis=-1, keepdims=True)
    xc = xf - mu
    y = xc * lax.rsqrt(jnp.mean(xc * xc, axis=-1, keepdims=True) + EPS)
    return (y * g.astype(jnp.float32) + b.astype(jnp.float32)).astype(x.dtype)


def causal_depthwise_conv(x, w):
    width, c = w.shape
    return lax.conv_general_dilated(
        x, w[:, None, :].astype(x.dtype), window_strides=(1,),
        padding=[(width - 1, 0)], dimension_numbers=('NWC', 'WIO', 'NWC'),
        feature_group_count=c)


def forgetting_attention(q, k, v, log_f):
    seq = q.shape[1]
    scale = 1.0 / float(np.sqrt(HEAD_DIM))
    c = jnp.transpose(lax.cumsum(log_f, axis=1), (0, 2, 1))
    outs = []
    for start in range(0, seq, Q_BLOCK):
        end = start + Q_BLOCK
        qb = q[:, start:end].astype(jnp.float32)
        kb = k[:, :end].astype(jnp.float32)
        vb = v[:, :end]
        s = jnp.einsum('bqhd,bkhd->bhqk', qb, kb) * scale
        decay = c[:, :, start:end, None] - c[:, :, None, :end]
        qpos = jnp.arange(start, end)[:, None]
        kpos = jnp.arange(end)[None, :]
        s = jnp.where(kpos <= qpos, s + decay, -jnp.inf)
        p = jax.nn.softmax(s, axis=-1)
        outs.append(jnp.einsum('bhqk,bkhd->bqhd', p.astype(v.dtype), vb))
    return jnp.concatenate(outs, axis=1)


def hybrid_layer(x, norm_g, w_in, b_f, cf_dw, cf_dw_b, cf_ln_g, cf_ln_b, cf_pw,
                 sc_dw, q_norm_g, k_norm_g, w_out):
    bsz, seq, _ = x.shape
    h = rmsnorm(x, norm_g)
    proj = h @ w_in
    (cf_a, cf_g, cf_z, sc_b, sc_c, sc_x, sc_z,
     q, k, v, att_z, f_logit) = jnp.split(proj, SPLIT_IDX, axis=-1)

    u = cf_a * jax.nn.sigmoid(cf_g)
    u = causal_depthwise_conv(u, cf_dw) + cf_dw_b
    u = jax.nn.silu(layernorm(u, cf_ln_g, cf_ln_b))
    y_cf = (u @ cf_pw) * jax.nn.silu(cf_z)

    y_sc = sc_b * causal_depthwise_conv(sc_c * sc_x, sc_dw) * jax.nn.silu(sc_z)

    q = rmsnorm(q.reshape(bsz, seq, N_ATT_HEADS, HEAD_DIM), q_norm_g)
    k = rmsnorm(k.reshape(bsz, seq, N_ATT_HEADS, HEAD_DIM), k_norm_g)
    v = v.reshape(bsz, seq, N_ATT_HEADS, HEAD_DIM)
    log_f = jax.nn.log_sigmoid((f_logit + b_f).astype(jnp.float32))
    o = forgetting_attention(q, k, v, log_f).reshape(bsz, seq, D_ATT)
    y_att = o * jax.nn.silu(att_z)

    mixed = jnp.concatenate([y_cf, y_sc, y_att], axis=-1)
    return x + mixed @ w_out


def _fwd_setup_inputs(seed: int = 0) -> dict:
    key = jax.random.key(seed)
    ks = jax.random.split(key, 14)
    f32 = jnp.float32
    x = jax.random.normal(ks[0], (BATCH, SEQ, D_MODEL), f32)
    norm_g = 1.0 + 0.05 * jax.random.normal(ks[1], (DEPTH, D_MODEL), f32)
    w_in = jax.random.normal(ks[2], (DEPTH, D_MODEL, N_IN), f32) * D_MODEL ** -0.5
    b_f = 2.0 + 0.5 * jax.random.normal(ks[3], (DEPTH, N_ATT_HEADS), f32)
    cf_dw = jax.random.normal(ks[4], (DEPTH, CF_WIDTH, D_CF), f32) * CF_WIDTH ** -0.5
    cf_dw_b = 0.02 * jax.random.normal(ks[5], (DEPTH, D_CF), f32)
    cf_ln_g = 1.0 + 0.05 * jax.random.normal(ks[6], (DEPTH, D_CF), f32)
    cf_ln_b = 0.02 * jax.random.normal(ks[7], (DEPTH, D_CF), f32)
    cf_pw = jax.random.normal(ks[8], (DEPTH, D_CF, D_CF), f32) * D_CF ** -0.5
    sc_dw = jax.random.normal(ks[9], (DEPTH, SC_WIDTH, D_SC), f32) * SC_WIDTH ** -0.5
    q_norm_g = 1.0 + 0.05 * jax.random.normal(ks[10], (DEPTH, N_ATT_HEADS, HEAD_DIM), f32)
    k_norm_g = 1.0 + 0.05 * jax.random.normal(ks[11], (DEPTH, N_ATT_HEADS, HEAD_DIM), f32)
    w_out = jax.random.normal(ks[12], (DEPTH, D_MIX, D_MODEL), f32) * D_MIX ** -0.5
    return {'x': x, 'norm_g': norm_g, 'w_in': w_in, 'b_f': b_f, 'cf_dw': cf_dw,
            'cf_dw_b': cf_dw_b, 'cf_ln_g': cf_ln_g, 'cf_ln_b': cf_ln_b, 'cf_pw': cf_pw,
            'sc_dw': sc_dw, 'q_norm_g': q_norm_g, 'k_norm_g': k_norm_g, 'w_out': w_out}


def _fwd_reference(x, norm_g, w_in, b_f, cf_dw, cf_dw_b, cf_ln_g, cf_ln_b, cf_pw,
              sc_dw, q_norm_g, k_norm_g, w_out):
    for l in range(DEPTH):
        x = hybrid_layer(x, norm_g[l], w_in[l], b_f[l], cf_dw[l], cf_dw_b[l],
                         cf_ln_g[l], cf_ln_b[l], cf_pw[l], sc_dw[l],
                         q_norm_g[l], k_norm_g[l], w_out[l])
    return x


import jax as _jax
import jax.numpy as _jnp

TWIN_FORMAT = 'train_step'
FWD_PARAMS = ['x', 'norm_g', 'w_in', 'b_f', 'cf_dw', 'cf_dw_b', 'cf_ln_g', 'cf_ln_b', 'cf_pw', 'sc_dw', 'q_norm_g', 'k_norm_g', 'w_out']
TWIN_WEIGHTS = ['norm_g', 'w_in', 'b_f', 'cf_dw', 'cf_dw_b', 'cf_ln_g', 'cf_ln_b', 'cf_pw', 'sc_dw', 'q_norm_g', 'k_norm_g', 'w_out']
TWIN_DIFF_INPUT = 'x'
TWIN_INPUTS = ['x', 'norm_g', 'w_in', 'b_f', 'cf_dw', 'cf_dw_b', 'cf_ln_g', 'cf_ln_b', 'cf_pw', 'sc_dw', 'q_norm_g', 'k_norm_g', 'w_out', 'loss_target', 'm_norm_g', 'm_w_in', 'm_b_f', 'm_cf_dw', 'm_cf_dw_b', 'm_cf_ln_g', 'm_cf_ln_b', 'm_cf_pw', 'm_sc_dw', 'm_q_norm_g', 'm_k_norm_g', 'm_w_out', 'v_norm_g', 'v_w_in', 'v_b_f', 'v_cf_dw', 'v_cf_dw_b', 'v_cf_ln_g', 'v_cf_ln_b', 'v_cf_pw', 'v_sc_dw', 'v_q_norm_g', 'v_k_norm_g', 'v_w_out']
TWIN_OUTPUTS = ['loss', 'grad_x', 'grad_norm_g', 'grad_w_in', 'grad_b_f', 'grad_cf_dw', 'grad_cf_dw_b', 'grad_cf_ln_g', 'grad_cf_ln_b', 'grad_cf_pw', 'grad_sc_dw', 'grad_q_norm_g', 'grad_k_norm_g', 'grad_w_out', 'delta_norm_g', 'delta_w_in', 'delta_b_f', 'delta_cf_dw', 'delta_cf_dw_b', 'delta_cf_ln_g', 'delta_cf_ln_b', 'delta_cf_pw', 'delta_sc_dw', 'delta_q_norm_g', 'delta_k_norm_g', 'delta_w_out', 'new_m_norm_g', 'new_m_w_in', 'new_m_b_f', 'new_m_cf_dw', 'new_m_cf_dw_b', 'new_m_cf_ln_g', 'new_m_cf_ln_b', 'new_m_cf_pw', 'new_m_sc_dw', 'new_m_q_norm_g', 'new_m_k_norm_g', 'new_m_w_out', 'new_v_norm_g', 'new_v_w_in', 'new_v_b_f', 'new_v_cf_dw', 'new_v_cf_dw_b', 'new_v_cf_ln_g', 'new_v_cf_ln_b', 'new_v_cf_pw', 'new_v_sc_dw', 'new_v_q_norm_g', 'new_v_k_norm_g', 'new_v_w_out']
TWIN_LEAF_KINDS = {'loss': 'loss', 'grad_x': 'grad_x', 'grad_norm_g': 'grad_w', 'grad_w_in': 'grad_w', 'grad_b_f': 'grad_w', 'grad_cf_dw': 'grad_w', 'grad_cf_dw_b': 'grad_w', 'grad_cf_ln_g': 'grad_w', 'grad_cf_ln_b': 'grad_w', 'grad_cf_pw': 'grad_w', 'grad_sc_dw': 'grad_w', 'grad_q_norm_g': 'grad_w', 'grad_k_norm_g': 'grad_w', 'grad_w_out': 'grad_w', 'delta_norm_g': 'delta_w', 'delta_w_in': 'delta_w', 'delta_b_f': 'delta_w', 'delta_cf_dw': 'delta_w', 'delta_cf_dw_b': 'delta_w', 'delta_cf_ln_g': 'delta_w', 'delta_cf_ln_b': 'delta_w', 'delta_cf_pw': 'delta_w', 'delta_sc_dw': 'delta_w', 'delta_q_norm_g': 'delta_w', 'delta_k_norm_g': 'delta_w', 'delta_w_out': 'delta_w', 'new_m_norm_g': 'new_m', 'new_m_w_in': 'new_m', 'new_m_b_f': 'new_m', 'new_m_cf_dw': 'new_m', 'new_m_cf_dw_b': 'new_m', 'new_m_cf_ln_g': 'new_m', 'new_m_cf_ln_b': 'new_m', 'new_m_cf_pw': 'new_m', 'new_m_sc_dw': 'new_m', 'new_m_q_norm_g': 'new_m', 'new_m_k_norm_g': 'new_m', 'new_m_w_out': 'new_m', 'new_v_norm_g': 'new_v', 'new_v_w_in': 'new_v', 'new_v_b_f': 'new_v', 'new_v_cf_dw': 'new_v', 'new_v_cf_dw_b': 'new_v', 'new_v_cf_ln_g': 'new_v', 'new_v_cf_ln_b': 'new_v', 'new_v_cf_pw': 'new_v', 'new_v_sc_dw': 'new_v', 'new_v_q_norm_g': 'new_v', 'new_v_k_norm_g': 'new_v', 'new_v_w_out': 'new_v'}


def _forward(args):
    return _fwd_reference(*[args[k] for k in FWD_PARAMS])


def _output_shape():
    out = _jax.eval_shape(lambda: _forward(_fwd_setup_inputs(0)))
    return out.shape, out.dtype

N_MICROBATCH = 1
ADAM_LR = 0.001
ADAM_B1 = 0.9
ADAM_B2 = 0.999
ADAM_EPS = 1e-08
ADAM_WD = 0.01
ADAM_STEP = 10
PER_EXAMPLE_BATCH_AXIS = {'x': 0, 'loss_target': 0}
SHARED_INPUTS = []
_WEIGHT_DTYPES = {'norm_g': _jnp.float32, 'w_in': _jnp.float32, 'b_f': _jnp.float32, 'cf_dw': _jnp.float32, 'cf_dw_b': _jnp.float32, 'cf_ln_g': _jnp.float32, 'cf_ln_b': _jnp.float32, 'cf_pw': _jnp.float32, 'sc_dw': _jnp.float32, 'q_norm_g': _jnp.float32, 'k_norm_g': _jnp.float32, 'w_out': _jnp.float32}
MOMENT_SCALE = {'norm_g': 1.519538e+01, 'w_in': 4.044685e-01, 'b_f': 3.865810e+01, 'cf_dw': 2.452225e-01, 'cf_dw_b': 2.119909e+00, 'cf_ln_g': 4.889931e+00, 'cf_ln_b': 3.334273e+00, 'cf_pw': 4.374630e-01, 'sc_dw': 5.994489e+00, 'q_norm_g': 5.670458e-01, 'k_norm_g': 5.721855e-01, 'w_out': 2.550294e-01}


def _to_microbatches(a, axis):
    t = _jnp.moveaxis(a, axis, 0)
    t = t.reshape((N_MICROBATCH, t.shape[0] // N_MICROBATCH) + t.shape[1:])
    return _jnp.moveaxis(t, 1, axis + 1)


def setup_inputs(seed: int = 0) -> dict:
    inp = _fwd_setup_inputs(seed)
    key = _jax.random.fold_in(_jax.random.key(seed), 7919)
    shape, _ = _output_shape()
    out = dict(inp)
    out["loss_target"] = _jax.random.normal(_jax.random.fold_in(key, 0), shape, _jnp.float32)
    for i, name in enumerate(TWIN_WEIGHTS):
        w = inp[name].astype(_jnp.float32)
        if MOMENT_SCALE is None:
            s = _jnp.sqrt(_jnp.mean(_jnp.square(w)) + 1e-30)
        else:
            s = MOMENT_SCALE[name]
        km, kv = _jax.random.split(_jax.random.fold_in(key, i + 1))
        out[name] = w
        out["m_" + name] = s * _jax.random.normal(km, w.shape, _jnp.float32)
        out["v_" + name] = (s * s) * _jax.random.uniform(kv, w.shape, _jnp.float32, 0.5, 1.5)
    if N_MICROBATCH > 1:
        for name, axis in PER_EXAMPLE_BATCH_AXIS.items():
            out[name] = _to_microbatches(out[name], axis)
    return {'x': out['x'], 'norm_g': out['norm_g'], 'w_in': out['w_in'], 'b_f': out['b_f'], 'cf_dw': out['cf_dw'], 'cf_dw_b': out['cf_dw_b'], 'cf_ln_g': out['cf_ln_g'], 'cf_ln_b': out['cf_ln_b'], 'cf_pw': out['cf_pw'], 'sc_dw': out['sc_dw'], 'q_norm_g': out['q_norm_g'], 'k_norm_g': out['k_norm_g'], 'w_out': out['w_out'], 'loss_target': out['loss_target'], 'm_norm_g': out['m_norm_g'], 'm_w_in': out['m_w_in'], 'm_b_f': out['m_b_f'], 'm_cf_dw': out['m_cf_dw'], 'm_cf_dw_b': out['m_cf_dw_b'], 'm_cf_ln_g': out['m_cf_ln_g'], 'm_cf_ln_b': out['m_cf_ln_b'], 'm_cf_pw': out['m_cf_pw'], 'm_sc_dw': out['m_sc_dw'], 'm_q_norm_g': out['m_q_norm_g'], 'm_k_norm_g': out['m_k_norm_g'], 'm_w_out': out['m_w_out'], 'v_norm_g': out['v_norm_g'], 'v_w_in': out['v_w_in'], 'v_b_f': out['v_b_f'], 'v_cf_dw': out['v_cf_dw'], 'v_cf_dw_b': out['v_cf_dw_b'], 'v_cf_ln_g': out['v_cf_ln_g'], 'v_cf_ln_b': out['v_cf_ln_b'], 'v_cf_pw': out['v_cf_pw'], 'v_sc_dw': out['v_sc_dw'], 'v_q_norm_g': out['v_q_norm_g'], 'v_k_norm_g': out['v_k_norm_g'], 'v_w_out': out['v_w_out']}


def _loss(weights, diff, rest, loss_target):
    with _jax.named_scope("forward"):
        args = {**rest, TWIN_DIFF_INPUT: diff, **{k: w.astype(_WEIGHT_DTYPES[k]) for k, w in weights.items()}}
        y = _forward(args)
    with _jax.named_scope("loss_head"):
        err = _jnp.square(y.astype(_jnp.float32) - loss_target)
        return 0.5 * _jnp.sum(_jnp.mean(err, axis=-1)) if err.ndim else 0.5 * err


def _adamw(w, g, m, v):
    m = ADAM_B1 * m + (1.0 - ADAM_B1) * g
    v = ADAM_B2 * v + (1.0 - ADAM_B2) * _jnp.square(g)
    m_hat = m / (1.0 - ADAM_B1 ** ADAM_STEP)
    v_hat = v / (1.0 - ADAM_B2 ** ADAM_STEP)
    delta = -ADAM_LR * (m_hat / (_jnp.sqrt(v_hat) + ADAM_EPS) + ADAM_WD * w)
    return delta, m, v


def reference(x, norm_g, w_in, b_f, cf_dw, cf_dw_b, cf_ln_g, cf_ln_b, cf_pw, sc_dw, q_norm_g, k_norm_g, w_out, loss_target, m_norm_g, m_w_in, m_b_f, m_cf_dw, m_cf_dw_b, m_cf_ln_g, m_cf_ln_b, m_cf_pw, m_sc_dw, m_q_norm_g, m_k_norm_g, m_w_out, v_norm_g, v_w_in, v_b_f, v_cf_dw, v_cf_dw_b, v_cf_ln_g, v_cf_ln_b, v_cf_pw, v_sc_dw, v_q_norm_g, v_k_norm_g, v_w_out):
    given = dict(x=x, norm_g=norm_g, w_in=w_in, b_f=b_f, cf_dw=cf_dw, cf_dw_b=cf_dw_b, cf_ln_g=cf_ln_g, cf_ln_b=cf_ln_b, cf_pw=cf_pw, sc_dw=sc_dw, q_norm_g=q_norm_g, k_norm_g=k_norm_g, w_out=w_out, loss_target=loss_target, m_norm_g=m_norm_g, m_w_in=m_w_in, m_b_f=m_b_f, m_cf_dw=m_cf_dw, m_cf_dw_b=m_cf_dw_b, m_cf_ln_g=m_cf_ln_g, m_cf_ln_b=m_cf_ln_b, m_cf_pw=m_cf_pw, m_sc_dw=m_sc_dw, m_q_norm_g=m_q_norm_g, m_k_norm_g=m_k_norm_g, m_w_out=m_w_out, v_norm_g=v_norm_g, v_w_in=v_w_in, v_b_f=v_b_f, v_cf_dw=v_cf_dw, v_cf_dw_b=v_cf_dw_b, v_cf_ln_g=v_cf_ln_g, v_cf_ln_b=v_cf_ln_b, v_cf_pw=v_cf_pw, v_sc_dw=v_sc_dw, v_q_norm_g=v_q_norm_g, v_k_norm_g=v_k_norm_g, v_w_out=v_w_out)
    weights = {n: given[n] for n in TWIN_WEIGHTS}
    shared = {n: given[n] for n in SHARED_INPUTS}
    per_example = {n: given[n] for n in ['x']}
    grad_fn = _jax.value_and_grad(_loss, argnums=(0, 1))

    def one_microbatch(ex, loss_target):
        ex = dict(ex)
        diff = ex.pop(TWIN_DIFF_INPUT)
        return grad_fn(weights, diff, {**shared, **ex}, loss_target)

    if N_MICROBATCH == 1:
        loss, (grad_w, grad_x) = one_microbatch(per_example, given["loss_target"])
    else:
        def body(carry, xs):
            loss_sum, grad_sum = carry
            l_k, (gw_k, gx_k) = one_microbatch(xs[0], xs[1])
            with _jax.named_scope("update"):
                return (loss_sum + l_k, _jax.tree.map(_jnp.add, grad_sum, gw_k)), gx_k

        init = (_jnp.zeros((), _jnp.float32), _jax.tree.map(_jnp.zeros_like, weights))
        (loss, grad_w), grad_x = _jax.lax.scan(body, init, (per_example, given["loss_target"]))
    with _jax.named_scope("update"):
        delta_w, new_m, new_v = {}, {}, {}
        for n in TWIN_WEIGHTS:
            delta_w[n], new_m[n], new_v[n] = _adamw(weights[n], grad_w[n], given["m_" + n], given["v_" + n])
    return (loss, grad_x, *[grad_w[n] for n in TWIN_WEIGHTS], *[delta_w[n] for n in TWIN_WEIGHTS],
            *[new_m[n] for n in TWIN_WEIGHTS], *[new_v[n] for n in TWIN_WEIGHTS])
```

```python
import jax
import jax.numpy as jnp
from jax import lax
from jax.experimental import pallas as pl
from jax.experimental.pallas import tpu as pltpu

F32 = jnp.float32
BF = jnp.bfloat16
MESH = pl.DeviceIdType.MESH

DEPTH = 2
D = 1024
DC = 256
DA = 512
NH = 8
HD = 64
CFW = 31
SCW = 3
N_IN = 3848
NP = 3968
NSH = N_IN // 4
HALO = 32
EPS = 1e-6
LANES = 128
VMEM_LIMIT = 56 * 1024 * 1024

C_CF, C_SC, C_Q, C_K, C_V, C_Z, C_F = 0, 768, 1792, 2304, 2816, 3328, 3840

ADAM_LR = 0.001
ADAM_B1 = 0.9
ADAM_B2 = 0.999
ADAM_EPS = 1e-08
ADAM_WD = 0.01
ADAM_STEP = 10


def _cparams(sem=None):
    return pltpu.CompilerParams(dimension_semantics=sem, vmem_limit_bytes=VMEM_LIMIT)


def _dot(a, b):
    return jnp.dot(a, b, preferred_element_type=F32)


def _dot_nt(a, b):
    return lax.dot_general(a, b, (((1,), (1,)), ((), ())), preferred_element_type=F32)


def _dot_tn(a, b):
    return lax.dot_general(a, b, (((0,), (0,)), ((), ())), preferred_element_type=F32)


def _split3(x):
    hi = x.astype(BF)
    r1 = x - hi.astype(F32)
    mid = r1.astype(BF)
    lo = (r1 - mid.astype(F32)).astype(BF)
    return hi, mid, lo


def _dot_exact(a_bf, x):
    hi, mid, lo = _split3(x)
    return _dot(a_bf, hi) + _dot(a_bf, mid) + _dot(a_bf, lo)


def _sigmoid(x):
    return 1.0 / (1.0 + jnp.exp(-x))


def _seg_sum64(x):
    i = lax.broadcasted_iota(jnp.int32, (LANES, LANES), 0)
    j = lax.broadcasted_iota(jnp.int32, (LANES, LANES), 1)
    g = ((i >= HD) == (j >= HD)).astype(BF)
    hi, mid, lo = _split3(x)
    return _dot(hi, g) + _dot(mid, g) + _dot(lo, g)


def _fwd_inproj(x, g, w, name):
    t = x.shape[0]
    tm = 256

    def body(x_ref, g_ref, w_ref, cf_ref, sc_ref, q_ref, k_ref, v_ref, z_ref, f_ref):
        xt = x_ref[...]
        r = lax.rsqrt(jnp.mean(xt * xt, axis=-1, keepdims=True) + EPS)
        h = ((xt * r) * g_ref[...]).astype(BF)
        cf_ref[:, 0:512] = _dot(h, w_ref[:, 0:512])
        cf_ref[:, 512:768] = _dot(h, w_ref[:, 512:768])
        sc_ref[:, 0:512] = _dot(h, w_ref[:, C_SC:C_SC + 512])
        sc_ref[:, 512:1024] = _dot(h, w_ref[:, C_SC + 512:C_Q])
        q_ref[...] = _dot(h, w_ref[:, C_Q:C_K])
        k_ref[...] = _dot(h, w_ref[:, C_K:C_V])
        v_ref[...] = _dot(h, w_ref[:, C_V:C_Z]).astype(BF)
        z_ref[...] = _dot(h, w_ref[:, C_Z:C_F])
        f_ref[...] = _dot(h, w_ref[:, C_F:NP])

    row = lambda n: pl.BlockSpec((tm, n), lambda i: (i, 0))
    return pl.pallas_call(
        body, grid=(t // tm,),
        in_specs=[row(D), pl.BlockSpec((1, D), lambda i: (0, 0)), pl.BlockSpec((D, NP), lambda i: (0, 0))],
        out_specs=[row(768), row(1024), row(DA), row(DA), row(DA), row(DA), row(LANES)],
        out_shape=[jax.ShapeDtypeStruct((t, 768), F32), jax.ShapeDtypeStruct((t, 1024), F32),
                   jax.ShapeDtypeStruct((t, DA), F32), jax.ShapeDtypeStruct((t, DA), F32),
                   jax.ShapeDtypeStruct((t, DA), BF), jax.ShapeDtypeStruct((t, DA), F32),
                   jax.ShapeDtypeStruct((t, LANES), F32)],
        compiler_params=_cparams(("arbitrary",)), name=name)(x, g, w)


def _conformer_fwd(a, g, ha, hg, first, ubuf, dw_ref, bias, lng, lnb):
    ts = a.shape[0]
    u0 = a * _sigmoid(g)
    ubuf[0:HALO, :] = jnp.where(first, 0.0, ha * _sigmoid(hg))
    ubuf[HALO:HALO + ts, :] = u0
    u1 = jnp.zeros((ts, DC), F32) + bias
    for k in range(CFW):
        u1 = u1 + dw_ref[k:k + 1, :] * ubuf[pl.ds(HALO - (CFW - 1) + k, ts), :]
    mu = jnp.mean(u1, axis=-1, keepdims=True)
    xc = u1 - mu
    rstd = lax.rsqrt(jnp.mean(xc * xc, axis=-1, keepdims=True) + EPS)
    n = xc * rstd
    u2 = n * lng + lnb
    s2 = _sigmoid(u2)
    u3 = u2 * s2
    return u0, n, rstd, u2, s2, u3


def _shortconv_fwd(c, xs, hc, hx, first, mbuf, dw_ref):
    ts = c.shape[0]
    mbuf[0:HALO, :] = jnp.where(first, 0.0, hc * hx)
    mbuf[HALO:HALO + ts, :] = c * xs
    cv = jnp.zeros((ts, DC), F32)
    for k in range(SCW):
        cv = cv + dw_ref[k:k + 1, :] * mbuf[pl.ds(HALO - (SCW - 1) + k, ts), :]
    return cv


def _conv_specs(bsz, seq, ts, order):
    nt = seq // ts
    tile = (lambda i: i) if order > 0 else (lambda i: nt - 1 - i)
    hrow = lambda i: jnp.maximum(tile(i) * (ts // HALO) - 1, 0)
    cur = lambda n: pl.BlockSpec((None, ts, n), lambda b, i: (b, tile(i), 0))
    halo = lambda j: pl.BlockSpec((None, HALO, DC), lambda b, i: (b, hrow(i), j))
    full = lambda r, c: pl.BlockSpec((r, c), lambda b, i: (0, 0))
    return nt, tile, cur, halo, full


def _fwd_conv(pcf, psc, cf_dw, cf_dw_b, ln_g, ln_b, cf_pw, sc_dw, name):
    bsz, seq, _ = pcf.shape
    ts = 512
    nt, tile, cur, halo, full = _conv_specs(bsz, seq, ts, +1)

    def body(cf_ref, ha_ref, hg_ref, sc_ref, hc_ref, hx_ref, dw_ref, b_ref, lg_ref, lb_ref, pw_ref, sdw_ref,
             y_ref, ubuf, mbuf):
        first = pl.program_id(1) == 0
        _, _, _, _, _, u3 = _conformer_fwd(cf_ref[:, 0:256], cf_ref[:, 256:512], ha_ref[...], hg_ref[...], first,
                                           ubuf, dw_ref, b_ref[...], lg_ref[...], lb_ref[...])
        z = cf_ref[:, 512:768]
        y_ref[:, 0:256] = _dot(u3.astype(BF), pw_ref[...]) * (z * _sigmoid(z))
        cv = _shortconv_fwd(sc_ref[:, 256:512], sc_ref[:, 512:768], hc_ref[...], hx_ref[...], first, mbuf, sdw_ref)
        zs = sc_ref[:, 768:1024]
        y_ref[:, 256:512] = sc_ref[:, 0:256] * cv * (zs * _sigmoid(zs))

    return pl.pallas_call(
        body, grid=(bsz, nt),
        in_specs=[cur(768), halo(0), halo(1), cur(1024), halo(1), halo(2),
                  full(CFW, DC), full(1, DC), full(1, DC), full(1, DC), full(DC, DC), full(SCW, DC)],
        out_specs=pl.BlockSpec((None, ts, 512), lambda b, i: (b, i, 0)),
        out_shape=jax.ShapeDtypeStruct((bsz, seq, 512), F32),
        scratch_shapes=[pltpu.VMEM((HALO + ts, DC), F32), pltpu.VMEM((HALO + ts, DC), F32)],
        compiler_params=_cparams(("arbitrary", "arbitrary")), name=name,
    )(pcf, pcf, pcf, psc, psc, psc, cf_dw, cf_dw_b, ln_g, ln_b, cf_pw, sc_dw)


def _head_rms(xb, gb):
    ms = _seg_sum64(xb * xb) * (1.0 / HD)
    r = lax.rsqrt(ms + EPS)
    xhat = xb * r
    return xhat, r, xhat * gb


def _fwd_attn_prep(pq, pk, pf, gq, gk, bf, name):
    bsz, seq, _ = pq.shape
    ts = 512
    nt = seq // ts

    def body(q_ref, k_ref, f_ref, gq_ref, gk_ref, bf_ref, qs_ref, kn_ref, cq_ref, crow_ref, carry):
        @pl.when(pl.program_id(1) == 0)
        def _():
            carry[...] = jnp.zeros_like(carry)

        for jb in range(DA // LANES):
            sl = slice(jb * LANES, (jb + 1) * LANES)
            _, _, qn = _head_rms(q_ref[:, sl], gq_ref[:, sl])
            qs_ref[:, sl] = (qn * (1.0 / 8.0)).astype(BF)
            _, _, kn = _head_rms(k_ref[:, sl], gk_ref[:, sl])
            kn_ref[:, sl] = kn.astype(BF)

        xf = f_ref[...] + bf_ref[...]
        lf = jnp.minimum(xf, 0.0) - jnp.log(1.0 + jnp.exp(-jnp.abs(xf)))
        ti = lax.broadcasted_iota(jnp.int32, (ts, ts), 0)
        si = lax.broadcasted_iota(jnp.int32, (ts, ts), 1)
        c = _dot_exact((si <= ti).astype(BF), lf) + carry[...]
        carry[...] = c[ts - 1:ts, :]
        hj = lax.broadcasted_iota(jnp.int32, (LANES, DA), 0)
        ll = lax.broadcasted_iota(jnp.int32, (LANES, DA), 1)
        dd = ll - hj * HD
        chi, cmid, clo = _split3(c)
        e = ((dd >= 0) & (dd < HD)).astype(BF)
        cq_ref[...] = _dot(chi, e) + _dot(cmid, e) + _dot(clo, e)
        hh = lax.broadcasted_iota(jnp.int32, (16, LANES), 0)
        jj = lax.broadcasted_iota(jnp.int32, (16, LANES), 1)
        sel = (hh == jj).astype(BF)
        cr = _dot_nt(sel, chi) + _dot_nt(sel, cmid) + _dot_nt(sel, clo)
        crow_ref[...] = cr[0:NH, :]

    tile = lambda n: pl.BlockSpec((None, ts, n), lambda b, i: (b, i, 0))
    vec = lambda n: pl.BlockSpec((1, n), lambda b, i: (0, 0))
    return pl.pallas_call(
        body, grid=(bsz, nt),
        in_specs=[tile(DA), tile(DA), tile(LANES), vec(DA), vec(DA), vec(LANES)],
        out_specs=[tile(DA), tile(DA), tile(DA), pl.BlockSpec((None, NH, ts), lambda b, i: (b, 0, i))],
        out_shape=[jax.ShapeDtypeStruct((bsz, seq, DA), BF), jax.ShapeDtypeStruct((bsz, seq, DA), BF),
                   jax.ShapeDtypeStruct((bsz, seq, DA), F32), jax.ShapeDtypeStruct((bsz, NH, seq), F32)],
        scratch_shapes=[pltpu.VMEM((1, LANES), F32)],
        compiler_params=_cparams(("arbitrary", "arbitrary")), name=name)(pq, pk, pf, gq, gk, bf)


def _causal_mask(tq, tk):
    r = lax.broadcasted_iota(jnp.int32, (tq, tk), 0)
    c = lax.broadcasted_iota(jnp.int32, (tq, tk), 1)
    return r - c


def _fwd_attn(qs, kn, vb, cq, crow, pz, name):
    bsz, seq, _ = qs.shape
    tq = 256
    nq = seq // tq
    npair = NH // 2

    def body(q_ref, k_ref, v_ref, cq_ref, cr_ref, z_ref, o_ref, ohp_ref, y_ref, lse_ref):
        qi = pl.program_id(2)
        rc = _causal_mask(tq, tq)
        outs, outs_hp, lses = [], [], []
        for hh in range(2):
            ls = slice(hh * HD, (hh + 1) * HD)
            qh = q_ref[:, ls]
            ct = cq_ref[:, hh * HD:hh * HD + 1]

            def step(j, carry, masked):
                m, l, acc, accl = carry
                rows = pl.ds(pl.multiple_of(j * tq, tq), tq)
                kh = k_ref[rows, :][:, ls]
                vh = v_ref[rows, :][:, ls]
                s = _dot_nt(qh, kh) + (ct - cr_ref[hh:hh + 1, rows])
                if masked:
                    s = jnp.where(rc >= 0, s, -1e30)
                m2 = jnp.maximum(m, jnp.max(s, axis=-1, keepdims=True))
                p = jnp.exp(s - m2)
                alpha = jnp.exp(m - m2)
                l2 = alpha * l + jnp.sum(p, axis=-1, keepdims=True)
                pb = p.astype(BF)
                plo = (p - pb.astype(F32)).astype(BF)
                acc2 = alpha * acc + _dot(pb, vh)
                accl2 = alpha * accl + _dot(plo, vh)
                return m2, l2, acc2, accl2

            zero = jnp.zeros((tq, HD), F32)
            init = (jnp.full((tq, 1), -1e30, F32), jnp.zeros((tq, 1), F32), zero, zero)
            carry = lax.fori_loop(0, qi, lambda j, cy: step(j, cy, False), init)
            m, l, acc, accl = step(qi, carry, True)
            outs.append(acc / l)
            outs_hp.append((acc + accl) / l)
            lses.append(jnp.broadcast_to(m + jnp.log(l), (tq, HD)))
        o = jnp.concatenate(outs, axis=-1)
        z = z_ref[...]
        o_ref[...] = o
        ohp_ref[...] = jnp.concatenate(outs_hp, axis=-1)
        y_ref[...] = o * (z * _sigmoid(z))
        lse_ref[...] = jnp.concatenate(lses, axis=-1)

    qblk = pl.BlockSpec((None, tq, LANES), lambda b, h, i: (b, i, h))
    kvblk = pl.BlockSpec((None, seq, LANES), lambda b, h, i: (b, 0, h))
    return pl.pallas_call(
        body, grid=(bsz, npair, nq),
        in_specs=[qblk, kvblk, kvblk, qblk, pl.BlockSpec((None, None, 2, seq), lambda b, h, i: (b, h, 0, 0)), qblk],
        out_specs=[qblk, qblk, qblk, qblk],
        out_shape=[jax.ShapeDtypeStruct((bsz, seq, DA), F32)] * 4,
        compiler_params=_cparams(("arbitrary", "arbitrary", "arbitrary")), name=name)(qs, kn, vb, cq, crow, pz)


def _fwd_outproj(x, ycs, yatt, wo, name):
    t = x.shape[0]
    tm = 512

    def body(x_ref, a_ref, b_ref, w_ref, o_ref):
        o_ref[...] = (x_ref[...] + _dot(a_ref[...].astype(BF), w_ref[0:512, :])
                      + _dot(b_ref[...].astype(BF), w_ref[512:1024, :]))

    row = lambda n: pl.BlockSpec((tm, n), lambda i: (i, 0))
    return pl.pallas_call(
        body, grid=(t // tm,),
        in_specs=[row(D), row(512), row(512), pl.BlockSpec((D, D), lambda i: (0, 0))],
        out_specs=row(D), out_shape=jax.ShapeDtypeStruct((t, D), F32),
        compiler_params=_cparams(("arbitrary",)), name=name)(x, ycs, yatt, wo)


def _fwd_outproj_loss(x, ycs, yatt, wo, target, name):
    t = x.shape[0]
    tm = 512

    def body(x_ref, a_ref, b_ref, w_ref, t_ref, dy_ref, loss_ref):
        @pl.when(pl.program_id(0) == 0)
        def _():
            loss_ref[...] = jnp.zeros_like(loss_ref)

        y = (x_ref[...] + _dot(a_ref[...].astype(BF), w_ref[0:512, :])
             + _dot(b_ref[...].astype(BF), w_ref[512:1024, :]))
        err = y - t_ref[...]
        dy_ref[...] = err * (1.0 / D)
        per_tok = jnp.mean(err * err, axis=-1, keepdims=True)
        loss_ref[...] += 0.5 * jnp.sum(per_tok, axis=0, keepdims=True)

    row = lambda n: pl.BlockSpec((tm, n), lambda i: (i, 0))
    return pl.pallas_call(
        body, grid=(t // tm,),
        in_specs=[row(D), row(512), row(512), pl.BlockSpec((D, D), lambda i: (0, 0)), row(D)],
        out_specs=[row(D), pl.BlockSpec((1, 1), lambda i: (0, 0))],
        out_shape=[jax.ShapeDtypeStruct((t, D), F32), jax.ShapeDtypeStruct((1, 1), F32)],
        compiler_params=_cparams(("arbitrary",)), name=name)(x, ycs, yatt, wo, target)


def _dsilu(x, s):
    return s * (1.0 + x * (1.0 - s))


def _bwd_outproj(dy, ycs, yatt, o, ohp, pz, wo, name):
    t = dy.shape[0]
    tm = 256

    def body(dy_ref, a_ref, b_ref, o_ref, ohp_ref, z_ref, w_ref, dcs_ref, do_ref, dz_ref, dl_ref, dw_ref):
        @pl.when(pl.program_id(0) == 0)
        def _():
            dw_ref[...] = jnp.zeros_like(dw_ref)

        dyb = dy_ref[...].astype(BF)
        dw_ref[0:512, :] += _dot_tn(a_ref[...].astype(BF), dyb)
        dw_ref[512:1024, :] += _dot_tn(b_ref[...].astype(BF), dyb)
        dcs_ref[...] = _dot_nt(dyb, w_ref[0:512, :])
        dyatt = _dot_nt(dyb, w_ref[512:1024, :])
        z = z_ref[...]
        sz = _sigmoid(z)
        o_t = o_ref[...]
        dob = (dyatt * (z * sz)).astype(BF)
        do_ref[...] = dob
        dz_ref[...] = dyatt * o_t * _dsilu(z, sz)
        prod = dob.astype(F32) * ohp_ref[...]
        for jb in range(DA // LANES):
            sl = slice(jb * LANES, (jb + 1) * LANES)
            dl_ref[:, sl] = _seg_sum64(prod[:, sl])

    row = lambda n: pl.BlockSpec((tm, n), lambda i: (i, 0))
    return pl.pallas_call(
        body, grid=(t // tm,),
        in_specs=[row(D), row(512), row(512), row(DA), row(DA), row(DA), pl.BlockSpec((D, D), lambda i: (0, 0))],
        out_specs=[row(512), row(DA), row(DA), row(DA), pl.BlockSpec((D, D), lambda i: (0, 0))],
        out_shape=[jax.ShapeDtypeStruct((t, 512), F32), jax.ShapeDtypeStruct((t, DA), BF),
                   jax.ShapeDtypeStruct((t, DA), F32), jax.ShapeDtypeStruct((t, DA), F32),
                   jax.ShapeDtypeStruct((D, D), F32)],
        compiler_params=_cparams(("arbitrary",)), name=name)(dy, ycs, yatt, o, ohp, pz, wo)


def _bwd_attn(qs, kn, vb, do, cq, lse, dl, crow, name):
    bsz, seq, _ = qs.shape
    tq = 256
    nq = seq // tq
    npair = NH // 2

    def body(q_ref, k_ref, v_ref, do_ref, cq_ref, lse_ref, dl_ref, cr_ref, dq_ref, dk_ref, dv_ref, dc_ref, dq_acc):
        dq_acc[...] = jnp.zeros_like(dq_acc)
        rc = _causal_mask(tq, tq)
        for hh in range(2):
            ls = slice(hh * HD, (hh + 1) * HD)
            c0 = hh * HD

            def kloop(kj, _):
                krows = pl.ds(pl.multiple_of(kj * tq, tq), tq)
                kh = k_ref[krows, :][:, ls]
                vh = v_ref[krows, :][:, ls]
                crj = cr_ref[hh:hh + 1, krows]

                def qloop(qi, carry):
                    dk, dv, cs = carry
                    qrows = pl.ds(pl.multiple_of(qi * tq, tq), tq)
                    qh = q_ref[qrows, :][:, ls]
                    doh = do_ref[qrows, :][:, ls]
                    ct = cq_ref[qrows, c0:c0 + 1]
                    s = _dot_nt(qh, kh) + (ct - crj)
                    keep = rc >= (kj - qi) * tq
                    p = jnp.where(keep, jnp.exp(s - lse_ref[qrows, c0:c0 + 1]), 0.0)
                    dp = _dot_nt(doh, vh)
                    ds = p * (dp - dl_ref[qrows, c0:c0 + 1])
                    dsb = ds.astype(BF)
                    dv = dv + _dot_tn(p.astype(BF), doh)
                    dk = dk + _dot_tn(dsb, qh)
                    cs = cs + jnp.sum(ds, axis=0, keepdims=True)
                    dq_acc[hh, qrows, :] += _dot(dsb, kh)
                    return dk, dv, cs

                zero = jnp.zeros((tq, HD), F32)
                dk, dv, cs = lax.fori_loop(kj, nq, qloop, (zero, zero, jnp.zeros((1, tq), F32)))
                dk_ref[krows, ls] = dk
                dv_ref[krows, ls] = dv
                dc_ref[hh:hh + 1, krows] = -cs
                return 0

            lax.fori_loop(0, nq, kloop, 0)
        dq_ref[...] = jnp.concatenate([dq_acc[0], dq_acc[1]], axis=-1)

    blk = pl.BlockSpec((None, seq, LANES), lambda b, h: (b, 0, h))
    rowblk = pl.BlockSpec((None, None, 2, seq), lambda b, h: (b, h, 0, 0))
    return pl.pallas_call(
        body, grid=(bsz, npair),
        in_specs=[blk, blk, blk, blk, blk, blk, blk, rowblk],
        out_specs=[blk, blk, blk, rowblk],
        out_shape=[jax.ShapeDtypeStruct((bsz, seq, DA), F32)] * 3 + [jax.ShapeDtypeStruct((bsz, npair, 2, seq), F32)],
        scratch_shapes=[pltpu.VMEM((2, seq, HD), F32)],
        compiler_params=_cparams(("arbitrary", "arbitrary")), name=name)(qs, kn, vb, do, cq, lse, dl, crow)


def _bwd_attn_post(pq, pk, dqs, dkn, pf, dcrow, gq, gk, bf, name):
    bsz, seq, _ = pq.shape
    ts = 512
    nt = seq // ts

    def body(q_ref, k_ref, dq_ref, dk_ref, f_ref, dc_ref, gq_ref, gk_ref, bf_ref,
             dpq_ref, dpk_ref, dpf_ref, dgq_ref, dgk_ref, dbf_ref, carry):
        @pl.when((pl.program_id(0) == 0) & (pl.program_id(1) == 0))
        def _():
            dgq_ref[...] = jnp.zeros_like(dgq_ref)
            dgk_ref[...] = jnp.zeros_like(dgk_ref)
            dbf_ref[...] = jnp.zeros_like(dbf_ref)

        @pl.when(pl.program_id(1) == 0)
        def _():
            carry[...] = jnp.zeros_like(carry)

        for x_ref, dx_ref, g_ref, dp_ref, dg_ref, scale in ((q_ref, dq_ref, gq_ref, dpq_ref, dgq_ref, 1.0 / 8.0),
                                                            (k_ref, dk_ref, gk_ref, dpk_ref, dgk_ref, 1.0)):
            for jb in range(DA // LANES):
                sl = slice(jb * LANES, (jb + 1) * LANES)
                gb = g_ref[:, sl]
                xhat, r, _ = _head_rms(x_ref[:, sl], gb)
                dn = dx_ref[:, sl] * scale
                dg_ref[:, sl] += jnp.sum(dn * xhat, axis=0, keepdims=True)
                dxh = dn * gb
                dp_ref[:, sl] = r * (dxh - xhat * (_seg_sum64(dxh * xhat) * (1.0 / HD)))

        ui = lax.broadcasted_iota(jnp.int32, (ts, ts), 0)
        tj = lax.broadcasted_iota(jnp.int32, (ts, ts), 1)
        tri = (tj >= ui).astype(BF)
        dc = jnp.concatenate([dc_ref[...], jnp.zeros((LANES - NH, ts), F32)], axis=0)
        hi, mid, lo = _split3(dc)
        dlf = _dot_nt(tri, hi) + _dot_nt(tri, mid) + _dot_nt(tri, lo) + carry[...]
        carry[...] = dlf[0:1, :]
        xf = f_ref[...] + bf_ref[...]
        lane = lax.broadcasted_iota(jnp.int32, (ts, LANES), 1)
        dfl = jnp.where(lane < NH, dlf * _sigmoid(-xf), 0.0)
        dpf_ref[...] = dfl
        dbf_ref[...] += jnp.sum(dfl, axis=0, keepdims=True)

    rev = lambda i: nt - 1 - i
    tile = lambda n: pl.BlockSpec((None, ts, n), lambda b, i: (b, rev(i), 0))
    vec = lambda n: pl.BlockSpec((1, n), lambda b, i: (0, 0))
    return pl.pallas_call(
        body, grid=(bsz, nt),
        in_specs=[tile(DA), tile(DA), tile(DA), tile(DA), tile(LANES),
                  pl.BlockSpec((None, NH, ts), lambda b, i: (b, 0, rev(i))), vec(DA), vec(DA), vec(LANES)],
        out_specs=[tile(DA), tile(DA), tile(LANES), vec(DA), vec(DA), vec(LANES)],
        out_shape=[jax.ShapeDtypeStruct((bsz, seq, DA), F32), jax.ShapeDtypeStruct((bsz, seq, DA), F32),
                   jax.ShapeDtypeStruct((bsz, seq, LANES), F32), jax.ShapeDtypeStruct((1, DA), F32),
                   jax.ShapeDtypeStruct((1, DA), F32), jax.ShapeDtypeStruct((1, LANES), F32)],
        scratch_shapes=[pltpu.VMEM((1, LANES), F32)],
        compiler_params=_cparams(("arbitrary", "arbitrary")), name=name)(pq, pk, dqs, dkn, pf, dcrow, gq, gk, bf)


def _bwd_conv(pcf, psc, dycs, cf_dw, cf_dw_b, ln_g, ln_b, cf_pw, sc_dw, name):
    bsz, seq, _ = pcf.shape
    ts = 512
    nt, tile, cur, halo, full = _conv_specs(bsz, seq, ts, -1)

    def body(cf_ref, ha_ref, hg_ref, sc_ref, hc_ref, hx_ref, dy_ref, dw_ref, b_ref, lg_ref, lb_ref, pw_ref, sdw_ref,
             dcf_ref, dsc_ref, ddw_ref, db_ref, dlg_ref, dlb_ref, dpw_ref, dsdw_ref,
             ubuf, mbuf, dubuf, dcbuf):
        step = pl.program_id(1)
        first = tile(step) == 0

        @pl.when((pl.program_id(0) == 0) & (step == 0))
        def _():
            for r in (ddw_ref, db_ref, dlg_ref, dlb_ref, dpw_ref, dsdw_ref):
                r[...] = jnp.zeros_like(r)

        @pl.when(step == 0)
        def _():
            dubuf[ts:ts + HALO, :] = jnp.zeros((HALO, DC), F32)
            dcbuf[ts:ts + HALO, :] = jnp.zeros((HALO, DC), F32)

        a = cf_ref[:, 0:256]
        g = cf_ref[:, 256:512]
        z = cf_ref[:, 512:768]
        lng = lg_ref[...]
        u0, n, rstd, u2, s2, u3 = _conformer_fwd(a, g, ha_ref[...], hg_ref[...], first, ubuf, dw_ref,
                                                 b_ref[...], lng, lb_ref[...])
        u3b = u3.astype(BF)
        p = _dot(u3b, pw_ref[...])
        sz = _sigmoid(z)
        dy = dy_ref[:, 0:256]
        dcf_ref[:, 512:768] = dy * p * _dsilu(z, sz)
        dpb = (dy * (z * sz)).astype(BF)
        dpw_ref[...] += _dot_tn(u3b, dpb)
        du2 = _dot_nt(dpb, pw_ref[...]) * _dsilu(u2, s2)
        dlg_ref[...] += jnp.sum(du2 * n, axis=0, keepdims=True)
        dlb_ref[...] += jnp.sum(du2, axis=0, keepdims=True)
        dn = du2 * lng
        du1 = rstd * (dn - jnp.mean(dn, axis=-1, keepdims=True) - n * jnp.mean(dn * n, axis=-1, keepdims=True))
        db_ref[...] += jnp.sum(du1, axis=0, keepdims=True)
        dubuf[0:ts, :] = du1
        du0 = jnp.zeros((ts, DC), F32)
        for k in range(CFW):
            du0 = du0 + dw_ref[k:k + 1, :] * dubuf[pl.ds(CFW - 1 - k, ts), :]
            ddw_ref[k:k + 1, :] += jnp.sum(du1 * ubuf[pl.ds(HALO - (CFW - 1) + k, ts), :], axis=0, keepdims=True)
        dubuf[ts:ts + HALO, :] = du1[0:HALO, :]
        sg = _sigmoid(g)
        dcf_ref[:, 0:256] = du0 * sg
        dcf_ref[:, 256:512] = du0 * a * sg * (1.0 - sg)

        bb = sc_ref[:, 0:256]
        c = sc_ref[:, 256:512]
        xs = sc_ref[:, 512:768]
        zs = sc_ref[:, 768:1024]
        cv = _shortconv_fwd(c, xs, hc_ref[...], hx_ref[...], first, mbuf, sdw_ref)
        szs = _sigmoid(zs)
        dys = dy_ref[:, 256:512]
        gate = zs * szs
        dsc_ref[:, 0:256] = dys * cv * gate
        dsc_ref[:, 768:1024] = dys * bb * cv * _dsilu(zs, szs)
        dcv = dys * bb * gate
        dcbuf[0:ts, :] = dcv
        dm = jnp.zeros((ts, DC), F32)
        for k in range(SCW):
            dm = dm + sdw_ref[k:k + 1, :] * dcbuf[pl.ds(SCW - 1 - k, ts), :]
            dsdw_ref[k:k + 1, :] += jnp.sum(dcv * mbuf[pl.ds(HALO - (SCW - 1) + k, ts), :], axis=0, keepdims=True)
        dcbuf[ts:ts + HALO, :] = dcv[0:HALO, :]
        dsc_ref[:, 256:512] = dm * xs
        dsc_ref[:, 512:768] = dm * c

    outt = lambda n: pl.BlockSpec((None, ts, n), lambda b, i: (b, tile(i), 0))
    return pl.pallas_call(
        body, grid=(bsz, nt),
        in_specs=[cur(768), halo(0), halo(1), cur(1024), halo(1), halo(2), cur(512),
                  full(CFW, DC), full(1, DC), full(1, DC), full(1, DC), full(DC, DC), full(SCW, DC)],
        out_specs=[outt(768), outt(1024), full(CFW, DC), full(1, DC), full(1, DC), full(1, DC), full(DC, DC),
                   full(SCW, DC)],
        out_shape=[jax.ShapeDtypeStruct((bsz, seq, 768), F32), jax.ShapeDtypeStruct((bsz, seq, 1024), F32),
                   jax.ShapeDtypeStruct((CFW, DC), F32), jax.ShapeDtypeStruct((1, DC), F32),
                   jax.ShapeDtypeStruct((1, DC), F32), jax.ShapeDtypeStruct((1, DC), F32),
                   jax.ShapeDtypeStruct((DC, DC), F32), jax.ShapeDtypeStruct((SCW, DC), F32)],
        scratch_shapes=[pltpu.VMEM((HALO + ts, DC), F32), pltpu.VMEM((HALO + ts, DC), F32),
                        pltpu.VMEM((ts + HALO, DC), F32), pltpu.VMEM((ts + HALO, DC), F32)],
        compiler_params=_cparams(("arbitrary", "arbitrary")), name=name,
    )(pcf, pcf, pcf, psc, psc, psc, dycs, cf_dw, cf_dw_b, ln_g, ln_b, cf_pw, sc_dw)


def _bwd_inproj(x, g, dyres, w, dcf, dsc, dq, dk, dv, dz, df, name):
    t = x.shape[0]
    tm = 256
    pieces = ((C_CF, 768), (C_SC, 1024), (C_Q, DA), (C_K, DA), (C_V, DA), (C_Z, DA), (C_F, LANES))

    def body(x_ref, g_ref, dy_ref, w_hbm, dcf_ref, dsc_ref, dq_ref, dk_ref, dv_ref, dz_ref, df_ref,
             dx_ref, dg_ref, dw_hbm, w_vmem, dw_acc, sem):
        i = pl.program_id(0)

        @pl.when(i == 0)
        def _():
            cp = pltpu.make_async_copy(w_hbm, w_vmem, sem)
            cp.start()
            dw_acc[...] = jnp.zeros_like(dw_acc)
            dg_ref[...] = jnp.zeros_like(dg_ref)
            cp.wait()

        xt = x_ref[...]
        gg = g_ref[...]
        r = lax.rsqrt(jnp.mean(xt * xt, axis=-1, keepdims=True) + EPS)
        xhat = xt * r
        ht = (xhat * gg).astype(BF).T
        dh = jnp.zeros((tm, D), F32)
        for (c0, n), ref in zip(pieces, (dcf_ref, dsc_ref, dq_ref, dk_ref, dv_ref, dz_ref, df_ref)):
            for s0 in range(0, n, 512):
                s1 = min(s0 + 512, n)
                d = ref[:, s0:s1].astype(BF)
                dh = dh + _dot_nt(d, w_vmem[:, c0 + s0:c0 + s1])
                dw_acc[:, c0 + s0:c0 + s1] += _dot(ht, d)
        dg_ref[...] += jnp.sum(dh * xhat, axis=0, keepdims=True)
        dhg = dh * gg
        dx_ref[...] = dy_ref[...] + r * (dhg - xhat * jnp.mean(dhg * xhat, axis=-1, keepdims=True))

        @pl.when(i == pl.num_programs(0) - 1)
        def _():
            out = pltpu.make_async_copy(dw_acc, dw_hbm, sem)
            out.start()
            out.wait()

    row = lambda n: pl.BlockSpec((tm, n), lambda i: (i, 0))
    anyspec = pl.BlockSpec(memory_space=pl.ANY)
    return pl.pallas_call(
        body, grid=(t // tm,),
        in_specs=[row(D), pl.BlockSpec((1, D), lambda i: (0, 0)), row(D), anyspec,
                  row(768), row(1024), row(DA), row(DA), row(DA), row(DA), row(LANES)],
        out_specs=[row(D), pl.BlockSpec((1, D), lambda i: (0, 0)), anyspec],
        out_shape=[jax.ShapeDtypeStruct((t, D), F32), jax.ShapeDtypeStruct((1, D), F32),
                   jax.ShapeDtypeStruct((D, NP), F32)],
        scratch_shapes=[pltpu.VMEM((D, NP), BF), pltpu.VMEM((D, NP), F32), pltpu.SemaphoreType.DMA],
        compiler_params=_cparams(("arbitrary",)), name=name)(x, g, dyres, w, dcf, dsc, dq, dk, dv, dz, df)


def _adamw(w, g, m, v, tr, name):
    rows, cols = w.shape

    def body(w_ref, g_ref, m_ref, v_ref, d_ref, m2_ref, v2_ref):
        gt = g_ref[...]
        m2 = ADAM_B1 * m_ref[...] + (1.0 - ADAM_B1) * gt
        v2 = ADAM_B2 * v_ref[...] + (1.0 - ADAM_B2) * (gt * gt)
        m_hat = m2 / (1.0 - ADAM_B1 ** ADAM_STEP)
        v_hat = v2 / (1.0 - ADAM_B2 ** ADAM_STEP)
        d_ref[...] = -ADAM_LR * (m_hat / (jnp.sqrt(v_hat) + ADAM_EPS) + ADAM_WD * w_ref[...])
        m2_ref[...] = m2
        v2_ref[...] = v2

    blk = pl.BlockSpec((tr, cols), lambda i: (i, 0))
    return pl.pallas_call(
        body, grid=(rows // tr,), in_specs=[blk] * 4, out_specs=[blk] * 3,
        out_shape=[jax.ShapeDtypeStruct((rows, cols), F32)] * 3,
        compiler_params=_cparams(("arbitrary",)), name=name)(w, g, m, v)


def _place():
    x, y, c = lax.axis_index("x"), lax.axis_index("y"), lax.axis_index("c")
    chips = [(1 - x, y), (x, 1 - y), (1 - x, 1 - y)]
    return x, y, c, chips


def _all_gather_chips(packed):
    rows = packed.shape[0]
    rh = rows // 2

    def body(src, out, send_sems, recv_sems, local_sem):
        x, y, c, chips = _place()
        me = 2 * x + y
        sib = (x, y, 1 - c)
        mine = pl.ds(pl.multiple_of(c * rh, 16), rh)
        other = pl.ds(pl.multiple_of((1 - c) * rh, 16), rh)
        local = pltpu.make_async_copy(src, out.at[me], local_sem)
        local.start()

        def ici(j, chip, k):
            return pltpu.make_async_remote_copy(src_ref=src.at[mine], dst_ref=out.at[k, mine], send_sem=send_sems.at[j],
                                                recv_sem=recv_sems.at[j], device_id=(*chip, c), device_id_type=MESH)

        def d2d(j, k, half):
            return pltpu.make_async_remote_copy(src_ref=out.at[k, half], dst_ref=out.at[k, half],
                                                send_sem=send_sems.at[3 + j], recv_sem=recv_sems.at[3 + j],
                                                device_id=sib, device_id_type=MESH)

        sent = [ici(j, chip, me) for j, chip in enumerate(chips)]
        for cp in sent:
            cp.start()
        passed = []
        for j, (px, py) in enumerate(chips):
            k = 2 * px + py
            ici(j, (px, py), k).wait_recv()
            fw = d2d(j, k, mine)
            fw.start()
            passed.append(fw)
        for j, (px, py) in enumerate(chips):
            d2d(j, 2 * px + py, other).wait_recv()
        for cp in sent + passed:
            cp.wait_send()
        local.wait()

    return pl.pallas_call(
        body, in_specs=[pl.BlockSpec(memory_space=pl.ANY)], out_specs=pl.BlockSpec(memory_space=pl.ANY),
        out_shape=jax.ShapeDtypeStruct((4, rows, LANES), packed.dtype),
        scratch_shapes=[pltpu.SemaphoreType.DMA((6,)), pltpu.SemaphoreType.DMA((6,)), pltpu.SemaphoreType.DMA],
        name="all_gather_chips")(packed)


def _rs_swap_halves(g):
    rows = g.shape[1]
    rh = rows // 2

    def body(g_ref, got, send_sem, recv_sem):
        x, y, c, _ = _place()
        other = pl.ds(pl.multiple_of((1 - c) * rh, 8), rh)
        cp = pltpu.make_async_remote_copy(src_ref=g_ref.at[:, other, :], dst_ref=got, send_sem=send_sem,
                                          recv_sem=recv_sem, device_id=(x, y, 1 - c), device_id_type=MESH)
        cp.start()
        cp.wait()

    return pl.pallas_call(
        body, in_specs=[pl.BlockSpec(memory_space=pl.ANY)], out_specs=pl.BlockSpec(memory_space=pl.ANY),
        out_shape=jax.ShapeDtypeStruct((4, rh, LANES), F32),
        scratch_shapes=[pltpu.SemaphoreType.DMA, pltpu.SemaphoreType.DMA], name="rs_swap_halves")(g)


def _rs_add_pair(g, got, cidx):
    rows = g.shape[1]
    rh = rows // 2
    tb = rh // 7
    nb = rh // tb

    def body(c_ref, g_ref, o_ref, s_ref):
        s_ref[...] = g_ref[...] + o_ref[...]

    return pl.pallas_call(
        body,
        grid_spec=pltpu.PrefetchScalarGridSpec(
            num_scalar_prefetch=1, grid=(4, nb),
            in_specs=[pl.BlockSpec((None, tb, LANES), lambda k, i, c: (k, c[0] * nb + i, 0)),
                      pl.BlockSpec((None, tb, LANES), lambda k, i, c: (k, i, 0))],
            out_specs=pl.BlockSpec((None, tb, LANES), lambda k, i, c: (k, i, 0))),
        out_shape=jax.ShapeDtypeStruct((4, rh, LANES), F32),
        compiler_params=_cparams(("arbitrary", "arbitrary")), name="rs_add_pair")(cidx, g, got)


def _rs_exchange_chips(pair):
    rh = pair.shape[1]

    def body(src, got, send_sems, recv_sems, local_sem):
        x, y, c, chips = _place()
        me = 2 * x + y
        local = pltpu.make_async_copy(src.at[me], got.at[me], local_sem)
        local.start()

        def ici(j, chip, frm, to):
            return pltpu.make_async_remote_copy(src_ref=src.at[to], dst_ref=got.at[frm], send_sem=send_sems.at[j],
                                                recv_sem=recv_sems.at[j], device_id=(*chip, c), device_id_type=MESH)

        sent = [ici(j, (px, py), me, 2 * px + py) for j, (px, py) in enumerate(chips)]
        for cp in sent:
            cp.start()
        for j, (px, py) in enumerate(chips):
            ici(j, (px, py), 2 * px + py, me).wait_recv()
        for cp in sent:
            cp.wait_send()
        local.wait()

    return pl.pallas_call(
        body, in_specs=[pl.BlockSpec(memory_space=pl.ANY)], out_specs=pl.BlockSpec(memory_space=pl.ANY),
        out_shape=jax.ShapeDtypeStruct((4, rh, LANES), F32),
        scratch_shapes=[pltpu.SemaphoreType.DMA((3,)), pltpu.SemaphoreType.DMA((3,)), pltpu.SemaphoreType.DMA],
        name="rs_exchange_chips")(pair)


def _rs_add_chips(got):
    rh = got.shape[1]
    tb = rh // 7

    def body(g_ref, s_ref):
        s_ref[...] = ((g_ref[0] + g_ref[1]) + g_ref[2]) + g_ref[3]

    return pl.pallas_call(
        body, grid=(rh // tb,),
        in_specs=[pl.BlockSpec((4, tb, LANES), lambda i: (0, i, 0))],
        out_specs=pl.BlockSpec((tb, LANES), lambda i: (i, 0)),
        out_shape=jax.ShapeDtypeStruct((rh, LANES), F32),
        compiler_params=_cparams(("arbitrary",)), name="rs_add_chips")(got)


def _rs_join_halves(total):
    rh = total.shape[0]

    def body(src, out, send_sem, recv_sem, local_sem):
        x, y, c, _ = _place()
        mine = pl.ds(pl.multiple_of(c * rh, 8), rh)
        other = pl.ds(pl.multiple_of((1 - c) * rh, 8), rh)
        local = pltpu.make_async_copy(src, out.at[mine], local_sem)
        local.start()
        cp = pltpu.make_async_remote_copy(src_ref=src, dst_ref=out.at[mine], send_sem=send_sem, recv_sem=recv_sem,
                                          device_id=(x, y, 1 - c), device_id_type=MESH)
        cp.start()
        pltpu.make_async_remote_copy(src_ref=src, dst_ref=out.at[other], send_sem=send_sem, recv_sem=recv_sem,
                                     device_id=(x, y, 1 - c), device_id_type=MESH).wait_recv()
        cp.wait_send()
        local.wait()

    return pl.pallas_call(
        body, in_specs=[pl.BlockSpec(memory_space=pl.ANY)], out_specs=pl.BlockSpec(memory_space=pl.ANY),
        out_shape=jax.ShapeDtypeStruct((2 * rh, LANES), F32),
        scratch_shapes=[pltpu.SemaphoreType.DMA, pltpu.SemaphoreType.DMA, pltpu.SemaphoreType.DMA],
        name="rs_join_halves")(total)


def _rows(a):
    return a.reshape(-1, LANES)


def _pad_rows(a, mult):
    r = (-a.shape[0]) % mult
    return a if r == 0 else jnp.concatenate([a, jnp.zeros((r, a.shape[1]), a.dtype)], axis=0)


def kernel(x, norm_g, w_in, b_f, cf_dw, cf_dw_b, cf_ln_g, cf_ln_b, cf_pw, sc_dw, q_norm_g, k_norm_g, w_out, loss_target, m_norm_g, m_w_in, m_b_f, m_cf_dw, m_cf_dw_b, m_cf_ln_g, m_cf_ln_b, m_cf_pw, m_sc_dw, m_q_norm_g, m_k_norm_g, m_w_out, v_norm_g, v_w_in, v_b_f, v_cf_dw, v_cf_dw_b, v_cf_ln_g, v_cf_ln_b, v_cf_pw, v_sc_dw, v_q_norm_g, v_k_norm_g, v_w_out):
    bsz, seq, _ = x.shape
    t = bsz * seq

    taps = jnp.concatenate([cf_dw.reshape(-1), sc_dw.reshape(-1)])
    taps_bf = lax.bitcast_convert_type(taps, BF).reshape(-1, LANES)
    n_win, n_wout, n_pw, n_taps = 2 * D * NSH // LANES, 2 * 256 * D // LANES, 2 * 64 * DC // LANES, taps_bf.shape[0]
    packed = _pad_rows(jnp.concatenate([_rows(w_in.astype(BF)), _rows(w_out.astype(BF)), _rows(cf_pw.astype(BF)),
                                        taps_bf], axis=0), 32)
    gathered = _all_gather_chips(packed)
    o1, o2, o3 = n_win, n_win + n_wout, n_win + n_wout + n_pw
    w_in_full = jnp.concatenate([gathered[k, 0:o1].reshape(DEPTH, D, NSH) for k in range(4)], axis=-1)
    w_in_full = jnp.pad(w_in_full, ((0, 0), (0, 0), (0, NP - N_IN)))
    w_out_full = jnp.concatenate([gathered[k, o1:o2].reshape(DEPTH, 256, D) for k in range(4)], axis=1)
    cf_pw_full = jnp.concatenate([gathered[k, o2:o3].reshape(DEPTH, 64, DC) for k in range(4)], axis=1)
    taps_all = [lax.bitcast_convert_type(gathered[k, o3:o3 + n_taps].reshape(-1, 2), F32) for k in range(4)]
    n_cfdw = DEPTH * CFW * 64
    cf_dw_full = jnp.concatenate([tk[:n_cfdw].reshape(DEPTH, CFW, 64) for tk in taps_all], axis=-1)
    sc_dw_full = jnp.concatenate([tk[n_cfdw:].reshape(DEPTH, SCW, 64) for tk in taps_all], axis=-1)

    bf_pad = jnp.pad(b_f, ((0, 0), (0, LANES - NH)))

    xs = [x.reshape(t, D)]
    saved = []
    dy = loss_part = None
    for l in range(DEPTH):
        xl = xs[-1]
        gq, gk = q_norm_g[l].reshape(1, DA), k_norm_g[l].reshape(1, DA)
        pcf, psc, pq, pk, vb, pz, pf = _fwd_inproj(xl, norm_g[l][None], w_in_full[l], f"fwd_inproj_{l}")
        b3 = lambda a: a.reshape(bsz, seq, a.shape[-1])
        ycs = _fwd_conv(b3(pcf), b3(psc), cf_dw_full[l], cf_dw_b[l][None], cf_ln_g[l][None], cf_ln_b[l][None],
                        cf_pw_full[l], sc_dw_full[l], f"fwd_conv_{l}")
        qs, kn, cq, crow8 = _fwd_attn_prep(b3(pq), b3(pk), b3(pf), gq, gk, bf_pad[l][None], f"fwd_attn_prep_{l}")
        crow = crow8.reshape(bsz, NH // 2, 2, seq)
        o, ohp, yatt, lse = _fwd_attn(qs, kn, b3(vb), cq, crow, b3(pz), f"fwd_attn_{l}")
        ycs2, yatt2 = ycs.reshape(t, 512), yatt.reshape(t, DA)
        if l + 1 < DEPTH:
            xs.append(_fwd_outproj(xl, ycs2, yatt2, w_out_full[l], f"fwd_outproj_{l}"))
        else:
            dy, loss_part = _fwd_outproj_loss(xl, ycs2, yatt2, w_out_full[l], loss_target.reshape(t, D),
                                              f"fwd_outproj_loss_{l}")
        saved.append((pcf, psc, pq, pk, vb, pz, pf, ycs2, yatt2, o, ohp, qs, kn, cq, crow, lse))

    grads = [None] * DEPTH
    for l in reversed(range(DEPTH)):
        pcf, psc, pq, pk, vb, pz, pf, ycs2, yatt2, o, ohp, qs, kn, cq, crow, lse = saved[l]
        gq, gk = q_norm_g[l].reshape(1, DA), k_norm_g[l].reshape(1, DA)
        b3 = lambda a: a.reshape(bsz, seq, a.shape[-1])
        f2 = lambda a: a.reshape(t, a.shape[-1])
        dycs, do, dz, dl, d_wout = _bwd_outproj(dy, ycs2, yatt2, f2(o), f2(ohp), pz, w_out_full[l],
                                                f"bwd_outproj_{l}")
        dqs, dkn, dv, dcrow = _bwd_attn(qs, kn, b3(vb), b3(do), cq, lse, b3(dl), crow, f"bwd_attn_{l}")
        dpq, dpk, dpf, d_gq, d_gk, d_bf = _bwd_attn_post(b3(pq), b3(pk), dqs, dkn, b3(pf),
                                                        dcrow.reshape(bsz, NH, seq), gq, gk, bf_pad[l][None],
                                                        f"bwd_attn_post_{l}")
        dcf, dsc, d_cfdw, d_cfb, d_lng, d_lnb, d_pw, d_scdw = _bwd_conv(
            b3(pcf), b3(psc), b3(dycs), cf_dw_full[l], cf_dw_b[l][None], cf_ln_g[l][None], cf_ln_b[l][None],
            cf_pw_full[l], sc_dw_full[l], f"bwd_conv_{l}")
        dy, d_ng, d_win = _bwd_inproj(xs[l], norm_g[l][None], dy, w_in_full[l], f2(dcf), f2(dsc), f2(dpq), f2(dpk),
                                      f2(dv), dz, f2(dpf), f"bwd_inproj_{l}")
        grads[l] = dict(norm_g=d_ng[0], w_in=d_win[:, :N_IN], b_f=d_bf[0, :NH], cf_dw=d_cfdw, cf_dw_b=d_cfb[0],
                        cf_ln_g=d_lng[0], cf_ln_b=d_lnb[0], cf_pw=d_pw, sc_dw=d_scdw,
                        q_norm_g=d_gq.reshape(NH, HD), k_norm_g=d_gk.reshape(NH, HD), w_out=d_wout)
    grad_x = dy.reshape(bsz, seq, D)
    gw = {n: jnp.stack([grads[l][n] for l in range(DEPTH)]) for n in grads[0]}

    rep_names = ("norm_g", "b_f", "cf_dw_b", "cf_ln_g", "cf_ln_b", "q_norm_g", "k_norm_g")
    rep = jnp.concatenate([jnp.pad(gw[n].reshape(-1), (0, (-gw[n].size) % LANES)) for n in rep_names]
                          + [jnp.pad(loss_part.reshape(-1), (0, LANES - 1))]).reshape(-1, LANES)
    blocks = []
    for k in range(4):
        small = jnp.concatenate([gw["cf_dw"][:, :, 64 * k:64 * (k + 1)].reshape(-1),
                                 gw["sc_dw"][:, :, 64 * k:64 * (k + 1)].reshape(-1)]).reshape(-1, LANES)
        blocks.append(_pad_rows(jnp.concatenate([
            _rows(gw["w_in"][:, :, NSH * k:NSH * (k + 1)]), _rows(gw["w_out"][:, 256 * k:256 * (k + 1), :]),
            _rows(gw["cf_pw"][:, 64 * k:64 * (k + 1), :]), small, rep], axis=0), 112))
    pack = jnp.stack(blocks)
    cidx = lax.axis_index("c").astype(jnp.int32).reshape(1)
    pair = _rs_add_pair(pack, _rs_swap_halves(pack), cidx)
    red = _rs_join_halves(_rs_add_chips(_rs_exchange_chips(pair)))

    n_small = (DEPTH * CFW * 64 + DEPTH * SCW * 64) // LANES
    pos = [0]

    def take(nrows):
        pos[0] += nrows
        return red[pos[0] - nrows:pos[0]]

    g = {}
    g["w_in"] = take(n_win).reshape(DEPTH, D, NSH)
    g["w_out"] = take(n_wout).reshape(DEPTH, 256, D)
    g["cf_pw"] = take(n_pw).reshape(DEPTH, 64, DC)
    small = take(n_small).reshape(-1)
    g["cf_dw"] = small[:n_cfdw].reshape(DEPTH, CFW, 64)
    g["sc_dw"] = small[n_cfdw:].reshape(DEPTH, SCW, 64)
    shapes = dict(norm_g=norm_g.shape, b_f=b_f.shape, cf_dw_b=cf_dw_b.shape, cf_ln_g=cf_ln_g.shape,
                  cf_ln_b=cf_ln_b.shape, q_norm_g=q_norm_g.shape, k_norm_g=k_norm_g.shape)
    for n in rep_names:
        size = 1
        for s in shapes[n]:
            size *= s
        g[n] = take(-(-size // LANES)).reshape(-1)[:size].reshape(shapes[n])
    loss = take(1)[0, 0]

    order = ("norm_g", "w_in", "b_f", "cf_dw", "cf_dw_b", "cf_ln_g", "cf_ln_b", "cf_pw", "sc_dw", "q_norm_g",
             "k_norm_g", "w_out")
    weights = dict(norm_g=norm_g, w_in=w_in, b_f=b_f, cf_dw=cf_dw, cf_dw_b=cf_dw_b, cf_ln_g=cf_ln_g, cf_ln_b=cf_ln_b,
                   cf_pw=cf_pw, sc_dw=sc_dw, q_norm_g=q_norm_g, k_norm_g=k_norm_g, w_out=w_out)
    ms = dict(norm_g=m_norm_g, w_in=m_w_in, b_f=m_b_f, cf_dw=m_cf_dw, cf_dw_b=m_cf_dw_b, cf_ln_g=m_cf_ln_g,
              cf_ln_b=m_cf_ln_b, cf_pw=m_cf_pw, sc_dw=m_sc_dw, q_norm_g=m_q_norm_g, k_norm_g=m_k_norm_g, w_out=m_w_out)
    vs = dict(norm_g=v_norm_g, w_in=v_w_in, b_f=v_b_f, cf_dw=v_cf_dw, cf_dw_b=v_cf_dw_b, cf_ln_g=v_cf_ln_g,
              cf_ln_b=v_cf_ln_b, cf_pw=v_cf_pw, sc_dw=v_sc_dw, q_norm_g=v_q_norm_g, k_norm_g=v_k_norm_g, w_out=v_w_out)
    delta, new_m, new_v = {}, {}, {}
    two_d = lambda a: a.reshape(-1, a.shape[-1])
    for n in ("w_in", "w_out"):
        outs = _adamw(two_d(weights[n]), two_d(g[n]), two_d(ms[n]), two_d(vs[n]), 256, f"adamw_{n}")
        delta[n], new_m[n], new_v[n] = (a.reshape(weights[n].shape) for a in outs)
    small_names = [n for n in order if n not in ("w_in", "w_out")]
    flat = lambda d: _pad_rows(jnp.pad(jnp.concatenate([d[n].reshape(-1) for n in small_names]),
                                       (0, (-sum(weights[n].size for n in small_names)) % LANES)).reshape(-1, LANES), 8)
    fw = flat(weights)
    outs = _adamw(fw, flat(g), flat(ms), flat(vs), fw.shape[0], "adamw_small")
    off = 0
    for n in small_names:
        size = weights[n].size
        for dst, src in zip((delta, new_m, new_v), outs):
            dst[n] = src.reshape(-1)[off:off + size].reshape(weights[n].shape)
        off += size

    return (loss, grad_x, *[g[n] for n in order], *[delta[n] for n in order], *[new_m[n] for n in order],
            *[new_v[n] for n in order])
```

```python
import jax
import jax.numpy as jnp
from jax import lax
from jax.experimental import pallas as pl
from jax.experimental.pallas import tpu as pltpu

F32 = jnp.float32
BF = jnp.bfloat16
MESH = pl.DeviceIdType.MESH

DEPTH = 2
D = 1024
DC = 256
DA = 512
NH = 8
HD = 64
CFW = 31
SCW = 3
N_IN = 3848
NP = 3968
NSH = N_IN // 4
HALO = 32
EPS = 1e-6
LANES = 128
VMEM_LIMIT = 56 * 1024 * 1024

C_CF, C_SC, C_Q, C_K, C_V, C_Z, C_F = 0, 768, 1792, 2304, 2816, 3328, 3840

ADAM_LR = 0.001
ADAM_B1 = 0.9
ADAM_B2 = 0.999
ADAM_EPS = 1e-08
ADAM_WD = 0.01
ADAM_STEP = 10


def _cparams(sem=None):
    return pltpu.CompilerParams(dimension_semantics=sem, vmem_limit_bytes=VMEM_LIMIT)


def _dot(a, b):
    return jnp.dot(a, b, preferred_element_type=F32)


def _dot_nt(a, b):
    return lax.dot_general(a, b, (((1,), (1,)), ((), ())), preferred_element_type=F32)


def _dot_tn(a, b):
    return lax.dot_general(a, b, (((0,), (0,)), ((), ())), preferred_element_type=F32)


def _split3(x):
    hi = x.astype(BF)
    r1 = x - hi.astype(F32)
    mid = r1.astype(BF)
    lo = (r1 - mid.astype(F32)).astype(BF)
    return hi, mid, lo


def _dot_exact(a_bf, x):
    hi, mid, lo = _split3(x)
    return _dot(a_bf, hi) + _dot(a_bf, mid) + _dot(a_bf, lo)


def _sigmoid(x):
    return 1.0 / (1.0 + jnp.exp(-x))


def _seg_sum64(x):
    i = lax.broadcasted_iota(jnp.int32, (LANES, LANES), 0)
    j = lax.broadcasted_iota(jnp.int32, (LANES, LANES), 1)
    g = ((i >= HD) == (j >= HD)).astype(BF)
    hi, mid, lo = _split3(x)
    return _dot(hi, g) + _dot(mid, g) + _dot(lo, g)


def _fwd_inproj(x, g, w, name):
    t = x.shape[0]
    tm = 256

    def body(x_ref, g_ref, w_ref, cf_ref, sc_ref, q_ref, k_ref, v_ref, z_ref, f_ref):
        xt = x_ref[...]
        r = lax.rsqrt(jnp.mean(xt * xt, axis=-1, keepdims=True) + EPS)
        h = ((xt * r) * g_ref[...]).astype(BF)
        cf_ref[:, 0:512] = _dot(h, w_ref[:, 0:512])
        cf_ref[:, 512:768] = _dot(h, w_ref[:, 512:768])
        sc_ref[:, 0:512] = _dot(h, w_ref[:, C_SC:C_SC + 512])
        sc_ref[:, 512:1024] = _dot(h, w_ref[:, C_SC + 512:C_Q])
        q_ref[...] = _dot(h, w_ref[:, C_Q:C_K])
        k_ref[...] = _dot(h, w_ref[:, C_K:C_V])
        v_ref[...] = _dot(h, w_ref[:, C_V:C_Z]).astype(BF)
        z_ref[...] = _dot(h, w_ref[:, C_Z:C_F])
        f_ref[...] = _dot(h, w_ref[:, C_F:NP])

    row = lambda n: pl.BlockSpec((tm, n), lambda i: (i, 0))
    return pl.pallas_call(
        body, grid=(t // tm,),
        in_specs=[row(D), pl.BlockSpec((1, D), lambda i: (0, 0)), pl.BlockSpec((D, NP), lambda i: (0, 0))],
        out_specs=[row(768), row(1024), row(DA), row(DA), row(DA), row(DA), row(LANES)],
        out_shape=[jax.ShapeDtypeStruct((t, 768), F32), jax.ShapeDtypeStruct((t, 1024), F32),
                   jax.ShapeDtypeStruct((t, DA), F32), jax.ShapeDtypeStruct((t, DA), F32),
                   jax.ShapeDtypeStruct((t, DA), BF), jax.ShapeDtypeStruct((t, DA), F32),
                   jax.ShapeDtypeStruct((t, LANES), F32)],
        compiler_params=_cparams(("arbitrary",)), name=name)(x, g, w)


def _conformer_fwd(a, g, ha, hg, first, ubuf, dw_ref, bias, lng, lnb):
    ts = a.shape[0]
    u0 = a * _sigmoid(g)
    ubuf[0:HALO, :] = jnp.where(first, 0.0, ha * _sigmoid(hg))
    ubuf[HALO:HALO + ts, :] = u0
    u1 = jnp.zeros((ts, DC), F32) + bias
    for k in range(CFW):
        u1 = u1 + dw_ref[k:k + 1, :] * ubuf[pl.ds(HALO - (CFW - 1) + k, ts), :]
    mu = jnp.mean(u1, axis=-1, keepdims=True)
    xc = u1 - mu
    rstd = lax.rsqrt(jnp.mean(xc * xc, axis=-1, keepdims=True) + EPS)
    n = xc * rstd
    u2 = n * lng + lnb
    s2 = _sigmoid(u2)
    u3 = u2 * s2
    return u0, n, rstd, u2, s2, u3


def _shortconv_fwd(c, xs, hc, hx, first, mbuf, dw_ref):
    ts = c.shape[0]
    mbuf[0:HALO, :] = jnp.where(first, 0.0, hc * hx)
    mbuf[HALO:HALO + ts, :] = c * xs
    cv = jnp.zeros((ts, DC), F32)
    for k in range(SCW):
        cv = cv + dw_ref[k:k + 1, :] * mbuf[pl.ds(HALO - (SCW - 1) + k, ts), :]
    return cv


def _conv_specs(bsz, seq, ts, order):
    nt = seq // ts
    tile = (lambda i: i) if order > 0 else (lambda i: nt - 1 - i)
    hrow = lambda i: jnp.maximum(tile(i) * (ts // HALO) - 1, 0)
    cur = lambda n: pl.BlockSpec((None, ts, n), lambda b, i: (b, tile(i), 0))
    halo = lambda j: pl.BlockSpec((None, HALO, DC), lambda b, i: (b, hrow(i), j))
    full = lambda r, c: pl.BlockSpec((r, c), lambda b, i: (0, 0))
    return nt, tile, cur, halo, full


def _fwd_conv(pcf, psc, cf_dw, cf_dw_b, ln_g, ln_b, cf_pw, sc_dw, name):
    bsz, seq, _ = pcf.shape
    ts = 512
    nt, tile, cur, halo, full = _conv_specs(bsz, seq, ts, +1)

    def body(cf_ref, ha_ref, hg_ref, sc_ref, hc_ref, hx_ref, dw_ref, b_ref, lg_ref, lb_ref, pw_ref, sdw_ref,
             y_ref, ubuf, mbuf):
        first = pl.program_id(1) == 0
        _, _, _, _, _, u3 = _conformer_fwd(cf_ref[:, 0:256], cf_ref[:, 256:512], ha_ref[...], hg_ref[...], first,
                                           ubuf, dw_ref, b_ref[...], lg_ref[...], lb_ref[...])
        z = cf_ref[:, 512:768]
        y_ref[:, 0:256] = _dot(u3.astype(BF), pw_ref[...]) * (z * _sigmoid(z))
        cv = _shortconv_fwd(sc_ref[:, 256:512], sc_ref[:, 512:768], hc_ref[...], hx_ref[...], first, mbuf, sdw_ref)
        zs = sc_ref[:, 768:1024]
        y_ref[:, 256:512] = sc_ref[:, 0:256] * cv * (zs * _sigmoid(zs))

    return pl.pallas_call(
        body, grid=(bsz, nt),
        in_specs=[cur(768), halo(0), halo(1), cur(1024), halo(1), halo(2),
                  full(CFW, DC), full(1, DC), full(1, DC), full(1, DC), full(DC, DC), full(SCW, DC)],
        out_specs=pl.BlockSpec((None, ts, 512), lambda b, i: (b, i, 0)),
        out_shape=jax.ShapeDtypeStruct((bsz, seq, 512), F32),
        scratch_shapes=[pltpu.VMEM((HALO + ts, DC), F32), pltpu.VMEM((HALO + ts, DC), F32)],
        compiler_params=_cparams(("arbitrary", "arbitrary")), name=name,
    )(pcf, pcf, pcf, psc, psc, psc, cf_dw, cf_dw_b, ln_g, ln_b, cf_pw, sc_dw)


def _head_rms(xb, gb):
    ms = _seg_sum64(xb * xb) * (1.0 / HD)
    r = lax.rsqrt(ms + EPS)
    xhat = xb * r
    return xhat, r, xhat * gb


def _fwd_attn_prep(pq, pk, pf, gq, gk, bf, name):
    bsz, seq, _ = pq.shape
    ts = 512
    nt = seq // ts

    def body(q_ref, k_ref, f_ref, gq_ref, gk_ref, bf_ref, qs_ref, kn_ref, cq_ref, crow_ref, carry):
        @pl.when(pl.program_id(1) == 0)
        def _():
            carry[...] = jnp.zeros_like(carry)

        for jb in range(DA // LANES):
            sl = slice(jb * LANES, (jb + 1) * LANES)
            _, _, qn = _head_rms(q_ref[:, sl], gq_ref[:, sl])
            qs_ref[:, sl] = (qn * (1.0 / 8.0)).astype(BF)
            _, _, kn = _head_rms(k_ref[:, sl], gk_ref[:, sl])
            kn_ref[:, sl] = kn.astype(BF)

        xf = f_ref[...] + bf_ref[...]
        lf = jnp.minimum(xf, 0.0) - jnp.log(1.0 + jnp.exp(-jnp.abs(xf)))
        ti = lax.broadcasted_iota(jnp.int32, (ts, ts), 0)
        si = lax.broadcasted_iota(jnp.int32, (ts, ts), 1)
        c = _dot_exact((si <= ti).astype(BF), lf) + carry[...]
        carry[...] = c[ts - 1:ts, :]
        hj = lax.broadcasted_iota(jnp.int32, (LANES, DA), 0)
        ll = lax.broadcasted_iota(jnp.int32, (LANES, DA), 1)
        dd = ll - hj * HD
        chi, cmid, clo = _split3(c)
        e = ((dd >= 0) & (dd < HD)).astype(BF)
        cq_ref[...] = _dot(chi, e) + _dot(cmid, e) + _dot(clo, e)
        hh = lax.broadcasted_iota(jnp.int32, (16, LANES), 0)
        jj = lax.broadcasted_iota(jnp.int32, (16, LANES), 1)
        sel = (hh == jj).astype(BF)
        cr = _dot_nt(sel, chi) + _dot_nt(sel, cmid) + _dot_nt(sel, clo)
        crow_ref[...] = cr[0:NH, :]

    tile = lambda n: pl.BlockSpec((None, ts, n), lambda b, i: (b, i, 0))
    vec = lambda n: pl.BlockSpec((1, n), lambda b, i: (0, 0))
    return pl.pallas_call(
        body, grid=(bsz, nt),
        in_specs=[tile(DA), tile(DA), tile(LANES), vec(DA), vec(DA), vec(LANES)],
        out_specs=[tile(DA), tile(DA), tile(DA), pl.BlockSpec((None, NH, ts), lambda b, i: (b, 0, i))],
        out_shape=[jax.ShapeDtypeStruct((bsz, seq, DA), BF), jax.ShapeDtypeStruct((bsz, seq, DA), BF),
                   jax.ShapeDtypeStruct((bsz, seq, DA), F32), jax.ShapeDtypeStruct((bsz, NH, seq), F32)],
        scratch_shapes=[pltpu.VMEM((1, LANES), F32)],
        compiler_params=_cparams(("arbitrary", "arbitrary")), name=name)(pq, pk, pf, gq, gk, bf)


def _causal_mask(tq, tk):
    r = lax.broadcasted_iota(jnp.int32, (tq, tk), 0)
    c = lax.broadcasted_iota(jnp.int32, (tq, tk), 1)
    return r - c


def _fwd_attn(qs, kn, vb, cq, crow, pz, name):
    bsz, seq, _ = qs.shape
    tq = 256
    nq = seq // tq
    npair = NH // 2

    def body(q_ref, k_ref, v_ref, cq_ref, cr_ref, z_ref, o_ref, ohp_ref, y_ref, lse_ref):
        qi = pl.program_id(2)
        rc = _causal_mask(tq, tq)
        head0 = lax.broadcasted_iota(jnp.int32, (1, LANES), 1) < HD
        q = q_ref[...]
        zq = jnp.zeros_like(q)
        qm = (jnp.where(head0, q, zq), jnp.where(head0, zq, q))
        ct = (cq_ref[:, 0:1], cq_ref[:, HD:HD + 1])

        def step(j, carry, masked):
            m, l, acc, accl = carry
            rows = pl.ds(pl.multiple_of(j * tq, tq), tq)
            kb = k_ref[rows, :]
            vb_t = v_ref[rows, :]
            m_new, l_new, alphas, pv, pvl = [], [], [], [], []
            for hh in range(2):
                s = _dot_nt(qm[hh], kb) + (ct[hh] - cr_ref[hh:hh + 1, rows])
                if masked:
                    s = jnp.where(rc >= 0, s, -1e30)
                m2 = jnp.maximum(m[hh], jnp.max(s, axis=-1, keepdims=True))
                p = jnp.exp(s - m2)
                alpha = jnp.exp(m[hh] - m2)
                l_new.append(alpha * l[hh] + jnp.sum(p, axis=-1, keepdims=True))
                m_new.append(m2)
                alphas.append(alpha)
                pb = p.astype(BF)
                plo = (p - pb.astype(F32)).astype(BF)
                pv.append(_dot(pb, vb_t))
                pvl.append(_dot(plo, vb_t))
            al = jnp.where(head0, alphas[0], alphas[1])
            acc2 = al * acc + jnp.where(head0, pv[0], pv[1])
            accl2 = al * accl + jnp.where(head0, pvl[0], pvl[1])
            return tuple(m_new), tuple(l_new), acc2, accl2

        neg = jnp.full((tq, 1), -1e30, F32)
        zcol = jnp.zeros((tq, 1), F32)
        zacc = jnp.zeros((tq, LANES), F32)
        carry = lax.fori_loop(0, qi, lambda j, cy: step(j, cy, False), ((neg, neg), (zcol, zcol), zacc, zacc))
        m, l, acc, accl = step(qi, carry, True)
        inv = jnp.where(head0, 1.0 / l[0], 1.0 / l[1])
        o = acc * inv
        z = z_ref[...]
        o_ref[...] = o
        ohp_ref[...] = (acc + accl) * inv
        y_ref[...] = o * (z * _sigmoid(z))
        lse_ref[...] = jnp.where(head0, m[0] + jnp.log(l[0]), m[1] + jnp.log(l[1]))

    qblk = pl.BlockSpec((None, tq, LANES), lambda b, h, i: (b, i, h))
    kvblk = pl.BlockSpec((None, seq, LANES), lambda b, h, i: (b, 0, h))
    return pl.pallas_call(
        body, grid=(bsz, npair, nq),
        in_specs=[qblk, kvblk, kvblk, qblk, pl.BlockSpec((None, None, 2, seq), lambda b, h, i: (b, h, 0, 0)), qblk],
        out_specs=[qblk, qblk, qblk, qblk],
        out_shape=[jax.ShapeDtypeStruct((bsz, seq, DA), F32)] * 4,
        compiler_params=_cparams(("arbitrary", "arbitrary", "arbitrary")), name=name)(qs, kn, vb, cq, crow, pz)


def _fwd_outproj(x, ycs, yatt, wo, name):
    t = x.shape[0]
    tm = 512

    def body(x_ref, a_ref, b_ref, w_ref, o_ref):
        o_ref[...] = (x_ref[...] + _dot(a_ref[...].astype(BF), w_ref[0:512, :])
                      + _dot(b_ref[...].astype(BF), w_ref[512:1024, :]))

    row = lambda n: pl.BlockSpec((tm, n), lambda i: (i, 0))
    return pl.pallas_call(
        body, grid=(t // tm,),
        in_specs=[row(D), row(512), row(512), pl.BlockSpec((D, D), lambda i: (0, 0))],
        out_specs=row(D), out_shape=jax.ShapeDtypeStruct((t, D), F32),
        compiler_params=_cparams(("arbitrary",)), name=name)(x, ycs, yatt, wo)


def _fwd_outproj_loss(x, ycs, yatt, wo, target, name):
    t = x.shape[0]
    tm = 512

    def body(x_ref, a_ref, b_ref, w_ref, t_ref, dy_ref, loss_ref):
        @pl.when(pl.program_id(0) == 0)
        def _():
            loss_ref[...] = jnp.zeros_like(loss_ref)

        y = (x_ref[...] + _dot(a_ref[...].astype(BF), w_ref[0:512, :])
             + _dot(b_ref[...].astype(BF), w_ref[512:1024, :]))
        err = y - t_ref[...]
        dy_ref[...] = err * (1.0 / D)
        per_tok = jnp.mean(err * err, axis=-1, keepdims=True)
        loss_ref[...] += 0.5 * jnp.sum(per_tok, axis=0, keepdims=True)

    row = lambda n: pl.BlockSpec((tm, n), lambda i: (i, 0))
    return pl.pallas_call(
        body, grid=(t // tm,),
        in_specs=[row(D), row(512), row(512), pl.BlockSpec((D, D), lambda i: (0, 0)), row(D)],
        out_specs=[row(D), pl.BlockSpec((1, 1), lambda i: (0, 0))],
        out_shape=[jax.ShapeDtypeStruct((t, D), F32), jax.ShapeDtypeStruct((1, 1), F32)],
        compiler_params=_cparams(("arbitrary",)), name=name)(x, ycs, yatt, wo, target)


def _dsilu(x, s):
    return s * (1.0 + x * (1.0 - s))


def _bwd_outproj(dy, ycs, yatt, o, ohp, pz, wo, name):
    t = dy.shape[0]
    tm = 256

    def body(dy_ref, a_ref, b_ref, o_ref, ohp_ref, z_ref, w_ref, dcs_ref, do_ref, dz_ref, dl_ref, dw_ref):
        @pl.when(pl.program_id(0) == 0)
        def _():
            dw_ref[...] = jnp.zeros_like(dw_ref)

        dyb = dy_ref[...].astype(BF)
        dw_ref[0:512, :] += _dot_tn(a_ref[...].astype(BF), dyb)
        dw_ref[512:1024, :] += _dot_tn(b_ref[...].astype(BF), dyb)
        dcs_ref[...] = _dot_nt(dyb, w_ref[0:512, :])
        dyatt = _dot_nt(dyb, w_ref[512:1024, :])
        z = z_ref[...]
        sz = _sigmoid(z)
        o_t = o_ref[...]
        dob = (dyatt * (z * sz)).astype(BF)
        do_ref[...] = dob
        dz_ref[...] = dyatt * o_t * _dsilu(z, sz)
        prod = dob.astype(F32) * ohp_ref[...]
        for jb in range(DA // LANES):
            sl = slice(jb * LANES, (jb + 1) * LANES)
            dl_ref[:, sl] = _seg_sum64(prod[:, sl])

    row = lambda n: pl.BlockSpec((tm, n), lambda i: (i, 0))
    return pl.pallas_call(
        body, grid=(t // tm,),
        in_specs=[row(D), row(512), row(512), row(DA), row(DA), row(DA), pl.BlockSpec((D, D), lambda i: (0, 0))],
        out_specs=[row(512), row(DA), row(DA), row(DA), pl.BlockSpec((D, D), lambda i: (0, 0))],
        out_shape=[jax.ShapeDtypeStruct((t, 512), F32), jax.ShapeDtypeStruct((t, DA), BF),
                   jax.ShapeDtypeStruct((t, DA), F32), jax.ShapeDtypeStruct((t, DA), F32),
                   jax.ShapeDtypeStruct((D, D), F32)],
        compiler_params=_cparams(("arbitrary",)), name=name)(dy, ycs, yatt, o, ohp, pz, wo)


def _bwd_attn(qs, kn, vb, do, cq, lse, dl, crow, name):
    bsz, seq, _ = qs.shape
    tq = 256
    nq = seq // tq
    npair = NH // 2

    def body(q_ref, k_ref, v_ref, do_ref, cq_ref, lse_ref, dl_ref, cr_ref, dq_ref, dk_ref, dv_ref, dc_ref, qm, dom):
        rc = _causal_mask(tq, tq)
        head0 = lax.broadcasted_iota(jnp.int32, (1, LANES), 1) < HD
        zb = jnp.zeros((seq, LANES), BF)
        qm[0] = jnp.where(head0, q_ref[...], zb)
        qm[1] = jnp.where(head0, zb, q_ref[...])
        dom[0] = jnp.where(head0, do_ref[...], zb)
        dom[1] = jnp.where(head0, zb, do_ref[...])
        dq_ref[...] = jnp.zeros_like(dq_ref)

        def kloop(kj, _):
            krows = pl.ds(pl.multiple_of(kj * tq, tq), tq)
            kb = k_ref[krows, :]
            vb_t = v_ref[krows, :]
            zk = jnp.zeros_like(kb)
            km = (jnp.where(head0, kb, zk), jnp.where(head0, zk, kb))

            def qstep(qi, carry, masked):
                dk, dv, cs = carry
                qrows = pl.ds(pl.multiple_of(qi * tq, tq), tq)
                dq = jnp.zeros((tq, LANES), F32)
                cs_new = []
                for hh in range(2):
                    c0 = hh * HD
                    qh = qm[hh, qrows, :]
                    doh = dom[hh, qrows, :]
                    s = _dot_nt(qh, kb) + (cq_ref[qrows, c0:c0 + 1] - cr_ref[hh:hh + 1, krows])
                    if masked:
                        s = jnp.where(rc >= 0, s, -1e30)
                    p = jnp.exp(s - lse_ref[qrows, c0:c0 + 1])
                    ds = p * (_dot_nt(doh, vb_t) - dl_ref[qrows, c0:c0 + 1])
                    dsb = ds.astype(BF)
                    dv = dv + _dot_tn(p.astype(BF), doh)
                    dk = dk + _dot_tn(dsb, qh)
                    dq = dq + _dot(dsb, km[hh])
                    cs_new.append(cs[hh] + jnp.sum(ds, axis=0, keepdims=True))
                dq_ref[qrows, :] += dq
                return dk, dv, tuple(cs_new)

            zero = jnp.zeros((tq, LANES), F32)
            zrow = jnp.zeros((1, tq), F32)
            carry = qstep(kj, (zero, zero, (zrow, zrow)), True)
            dk, dv, cs = lax.fori_loop(kj + 1, nq, lambda qi, cy: qstep(qi, cy, False), carry)
            dk_ref[krows, :] = dk
            dv_ref[krows, :] = dv
            dc_ref[0:1, krows] = -cs[0]
            dc_ref[1:2, krows] = -cs[1]
            return 0

        lax.fori_loop(0, nq, kloop, 0)

    blk = pl.BlockSpec((None, seq, LANES), lambda b, h: (b, 0, h))
    rowblk = pl.BlockSpec((None, None, 2, seq), lambda b, h: (b, h, 0, 0))
    return pl.pallas_call(
        body, grid=(bsz, npair),
        in_specs=[blk, blk, blk, blk, blk, blk, blk, rowblk],
        out_specs=[blk, blk, blk, rowblk],
        out_shape=[jax.ShapeDtypeStruct((bsz, seq, DA), F32)] * 3 + [jax.ShapeDtypeStruct((bsz, npair, 2, seq), F32)],
        scratch_shapes=[pltpu.VMEM((2, seq, LANES), BF), pltpu.VMEM((2, seq, LANES), BF)],
        compiler_params=_cparams(("arbitrary", "arbitrary")), name=name)(qs, kn, vb, do, cq, lse, dl, crow)


def _bwd_attn_post(pq, pk, dqs, dkn, pf, dcrow, gq, gk, bf, name):
    bsz, seq, _ = pq.shape
    ts = 512
    nt = seq // ts

    def body(q_ref, k_ref, dq_ref, dk_ref, f_ref, dc_ref, gq_ref, gk_ref, bf_ref,
             dpq_ref, dpk_ref, dpf_ref, dgq_ref, dgk_ref, dbf_ref, carry):
        @pl.when((pl.program_id(0) == 0) & (pl.program_id(1) == 0))
        def _():
            dgq_ref[...] = jnp.zeros_like(dgq_ref)
            dgk_ref[...] = jnp.zeros_like(dgk_ref)
            dbf_ref[...] = jnp.zeros_like(dbf_ref)

        @pl.when(pl.program_id(1) == 0)
        def _():
            carry[...] = jnp.zeros_like(carry)

        for x_ref, dx_ref, g_ref, dp_ref, dg_ref, scale in ((q_ref, dq_ref, gq_ref, dpq_ref, dgq_ref, 1.0 / 8.0),
                                                            (k_ref, dk_ref, gk_ref, dpk_ref, dgk_ref, 1.0)):
            for jb in range(DA // LANES):
                sl = slice(jb * LANES, (jb + 1) * LANES)
                gb = g_ref[:, sl]
                xhat, r, _ = _head_rms(x_ref[:, sl], gb)
                dn = dx_ref[:, sl] * scale
                dg_ref[:, sl] += jnp.sum(dn * xhat, axis=0, keepdims=True)
                dxh = dn * gb
                dp_ref[:, sl] = r * (dxh - xhat * (_seg_sum64(dxh * xhat) * (1.0 / HD)))

        ui = lax.broadcasted_iota(jnp.int32, (ts, ts), 0)
        tj = lax.broadcasted_iota(jnp.int32, (ts, ts), 1)
        tri = (tj >= ui).astype(BF)
        dc = jnp.concatenate([dc_ref[...], jnp.zeros((LANES - NH, ts), F32)], axis=0)
        hi, mid, lo = _split3(dc)
        dlf = _dot_nt(tri, hi) + _dot_nt(tri, mid) + _dot_nt(tri, lo) + carry[...]
        carry[...] = dlf[0:1, :]
        xf = f_ref[...] + bf_ref[...]
        lane = lax.broadcasted_iota(jnp.int32, (ts, LANES), 1)
        dfl = jnp.where(lane < NH, dlf * _sigmoid(-xf), 0.0)
        dpf_ref[...] = dfl
        dbf_ref[...] += jnp.sum(dfl, axis=0, keepdims=True)

    rev = lambda i: nt - 1 - i
    tile = lambda n: pl.BlockSpec((None, ts, n), lambda b, i: (b, rev(i), 0))
    vec = lambda n: pl.BlockSpec((1, n), lambda b, i: (0, 0))
    return pl.pallas_call(
        body, grid=(bsz, nt),
        in_specs=[tile(DA), tile(DA), tile(DA), tile(DA), tile(LANES),
                  pl.BlockSpec((None, NH, ts), lambda b, i: (b, 0, rev(i))), vec(DA), vec(DA), vec(LANES)],
        out_specs=[tile(DA), tile(DA), tile(LANES), vec(DA), vec(DA), vec(LANES)],
        out_shape=[jax.ShapeDtypeStruct((bsz, seq, DA), F32), jax.ShapeDtypeStruct((bsz, seq, DA), F32),
                   jax.ShapeDtypeStruct((bsz, seq, LANES), F32), jax.ShapeDtypeStruct((1, DA), F32),
                   jax.ShapeDtypeStruct((1, DA), F32), jax.ShapeDtypeStruct((1, LANES), F32)],
        scratch_shapes=[pltpu.VMEM((1, LANES), F32)],
        compiler_params=_cparams(("arbitrary", "arbitrary")), name=name)(pq, pk, dqs, dkn, pf, dcrow, gq, gk, bf)


def _bwd_conv(pcf, psc, dycs, cf_dw, cf_dw_b, ln_g, ln_b, cf_pw, sc_dw, name):
    bsz, seq, _ = pcf.shape
    ts = 512
    nt, tile, cur, halo, full = _conv_specs(bsz, seq, ts, -1)

    def body(cf_ref, ha_ref, hg_ref, sc_ref, hc_ref, hx_ref, dy_ref, dw_ref, b_ref, lg_ref, lb_ref, pw_ref, sdw_ref,
             dcf_ref, dsc_ref, ddw_ref, db_ref, dlg_ref, dlb_ref, dpw_ref, dsdw_ref,
             ubuf, mbuf, dubuf, dcbuf):
        step = pl.program_id(1)
        first = tile(step) == 0

        @pl.when((pl.program_id(0) == 0) & (step == 0))
        def _():
            for r in (ddw_ref, db_ref, dlg_ref, dlb_ref, dpw_ref, dsdw_ref):
                r[...] = jnp.zeros_like(r)

        @pl.when(step == 0)
        def _():
            dubuf[ts:ts + HALO, :] = jnp.zeros((HALO, DC), F32)
            dcbuf[ts:ts + HALO, :] = jnp.zeros((HALO, DC), F32)

        a = cf_ref[:, 0:256]
        g = cf_ref[:, 256:512]
        z = cf_ref[:, 512:768]
        lng = lg_ref[...]
        u0, n, rstd, u2, s2, u3 = _conformer_fwd(a, g, ha_ref[...], hg_ref[...], first, ubuf, dw_ref,
                                                 b_ref[...], lng, lb_ref[...])
        u3b = u3.astype(BF)
        p = _dot(u3b, pw_ref[...])
        sz = _sigmoid(z)
        dy = dy_ref[:, 0:256]
        dcf_ref[:, 512:768] = dy * p * _dsilu(z, sz)
        dpb = (dy * (z * sz)).astype(BF)
        dpw_ref[...] += _dot_tn(u3b, dpb)
        du2 = _dot_nt(dpb, pw_ref[...]) * _dsilu(u2, s2)
        dlg_ref[...] += jnp.sum(du2 * n, axis=0, keepdims=True)
        dlb_ref[...] += jnp.sum(du2, axis=0, keepdims=True)
        dn = du2 * lng
        du1 = rstd * (dn - jnp.mean(dn, axis=-1, keepdims=True) - n * jnp.mean(dn * n, axis=-1, keepdims=True))
        db_ref[...] += jnp.sum(du1, axis=0, keepdims=True)
        dubuf[0:ts, :] = du1
        du0 = jnp.zeros((ts, DC), F32)
        for k in range(CFW):
            du0 = du0 + dw_ref[k:k + 1, :] * dubuf[pl.ds(CFW - 1 - k, ts), :]
            ddw_ref[k:k + 1, :] += jnp.sum(du1 * ubuf[pl.ds(HALO - (CFW - 1) + k, ts), :], axis=0, keepdims=True)
        dubuf[ts:ts + HALO, :] = du1[0:HALO, :]
        sg = _sigmoid(g)
        dcf_ref[:, 0:256] = du0 * sg
        dcf_ref[:, 256:512] = du0 * a * sg * (1.0 - sg)

        bb = sc_ref[:, 0:256]
        c = sc_ref[:, 256:512]
        xs = sc_ref[:, 512:768]
        zs = sc_ref[:, 768:1024]
        cv = _shortconv_fwd(c, xs, hc_ref[...], hx_ref[...], first, mbuf, sdw_ref)
        szs = _sigmoid(zs)
        dys = dy_ref[:, 256:512]
        gate = zs * szs
        dsc_ref[:, 0:256] = dys * cv * gate
        dsc_ref[:, 768:1024] = dys * bb * cv * _dsilu(zs, szs)
        dcv = dys * bb * gate
        dcbuf[0:ts, :] = dcv
        dm = jnp.zeros((ts, DC), F32)
        for k in range(SCW):
            dm = dm + sdw_ref[k:k + 1, :] * dcbuf[pl.ds(SCW - 1 - k, ts), :]
            dsdw_ref[k:k + 1, :] += jnp.sum(dcv * mbuf[pl.ds(HALO - (SCW - 1) + k, ts), :], axis=0, keepdims=True)
        dcbuf[ts:ts + HALO, :] = dcv[0:HALO, :]
        dsc_ref[:, 256:512] = dm * xs
        dsc_ref[:, 512:768] = dm * c

    outt = lambda n: pl.BlockSpec((None, ts, n), lambda b, i: (b, tile(i), 0))
    return pl.pallas_call(
        body, grid=(bsz, nt),
        in_specs=[cur(768), halo(0), halo(1), cur(1024), halo(1), halo(2), cur(512),
                  full(CFW, DC), full(1, DC), full(1, DC), full(1, DC), full(DC, DC), full(SCW, DC)],
        out_specs=[outt(768), outt(1024), full(CFW, DC), full(1, DC), full(1, DC), full(1, DC), full(DC, DC),
                   full(SCW, DC)],
        out_shape=[jax.ShapeDtypeStruct((bsz, seq, 768), F32), jax.ShapeDtypeStruct((bsz, seq, 1024), F32),
                   jax.ShapeDtypeStruct((CFW, DC), F32), jax.ShapeDtypeStruct((1, DC), F32),
                   jax.ShapeDtypeStruct((1, DC), F32), jax.ShapeDtypeStruct((1, DC), F32),
                   jax.ShapeDtypeStruct((DC, DC), F32), jax.ShapeDtypeStruct((SCW, DC), F32)],
        scratch_shapes=[pltpu.VMEM((HALO + ts, DC), F32), pltpu.VMEM((HALO + ts, DC), F32),
                        pltpu.VMEM((ts + HALO, DC), F32), pltpu.VMEM((ts + HALO, DC), F32)],
        compiler_params=_cparams(("arbitrary", "arbitrary")), name=name,
    )(pcf, pcf, pcf, psc, psc, psc, dycs, cf_dw, cf_dw_b, ln_g, ln_b, cf_pw, sc_dw)


def _bwd_inproj(x, g, dyres, w, dcf, dsc, dq, dk, dv, dz, df, name):
    t = x.shape[0]
    tm = 256
    pieces = ((C_CF, 768), (C_SC, 1024), (C_Q, DA), (C_K, DA), (C_V, DA), (C_Z, DA), (C_F, LANES))

    def body(x_ref, g_ref, dy_ref, w_hbm, dcf_ref, dsc_ref, dq_ref, dk_ref, dv_ref, dz_ref, df_ref,
             dx_ref, dg_ref, dw_hbm, w_vmem, dw_acc, sem):
        i = pl.program_id(0)

        @pl.when(i == 0)
        def _():
            cp = pltpu.make_async_copy(w_hbm, w_vmem, sem)
            cp.start()
            dw_acc[...] = jnp.zeros_like(dw_acc)
            dg_ref[...] = jnp.zeros_like(dg_ref)
            cp.wait()

        xt = x_ref[...]
        gg = g_ref[...]
        r = lax.rsqrt(jnp.mean(xt * xt, axis=-1, keepdims=True) + EPS)
        xhat = xt * r
        ht = (xhat * gg).astype(BF).T
        dh = jnp.zeros((tm, D), F32)
        for (c0, n), ref in zip(pieces, (dcf_ref, dsc_ref, dq_ref, dk_ref, dv_ref, dz_ref, df_ref)):
            for s0 in range(0, n, 512):
                s1 = min(s0 + 512, n)
                d = ref[:, s0:s1].astype(BF)
                dh = dh + _dot_nt(d, w_vmem[:, c0 + s0:c0 + s1])
                dw_acc[:, c0 + s0:c0 + s1] += _dot(ht, d)
        dg_ref[...] += jnp.sum(dh * xhat, axis=0, keepdims=True)
        dhg = dh * gg
        dx_ref[...] = dy_ref[...] + r * (dhg - xhat * jnp.mean(dhg * xhat, axis=-1, keepdims=True))

        @pl.when(i == pl.num_programs(0) - 1)
        def _():
            out = pltpu.make_async_copy(dw_acc, dw_hbm, sem)
            out.start()
            out.wait()

    row = lambda n: pl.BlockSpec((tm, n), lambda i: (i, 0))
    anyspec = pl.BlockSpec(memory_space=pl.ANY)
    return pl.pallas_call(
        body, grid=(t // tm,),
        in_specs=[row(D), pl.BlockSpec((1, D), lambda i: (0, 0)), row(D), anyspec,
                  row(768), row(1024), row(DA), row(DA), row(DA), row(DA), row(LANES)],
        out_specs=[row(D), pl.BlockSpec((1, D), lambda i: (0, 0)), anyspec],
        out_shape=[jax.ShapeDtypeStruct((t, D), F32), jax.ShapeDtypeStruct((1, D), F32),
                   jax.ShapeDtypeStruct((D, NP), F32)],
        scratch_shapes=[pltpu.VMEM((D, NP), BF), pltpu.VMEM((D, NP), F32), pltpu.SemaphoreType.DMA],
        compiler_params=_cparams(("arbitrary",)), name=name)(x, g, dyres, w, dcf, dsc, dq, dk, dv, dz, df)


def _adamw(w, g, m, v, tr, name):
    rows, cols = w.shape

    def body(w_ref, g_ref, m_ref, v_ref, d_ref, m2_ref, v2_ref):
        gt = g_ref[...]
        m2 = ADAM_B1 * m_ref[...] + (1.0 - ADAM_B1) * gt
        v2 = ADAM_B2 * v_ref[...] + (1.0 - ADAM_B2) * (gt * gt)
        m_hat = m2 / (1.0 - ADAM_B1 ** ADAM_STEP)
        v_hat = v2 / (1.0 - ADAM_B2 ** ADAM_STEP)
        d_ref[...] = -ADAM_LR * (m_hat / (jnp.sqrt(v_hat) + ADAM_EPS) + ADAM_WD * w_ref[...])
        m2_ref[...] = m2
        v2_ref[...] = v2

    blk = pl.BlockSpec((tr, cols), lambda i: (i, 0))
    return pl.pallas_call(
        body, grid=(rows // tr,), in_specs=[blk] * 4, out_specs=[blk] * 3,
        out_shape=[jax.ShapeDtypeStruct((rows, cols), F32)] * 3,
        compiler_params=_cparams(("arbitrary",)), name=name)(w, g, m, v)


def _place():
    x, y, c = lax.axis_index("x"), lax.axis_index("y"), lax.axis_index("c")
    chips = [(1 - x, y), (x, 1 - y), (1 - x, 1 - y)]
    return x, y, c, chips


def _halves(c, rows, align):
    rh = rows // 2
    return pl.ds(pl.multiple_of(c * rh, align), rh), pl.ds(pl.multiple_of((1 - c) * rh, align), rh)


ANY = pl.BlockSpec(memory_space=pl.ANY)


def _all_gather_chips(shards):
    n = len(shards)

    def body(*refs):
        srcs, outs = refs[:n], refs[n:2 * n]
        send_sems, recv_sems, local_sems = refs[2 * n:]
        x, y, c, chips = _place()
        me = 2 * x + y
        sib = (x, y, 1 - c)
        halves = [_halves(c, s.shape[1], 16) for s in shards]
        local = [pltpu.make_async_copy(srcs[i], outs[i].at[:, me], local_sems.at[i]) for i in range(n)]
        for cp in local:
            cp.start()

        def ici(i, j, chip, k):
            mine = halves[i][0]
            return pltpu.make_async_remote_copy(
                src_ref=srcs[i].at[:, mine, :], dst_ref=outs[i].at[:, k, mine, :], send_sem=send_sems.at[6 * i + j],
                recv_sem=recv_sems.at[6 * i + j], device_id=(*chip, c), device_id_type=MESH)

        def d2d(i, j, k, half):
            return pltpu.make_async_remote_copy(
                src_ref=outs[i].at[:, k, half, :], dst_ref=outs[i].at[:, k, half, :], send_sem=send_sems.at[6 * i + 3 + j],
                recv_sem=recv_sems.at[6 * i + 3 + j], device_id=sib, device_id_type=MESH)

        sent = [ici(i, j, chip, me) for i in range(n) for j, chip in enumerate(chips)]
        for cp in sent:
            cp.start()
        passed = []
        for i in range(n):
            for j, (px, py) in enumerate(chips):
                k = 2 * px + py
                ici(i, j, (px, py), k).wait_recv()
                fw = d2d(i, j, k, halves[i][0])
                fw.start()
                passed.append(fw)
        for i in range(n):
            for j, (px, py) in enumerate(chips):
                d2d(i, j, 2 * px + py, halves[i][1]).wait_recv()
        for cp in sent + passed:
            cp.wait_send()
        for cp in local:
            cp.wait()

    return pl.pallas_call(
        body, in_specs=[ANY] * n, out_specs=[ANY] * n,
        out_shape=[jax.ShapeDtypeStruct((s.shape[0], 4) + s.shape[1:], s.dtype) for s in shards],
        scratch_shapes=[pltpu.SemaphoreType.DMA((6 * n,)), pltpu.SemaphoreType.DMA((6 * n,)),
                        pltpu.SemaphoreType.DMA((n,))],
        name="all_gather_chips")(*shards)


def _rs_swap_halves(gs):
    n = len(gs)

    def body(*refs):
        srcs, gots = refs[:n], refs[n:2 * n]
        send_sems, recv_sems = refs[2 * n:]
        x, y, c, _ = _place()
        cps = []
        for i in range(n):
            other = _halves(c, gs[i].shape[2], 8)[1]
            cps.append(pltpu.make_async_remote_copy(
                src_ref=srcs[i].at[:, :, other, :], dst_ref=gots[i], send_sem=send_sems.at[i], recv_sem=recv_sems.at[i],
                device_id=(x, y, 1 - c), device_id_type=MESH))
        for cp in cps:
            cp.start()
        for cp in cps:
            cp.wait()

    return pl.pallas_call(
        body, in_specs=[ANY] * n, out_specs=[ANY] * n,
        out_shape=[jax.ShapeDtypeStruct(g.shape[:2] + (g.shape[2] // 2, g.shape[3]), F32) for g in gs],
        scratch_shapes=[pltpu.SemaphoreType.DMA((n,)), pltpu.SemaphoreType.DMA((n,))], name="rs_swap_halves")(*gs)


def _rs_add_pair(g, got, cidx, tb, name):
    nl, _, rh, cols = got.shape
    nb = rh // tb

    def body(c_ref, g_ref, o_ref, s_ref):
        s_ref[...] = g_ref[...] + o_ref[...]

    blk = lambda half: pl.BlockSpec((None, None, tb, cols), (lambda l, k, i, c: (l, k, c[0] * nb + i, 0)) if half
                                    else (lambda l, k, i, c: (l, k, i, 0)))
    return pl.pallas_call(
        body,
        grid_spec=pltpu.PrefetchScalarGridSpec(num_scalar_prefetch=1, grid=(nl, 4, nb),
                                               in_specs=[blk(True), blk(False)], out_specs=blk(False)),
        out_shape=jax.ShapeDtypeStruct(got.shape, F32),
        compiler_params=_cparams(("arbitrary", "arbitrary", "arbitrary")), name=name)(cidx, g, got)


def _rs_exchange_chips(pairs):
    n = len(pairs)

    def body(*refs):
        srcs, gots = refs[:n], refs[n:2 * n]
        send_sems, recv_sems, local_sems = refs[2 * n:]
        x, y, c, chips = _place()
        me = 2 * x + y
        local = [pltpu.make_async_copy(srcs[i].at[:, me], gots[i].at[:, me], local_sems.at[i]) for i in range(n)]
        for cp in local:
            cp.start()

        def ici(i, j, chip, frm, to):
            return pltpu.make_async_remote_copy(
                src_ref=srcs[i].at[:, to], dst_ref=gots[i].at[:, frm], send_sem=send_sems.at[3 * i + j],
                recv_sem=recv_sems.at[3 * i + j], device_id=(*chip, c), device_id_type=MESH)

        sent = [ici(i, j, (px, py), me, 2 * px + py) for i in range(n) for j, (px, py) in enumerate(chips)]
        for cp in sent:
            cp.start()
        for i in range(n):
            for j, (px, py) in enumerate(chips):
                ici(i, j, (px, py), 2 * px + py, me).wait_recv()
        for cp in sent:
            cp.wait_send()
        for cp in local:
            cp.wait()

    return pl.pallas_call(
        body, in_specs=[ANY] * n, out_specs=[ANY] * n,
        out_shape=[jax.ShapeDtypeStruct(p.shape, F32) for p in pairs],
        scratch_shapes=[pltpu.SemaphoreType.DMA((3 * n,)), pltpu.SemaphoreType.DMA((3 * n,)),
                        pltpu.SemaphoreType.DMA((n,))],
        name="rs_exchange_chips")(*pairs)


def _rs_add_chips(got, tb, name):
    nl, _, rh, cols = got.shape

    def body(g_ref, s_ref):
        s_ref[...] = ((g_ref[0] + g_ref[1]) + g_ref[2]) + g_ref[3]

    return pl.pallas_call(
        body, grid=(nl, rh // tb),
        in_specs=[pl.BlockSpec((None, 4, tb, cols), lambda l, i: (l, 0, i, 0))],
        out_specs=pl.BlockSpec((None, tb, cols), lambda l, i: (l, i, 0)),
        out_shape=jax.ShapeDtypeStruct((nl, rh, cols), F32),
        compiler_params=_cparams(("arbitrary", "arbitrary")), name=name)(got)


def _rs_join_halves(totals):
    n = len(totals)

    def body(*refs):
        srcs, outs = refs[:n], refs[n:2 * n]
        send_sems, recv_sems, local_sems = refs[2 * n:]
        x, y, c, _ = _place()
        cps, local = [], []
        for i in range(n):
            mine, other = _halves(c, 2 * totals[i].shape[1], 8)
            local.append(pltpu.make_async_copy(srcs[i], outs[i].at[:, mine, :], local_sems.at[i]))
            cps.append((pltpu.make_async_remote_copy(
                src_ref=srcs[i], dst_ref=outs[i].at[:, mine, :], send_sem=send_sems.at[i], recv_sem=recv_sems.at[i],
                device_id=(x, y, 1 - c), device_id_type=MESH),
                pltpu.make_async_remote_copy(
                src_ref=srcs[i], dst_ref=outs[i].at[:, other, :], send_sem=send_sems.at[i], recv_sem=recv_sems.at[i],
                device_id=(x, y, 1 - c), device_id_type=MESH)))
        for cp in local:
            cp.start()
        for send, _ in cps:
            send.start()
        for _, recv in cps:
            recv.wait_recv()
        for send, _ in cps:
            send.wait_send()
        for cp in local:
            cp.wait()

    return pl.pallas_call(
        body, in_specs=[ANY] * n, out_specs=[ANY] * n,
        out_shape=[jax.ShapeDtypeStruct((t.shape[0], 2 * t.shape[1], t.shape[2]), F32) for t in totals],
        scratch_shapes=[pltpu.SemaphoreType.DMA((n,)), pltpu.SemaphoreType.DMA((n,)), pltpu.SemaphoreType.DMA((n,))],
        name="rs_join_halves")(*totals)


def _rows(a):
    return a.reshape(-1, LANES)


def _pad_rows(a, mult):
    r = (-a.shape[0]) % mult
    return a if r == 0 else jnp.concatenate([a, jnp.zeros((r, a.shape[1]), a.dtype)], axis=0)


def kernel(x, norm_g, w_in, b_f, cf_dw, cf_dw_b, cf_ln_g, cf_ln_b, cf_pw, sc_dw, q_norm_g, k_norm_g, w_out, loss_target, m_norm_g, m_w_in, m_b_f, m_cf_dw, m_cf_dw_b, m_cf_ln_g, m_cf_ln_b, m_cf_pw, m_sc_dw, m_q_norm_g, m_k_norm_g, m_w_out, v_norm_g, v_w_in, v_b_f, v_cf_dw, v_cf_dw_b, v_cf_ln_g, v_cf_ln_b, v_cf_pw, v_sc_dw, v_q_norm_g, v_k_norm_g, v_w_out):
    bsz, seq, _ = x.shape
    t = bsz * seq

    taps = jnp.concatenate([cf_dw.reshape(-1), sc_dw.reshape(-1)])
    taps_bf = lax.bitcast_convert_type(taps, jnp.bfloat16).reshape(-1, LANES)
    n_pw, n_taps = 2 * 64 * DC // LANES, taps_bf.shape[0]
    small_w = _pad_rows(jnp.concatenate([_rows(cf_pw.astype(BF)), taps_bf], axis=0), 32)[None]
    win_g, wout_g, small_g = _all_gather_chips([w_in.astype(BF), w_out.astype(BF), small_w])
    w_in_full = jnp.pad(jnp.concatenate([win_g[:, k] for k in range(4)], axis=-1), ((0, 0), (0, 0), (0, NP - N_IN)))
    w_out_full = wout_g.reshape(DEPTH, D, D)
    cf_pw_full = jnp.concatenate([small_g[0, k, 0:n_pw].reshape(DEPTH, 64, DC) for k in range(4)], axis=1)
    taps_all = [lax.bitcast_convert_type(small_g[0, k, n_pw:n_pw + n_taps].reshape(-1, 2), F32) for k in range(4)]
    n_cfdw = DEPTH * CFW * 64
    cf_dw_full = jnp.concatenate([tk[:n_cfdw].reshape(DEPTH, CFW, 64) for tk in taps_all], axis=-1)
    sc_dw_full = jnp.concatenate([tk[n_cfdw:].reshape(DEPTH, SCW, 64) for tk in taps_all], axis=-1)

    bf_pad = jnp.pad(b_f, ((0, 0), (0, LANES - NH)))

    xs = [x.reshape(t, D)]
    saved = []
    dy = loss_part = None
    for l in range(DEPTH):
        xl = xs[-1]
        gq, gk = q_norm_g[l].reshape(1, DA), k_norm_g[l].reshape(1, DA)
        pcf, psc, pq, pk, vb, pz, pf = _fwd_inproj(xl, norm_g[l][None], w_in_full[l], f"fwd_inproj_{l}")
        b3 = lambda a: a.reshape(bsz, seq, a.shape[-1])
        ycs = _fwd_conv(b3(pcf), b3(psc), cf_dw_full[l], cf_dw_b[l][None], cf_ln_g[l][None], cf_ln_b[l][None],
                        cf_pw_full[l], sc_dw_full[l], f"fwd_conv_{l}")
        qs, kn, cq, crow8 = _fwd_attn_prep(b3(pq), b3(pk), b3(pf), gq, gk, bf_pad[l][None], f"fwd_attn_prep_{l}")
        crow = crow8.reshape(bsz, NH // 2, 2, seq)
        o, ohp, yatt, lse = _fwd_attn(qs, kn, b3(vb), cq, crow, b3(pz), f"fwd_attn_{l}")
        ycs2, yatt2 = ycs.reshape(t, 512), yatt.reshape(t, DA)
        if l + 1 < DEPTH:
            xs.append(_fwd_outproj(xl, ycs2, yatt2, w_out_full[l], f"fwd_outproj_{l}"))
        else:
            dy, loss_part = _fwd_outproj_loss(xl, ycs2, yatt2, w_out_full[l], loss_target.reshape(t, D),
                                              f"fwd_outproj_loss_{l}")
        saved.append((pcf, psc, pq, pk, vb, pz, pf, ycs2, yatt2, o, ohp, qs, kn, cq, crow, lse))

    grads = [None] * DEPTH
    for l in reversed(range(DEPTH)):
        pcf, psc, pq, pk, vb, pz, pf, ycs2, yatt2, o, ohp, qs, kn, cq, crow, lse = saved[l]
        gq, gk = q_norm_g[l].reshape(1, DA), k_norm_g[l].reshape(1, DA)
        b3 = lambda a: a.reshape(bsz, seq, a.shape[-1])
        f2 = lambda a: a.reshape(t, a.shape[-1])
        dycs, do, dz, dl, d_wout = _bwd_outproj(dy, ycs2, yatt2, f2(o), f2(ohp), pz, w_out_full[l],
                                                f"bwd_outproj_{l}")
        dqs, dkn, dv, dcrow = _bwd_attn(qs, kn, b3(vb), b3(do), cq, lse, b3(dl), crow, f"bwd_attn_{l}")
        dpq, dpk, dpf, d_gq, d_gk, d_bf = _bwd_attn_post(b3(pq), b3(pk), dqs, dkn, b3(pf),
                                                        dcrow.reshape(bsz, NH, seq), gq, gk, bf_pad[l][None],
                                                        f"bwd_attn_post_{l}")
        dcf, dsc, d_cfdw, d_cfb, d_lng, d_lnb, d_pw, d_scdw = _bwd_conv(
            b3(pcf), b3(psc), b3(dycs), cf_dw_full[l], cf_dw_b[l][None], cf_ln_g[l][None], cf_ln_b[l][None],
            cf_pw_full[l], sc_dw_full[l], f"bwd_conv_{l}")
        dy, d_ng, d_win = _bwd_inproj(xs[l], norm_g[l][None], dy, w_in_full[l], f2(dcf), f2(dsc), f2(dpq), f2(dpk),
                                      f2(dv), dz, f2(dpf), f"bwd_inproj_{l}")
        grads[l] = dict(norm_g=d_ng[0], w_in=d_win, b_f=d_bf[0, :NH], cf_dw=d_cfdw, cf_dw_b=d_cfb[0],
                        cf_ln_g=d_lng[0], cf_ln_b=d_lnb[0], cf_pw=d_pw, sc_dw=d_scdw,
                        q_norm_g=d_gq.reshape(NH, HD), k_norm_g=d_gk.reshape(NH, HD), w_out=d_wout)
    grad_x = dy.reshape(bsz, seq, D)
    gw = {n: jnp.stack([grads[l][n] for l in range(DEPTH)]) for n in grads[0] if n != "w_in"}

    rep_names = ("norm_g", "b_f", "cf_dw_b", "cf_ln_g", "cf_ln_b", "q_norm_g", "k_norm_g")
    rep = jnp.concatenate([jnp.pad(gw[n].reshape(-1), (0, (-gw[n].size) % LANES)) for n in rep_names]
                          + [jnp.pad(loss_part.reshape(-1), (0, LANES - 1))]).reshape(-1, LANES)
    blocks = []
    for k in range(4):
        small = jnp.concatenate([gw["cf_dw"][:, :, 64 * k:64 * (k + 1)].reshape(-1),
                                 gw["sc_dw"][:, :, 64 * k:64 * (k + 1)].reshape(-1)]).reshape(-1, LANES)
        blocks.append(_pad_rows(jnp.concatenate([_rows(gw["cf_pw"][:, 64 * k:64 * (k + 1), :]), small, rep], axis=0), 16))
    g_small = jnp.stack(blocks)[None]
    g_win = jnp.stack([jnp.stack([grads[l]["w_in"][:, NSH * k:NSH * (k + 1)] for k in range(4)])
                       for l in range(DEPTH)])
    g_wout = gw["w_out"].reshape(DEPTH, 4, 256, D)
    cidx = lax.axis_index("c").astype(jnp.int32).reshape(1)
    packs = [g_win, g_wout, g_small]
    tiles = [256, 128, g_small.shape[2] // 2]
    gots = _rs_swap_halves(packs)
    pairs = [_rs_add_pair(p, got, cidx, tb, f"rs_add_pair_{i}") for i, (p, got, tb) in enumerate(zip(packs, gots, tiles))]
    gots2 = _rs_exchange_chips(pairs)
    totals = [_rs_add_chips(got, tb, f"rs_add_chips_{i}") for i, (got, tb) in enumerate(zip(gots2, tiles))]
    red_win, red_wout, red_small = _rs_join_halves(totals)

    n_small = (DEPTH * CFW * 64 + DEPTH * SCW * 64) // LANES
    red = red_small[0]
    pos = [0]

    def take(nrows):
        pos[0] += nrows
        return red[pos[0] - nrows:pos[0]]

    g = {}
    g["w_in"] = red_win
    g["w_out"] = red_wout
    g["cf_pw"] = take(n_pw).reshape(DEPTH, 64, DC)
    small = take(n_small).reshape(-1)
    g["cf_dw"] = small[:n_cfdw].reshape(DEPTH, CFW, 64)
    g["sc_dw"] = small[n_cfdw:].reshape(DEPTH, SCW, 64)
    shapes = dict(norm_g=norm_g.shape, b_f=b_f.shape, cf_dw_b=cf_dw_b.shape, cf_ln_g=cf_ln_g.shape,
                  cf_ln_b=cf_ln_b.shape, q_norm_g=q_norm_g.shape, k_norm_g=k_norm_g.shape)
    for n in rep_names:
        size = 1
        for s in shapes[n]:
            size *= s
        g[n] = take(-(-size // LANES)).reshape(-1)[:size].reshape(shapes[n])
    loss = take(1)[0, 0]

    order = ("norm_g", "w_in", "b_f", "cf_dw", "cf_dw_b", "cf_ln_g", "cf_ln_b", "cf_pw", "sc_dw", "q_norm_g",
             "k_norm_g", "w_out")
    weights = dict(norm_g=norm_g, w_in=w_in, b_f=b_f, cf_dw=cf_dw, cf_dw_b=cf_dw_b, cf_ln_g=cf_ln_g, cf_ln_b=cf_ln_b,
                   cf_pw=cf_pw, sc_dw=sc_dw, q_norm_g=q_norm_g, k_norm_g=k_norm_g, w_out=w_out)
    ms = dict(norm_g=m_norm_g, w_in=m_w_in, b_f=m_b_f, cf_dw=m_cf_dw, cf_dw_b=m_cf_dw_b, cf_ln_g=m_cf_ln_g,
              cf_ln_b=m_cf_ln_b, cf_pw=m_cf_pw, sc_dw=m_sc_dw, q_norm_g=m_q_norm_g, k_norm_g=m_k_norm_g, w_out=m_w_out)
    vs = dict(norm_g=v_norm_g, w_in=v_w_in, b_f=v_b_f, cf_dw=v_cf_dw, cf_dw_b=v_cf_dw_b, cf_ln_g=v_cf_ln_g,
              cf_ln_b=v_cf_ln_b, cf_pw=v_cf_pw, sc_dw=v_sc_dw, q_norm_g=v_q_norm_g, k_norm_g=v_k_norm_g, w_out=v_w_out)
    delta, new_m, new_v = {}, {}, {}
    two_d = lambda a: a.reshape(-1, a.shape[-1])
    for n in ("w_in", "w_out"):
        outs = _adamw(two_d(weights[n]), two_d(g[n]), two_d(ms[n]), two_d(vs[n]), 256, f"adamw_{n}")
        delta[n], new_m[n], new_v[n] = (a.reshape(weights[n].shape) for a in outs)
    small_names = [n for n in order if n not in ("w_in", "w_out")]
    flat = lambda d: _pad_rows(jnp.pad(jnp.concatenate([d[n].reshape(-1) for n in small_names]),
                                       (0, (-sum(weights[n].size for n in small_names)) % LANES)).reshape(-1, LANES), 8)
    fw = flat(weights)
    outs = _adamw(fw, flat(g), flat(ms), flat(vs), fw.shape[0], "adamw_small")
    off = 0
    for n in small_names:
        size = weights[n].size
        for dst, src in zip((delta, new_m, new_v), outs):
            dst[n] = src.reshape(-1)[off:off + size].reshape(weights[n].shape)
        off += size

    return (loss, grad_x, *[g[n] for n in order], *[delta[n] for n in order], *[new_m[n] for n in order],
            *[new_v[n] for n in order])
```

```python
import jax
import jax.numpy as jnp
from jax import lax
from jax.experimental import pallas as pl
from jax.experimental.pallas import tpu as pltpu

F32 = jnp.float32
BF = jnp.bfloat16
MESH = pl.DeviceIdType.MESH

DEPTH = 2
D = 1024
DC = 256
DA = 512
NH = 8
HD = 64
CFW = 31
SCW = 3
N_IN = 3848
NP = 3968
NSH = N_IN // 4
HALO = 32
EPS = 1e-6
LANES = 128
VMEM_LIMIT = 56 * 1024 * 1024

C_CF, C_SC, C_Q, C_K, C_V, C_Z, C_F = 0, 768, 1792, 2304, 2816, 3328, 3840

ADAM_LR = 0.001
ADAM_B1 = 0.9
ADAM_B2 = 0.999
ADAM_EPS = 1e-08
ADAM_WD = 0.01
ADAM_STEP = 10


def _cparams(sem=None):
    return pltpu.CompilerParams(dimension_semantics=sem, vmem_limit_bytes=VMEM_LIMIT)


def _dot(a, b):
    return jnp.dot(a, b, preferred_element_type=F32)


def _dot_nt(a, b):
    return lax.dot_general(a, b, (((1,), (1,)), ((), ())), preferred_element_type=F32)


def _dot_tn(a, b):
    return lax.dot_general(a, b, (((0,), (0,)), ((), ())), preferred_element_type=F32)


def _split3(x):
    hi = x.astype(BF)
    r1 = x - hi.astype(F32)
    mid = r1.astype(BF)
    lo = (r1 - mid.astype(F32)).astype(BF)
    return hi, mid, lo


def _dot_exact(a_bf, x):
    hi, mid, lo = _split3(x)
    return _dot(a_bf, hi) + _dot(a_bf, mid) + _dot(a_bf, lo)


def _sigmoid(x):
    return 1.0 / (1.0 + jnp.exp(-x))


def _seg_sum64(x):
    i = lax.broadcasted_iota(jnp.int32, (LANES, LANES), 0)
    j = lax.broadcasted_iota(jnp.int32, (LANES, LANES), 1)
    g = ((i >= HD) == (j >= HD)).astype(BF)
    hi, mid, lo = _split3(x)
    return _dot(hi, g) + _dot(mid, g) + _dot(lo, g)


def _fwd_inproj(x, g, w, name):
    t = x.shape[0]
    tm = 256

    def body(x_ref, g_ref, w_ref, cf_ref, sc_ref, q_ref, k_ref, v_ref, z_ref, f_ref):
        xt = x_ref[...]
        r = lax.rsqrt(jnp.mean(xt * xt, axis=-1, keepdims=True) + EPS)
        h = ((xt * r) * g_ref[...]).astype(BF)
        cf_ref[:, 0:512] = _dot(h, w_ref[:, 0:512])
        cf_ref[:, 512:768] = _dot(h, w_ref[:, 512:768])
        sc_ref[:, 0:512] = _dot(h, w_ref[:, C_SC:C_SC + 512])
        sc_ref[:, 512:1024] = _dot(h, w_ref[:, C_SC + 512:C_Q])
        q_ref[...] = _dot(h, w_ref[:, C_Q:C_K])
        k_ref[...] = _dot(h, w_ref[:, C_K:C_V])
        v_ref[...] = _dot(h, w_ref[:, C_V:C_Z]).astype(BF)
        z_ref[...] = _dot(h, w_ref[:, C_Z:C_F])
        f_ref[...] = _dot(h, w_ref[:, C_F:NP])

    row = lambda n: pl.BlockSpec((tm, n), lambda i: (i, 0))
    return pl.pallas_call(
        body, grid=(t // tm,),
        in_specs=[row(D), pl.BlockSpec((1, D), lambda i: (0, 0)), pl.BlockSpec((D, NP), lambda i: (0, 0))],
        out_specs=[row(768), row(1024), row(DA), row(DA), row(DA), row(DA), row(LANES)],
        out_shape=[jax.ShapeDtypeStruct((t, 768), F32), jax.ShapeDtypeStruct((t, 1024), F32),
                   jax.ShapeDtypeStruct((t, DA), F32), jax.ShapeDtypeStruct((t, DA), F32),
                   jax.ShapeDtypeStruct((t, DA), BF), jax.ShapeDtypeStruct((t, DA), F32),
                   jax.ShapeDtypeStruct((t, LANES), F32)],
        compiler_params=_cparams(("arbitrary",)), name=name)(x, g, w)


def _conformer_fwd(a, g, ha, hg, first, ubuf, dw_ref, bias, lng, lnb):
    ts = a.shape[0]
    u0 = a * _sigmoid(g)
    ubuf[0:HALO, :] = jnp.where(first, 0.0, ha * _sigmoid(hg))
    ubuf[HALO:HALO + ts, :] = u0
    u1 = jnp.zeros((ts, DC), F32) + bias
    for k in range(CFW):
        u1 = u1 + dw_ref[k:k + 1, :] * ubuf[pl.ds(HALO - (CFW - 1) + k, ts), :]
    mu = jnp.mean(u1, axis=-1, keepdims=True)
    xc = u1 - mu
    rstd = lax.rsqrt(jnp.mean(xc * xc, axis=-1, keepdims=True) + EPS)
    n = xc * rstd
    u2 = n * lng + lnb
    s2 = _sigmoid(u2)
    u3 = u2 * s2
    return u0, n, rstd, u2, s2, u3


def _shortconv_fwd(c, xs, hc, hx, first, mbuf, dw_ref):
    ts = c.shape[0]
    mbuf[0:HALO, :] = jnp.where(first, 0.0, hc * hx)
    mbuf[HALO:HALO + ts, :] = c * xs
    cv = jnp.zeros((ts, DC), F32)
    for k in range(SCW):
        cv = cv + dw_ref[k:k + 1, :] * mbuf[pl.ds(HALO - (SCW - 1) + k, ts), :]
    return cv


def _conv_specs(bsz, seq, ts, order):
    nt = seq // ts
    tile = (lambda i: i) if order > 0 else (lambda i: nt - 1 - i)
    hrow = lambda i: jnp.maximum(tile(i) * (ts // HALO) - 1, 0)
    cur = lambda n: pl.BlockSpec((None, ts, n), lambda b, i: (b, tile(i), 0))
    halo = lambda j: pl.BlockSpec((None, HALO, DC), lambda b, i: (b, hrow(i), j))
    full = lambda r, c: pl.BlockSpec((r, c), lambda b, i: (0, 0))
    return nt, tile, cur, halo, full


def _fwd_conv(pcf, psc, cf_dw, cf_dw_b, ln_g, ln_b, cf_pw, sc_dw, name):
    bsz, seq, _ = pcf.shape
    ts = 512
    nt, tile, cur, halo, full = _conv_specs(bsz, seq, ts, +1)

    def body(cf_ref, ha_ref, hg_ref, sc_ref, hc_ref, hx_ref, dw_ref, b_ref, lg_ref, lb_ref, pw_ref, sdw_ref,
             y_ref, ubuf, mbuf):
        first = pl.program_id(1) == 0
        _, _, _, _, _, u3 = _conformer_fwd(cf_ref[:, 0:256], cf_ref[:, 256:512], ha_ref[...], hg_ref[...], first,
                                           ubuf, dw_ref, b_ref[...], lg_ref[...], lb_ref[...])
        z = cf_ref[:, 512:768]
        y_ref[:, 0:256] = _dot(u3.astype(BF), pw_ref[...]) * (z * _sigmoid(z))
        cv = _shortconv_fwd(sc_ref[:, 256:512], sc_ref[:, 512:768], hc_ref[...], hx_ref[...], first, mbuf, sdw_ref)
        zs = sc_ref[:, 768:1024]
        y_ref[:, 256:512] = sc_ref[:, 0:256] * cv * (zs * _sigmoid(zs))

    return pl.pallas_call(
        body, grid=(bsz, nt),
        in_specs=[cur(768), halo(0), halo(1), cur(1024), halo(1), halo(2),
                  full(CFW, DC), full(1, DC), full(1, DC), full(1, DC), full(DC, DC), full(SCW, DC)],
        out_specs=pl.BlockSpec((None, ts, 512), lambda b, i: (b, i, 0)),
        out_shape=jax.ShapeDtypeStruct((bsz, seq, 512), F32),
        scratch_shapes=[pltpu.VMEM((HALO + ts, DC), F32), pltpu.VMEM((HALO + ts, DC), F32)],
        compiler_params=_cparams(("arbitrary", "arbitrary")), name=name,
    )(pcf, pcf, pcf, psc, psc, psc, cf_dw, cf_dw_b, ln_g, ln_b, cf_pw, sc_dw)


def _head_rms(xb, gb):
    ms = _seg_sum64(xb * xb) * (1.0 / HD)
    r = lax.rsqrt(ms + EPS)
    xhat = xb * r
    return xhat, r, xhat * gb


def _fwd_attn_prep(pq, pk, pf, gq, gk, bf, name):
    bsz, seq, _ = pq.shape
    ts = 512
    nt = seq // ts

    def body(q_ref, k_ref, f_ref, gq_ref, gk_ref, bf_ref, qs_ref, kn_ref, cq_ref, crow_ref, carry):
        @pl.when(pl.program_id(1) == 0)
        def _():
            carry[...] = jnp.zeros_like(carry)

        for jb in range(DA // LANES):
            sl = slice(jb * LANES, (jb + 1) * LANES)
            _, _, qn = _head_rms(q_ref[:, sl], gq_ref[:, sl])
            qs_ref[:, sl] = (qn * (1.0 / 8.0)).astype(BF)
            _, _, kn = _head_rms(k_ref[:, sl], gk_ref[:, sl])
            kn_ref[:, sl] = kn.astype(BF)

        xf = f_ref[...] + bf_ref[...]
        lf = jnp.minimum(xf, 0.0) - jnp.log(1.0 + jnp.exp(-jnp.abs(xf)))
        ti = lax.broadcasted_iota(jnp.int32, (ts, ts), 0)
        si = lax.broadcasted_iota(jnp.int32, (ts, ts), 1)
        c = _dot_exact((si <= ti).astype(BF), lf) + carry[...]
        carry[...] = c[ts - 1:ts, :]
        hj = lax.broadcasted_iota(jnp.int32, (LANES, DA), 0)
        ll = lax.broadcasted_iota(jnp.int32, (LANES, DA), 1)
        dd = ll - hj * HD
        chi, cmid, clo = _split3(c)
        e = ((dd >= 0) & (dd < HD)).astype(BF)
        cq_ref[...] = _dot(chi, e) + _dot(cmid, e) + _dot(clo, e)
        hh = lax.broadcasted_iota(jnp.int32, (16, LANES), 0)
        jj = lax.broadcasted_iota(jnp.int32, (16, LANES), 1)
        sel = (hh == jj).astype(BF)
        cr = _dot_nt(sel, chi) + _dot_nt(sel, cmid) + _dot_nt(sel, clo)
        crow_ref[...] = cr[0:NH, :]

    tile = lambda n: pl.BlockSpec((None, ts, n), lambda b, i: (b, i, 0))
    vec = lambda n: pl.BlockSpec((1, n), lambda b, i: (0, 0))
    return pl.pallas_call(
        body, grid=(bsz, nt),
        in_specs=[tile(DA), tile(DA), tile(LANES), vec(DA), vec(DA), vec(LANES)],
        out_specs=[tile(DA), tile(DA), tile(DA), pl.BlockSpec((None, NH, ts), lambda b, i: (b, 0, i))],
        out_shape=[jax.ShapeDtypeStruct((bsz, seq, DA), BF), jax.ShapeDtypeStruct((bsz, seq, DA), BF),
                   jax.ShapeDtypeStruct((bsz, seq, DA), F32), jax.ShapeDtypeStruct((bsz, NH, seq), F32)],
        scratch_shapes=[pltpu.VMEM((1, LANES), F32)],
        compiler_params=_cparams(("arbitrary", "arbitrary")), name=name)(pq, pk, pf, gq, gk, bf)


def _causal_mask(tq, tk):
    r = lax.broadcasted_iota(jnp.int32, (tq, tk), 0)
    c = lax.broadcasted_iota(jnp.int32, (tq, tk), 1)
    return r - c


def _fwd_attn(qs, kn, vb, cq, crow, pz, name, ex=None):
    bsz, seq, _ = qs.shape
    tq = 256
    nq = seq // tq
    npair = NH // 2

    def body(q_ref, k_ref, v_ref, cq_ref, cr_ref, z_ref, o_ref, ohp_ref, y_ref, lse_ref):
        qi = pl.program_id(2)
        rc = _causal_mask(tq, tq)
        head0 = lax.broadcasted_iota(jnp.int32, (1, LANES), 1) < HD
        q = q_ref[...]
        zq = jnp.zeros_like(q)
        qm = (jnp.where(head0, q, zq), jnp.where(head0, zq, q))
        ct = (cq_ref[:, 0:1], cq_ref[:, HD:HD + 1])

        def step(j, carry, masked):
            m, l, acc, accl = carry
            rows = pl.ds(pl.multiple_of(j * tq, tq), tq)
            kb = k_ref[rows, :]
            vb_t = v_ref[rows, :]
            m_new, l_new, alphas, pv, pvl = [], [], [], [], []
            for hh in range(2):
                s = _dot_nt(qm[hh], kb) + (ct[hh] - cr_ref[hh:hh + 1, rows])
                if masked:
                    s = jnp.where(rc >= 0, s, -1e30)
                m2 = jnp.maximum(m[hh], jnp.max(s, axis=-1, keepdims=True))
                p = jnp.exp(s - m2)
                alpha = jnp.exp(m[hh] - m2)
                l_new.append(alpha * l[hh] + jnp.sum(p, axis=-1, keepdims=True))
                m_new.append(m2)
                alphas.append(alpha)
                pb = p.astype(BF)
                plo = (p - pb.astype(F32)).astype(BF)
                pv.append(_dot(pb, vb_t))
                pvl.append(_dot(plo, vb_t))
            al = jnp.where(head0, alphas[0], alphas[1])
            acc2 = al * acc + jnp.where(head0, pv[0], pv[1])
            accl2 = al * accl + jnp.where(head0, pvl[0], pvl[1])
            return tuple(m_new), tuple(l_new), acc2, accl2

        neg = jnp.full((tq, 1), -1e30, F32)
        zcol = jnp.zeros((tq, 1), F32)
        zacc = jnp.zeros((tq, LANES), F32)
        carry = lax.fori_loop(0, qi, lambda j, cy: step(j, cy, False), ((neg, neg), (zcol, zcol), zacc, zacc))
        m, l, acc, accl = step(qi, carry, True)
        inv = jnp.where(head0, 1.0 / l[0], 1.0 / l[1])
        o = acc * inv
        z = z_ref[...]
        o_ref[...] = o
        ohp_ref[...] = (acc + accl) * inv
        y_ref[...] = o * (z * _sigmoid(z))
        lse_ref[...] = jnp.where(head0, m[0] + jnp.log(l[0]), m[1] + jnp.log(l[1]))

    qblk = pl.BlockSpec((None, tq, LANES), lambda b, h, i: (b, i, h))
    kvblk = pl.BlockSpec((None, seq, LANES), lambda b, h, i: (b, 0, h))
    return _host_call(
        body, (bsz, npair, nq),
        [qblk, kvblk, kvblk, qblk, pl.BlockSpec((None, None, 2, seq), lambda b, h, i: (b, h, 0, 0)), qblk],
        [qblk, qblk, qblk, qblk], [jax.ShapeDtypeStruct((bsz, seq, DA), F32)] * 4, [],
        (qs, kn, vb, cq, crow, pz), name, ex)


def _fwd_outproj(x, ycs, yatt, wo, name):
    t = x.shape[0]
    tm = 512

    def body(x_ref, a_ref, b_ref, w_ref, o_ref):
        o_ref[...] = (x_ref[...] + _dot(a_ref[...].astype(BF), w_ref[0:512, :])
                      + _dot(b_ref[...].astype(BF), w_ref[512:1024, :]))

    row = lambda n: pl.BlockSpec((tm, n), lambda i: (i, 0))
    return pl.pallas_call(
        body, grid=(t // tm,),
        in_specs=[row(D), row(512), row(512), pl.BlockSpec((D, D), lambda i: (0, 0))],
        out_specs=row(D), out_shape=jax.ShapeDtypeStruct((t, D), F32),
        compiler_params=_cparams(("arbitrary",)), name=name)(x, ycs, yatt, wo)


def _fwd_outproj_loss(x, ycs, yatt, wo, target, name):
    t = x.shape[0]
    tm = 512

    def body(x_ref, a_ref, b_ref, w_ref, t_ref, dy_ref, loss_ref):
        @pl.when(pl.program_id(0) == 0)
        def _():
            loss_ref[...] = jnp.zeros_like(loss_ref)

        y = (x_ref[...] + _dot(a_ref[...].astype(BF), w_ref[0:512, :])
             + _dot(b_ref[...].astype(BF), w_ref[512:1024, :]))
        err = y - t_ref[...]
        dy_ref[...] = err * (1.0 / D)
        per_tok = jnp.mean(err * err, axis=-1, keepdims=True)
        loss_ref[...] += 0.5 * jnp.sum(per_tok, axis=0, keepdims=True)

    row = lambda n: pl.BlockSpec((tm, n), lambda i: (i, 0))
    return pl.pallas_call(
        body, grid=(t // tm,),
        in_specs=[row(D), row(512), row(512), pl.BlockSpec((D, D), lambda i: (0, 0)), row(D)],
        out_specs=[row(D), pl.BlockSpec((1, 1), lambda i: (0, 0))],
        out_shape=[jax.ShapeDtypeStruct((t, D), F32), jax.ShapeDtypeStruct((1, 1), F32)],
        compiler_params=_cparams(("arbitrary",)), name=name)(x, ycs, yatt, wo, target)


def _dsilu(x, s):
    return s * (1.0 + x * (1.0 - s))


def _bwd_outproj(dy, ycs, yatt, o, ohp, pz, wo, name, ex=None):
    t = dy.shape[0]
    tm = 256

    def body(dy_ref, a_ref, b_ref, o_ref, ohp_ref, z_ref, w_ref, dcs_ref, do_ref, dz_ref, dl_ref, dw_ref):
        @pl.when(pl.program_id(0) == 0)
        def _():
            dw_ref[...] = jnp.zeros_like(dw_ref)

        dyb = dy_ref[...].astype(BF)
        dw_ref[0:512, :] += _dot_tn(a_ref[...].astype(BF), dyb)
        dw_ref[512:1024, :] += _dot_tn(b_ref[...].astype(BF), dyb)
        dcs_ref[...] = _dot_nt(dyb, w_ref[0:512, :])
        dyatt = _dot_nt(dyb, w_ref[512:1024, :])
        z = z_ref[...]
        sz = _sigmoid(z)
        o_t = o_ref[...]
        dob = (dyatt * (z * sz)).astype(BF)
        do_ref[...] = dob
        dz_ref[...] = dyatt * o_t * _dsilu(z, sz)
        prod = dob.astype(F32) * ohp_ref[...]
        for jb in range(DA // LANES):
            sl = slice(jb * LANES, (jb + 1) * LANES)
            dl_ref[:, sl] = _seg_sum64(prod[:, sl])

    row = lambda n: pl.BlockSpec((tm, n), lambda i: (i, 0))
    return _host_call(
        body, (t // tm,),
        [row(D), row(512), row(512), row(DA), row(DA), row(DA), pl.BlockSpec((D, D), lambda i: (0, 0))],
        [row(512), row(DA), row(DA), row(DA), pl.BlockSpec((D, D), lambda i: (0, 0))],
        [jax.ShapeDtypeStruct((t, 512), F32), jax.ShapeDtypeStruct((t, DA), BF),
         jax.ShapeDtypeStruct((t, DA), F32), jax.ShapeDtypeStruct((t, DA), F32),
         jax.ShapeDtypeStruct((D, D), F32)], [],
        (dy, ycs, yatt, o, ohp, pz, wo), name, ex)


def _bwd_attn(qs, kn, vb, do, cq, lse, dl, crow, name, ex=None):
    bsz, seq, _ = qs.shape
    tq = 256
    nq = seq // tq
    npair = NH // 2

    def body(q_ref, k_ref, v_ref, do_ref, cq_ref, lse_ref, dl_ref, cr_ref, dq_ref, dk_ref, dv_ref, dc_ref, qm, dom):
        rc = _causal_mask(tq, tq)
        head0 = lax.broadcasted_iota(jnp.int32, (1, LANES), 1) < HD
        zb = jnp.zeros((seq, LANES), BF)
        qm[0] = jnp.where(head0, q_ref[...], zb)
        qm[1] = jnp.where(head0, zb, q_ref[...])
        dom[0] = jnp.where(head0, do_ref[...], zb)
        dom[1] = jnp.where(head0, zb, do_ref[...])
        dq_ref[...] = jnp.zeros_like(dq_ref)

        def kloop(kj, _):
            krows = pl.ds(pl.multiple_of(kj * tq, tq), tq)
            kb = k_ref[krows, :]
            vb_t = v_ref[krows, :]
            zk = jnp.zeros_like(kb)
            km = (jnp.where(head0, kb, zk), jnp.where(head0, zk, kb))

            def qstep(qi, carry, masked):
                dk, dv, cs = carry
                qrows = pl.ds(pl.multiple_of(qi * tq, tq), tq)
                dq = jnp.zeros((tq, LANES), F32)
                cs_new = []
                for hh in range(2):
                    c0 = hh * HD
                    qh = qm[hh, qrows, :]
                    doh = dom[hh, qrows, :]
                    s = _dot_nt(qh, kb) + (cq_ref[qrows, c0:c0 + 1] - cr_ref[hh:hh + 1, krows])
                    if masked:
                        s = jnp.where(rc >= 0, s, -1e30)
                    p = jnp.exp(s - lse_ref[qrows, c0:c0 + 1])
                    ds = p * (_dot_nt(doh, vb_t) - dl_ref[qrows, c0:c0 + 1])
                    dsb = ds.astype(BF)
                    dv = dv + _dot_tn(p.astype(BF), doh)
                    dk = dk + _dot_tn(dsb, qh)
                    dq = dq + _dot(dsb, km[hh])
                    cs_new.append(cs[hh] + jnp.sum(ds, axis=0, keepdims=True))
                dq_ref[qrows, :] += dq
                return dk, dv, tuple(cs_new)

            zero = jnp.zeros((tq, LANES), F32)
            zrow = jnp.zeros((1, tq), F32)
            carry = qstep(kj, (zero, zero, (zrow, zrow)), True)
            dk, dv, cs = lax.fori_loop(kj + 1, nq, lambda qi, cy: qstep(qi, cy, False), carry)
            dk_ref[krows, :] = dk
            dv_ref[krows, :] = dv
            dc_ref[0:1, krows] = -cs[0]
            dc_ref[1:2, krows] = -cs[1]
            return 0

        lax.fori_loop(0, nq, kloop, 0)

    blk = pl.BlockSpec((None, seq, LANES), lambda b, h: (b, 0, h))
    rowblk = pl.BlockSpec((None, None, 2, seq), lambda b, h: (b, h, 0, 0))
    return _host_call(
        body, (bsz, npair), [blk, blk, blk, blk, blk, blk, blk, rowblk], [blk, blk, blk, rowblk],
        [jax.ShapeDtypeStruct((bsz, seq, DA), F32)] * 3 + [jax.ShapeDtypeStruct((bsz, npair, 2, seq), F32)],
        [pltpu.VMEM((2, seq, LANES), BF), pltpu.VMEM((2, seq, LANES), BF)],
        (qs, kn, vb, do, cq, lse, dl, crow), name, ex)


def _bwd_attn_post(pq, pk, dqs, dkn, pf, dcrow, gq, gk, bf, name, ex=None):
    bsz, seq, _ = pq.shape
    ts = 512
    nt = seq // ts

    def body(q_ref, k_ref, dq_ref, dk_ref, f_ref, dc_ref, gq_ref, gk_ref, bf_ref,
             dpq_ref, dpk_ref, dpf_ref, dgq_ref, dgk_ref, dbf_ref, carry):
        @pl.when((pl.program_id(0) == 0) & (pl.program_id(1) == 0))
        def _():
            dgq_ref[...] = jnp.zeros_like(dgq_ref)
            dgk_ref[...] = jnp.zeros_like(dgk_ref)
            dbf_ref[...] = jnp.zeros_like(dbf_ref)

        @pl.when(pl.program_id(1) == 0)
        def _():
            carry[...] = jnp.zeros_like(carry)

        for x_ref, dx_ref, g_ref, dp_ref, dg_ref, scale in ((q_ref, dq_ref, gq_ref, dpq_ref, dgq_ref, 1.0 / 8.0),
                                                            (k_ref, dk_ref, gk_ref, dpk_ref, dgk_ref, 1.0)):
            for jb in range(DA // LANES):
                sl = slice(jb * LANES, (jb + 1) * LANES)
                gb = g_ref[:, sl]
                xhat, r, _ = _head_rms(x_ref[:, sl], gb)
                dn = dx_ref[:, sl] * scale
                dg_ref[:, sl] += jnp.sum(dn * xhat, axis=0, keepdims=True)
                dxh = dn * gb
                dp_ref[:, sl] = r * (dxh - xhat * (_seg_sum64(dxh * xhat) * (1.0 / HD)))

        ui = lax.broadcasted_iota(jnp.int32, (ts, ts), 0)
        tj = lax.broadcasted_iota(jnp.int32, (ts, ts), 1)
        tri = (tj >= ui).astype(BF)
        dc = jnp.concatenate([dc_ref[...], jnp.zeros((LANES - NH, ts), F32)], axis=0)
        hi, mid, lo = _split3(dc)
        dlf = _dot_nt(tri, hi) + _dot_nt(tri, mid) + _dot_nt(tri, lo) + carry[...]
        carry[...] = dlf[0:1, :]
        xf = f_ref[...] + bf_ref[...]
        lane = lax.broadcasted_iota(jnp.int32, (ts, LANES), 1)
        dfl = jnp.where(lane < NH, dlf * _sigmoid(-xf), 0.0)
        dpf_ref[...] = dfl
        dbf_ref[...] += jnp.sum(dfl, axis=0, keepdims=True)

    rev = lambda i: nt - 1 - i
    tile = lambda n: pl.BlockSpec((None, ts, n), lambda b, i: (b, rev(i), 0))
    vec = lambda n: pl.BlockSpec((1, n), lambda b, i: (0, 0))
    return _host_call(
        body, (bsz, nt),
        [tile(DA), tile(DA), tile(DA), tile(DA), tile(LANES),
         pl.BlockSpec((None, NH, ts), lambda b, i: (b, 0, rev(i))), vec(DA), vec(DA), vec(LANES)],
        [tile(DA), tile(DA), tile(LANES), vec(DA), vec(DA), vec(LANES)],
        [jax.ShapeDtypeStruct((bsz, seq, DA), F32), jax.ShapeDtypeStruct((bsz, seq, DA), F32),
         jax.ShapeDtypeStruct((bsz, seq, LANES), F32), jax.ShapeDtypeStruct((1, DA), F32),
         jax.ShapeDtypeStruct((1, DA), F32), jax.ShapeDtypeStruct((1, LANES), F32)],
        [pltpu.VMEM((1, LANES), F32)], (pq, pk, dqs, dkn, pf, dcrow, gq, gk, bf), name, ex)


def _bwd_conv(pcf, psc, dycs, cf_dw, cf_dw_b, ln_g, ln_b, cf_pw, sc_dw, name):
    bsz, seq, _ = pcf.shape
    ts = 512
    nt, tile, cur, halo, full = _conv_specs(bsz, seq, ts, -1)

    def body(cf_ref, ha_ref, hg_ref, sc_ref, hc_ref, hx_ref, dy_ref, dw_ref, b_ref, lg_ref, lb_ref, pw_ref, sdw_ref,
             dcf_ref, dsc_ref, ddw_ref, db_ref, dlg_ref, dlb_ref, dpw_ref, dsdw_ref,
             ubuf, mbuf, dubuf, dcbuf):
        step = pl.program_id(1)
        first = tile(step) == 0

        @pl.when((pl.program_id(0) == 0) & (step == 0))
        def _():
            for r in (ddw_ref, db_ref, dlg_ref, dlb_ref, dpw_ref, dsdw_ref):
                r[...] = jnp.zeros_like(r)

        @pl.when(step == 0)
        def _():
            dubuf[ts:ts + HALO, :] = jnp.zeros((HALO, DC), F32)
            dcbuf[ts:ts + HALO, :] = jnp.zeros((HALO, DC), F32)

        a = cf_ref[:, 0:256]
        g = cf_ref[:, 256:512]
        z = cf_ref[:, 512:768]
        lng = lg_ref[...]
        u0, n, rstd, u2, s2, u3 = _conformer_fwd(a, g, ha_ref[...], hg_ref[...], first, ubuf, dw_ref,
                                                 b_ref[...], lng, lb_ref[...])
        u3b = u3.astype(BF)
        p = _dot(u3b, pw_ref[...])
        sz = _sigmoid(z)
        dy = dy_ref[:, 0:256]
        dcf_ref[:, 512:768] = dy * p * _dsilu(z, sz)
        dpb = (dy * (z * sz)).astype(BF)
        dpw_ref[...] += _dot_tn(u3b, dpb)
        du2 = _dot_nt(dpb, pw_ref[...]) * _dsilu(u2, s2)
        dlg_ref[...] += jnp.sum(du2 * n, axis=0, keepdims=True)
        dlb_ref[...] += jnp.sum(du2, axis=0, keepdims=True)
        dn = du2 * lng
        du1 = rstd * (dn - jnp.mean(dn, axis=-1, keepdims=True) - n * jnp.mean(dn * n, axis=-1, keepdims=True))
        db_ref[...] += jnp.sum(du1, axis=0, keepdims=True)
        dubuf[0:ts, :] = du1
        du0 = jnp.zeros((ts, DC), F32)
        for k in range(CFW):
            du0 = du0 + dw_ref[k:k + 1, :] * dubuf[pl.ds(CFW - 1 - k, ts), :]
            ddw_ref[k:k + 1, :] += jnp.sum(du1 * ubuf[pl.ds(HALO - (CFW - 1) + k, ts), :], axis=0, keepdims=True)
        dubuf[ts:ts + HALO, :] = du1[0:HALO, :]
        sg = _sigmoid(g)
        dcf_ref[:, 0:256] = du0 * sg
        dcf_ref[:, 256:512] = du0 * a * sg * (1.0 - sg)

        bb = sc_ref[:, 0:256]
        c = sc_ref[:, 256:512]
        xs = sc_ref[:, 512:768]
        zs = sc_ref[:, 768:1024]
        cv = _shortconv_fwd(c, xs, hc_ref[...], hx_ref[...], first, mbuf, sdw_ref)
        szs = _sigmoid(zs)
        dys = dy_ref[:, 256:512]
        gate = zs * szs
        dsc_ref[:, 0:256] = dys * cv * gate
        dsc_ref[:, 768:1024] = dys * bb * cv * _dsilu(zs, szs)
        dcv = dys * bb * gate
        dcbuf[0:ts, :] = dcv
        dm = jnp.zeros((ts, DC), F32)
        for k in range(SCW):
            dm = dm + sdw_ref[k:k + 1, :] * dcbuf[pl.ds(SCW - 1 - k, ts), :]
            dsdw_ref[k:k + 1, :] += jnp.sum(dcv * mbuf[pl.ds(HALO - (SCW - 1) + k, ts), :], axis=0, keepdims=True)
        dcbuf[ts:ts + HALO, :] = dcv[0:HALO, :]
        dsc_ref[:, 256:512] = dm * xs
        dsc_ref[:, 512:768] = dm * c

    outt = lambda n: pl.BlockSpec((None, ts, n), lambda b, i: (b, tile(i), 0))
    return pl.pallas_call(
        body, grid=(bsz, nt),
        in_specs=[cur(768), halo(0), halo(1), cur(1024), halo(1), halo(2), cur(512),
                  full(CFW, DC), full(1, DC), full(1, DC), full(1, DC), full(DC, DC), full(SCW, DC)],
        out_specs=[outt(768), outt(1024), full(CFW, DC), full(1, DC), full(1, DC), full(1, DC), full(DC, DC),
                   full(SCW, DC)],
        out_shape=[jax.ShapeDtypeStruct((bsz, seq, 768), F32), jax.ShapeDtypeStruct((bsz, seq, 1024), F32),
                   jax.ShapeDtypeStruct((CFW, DC), F32), jax.ShapeDtypeStruct((1, DC), F32),
                   jax.ShapeDtypeStruct((1, DC), F32), jax.ShapeDtypeStruct((1, DC), F32),
                   jax.ShapeDtypeStruct((DC, DC), F32), jax.ShapeDtypeStruct((SCW, DC), F32)],
        scratch_shapes=[pltpu.VMEM((HALO + ts, DC), F32), pltpu.VMEM((HALO + ts, DC), F32),
                        pltpu.VMEM((ts + HALO, DC), F32), pltpu.VMEM((ts + HALO, DC), F32)],
        compiler_params=_cparams(("arbitrary", "arbitrary")), name=name,
    )(pcf, pcf, pcf, psc, psc, psc, dycs, cf_dw, cf_dw_b, ln_g, ln_b, cf_pw, sc_dw)


def _bwd_inproj(x, g, dyres, w, dcf, dsc, dq, dk, dv, dz, df, name):
    t = x.shape[0]
    tm = 256
    pieces = ((C_CF, 768), (C_SC, 1024), (C_Q, DA), (C_K, DA), (C_V, DA), (C_Z, DA), (C_F, LANES))

    def body(x_ref, g_ref, dy_ref, w_hbm, dcf_ref, dsc_ref, dq_ref, dk_ref, dv_ref, dz_ref, df_ref,
             dx_ref, dg_ref, dw_hbm, w_vmem, dw_acc, sem):
        i = pl.program_id(0)

        @pl.when(i == 0)
        def _():
            cp = pltpu.make_async_copy(w_hbm, w_vmem, sem)
            cp.start()
            dw_acc[...] = jnp.zeros_like(dw_acc)
            dg_ref[...] = jnp.zeros_like(dg_ref)
            cp.wait()

        xt = x_ref[...]
        gg = g_ref[...]
        r = lax.rsqrt(jnp.mean(xt * xt, axis=-1, keepdims=True) + EPS)
        xhat = xt * r
        ht = (xhat * gg).astype(BF).T
        dh = jnp.zeros((tm, D), F32)
        for (c0, n), ref in zip(pieces, (dcf_ref, dsc_ref, dq_ref, dk_ref, dv_ref, dz_ref, df_ref)):
            for s0 in range(0, n, 512):
                s1 = min(s0 + 512, n)
                d = ref[:, s0:s1].astype(BF)
                dh = dh + _dot_nt(d, w_vmem[:, c0 + s0:c0 + s1])
                dw_acc[:, c0 + s0:c0 + s1] += _dot(ht, d)
        dg_ref[...] += jnp.sum(dh * xhat, axis=0, keepdims=True)
        dhg = dh * gg
        dx_ref[...] = dy_ref[...] + r * (dhg - xhat * jnp.mean(dhg * xhat, axis=-1, keepdims=True))

        @pl.when(i == pl.num_programs(0) - 1)
        def _():
            out = pltpu.make_async_copy(dw_acc, dw_hbm, sem)
            out.start()
            out.wait()

    row = lambda n: pl.BlockSpec((tm, n), lambda i: (i, 0))
    anyspec = pl.BlockSpec(memory_space=pl.ANY)
    return pl.pallas_call(
        body, grid=(t // tm,),
        in_specs=[row(D), pl.BlockSpec((1, D), lambda i: (0, 0)), row(D), anyspec,
                  row(768), row(1024), row(DA), row(DA), row(DA), row(DA), row(LANES)],
        out_specs=[row(D), pl.BlockSpec((1, D), lambda i: (0, 0)), anyspec],
        out_shape=[jax.ShapeDtypeStruct((t, D), F32), jax.ShapeDtypeStruct((1, D), F32),
                   jax.ShapeDtypeStruct((D, NP), F32)],
        scratch_shapes=[pltpu.VMEM((D, NP), BF), pltpu.VMEM((D, NP), F32), pltpu.SemaphoreType.DMA],
        compiler_params=_cparams(("arbitrary",)), name=name)(x, g, dyres, w, dcf, dsc, dq, dk, dv, dz, df)


def _adamw(w, g, m, v, tr, name):
    rows, cols = w.shape

    def body(w_ref, g_ref, m_ref, v_ref, d_ref, m2_ref, v2_ref):
        gt = g_ref[...]
        m2 = ADAM_B1 * m_ref[...] + (1.0 - ADAM_B1) * gt
        v2 = ADAM_B2 * v_ref[...] + (1.0 - ADAM_B2) * (gt * gt)
        m_hat = m2 / (1.0 - ADAM_B1 ** ADAM_STEP)
        v_hat = v2 / (1.0 - ADAM_B2 ** ADAM_STEP)
        d_ref[...] = -ADAM_LR * (m_hat / (jnp.sqrt(v_hat) + ADAM_EPS) + ADAM_WD * w_ref[...])
        m2_ref[...] = m2
        v2_ref[...] = v2

    blk = pl.BlockSpec((tr, cols), lambda i: (i, 0))
    return pl.pallas_call(
        body, grid=(rows // tr,), in_specs=[blk] * 4, out_specs=[blk] * 3,
        out_shape=[jax.ShapeDtypeStruct((rows, cols), F32)] * 3,
        compiler_params=_cparams(("arbitrary",)), name=name)(w, g, m, v)


def _place():
    x, y, c = lax.axis_index("x"), lax.axis_index("y"), lax.axis_index("c")
    chips = [(1 - x, y), (x, 1 - y), (1 - x, 1 - y)]
    return x, y, c, chips


def _halves(c, rows, align):
    rh = rows // 2
    return pl.ds(pl.multiple_of(c * rh, align), rh), pl.ds(pl.multiple_of((1 - c) * rh, align), rh)


ANY = pl.BlockSpec(memory_space=pl.ANY)


class _Exchange:
    def __init__(self, operands, out_shape, sems, start, wait):
        self.operands, self.out_shape, self.sems, self.start, self.wait = operands, out_shape, sems, start, wait


def _run_exchange(ex, name):
    n_in, n_out = len(ex.operands), len(ex.out_shape)

    def body(*refs):
        ins, outs, sems = refs[:n_in], refs[n_in:n_in + n_out], refs[n_in + n_out:]
        ex.start(ins, outs, sems)
        ex.wait(ins, outs, sems)

    return pl.pallas_call(body, in_specs=[ANY] * n_in, out_specs=[ANY] * n_out, out_shape=ex.out_shape,
                          scratch_shapes=ex.sems, name=name)(*ex.operands)


def _host_call(body, grid, in_specs, out_specs, out_shape, scratch, operands, name, ex=None):
    sem = ("arbitrary",) * len(grid)
    if ex is None:
        outs = pl.pallas_call(body, grid=grid, in_specs=in_specs, out_specs=out_specs, out_shape=out_shape,
                              scratch_shapes=scratch, compiler_params=_cparams(sem), name=name)(*operands)
        return outs, None
    n_in, n_out, n_scr = len(in_specs), len(out_specs), len(scratch)
    xi, xo = len(ex.operands), len(ex.out_shape)

    def hosted(*refs):
        ins, xins = refs[:n_in], refs[n_in:n_in + xi]
        o0 = n_in + xi
        outs, xouts = refs[o0:o0 + n_out], refs[o0 + n_out:o0 + n_out + xo]
        s0 = o0 + n_out + xo
        scr, xsems = refs[s0:s0 + n_scr], refs[s0 + n_scr:]
        first = pl.program_id(0) == 0
        last = pl.program_id(0) == pl.num_programs(0) - 1
        for d in range(1, len(grid)):
            first = first & (pl.program_id(d) == 0)
            last = last & (pl.program_id(d) == pl.num_programs(d) - 1)

        @pl.when(first)
        def _():
            ex.start(xins, xouts, xsems)

        body(*ins, *outs, *scr)

        @pl.when(last)
        def _():
            ex.wait(xins, xouts, xsems)

    res = pl.pallas_call(
        hosted, grid=grid, in_specs=list(in_specs) + [ANY] * xi, out_specs=list(out_specs) + [ANY] * xo,
        out_shape=list(out_shape) + list(ex.out_shape), scratch_shapes=list(scratch) + list(ex.sems),
        compiler_params=_cparams(sem), name=name)(*operands, *ex.operands)
    return res[:n_out], res[n_out:]


def _all_gather_chips(shards):
    n = len(shards)

    def copies(srcs, outs, sems):
        send_sems, recv_sems, local_sems = sems
        x, y, c, chips = _place()
        me = 2 * x + y
        halves = [_halves(c, s.shape[1], 16) for s in shards]
        local = [pltpu.make_async_copy(srcs[i], outs[i].at[:, me], local_sems.at[i]) for i in range(n)]

        def ici(i, j, chip, k):
            mine = halves[i][0]
            return pltpu.make_async_remote_copy(
                src_ref=srcs[i].at[:, mine, :], dst_ref=outs[i].at[:, k, mine, :], send_sem=send_sems.at[6 * i + j],
                recv_sem=recv_sems.at[6 * i + j], device_id=(*chip, c), device_id_type=MESH)

        def d2d(i, j, k, half):
            return pltpu.make_async_remote_copy(
                src_ref=outs[i].at[:, k, half, :], dst_ref=outs[i].at[:, k, half, :], send_sem=send_sems.at[6 * i + 3 + j],
                recv_sem=recv_sems.at[6 * i + 3 + j], device_id=(x, y, 1 - c), device_id_type=MESH)

        return me, chips, halves, local, ici, d2d

    def start(srcs, outs, sems):
        me, chips, _, local, ici, _ = copies(srcs, outs, sems)
        for cp in local:
            cp.start()
        for i in range(n):
            for j, chip in enumerate(chips):
                ici(i, j, chip, me).start()

    def wait(srcs, outs, sems):
        me, chips, halves, local, ici, d2d = copies(srcs, outs, sems)
        passed = []
        for i in range(n):
            for j, (px, py) in enumerate(chips):
                k = 2 * px + py
                ici(i, j, (px, py), k).wait_recv()
                fw = d2d(i, j, k, halves[i][0])
                fw.start()
                passed.append(fw)
        for i in range(n):
            for j, (px, py) in enumerate(chips):
                d2d(i, j, 2 * px + py, halves[i][1]).wait_recv()
        for i in range(n):
            for j, chip in enumerate(chips):
                ici(i, j, chip, me).wait_send()
        for cp in passed:
            cp.wait_send()
        for cp in local:
            cp.wait()

    return _Exchange(
        list(shards), [jax.ShapeDtypeStruct((s.shape[0], 4) + s.shape[1:], s.dtype) for s in shards],
        [pltpu.SemaphoreType.DMA((6 * n,)), pltpu.SemaphoreType.DMA((6 * n,)), pltpu.SemaphoreType.DMA((n,))],
        start, wait)


def _rs_swap_halves(gs):
    n = len(gs)

    def copies(srcs, gots, sems):
        send_sems, recv_sems = sems
        x, y, c, _ = _place()
        return [pltpu.make_async_remote_copy(
            src_ref=srcs[i].at[:, :, _halves(c, gs[i].shape[2], 8)[1], :], dst_ref=gots[i], send_sem=send_sems.at[i],
            recv_sem=recv_sems.at[i], device_id=(x, y, 1 - c), device_id_type=MESH) for i in range(n)]

    def start(srcs, gots, sems):
        for cp in copies(srcs, gots, sems):
            cp.start()

    def wait(srcs, gots, sems):
        for cp in copies(srcs, gots, sems):
            cp.wait()

    return _Exchange(
        list(gs), [jax.ShapeDtypeStruct(g.shape[:2] + (g.shape[2] // 2, g.shape[3]), g.dtype) for g in gs],
        [pltpu.SemaphoreType.DMA((n,)), pltpu.SemaphoreType.DMA((n,))], start, wait)


def _rs_add_pair(g, got, cidx, tb, out_dtype, name):
    nl, _, rh, cols = got.shape
    nb = rh // tb

    def body(c_ref, g_ref, o_ref, s_ref):
        s_ref[...] = (g_ref[...] + o_ref[...]).astype(out_dtype)

    blk = lambda half: pl.BlockSpec((None, None, tb, cols), (lambda l, k, i, c: (l, k, c[0] * nb + i, 0)) if half
                                    else (lambda l, k, i, c: (l, k, i, 0)))
    return pl.pallas_call(
        body,
        grid_spec=pltpu.PrefetchScalarGridSpec(num_scalar_prefetch=1, grid=(nl, 4, nb),
                                               in_specs=[blk(True), blk(False)], out_specs=blk(False)),
        out_shape=jax.ShapeDtypeStruct(got.shape, out_dtype),
        compiler_params=_cparams(("arbitrary", "arbitrary", "arbitrary")), name=name)(cidx, g, got)


def _rs_exchange_chips(pairs):
    n = len(pairs)

    def copies(srcs, gots, sems):
        send_sems, recv_sems, local_sems = sems
        x, y, c, chips = _place()
        me = 2 * x + y
        local = [pltpu.make_async_copy(srcs[i].at[:, me], gots[i].at[:, me], local_sems.at[i]) for i in range(n)]

        def ici(i, j, chip, frm, to):
            return pltpu.make_async_remote_copy(
                src_ref=srcs[i].at[:, to], dst_ref=gots[i].at[:, frm], send_sem=send_sems.at[3 * i + j],
                recv_sem=recv_sems.at[3 * i + j], device_id=(*chip, c), device_id_type=MESH)

        sent = [ici(i, j, (px, py), me, 2 * px + py) for i in range(n) for j, (px, py) in enumerate(chips)]
        recvd = [ici(i, j, (px, py), 2 * px + py, me) for i in range(n) for j, (px, py) in enumerate(chips)]
        return local, sent, recvd

    def start(srcs, gots, sems):
        local, sent, _ = copies(srcs, gots, sems)
        for cp in local + sent:
            cp.start()

    def wait(srcs, gots, sems):
        local, sent, recvd = copies(srcs, gots, sems)
        for cp in recvd:
            cp.wait_recv()
        for cp in sent:
            cp.wait_send()
        for cp in local:
            cp.wait()

    return _Exchange(
        list(pairs), [jax.ShapeDtypeStruct(p.shape, p.dtype) for p in pairs],
        [pltpu.SemaphoreType.DMA((3 * n,)), pltpu.SemaphoreType.DMA((3 * n,)), pltpu.SemaphoreType.DMA((n,))],
        start, wait)


def _rs_add_chips(got, tb, name):
    nl, _, rh, cols = got.shape

    def body(g_ref, s_ref):
        s_ref[...] = ((g_ref[0].astype(F32) + g_ref[1].astype(F32)) + g_ref[2].astype(F32)) + g_ref[3].astype(F32)

    return pl.pallas_call(
        body, grid=(nl, rh // tb),
        in_specs=[pl.BlockSpec((None, 4, tb, cols), lambda l, i: (l, 0, i, 0))],
        out_specs=pl.BlockSpec((None, tb, cols), lambda l, i: (l, i, 0)),
        out_shape=jax.ShapeDtypeStruct((nl, rh, cols), F32),
        compiler_params=_cparams(("arbitrary", "arbitrary")), name=name)(got)


def _rs_join_halves(totals):
    n = len(totals)

    def copies(srcs, outs, sems):
        send_sems, recv_sems, local_sems = sems
        x, y, c, _ = _place()
        local, send, recv = [], [], []
        for i in range(n):
            mine, other = _halves(c, 2 * totals[i].shape[1], 8)
            local.append(pltpu.make_async_copy(srcs[i], outs[i].at[:, mine, :], local_sems.at[i]))
            for lst, rows in ((send, mine), (recv, other)):
                lst.append(pltpu.make_async_remote_copy(
                    src_ref=srcs[i], dst_ref=outs[i].at[:, rows, :], send_sem=send_sems.at[i], recv_sem=recv_sems.at[i],
                    device_id=(x, y, 1 - c), device_id_type=MESH))
        return local, send, recv

    def start(srcs, outs, sems):
        local, send, _ = copies(srcs, outs, sems)
        for cp in local + send:
            cp.start()

    def wait(srcs, outs, sems):
        local, send, recv = copies(srcs, outs, sems)
        for cp in recv:
            cp.wait_recv()
        for cp in send:
            cp.wait_send()
        for cp in local:
            cp.wait()

    return _Exchange(
        list(totals), [jax.ShapeDtypeStruct((t.shape[0], 2 * t.shape[1], t.shape[2]), F32) for t in totals],
        [pltpu.SemaphoreType.DMA((n,)), pltpu.SemaphoreType.DMA((n,)), pltpu.SemaphoreType.DMA((n,))], start, wait)


def _rows(a):
    return a.reshape(-1, LANES)


def _pad_rows(a, mult):
    r = (-a.shape[0]) % mult
    return a if r == 0 else jnp.concatenate([a, jnp.zeros((r, a.shape[1]), a.dtype)], axis=0)


def kernel(x, norm_g, w_in, b_f, cf_dw, cf_dw_b, cf_ln_g, cf_ln_b, cf_pw, sc_dw, q_norm_g, k_norm_g, w_out, loss_target, m_norm_g, m_w_in, m_b_f, m_cf_dw, m_cf_dw_b, m_cf_ln_g, m_cf_ln_b, m_cf_pw, m_sc_dw, m_q_norm_g, m_k_norm_g, m_w_out, v_norm_g, v_w_in, v_b_f, v_cf_dw, v_cf_dw_b, v_cf_ln_g, v_cf_ln_b, v_cf_pw, v_sc_dw, v_q_norm_g, v_k_norm_g, v_w_out):
    bsz, seq, _ = x.shape
    t = bsz * seq

    taps = jnp.concatenate([cf_dw.reshape(-1), sc_dw.reshape(-1)])
    taps_bf = lax.bitcast_convert_type(taps, jnp.bfloat16).reshape(-1, LANES)
    n_pw, n_taps = 2 * 64 * DC // LANES, taps_bf.shape[0]
    small_w = _pad_rows(jnp.concatenate([_rows(cf_pw.astype(BF)), taps_bf], axis=0), 32)[None]
    win_b, wout_b = w_in.astype(BF), w_out.astype(BF)
    win0_g, small_g = _run_exchange(_all_gather_chips([win_b[0:1], small_w]), "all_gather_first")
    gather_rest = _all_gather_chips([win_b[1:DEPTH], wout_b])
    full_w_in = lambda g: jnp.pad(jnp.concatenate([g[k] for k in range(4)], axis=-1), ((0, 0), (0, NP - N_IN)))
    w_in_full = [full_w_in(win0_g[0])] + [None] * (DEPTH - 1)
    w_out_full = None
    cf_pw_full = jnp.concatenate([small_g[0, k, 0:n_pw].reshape(DEPTH, 64, DC) for k in range(4)], axis=1)
    taps_all = [lax.bitcast_convert_type(small_g[0, k, n_pw:n_pw + n_taps].reshape(-1, 2), F32) for k in range(4)]
    n_cfdw = DEPTH * CFW * 64
    cf_dw_full = jnp.concatenate([tk[:n_cfdw].reshape(DEPTH, CFW, 64) for tk in taps_all], axis=-1)
    sc_dw_full = jnp.concatenate([tk[n_cfdw:].reshape(DEPTH, SCW, 64) for tk in taps_all], axis=-1)

    bf_pad = jnp.pad(b_f, ((0, 0), (0, LANES - NH)))

    xs = [x.reshape(t, D)]
    saved = []
    dy = loss_part = None
    for l in range(DEPTH):
        xl = xs[-1]
        gq, gk = q_norm_g[l].reshape(1, DA), k_norm_g[l].reshape(1, DA)
        pcf, psc, pq, pk, vb, pz, pf = _fwd_inproj(xl, norm_g[l][None], w_in_full[l], f"fwd_inproj_{l}")
        b3 = lambda a: a.reshape(bsz, seq, a.shape[-1])
        ycs = _fwd_conv(b3(pcf), b3(psc), cf_dw_full[l], cf_dw_b[l][None], cf_ln_g[l][None], cf_ln_b[l][None],
                        cf_pw_full[l], sc_dw_full[l], f"fwd_conv_{l}")
        qs, kn, cq, crow8 = _fwd_attn_prep(b3(pq), b3(pk), b3(pf), gq, gk, bf_pad[l][None], f"fwd_attn_prep_{l}")
        crow = crow8.reshape(bsz, NH // 2, 2, seq)
        (o, ohp, yatt, lse), landed = _fwd_attn(qs, kn, b3(vb), cq, crow, b3(pz), f"fwd_attn_{l}",
                                                gather_rest if l == 0 else None)
        if l == 0:
            win_rest_g, wout_g = landed
            w_in_full[1:] = [full_w_in(win_rest_g[i]) for i in range(DEPTH - 1)]
            w_out_full = wout_g.reshape(DEPTH, D, D)
        ycs2, yatt2 = ycs.reshape(t, 512), yatt.reshape(t, DA)
        if l + 1 < DEPTH:
            xs.append(_fwd_outproj(xl, ycs2, yatt2, w_out_full[l], f"fwd_outproj_{l}"))
        else:
            dy, loss_part = _fwd_outproj_loss(xl, ycs2, yatt2, w_out_full[l], loss_target.reshape(t, D),
                                              f"fwd_outproj_loss_{l}")
        saved.append((pcf, psc, pq, pk, vb, pz, pf, ycs2, yatt2, o, ohp, qs, kn, cq, crow, lse))

    grads = [None] * DEPTH
    reduced = [None] * DEPTH
    cidx = lax.axis_index("c").astype(jnp.int32).reshape(1)
    big_tiles = [256, 128]

    def big_packs(gl):
        return [jnp.stack([gl["w_in"][:, NSH * k:NSH * (k + 1)] for k in range(4)])[None],
                gl["w_out"].reshape(1, 4, 256, D)]

    def add_pairs(packs, gots, tiles, dtypes, tag):
        return [_rs_add_pair(p, got, cidx, tb, dt, f"rs_add_pair_{tag}_{i}")
                for i, (p, got, tb, dt) in enumerate(zip(packs, gots, tiles, dtypes))]

    def add_chips(gots2, tiles, tag):
        return [_rs_add_chips(got, tb, f"rs_add_chips_{tag}_{i}") for i, (got, tb) in enumerate(zip(gots2, tiles))]

    for l in reversed(range(DEPTH)):
        pcf, psc, pq, pk, vb, pz, pf, ycs2, yatt2, o, ohp, qs, kn, cq, crow, lse = saved[l]
        gq, gk = q_norm_g[l].reshape(1, DA), k_norm_g[l].reshape(1, DA)
        b3 = lambda a: a.reshape(bsz, seq, a.shape[-1])
        f2 = lambda a: a.reshape(t, a.shape[-1])
        packs = big_packs(grads[l + 1]) if l + 1 < DEPTH else None
        (dycs, do, dz, dl, d_wout), gots = _bwd_outproj(
            dy, ycs2, yatt2, f2(o), f2(ohp), pz, w_out_full[l], f"bwd_outproj_{l}",
            _rs_swap_halves(packs) if packs else None)
        pairs = add_pairs(packs, gots, big_tiles, [BF, BF], l + 1) if packs else None
        (dqs, dkn, dv, dcrow), gots2 = _bwd_attn(qs, kn, b3(vb), b3(do), cq, lse, b3(dl), crow, f"bwd_attn_{l}",
                                                 _rs_exchange_chips(pairs) if packs else None)
        totals = add_chips(gots2, big_tiles, l + 1) if packs else None
        (dpq, dpk, dpf, d_gq, d_gk, d_bf), joined = _bwd_attn_post(
            b3(pq), b3(pk), dqs, dkn, b3(pf), dcrow.reshape(bsz, NH, seq), gq, gk, bf_pad[l][None],
            f"bwd_attn_post_{l}", _rs_join_halves(totals) if packs else None)
        if packs:
            reduced[l + 1] = joined
        dcf, dsc, d_cfdw, d_cfb, d_lng, d_lnb, d_pw, d_scdw = _bwd_conv(
            b3(pcf), b3(psc), b3(dycs), cf_dw_full[l], cf_dw_b[l][None], cf_ln_g[l][None], cf_ln_b[l][None],
            cf_pw_full[l], sc_dw_full[l], f"bwd_conv_{l}")
        dy, d_ng, d_win = _bwd_inproj(xs[l], norm_g[l][None], dy, w_in_full[l], f2(dcf), f2(dsc), f2(dpq), f2(dpk),
                                      f2(dv), dz, f2(dpf), f"bwd_inproj_{l}")
        grads[l] = dict(norm_g=d_ng[0], w_in=d_win, b_f=d_bf[0, :NH], cf_dw=d_cfdw, cf_dw_b=d_cfb[0],
                        cf_ln_g=d_lng[0], cf_ln_b=d_lnb[0], cf_pw=d_pw, sc_dw=d_scdw,
                        q_norm_g=d_gq.reshape(NH, HD), k_norm_g=d_gk.reshape(NH, HD), w_out=d_wout)
    grad_x = dy.reshape(bsz, seq, D)
    gw = {n: jnp.stack([grads[l][n] for l in range(DEPTH)]) for n in grads[0] if n not in ("w_in", "w_out")}

    rep_names = ("norm_g", "b_f", "cf_dw_b", "cf_ln_g", "cf_ln_b", "q_norm_g", "k_norm_g")
    rep = jnp.concatenate([jnp.pad(gw[n].reshape(-1), (0, (-gw[n].size) % LANES)) for n in rep_names]
                          + [jnp.pad(loss_part.reshape(-1), (0, LANES - 1))]).reshape(-1, LANES)
    blocks = []
    for k in range(4):
        small = jnp.concatenate([gw["cf_dw"][:, :, 64 * k:64 * (k + 1)].reshape(-1),
                                 gw["sc_dw"][:, :, 64 * k:64 * (k + 1)].reshape(-1)]).reshape(-1, LANES)
        blocks.append(_pad_rows(jnp.concatenate([_rows(gw["cf_pw"][:, 64 * k:64 * (k + 1), :]), small, rep], axis=0), 16))
    g_small = jnp.stack(blocks)[None]
    packs = big_packs(grads[0]) + [g_small]
    tiles = big_tiles + [g_small.shape[2] // 2]
    gots = _run_exchange(_rs_swap_halves(packs), "rs_swap_halves_last")
    pairs = add_pairs(packs, gots, tiles, [BF, BF, F32], 0)
    gots2 = _run_exchange(_rs_exchange_chips(pairs), "rs_exchange_chips_last")
    red_win0, red_wout0, red_small = _run_exchange(_rs_join_halves(add_chips(gots2, tiles, 0)), "rs_join_halves_last")
    reduced[0] = (red_win0, red_wout0)

    n_small = (DEPTH * CFW * 64 + DEPTH * SCW * 64) // LANES
    red = red_small[0]
    pos = [0]

    def take(nrows):
        pos[0] += nrows
        return red[pos[0] - nrows:pos[0]]

    g = {}
    g["w_in"] = jnp.concatenate([reduced[l][0] for l in range(DEPTH)], axis=0)
    g["w_out"] = jnp.concatenate([reduced[l][1] for l in range(DEPTH)], axis=0)
    g["cf_pw"] = take(n_pw).reshape(DEPTH, 64, DC)
    small = take(n_small).reshape(-1)
    g["cf_dw"] = small[:n_cfdw].reshape(DEPTH, CFW, 64)
    g["sc_dw"] = small[n_cfdw:].reshape(DEPTH, SCW, 64)
    shapes = dict(norm_g=norm_g.shape, b_f=b_f.shape, cf_dw_b=cf_dw_b.shape, cf_ln_g=cf_ln_g.shape,
                  cf_ln_b=cf_ln_b.shape, q_norm_g=q_norm_g.shape, k_norm_g=k_norm_g.shape)
    for n in rep_names:
        size = 1
        for s in shapes[n]:
            size *= s
        g[n] = take(-(-size // LANES)).reshape(-1)[:size].reshape(shapes[n])
    loss = take(1)[0, 0]

    order = ("norm_g", "w_in", "b_f", "cf_dw", "cf_dw_b", "cf_ln_g", "cf_ln_b", "cf_pw", "sc_dw", "q_norm_g",
             "k_norm_g", "w_out")
    weights = dict(norm_g=norm_g, w_in=w_in, b_f=b_f, cf_dw=cf_dw, cf_dw_b=cf_dw_b, cf_ln_g=cf_ln_g, cf_ln_b=cf_ln_b,
                   cf_pw=cf_pw, sc_dw=sc_dw, q_norm_g=q_norm_g, k_norm_g=k_norm_g, w_out=w_out)
    ms = dict(norm_g=m_norm_g, w_in=m_w_in, b_f=m_b_f, cf_dw=m_cf_dw, cf_dw_b=m_cf_dw_b, cf_ln_g=m_cf_ln_g,
              cf_ln_b=m_cf_ln_b, cf_pw=m_cf_pw, sc_dw=m_sc_dw, q_norm_g=m_q_norm_g, k_norm_g=m_k_norm_g, w_out=m_w_out)
    vs = dict(norm_g=v_norm_g, w_in=v_w_in, b_f=v_b_f, cf_dw=v_cf_dw, cf_dw_b=v_cf_dw_b, cf_ln_g=v_cf_ln_g,
              cf_ln_b=v_cf_ln_b, cf_pw=v_cf_pw, sc_dw=v_sc_dw, q_norm_g=v_q_norm_g, k_norm_g=v_k_norm_g, w_out=v_w_out)
    delta, new_m, new_v = {}, {}, {}
    two_d = lambda a: a.reshape(-1, a.shape[-1])
    for n in ("w_in", "w_out"):
        outs = _adamw(two_d(weights[n]), two_d(g[n]), two_d(ms[n]), two_d(vs[n]), 256, f"adamw_{n}")
        delta[n], new_m[n], new_v[n] = (a.reshape(weights[n].shape) for a in outs)
    small_names = [n for n in order if n not in ("w_in", "w_out")]
    flat = lambda d: _pad_rows(jnp.pad(jnp.concatenate([d[n].reshape(-1) for n in small_names]),
                                       (0, (-sum(weights[n].size for n in small_names)) % LANES)).reshape(-1, LANES), 8)
    fw = flat(weights)
    outs = _adamw(fw, flat(g), flat(ms), flat(vs), fw.shape[0], "adamw_small")
    off = 0
    for n in small_names:
        size = weights[n].size
        for dst, src in zip((delta, new_m, new_v), outs):
            dst[n] = src.reshape(-1)[off:off + size].reshape(weights[n].shape)
        off += size

    return (loss, grad_x, *[g[n] for n in order], *[delta[n] for n in order], *[new_m[n] for n in order],
            *[new_v[n] for n in order])
```

```python
import jax
import jax.numpy as jnp
from jax import lax
from jax.experimental import pallas as pl
from jax.experimental.pallas import tpu as pltpu

F32 = jnp.float32
BF = jnp.bfloat16
MESH = pl.DeviceIdType.MESH

DEPTH = 2
D = 1024
DC = 256
DA = 512
NH = 8
HD = 64
CFW = 31
SCW = 3
N_IN = 3848
NP = 3968
NSH = N_IN // 4
HALO = 32
ATT_ROWS = 32
FWD_ROWS = 256
EPS = 1e-6
LANES = 128
VMEM_LIMIT = 56 * 1024 * 1024

C_CF, C_SC, C_Q, C_K, C_V, C_Z, C_F = 0, 768, 1792, 2304, 2816, 3328, 3840

ADAM_LR = 0.001
ADAM_B1 = 0.9
ADAM_B2 = 0.999
ADAM_EPS = 1e-08
ADAM_WD = 0.01
ADAM_STEP = 10


def _cparams(sem=None):
    return pltpu.CompilerParams(dimension_semantics=sem, vmem_limit_bytes=VMEM_LIMIT)


def _dot(a, b):
    return jnp.dot(a, b, preferred_element_type=F32)


def _dot_nt(a, b):
    return lax.dot_general(a, b, (((1,), (1,)), ((), ())), preferred_element_type=F32)


def _dot_tn(a, b):
    return lax.dot_general(a, b, (((0,), (0,)), ((), ())), preferred_element_type=F32)


def _split3(x):
    hi = x.astype(BF)
    r1 = x - hi.astype(F32)
    mid = r1.astype(BF)
    lo = (r1 - mid.astype(F32)).astype(BF)
    return hi, mid, lo


def _dot_exact(a_bf, x):
    hi, mid, lo = _split3(x)
    return _dot(a_bf, hi) + _dot(a_bf, mid) + _dot(a_bf, lo)


def _sigmoid(x):
    return 1.0 / (1.0 + jnp.exp(-x))


def _seg_sum64(x):
    i = lax.broadcasted_iota(jnp.int32, (LANES, LANES), 0)
    j = lax.broadcasted_iota(jnp.int32, (LANES, LANES), 1)
    g = ((i >= HD) == (j >= HD)).astype(BF)
    hi, mid, lo = _split3(x)
    return _dot(hi, g) + _dot(mid, g) + _dot(lo, g)


def _fwd_inproj(x, g, w, name):
    t = x.shape[0]
    tm = 256

    def body(x_ref, g_ref, w_ref, cf_ref, sc_ref, q_ref, k_ref, v_ref, z_ref, f_ref):
        xt = x_ref[...]
        r = lax.rsqrt(jnp.mean(xt * xt, axis=-1, keepdims=True) + EPS)
        h = ((xt * r) * g_ref[...]).astype(BF)
        cf_ref[:, 0:512] = _dot(h, w_ref[:, 0:512])
        cf_ref[:, 512:768] = _dot(h, w_ref[:, 512:768])
        sc_ref[:, 0:512] = _dot(h, w_ref[:, C_SC:C_SC + 512])
        sc_ref[:, 512:1024] = _dot(h, w_ref[:, C_SC + 512:C_Q])
        q_ref[...] = _dot(h, w_ref[:, C_Q:C_K])
        k_ref[...] = _dot(h, w_ref[:, C_K:C_V])
        v_ref[...] = _dot(h, w_ref[:, C_V:C_Z]).astype(BF)
        z_ref[...] = _dot(h, w_ref[:, C_Z:C_F])
        f_ref[...] = _dot(h, w_ref[:, C_F:NP])

    row = lambda n: pl.BlockSpec((tm, n), lambda i: (i, 0))
    return pl.pallas_call(
        body, grid=(t // tm,),
        in_specs=[row(D), pl.BlockSpec((1, D), lambda i: (0, 0)), pl.BlockSpec((D, NP), lambda i: (0, 0))],
        out_specs=[row(768), row(1024), row(DA), row(DA), row(DA), row(DA), row(LANES)],
        out_shape=[jax.ShapeDtypeStruct((t, 768), F32), jax.ShapeDtypeStruct((t, 1024), F32),
                   jax.ShapeDtypeStruct((t, DA), F32), jax.ShapeDtypeStruct((t, DA), F32),
                   jax.ShapeDtypeStruct((t, DA), BF), jax.ShapeDtypeStruct((t, DA), F32),
                   jax.ShapeDtypeStruct((t, LANES), F32)],
        compiler_params=_cparams(("arbitrary",)), name=name)(x, g, w)


def _conformer_fwd(a, g, ha, hg, first, ubuf, dw_ref, bias, lng, lnb):
    ts = a.shape[0]
    u0 = a * _sigmoid(g)
    ubuf[0:HALO, :] = jnp.where(first, 0.0, ha * _sigmoid(hg))
    ubuf[HALO:HALO + ts, :] = u0
    u1 = jnp.zeros((ts, DC), F32) + bias
    for k in range(CFW):
        u1 = u1 + dw_ref[k:k + 1, :] * ubuf[pl.ds(HALO - (CFW - 1) + k, ts), :]
    mu = jnp.mean(u1, axis=-1, keepdims=True)
    xc = u1 - mu
    rstd = lax.rsqrt(jnp.mean(xc * xc, axis=-1, keepdims=True) + EPS)
    n = xc * rstd
    u2 = n * lng + lnb
    s2 = _sigmoid(u2)
    u3 = u2 * s2
    return u0, n, rstd, u2, s2, u3


def _shortconv_fwd(c, xs, hc, hx, first, mbuf, dw_ref):
    ts = c.shape[0]
    mbuf[0:HALO, :] = jnp.where(first, 0.0, hc * hx)
    mbuf[HALO:HALO + ts, :] = c * xs
    cv = jnp.zeros((ts, DC), F32)
    for k in range(SCW):
        cv = cv + dw_ref[k:k + 1, :] * mbuf[pl.ds(HALO - (SCW - 1) + k, ts), :]
    return cv


def _conv_specs(bsz, seq, ts, order):
    nt = seq // ts
    tile = (lambda i: i) if order > 0 else (lambda i: nt - 1 - i)
    hrow = lambda i: jnp.maximum(tile(i) * (ts // HALO) - 1, 0)
    cur = lambda n: pl.BlockSpec((None, ts, n), lambda b, i: (b, tile(i), 0))
    halo = lambda j: pl.BlockSpec((None, HALO, DC), lambda b, i: (b, hrow(i), j))
    full = lambda r, c: pl.BlockSpec((r, c), lambda b, i: (0, 0))
    return nt, tile, cur, halo, full


def _fwd_conv(pcf, psc, cf_dw, cf_dw_b, ln_g, ln_b, cf_pw, sc_dw, name):
    bsz, seq, _ = pcf.shape
    ts = 512
    nt, tile, cur, halo, full = _conv_specs(bsz, seq, ts, +1)

    def body(cf_ref, ha_ref, hg_ref, sc_ref, hc_ref, hx_ref, dw_ref, b_ref, lg_ref, lb_ref, pw_ref, sdw_ref,
             y_ref, ubuf, mbuf):
        first = pl.program_id(1) == 0
        _, _, _, _, _, u3 = _conformer_fwd(cf_ref[:, 0:256], cf_ref[:, 256:512], ha_ref[...], hg_ref[...], first,
                                           ubuf, dw_ref, b_ref[...], lg_ref[...], lb_ref[...])
        z = cf_ref[:, 512:768]
        y_ref[:, 0:256] = _dot(u3.astype(BF), pw_ref[...]) * (z * _sigmoid(z))
        cv = _shortconv_fwd(sc_ref[:, 256:512], sc_ref[:, 512:768], hc_ref[...], hx_ref[...], first, mbuf, sdw_ref)
        zs = sc_ref[:, 768:1024]
        y_ref[:, 256:512] = sc_ref[:, 0:256] * cv * (zs * _sigmoid(zs))

    return pl.pallas_call(
        body, grid=(bsz, nt),
        in_specs=[cur(768), halo(0), halo(1), cur(1024), halo(1), halo(2),
                  full(CFW, DC), full(1, DC), full(1, DC), full(1, DC), full(DC, DC), full(SCW, DC)],
        out_specs=pl.BlockSpec((None, ts, 512), lambda b, i: (b, i, 0)),
        out_shape=jax.ShapeDtypeStruct((bsz, seq, 512), F32),
        scratch_shapes=[pltpu.VMEM((HALO + ts, DC), F32), pltpu.VMEM((HALO + ts, DC), F32)],
        compiler_params=_cparams(("arbitrary", "arbitrary")), name=name,
    )(pcf, pcf, pcf, psc, psc, psc, cf_dw, cf_dw_b, ln_g, ln_b, cf_pw, sc_dw)


def _head_rms(xb, gb):
    ms = _seg_sum64(xb * xb) * (1.0 / HD)
    r = lax.rsqrt(ms + EPS)
    xhat = xb * r
    return xhat, r, xhat * gb


def _fwd_attn_prep(pq, pk, pf, gq, gk, bf, name):
    bsz, seq, _ = pq.shape
    ts = 512
    nt = seq // ts

    def body(q_ref, k_ref, f_ref, gq_ref, gk_ref, bf_ref, qs_ref, kn_ref, cq_ref, crow_ref, carry):
        @pl.when(pl.program_id(1) == 0)
        def _():
            carry[...] = jnp.zeros_like(carry)

        for jb in range(DA // LANES):
            sl = slice(jb * LANES, (jb + 1) * LANES)
            _, _, qn = _head_rms(q_ref[:, sl], gq_ref[:, sl])
            qs_ref[:, sl] = (qn * (1.0 / 8.0)).astype(BF)
            _, _, kn = _head_rms(k_ref[:, sl], gk_ref[:, sl])
            kn_ref[:, sl] = kn.astype(BF)

        xf = f_ref[...] + bf_ref[...]
        lf = jnp.minimum(xf, 0.0) - jnp.log(1.0 + jnp.exp(-jnp.abs(xf)))
        ti = lax.broadcasted_iota(jnp.int32, (ts, ts), 0)
        si = lax.broadcasted_iota(jnp.int32, (ts, ts), 1)
        c = _dot_exact((si <= ti).astype(BF), lf) + carry[...]
        carry[...] = c[ts - 1:ts, :]
        hj = lax.broadcasted_iota(jnp.int32, (LANES, DA), 0)
        ll = lax.broadcasted_iota(jnp.int32, (LANES, DA), 1)
        dd = ll - hj * HD
        chi, cmid, clo = _split3(c)
        e = ((dd >= 0) & (dd < HD)).astype(BF)
        cq_ref[...] = _dot(chi, e) + _dot(cmid, e) + _dot(clo, e)
        hh = lax.broadcasted_iota(jnp.int32, (16, LANES), 0)
        jj = lax.broadcasted_iota(jnp.int32, (16, LANES), 1)
        sel = (hh == jj).astype(BF)
        cr = _dot_nt(sel, chi) + _dot_nt(sel, cmid) + _dot_nt(sel, clo)
        crow_ref[...] = cr[0:NH, :]

    tile = lambda n: pl.BlockSpec((None, ts, n), lambda b, i: (b, i, 0))
    vec = lambda n: pl.BlockSpec((1, n), lambda b, i: (0, 0))
    return pl.pallas_call(
        body, grid=(bsz, nt),
        in_specs=[tile(DA), tile(DA), tile(LANES), vec(DA), vec(DA), vec(LANES)],
        out_specs=[tile(DA), tile(DA), tile(DA), pl.BlockSpec((None, NH, ts), lambda b, i: (b, 0, i))],
        out_shape=[jax.ShapeDtypeStruct((bsz, seq, DA), BF), jax.ShapeDtypeStruct((bsz, seq, DA), BF),
                   jax.ShapeDtypeStruct((bsz, seq, DA), F32), jax.ShapeDtypeStruct((bsz, NH, seq), F32)],
        scratch_shapes=[pltpu.VMEM((1, LANES), F32)],
        compiler_params=_cparams(("arbitrary", "arbitrary")), name=name)(pq, pk, pf, gq, gk, bf)


def _tile_rows(i, rows):
    return pl.ds(i * rows, rows) if isinstance(i, int) else pl.ds(pl.multiple_of(i * rows, rows), rows)


def _causal_mask(tq, tk):
    r = lax.broadcasted_iota(jnp.int32, (tq, tk), 0)
    c = lax.broadcasted_iota(jnp.int32, (tq, tk), 1)
    return r - c


def _fwd_attn(qs, kn, vb, cq, crow, pz, name, ex=None):
    bsz, seq, _ = qs.shape
    tq = 256
    nq = seq // tq
    npair = NH // 2

    def body(q_ref, k_ref, v_ref, cq_ref, cr_ref, z_ref, o_ref, ohp_ref, y_ref, lse_ref,
             s_buf, p_buf, acc_ref, accl_ref):
        qi = pl.program_id(2)
        head0 = lax.broadcasted_iota(jnp.int32, (1, LANES), 1) < HD
        q = q_ref[...]
        zq = jnp.zeros_like(q)
        qm = (jnp.where(head0, q, zq), jnp.where(head0, zq, q))
        ct = (cq_ref[:, 0:1], cq_ref[:, HD:HD + 1])

        tile = lambda j: _tile_rows(j, tq)

        def scores(j, slot):
            kb = k_ref[tile(j), :]
            for hh in range(2):
                s_buf[slot, hh] = _dot_nt(qm[hh], kb) + (ct[hh] - cr_ref[hh:hh + 1, tile(j)])

        def weighted_values(slot, j, al):
            vb_t = v_ref[tile(j), :]
            acc_ref[...] = al * acc_ref[...] + jnp.where(head0, _dot(p_buf[slot, 0, 0], vb_t),
                                                         _dot(p_buf[slot, 1, 0], vb_t))
            accl_ref[...] = al * accl_ref[...] + jnp.where(head0, _dot(p_buf[slot, 0, 1], vb_t),
                                                           _dot(p_buf[slot, 1, 1], vb_t))

        def softmax(slot, first_visible, m, l):
            m_new, l_new, al_new = ([], []), ([], []), []
            for r in range(tq // FWD_ROWS):
                rows = slice(r * FWD_ROWS, (r + 1) * FWD_ROWS)
                visible = _causal_mask(FWD_ROWS, tq) >= first_visible - r * FWD_ROWS
                alphas = []
                for hh in range(2):
                    s = jnp.where(visible, s_buf[slot, hh, rows, :], -1e30)
                    m_old = m[hh][rows]
                    m2 = jnp.maximum(m_old, jnp.max(s, axis=-1, keepdims=True))
                    p = jnp.exp(s - m2)
                    alpha = jnp.exp(m_old - m2)
                    l_new[hh].append(alpha * l[hh][rows] + jnp.sum(p, axis=-1, keepdims=True))
                    m_new[hh].append(m2)
                    pb = p.astype(BF)
                    p_buf[slot, hh, 0, rows, :] = pb
                    p_buf[slot, hh, 1, rows, :] = (p - pb.astype(F32)).astype(BF)
                    alphas.append(alpha)
                al_new.append(jnp.where(head0, alphas[0], alphas[1]))
            cat = lambda parts: jnp.concatenate(parts, axis=0)
            return (cat(m_new[0]), cat(m_new[1])), (cat(l_new[0]), cat(l_new[1])), cat(al_new)

        clamp = lambda j: jnp.clip(j, 0, nq - 1)

        def step(j, slot, carry):
            m, l, al = carry
            weighted_values(1 - slot, clamp(j - 1), al)
            m, l, al = softmax(slot, (j - qi) * tq, m, l)
            scores(clamp(j + 1), 1 - slot)
            return m, l, al

        scores(0, 0)
        p_buf[1] = jnp.zeros((2, 2, tq, tq), BF)
        acc_ref[...] = jnp.zeros((tq, LANES), F32)
        accl_ref[...] = jnp.zeros((tq, LANES), F32)
        neg = jnp.full((tq, 1), -1e30, F32)
        zcol = jnp.zeros((tq, 1), F32)
        trips = (qi + 2) // 2
        m, l, al = lax.fori_loop(0, trips, lambda t, cy: step(2 * t + 1, 1, step(2 * t, 0, cy)),
                                 ((neg, neg), (zcol, zcol), jnp.ones((tq, LANES), F32)))
        weighted_values(1, clamp(2 * trips - 1), al)
        inv = jnp.where(head0, 1.0 / l[0], 1.0 / l[1])
        o = acc_ref[...] * inv
        z = z_ref[...]
        o_ref[...] = o
        ohp_ref[...] = (acc_ref[...] + accl_ref[...]) * inv
        y_ref[...] = o * (z * _sigmoid(z))
        lse_ref[...] = jnp.where(head0, m[0] + jnp.log(l[0]), m[1] + jnp.log(l[1]))

    qblk = pl.BlockSpec((None, tq, LANES), lambda b, h, i: (b, i, h))
    kvblk = pl.BlockSpec((None, seq, LANES), lambda b, h, i: (b, 0, h))
    return _host_call(
        body, (bsz, npair, nq),
        [qblk, kvblk, kvblk, qblk, pl.BlockSpec((None, None, 2, seq), lambda b, h, i: (b, h, 0, 0)), qblk],
        [qblk, qblk, qblk, qblk], [jax.ShapeDtypeStruct((bsz, seq, DA), F32)] * 4,
        [pltpu.VMEM((2, 2, tq, tq), F32), pltpu.VMEM((2, 2, 2, tq, tq), BF), pltpu.VMEM((tq, LANES), F32),
         pltpu.VMEM((tq, LANES), F32)],
        (qs, kn, vb, cq, crow, pz), name, ex)


def _fwd_outproj(x, ycs, yatt, wo, name):
    t = x.shape[0]
    tm = 512

    def body(x_ref, a_ref, b_ref, w_ref, o_ref):
        o_ref[...] = (x_ref[...] + _dot(a_ref[...].astype(BF), w_ref[0:512, :])
                      + _dot(b_ref[...].astype(BF), w_ref[512:1024, :]))

    row = lambda n: pl.BlockSpec((tm, n), lambda i: (i, 0))
    return pl.pallas_call(
        body, grid=(t // tm,),
        in_specs=[row(D), row(512), row(512), pl.BlockSpec((D, D), lambda i: (0, 0))],
        out_specs=row(D), out_shape=jax.ShapeDtypeStruct((t, D), F32),
        compiler_params=_cparams(("arbitrary",)), name=name)(x, ycs, yatt, wo)


def _fwd_outproj_loss(x, ycs, yatt, wo, target, name):
    t = x.shape[0]
    tm = 512

    def body(x_ref, a_ref, b_ref, w_ref, t_ref, dy_ref, loss_ref):
        @pl.when(pl.program_id(0) == 0)
        def _():
            loss_ref[...] = jnp.zeros_like(loss_ref)

        y = (x_ref[...] + _dot(a_ref[...].astype(BF), w_ref[0:512, :])
             + _dot(b_ref[...].astype(BF), w_ref[512:1024, :]))
        err = y - t_ref[...]
        dy_ref[...] = err * (1.0 / D)
        per_tok = jnp.mean(err * err, axis=-1, keepdims=True)
        loss_ref[...] += 0.5 * jnp.sum(per_tok, axis=0, keepdims=True)

    row = lambda n: pl.BlockSpec((tm, n), lambda i: (i, 0))
    return pl.pallas_call(
        body, grid=(t // tm,),
        in_specs=[row(D), row(512), row(512), pl.BlockSpec((D, D), lambda i: (0, 0)), row(D)],
        out_specs=[row(D), pl.BlockSpec((1, 1), lambda i: (0, 0))],
        out_shape=[jax.ShapeDtypeStruct((t, D), F32), jax.ShapeDtypeStruct((1, 1), F32)],
        compiler_params=_cparams(("arbitrary",)), name=name)(x, ycs, yatt, wo, target)


def _dsilu(x, s):
    return s * (1.0 + x * (1.0 - s))


def _bwd_outproj(dy, ycs, yatt, o, ohp, pz, wo, name, ex=None):
    t = dy.shape[0]
    tm = 256

    def body(dy_ref, a_ref, b_ref, o_ref, ohp_ref, z_ref, w_ref, dcs_ref, do_ref, dz_ref, dl_ref, dw_ref):
        @pl.when(pl.program_id(0) == 0)
        def _():
            dw_ref[...] = jnp.zeros_like(dw_ref)

        dyb = dy_ref[...].astype(BF)
        dw_ref[0:512, :] += _dot_tn(a_ref[...].astype(BF), dyb)
        dw_ref[512:1024, :] += _dot_tn(b_ref[...].astype(BF), dyb)
        dcs_ref[...] = _dot_nt(dyb, w_ref[0:512, :])
        dyatt = _dot_nt(dyb, w_ref[512:1024, :])
        z = z_ref[...]
        sz = _sigmoid(z)
        o_t = o_ref[...]
        dob = (dyatt * (z * sz)).astype(BF)
        do_ref[...] = dob
        dz_ref[...] = dyatt * o_t * _dsilu(z, sz)
        prod = dob.astype(F32) * ohp_ref[...]
        for jb in range(DA // LANES):
            sl = slice(jb * LANES, (jb + 1) * LANES)
            dl_ref[:, sl] = _seg_sum64(prod[:, sl])

    row = lambda n: pl.BlockSpec((tm, n), lambda i: (i, 0))
    return _host_call(
        body, (t // tm,),
        [row(D), row(512), row(512), row(DA), row(DA), row(DA), pl.BlockSpec((D, D), lambda i: (0, 0))],
        [row(512), row(DA), row(DA), row(DA), pl.BlockSpec((D, D), lambda i: (0, 0))],
        [jax.ShapeDtypeStruct((t, 512), F32), jax.ShapeDtypeStruct((t, DA), BF),
         jax.ShapeDtypeStruct((t, DA), F32), jax.ShapeDtypeStruct((t, DA), F32),
         jax.ShapeDtypeStruct((D, D), F32)], [],
        (dy, ycs, yatt, o, ohp, pz, wo), name, ex)


def _bwd_attn(qs, kn, vb, do, cq, lse, dl, crow, name, ex=None):
    bsz, seq, _ = qs.shape
    tq = 256
    nq = seq // tq
    npair = NH // 2

    def body(q_ref, k_ref, v_ref, do_ref, cq_ref, lse_ref, dl_ref, cr_ref, dq_ref, dk_ref, dv_ref, dc_ref,
             qm, dom, s_buf, dp_buf, pd_buf):
        rc = _causal_mask(tq, tq)
        head0 = lax.broadcasted_iota(jnp.int32, (1, LANES), 1) < HD
        zb = jnp.zeros((seq, LANES), BF)
        qm[0] = jnp.where(head0, q_ref[...], zb)
        qm[1] = jnp.where(head0, zb, q_ref[...])
        dom[0] = jnp.where(head0, do_ref[...], zb)
        dom[1] = jnp.where(head0, zb, do_ref[...])
        dq_ref[...] = jnp.zeros_like(dq_ref)

        def kloop(kj, _):
            krows = pl.ds(pl.multiple_of(kj * tq, tq), tq)
            kb = k_ref[krows, :]
            vb_t = v_ref[krows, :]
            zk = jnp.zeros_like(kb)
            km = (jnp.where(head0, kb, zk), jnp.where(head0, zk, kb))

            tile = lambda i: _tile_rows(i, tq)

            def scores(i, slot):
                for hh in range(2):
                    c0 = hh * HD
                    s_buf[slot, hh] = (_dot_nt(qm[hh, tile(i), :], kb)
                                       + (cq_ref[tile(i), c0:c0 + 1] - cr_ref[hh:hh + 1, krows]))
                    dp_buf[slot, hh] = _dot_nt(dom[hh, tile(i), :], vb_t)

            def products(slot, i, dk, dv):
                dq = jnp.zeros((tq, LANES), F32)
                for hh in range(2):
                    dsb = pd_buf[slot, hh, 1]
                    dv = dv + _dot_tn(pd_buf[slot, hh, 0], dom[hh, tile(i), :])
                    dk = dk + _dot_tn(dsb, qm[hh, tile(i), :])
                    dq = dq + _dot(dsb, km[hh])
                dq_ref[tile(i), :] += dq
                return dk, dv

            clamp = lambda i: jnp.clip(i, 0, nq - 1)

            def qstep(i, slot, carry):
                dk, dv, cs = carry
                dk, dv = products(1 - slot, clamp(i - 1), dk, dv)
                cs = list(cs)
                ic = clamp(i)
                first_visible = jnp.where(i < nq, (kj - i) * tq, 2 * tq)
                for r in range(tq // ATT_ROWS):
                    rows = slice(r * ATT_ROWS, (r + 1) * ATT_ROWS)
                    qrows = pl.ds(pl.multiple_of(ic * tq + r * ATT_ROWS, ATT_ROWS), ATT_ROWS)
                    visible = _causal_mask(ATT_ROWS, tq) >= first_visible - r * ATT_ROWS
                    for hh in range(2):
                        c0 = hh * HD
                        s = jnp.where(visible, s_buf[slot, hh, rows, :], -1e30)
                        p = jnp.exp(s - lse_ref[qrows, c0:c0 + 1])
                        ds = p * (dp_buf[slot, hh, rows, :] - dl_ref[qrows, c0:c0 + 1])
                        pd_buf[slot, hh, 0, rows, :] = p.astype(BF)
                        pd_buf[slot, hh, 1, rows, :] = ds.astype(BF)
                        cs[hh] = cs[hh] + jnp.sum(ds, axis=0, keepdims=True)
                scores(clamp(i + 1), 1 - slot)
                return dk, dv, tuple(cs)

            scores(kj, 0)
            pd_buf[1] = jnp.zeros((2, 2, tq, tq), BF)
            zero = jnp.zeros((tq, LANES), F32)
            zrow = jnp.zeros((1, tq), F32)
            trips = (nq - kj + 1) // 2
            dk, dv, cs = lax.fori_loop(
                0, trips, lambda t, cy: qstep(kj + 2 * t + 1, 1, qstep(kj + 2 * t, 0, cy)), (zero, zero, (zrow, zrow)))
            dk, dv = products(1, clamp(kj + 2 * trips - 1), dk, dv)
            dk_ref[krows, :] = dk
            dv_ref[krows, :] = dv
            dc_ref[0:1, krows] = -cs[0]
            dc_ref[1:2, krows] = -cs[1]
            return 0

        lax.fori_loop(0, nq, kloop, 0)

    blk = pl.BlockSpec((None, seq, LANES), lambda b, h: (b, 0, h))
    rowblk = pl.BlockSpec((None, None, 2, seq), lambda b, h: (b, h, 0, 0))
    return _host_call(
        body, (bsz, npair), [blk, blk, blk, blk, blk, blk, blk, rowblk], [blk, blk, blk, rowblk],
        [jax.ShapeDtypeStruct((bsz, seq, DA), F32)] * 3 + [jax.ShapeDtypeStruct((bsz, npair, 2, seq), F32)],
        [pltpu.VMEM((2, seq, LANES), BF), pltpu.VMEM((2, seq, LANES), BF), pltpu.VMEM((2, 2, tq, tq), F32),
         pltpu.VMEM((2, 2, tq, tq), F32), pltpu.VMEM((2, 2, 2, tq, tq), BF)],
        (qs, kn, vb, do, cq, lse, dl, crow), name, ex)


def _bwd_attn_post(pq, pk, dqs, dkn, pf, dcrow, gq, gk, bf, name, ex=None):
    bsz, seq, _ = pq.shape
    ts = 512
    nt = seq // ts

    def body(q_ref, k_ref, dq_ref, dk_ref, f_ref, dc_ref, gq_ref, gk_ref, bf_ref,
             dpq_ref, dpk_ref, dpf_ref, dgq_ref, dgk_ref, dbf_ref, carry):
        @pl.when((pl.program_id(0) == 0) & (pl.program_id(1) == 0))
        def _():
            dgq_ref[...] = jnp.zeros_like(dgq_ref)
            dgk_ref[...] = jnp.zeros_like(dgk_ref)
            dbf_ref[...] = jnp.zeros_like(dbf_ref)

        @pl.when(pl.program_id(1) == 0)
        def _():
            carry[...] = jnp.zeros_like(carry)

        for x_ref, dx_ref, g_ref, dp_ref, dg_ref, scale in ((q_ref, dq_ref, gq_ref, dpq_ref, dgq_ref, 1.0 / 8.0),
                                                            (k_ref, dk_ref, gk_ref, dpk_ref, dgk_ref, 1.0)):
            for jb in range(DA // LANES):
                sl = slice(jb * LANES, (jb + 1) * LANES)
                gb = g_ref[:, sl]
                xhat, r, _ = _head_rms(x_ref[:, sl], gb)
                dn = dx_ref[:, sl] * scale
                dg_ref[:, sl] += jnp.sum(dn * xhat, axis=0, keepdims=True)
                dxh = dn * gb
                dp_ref[:, sl] = r * (dxh - xhat * (_seg_sum64(dxh * xhat) * (1.0 / HD)))

        ui = lax.broadcasted_iota(jnp.int32, (ts, ts), 0)
        tj = lax.broadcasted_iota(jnp.int32, (ts, ts), 1)
        tri = (tj >= ui).astype(BF)
        dc = jnp.concatenate([dc_ref[...], jnp.zeros((LANES - NH, ts), F32)], axis=0)
        hi, mid, lo = _split3(dc)
        dlf = _dot_nt(tri, hi) + _dot_nt(tri, mid) + _dot_nt(tri, lo) + carry[...]
        carry[...] = dlf[0:1, :]
        xf = f_ref[...] + bf_ref[...]
        lane = lax.broadcasted_iota(jnp.int32, (ts, LANES), 1)
        dfl = jnp.where(lane < NH, dlf * _sigmoid(-xf), 0.0)
        dpf_ref[...] = dfl
        dbf_ref[...] += jnp.sum(dfl, axis=0, keepdims=True)

    rev = lambda i: nt - 1 - i
    tile = lambda n: pl.BlockSpec((None, ts, n), lambda b, i: (b, rev(i), 0))
    vec = lambda n: pl.BlockSpec((1, n), lambda b, i: (0, 0))
    return _host_call(
        body, (bsz, nt),
        [tile(DA), tile(DA), tile(DA), tile(DA), tile(LANES),
         pl.BlockSpec((None, NH, ts), lambda b, i: (b, 0, rev(i))), vec(DA), vec(DA), vec(LANES)],
        [tile(DA), tile(DA), tile(LANES), vec(DA), vec(DA), vec(LANES)],
        [jax.ShapeDtypeStruct((bsz, seq, DA), F32), jax.ShapeDtypeStruct((bsz, seq, DA), F32),
         jax.ShapeDtypeStruct((bsz, seq, LANES), F32), jax.ShapeDtypeStruct((1, DA), F32),
         jax.ShapeDtypeStruct((1, DA), F32), jax.ShapeDtypeStruct((1, LANES), F32)],
        [pltpu.VMEM((1, LANES), F32)], (pq, pk, dqs, dkn, pf, dcrow, gq, gk, bf), name, ex)


def _bwd_conv(pcf, psc, dycs, cf_dw, cf_dw_b, ln_g, ln_b, cf_pw, sc_dw, name):
    bsz, seq, _ = pcf.shape
    ts = 512
    nt, tile, cur, halo, full = _conv_specs(bsz, seq, ts, -1)

    def body(cf_ref, ha_ref, hg_ref, sc_ref, hc_ref, hx_ref, dy_ref, dw_ref, b_ref, lg_ref, lb_ref, pw_ref, sdw_ref,
             dcf_ref, dsc_ref, ddw_ref, db_ref, dlg_ref, dlb_ref, dpw_ref, dsdw_ref,
             ubuf, mbuf, dubuf, dcbuf):
        step = pl.program_id(1)
        first = tile(step) == 0

        @pl.when((pl.program_id(0) == 0) & (step == 0))
        def _():
            for r in (ddw_ref, db_ref, dlg_ref, dlb_ref, dpw_ref, dsdw_ref):
                r[...] = jnp.zeros_like(r)

        @pl.when(step == 0)
        def _():
            dubuf[ts:ts + HALO, :] = jnp.zeros((HALO, DC), F32)
            dcbuf[ts:ts + HALO, :] = jnp.zeros((HALO, DC), F32)

        a = cf_ref[:, 0:256]
        g = cf_ref[:, 256:512]
        z = cf_ref[:, 512:768]
        lng = lg_ref[...]
        u0, n, rstd, u2, s2, u3 = _conformer_fwd(a, g, ha_ref[...], hg_ref[...], first, ubuf, dw_ref,
                                                 b_ref[...], lng, lb_ref[...])
        u3b = u3.astype(BF)
        p = _dot(u3b, pw_ref[...])
        sz = _sigmoid(z)
        dy = dy_ref[:, 0:256]
        dcf_ref[:, 512:768] = dy * p * _dsilu(z, sz)
        dpb = (dy * (z * sz)).astype(BF)
        dpw_ref[...] += _dot_tn(u3b, dpb)
        du2 = _dot_nt(dpb, pw_ref[...]) * _dsilu(u2, s2)
        dlg_ref[...] += jnp.sum(du2 * n, axis=0, keepdims=True)
        dlb_ref[...] += jnp.sum(du2, axis=0, keepdims=True)
        dn = du2 * lng
        du1 = rstd * (dn - jnp.mean(dn, axis=-1, keepdims=True) - n * jnp.mean(dn * n, axis=-1, keepdims=True))
        db_ref[...] += jnp.sum(du1, axis=0, keepdims=True)
        dubuf[0:ts, :] = du1
        du0 = jnp.zeros((ts, DC), F32)
        for k in range(CFW):
            du0 = du0 + dw_ref[k:k + 1, :] * dubuf[pl.ds(CFW - 1 - k, ts), :]
            ddw_ref[k:k + 1, :] += jnp.sum(du1 * ubuf[pl.ds(HALO - (CFW - 1) + k, ts), :], axis=0, keepdims=True)
        dubuf[ts:ts + HALO, :] = du1[0:HALO, :]
        sg = _sigmoid(g)
        dcf_ref[:, 0:256] = du0 * sg
        dcf_ref[:, 256:512] = du0 * a * sg * (1.0 - sg)

        bb = sc_ref[:, 0:256]
        c = sc_ref[:, 256:512]
        xs = sc_ref[:, 512:768]
        zs = sc_ref[:, 768:1024]
        cv = _shortconv_fwd(c, xs, hc_ref[...], hx_ref[...], first, mbuf, sdw_ref)
        szs = _sigmoid(zs)
        dys = dy_ref[:, 256:512]
        gate = zs * szs
        dsc_ref[:, 0:256] = dys * cv * gate
        dsc_ref[:, 768:1024] = dys * bb * cv * _dsilu(zs, szs)
        dcv = dys * bb * gate
        dcbuf[0:ts, :] = dcv
        dm = jnp.zeros((ts, DC), F32)
        for k in range(SCW):
            dm = dm + sdw_ref[k:k + 1, :] * dcbuf[pl.ds(SCW - 1 - k, ts), :]
            dsdw_ref[k:k + 1, :] += jnp.sum(dcv * mbuf[pl.ds(HALO - (SCW - 1) + k, ts), :], axis=0, keepdims=True)
        dcbuf[ts:ts + HALO, :] = dcv[0:HALO, :]
        dsc_ref[:, 256:512] = dm * xs
        dsc_ref[:, 512:768] = dm * c

    outt = lambda n: pl.BlockSpec((None, ts, n), lambda b, i: (b, tile(i), 0))
    return pl.pallas_call(
        body, grid=(bsz, nt),
        in_specs=[cur(768), halo(0), halo(1), cur(1024), halo(1), halo(2), cur(512),
                  full(CFW, DC), full(1, DC), full(1, DC), full(1, DC), full(DC, DC), full(SCW, DC)],
        out_specs=[outt(768), outt(1024), full(CFW, DC), full(1, DC), full(1, DC), full(1, DC), full(DC, DC),
                   full(SCW, DC)],
        out_shape=[jax.ShapeDtypeStruct((bsz, seq, 768), F32), jax.ShapeDtypeStruct((bsz, seq, 1024), F32),
                   jax.ShapeDtypeStruct((CFW, DC), F32), jax.ShapeDtypeStruct((1, DC), F32),
                   jax.ShapeDtypeStruct((1, DC), F32), jax.ShapeDtypeStruct((1, DC), F32),
                   jax.ShapeDtypeStruct((DC, DC), F32), jax.ShapeDtypeStruct((SCW, DC), F32)],
        scratch_shapes=[pltpu.VMEM((HALO + ts, DC), F32), pltpu.VMEM((HALO + ts, DC), F32),
                        pltpu.VMEM((ts + HALO, DC), F32), pltpu.VMEM((ts + HALO, DC), F32)],
        compiler_params=_cparams(("arbitrary", "arbitrary")), name=name,
    )(pcf, pcf, pcf, psc, psc, psc, dycs, cf_dw, cf_dw_b, ln_g, ln_b, cf_pw, sc_dw)


def _bwd_inproj(x, g, dyres, w, dcf, dsc, dq, dk, dv, dz, df, name):
    t = x.shape[0]
    tm = 256
    pieces = ((C_CF, 768), (C_SC, 1024), (C_Q, DA), (C_K, DA), (C_V, DA), (C_Z, DA), (C_F, LANES))

    def body(x_ref, g_ref, dy_ref, w_hbm, dcf_ref, dsc_ref, dq_ref, dk_ref, dv_ref, dz_ref, df_ref,
             dx_ref, dg_ref, dw_hbm, w_vmem, dw_acc, sem):
        i = pl.program_id(0)

        @pl.when(i == 0)
        def _():
            cp = pltpu.make_async_copy(w_hbm, w_vmem, sem)
            cp.start()
            dw_acc[...] = jnp.zeros_like(dw_acc)
            dg_ref[...] = jnp.zeros_like(dg_ref)
            cp.wait()

        xt = x_ref[...]
        gg = g_ref[...]
        r = lax.rsqrt(jnp.mean(xt * xt, axis=-1, keepdims=True) + EPS)
        xhat = xt * r
        ht = (xhat * gg).astype(BF).T
        dh = jnp.zeros((tm, D), F32)
        for (c0, n), ref in zip(pieces, (dcf_ref, dsc_ref, dq_ref, dk_ref, dv_ref, dz_ref, df_ref)):
            for s0 in range(0, n, 512):
                s1 = min(s0 + 512, n)
                d = ref[:, s0:s1].astype(BF)
                dh = dh + _dot_nt(d, w_vmem[:, c0 + s0:c0 + s1])
                dw_acc[:, c0 + s0:c0 + s1] += _dot(ht, d)
        dg_ref[...] += jnp.sum(dh * xhat, axis=0, keepdims=True)
        dhg = dh * gg
        dx_ref[...] = dy_ref[...] + r * (dhg - xhat * jnp.mean(dhg * xhat, axis=-1, keepdims=True))

        @pl.when(i == pl.num_programs(0) - 1)
        def _():
            out = pltpu.make_async_copy(dw_acc, dw_hbm, sem)
            out.start()
            out.wait()

    row = lambda n: pl.BlockSpec((tm, n), lambda i: (i, 0))
    anyspec = pl.BlockSpec(memory_space=pl.ANY)
    return pl.pallas_call(
        body, grid=(t // tm,),
        in_specs=[row(D), pl.BlockSpec((1, D), lambda i: (0, 0)), row(D), anyspec,
                  row(768), row(1024), row(DA), row(DA), row(DA), row(DA), row(LANES)],
        out_specs=[row(D), pl.BlockSpec((1, D), lambda i: (0, 0)), anyspec],
        out_shape=[jax.ShapeDtypeStruct((t, D), F32), jax.ShapeDtypeStruct((1, D), F32),
                   jax.ShapeDtypeStruct((D, NP), F32)],
        scratch_shapes=[pltpu.VMEM((D, NP), BF), pltpu.VMEM((D, NP), F32), pltpu.SemaphoreType.DMA],
        compiler_params=_cparams(("arbitrary",)), name=name)(x, g, dyres, w, dcf, dsc, dq, dk, dv, dz, df)


def _adamw(w, g, m, v, tr, name):
    rows, cols = w.shape

    def body(w_ref, g_ref, m_ref, v_ref, d_ref, m2_ref, v2_ref):
        gt = g_ref[...]
        m2 = ADAM_B1 * m_ref[...] + (1.0 - ADAM_B1) * gt
        v2 = ADAM_B2 * v_ref[...] + (1.0 - ADAM_B2) * (gt * gt)
        m_hat = m2 / (1.0 - ADAM_B1 ** ADAM_STEP)
        v_hat = v2 / (1.0 - ADAM_B2 ** ADAM_STEP)
        d_ref[...] = -ADAM_LR * (m_hat / (jnp.sqrt(v_hat) + ADAM_EPS) + ADAM_WD * w_ref[...])
        m2_ref[...] = m2
        v2_ref[...] = v2

    blk = pl.BlockSpec((tr, cols), lambda i: (i, 0))
    return pl.pallas_call(
        body, grid=(rows // tr,), in_specs=[blk] * 4, out_specs=[blk] * 3,
        out_shape=[jax.ShapeDtypeStruct((rows, cols), F32)] * 3,
        compiler_params=_cparams(("arbitrary",)), name=name)(w, g, m, v)


def _place():
    x, y, c = lax.axis_index("x"), lax.axis_index("y"), lax.axis_index("c")
    chips = [(1 - x, y), (x, 1 - y), (1 - x, 1 - y)]
    return x, y, c, chips


def _halves(c, rows, align):
    rh = rows // 2
    return pl.ds(pl.multiple_of(c * rh, align), rh), pl.ds(pl.multiple_of((1 - c) * rh, align), rh)


ANY = pl.BlockSpec(memory_space=pl.ANY)


class _Exchange:
    def __init__(self, operands, out_shape, sems, start, wait):
        self.operands, self.out_shape, self.sems, self.start, self.wait = operands, out_shape, sems, start, wait


def _run_exchange(ex, name):
    n_in, n_out = len(ex.operands), len(ex.out_shape)

    def body(*refs):
        ins, outs, sems = refs[:n_in], refs[n_in:n_in + n_out], refs[n_in + n_out:]
        ex.start(ins, outs, sems)
        ex.wait(ins, outs, sems)

    return pl.pallas_call(body, in_specs=[ANY] * n_in, out_specs=[ANY] * n_out, out_shape=ex.out_shape,
                          scratch_shapes=ex.sems, name=name)(*ex.operands)


def _host_call(body, grid, in_specs, out_specs, out_shape, scratch, operands, name, ex=None):
    sem = ("arbitrary",) * len(grid)
    if ex is None:
        outs = pl.pallas_call(body, grid=grid, in_specs=in_specs, out_specs=out_specs, out_shape=out_shape,
                              scratch_shapes=scratch, compiler_params=_cparams(sem), name=name)(*operands)
        return outs, None
    n_in, n_out, n_scr = len(in_specs), len(out_specs), len(scratch)
    xi, xo = len(ex.operands), len(ex.out_shape)

    def hosted(*refs):
        ins, xins = refs[:n_in], refs[n_in:n_in + xi]
        o0 = n_in + xi
        outs, xouts = refs[o0:o0 + n_out], refs[o0 + n_out:o0 + n_out + xo]
        s0 = o0 + n_out + xo
        scr, xsems = refs[s0:s0 + n_scr], refs[s0 + n_scr:]
        first = pl.program_id(0) == 0
        last = pl.program_id(0) == pl.num_programs(0) - 1
        for d in range(1, len(grid)):
            first = first & (pl.program_id(d) == 0)
            last = last & (pl.program_id(d) == pl.num_programs(d) - 1)

        @pl.when(first)
        def _():
            ex.start(xins, xouts, xsems)

        body(*ins, *outs, *scr)

        @pl.when(last)
        def _():
            ex.wait(xins, xouts, xsems)

    res = pl.pallas_call(
        hosted, grid=grid, in_specs=list(in_specs) + [ANY] * xi, out_specs=list(out_specs) + [ANY] * xo,
        out_shape=list(out_shape) + list(ex.out_shape), scratch_shapes=list(scratch) + list(ex.sems),
        compiler_params=_cparams(sem), name=name)(*operands, *ex.operands)
    return res[:n_out], res[n_out:]


def _all_gather_chips(shards):
    n = len(shards)

    def copies(srcs, outs, sems):
        send_sems, recv_sems, local_sems = sems
        x, y, c, chips = _place()
        me = 2 * x + y
        halves = [_halves(c, s.shape[1], 16) for s in shards]
        local = [pltpu.make_async_copy(srcs[i], outs[i].at[:, me], local_sems.at[i]) for i in range(n)]

        def ici(i, j, chip, k):
            mine = halves[i][0]
            return pltpu.make_async_remote_copy(
                src_ref=srcs[i].at[:, mine, :], dst_ref=outs[i].at[:, k, mine, :], send_sem=send_sems.at[6 * i + j],
                recv_sem=recv_sems.at[6 * i + j], device_id=(*chip, c), device_id_type=MESH)

        def d2d(i, j, k, half):
            return pltpu.make_async_remote_copy(
                src_ref=outs[i].at[:, k, half, :], dst_ref=outs[i].at[:, k, half, :], send_sem=send_sems.at[6 * i + 3 + j],
                recv_sem=recv_sems.at[6 * i + 3 + j], device_id=(x, y, 1 - c), device_id_type=MESH)

        return me, chips, halves, local, ici, d2d

    def start(srcs, outs, sems):
        me, chips, _, local, ici, _ = copies(srcs, outs, sems)
        for cp in local:
            cp.start()
        for i in range(n):
            for j, chip in enumerate(chips):
                ici(i, j, chip, me).start()

    def wait(srcs, outs, sems):
        me, chips, halves, local, ici, d2d = copies(srcs, outs, sems)
        passed = []
        for i in range(n):
            for j, (px, py) in enumerate(chips):
                k = 2 * px + py
                ici(i, j, (px, py), k).wait_recv()
                fw = d2d(i, j, k, halves[i][0])
                fw.start()
                passed.append(fw)
        for i in range(n):
            for j, (px, py) in enumerate(chips):
                d2d(i, j, 2 * px + py, halves[i][1]).wait_recv()
        for i in range(n):
            for j, chip in enumerate(chips):
                ici(i, j, chip, me).wait_send()
        for cp in passed:
            cp.wait_send()
        for cp in local:
            cp.wait()

    return _Exchange(
        list(shards), [jax.ShapeDtypeStruct((s.shape[0], 4) + s.shape[1:], s.dtype) for s in shards],
        [pltpu.SemaphoreType.DMA((6 * n,)), pltpu.SemaphoreType.DMA((6 * n,)), pltpu.SemaphoreType.DMA((n,))],
        start, wait)


def _rs_swap_halves(gs):
    n = len(gs)

    def copies(srcs, gots, sems):
        send_sems, recv_sems = sems
        x, y, c, _ = _place()
        return [pltpu.make_async_remote_copy(
            src_ref=srcs[i].at[:, :, _halves(c, gs[i].shape[2], 8)[1], :], dst_ref=gots[i], send_sem=send_sems.at[i],
            recv_sem=recv_sems.at[i], device_id=(x, y, 1 - c), device_id_type=MESH) for i in range(n)]

    def start(srcs, gots, sems):
        for cp in copies(srcs, gots, sems):
            cp.start()

    def wait(srcs, gots, sems):
        for cp in copies(srcs, gots, sems):
            cp.wait()

    return _Exchange(
        list(gs), [jax.ShapeDtypeStruct(g.shape[:2] + (g.shape[2] // 2, g.shape[3]), g.dtype) for g in gs],
        [pltpu.SemaphoreType.DMA((n,)), pltpu.SemaphoreType.DMA((n,))], start, wait)


def _rs_add_pair(g, got, cidx, tb, out_dtype, name):
    nl, _, rh, cols = got.shape
    nb = rh // tb

    def body(c_ref, g_ref, o_ref, s_ref):
        s_ref[...] = (g_ref[...] + o_ref[...]).astype(out_dtype)

    blk = lambda half: pl.BlockSpec((None, None, tb, cols), (lambda l, k, i, c: (l, k, c[0] * nb + i, 0)) if half
                                    else (lambda l, k, i, c: (l, k, i, 0)))
    return pl.pallas_call(
        body,
        grid_spec=pltpu.PrefetchScalarGridSpec(num_scalar_prefetch=1, grid=(nl, 4, nb),
                                               in_specs=[blk(True), blk(False)], out_specs=blk(False)),
        out_shape=jax.ShapeDtypeStruct(got.shape, out_dtype),
        compiler_params=_cparams(("arbitrary", "arbitrary", "arbitrary")), name=name)(cidx, g, got)


def _rs_exchange_chips(pairs):
    n = len(pairs)

    def copies(srcs, gots, sems):
        send_sems, recv_sems, local_sems = sems
        x, y, c, chips = _place()
        me = 2 * x + y
        local = [pltpu.make_async_copy(srcs[i].at[:, me], gots[i].at[:, me], local_sems.at[i]) for i in range(n)]

        def ici(i, j, chip, frm, to):
            return pltpu.make_async_remote_copy(
                src_ref=srcs[i].at[:, to], dst_ref=gots[i].at[:, frm], send_sem=send_sems.at[3 * i + j],
                recv_sem=recv_sems.at[3 * i + j], device_id=(*chip, c), device_id_type=MESH)

        sent = [ici(i, j, (px, py), me, 2 * px + py) for i in range(n) for j, (px, py) in enumerate(chips)]
        recvd = [ici(i, j, (px, py), 2 * px + py, me) for i in range(n) for j, (px, py) in enumerate(chips)]
        return local, sent, recvd

    def start(srcs, gots, sems):
        local, sent, _ = copies(srcs, gots, sems)
        for cp in local + sent:
            cp.start()

    def wait(srcs, gots, sems):
        local, sent, recvd = copies(srcs, gots, sems)
        for cp in recvd:
            cp.wait_recv()
        for cp in sent:
            cp.wait_send()
        for cp in local:
            cp.wait()

    return _Exchange(
        list(pairs), [jax.ShapeDtypeStruct(p.shape, p.dtype) for p in pairs],
        [pltpu.SemaphoreType.DMA((3 * n,)), pltpu.SemaphoreType.DMA((3 * n,)), pltpu.SemaphoreType.DMA((n,))],
        start, wait)


def _rs_add_chips(got, tb, name):
    nl, _, rh, cols = got.shape

    def body(g_ref, s_ref):
        s_ref[...] = ((g_ref[0].astype(F32) + g_ref[1].astype(F32)) + g_ref[2].astype(F32)) + g_ref[3].astype(F32)

    return pl.pallas_call(
        body, grid=(nl, rh // tb),
        in_specs=[pl.BlockSpec((None, 4, tb, cols), lambda l, i: (l, 0, i, 0))],
        out_specs=pl.BlockSpec((None, tb, cols), lambda l, i: (l, i, 0)),
        out_shape=jax.ShapeDtypeStruct((nl, rh, cols), F32),
        compiler_params=_cparams(("arbitrary", "arbitrary")), name=name)(got)


def _rs_join_halves(totals):
    n = len(totals)

    def copies(srcs, outs, sems):
        send_sems, recv_sems, local_sems = sems
        x, y, c, _ = _place()
        local, send, recv = [], [], []
        for i in range(n):
            mine, other = _halves(c, 2 * totals[i].shape[1], 8)
            local.append(pltpu.make_async_copy(srcs[i], outs[i].at[:, mine, :], local_sems.at[i]))
            for lst, rows in ((send, mine), (recv, other)):
                lst.append(pltpu.make_async_remote_copy(
                    src_ref=srcs[i], dst_ref=outs[i].at[:, rows, :], send_sem=send_sems.at[i], recv_sem=recv_sems.at[i],
                    device_id=(x, y, 1 - c), device_id_type=MESH))
        return local, send, recv

    def start(srcs, outs, sems):
        local, send, _ = copies(srcs, outs, sems)
        for cp in local + send:
            cp.start()

    def wait(srcs, outs, sems):
        local, send, recv = copies(srcs, outs, sems)
        for cp in recv:
            cp.wait_recv()
        for cp in send:
            cp.wait_send()
        for cp in local:
            cp.wait()

    return _Exchange(
        list(totals), [jax.ShapeDtypeStruct((t.shape[0], 2 * t.shape[1], t.shape[2]), F32) for t in totals],
        [pltpu.SemaphoreType.DMA((n,)), pltpu.SemaphoreType.DMA((n,)), pltpu.SemaphoreType.DMA((n,))], start, wait)


def _rows(a):
    return a.reshape(-1, LANES)


def _pad_rows(a, mult):
    r = (-a.shape[0]) % mult
    return a if r == 0 else jnp.concatenate([a, jnp.zeros((r, a.shape[1]), a.dtype)], axis=0)


def kernel(x, norm_g, w_in, b_f, cf_dw, cf_dw_b, cf_ln_g, cf_ln_b, cf_pw, sc_dw, q_norm_g, k_norm_g, w_out, loss_target, m_norm_g, m_w_in, m_b_f, m_cf_dw, m_cf_dw_b, m_cf_ln_g, m_cf_ln_b, m_cf_pw, m_sc_dw, m_q_norm_g, m_k_norm_g, m_w_out, v_norm_g, v_w_in, v_b_f, v_cf_dw, v_cf_dw_b, v_cf_ln_g, v_cf_ln_b, v_cf_pw, v_sc_dw, v_q_norm_g, v_k_norm_g, v_w_out):
    bsz, seq, _ = x.shape
    t = bsz * seq

    taps = jnp.concatenate([cf_dw.reshape(-1), sc_dw.reshape(-1)])
    taps_bf = lax.bitcast_convert_type(taps, jnp.bfloat16).reshape(-1, LANES)
    n_pw, n_taps = 2 * 64 * DC // LANES, taps_bf.shape[0]
    small_w = _pad_rows(jnp.concatenate([_rows(cf_pw.astype(BF)), taps_bf], axis=0), 32)[None]
    win_b, wout_b = w_in.astype(BF), w_out.astype(BF)
    win0_g, small_g = _run_exchange(_all_gather_chips([win_b[0:1], small_w]), "all_gather_first")
    gather_rest = _all_gather_chips([win_b[1:DEPTH], wout_b])
    full_w_in = lambda g: jnp.pad(jnp.concatenate([g[k] for k in range(4)], axis=-1), ((0, 0), (0, NP - N_IN)))
    w_in_full = [full_w_in(win0_g[0])] + [None] * (DEPTH - 1)
    w_out_full = None
    cf_pw_full = jnp.concatenate([small_g[0, k, 0:n_pw].reshape(DEPTH, 64, DC) for k in range(4)], axis=1)
    taps_all = [lax.bitcast_convert_type(small_g[0, k, n_pw:n_pw + n_taps].reshape(-1, 2), F32) for k in range(4)]
    n_cfdw = DEPTH * CFW * 64
    cf_dw_full = jnp.concatenate([tk[:n_cfdw].reshape(DEPTH, CFW, 64) for tk in taps_all], axis=-1)
    sc_dw_full = jnp.concatenate([tk[n_cfdw:].reshape(DEPTH, SCW, 64) for tk in taps_all], axis=-1)

    bf_pad = jnp.pad(b_f, ((0, 0), (0, LANES - NH)))

    xs = [x.reshape(t, D)]
    saved = []
    dy = loss_part = None
    for l in range(DEPTH):
        xl = xs[-1]
        gq, gk = q_norm_g[l].reshape(1, DA), k_norm_g[l].reshape(1, DA)
        pcf, psc, pq, pk, vb, pz, pf = _fwd_inproj(xl, norm_g[l][None], w_in_full[l], f"fwd_inproj_{l}")
        b3 = lambda a: a.reshape(bsz, seq, a.shape[-1])
        ycs = _fwd_conv(b3(pcf), b3(psc), cf_dw_full[l], cf_dw_b[l][None], cf_ln_g[l][None], cf_ln_b[l][None],
                        cf_pw_full[l], sc_dw_full[l], f"fwd_conv_{l}")
        qs, kn, cq, crow8 = _fwd_attn_prep(b3(pq), b3(pk), b3(pf), gq, gk, bf_pad[l][None], f"fwd_attn_prep_{l}")
        crow = crow8.reshape(bsz, NH // 2, 2, seq)
        (o, ohp, yatt, lse), landed = _fwd_attn(qs, kn, b3(vb), cq, crow, b3(pz), f"fwd_attn_{l}",
                                                gather_rest if l == 0 else None)
        if l == 0:
            win_rest_g, wout_g = landed
            w_in_full[1:] = [full_w_in(win_rest_g[i]) for i in range(DEPTH - 1)]
            w_out_full = wout_g.reshape(DEPTH, D, D)
        ycs2, yatt2 = ycs.reshape(t, 512), yatt.reshape(t, DA)
        if l + 1 < DEPTH:
            xs.append(_fwd_outproj(xl, ycs2, yatt2, w_out_full[l], f"fwd_outproj_{l}"))
        else:
            dy, loss_part = _fwd_outproj_loss(xl, ycs2, yatt2, w_out_full[l], loss_target.reshape(t, D),
                                              f"fwd_outproj_loss_{l}")
        saved.append((pcf, psc, pq, pk, vb, pz, pf, ycs2, yatt2, o, ohp, qs, kn, cq, crow, lse))

    grads = [None] * DEPTH
    reduced = [None] * DEPTH
    cidx = lax.axis_index("c").astype(jnp.int32).reshape(1)
    big_tiles = [256, 128]

    def big_packs(gl):
        return [jnp.stack([gl["w_in"][:, NSH * k:NSH * (k + 1)] for k in range(4)])[None],
                gl["w_out"].reshape(1, 4, 256, D)]

    def add_pairs(packs, gots, tiles, dtypes, tag):
        return [_rs_add_pair(p, got, cidx, tb, dt, f"rs_add_pair_{tag}_{i}")
                for i, (p, got, tb, dt) in enumerate(zip(packs, gots, tiles, dtypes))]

    def add_chips(gots2, tiles, tag):
        return [_rs_add_chips(got, tb, f"rs_add_chips_{tag}_{i}") for i, (got, tb) in enumerate(zip(gots2, tiles))]

    for l in reversed(range(DEPTH)):
        pcf, psc, pq, pk, vb, pz, pf, ycs2, yatt2, o, ohp, qs, kn, cq, crow, lse = saved[l]
        gq, gk = q_norm_g[l].reshape(1, DA), k_norm_g[l].reshape(1, DA)
        b3 = lambda a: a.reshape(bsz, seq, a.shape[-1])
        f2 = lambda a: a.reshape(t, a.shape[-1])
        packs = big_packs(grads[l + 1]) if l + 1 < DEPTH else None
        (dycs, do, dz, dl, d_wout), gots = _bwd_outproj(
            dy, ycs2, yatt2, f2(o), f2(ohp), pz, w_out_full[l], f"bwd_outproj_{l}",
            _rs_swap_halves(packs) if packs else None)
        pairs = add_pairs(packs, gots, big_tiles, [BF, BF], l + 1) if packs else None
        (dqs, dkn, dv, dcrow), gots2 = _bwd_attn(qs, kn, b3(vb), b3(do), cq, lse, b3(dl), crow, f"bwd_attn_{l}",
                                                 _rs_exchange_chips(pairs) if packs else None)
        totals = add_chips(gots2, big_tiles, l + 1) if packs else None
        (dpq, dpk, dpf, d_gq, d_gk, d_bf), joined = _bwd_attn_post(
            b3(pq), b3(pk), dqs, dkn, b3(pf), dcrow.reshape(bsz, NH, seq), gq, gk, bf_pad[l][None],
            f"bwd_attn_post_{l}", _rs_join_halves(totals) if packs else None)
        if packs:
            reduced[l + 1] = joined
        dcf, dsc, d_cfdw, d_cfb, d_lng, d_lnb, d_pw, d_scdw = _bwd_conv(
            b3(pcf), b3(psc), b3(dycs), cf_dw_full[l], cf_dw_b[l][None], cf_ln_g[l][None], cf_ln_b[l][None],
            cf_pw_full[l], sc_dw_full[l], f"bwd_conv_{l}")
        dy, d_ng, d_win = _bwd_inproj(xs[l], norm_g[l][None], dy, w_in_full[l], f2(dcf), f2(dsc), f2(dpq), f2(dpk),
                                      f2(dv), dz, f2(dpf), f"bwd_inproj_{l}")
        grads[l] = dict(norm_g=d_ng[0], w_in=d_win, b_f=d_bf[0, :NH], cf_dw=d_cfdw, cf_dw_b=d_cfb[0],
                        cf_ln_g=d_lng[0], cf_ln_b=d_lnb[0], cf_pw=d_pw, sc_dw=d_scdw,
                        q_norm_g=d_gq.reshape(NH, HD), k_norm_g=d_gk.reshape(NH, HD), w_out=d_wout)
    grad_x = dy.reshape(bsz, seq, D)
    gw = {n: jnp.stack([grads[l][n] for l in range(DEPTH)]) for n in grads[0] if n not in ("w_in", "w_out")}

    rep_names = ("norm_g", "b_f", "cf_dw_b", "cf_ln_g", "cf_ln_b", "q_norm_g", "k_norm_g")
    rep = jnp.concatenate([jnp.pad(gw[n].reshape(-1), (0, (-gw[n].size) % LANES)) for n in rep_names]
                          + [jnp.pad(loss_part.reshape(-1), (0, LANES - 1))]).reshape(-1, LANES)
    blocks = []
    for k in range(4):
        small = jnp.concatenate([gw["cf_dw"][:, :, 64 * k:64 * (k + 1)].reshape(-1),
                                 gw["sc_dw"][:, :, 64 * k:64 * (k + 1)].reshape(-1)]).reshape(-1, LANES)
        blocks.append(_pad_rows(jnp.concatenate([_rows(gw["cf_pw"][:, 64 * k:64 * (k + 1), :]), small, rep], axis=0), 16))
    g_small = jnp.stack(blocks)[None]
    packs = big_packs(grads[0]) + [g_small]
    tiles = big_tiles + [g_small.shape[2] // 2]
    gots = _run_exchange(_rs_swap_halves(packs), "rs_swap_halves_last")
    pairs = add_pairs(packs, gots, tiles, [BF, BF, F32], 0)
    gots2 = _run_exchange(_rs_exchange_chips(pairs), "rs_exchange_chips_last")
    red_win0, red_wout0, red_small = _run_exchange(_rs_join_halves(add_chips(gots2, tiles, 0)), "rs_join_halves_last")
    reduced[0] = (red_win0, red_wout0)

    n_small = (DEPTH * CFW * 64 + DEPTH * SCW * 64) // LANES
    red = red_small[0]
    pos = [0]

    def take(nrows):
        pos[0] += nrows
        return red[pos[0] - nrows:pos[0]]

    g = {}
    g["w_in"] = jnp.concatenate([reduced[l][0] for l in range(DEPTH)], axis=0)
    g["w_out"] = jnp.concatenate([reduced[l][1] for l in range(DEPTH)], axis=0)
    g["cf_pw"] = take(n_pw).reshape(DEPTH, 64, DC)
    small = take(n_small).reshape(-1)
    g["cf_dw"] = small[:n_cfdw].reshape(DEPTH, CFW, 64)
    g["sc_dw"] = small[n_cfdw:].reshape(DEPTH, SCW, 64)
    shapes = dict(norm_g=norm_g.shape, b_f=b_f.shape, cf_dw_b=cf_dw_b.shape, cf_ln_g=cf_ln_g.shape,
                  cf_ln_b=cf_ln_b.shape, q_norm_g=q_norm_g.shape, k_norm_g=k_norm_g.shape)
    for n in rep_names:
        size = 1
        for s in shapes[n]:
            size *= s
        g[n] = take(-(-size // LANES)).reshape(-1)[:size].reshape(shapes[n])
    loss = take(1)[0, 0]

    order = ("norm_g", "w_in", "b_f", "cf_dw", "cf_dw_b", "cf_ln_g", "cf_ln_b", "cf_pw", "sc_dw", "q_norm_g",
             "k_norm_g", "w_out")
    weights = dict(norm_g=norm_g, w_in=w_in, b_f=b_f, cf_dw=cf_dw, cf_dw_b=cf_dw_b, cf_ln_g=cf_ln_g, cf_ln_b=cf_ln_b,
                   cf_pw=cf_pw, sc_dw=sc_dw, q_norm_g=q_norm_g, k_norm_g=k_norm_g, w_out=w_out)
    ms = dict(norm_g=m_norm_g, w_in=m_w_in, b_f=m_b_f, cf_dw=m_cf_dw, cf_dw_b=m_cf_dw_b, cf_ln_g=m_cf_ln_g,
              cf_ln_b=m_cf_ln_b, cf_pw=m_cf_pw, sc_dw=m_sc_dw, q_norm_g=m_q_norm_g, k_norm_g=m_k_norm_g, w_out=m_w_out)
    vs = dict(norm_g=v_norm_g, w_in=v_w_in, b_f=v_b_f, cf_dw=v_cf_dw, cf_dw_b=v_cf_dw_b, cf_ln_g=v_cf_ln_g,
              cf_ln_b=v_cf_ln_b, cf_pw=v_cf_pw, sc_dw=v_sc_dw, q_norm_g=v_q_norm_g, k_norm_g=v_k_norm_g, w_out=v_w_out)
    delta, new_m, new_v = {}, {}, {}
    two_d = lambda a: a.reshape(-1, a.shape[-1])
    for n in ("w_in", "w_out"):
        outs = _adamw(two_d(weights[n]), two_d(g[n]), two_d(ms[n]), two_d(vs[n]), 256, f"adamw_{n}")
        delta[n], new_m[n], new_v[n] = (a.reshape(weights[n].shape) for a in outs)
    small_names = [n for n in order if n not in ("w_in", "w_out")]
    flat = lambda d: _pad_rows(jnp.pad(jnp.concatenate([d[n].reshape(-1) for n in small_names]),
                                       (0, (-sum(weights[n].size for n in small_names)) % LANES)).reshape(-1, LANES), 8)
    fw = flat(weights)
    outs = _adamw(fw, flat(g), flat(ms), flat(vs), fw.shape[0], "adamw_small")
    off = 0
    for n in small_names:
        size = weights[n].size
        for dst, src in zip((delta, new_m, new_v), outs):
            dst[n] = src.reshape(-1)[off:off + size].reshape(weights[n].shape)
        off += size

    return (loss, grad_x, *[g[n] for n in order], *[delta[n] for n in order], *[new_m[n] for n in order],
            *[new_v[n] for n in order])
```

```python
import jax
import jax.numpy as jnp
from jax import lax
from jax.experimental import pallas as pl
from jax.experimental.pallas import tpu as pltpu

F32 = jnp.float32
BF = jnp.bfloat16
MESH = pl.DeviceIdType.MESH

DEPTH = 2
D = 1024
DC = 256
DA = 512
NH = 8
HD = 64
CFW = 31
SCW = 3
N_IN = 3848
NP = 3968
NSH = N_IN // 4
HALO = 32
ATT_ROWS = 32
FWD_ROWS = 256
SUBLANES = 8
CONV_ROWS = 32
TAP_GROUP = 8
EPS = 1e-6
LANES = 128
VMEM_LIMIT = 56 * 1024 * 1024

C_CF, C_SC, C_Q, C_K, C_V, C_Z, C_F = 0, 768, 1792, 2304, 2816, 3328, 3840

ADAM_LR = 0.001
ADAM_B1 = 0.9
ADAM_B2 = 0.999
ADAM_EPS = 1e-08
ADAM_WD = 0.01
ADAM_STEP = 10


def _cparams(sem=None):
    return pltpu.CompilerParams(dimension_semantics=sem, vmem_limit_bytes=VMEM_LIMIT)


def _dot(a, b):
    return jnp.dot(a, b, preferred_element_type=F32)


def _dot_nt(a, b):
    return lax.dot_general(a, b, (((1,), (1,)), ((), ())), preferred_element_type=F32)


def _dot_tn(a, b):
    return lax.dot_general(a, b, (((0,), (0,)), ((), ())), preferred_element_type=F32)


def _split3(x):
    hi = x.astype(BF)
    r1 = x - hi.astype(F32)
    mid = r1.astype(BF)
    lo = (r1 - mid.astype(F32)).astype(BF)
    return hi, mid, lo


def _dot_exact(a_bf, x):
    hi, mid, lo = _split3(x)
    return _dot(a_bf, hi) + _dot(a_bf, mid) + _dot(a_bf, lo)


def _sigmoid(x):
    return 1.0 / (1.0 + jnp.exp(-x))


def _seg_sum64(x):
    i = lax.broadcasted_iota(jnp.int32, (LANES, LANES), 0)
    j = lax.broadcasted_iota(jnp.int32, (LANES, LANES), 1)
    g = ((i >= HD) == (j >= HD)).astype(BF)
    hi, mid, lo = _split3(x)
    return _dot(hi, g) + _dot(mid, g) + _dot(lo, g)


def _fwd_inproj(x, g, w, name):
    t = x.shape[0]
    tm = 256

    def body(x_ref, g_ref, w_ref, cf_ref, sc_ref, q_ref, k_ref, v_ref, z_ref, f_ref):
        xt = x_ref[...]
        r = lax.rsqrt(jnp.mean(xt * xt, axis=-1, keepdims=True) + EPS)
        h = ((xt * r) * g_ref[...]).astype(BF)
        cf_ref[:, 0:512] = _dot(h, w_ref[:, 0:512])
        cf_ref[:, 512:768] = _dot(h, w_ref[:, 512:768])
        sc_ref[:, 0:512] = _dot(h, w_ref[:, C_SC:C_SC + 512])
        sc_ref[:, 512:1024] = _dot(h, w_ref[:, C_SC + 512:C_Q])
        q_ref[...] = _dot(h, w_ref[:, C_Q:C_K])
        k_ref[...] = _dot(h, w_ref[:, C_K:C_V])
        v_ref[...] = _dot(h, w_ref[:, C_V:C_Z]).astype(BF)
        z_ref[...] = _dot(h, w_ref[:, C_Z:C_F])
        f_ref[...] = _dot(h, w_ref[:, C_F:NP])

    row = lambda n: pl.BlockSpec((tm, n), lambda i: (i, 0))
    return pl.pallas_call(
        body, grid=(t // tm,),
        in_specs=[row(D), pl.BlockSpec((1, D), lambda i: (0, 0)), pl.BlockSpec((D, NP), lambda i: (0, 0))],
        out_specs=[row(768), row(1024), row(DA), row(DA), row(DA), row(DA), row(LANES)],
        out_shape=[jax.ShapeDtypeStruct((t, 768), F32), jax.ShapeDtypeStruct((t, 1024), F32),
                   jax.ShapeDtypeStruct((t, DA), F32), jax.ShapeDtypeStruct((t, DA), F32),
                   jax.ShapeDtypeStruct((t, DA), BF), jax.ShapeDtypeStruct((t, DA), F32),
                   jax.ShapeDtypeStruct((t, LANES), F32)],
        compiler_params=_cparams(("arbitrary",)), name=name)(x, g, w)


def _shift_copies(buf, sbuf, shifts):
    n = buf.shape[0]
    for sh in shifts:
        rows = n if sh == 0 else n - SUBLANES
        sbuf[sh, 0:rows, :] = buf[pl.ds(sh, rows), :]


def _tap_rows(sbuf, off, r0):
    sh = off % SUBLANES
    return sbuf[sh, pl.ds(pl.multiple_of(r0 + (off - sh), SUBLANES), CONV_ROWS), :]


def _tap_conv(sbuf, w_ref, offsets, dst, ts, bias):
    def chunk(c, _):
        r0 = pl.multiple_of(c * CONV_ROWS, CONV_ROWS)
        acc = jnp.zeros((CONV_ROWS, DC), F32) + bias
        for k, off in enumerate(offsets):
            acc = acc + w_ref[k:k + 1, :] * _tap_rows(sbuf, off, r0)
        dst[pl.ds(r0, CONV_ROWS), :] = acc
        return 0

    lax.fori_loop(0, ts // CONV_ROWS, chunk, 0)


def _tap_sums(a_buf, sbuf, offsets, ts):
    outs = []
    for g0 in range(0, len(offsets), TAP_GROUP):
        group = offsets[g0:g0 + TAP_GROUP]

        def chunk(c, accs, group=group):
            r0 = pl.multiple_of(c * CONV_ROWS, CONV_ROWS)
            a = a_buf[pl.ds(r0, CONV_ROWS), :]
            return tuple(acc + (a * _tap_rows(sbuf, off, r0)).reshape(CONV_ROWS // SUBLANES, SUBLANES, DC).sum(axis=0)
                         for acc, off in zip(accs, group))

        accs = lax.fori_loop(0, ts // CONV_ROWS, chunk, tuple(jnp.zeros((SUBLANES, DC), F32) for _ in group))
        outs += [jnp.sum(acc, axis=0, keepdims=True) for acc in accs]
    return outs


CF_TAPS = tuple(HALO - (CFW - 1) + k for k in range(CFW))
SC_TAPS = tuple(HALO - (SCW - 1) + k for k in range(SCW))
CF_TAPS_T = tuple(CFW - 1 - k for k in range(CFW))
SC_TAPS_T = tuple(SCW - 1 - k for k in range(SCW))


def _conformer_fwd(a, g, ha, hg, first, ubuf, usbuf, u1buf, dw_ref, bias, lng, lnb):
    ts = a.shape[0]
    u0 = a * _sigmoid(g)
    ubuf[0:HALO, :] = jnp.where(first, 0.0, ha * _sigmoid(hg))
    ubuf[HALO:HALO + ts, :] = u0
    _shift_copies(ubuf, usbuf, range(SUBLANES))
    _tap_conv(usbuf, dw_ref, CF_TAPS, u1buf, ts, bias)
    u1 = u1buf[...]
    mu = jnp.mean(u1, axis=-1, keepdims=True)
    xc = u1 - mu
    rstd = lax.rsqrt(jnp.mean(xc * xc, axis=-1, keepdims=True) + EPS)
    n = xc * rstd
    u2 = n * lng + lnb
    s2 = _sigmoid(u2)
    u3 = u2 * s2
    return u0, n, rstd, u2, s2, u3


def _shortconv_fwd(c, xs, hc, hx, first, mbuf, msbuf, cvbuf, dw_ref):
    ts = c.shape[0]
    mbuf[0:HALO, :] = jnp.where(first, 0.0, hc * hx)
    mbuf[HALO:HALO + ts, :] = c * xs
    _shift_copies(mbuf, msbuf, sorted({off % SUBLANES for off in SC_TAPS}))
    _tap_conv(msbuf, dw_ref, SC_TAPS, cvbuf, ts, 0.0)
    return cvbuf[...]


def _conv_specs(bsz, seq, ts, order):
    nt = seq // ts
    tile = (lambda i: i) if order > 0 else (lambda i: nt - 1 - i)
    hrow = lambda i: jnp.maximum(tile(i) * (ts // HALO) - 1, 0)
    cur = lambda n: pl.BlockSpec((None, ts, n), lambda b, i: (b, tile(i), 0))
    halo = lambda j: pl.BlockSpec((None, HALO, DC), lambda b, i: (b, hrow(i), j))
    full = lambda r, c: pl.BlockSpec((r, c), lambda b, i: (0, 0))
    return nt, tile, cur, halo, full


def _fwd_conv(pcf, psc, cf_dw, cf_dw_b, ln_g, ln_b, cf_pw, sc_dw, name):
    bsz, seq, _ = pcf.shape
    ts = 512
    nt, tile, cur, halo, full = _conv_specs(bsz, seq, ts, +1)

    def body(cf_ref, ha_ref, hg_ref, sc_ref, hc_ref, hx_ref, dw_ref, b_ref, lg_ref, lb_ref, pw_ref, sdw_ref,
             y_ref, ubuf, sbuf, obuf):
        first = pl.program_id(1) == 0
        _, _, _, _, _, u3 = _conformer_fwd(cf_ref[:, 0:256], cf_ref[:, 256:512], ha_ref[...], hg_ref[...], first,
                                           ubuf, sbuf, obuf, dw_ref, b_ref[...], lg_ref[...], lb_ref[...])
        z = cf_ref[:, 512:768]
        y_ref[:, 0:256] = _dot(u3.astype(BF), pw_ref[...]) * (z * _sigmoid(z))
        cv = _shortconv_fwd(sc_ref[:, 256:512], sc_ref[:, 512:768], hc_ref[...], hx_ref[...], first,
                            ubuf, sbuf, obuf, sdw_ref)
        zs = sc_ref[:, 768:1024]
        y_ref[:, 256:512] = sc_ref[:, 0:256] * cv * (zs * _sigmoid(zs))

    return pl.pallas_call(
        body, grid=(bsz, nt),
        in_specs=[cur(768), halo(0), halo(1), cur(1024), halo(1), halo(2),
                  full(CFW, DC), full(1, DC), full(1, DC), full(1, DC), full(DC, DC), full(SCW, DC)],
        out_specs=pl.BlockSpec((None, ts, 512), lambda b, i: (b, i, 0)),
        out_shape=jax.ShapeDtypeStruct((bsz, seq, 512), F32),
        scratch_shapes=[pltpu.VMEM((HALO + ts, DC), F32), pltpu.VMEM((SUBLANES, HALO + ts, DC), F32),
                        pltpu.VMEM((ts, DC), F32)],
        compiler_params=_cparams(("arbitrary", "arbitrary")), name=name,
    )(pcf, pcf, pcf, psc, psc, psc, cf_dw, cf_dw_b, ln_g, ln_b, cf_pw, sc_dw)


def _head_rms(xb, gb):
    ms = _seg_sum64(xb * xb) * (1.0 / HD)
    r = lax.rsqrt(ms + EPS)
    xhat = xb * r
    return xhat, r, xhat * gb


def _fwd_attn_prep(pq, pk, pf, gq, gk, bf, name):
    bsz, seq, _ = pq.shape
    ts = 512
    nt = seq // ts

    def body(q_ref, k_ref, f_ref, gq_ref, gk_ref, bf_ref, qs_ref, kn_ref, cq_ref, crow_ref, carry):
        @pl.when(pl.program_id(1) == 0)
        def _():
            carry[...] = jnp.zeros_like(carry)

        for jb in range(DA // LANES):
            sl = slice(jb * LANES, (jb + 1) * LANES)
            _, _, qn = _head_rms(q_ref[:, sl], gq_ref[:, sl])
            qs_ref[:, sl] = (qn * (1.0 / 8.0)).astype(BF)
            _, _, kn = _head_rms(k_ref[:, sl], gk_ref[:, sl])
            kn_ref[:, sl] = kn.astype(BF)

        xf = f_ref[...] + bf_ref[...]
        lf = jnp.minimum(xf, 0.0) - jnp.log(1.0 + jnp.exp(-jnp.abs(xf)))
        ti = lax.broadcasted_iota(jnp.int32, (ts, ts), 0)
        si = lax.broadcasted_iota(jnp.int32, (ts, ts), 1)
        c = _dot_exact((si <= ti).astype(BF), lf) + carry[...]
        carry[...] = c[ts - 1:ts, :]
        hj = lax.broadcasted_iota(jnp.int32, (LANES, DA), 0)
        ll = lax.broadcasted_iota(jnp.int32, (LANES, DA), 1)
        dd = ll - hj * HD
        chi, cmid, clo = _split3(c)
        e = ((dd >= 0) & (dd < HD)).astype(BF)
        cq_ref[...] = _dot(chi, e) + _dot(cmid, e) + _dot(clo, e)
        hh = lax.broadcasted_iota(jnp.int32, (16, LANES), 0)
        jj = lax.broadcasted_iota(jnp.int32, (16, LANES), 1)
        sel = (hh == jj).astype(BF)
        cr = _dot_nt(sel, chi) + _dot_nt(sel, cmid) + _dot_nt(sel, clo)
        crow_ref[...] = cr[0:NH, :]

    tile = lambda n: pl.BlockSpec((None, ts, n), lambda b, i: (b, i, 0))
    vec = lambda n: pl.BlockSpec((1, n), lambda b, i: (0, 0))
    return pl.pallas_call(
        body, grid=(bsz, nt),
        in_specs=[tile(DA), tile(DA), tile(LANES), vec(DA), vec(DA), vec(LANES)],
        out_specs=[tile(DA), tile(DA), tile(DA), pl.BlockSpec((None, NH, ts), lambda b, i: (b, 0, i))],
        out_shape=[jax.ShapeDtypeStruct((bsz, seq, DA), BF), jax.ShapeDtypeStruct((bsz, seq, DA), BF),
                   jax.ShapeDtypeStruct((bsz, seq, DA), F32), jax.ShapeDtypeStruct((bsz, NH, seq), F32)],
        scratch_shapes=[pltpu.VMEM((1, LANES), F32)],
        compiler_params=_cparams(("arbitrary", "arbitrary")), name=name)(pq, pk, pf, gq, gk, bf)


def _tile_rows(i, rows):
    return pl.ds(i * rows, rows) if isinstance(i, int) else pl.ds(pl.multiple_of(i * rows, rows), rows)


def _causal_mask(tq, tk):
    r = lax.broadcasted_iota(jnp.int32, (tq, tk), 0)
    c = lax.broadcasted_iota(jnp.int32, (tq, tk), 1)
    return r - c


def _fwd_attn(qs, kn, vb, cq, crow, pz, name, ex=None):
    bsz, seq, _ = qs.shape
    tq = 256
    nq = seq // tq
    npair = NH // 2

    def body(q_ref, k_ref, v_ref, cq_ref, cr_ref, z_ref, o_ref, ohp_ref, y_ref, lse_ref,
             s_buf, p_buf, acc_ref, accl_ref):
        qi = pl.program_id(2)
        head0 = lax.broadcasted_iota(jnp.int32, (1, LANES), 1) < HD
        q = q_ref[...]
        zq = jnp.zeros_like(q)
        qm = (jnp.where(head0, q, zq), jnp.where(head0, zq, q))
        ct = (cq_ref[:, 0:1], cq_ref[:, HD:HD + 1])

        tile = lambda j: _tile_rows(j, tq)

        def scores(j, slot):
            kb = k_ref[tile(j), :]
            for hh in range(2):
                s_buf[slot, hh] = _dot_nt(qm[hh], kb) + (ct[hh] - cr_ref[hh:hh + 1, tile(j)])

        def weighted_values(slot, j, al):
            vb_t = v_ref[tile(j), :]
            acc_ref[...] = al * acc_ref[...] + jnp.where(head0, _dot(p_buf[slot, 0, 0], vb_t),
                                                         _dot(p_buf[slot, 1, 0], vb_t))
            accl_ref[...] = al * accl_ref[...] + jnp.where(head0, _dot(p_buf[slot, 0, 1], vb_t),
                                                           _dot(p_buf[slot, 1, 1], vb_t))

        def softmax(slot, first_visible, m, l):
            m_new, l_new, al_new = ([], []), ([], []), []
            for r in range(tq // FWD_ROWS):
                rows = slice(r * FWD_ROWS, (r + 1) * FWD_ROWS)
                visible = _causal_mask(FWD_ROWS, tq) >= first_visible - r * FWD_ROWS
                alphas = []
                for hh in range(2):
                    s = jnp.where(visible, s_buf[slot, hh, rows, :], -1e30)
                    m_old = m[hh][rows]
                    m2 = jnp.maximum(m_old, jnp.max(s, axis=-1, keepdims=True))
                    p = jnp.exp(s - m2)
                    alpha = jnp.exp(m_old - m2)
                    l_new[hh].append(alpha * l[hh][rows] + jnp.sum(p, axis=-1, keepdims=True))
                    m_new[hh].append(m2)
                    pb = p.astype(BF)
                    p_buf[slot, hh, 0, rows, :] = pb
                    p_buf[slot, hh, 1, rows, :] = (p - pb.astype(F32)).astype(BF)
                    alphas.append(alpha)
                al_new.append(jnp.where(head0, alphas[0], alphas[1]))
            cat = lambda parts: jnp.concatenate(parts, axis=0)
            return (cat(m_new[0]), cat(m_new[1])), (cat(l_new[0]), cat(l_new[1])), cat(al_new)

        clamp = lambda j: jnp.clip(j, 0, nq - 1)

        def step(j, slot, carry):
            m, l, al = carry
            weighted_values(1 - slot, clamp(j - 1), al)
            m, l, al = softmax(slot, (j - qi) * tq, m, l)
            scores(clamp(j + 1), 1 - slot)
            return m, l, al

        scores(0, 0)
        p_buf[1] = jnp.zeros((2, 2, tq, tq), BF)
        acc_ref[...] = jnp.zeros((tq, LANES), F32)
        accl_ref[...] = jnp.zeros((tq, LANES), F32)
        neg = jnp.full((tq, 1), -1e30, F32)
        zcol = jnp.zeros((tq, 1), F32)
        trips = (qi + 2) // 2
        m, l, al = lax.fori_loop(0, trips, lambda t, cy: step(2 * t + 1, 1, step(2 * t, 0, cy)),
                                 ((neg, neg), (zcol, zcol), jnp.ones((tq, LANES), F32)))
        weighted_values(1, clamp(2 * trips - 1), al)
        inv = jnp.where(head0, 1.0 / l[0], 1.0 / l[1])
        o = acc_ref[...] * inv
        z = z_ref[...]
        o_ref[...] = o
        ohp_ref[...] = (acc_ref[...] + accl_ref[...]) * inv
        y_ref[...] = o * (z * _sigmoid(z))
        lse_ref[...] = jnp.where(head0, m[0] + jnp.log(l[0]), m[1] + jnp.log(l[1]))

    qblk = pl.BlockSpec((None, tq, LANES), lambda b, h, i: (b, i, h))
    kvblk = pl.BlockSpec((None, seq, LANES), lambda b, h, i: (b, 0, h))
    return _host_call(
        body, (bsz, npair, nq),
        [qblk, kvblk, kvblk, qblk, pl.BlockSpec((None, None, 2, seq), lambda b, h, i: (b, h, 0, 0)), qblk],
        [qblk, qblk, qblk, qblk], [jax.ShapeDtypeStruct((bsz, seq, DA), F32)] * 4,
        [pltpu.VMEM((2, 2, tq, tq), F32), pltpu.VMEM((2, 2, 2, tq, tq), BF), pltpu.VMEM((tq, LANES), F32),
         pltpu.VMEM((tq, LANES), F32)],
        (qs, kn, vb, cq, crow, pz), name, ex)


def _fwd_outproj(x, ycs, yatt, wo, name):
    t = x.shape[0]
    tm = 512

    def body(x_ref, a_ref, b_ref, w_ref, o_ref):
        o_ref[...] = (x_ref[...] + _dot(a_ref[...].astype(BF), w_ref[0:512, :])
                      + _dot(b_ref[...].astype(BF), w_ref[512:1024, :]))

    row = lambda n: pl.BlockSpec((tm, n), lambda i: (i, 0))
    return pl.pallas_call(
        body, grid=(t // tm,),
        in_specs=[row(D), row(512), row(512), pl.BlockSpec((D, D), lambda i: (0, 0))],
        out_specs=row(D), out_shape=jax.ShapeDtypeStruct((t, D), F32),
        compiler_params=_cparams(("arbitrary",)), name=name)(x, ycs, yatt, wo)


def _fwd_outproj_loss(x, ycs, yatt, wo, target, name):
    t = x.shape[0]
    tm = 512

    def body(x_ref, a_ref, b_ref, w_ref, t_ref, dy_ref, loss_ref):
        @pl.when(pl.program_id(0) == 0)
        def _():
            loss_ref[...] = jnp.zeros_like(loss_ref)

        y = (x_ref[...] + _dot(a_ref[...].astype(BF), w_ref[0:512, :])
             + _dot(b_ref[...].astype(BF), w_ref[512:1024, :]))
        err = y - t_ref[...]
        dy_ref[...] = err * (1.0 / D)
        per_tok = jnp.mean(err * err, axis=-1, keepdims=True)
        loss_ref[...] += 0.5 * jnp.sum(per_tok, axis=0, keepdims=True)

    row = lambda n: pl.BlockSpec((tm, n), lambda i: (i, 0))
    return pl.pallas_call(
        body, grid=(t // tm,),
        in_specs=[row(D), row(512), row(512), pl.BlockSpec((D, D), lambda i: (0, 0)), row(D)],
        out_specs=[row(D), pl.BlockSpec((1, 1), lambda i: (0, 0))],
        out_shape=[jax.ShapeDtypeStruct((t, D), F32), jax.ShapeDtypeStruct((1, 1), F32)],
        compiler_params=_cparams(("arbitrary",)), name=name)(x, ycs, yatt, wo, target)


def _dsilu(x, s):
    return s * (1.0 + x * (1.0 - s))


def _bwd_outproj(dy, ycs, yatt, o, ohp, pz, wo, name, ex=None):
    t = dy.shape[0]
    tm = 256

    def body(dy_ref, a_ref, b_ref, o_ref, ohp_ref, z_ref, w_ref, dcs_ref, do_ref, dz_ref, dl_ref, dw_ref):
        @pl.when(pl.program_id(0) == 0)
        def _():
            dw_ref[...] = jnp.zeros_like(dw_ref)

        dyb = dy_ref[...].astype(BF)
        dw_ref[0:512, :] += _dot_tn(a_ref[...].astype(BF), dyb)
        dw_ref[512:1024, :] += _dot_tn(b_ref[...].astype(BF), dyb)
        dcs_ref[...] = _dot_nt(dyb, w_ref[0:512, :])
        dyatt = _dot_nt(dyb, w_ref[512:1024, :])
        z = z_ref[...]
        sz = _sigmoid(z)
        o_t = o_ref[...]
        dob = (dyatt * (z * sz)).astype(BF)
        do_ref[...] = dob
        dz_ref[...] = dyatt * o_t * _dsilu(z, sz)
        prod = dob.astype(F32) * ohp_ref[...]
        for jb in range(DA // LANES):
            sl = slice(jb * LANES, (jb + 1) * LANES)
            dl_ref[:, sl] = _seg_sum64(prod[:, sl])

    row = lambda n: pl.BlockSpec((tm, n), lambda i: (i, 0))
    return _host_call(
        body, (t // tm,),
        [row(D), row(512), row(512), row(DA), row(DA), row(DA), pl.BlockSpec((D, D), lambda i: (0, 0))],
        [row(512), row(DA), row(DA), row(DA), pl.BlockSpec((D, D), lambda i: (0, 0))],
        [jax.ShapeDtypeStruct((t, 512), F32), jax.ShapeDtypeStruct((t, DA), BF),
         jax.ShapeDtypeStruct((t, DA), F32), jax.ShapeDtypeStruct((t, DA), F32),
         jax.ShapeDtypeStruct((D, D), F32)], [],
        (dy, ycs, yatt, o, ohp, pz, wo), name, ex)


def _bwd_attn(qs, kn, vb, do, cq, lse, dl, crow, name, ex=None):
    bsz, seq, _ = qs.shape
    tq = 256
    nq = seq // tq
    npair = NH // 2

    def body(q_ref, k_ref, v_ref, do_ref, cq_ref, lse_ref, dl_ref, cr_ref, dq_ref, dk_ref, dv_ref, dc_ref,
             qm, dom, s_buf, dp_buf, pd_buf):
        rc = _causal_mask(tq, tq)
        head0 = lax.broadcasted_iota(jnp.int32, (1, LANES), 1) < HD
        zb = jnp.zeros((seq, LANES), BF)
        qm[0] = jnp.where(head0, q_ref[...], zb)
        qm[1] = jnp.where(head0, zb, q_ref[...])
        dom[0] = jnp.where(head0, do_ref[...], zb)
        dom[1] = jnp.where(head0, zb, do_ref[...])
        dq_ref[...] = jnp.zeros_like(dq_ref)

        def kloop(kj, _):
            krows = pl.ds(pl.multiple_of(kj * tq, tq), tq)
            kb = k_ref[krows, :]
            vb_t = v_ref[krows, :]
            zk = jnp.zeros_like(kb)
            km = (jnp.where(head0, kb, zk), jnp.where(head0, zk, kb))

            tile = lambda i: _tile_rows(i, tq)

            def scores(i, slot):
                for hh in range(2):
                    c0 = hh * HD
                    s_buf[slot, hh] = (_dot_nt(qm[hh, tile(i), :], kb)
                                       + (cq_ref[tile(i), c0:c0 + 1] - cr_ref[hh:hh + 1, krows]))
                    dp_buf[slot, hh] = _dot_nt(dom[hh, tile(i), :], vb_t)

            def products(slot, i, dk, dv):
                dq = jnp.zeros((tq, LANES), F32)
                for hh in range(2):
                    dsb = pd_buf[slot, hh, 1]
                    dv = dv + _dot_tn(pd_buf[slot, hh, 0], dom[hh, tile(i), :])
                    dk = dk + _dot_tn(dsb, qm[hh, tile(i), :])
                    dq = dq + _dot(dsb, km[hh])
                dq_ref[tile(i), :] += dq
                return dk, dv

            clamp = lambda i: jnp.clip(i, 0, nq - 1)

            def qstep(i, slot, carry):
                dk, dv, cs = carry
                dk, dv = products(1 - slot, clamp(i - 1), dk, dv)
                cs = list(cs)
                ic = clamp(i)
                first_visible = jnp.where(i < nq, (kj - i) * tq, 2 * tq)
                for r in range(tq // ATT_ROWS):
                    rows = slice(r * ATT_ROWS, (r + 1) * ATT_ROWS)
                    qrows = pl.ds(pl.multiple_of(ic * tq + r * ATT_ROWS, ATT_ROWS), ATT_ROWS)
                    visible = _causal_mask(ATT_ROWS, tq) >= first_visible - r * ATT_ROWS
                    for hh in range(2):
                        c0 = hh * HD
                        s = jnp.where(visible, s_buf[slot, hh, rows, :], -1e30)
                        p = jnp.exp(s - lse_ref[qrows, c0:c0 + 1])
                        ds = p * (dp_buf[slot, hh, rows, :] - dl_ref[qrows, c0:c0 + 1])
                        pd_buf[slot, hh, 0, rows, :] = p.astype(BF)
                        pd_buf[slot, hh, 1, rows, :] = ds.astype(BF)
                        cs[hh] = cs[hh] + jnp.sum(ds, axis=0, keepdims=True)
                scores(clamp(i + 1), 1 - slot)
                return dk, dv, tuple(cs)

            scores(kj, 0)
            pd_buf[1] = jnp.zeros((2, 2, tq, tq), BF)
            zero = jnp.zeros((tq, LANES), F32)
            zrow = jnp.zeros((1, tq), F32)
            trips = (nq - kj + 1) // 2
            dk, dv, cs = lax.fori_loop(
                0, trips, lambda t, cy: qstep(kj + 2 * t + 1, 1, qstep(kj + 2 * t, 0, cy)), (zero, zero, (zrow, zrow)))
            dk, dv = products(1, clamp(kj + 2 * trips - 1), dk, dv)
            dk_ref[krows, :] = dk
            dv_ref[krows, :] = dv
            dc_ref[0:1, krows] = -cs[0]
            dc_ref[1:2, krows] = -cs[1]
            return 0

        lax.fori_loop(0, nq, kloop, 0)

    blk = pl.BlockSpec((None, seq, LANES), lambda b, h: (b, 0, h))
    rowblk = pl.BlockSpec((None, None, 2, seq), lambda b, h: (b, h, 0, 0))
    return _host_call(
        body, (bsz, npair), [blk, blk, blk, blk, blk, blk, blk, rowblk], [blk, blk, blk, rowblk],
        [jax.ShapeDtypeStruct((bsz, seq, DA), F32)] * 3 + [jax.ShapeDtypeStruct((bsz, npair, 2, seq), F32)],
        [pltpu.VMEM((2, seq, LANES), BF), pltpu.VMEM((2, seq, LANES), BF), pltpu.VMEM((2, 2, tq, tq), F32),
         pltpu.VMEM((2, 2, tq, tq), F32), pltpu.VMEM((2, 2, 2, tq, tq), BF)],
        (qs, kn, vb, do, cq, lse, dl, crow), name, ex)


def _bwd_attn_post(pq, pk, dqs, dkn, pf, dcrow, gq, gk, bf, name, ex=None):
    bsz, seq, _ = pq.shape
    ts = 512
    nt = seq // ts

    def body(q_ref, k_ref, dq_ref, dk_ref, f_ref, dc_ref, gq_ref, gk_ref, bf_ref,
             dpq_ref, dpk_ref, dpf_ref, dgq_ref, dgk_ref, dbf_ref, carry):
        @pl.when((pl.program_id(0) == 0) & (pl.program_id(1) == 0))
        def _():
            dgq_ref[...] = jnp.zeros_like(dgq_ref)
            dgk_ref[...] = jnp.zeros_like(dgk_ref)
            dbf_ref[...] = jnp.zeros_like(dbf_ref)

        @pl.when(pl.program_id(1) == 0)
        def _():
            carry[...] = jnp.zeros_like(carry)

        for x_ref, dx_ref, g_ref, dp_ref, dg_ref, scale in ((q_ref, dq_ref, gq_ref, dpq_ref, dgq_ref, 1.0 / 8.0),
                                                            (k_ref, dk_ref, gk_ref, dpk_ref, dgk_ref, 1.0)):
            for jb in range(DA // LANES):
                sl = slice(jb * LANES, (jb + 1) * LANES)
                gb = g_ref[:, sl]
                xhat, r, _ = _head_rms(x_ref[:, sl], gb)
                dn = dx_ref[:, sl] * scale
                dg_ref[:, sl] += jnp.sum(dn * xhat, axis=0, keepdims=True)
                dxh = dn * gb
                dp_ref[:, sl] = r * (dxh - xhat * (_seg_sum64(dxh * xhat) * (1.0 / HD)))

        ui = lax.broadcasted_iota(jnp.int32, (ts, ts), 0)
        tj = lax.broadcasted_iota(jnp.int32, (ts, ts), 1)
        tri = (tj >= ui).astype(BF)
        dc = jnp.concatenate([dc_ref[...], jnp.zeros((LANES - NH, ts), F32)], axis=0)
        hi, mid, lo = _split3(dc)
        dlf = _dot_nt(tri, hi) + _dot_nt(tri, mid) + _dot_nt(tri, lo) + carry[...]
        carry[...] = dlf[0:1, :]
        xf = f_ref[...] + bf_ref[...]
        lane = lax.broadcasted_iota(jnp.int32, (ts, LANES), 1)
        dfl = jnp.where(lane < NH, dlf * _sigmoid(-xf), 0.0)
        dpf_ref[...] = dfl
        dbf_ref[...] += jnp.sum(dfl, axis=0, keepdims=True)

    rev = lambda i: nt - 1 - i
    tile = lambda n: pl.BlockSpec((None, ts, n), lambda b, i: (b, rev(i), 0))
    vec = lambda n: pl.BlockSpec((1, n), lambda b, i: (0, 0))
    return _host_call(
        body, (bsz, nt),
        [tile(DA), tile(DA), tile(DA), tile(DA), tile(LANES),
         pl.BlockSpec((None, NH, ts), lambda b, i: (b, 0, rev(i))), vec(DA), vec(DA), vec(LANES)],
        [tile(DA), tile(DA), tile(LANES), vec(DA), vec(DA), vec(LANES)],
        [jax.ShapeDtypeStruct((bsz, seq, DA), F32), jax.ShapeDtypeStruct((bsz, seq, DA), F32),
         jax.ShapeDtypeStruct((bsz, seq, LANES), F32), jax.ShapeDtypeStruct((1, DA), F32),
         jax.ShapeDtypeStruct((1, DA), F32), jax.ShapeDtypeStruct((1, LANES), F32)],
        [pltpu.VMEM((1, LANES), F32)], (pq, pk, dqs, dkn, pf, dcrow, gq, gk, bf), name, ex)


def _bwd_conv(pcf, psc, dycs, cf_dw, cf_dw_b, ln_g, ln_b, cf_pw, sc_dw, name):
    bsz, seq, _ = pcf.shape
    ts = 512
    nt, tile, cur, halo, full = _conv_specs(bsz, seq, ts, -1)

    def body(cf_ref, ha_ref, hg_ref, sc_ref, hc_ref, hx_ref, dy_ref, dw_ref, b_ref, lg_ref, lb_ref, pw_ref, sdw_ref,
             dcf_ref, dsc_ref, ddw_ref, db_ref, dlg_ref, dlb_ref, dpw_ref, dsdw_ref,
             ubuf, usbuf, obuf, dubuf, dsbuf, carry_du, carry_dc):
        step = pl.program_id(1)
        first = tile(step) == 0

        @pl.when((pl.program_id(0) == 0) & (step == 0))
        def _():
            for r in (ddw_ref, db_ref, dlg_ref, dlb_ref, dpw_ref, dsdw_ref):
                r[...] = jnp.zeros_like(r)

        @pl.when(step == 0)
        def _():
            carry_du[...] = jnp.zeros((HALO, DC), F32)
            carry_dc[...] = jnp.zeros((HALO, DC), F32)

        def transposed_conv(d, carry, w_ref, taps_t):
            dubuf[0:ts, :] = d
            dubuf[ts:ts + HALO, :] = carry[...]
            carry[...] = d[0:HALO, :]
            _shift_copies(dubuf, dsbuf, sorted({off % SUBLANES for off in taps_t}))
            _tap_conv(dsbuf, w_ref, taps_t, obuf, ts, 0.0)
            return obuf[...]

        a = cf_ref[:, 0:256]
        g = cf_ref[:, 256:512]
        z = cf_ref[:, 512:768]
        lng = lg_ref[...]
        u0, n, rstd, u2, s2, u3 = _conformer_fwd(a, g, ha_ref[...], hg_ref[...], first, ubuf, usbuf, obuf, dw_ref,
                                                 b_ref[...], lng, lb_ref[...])
        u3b = u3.astype(BF)
        p = _dot(u3b, pw_ref[...])
        sz = _sigmoid(z)
        dy = dy_ref[:, 0:256]
        dcf_ref[:, 512:768] = dy * p * _dsilu(z, sz)
        dpb = (dy * (z * sz)).astype(BF)
        dpw_ref[...] += _dot_tn(u3b, dpb)
        du2 = _dot_nt(dpb, pw_ref[...]) * _dsilu(u2, s2)
        dlg_ref[...] += jnp.sum(du2 * n, axis=0, keepdims=True)
        dlb_ref[...] += jnp.sum(du2, axis=0, keepdims=True)
        dn = du2 * lng
        du1 = rstd * (dn - jnp.mean(dn, axis=-1, keepdims=True) - n * jnp.mean(dn * n, axis=-1, keepdims=True))
        db_ref[...] += jnp.sum(du1, axis=0, keepdims=True)
        du0 = transposed_conv(du1, carry_du, dw_ref, CF_TAPS_T)
        for k, tap_grad in enumerate(_tap_sums(dubuf, usbuf, CF_TAPS, ts)):
            ddw_ref[k:k + 1, :] += tap_grad
        sg = _sigmoid(g)
        dcf_ref[:, 0:256] = du0 * sg
        dcf_ref[:, 256:512] = du0 * a * sg * (1.0 - sg)

        bb = sc_ref[:, 0:256]
        c = sc_ref[:, 256:512]
        xs = sc_ref[:, 512:768]
        zs = sc_ref[:, 768:1024]
        cv = _shortconv_fwd(c, xs, hc_ref[...], hx_ref[...], first, ubuf, usbuf, obuf, sdw_ref)
        szs = _sigmoid(zs)
        dys = dy_ref[:, 256:512]
        gate = zs * szs
        dsc_ref[:, 0:256] = dys * cv * gate
        dsc_ref[:, 768:1024] = dys * bb * cv * _dsilu(zs, szs)
        dm = transposed_conv(dys * bb * gate, carry_dc, sdw_ref, SC_TAPS_T)
        for k, tap_grad in enumerate(_tap_sums(dubuf, usbuf, SC_TAPS, ts)):
            dsdw_ref[k:k + 1, :] += tap_grad
        dsc_ref[:, 256:512] = dm * xs
        dsc_ref[:, 512:768] = dm * c

    outt = lambda n: pl.BlockSpec((None, ts, n), lambda b, i: (b, tile(i), 0))
    return pl.pallas_call(
        body, grid=(bsz, nt),
        in_specs=[cur(768), halo(0), halo(1), cur(1024), halo(1), halo(2), cur(512),
                  full(CFW, DC), full(1, DC), full(1, DC), full(1, DC), full(DC, DC), full(SCW, DC)],
        out_specs=[outt(768), outt(1024), full(CFW, DC), full(1, DC), full(1, DC), full(1, DC), full(DC, DC),
                   full(SCW, DC)],
        out_shape=[jax.ShapeDtypeStruct((bsz, seq, 768), F32), jax.ShapeDtypeStruct((bsz, seq, 1024), F32),
                   jax.ShapeDtypeStruct((CFW, DC), F32), jax.ShapeDtypeStruct((1, DC), F32),
                   jax.ShapeDtypeStruct((1, DC), F32), jax.ShapeDtypeStruct((1, DC), F32),
                   jax.ShapeDtypeStruct((DC, DC), F32), jax.ShapeDtypeStruct((SCW, DC), F32)],
        scratch_shapes=[pltpu.VMEM((HALO + ts, DC), F32), pltpu.VMEM((SUBLANES, HALO + ts, DC), F32),
                        pltpu.VMEM((ts, DC), F32), pltpu.VMEM((ts + HALO, DC), F32),
                        pltpu.VMEM((SUBLANES, ts + HALO, DC), F32), pltpu.VMEM((HALO, DC), F32),
                        pltpu.VMEM((HALO, DC), F32)],
        compiler_params=_cparams(("arbitrary", "arbitrary")), name=name,
    )(pcf, pcf, pcf, psc, psc, psc, dycs, cf_dw, cf_dw_b, ln_g, ln_b, cf_pw, sc_dw)


def _bwd_inproj(x, g, dyres, w, dcf, dsc, dq, dk, dv, dz, df, name):
    t = x.shape[0]
    tm = 256
    pieces = ((C_CF, 768), (C_SC, 1024), (C_Q, DA), (C_K, DA), (C_V, DA), (C_Z, DA), (C_F, LANES))

    def body(x_ref, g_ref, dy_ref, w_hbm, dcf_ref, dsc_ref, dq_ref, dk_ref, dv_ref, dz_ref, df_ref,
             dx_ref, dg_ref, dw_hbm, w_vmem, dw_acc, sem):
        i = pl.program_id(0)

        @pl.when(i == 0)
        def _():
            cp = pltpu.make_async_copy(w_hbm, w_vmem, sem)
            cp.start()
            dw_acc[...] = jnp.zeros_like(dw_acc)
            dg_ref[...] = jnp.zeros_like(dg_ref)
            cp.wait()

        xt = x_ref[...]
        gg = g_ref[...]
        r = lax.rsqrt(jnp.mean(xt * xt, axis=-1, keepdims=True) + EPS)
        xhat = xt * r
        ht = (xhat * gg).astype(BF).T
        dh = jnp.zeros((tm, D), F32)
        for (c0, n), ref in zip(pieces, (dcf_ref, dsc_ref, dq_ref, dk_ref, dv_ref, dz_ref, df_ref)):
            for s0 in range(0, n, 512):
                s1 = min(s0 + 512, n)
                d = ref[:, s0:s1].astype(BF)
                dh = dh + _dot_nt(d, w_vmem[:, c0 + s0:c0 + s1])
                dw_acc[:, c0 + s0:c0 + s1] += _dot(ht, d)
        dg_ref[...] += jnp.sum(dh * xhat, axis=0, keepdims=True)
        dhg = dh * gg
        dx_ref[...] = dy_ref[...] + r * (dhg - xhat * jnp.mean(dhg * xhat, axis=-1, keepdims=True))

        @pl.when(i == pl.num_programs(0) - 1)
        def _():
            out = pltpu.make_async_copy(dw_acc, dw_hbm, sem)
            out.start()
            out.wait()

    row = lambda n: pl.BlockSpec((tm, n), lambda i: (i, 0))
    anyspec = pl.BlockSpec(memory_space=pl.ANY)
    return pl.pallas_call(
        body, grid=(t // tm,),
        in_specs=[row(D), pl.BlockSpec((1, D), lambda i: (0, 0)), row(D), anyspec,
                  row(768), row(1024), row(DA), row(DA), row(DA), row(DA), row(LANES)],
        out_specs=[row(D), pl.BlockSpec((1, D), lambda i: (0, 0)), anyspec],
        out_shape=[jax.ShapeDtypeStruct((t, D), F32), jax.ShapeDtypeStruct((1, D), F32),
                   jax.ShapeDtypeStruct((D, NP), F32)],
        scratch_shapes=[pltpu.VMEM((D, NP), BF), pltpu.VMEM((D, NP), F32), pltpu.SemaphoreType.DMA],
        compiler_params=_cparams(("arbitrary",)), name=name)(x, g, dyres, w, dcf, dsc, dq, dk, dv, dz, df)


def _adamw(w, g, m, v, tr, name):
    rows, cols = w.shape

    def body(w_ref, g_ref, m_ref, v_ref, d_ref, m2_ref, v2_ref):
        gt = g_ref[...]
        m2 = ADAM_B1 * m_ref[...] + (1.0 - ADAM_B1) * gt
        v2 = ADAM_B2 * v_ref[...] + (1.0 - ADAM_B2) * (gt * gt)
        m_hat = m2 / (1.0 - ADAM_B1 ** ADAM_STEP)
        v_hat = v2 / (1.0 - ADAM_B2 ** ADAM_STEP)
        d_ref[...] = -ADAM_LR * (m_hat / (jnp.sqrt(v_hat) + ADAM_EPS) + ADAM_WD * w_ref[...])
        m2_ref[...] = m2
        v2_ref[...] = v2

    blk = pl.BlockSpec((tr, cols), lambda i: (i, 0))
    return pl.pallas_call(
        body, grid=(rows // tr,), in_specs=[blk] * 4, out_specs=[blk] * 3,
        out_shape=[jax.ShapeDtypeStruct((rows, cols), F32)] * 3,
        compiler_params=_cparams(("arbitrary",)), name=name)(w, g, m, v)


def _place():
    x, y, c = lax.axis_index("x"), lax.axis_index("y"), lax.axis_index("c")
    chips = [(1 - x, y), (x, 1 - y), (1 - x, 1 - y)]
    return x, y, c, chips


def _halves(c, rows, align):
    rh = rows // 2
    return pl.ds(pl.multiple_of(c * rh, align), rh), pl.ds(pl.multiple_of((1 - c) * rh, align), rh)


ANY = pl.BlockSpec(memory_space=pl.ANY)


class _Exchange:
    def __init__(self, operands, out_shape, sems, start, wait):
        self.operands, self.out_shape, self.sems, self.start, self.wait = operands, out_shape, sems, start, wait


def _run_exchange(ex, name):
    n_in, n_out = len(ex.operands), len(ex.out_shape)

    def body(*refs):
        ins, outs, sems = refs[:n_in], refs[n_in:n_in + n_out], refs[n_in + n_out:]
        ex.start(ins, outs, sems)
        ex.wait(ins, outs, sems)

    return pl.pallas_call(body, in_specs=[ANY] * n_in, out_specs=[ANY] * n_out, out_shape=ex.out_shape,
                          scratch_shapes=ex.sems, name=name)(*ex.operands)


def _host_call(body, grid, in_specs, out_specs, out_shape, scratch, operands, name, ex=None):
    sem = ("arbitrary",) * len(grid)
    if ex is None:
        outs = pl.pallas_call(body, grid=grid, in_specs=in_specs, out_specs=out_specs, out_shape=out_shape,
                              scratch_shapes=scratch, compiler_params=_cparams(sem), name=name)(*operands)
        return outs, None
    n_in, n_out, n_scr = len(in_specs), len(out_specs), len(scratch)
    xi, xo = len(ex.operands), len(ex.out_shape)

    def hosted(*refs):
        ins, xins = refs[:n_in], refs[n_in:n_in + xi]
        o0 = n_in + xi
        outs, xouts = refs[o0:o0 + n_out], refs[o0 + n_out:o0 + n_out + xo]
        s0 = o0 + n_out + xo
        scr, xsems = refs[s0:s0 + n_scr], refs[s0 + n_scr:]
        first = pl.program_id(0) == 0
        last = pl.program_id(0) == pl.num_programs(0) - 1
        for d in range(1, len(grid)):
            first = first & (pl.program_id(d) == 0)
            last = last & (pl.program_id(d) == pl.num_programs(d) - 1)

        @pl.when(first)
        def _():
            ex.start(xins, xouts, xsems)

        body(*ins, *outs, *scr)

        @pl.when(last)
        def _():
            ex.wait(xins, xouts, xsems)

    res = pl.pallas_call(
        hosted, grid=grid, in_specs=list(in_specs) + [ANY] * xi, out_specs=list(out_specs) + [ANY] * xo,
        out_shape=list(out_shape) + list(ex.out_shape), scratch_shapes=list(scratch) + list(ex.sems),
        compiler_params=_cparams(sem), name=name)(*operands, *ex.operands)
    return res[:n_out], res[n_out:]


def _all_gather_chips(shards):
    n = len(shards)

    def copies(srcs, outs, sems):
        send_sems, recv_sems, local_sems = sems
        x, y, c, chips = _place()
        me = 2 * x + y
        halves = [_halves(c, s.shape[1], 16) for s in shards]
        local = [pltpu.make_async_copy(srcs[i], outs[i].at[:, me], local_sems.at[i]) for i in range(n)]

        def ici(i, j, chip, k):
            mine = halves[i][0]
            return pltpu.make_async_remote_copy(
                src_ref=srcs[i].at[:, mine, :], dst_ref=outs[i].at[:, k, mine, :], send_sem=send_sems.at[6 * i + j],
                recv_sem=recv_sems.at[6 * i + j], device_id=(*chip, c), device_id_type=MESH)

        def d2d(i, j, k, half):
            return pltpu.make_async_remote_copy(
                src_ref=outs[i].at[:, k, half, :], dst_ref=outs[i].at[:, k, half, :], send_sem=send_sems.at[6 * i + 3 + j],
                recv_sem=recv_sems.at[6 * i + 3 + j], device_id=(x, y, 1 - c), device_id_type=MESH)

        return me, chips, halves, local, ici, d2d

    def start(srcs, outs, sems):
        me, chips, _, local, ici, _ = copies(srcs, outs, sems)
        for cp in local:
            cp.start()
        for i in range(n):
            for j, chip in enumerate(chips):
                ici(i, j, chip, me).start()

    def wait(srcs, outs, sems):
        me, chips, halves, local, ici, d2d = copies(srcs, outs, sems)
        passed = []
        for i in range(n):
            for j, (px, py) in enumerate(chips):
                k = 2 * px + py
                ici(i, j, (px, py), k).wait_recv()
                fw = d2d(i, j, k, halves[i][0])
                fw.start()
                passed.append(fw)
        for i in range(n):
            for j, (px, py) in enumerate(chips):
                d2d(i, j, 2 * px + py, halves[i][1]).wait_recv()
        for i in range(n):
            for j, chip in enumerate(chips):
                ici(i, j, chip, me).wait_send()
        for cp in passed:
            cp.wait_send()
        for cp in local:
            cp.wait()

    return _Exchange(
        list(shards), [jax.ShapeDtypeStruct((s.shape[0], 4) + s.shape[1:], s.dtype) for s in shards],
        [pltpu.SemaphoreType.DMA((6 * n,)), pltpu.SemaphoreType.DMA((6 * n,)), pltpu.SemaphoreType.DMA((n,))],
        start, wait)


def _rs_swap_halves(gs):
    n = len(gs)

    def copies(srcs, gots, sems):
        send_sems, recv_sems = sems
        x, y, c, _ = _place()
        return [pltpu.make_async_remote_copy(
            src_ref=srcs[i].at[:, :, _halves(c, gs[i].shape[2], 8)[1], :], dst_ref=gots[i], send_sem=send_sems.at[i],
            recv_sem=recv_sems.at[i], device_id=(x, y, 1 - c), device_id_type=MESH) for i in range(n)]

    def start(srcs, gots, sems):
        for cp in copies(srcs, gots, sems):
            cp.start()

    def wait(srcs, gots, sems):
        for cp in copies(srcs, gots, sems):
            cp.wait()

    return _Exchange(
        list(gs), [jax.ShapeDtypeStruct(g.shape[:2] + (g.shape[2] // 2, g.shape[3]), g.dtype) for g in gs],
        [pltpu.SemaphoreType.DMA((n,)), pltpu.SemaphoreType.DMA((n,))], start, wait)


def _rs_add_pair(g, got, cidx, tb, out_dtype, name):
    nl, _, rh, cols = got.shape
    nb = rh // tb

    def body(c_ref, g_ref, o_ref, s_ref):
        s_ref[...] = (g_ref[...] + o_ref[...]).astype(out_dtype)

    blk = lambda half: pl.BlockSpec((None, None, tb, cols), (lambda l, k, i, c: (l, k, c[0] * nb + i, 0)) if half
                                    else (lambda l, k, i, c: (l, k, i, 0)))
    return pl.pallas_call(
        body,
        grid_spec=pltpu.PrefetchScalarGridSpec(num_scalar_prefetch=1, grid=(nl, 4, nb),
                                               in_specs=[blk(True), blk(False)], out_specs=blk(False)),
        out_shape=jax.ShapeDtypeStruct(got.shape, out_dtype),
        compiler_params=_cparams(("arbitrary", "arbitrary", "arbitrary")), name=name)(cidx, g, got)


def _rs_exchange_chips(pairs):
    n = len(pairs)

    def copies(srcs, gots, sems):
        send_sems, recv_sems, local_sems = sems
        x, y, c, chips = _place()
        me = 2 * x + y
        local = [pltpu.make_async_copy(srcs[i].at[:, me], gots[i].at[:, me], local_sems.at[i]) for i in range(n)]

        def ici(i, j, chip, frm, to):
            return pltpu.make_async_remote_copy(
                src_ref=srcs[i].at[:, to], dst_ref=gots[i].at[:, frm], send_sem=send_sems.at[3 * i + j],
                recv_sem=recv_sems.at[3 * i + j], device_id=(*chip, c), device_id_type=MESH)

        sent = [ici(i, j, (px, py), me, 2 * px + py) for i in range(n) for j, (px, py) in enumerate(chips)]
        recvd = [ici(i, j, (px, py), 2 * px + py, me) for i in range(n) for j, (px, py) in enumerate(chips)]
        return local, sent, recvd

    def start(srcs, gots, sems):
        local, sent, _ = copies(srcs, gots, sems)
        for cp in local + sent:
            cp.start()

    def wait(srcs, gots, sems):
        local, sent, recvd = copies(srcs, gots, sems)
        for cp in recvd:
            cp.wait_recv()
        for cp in sent:
            cp.wait_send()
        for cp in local:
            cp.wait()

    return _Exchange(
        list(pairs), [jax.ShapeDtypeStruct(p.shape, p.dtype) for p in pairs],
        [pltpu.SemaphoreType.DMA((3 * n,)), pltpu.SemaphoreType.DMA((3 * n,)), pltpu.SemaphoreType.DMA((n,))],
        start, wait)


def _rs_add_chips(got, tb, name):
    nl, _, rh, cols = got.shape

    def body(g_ref, s_ref):
        s_ref[...] = ((g_ref[0].astype(F32) + g_ref[1].astype(F32)) + g_ref[2].astype(F32)) + g_ref[3].astype(F32)

    return pl.pallas_call(
        body, grid=(nl, rh // tb),
        in_specs=[pl.BlockSpec((None, 4, tb, cols), lambda l, i: (l, 0, i, 0))],
        out_specs=pl.BlockSpec((None, tb, cols), lambda l, i: (l, i, 0)),
        out_shape=jax.ShapeDtypeStruct((nl, rh, cols), F32),
        compiler_params=_cparams(("arbitrary", "arbitrary")), name=name)(got)


def _rs_join_halves(totals):
    n = len(totals)

    def copies(srcs, outs, sems):
        send_sems, recv_sems, local_sems = sems
        x, y, c, _ = _place()
        local, send, recv = [], [], []
        for i in range(n):
            mine, other = _halves(c, 2 * totals[i].shape[1], 8)
            local.append(pltpu.make_async_copy(srcs[i], outs[i].at[:, mine, :], local_sems.at[i]))
            for lst, rows in ((send, mine), (recv, other)):
                lst.append(pltpu.make_async_remote_copy(
                    src_ref=srcs[i], dst_ref=outs[i].at[:, rows, :], send_sem=send_sems.at[i], recv_sem=recv_sems.at[i],
                    device_id=(x, y, 1 - c), device_id_type=MESH))
        return local, send, recv

    def start(srcs, outs, sems):
        local, send, _ = copies(srcs, outs, sems)
        for cp in local + send:
            cp.start()

    def wait(srcs, outs, sems):
        local, send, recv = copies(srcs, outs, sems)
        for cp in recv:
            cp.wait_recv()
        for cp in send:
            cp.wait_send()
        for cp in local:
            cp.wait()

    return _Exchange(
        list(totals), [jax.ShapeDtypeStruct((t.shape[0], 2 * t.shape[1], t.shape[2]), F32) for t in totals],
        [pltpu.SemaphoreType.DMA((n,)), pltpu.SemaphoreType.DMA((n,)), pltpu.SemaphoreType.DMA((n,))], start, wait)


def _rows(a):
    return a.reshape(-1, LANES)


def _pad_rows(a, mult):
    r = (-a.shape[0]) % mult
    return a if r == 0 else jnp.concatenate([a, jnp.zeros((r, a.shape[1]), a.dtype)], axis=0)


def kernel(x, norm_g, w_in, b_f, cf_dw, cf_dw_b, cf_ln_g, cf_ln_b, cf_pw, sc_dw, q_norm_g, k_norm_g, w_out, loss_target, m_norm_g, m_w_in, m_b_f, m_cf_dw, m_cf_dw_b, m_cf_ln_g, m_cf_ln_b, m_cf_pw, m_sc_dw, m_q_norm_g, m_k_norm_g, m_w_out, v_norm_g, v_w_in, v_b_f, v_cf_dw, v_cf_dw_b, v_cf_ln_g, v_cf_ln_b, v_cf_pw, v_sc_dw, v_q_norm_g, v_k_norm_g, v_w_out):
    bsz, seq, _ = x.shape
    t = bsz * seq

    taps = jnp.concatenate([cf_dw.reshape(-1), sc_dw.reshape(-1)])
    taps_bf = lax.bitcast_convert_type(taps, jnp.bfloat16).reshape(-1, LANES)
    n_pw, n_taps = 2 * 64 * DC // LANES, taps_bf.shape[0]
    small_w = _pad_rows(jnp.concatenate([_rows(cf_pw.astype(BF)), taps_bf], axis=0), 32)[None]
    win_b, wout_b = w_in.astype(BF), w_out.astype(BF)
    win0_g, small_g = _run_exchange(_all_gather_chips([win_b[0:1], small_w]), "all_gather_first")
    gather_rest = _all_gather_chips([win_b[1:DEPTH], wout_b])
    full_w_in = lambda g: jnp.pad(jnp.concatenate([g[k] for k in range(4)], axis=-1), ((0, 0), (0, NP - N_IN)))
    w_in_full = [full_w_in(win0_g[0])] + [None] * (DEPTH - 1)
    w_out_full = None
    cf_pw_full = jnp.concatenate([small_g[0, k, 0:n_pw].reshape(DEPTH, 64, DC) for k in range(4)], axis=1)
    taps_all = [lax.bitcast_convert_type(small_g[0, k, n_pw:n_pw + n_taps].reshape(-1, 2), F32) for k in range(4)]
    n_cfdw = DEPTH * CFW * 64
    cf_dw_full = jnp.concatenate([tk[:n_cfdw].reshape(DEPTH, CFW, 64) for tk in taps_all], axis=-1)
    sc_dw_full = jnp.concatenate([tk[n_cfdw:].reshape(DEPTH, SCW, 64) for tk in taps_all], axis=-1)

    bf_pad = jnp.pad(b_f, ((0, 0), (0, LANES - NH)))

    xs = [x.reshape(t, D)]
    saved = []
    dy = loss_part = None
    for l in range(DEPTH):
        xl = xs[-1]
        gq, gk = q_norm_g[l].reshape(1, DA), k_norm_g[l].reshape(1, DA)
        pcf, psc, pq, pk, vb, pz, pf = _fwd_inproj(xl, norm_g[l][None], w_in_full[l], f"fwd_inproj_{l}")
        b3 = lambda a: a.reshape(bsz, seq, a.shape[-1])
        ycs = _fwd_conv(b3(pcf), b3(psc), cf_dw_full[l], cf_dw_b[l][None], cf_ln_g[l][None], cf_ln_b[l][None],
                        cf_pw_full[l], sc_dw_full[l], f"fwd_conv_{l}")
        qs, kn, cq, crow8 = _fwd_attn_prep(b3(pq), b3(pk), b3(pf), gq, gk, bf_pad[l][None], f"fwd_attn_prep_{l}")
        crow = crow8.reshape(bsz, NH // 2, 2, seq)
        (o, ohp, yatt, lse), landed = _fwd_attn(qs, kn, b3(vb), cq, crow, b3(pz), f"fwd_attn_{l}",
                                                gather_rest if l == 0 else None)
        if l == 0:
            win_rest_g, wout_g = landed
            w_in_full[1:] = [full_w_in(win_rest_g[i]) for i in range(DEPTH - 1)]
            w_out_full = wout_g.reshape(DEPTH, D, D)
        ycs2, yatt2 = ycs.reshape(t, 512), yatt.reshape(t, DA)
        if l + 1 < DEPTH:
            xs.append(_fwd_outproj(xl, ycs2, yatt2, w_out_full[l], f"fwd_outproj_{l}"))
        else:
            dy, loss_part = _fwd_outproj_loss(xl, ycs2, yatt2, w_out_full[l], loss_target.reshape(t, D),
                                              f"fwd_outproj_loss_{l}")
        saved.append((pcf, psc, pq, pk, vb, pz, pf, ycs2, yatt2, o, ohp, qs, kn, cq, crow, lse))

    grads = [None] * DEPTH
    reduced = [None] * DEPTH
    cidx = lax.axis_index("c").astype(jnp.int32).reshape(1)
    big_tiles = [256, 128]

    def big_packs(gl):
        return [jnp.stack([gl["w_in"][:, NSH * k:NSH * (k + 1)] for k in range(4)])[None],
                gl["w_out"].reshape(1, 4, 256, D)]

    def add_pairs(packs, gots, tiles, dtypes, tag):
        return [_rs_add_pair(p, got, cidx, tb, dt, f"rs_add_pair_{tag}_{i}")
                for i, (p, got, tb, dt) in enumerate(zip(packs, gots, tiles, dtypes))]

    def add_chips(gots2, tiles, tag):
        return [_rs_add_chips(got, tb, f"rs_add_chips_{tag}_{i}") for i, (got, tb) in enumerate(zip(gots2, tiles))]

    for l in reversed(range(DEPTH)):
        pcf, psc, pq, pk, vb, pz, pf, ycs2, yatt2, o, ohp, qs, kn, cq, crow, lse = saved[l]
        gq, gk = q_norm_g[l].reshape(1, DA), k_norm_g[l].reshape(1, DA)
        b3 = lambda a: a.reshape(bsz, seq, a.shape[-1])
        f2 = lambda a: a.reshape(t, a.shape[-1])
        packs = big_packs(grads[l + 1]) if l + 1 < DEPTH else None
        (dycs, do, dz, dl, d_wout), gots = _bwd_outproj(
            dy, ycs2, yatt2, f2(o), f2(ohp), pz, w_out_full[l], f"bwd_outproj_{l}",
            _rs_swap_halves(packs) if packs else None)
        pairs = add_pairs(packs, gots, big_tiles, [BF, BF], l + 1) if packs else None
        (dqs, dkn, dv, dcrow), gots2 = _bwd_attn(qs, kn, b3(vb), b3(do), cq, lse, b3(dl), crow, f"bwd_attn_{l}",
                                                 _rs_exchange_chips(pairs) if packs else None)
        totals = add_chips(gots2, big_tiles, l + 1) if packs else None
        (dpq, dpk, dpf, d_gq, d_gk, d_bf), joined = _bwd_attn_post(
            b3(pq), b3(pk), dqs, dkn, b3(pf), dcrow.reshape(bsz, NH, seq), gq, gk, bf_pad[l][None],
            f"bwd_attn_post_{l}", _rs_join_halves(totals) if packs else None)
        if packs:
            reduced[l + 1] = joined
        dcf, dsc, d_cfdw, d_cfb, d_lng, d_lnb, d_pw, d_scdw = _bwd_conv(
            b3(pcf), b3(psc), b3(dycs), cf_dw_full[l], cf_dw_b[l][None], cf_ln_g[l][None], cf_ln_b[l][None],
            cf_pw_full[l], sc_dw_full[l], f"bwd_conv_{l}")
        dy, d_ng, d_win = _bwd_inproj(xs[l], norm_g[l][None], dy, w_in_full[l], f2(dcf), f2(dsc), f2(dpq), f2(dpk),
                                      f2(dv), dz, f2(dpf), f"bwd_inproj_{l}")
        grads[l] = dict(norm_g=d_ng[0], w_in=d_win, b_f=d_bf[0, :NH], cf_dw=d_cfdw, cf_dw_b=d_cfb[0],
                        cf_ln_g=d_lng[0], cf_ln_b=d_lnb[0], cf_pw=d_pw, sc_dw=d_scdw,
                        q_norm_g=d_gq.reshape(NH, HD), k_norm_g=d_gk.reshape(NH, HD), w_out=d_wout)
    grad_x = dy.reshape(bsz, seq, D)
    gw = {n: jnp.stack([grads[l][n] for l in range(DEPTH)]) for n in grads[0] if n not in ("w_in", "w_out")}

    rep_names = ("norm_g", "b_f", "cf_dw_b", "cf_ln_g", "cf_ln_b", "q_norm_g", "k_norm_g")
    rep = jnp.concatenate([jnp.pad(gw[n].reshape(-1), (0, (-gw[n].size) % LANES)) for n in rep_names]
                          + [jnp.pad(loss_part.reshape(-1), (0, LANES - 1))]).reshape(-1, LANES)
    blocks = []
    for k in range(4):
        small = jnp.concatenate([gw["cf_dw"][:, :, 64 * k:64 * (k + 1)].reshape(-1),
                                 gw["sc_dw"][:, :, 64 * k:64 * (k + 1)].reshape(-1)]).reshape(-1, LANES)
        blocks.append(_pad_rows(jnp.concatenate([_rows(gw["cf_pw"][:, 64 * k:64 * (k + 1), :]), small, rep], axis=0), 16))
    g_small = jnp.stack(blocks)[None]
    packs = big_packs(grads[0]) + [g_small]
    tiles = big_tiles + [g_small.shape[2] // 2]
    gots = _run_exchange(_rs_swap_halves(packs), "rs_swap_halves_last")
    pairs = add_pairs(packs, gots, tiles, [BF, BF, F32], 0)
    gots2 = _run_exchange(_rs_exchange_chips(pairs), "rs_exchange_chips_last")
    red_win0, red_wout0, red_small = _run_exchange(_rs_join_halves(add_chips(gots2, tiles, 0)), "rs_join_halves_last")
    reduced[0] = (red_win0, red_wout0)

    n_small = (DEPTH * CFW * 64 + DEPTH * SCW * 64) // LANES
    red = red_small[0]
    pos = [0]

    def take(nrows):
        pos[0] += nrows
        return red[pos[0] - nrows:pos[0]]

    g = {}
    g["w_in"] = jnp.concatenate([reduced[l][0] for l in range(DEPTH)], axis=0)
    g["w_out"] = jnp.concatenate([reduced[l][1] for l in range(DEPTH)], axis=0)
    g["cf_pw"] = take(n_pw).reshape(DEPTH, 64, DC)
    small = take(n_small).reshape(-1)
    g["cf_dw"] = small[:n_cfdw].reshape(DEPTH, CFW, 64)
    g["sc_dw"] = small[n_cfdw:].reshape(DEPTH, SCW, 64)
    shapes = dict(norm_g=norm_g.shape, b_f=b_f.shape, cf_dw_b=cf_dw_b.shape, cf_ln_g=cf_ln_g.shape,
                  cf_ln_b=cf_ln_b.shape, q_norm_g=q_norm_g.shape, k_norm_g=k_norm_g.shape)
    for n in rep_names:
        size = 1
        for s in shapes[n]:
            size *= s
        g[n] = take(-(-size // LANES)).reshape(-1)[:size].reshape(shapes[n])
    loss = take(1)[0, 0]

    order = ("norm_g", "w_in", "b_f", "cf_dw", "cf_dw_b", "cf_ln_g", "cf_ln_b", "cf_pw", "sc_dw", "q_norm_g",
             "k_norm_g", "w_out")
    weights = dict(norm_g=norm_g, w_in=w_in, b_f=b_f, cf_dw=cf_dw, cf_dw_b=cf_dw_b, cf_ln_g=cf_ln_g, cf_ln_b=cf_ln_b,
                   cf_pw=cf_pw, sc_dw=sc_dw, q_norm_g=q_norm_g, k_norm_g=k_norm_g, w_out=w_out)
    ms = dict(norm_g=m_norm_g, w_in=m_w_in, b_f=m_b_f, cf_dw=m_cf_dw, cf_dw_b=m_cf_dw_b, cf_ln_g=m_cf_ln_g,
              cf_ln_b=m_cf_ln_b, cf_pw=m_cf_pw, sc_dw=m_sc_dw, q_norm_g=m_q_norm_g, k_norm_g=m_k_norm_g, w_out=m_w_out)
    vs = dict(norm_g=v_norm_g, w_in=v_w_in, b_f=v_b_f, cf_dw=v_cf_dw, cf_dw_b=v_cf_dw_b, cf_ln_g=v_cf_ln_g,
              cf_ln_b=v_cf_ln_b, cf_pw=v_cf_pw, sc_dw=v_sc_dw, q_norm_g=v_q_norm_g, k_norm_g=v_k_norm_g, w_out=v_w_out)
    delta, new_m, new_v = {}, {}, {}
    two_d = lambda a: a.reshape(-1, a.shape[-1])
    for n in ("w_in", "w_out"):
        outs = _adamw(two_d(weights[n]), two_d(g[n]), two_d(ms[n]), two_d(vs[n]), 256, f"adamw_{n}")
        delta[n], new_m[n], new_v[n] = (a.reshape(weights[n].shape) for a in outs)
    small_names = [n for n in order if n not in ("w_in", "w_out")]
    flat = lambda d: _pad_rows(jnp.pad(jnp.concatenate([d[n].reshape(-1) for n in small_names]),
                                       (0, (-sum(weights[n].size for n in small_names)) % LANES)).reshape(-1, LANES), 8)
    fw = flat(weights)
    outs = _adamw(fw, flat(g), flat(ms), flat(vs), fw.shape[0], "adamw_small")
    off = 0
    for n in small_names:
        size = weights[n].size
        for dst, src in zip((delta, new_m, new_v), outs):
            dst[n] = src.reshape(-1)[off:off + size].reshape(weights[n].shape)
        off += size

    return (loss, grad_x, *[g[n] for n in order], *[delta[n] for n in order], *[new_m[n] for n in order],
            *[new_v[n] for n in order])
```

```python
import jax
import jax.numpy as jnp
from jax import lax
from jax.experimental import pallas as pl
from jax.experimental.pallas import tpu as pltpu

F32 = jnp.float32
BF = jnp.bfloat16
MESH = pl.DeviceIdType.MESH

DEPTH = 2
D = 1024
DC = 256
DA = 512
NH = 8
HD = 64
CFW = 31
SCW = 3
N_IN = 3848
NP = 3968
NSH = N_IN // 4
HALO = 32
ATT_ROWS = 32
FWD_ROWS = 256
SUBLANES = 8
CONV_ROWS = 32
TAP_GROUP = 8
EPS = 1e-6
LANES = 128
VMEM_LIMIT = 56 * 1024 * 1024

C_CF, C_SC, C_Q, C_K, C_V, C_Z, C_F = 0, 768, 1792, 2304, 2816, 3328, 3840

ADAM_LR = 0.001
ADAM_B1 = 0.9
ADAM_B2 = 0.999
ADAM_EPS = 1e-08
ADAM_WD = 0.01
ADAM_STEP = 10


def _cparams(sem=None):
    return pltpu.CompilerParams(dimension_semantics=sem, vmem_limit_bytes=VMEM_LIMIT)


def _dot(a, b):
    return jnp.dot(a, b, preferred_element_type=F32)


def _dot_nt(a, b):
    return lax.dot_general(a, b, (((1,), (1,)), ((), ())), preferred_element_type=F32)


def _dot_tn(a, b):
    return lax.dot_general(a, b, (((0,), (0,)), ((), ())), preferred_element_type=F32)


def _split3(x):
    hi = x.astype(BF)
    r1 = x - hi.astype(F32)
    mid = r1.astype(BF)
    lo = (r1 - mid.astype(F32)).astype(BF)
    return hi, mid, lo


def _dot_exact(a_bf, x):
    hi, mid, lo = _split3(x)
    return _dot(a_bf, hi) + _dot(a_bf, mid) + _dot(a_bf, lo)


def _sigmoid(x):
    return 1.0 / (1.0 + jnp.exp(-x))


def _seg_sum64(x):
    i = lax.broadcasted_iota(jnp.int32, (LANES, LANES), 0)
    j = lax.broadcasted_iota(jnp.int32, (LANES, LANES), 1)
    g = ((i >= HD) == (j >= HD)).astype(BF)
    hi, mid, lo = _split3(x)
    return _dot(hi, g) + _dot(mid, g) + _dot(lo, g)


def _fwd_inproj(x, g, w, name):
    t = x.shape[0]
    tm = 256

    def body(x_ref, g_ref, w_ref, cf_ref, sc_ref, q_ref, k_ref, v_ref, z_ref, f_ref):
        xt = x_ref[...]
        r = lax.rsqrt(jnp.mean(xt * xt, axis=-1, keepdims=True) + EPS)
        h = ((xt * r) * g_ref[...]).astype(BF)
        cf_ref[:, 0:512] = _dot(h, w_ref[:, 0:512])
        cf_ref[:, 512:768] = _dot(h, w_ref[:, 512:768])
        sc_ref[:, 0:512] = _dot(h, w_ref[:, C_SC:C_SC + 512])
        sc_ref[:, 512:1024] = _dot(h, w_ref[:, C_SC + 512:C_Q])
        q_ref[...] = _dot(h, w_ref[:, C_Q:C_K])
        k_ref[...] = _dot(h, w_ref[:, C_K:C_V])
        v_ref[...] = _dot(h, w_ref[:, C_V:C_Z]).astype(BF)
        z_ref[...] = _dot(h, w_ref[:, C_Z:C_F])
        f_ref[...] = _dot(h, w_ref[:, C_F:NP])

    row = lambda n: pl.BlockSpec((tm, n), lambda i: (i, 0))
    return pl.pallas_call(
        body, grid=(t // tm,),
        in_specs=[row(D), pl.BlockSpec((1, D), lambda i: (0, 0)), pl.BlockSpec((D, NP), lambda i: (0, 0))],
        out_specs=[row(768), row(1024), row(DA), row(DA), row(DA), row(DA), row(LANES)],
        out_shape=[jax.ShapeDtypeStruct((t, 768), F32), jax.ShapeDtypeStruct((t, 1024), F32),
                   jax.ShapeDtypeStruct((t, DA), F32), jax.ShapeDtypeStruct((t, DA), F32),
                   jax.ShapeDtypeStruct((t, DA), BF), jax.ShapeDtypeStruct((t, DA), F32),
                   jax.ShapeDtypeStruct((t, LANES), F32)],
        compiler_params=_cparams(("arbitrary",)), name=name)(x, g, w)


def _shift_copies(buf, sbuf, shifts):
    n = buf.shape[0]
    for sh in shifts:
        rows = n if sh == 0 else n - SUBLANES
        sbuf[sh, 0:rows, :] = buf[pl.ds(sh, rows), :]


def _tap_rows(sbuf, off, r0):
    sh = off % SUBLANES
    return sbuf[sh, pl.ds(pl.multiple_of(r0 + (off - sh), SUBLANES), CONV_ROWS), :]


def _tap_conv(sbuf, w_ref, offsets, dst, ts, bias):
    def chunk(c, _):
        r0 = pl.multiple_of(c * CONV_ROWS, CONV_ROWS)
        acc = jnp.zeros((CONV_ROWS, DC), F32) + bias
        for k, off in enumerate(offsets):
            acc = acc + w_ref[k:k + 1, :] * _tap_rows(sbuf, off, r0)
        dst[pl.ds(r0, CONV_ROWS), :] = acc
        return 0

    lax.fori_loop(0, ts // CONV_ROWS, chunk, 0)


def _tap_sums(a_buf, sbuf, offsets, ts):
    outs = []
    for g0 in range(0, len(offsets), TAP_GROUP):
        group = offsets[g0:g0 + TAP_GROUP]

        def chunk(c, accs, group=group):
            r0 = pl.multiple_of(c * CONV_ROWS, CONV_ROWS)
            a = a_buf[pl.ds(r0, CONV_ROWS), :]
            return tuple(acc + (a * _tap_rows(sbuf, off, r0)).reshape(CONV_ROWS // SUBLANES, SUBLANES, DC).sum(axis=0)
                         for acc, off in zip(accs, group))

        accs = lax.fori_loop(0, ts // CONV_ROWS, chunk, tuple(jnp.zeros((SUBLANES, DC), F32) for _ in group))
        outs += [jnp.sum(acc, axis=0, keepdims=True) for acc in accs]
    return outs


CF_TAPS = tuple(HALO - (CFW - 1) + k for k in range(CFW))
SC_TAPS = tuple(HALO - (SCW - 1) + k for k in range(SCW))
CF_TAPS_T = tuple(CFW - 1 - k for k in range(CFW))
SC_TAPS_T = tuple(SCW - 1 - k for k in range(SCW))


def _conformer_fwd(a, g, ha, hg, first, ubuf, usbuf, u1buf, dw_ref, bias, lng, lnb):
    ts = a.shape[0]
    u0 = a * _sigmoid(g)
    ubuf[0:HALO, :] = jnp.where(first, 0.0, ha * _sigmoid(hg))
    ubuf[HALO:HALO + ts, :] = u0
    _shift_copies(ubuf, usbuf, range(SUBLANES))
    _tap_conv(usbuf, dw_ref, CF_TAPS, u1buf, ts, bias)
    u1 = u1buf[...]
    mu = jnp.mean(u1, axis=-1, keepdims=True)
    xc = u1 - mu
    rstd = lax.rsqrt(jnp.mean(xc * xc, axis=-1, keepdims=True) + EPS)
    n = xc * rstd
    u2 = n * lng + lnb
    s2 = _sigmoid(u2)
    u3 = u2 * s2
    return u0, n, rstd, u2, s2, u3


def _shortconv_fwd(c, xs, hc, hx, first, mbuf, msbuf, cvbuf, dw_ref):
    ts = c.shape[0]
    mbuf[0:HALO, :] = jnp.where(first, 0.0, hc * hx)
    mbuf[HALO:HALO + ts, :] = c * xs
    _shift_copies(mbuf, msbuf, sorted({off % SUBLANES for off in SC_TAPS}))
    _tap_conv(msbuf, dw_ref, SC_TAPS, cvbuf, ts, 0.0)
    return cvbuf[...]


def _conv_specs(bsz, seq, ts, order):
    nt = seq // ts
    tile = (lambda i: i) if order > 0 else (lambda i: nt - 1 - i)
    hrow = lambda i: jnp.maximum(tile(i) * (ts // HALO) - 1, 0)
    cur = lambda n: pl.BlockSpec((None, ts, n), lambda b, i: (b, tile(i), 0))
    halo = lambda j: pl.BlockSpec((None, HALO, DC), lambda b, i: (b, hrow(i), j))
    full = lambda r, c: pl.BlockSpec((r, c), lambda b, i: (0, 0))
    return nt, tile, cur, halo, full


def _fwd_conv(pcf, psc, cf_dw, cf_dw_b, ln_g, ln_b, cf_pw, sc_dw, name):
    bsz, seq, _ = pcf.shape
    ts = 512
    nt, tile, cur, halo, full = _conv_specs(bsz, seq, ts, +1)

    def body(cf_ref, ha_ref, hg_ref, sc_ref, hc_ref, hx_ref, dw_ref, b_ref, lg_ref, lb_ref, pw_ref, sdw_ref,
             y_ref, ubuf, sbuf, obuf):
        first = pl.program_id(1) == 0
        _, _, _, _, _, u3 = _conformer_fwd(cf_ref[:, 0:256], cf_ref[:, 256:512], ha_ref[...], hg_ref[...], first,
                                           ubuf, sbuf, obuf, dw_ref, b_ref[...], lg_ref[...], lb_ref[...])
        z = cf_ref[:, 512:768]
        y_ref[:, 0:256] = _dot(u3.astype(BF), pw_ref[...]) * (z * _sigmoid(z))
        cv = _shortconv_fwd(sc_ref[:, 256:512], sc_ref[:, 512:768], hc_ref[...], hx_ref[...], first,
                            ubuf, sbuf, obuf, sdw_ref)
        zs = sc_ref[:, 768:1024]
        y_ref[:, 256:512] = sc_ref[:, 0:256] * cv * (zs * _sigmoid(zs))

    return pl.pallas_call(
        body, grid=(bsz, nt),
        in_specs=[cur(768), halo(0), halo(1), cur(1024), halo(1), halo(2),
                  full(CFW, DC), full(1, DC), full(1, DC), full(1, DC), full(DC, DC), full(SCW, DC)],
        out_specs=pl.BlockSpec((None, ts, 512), lambda b, i: (b, i, 0)),
        out_shape=jax.ShapeDtypeStruct((bsz, seq, 512), F32),
        scratch_shapes=[pltpu.VMEM((HALO + ts, DC), F32), pltpu.VMEM((SUBLANES, HALO + ts, DC), F32),
                        pltpu.VMEM((ts, DC), F32)],
        compiler_params=_cparams(("arbitrary", "arbitrary")), name=name,
    )(pcf, pcf, pcf, psc, psc, psc, cf_dw, cf_dw_b, ln_g, ln_b, cf_pw, sc_dw)


def _head_rms(xb, gb):
    ms = _seg_sum64(xb * xb) * (1.0 / HD)
    r = lax.rsqrt(ms + EPS)
    xhat = xb * r
    return xhat, r, xhat * gb


def _fwd_attn_prep(pq, pk, pf, gq, gk, bf, name):
    bsz, seq, _ = pq.shape
    ts = 512
    nt = seq // ts

    def body(q_ref, k_ref, f_ref, gq_ref, gk_ref, bf_ref, qs_ref, kn_ref, cq_ref, crow_ref, carry):
        @pl.when(pl.program_id(1) == 0)
        def _():
            carry[...] = jnp.zeros_like(carry)

        for jb in range(DA // LANES):
            sl = slice(jb * LANES, (jb + 1) * LANES)
            _, _, qn = _head_rms(q_ref[:, sl], gq_ref[:, sl])
            qs_ref[:, sl] = (qn * (1.0 / 8.0)).astype(BF)
            _, _, kn = _head_rms(k_ref[:, sl], gk_ref[:, sl])
            kn_ref[:, sl] = kn.astype(BF)

        xf = f_ref[...] + bf_ref[...]
        lf = jnp.minimum(xf, 0.0) - jnp.log(1.0 + jnp.exp(-jnp.abs(xf)))
        ti = lax.broadcasted_iota(jnp.int32, (ts, ts), 0)
        si = lax.broadcasted_iota(jnp.int32, (ts, ts), 1)
        c = _dot_exact((si <= ti).astype(BF), lf) + carry[...]
        carry[...] = c[ts - 1:ts, :]
        hj = lax.broadcasted_iota(jnp.int32, (LANES, DA), 0)
        ll = lax.broadcasted_iota(jnp.int32, (LANES, DA), 1)
        dd = ll - hj * HD
        chi, cmid, clo = _split3(c)
        e = ((dd >= 0) & (dd < HD)).astype(BF)
        cq_ref[...] = _dot(chi, e) + _dot(cmid, e) + _dot(clo, e)
        hh = lax.broadcasted_iota(jnp.int32, (16, LANES), 0)
        jj = lax.broadcasted_iota(jnp.int32, (16, LANES), 1)
        sel = (hh == jj).astype(BF)
        cr = _dot_nt(sel, chi) + _dot_nt(sel, cmid) + _dot_nt(sel, clo)
        crow_ref[...] = cr[0:NH, :]

    tile = lambda n: pl.BlockSpec((None, ts, n), lambda b, i: (b, i, 0))
    vec = lambda n: pl.BlockSpec((1, n), lambda b, i: (0, 0))
    return pl.pallas_call(
        body, grid=(bsz, nt),
        in_specs=[tile(DA), tile(DA), tile(LANES), vec(DA), vec(DA), vec(LANES)],
        out_specs=[tile(DA), tile(DA), tile(DA), pl.BlockSpec((None, NH, ts), lambda b, i: (b, 0, i))],
        out_shape=[jax.ShapeDtypeStruct((bsz, seq, DA), BF), jax.ShapeDtypeStruct((bsz, seq, DA), BF),
                   jax.ShapeDtypeStruct((bsz, seq, DA), F32), jax.ShapeDtypeStruct((bsz, NH, seq), F32)],
        scratch_shapes=[pltpu.VMEM((1, LANES), F32)],
        compiler_params=_cparams(("arbitrary", "arbitrary")), name=name)(pq, pk, pf, gq, gk, bf)


def _tile_rows(i, rows):
    return pl.ds(i * rows, rows) if isinstance(i, int) else pl.ds(pl.multiple_of(i * rows, rows), rows)


def _causal_mask(tq, tk):
    r = lax.broadcasted_iota(jnp.int32, (tq, tk), 0)
    c = lax.broadcasted_iota(jnp.int32, (tq, tk), 1)
    return r - c


def _fwd_attn(qs, kn, vb, cq, crow, pz, name, ex=None):
    bsz, seq, _ = qs.shape
    tq = 256
    nq = seq // tq
    npair = NH // 2

    def body(q_ref, k_ref, v_ref, cq_ref, cr_ref, z_ref, o_ref, ohp_ref, y_ref, lse_ref,
             s_buf, p_buf, acc_ref, accl_ref):
        qi = pl.program_id(2)
        head0 = lax.broadcasted_iota(jnp.int32, (1, LANES), 1) < HD
        q = q_ref[...]
        zq = jnp.zeros_like(q)
        qm = (jnp.where(head0, q, zq), jnp.where(head0, zq, q))
        ct = (cq_ref[:, 0:1], cq_ref[:, HD:HD + 1])

        tile = lambda j: _tile_rows(j, tq)

        def scores(j, slot):
            kb = k_ref[tile(j), :]
            for hh in range(2):
                s_buf[slot, hh] = _dot_nt(qm[hh], kb) + (ct[hh] - cr_ref[hh:hh + 1, tile(j)])

        def weighted_values(slot, j, al):
            vb_t = v_ref[tile(j), :]
            acc_ref[...] = al * acc_ref[...] + jnp.where(head0, _dot(p_buf[slot, 0, 0], vb_t),
                                                         _dot(p_buf[slot, 1, 0], vb_t))
            accl_ref[...] = al * accl_ref[...] + jnp.where(head0, _dot(p_buf[slot, 0, 1], vb_t),
                                                           _dot(p_buf[slot, 1, 1], vb_t))

        def softmax(slot, first_visible, m, l):
            m_new, l_new, al_new = ([], []), ([], []), []
            for r in range(tq // FWD_ROWS):
                rows = slice(r * FWD_ROWS, (r + 1) * FWD_ROWS)
                visible = _causal_mask(FWD_ROWS, tq) >= first_visible - r * FWD_ROWS
                alphas = []
                for hh in range(2):
                    s = jnp.where(visible, s_buf[slot, hh, rows, :], -1e30)
                    m_old = m[hh][rows]
                    m2 = jnp.maximum(m_old, jnp.max(s, axis=-1, keepdims=True))
                    p = jnp.exp(s - m2)
                    alpha = jnp.exp(m_old - m2)
                    l_new[hh].append(alpha * l[hh][rows] + jnp.sum(p, axis=-1, keepdims=True))
                    m_new[hh].append(m2)
                    pb = p.astype(BF)
                    p_buf[slot, hh, 0, rows, :] = pb
                    p_buf[slot, hh, 1, rows, :] = (p - pb.astype(F32)).astype(BF)
                    alphas.append(alpha)
                al_new.append(jnp.where(head0, alphas[0], alphas[1]))
            cat = lambda parts: jnp.concatenate(parts, axis=0)
            return (cat(m_new[0]), cat(m_new[1])), (cat(l_new[0]), cat(l_new[1])), cat(al_new)

        clamp = lambda j: jnp.clip(j, 0, nq - 1)

        def step(j, slot, carry):
            m, l, al = carry
            weighted_values(1 - slot, clamp(j - 1), al)
            m, l, al = softmax(slot, (j - qi) * tq, m, l)
            scores(clamp(j + 1), 1 - slot)
            return m, l, al

        scores(0, 0)
        p_buf[1] = jnp.zeros((2, 2, tq, tq), BF)
        acc_ref[...] = jnp.zeros((tq, LANES), F32)
        accl_ref[...] = jnp.zeros((tq, LANES), F32)
        neg = jnp.full((tq, 1), -1e30, F32)
        zcol = jnp.zeros((tq, 1), F32)
        trips = (qi + 2) // 2
        m, l, al = lax.fori_loop(0, trips, lambda t, cy: step(2 * t + 1, 1, step(2 * t, 0, cy)),
                                 ((neg, neg), (zcol, zcol), jnp.ones((tq, LANES), F32)))
        weighted_values(1, clamp(2 * trips - 1), al)
        inv = jnp.where(head0, 1.0 / l[0], 1.0 / l[1])
        o = acc_ref[...] * inv
        z = z_ref[...]
        o_ref[...] = o
        ohp_ref[...] = (acc_ref[...] + accl_ref[...]) * inv
        y_ref[...] = o * (z * _sigmoid(z))
        lse_ref[...] = jnp.where(head0, m[0] + jnp.log(l[0]), m[1] + jnp.log(l[1]))

    qblk = pl.BlockSpec((None, tq, LANES), lambda b, h, i: (b, i, h))
    kvblk = pl.BlockSpec((None, seq, LANES), lambda b, h, i: (b, 0, h))
    return _host_call(
        body, (bsz, npair, nq),
        [qblk, kvblk, kvblk, qblk, pl.BlockSpec((None, None, 2, seq), lambda b, h, i: (b, h, 0, 0)), qblk],
        [qblk, qblk, qblk, qblk], [jax.ShapeDtypeStruct((bsz, seq, DA), F32)] * 4,
        [pltpu.VMEM((2, 2, tq, tq), F32), pltpu.VMEM((2, 2, 2, tq, tq), BF), pltpu.VMEM((tq, LANES), F32),
         pltpu.VMEM((tq, LANES), F32)],
        (qs, kn, vb, cq, crow, pz), name, ex)


def _fwd_outproj(x, ycs, yatt, wo, name):
    t = x.shape[0]
    tm = 512

    def body(x_ref, a_ref, b_ref, w_ref, o_ref):
        o_ref[...] = (x_ref[...] + _dot(a_ref[...].astype(BF), w_ref[0:512, :])
                      + _dot(b_ref[...].astype(BF), w_ref[512:1024, :]))

    row = lambda n: pl.BlockSpec((tm, n), lambda i: (i, 0))
    return pl.pallas_call(
        body, grid=(t // tm,),
        in_specs=[row(D), row(512), row(512), pl.BlockSpec((D, D), lambda i: (0, 0))],
        out_specs=row(D), out_shape=jax.ShapeDtypeStruct((t, D), F32),
        compiler_params=_cparams(("arbitrary",)), name=name)(x, ycs, yatt, wo)


def _fwd_outproj_loss(x, ycs, yatt, wo, target, name):
    t = x.shape[0]
    tm = 512

    def body(x_ref, a_ref, b_ref, w_ref, t_ref, dy_ref, loss_ref):
        @pl.when(pl.program_id(0) == 0)
        def _():
            loss_ref[...] = jnp.zeros_like(loss_ref)

        y = (x_ref[...] + _dot(a_ref[...].astype(BF), w_ref[0:512, :])
             + _dot(b_ref[...].astype(BF), w_ref[512:1024, :]))
        err = y - t_ref[...]
        dy_ref[...] = err * (1.0 / D)
        per_tok = jnp.mean(err * err, axis=-1, keepdims=True)
        loss_ref[...] += 0.5 * jnp.sum(per_tok, axis=0, keepdims=True)

    row = lambda n: pl.BlockSpec((tm, n), lambda i: (i, 0))
    return pl.pallas_call(
        body, grid=(t // tm,),
        in_specs=[row(D), row(512), row(512), pl.BlockSpec((D, D), lambda i: (0, 0)), row(D)],
        out_specs=[row(D), pl.BlockSpec((1, 1), lambda i: (0, 0))],
        out_shape=[jax.ShapeDtypeStruct((t, D), F32), jax.ShapeDtypeStruct((1, 1), F32)],
        compiler_params=_cparams(("arbitrary",)), name=name)(x, ycs, yatt, wo, target)


def _dsilu(x, s):
    return s * (1.0 + x * (1.0 - s))


def _bwd_outproj(dy, ycs, yatt, o, ohp, pz, wo, name, ex=None):
    t = dy.shape[0]
    tm = 256

    def body(dy_ref, a_ref, b_ref, o_ref, ohp_ref, z_ref, w_ref, dcs_ref, do_ref, dz_ref, dl_ref, dw_ref):
        @pl.when(pl.program_id(0) == 0)
        def _():
            dw_ref[...] = jnp.zeros_like(dw_ref)

        dyb = dy_ref[...].astype(BF)
        dw_ref[0:512, :] += _dot_tn(a_ref[...].astype(BF), dyb)
        dw_ref[512:1024, :] += _dot_tn(b_ref[...].astype(BF), dyb)
        dcs_ref[...] = _dot_nt(dyb, w_ref[0:512, :])
        dyatt = _dot_nt(dyb, w_ref[512:1024, :])
        z = z_ref[...]
        sz = _sigmoid(z)
        o_t = o_ref[...]
        dob = (dyatt * (z * sz)).astype(BF)
        do_ref[...] = dob
        dz_ref[...] = dyatt * o_t * _dsilu(z, sz)
        prod = dob.astype(F32) * ohp_ref[...]
        for jb in range(DA // LANES):
            sl = slice(jb * LANES, (jb + 1) * LANES)
            dl_ref[:, sl] = _seg_sum64(prod[:, sl])

    row = lambda n: pl.BlockSpec((tm, n), lambda i: (i, 0))
    return _host_call(
        body, (t // tm,),
        [row(D), row(512), row(512), row(DA), row(DA), row(DA), pl.BlockSpec((D, D), lambda i: (0, 0))],
        [row(512), row(DA), row(DA), row(DA), pl.BlockSpec((D, D), lambda i: (0, 0))],
        [jax.ShapeDtypeStruct((t, 512), F32), jax.ShapeDtypeStruct((t, DA), BF),
         jax.ShapeDtypeStruct((t, DA), F32), jax.ShapeDtypeStruct((t, DA), F32),
         jax.ShapeDtypeStruct((D, D), F32)], [],
        (dy, ycs, yatt, o, ohp, pz, wo), name, ex)


def _bwd_attn(qs, kn, vb, do, cq, lse, dl, crow, name, ex=None):
    bsz, seq, _ = qs.shape
    tq = 256
    nq = seq // tq
    npair = NH // 2

    def body(q_ref, k_ref, v_ref, do_ref, cq_ref, lse_ref, dl_ref, cr_ref, dq_ref, dk_ref, dv_ref, dc_ref,
             qm, dom, s_buf, dp_buf, pd_buf):
        rc = _causal_mask(tq, tq)
        head0 = lax.broadcasted_iota(jnp.int32, (1, LANES), 1) < HD
        zb = jnp.zeros((seq, LANES), BF)
        qm[0] = jnp.where(head0, q_ref[...], zb)
        qm[1] = jnp.where(head0, zb, q_ref[...])
        dom[0] = jnp.where(head0, do_ref[...], zb)
        dom[1] = jnp.where(head0, zb, do_ref[...])
        dq_ref[...] = jnp.zeros_like(dq_ref)

        def kloop(kj, _):
            krows = pl.ds(pl.multiple_of(kj * tq, tq), tq)
            kb = k_ref[krows, :]
            vb_t = v_ref[krows, :]
            zk = jnp.zeros_like(kb)
            km = (jnp.where(head0, kb, zk), jnp.where(head0, zk, kb))

            tile = lambda i: _tile_rows(i, tq)

            def scores(i, slot):
                for hh in range(2):
                    c0 = hh * HD
                    s_buf[slot, hh] = (_dot_nt(qm[hh, tile(i), :], kb)
                                       + (cq_ref[tile(i), c0:c0 + 1] - cr_ref[hh:hh + 1, krows]))
                    dp_buf[slot, hh] = _dot_nt(dom[hh, tile(i), :], vb_t)

            def products(slot, i, dk, dv):
                dq = jnp.zeros((tq, LANES), F32)
                for hh in range(2):
                    dsb = pd_buf[slot, hh, 1]
                    dv = dv + _dot_tn(pd_buf[slot, hh, 0], dom[hh, tile(i), :])
                    dk = dk + _dot_tn(dsb, qm[hh, tile(i), :])
                    dq = dq + _dot(dsb, km[hh])
                dq_ref[tile(i), :] += dq
                return dk, dv

            clamp = lambda i: jnp.clip(i, 0, nq - 1)

            def qstep(i, slot, carry):
                dk, dv, cs = carry
                dk, dv = products(1 - slot, clamp(i - 1), dk, dv)
                cs = list(cs)
                ic = clamp(i)
                first_visible = jnp.where(i < nq, (kj - i) * tq, 2 * tq)
                for r in range(tq // ATT_ROWS):
                    rows = slice(r * ATT_ROWS, (r + 1) * ATT_ROWS)
                    qrows = pl.ds(pl.multiple_of(ic * tq + r * ATT_ROWS, ATT_ROWS), ATT_ROWS)
                    visible = _causal_mask(ATT_ROWS, tq) >= first_visible - r * ATT_ROWS
                    for hh in range(2):
                        c0 = hh * HD
                        s = jnp.where(visible, s_buf[slot, hh, rows, :], -1e30)
                        p = jnp.exp(s - lse_ref[qrows, c0:c0 + 1])
                        ds = p * (dp_buf[slot, hh, rows, :] - dl_ref[qrows, c0:c0 + 1])
                        pd_buf[slot, hh, 0, rows, :] = p.astype(BF)
                        pd_buf[slot, hh, 1, rows, :] = ds.astype(BF)
                        cs[hh] = cs[hh] + jnp.sum(ds, axis=0, keepdims=True)
                scores(clamp(i + 1), 1 - slot)
                return dk, dv, tuple(cs)

            scores(kj, 0)
            pd_buf[1] = jnp.zeros((2, 2, tq, tq), BF)
            zero = jnp.zeros((tq, LANES), F32)
            zrow = jnp.zeros((1, tq), F32)
            trips = (nq - kj + 1) // 2
            dk, dv, cs = lax.fori_loop(
                0, trips, lambda t, cy: qstep(kj + 2 * t + 1, 1, qstep(kj + 2 * t, 0, cy)), (zero, zero, (zrow, zrow)))
            dk, dv = products(1, clamp(kj + 2 * trips - 1), dk, dv)
            dk_ref[krows, :] = dk
            dv_ref[krows, :] = dv
            dc_ref[0:1, krows] = -cs[0]
            dc_ref[1:2, krows] = -cs[1]
            return 0

        lax.fori_loop(0, nq, kloop, 0)

    blk = pl.BlockSpec((None, seq, LANES), lambda b, h: (b, 0, h))
    rowblk = pl.BlockSpec((None, None, 2, seq), lambda b, h: (b, h, 0, 0))
    return _host_call(
        body, (bsz, npair), [blk, blk, blk, blk, blk, blk, blk, rowblk], [blk, blk, blk, rowblk],
        [jax.ShapeDtypeStruct((bsz, seq, DA), F32)] * 3 + [jax.ShapeDtypeStruct((bsz, npair, 2, seq), F32)],
        [pltpu.VMEM((2, seq, LANES), BF), pltpu.VMEM((2, seq, LANES), BF), pltpu.VMEM((2, 2, tq, tq), F32),
         pltpu.VMEM((2, 2, tq, tq), F32), pltpu.VMEM((2, 2, 2, tq, tq), BF)],
        (qs, kn, vb, do, cq, lse, dl, crow), name, ex)


def _bwd_attn_post(pq, pk, dqs, dkn, pf, dcrow, gq, gk, bf, name, ex=None):
    bsz, seq, _ = pq.shape
    ts = 512
    nt = seq // ts

    def body(q_ref, k_ref, dq_ref, dk_ref, f_ref, dc_ref, gq_ref, gk_ref, bf_ref,
             dpq_ref, dpk_ref, dpf_ref, dgq_ref, dgk_ref, dbf_ref, carry):
        @pl.when((pl.program_id(0) == 0) & (pl.program_id(1) == 0))
        def _():
            dgq_ref[...] = jnp.zeros_like(dgq_ref)
            dgk_ref[...] = jnp.zeros_like(dgk_ref)
            dbf_ref[...] = jnp.zeros_like(dbf_ref)

        @pl.when(pl.program_id(1) == 0)
        def _():
            carry[...] = jnp.zeros_like(carry)

        for x_ref, dx_ref, g_ref, dp_ref, dg_ref, scale in ((q_ref, dq_ref, gq_ref, dpq_ref, dgq_ref, 1.0 / 8.0),
                                                            (k_ref, dk_ref, gk_ref, dpk_ref, dgk_ref, 1.0)):
            for jb in range(DA // LANES):
                sl = slice(jb * LANES, (jb + 1) * LANES)
                gb = g_ref[:, sl]
                xhat, r, _ = _head_rms(x_ref[:, sl], gb)
                dn = dx_ref[:, sl] * scale
                dg_ref[:, sl] += jnp.sum(dn * xhat, axis=0, keepdims=True)
                dxh = dn * gb
                dp_ref[:, sl] = r * (dxh - xhat * (_seg_sum64(dxh * xhat) * (1.0 / HD)))

        ui = lax.broadcasted_iota(jnp.int32, (ts, ts), 0)
        tj = lax.broadcasted_iota(jnp.int32, (ts, ts), 1)
        tri = (tj >= ui).astype(BF)
        dc = jnp.concatenate([dc_ref[...], jnp.zeros((LANES - NH, ts), F32)], axis=0)
        hi, mid, lo = _split3(dc)
        dlf = _dot_nt(tri, hi) + _dot_nt(tri, mid) + _dot_nt(tri, lo) + carry[...]
        carry[...] = dlf[0:1, :]
        xf = f_ref[...] + bf_ref[...]
        lane = lax.broadcasted_iota(jnp.int32, (ts, LANES), 1)
        dfl = jnp.where(lane < NH, dlf * _sigmoid(-xf), 0.0)
        dpf_ref[...] = dfl
        dbf_ref[...] += jnp.sum(dfl, axis=0, keepdims=True)

    rev = lambda i: nt - 1 - i
    tile = lambda n: pl.BlockSpec((None, ts, n), lambda b, i: (b, rev(i), 0))
    vec = lambda n: pl.BlockSpec((1, n), lambda b, i: (0, 0))
    return _host_call(
        body, (bsz, nt),
        [tile(DA), tile(DA), tile(DA), tile(DA), tile(LANES),
         pl.BlockSpec((None, NH, ts), lambda b, i: (b, 0, rev(i))), vec(DA), vec(DA), vec(LANES)],
        [tile(DA), tile(DA), tile(LANES), vec(DA), vec(DA), vec(LANES)],
        [jax.ShapeDtypeStruct((bsz, seq, DA), F32), jax.ShapeDtypeStruct((bsz, seq, DA), F32),
         jax.ShapeDtypeStruct((bsz, seq, LANES), F32), jax.ShapeDtypeStruct((1, DA), F32),
         jax.ShapeDtypeStruct((1, DA), F32), jax.ShapeDtypeStruct((1, LANES), F32)],
        [pltpu.VMEM((1, LANES), F32)], (pq, pk, dqs, dkn, pf, dcrow, gq, gk, bf), name, ex)


def _bwd_conv(pcf, psc, dycs, cf_dw, cf_dw_b, ln_g, ln_b, cf_pw, sc_dw, name):
    bsz, seq, _ = pcf.shape
    ts = 512
    nt, tile, cur, halo, full = _conv_specs(bsz, seq, ts, -1)

    def body(cf_ref, ha_ref, hg_ref, sc_ref, hc_ref, hx_ref, dy_ref, dw_ref, b_ref, lg_ref, lb_ref, pw_ref, sdw_ref,
             dcf_ref, dsc_ref, ddw_ref, db_ref, dlg_ref, dlb_ref, dpw_ref, dsdw_ref,
             ubuf, usbuf, obuf, dubuf, dsbuf, carry_du, carry_dc):
        step = pl.program_id(1)
        first = tile(step) == 0

        @pl.when((pl.program_id(0) == 0) & (step == 0))
        def _():
            for r in (ddw_ref, db_ref, dlg_ref, dlb_ref, dpw_ref, dsdw_ref):
                r[...] = jnp.zeros_like(r)

        @pl.when(step == 0)
        def _():
            carry_du[...] = jnp.zeros((HALO, DC), F32)
            carry_dc[...] = jnp.zeros((HALO, DC), F32)

        def transposed_conv(d, carry, w_ref, taps_t):
            dubuf[0:ts, :] = d
            dubuf[ts:ts + HALO, :] = carry[...]
            carry[...] = d[0:HALO, :]
            _shift_copies(dubuf, dsbuf, sorted({off % SUBLANES for off in taps_t}))
            _tap_conv(dsbuf, w_ref, taps_t, obuf, ts, 0.0)
            return obuf[...]

        a = cf_ref[:, 0:256]
        g = cf_ref[:, 256:512]
        z = cf_ref[:, 512:768]
        lng = lg_ref[...]
        u0, n, rstd, u2, s2, u3 = _conformer_fwd(a, g, ha_ref[...], hg_ref[...], first, ubuf, usbuf, obuf, dw_ref,
                                                 b_ref[...], lng, lb_ref[...])
        u3b = u3.astype(BF)
        p = _dot(u3b, pw_ref[...])
        sz = _sigmoid(z)
        dy = dy_ref[:, 0:256]
        dcf_ref[:, 512:768] = dy * p * _dsilu(z, sz)
        dpb = (dy * (z * sz)).astype(BF)
        dpw_ref[...] += _dot_tn(u3b, dpb)
        du2 = _dot_nt(dpb, pw_ref[...]) * _dsilu(u2, s2)
        dlg_ref[...] += jnp.sum(du2 * n, axis=0, keepdims=True)
        dlb_ref[...] += jnp.sum(du2, axis=0, keepdims=True)
        dn = du2 * lng
        du1 = rstd * (dn - jnp.mean(dn, axis=-1, keepdims=True) - n * jnp.mean(dn * n, axis=-1, keepdims=True))
        db_ref[...] += jnp.sum(du1, axis=0, keepdims=True)
        du0 = transposed_conv(du1, carry_du, dw_ref, CF_TAPS_T)
        for k, tap_grad in enumerate(_tap_sums(dubuf, usbuf, CF_TAPS, ts)):
            ddw_ref[k:k + 1, :] += tap_grad
        sg = _sigmoid(g)
        dcf_ref[:, 0:256] = du0 * sg
        dcf_ref[:, 256:512] = du0 * a * sg * (1.0 - sg)

        bb = sc_ref[:, 0:256]
        c = sc_ref[:, 256:512]
        xs = sc_ref[:, 512:768]
        zs = sc_ref[:, 768:1024]
        cv = _shortconv_fwd(c, xs, hc_ref[...], hx_ref[...], first, ubuf, usbuf, obuf, sdw_ref)
        szs = _sigmoid(zs)
        dys = dy_ref[:, 256:512]
        gate = zs * szs
        dsc_ref[:, 0:256] = dys * cv * gate
        dsc_ref[:, 768:1024] = dys * bb * cv * _dsilu(zs, szs)
        dm = transposed_conv(dys * bb * gate, carry_dc, sdw_ref, SC_TAPS_T)
        for k, tap_grad in enumerate(_tap_sums(dubuf, usbuf, SC_TAPS, ts)):
            dsdw_ref[k:k + 1, :] += tap_grad
        dsc_ref[:, 256:512] = dm * xs
        dsc_ref[:, 512:768] = dm * c

    outt = lambda n: pl.BlockSpec((None, ts, n), lambda b, i: (b, tile(i), 0))
    return pl.pallas_call(
        body, grid=(bsz, nt),
        in_specs=[cur(768), halo(0), halo(1), cur(1024), halo(1), halo(2), cur(512),
                  full(CFW, DC), full(1, DC), full(1, DC), full(1, DC), full(DC, DC), full(SCW, DC)],
        out_specs=[outt(768), outt(1024), full(CFW, DC), full(1, DC), full(1, DC), full(1, DC), full(DC, DC),
                   full(SCW, DC)],
        out_shape=[jax.ShapeDtypeStruct((bsz, seq, 768), F32), jax.ShapeDtypeStruct((bsz, seq, 1024), F32),
                   jax.ShapeDtypeStruct((CFW, DC), F32), jax.ShapeDtypeStruct((1, DC), F32),
                   jax.ShapeDtypeStruct((1, DC), F32), jax.ShapeDtypeStruct((1, DC), F32),
                   jax.ShapeDtypeStruct((DC, DC), F32), jax.ShapeDtypeStruct((SCW, DC), F32)],
        scratch_shapes=[pltpu.VMEM((HALO + ts, DC), F32), pltpu.VMEM((SUBLANES, HALO + ts, DC), F32),
                        pltpu.VMEM((ts, DC), F32), pltpu.VMEM((ts + HALO, DC), F32),
                        pltpu.VMEM((SUBLANES, ts + HALO, DC), F32), pltpu.VMEM((HALO, DC), F32),
                        pltpu.VMEM((HALO, DC), F32)],
        compiler_params=_cparams(("arbitrary", "arbitrary")), name=name,
    )(pcf, pcf, pcf, psc, psc, psc, dycs, cf_dw, cf_dw_b, ln_g, ln_b, cf_pw, sc_dw)


def _bwd_inproj(x, g, dyres, w, dcf, dsc, dq, dk, dv, dz, df, name):
    t = x.shape[0]
    tm = 256
    pieces = ((C_CF, 768), (C_SC, 1024), (C_Q, DA), (C_K, DA), (C_V, DA), (C_Z, DA), (C_F, LANES))

    def body(x_ref, g_ref, dy_ref, w_hbm, dcf_ref, dsc_ref, dq_ref, dk_ref, dv_ref, dz_ref, df_ref,
             dx_ref, dg_ref, dw_hbm, w_vmem, dw_acc, sem):
        i = pl.program_id(0)

        @pl.when(i == 0)
        def _():
            cp = pltpu.make_async_copy(w_hbm, w_vmem, sem)
            cp.start()
            dw_acc[...] = jnp.zeros_like(dw_acc)
            dg_ref[...] = jnp.zeros_like(dg_ref)
            cp.wait()

        xt = x_ref[...]
        gg = g_ref[...]
        r = lax.rsqrt(jnp.mean(xt * xt, axis=-1, keepdims=True) + EPS)
        xhat = xt * r
        ht = (xhat * gg).astype(BF).T
        dh = jnp.zeros((tm, D), F32)
        for (c0, n), ref in zip(pieces, (dcf_ref, dsc_ref, dq_ref, dk_ref, dv_ref, dz_ref, df_ref)):
            for s0 in range(0, n, 512):
                s1 = min(s0 + 512, n)
                d = ref[:, s0:s1].astype(BF)
                dh = dh + _dot_nt(d, w_vmem[:, c0 + s0:c0 + s1])
                dw_acc[:, c0 + s0:c0 + s1] += _dot(ht, d)
        dg_ref[...] += jnp.sum(dh * xhat, axis=0, keepdims=True)
        dhg = dh * gg
        dx_ref[...] = dy_ref[...] + r * (dhg - xhat * jnp.mean(dhg * xhat, axis=-1, keepdims=True))

        @pl.when(i == pl.num_programs(0) - 1)
        def _():
            out = pltpu.make_async_copy(dw_acc, dw_hbm, sem)
            out.start()
            out.wait()

    row = lambda n: pl.BlockSpec((tm, n), lambda i: (i, 0))
    anyspec = pl.BlockSpec(memory_space=pl.ANY)
    return pl.pallas_call(
        body, grid=(t // tm,),
        in_specs=[row(D), pl.BlockSpec((1, D), lambda i: (0, 0)), row(D), anyspec,
                  row(768), row(1024), row(DA), row(DA), row(DA), row(DA), row(LANES)],
        out_specs=[row(D), pl.BlockSpec((1, D), lambda i: (0, 0)), anyspec],
        out_shape=[jax.ShapeDtypeStruct((t, D), F32), jax.ShapeDtypeStruct((1, D), F32),
                   jax.ShapeDtypeStruct((D, NP), F32)],
        scratch_shapes=[pltpu.VMEM((D, NP), BF), pltpu.VMEM((D, NP), F32), pltpu.SemaphoreType.DMA],
        compiler_params=_cparams(("arbitrary",)), name=name)(x, g, dyres, w, dcf, dsc, dq, dk, dv, dz, df)


def _adamw(w, g, m, v, tr, name):
    rows, cols = w.shape

    def body(w_ref, g_ref, m_ref, v_ref, d_ref, m2_ref, v2_ref):
        gt = g_ref[...]
        m2 = ADAM_B1 * m_ref[...] + (1.0 - ADAM_B1) * gt
        v2 = ADAM_B2 * v_ref[...] + (1.0 - ADAM_B2) * (gt * gt)
        m_hat = m2 / (1.0 - ADAM_B1 ** ADAM_STEP)
        v_hat = v2 / (1.0 - ADAM_B2 ** ADAM_STEP)
        d_ref[...] = -ADAM_LR * (m_hat / (jnp.sqrt(v_hat) + ADAM_EPS) + ADAM_WD * w_ref[...])
        m2_ref[...] = m2
        v2_ref[...] = v2

    blk = pl.BlockSpec((tr, cols), lambda i: (i, 0))
    return pl.pallas_call(
        body, grid=(rows // tr,), in_specs=[blk] * 4, out_specs=[blk] * 3,
        out_shape=[jax.ShapeDtypeStruct((rows, cols), F32)] * 3,
        compiler_params=_cparams(("arbitrary",)), name=name)(w, g, m, v)


def _place():
    x, y, c = lax.axis_index("x"), lax.axis_index("y"), lax.axis_index("c")
    chips = [(1 - x, y), (x, 1 - y), (1 - x, 1 - y)]
    return x, y, c, chips


def _halves(c, rows, align):
    rh = rows // 2
    return pl.ds(pl.multiple_of(c * rh, align), rh), pl.ds(pl.multiple_of((1 - c) * rh, align), rh)


ANY = pl.BlockSpec(memory_space=pl.ANY)


class _Exchange:
    def __init__(self, operands, out_shape, sems, start, wait):
        self.operands, self.out_shape, self.sems, self.start, self.wait = operands, out_shape, sems, start, wait


class _StagedCopy:
    def __init__(self, src, dst, stage, sem):
        self.src, self.dst, self.stage, self.sem = src, dst, stage, sem

    def start(self):
        pltpu.make_async_copy(self.src, self.stage, self.sem).start()

    def finish(self):
        pltpu.make_async_copy(self.src, self.stage, self.sem).wait()
        out = pltpu.make_async_copy(self.stage, self.dst, self.sem)
        out.start()
        out.wait()


def _run_exchange(ex, name):
    n_in, n_out = len(ex.operands), len(ex.out_shape)

    def body(*refs):
        ins, outs, sems = refs[:n_in], refs[n_in:n_in + n_out], refs[n_in + n_out:]
        ex.start(ins, outs, sems)
        ex.wait(ins, outs, sems)

    return pl.pallas_call(body, in_specs=[ANY] * n_in, out_specs=[ANY] * n_out, out_shape=ex.out_shape,
                          scratch_shapes=ex.sems, name=name)(*ex.operands)


def _host_call(body, grid, in_specs, out_specs, out_shape, scratch, operands, name, ex=None):
    sem = ("arbitrary",) * len(grid)
    if ex is None:
        outs = pl.pallas_call(body, grid=grid, in_specs=in_specs, out_specs=out_specs, out_shape=out_shape,
                              scratch_shapes=scratch, compiler_params=_cparams(sem), name=name)(*operands)
        return outs, None
    n_in, n_out, n_scr = len(in_specs), len(out_specs), len(scratch)
    xi, xo = len(ex.operands), len(ex.out_shape)

    def hosted(*refs):
        ins, xins = refs[:n_in], refs[n_in:n_in + xi]
        o0 = n_in + xi
        outs, xouts = refs[o0:o0 + n_out], refs[o0 + n_out:o0 + n_out + xo]
        s0 = o0 + n_out + xo
        scr, xsems = refs[s0:s0 + n_scr], refs[s0 + n_scr:]
        first = pl.program_id(0) == 0
        last = pl.program_id(0) == pl.num_programs(0) - 1
        for d in range(1, len(grid)):
            first = first & (pl.program_id(d) == 0)
            last = last & (pl.program_id(d) == pl.num_programs(d) - 1)

        @pl.when(first)
        def _():
            ex.start(xins, xouts, xsems)

        body(*ins, *outs, *scr)

        @pl.when(last)
        def _():
            ex.wait(xins, xouts, xsems)

    res = pl.pallas_call(
        hosted, grid=grid, in_specs=list(in_specs) + [ANY] * xi, out_specs=list(out_specs) + [ANY] * xo,
        out_shape=list(out_shape) + list(ex.out_shape), scratch_shapes=list(scratch) + list(ex.sems),
        compiler_params=_cparams(sem), name=name)(*operands, *ex.operands)
    return res[:n_out], res[n_out:]


def _all_gather_chips(shards):
    n = len(shards)

    def copies(srcs, outs, sems):
        send_sems, recv_sems, local_sems, *stage = sems
        x, y, c, chips = _place()
        me = 2 * x + y
        halves = [_halves(c, s.shape[1], 16) for s in shards]
        local = [_StagedCopy(srcs[i], outs[i].at[:, me], stage[i], local_sems.at[i]) for i in range(n)]

        def ici(i, j, chip, k):
            mine = halves[i][0]
            return pltpu.make_async_remote_copy(
                src_ref=srcs[i].at[:, mine, :], dst_ref=outs[i].at[:, k, mine, :], send_sem=send_sems.at[6 * i + j],
                recv_sem=recv_sems.at[6 * i + j], device_id=(*chip, c), device_id_type=MESH)

        def d2d(i, j, k, half):
            return pltpu.make_async_remote_copy(
                src_ref=outs[i].at[:, k, half, :], dst_ref=outs[i].at[:, k, half, :], send_sem=send_sems.at[6 * i + 3 + j],
                recv_sem=recv_sems.at[6 * i + 3 + j], device_id=(x, y, 1 - c), device_id_type=MESH)

        return me, chips, halves, local, ici, d2d

    def start(srcs, outs, sems):
        me, chips, _, local, ici, _ = copies(srcs, outs, sems)
        for cp in local:
            cp.start()
        for i in range(n):
            for j, chip in enumerate(chips):
                ici(i, j, chip, me).start()

    def wait(srcs, outs, sems):
        me, chips, halves, local, ici, d2d = copies(srcs, outs, sems)
        passed = []
        for i in range(n):
            for j, (px, py) in enumerate(chips):
                k = 2 * px + py
                ici(i, j, (px, py), k).wait_recv()
                fw = d2d(i, j, k, halves[i][0])
                fw.start()
                passed.append(fw)
        for i in range(n):
            for j, (px, py) in enumerate(chips):
                d2d(i, j, 2 * px + py, halves[i][1]).wait_recv()
        for i in range(n):
            for j, chip in enumerate(chips):
                ici(i, j, chip, me).wait_send()
        for cp in passed:
            cp.wait_send()
        for cp in local:
            cp.finish()

    return _Exchange(
        list(shards), [jax.ShapeDtypeStruct((s.shape[0], 4) + s.shape[1:], s.dtype) for s in shards],
        [pltpu.SemaphoreType.DMA((6 * n,)), pltpu.SemaphoreType.DMA((6 * n,)), pltpu.SemaphoreType.DMA((n,))]
        + [pltpu.VMEM(s.shape, s.dtype) for s in shards], start, wait)


def _rs_swap_halves(gs):
    n = len(gs)

    def copies(srcs, gots, sems):
        send_sems, recv_sems = sems
        x, y, c, _ = _place()
        return [pltpu.make_async_remote_copy(
            src_ref=srcs[i].at[:, :, _halves(c, gs[i].shape[2], 8)[1], :], dst_ref=gots[i], send_sem=send_sems.at[i],
            recv_sem=recv_sems.at[i], device_id=(x, y, 1 - c), device_id_type=MESH) for i in range(n)]

    def start(srcs, gots, sems):
        for cp in copies(srcs, gots, sems):
            cp.start()

    def wait(srcs, gots, sems):
        for cp in copies(srcs, gots, sems):
            cp.wait()

    return _Exchange(
        list(gs), [jax.ShapeDtypeStruct(g.shape[:2] + (g.shape[2] // 2, g.shape[3]), g.dtype) for g in gs],
        [pltpu.SemaphoreType.DMA((n,)), pltpu.SemaphoreType.DMA((n,))], start, wait)


def _rs_add_pair(g, got, cidx, tb, out_dtype, name):
    nl, _, rh, cols = got.shape
    nb = rh // tb

    def body(c_ref, g_ref, o_ref, s_ref):
        s_ref[...] = (g_ref[...] + o_ref[...]).astype(out_dtype)

    blk = lambda half: pl.BlockSpec((None, None, tb, cols), (lambda l, k, i, c: (l, k, c[0] * nb + i, 0)) if half
                                    else (lambda l, k, i, c: (l, k, i, 0)))
    return pl.pallas_call(
        body,
        grid_spec=pltpu.PrefetchScalarGridSpec(num_scalar_prefetch=1, grid=(nl, 4, nb),
                                               in_specs=[blk(True), blk(False)], out_specs=blk(False)),
        out_shape=jax.ShapeDtypeStruct(got.shape, out_dtype),
        compiler_params=_cparams(("arbitrary", "arbitrary", "arbitrary")), name=name)(cidx, g, got)


def _rs_exchange_chips(pairs):
    n = len(pairs)

    def copies(srcs, gots, sems):
        send_sems, recv_sems, local_sems, *stage = sems
        x, y, c, chips = _place()
        me = 2 * x + y
        local = [_StagedCopy(srcs[i].at[:, me], gots[i].at[:, me], stage[i], local_sems.at[i]) for i in range(n)]

        def ici(i, j, chip, frm, to):
            return pltpu.make_async_remote_copy(
                src_ref=srcs[i].at[:, to], dst_ref=gots[i].at[:, frm], send_sem=send_sems.at[3 * i + j],
                recv_sem=recv_sems.at[3 * i + j], device_id=(*chip, c), device_id_type=MESH)

        sent = [ici(i, j, (px, py), me, 2 * px + py) for i in range(n) for j, (px, py) in enumerate(chips)]
        recvd = [ici(i, j, (px, py), 2 * px + py, me) for i in range(n) for j, (px, py) in enumerate(chips)]
        return local, sent, recvd

    def start(srcs, gots, sems):
        local, sent, _ = copies(srcs, gots, sems)
        for cp in local + sent:
            cp.start()

    def wait(srcs, gots, sems):
        local, sent, recvd = copies(srcs, gots, sems)
        for cp in recvd:
            cp.wait_recv()
        for cp in sent:
            cp.wait_send()
        for cp in local:
            cp.finish()

    return _Exchange(
        list(pairs), [jax.ShapeDtypeStruct(p.shape, p.dtype) for p in pairs],
        [pltpu.SemaphoreType.DMA((3 * n,)), pltpu.SemaphoreType.DMA((3 * n,)), pltpu.SemaphoreType.DMA((n,))]
        + [pltpu.VMEM((p.shape[0],) + p.shape[2:], p.dtype) for p in pairs], start, wait)


def _rs_add_chips(got, tb, name):
    nl, _, rh, cols = got.shape

    def body(g_ref, s_ref):
        s_ref[...] = ((g_ref[0].astype(F32) + g_ref[1].astype(F32)) + g_ref[2].astype(F32)) + g_ref[3].astype(F32)

    return pl.pallas_call(
        body, grid=(nl, rh // tb),
        in_specs=[pl.BlockSpec((None, 4, tb, cols), lambda l, i: (l, 0, i, 0))],
        out_specs=pl.BlockSpec((None, tb, cols), lambda l, i: (l, i, 0)),
        out_shape=jax.ShapeDtypeStruct((nl, rh, cols), F32),
        compiler_params=_cparams(("arbitrary", "arbitrary")), name=name)(got)


def _rs_join_halves(totals):
    n = len(totals)

    def copies(srcs, outs, sems):
        send_sems, recv_sems, local_sems, *stage = sems
        x, y, c, _ = _place()
        local, send, recv = [], [], []
        for i in range(n):
            mine, other = _halves(c, 2 * totals[i].shape[1], 8)
            local.append(_StagedCopy(srcs[i], outs[i].at[:, mine, :], stage[i], local_sems.at[i]))
            for lst, rows in ((send, mine), (recv, other)):
                lst.append(pltpu.make_async_remote_copy(
                    src_ref=srcs[i], dst_ref=outs[i].at[:, rows, :], send_sem=send_sems.at[i], recv_sem=recv_sems.at[i],
                    device_id=(x, y, 1 - c), device_id_type=MESH))
        return local, send, recv

    def start(srcs, outs, sems):
        local, send, _ = copies(srcs, outs, sems)
        for cp in local + send:
            cp.start()

    def wait(srcs, outs, sems):
        local, send, recv = copies(srcs, outs, sems)
        for cp in recv:
            cp.wait_recv()
        for cp in send:
            cp.wait_send()
        for cp in local:
            cp.finish()

    return _Exchange(
        list(totals), [jax.ShapeDtypeStruct((t.shape[0], 2 * t.shape[1], t.shape[2]), F32) for t in totals],
        [pltpu.SemaphoreType.DMA((n,)), pltpu.SemaphoreType.DMA((n,)), pltpu.SemaphoreType.DMA((n,))]
        + [pltpu.VMEM(t.shape, F32) for t in totals], start, wait)


def _rows(a):
    return a.reshape(-1, LANES)


def _pad_rows(a, mult):
    r = (-a.shape[0]) % mult
    return a if r == 0 else jnp.concatenate([a, jnp.zeros((r, a.shape[1]), a.dtype)], axis=0)


def kernel(x, norm_g, w_in, b_f, cf_dw, cf_dw_b, cf_ln_g, cf_ln_b, cf_pw, sc_dw, q_norm_g, k_norm_g, w_out, loss_target, m_norm_g, m_w_in, m_b_f, m_cf_dw, m_cf_dw_b, m_cf_ln_g, m_cf_ln_b, m_cf_pw, m_sc_dw, m_q_norm_g, m_k_norm_g, m_w_out, v_norm_g, v_w_in, v_b_f, v_cf_dw, v_cf_dw_b, v_cf_ln_g, v_cf_ln_b, v_cf_pw, v_sc_dw, v_q_norm_g, v_k_norm_g, v_w_out):
    bsz, seq, _ = x.shape
    t = bsz * seq

    taps = jnp.concatenate([cf_dw.reshape(-1), sc_dw.reshape(-1)])
    taps_bf = lax.bitcast_convert_type(taps, jnp.bfloat16).reshape(-1, LANES)
    n_pw, n_taps = 2 * 64 * DC // LANES, taps_bf.shape[0]
    small_w = _pad_rows(jnp.concatenate([_rows(cf_pw.astype(BF)), taps_bf], axis=0), 32)[None]
    win_b, wout_b = w_in.astype(BF), w_out.astype(BF)
    win0_g, small_g = _run_exchange(_all_gather_chips([win_b[0:1], small_w]), "all_gather_first")
    gather_rest = _all_gather_chips([win_b[1:DEPTH], wout_b])
    full_w_in = lambda g: jnp.pad(jnp.concatenate([g[k] for k in range(4)], axis=-1), ((0, 0), (0, NP - N_IN)))
    w_in_full = [full_w_in(win0_g[0])] + [None] * (DEPTH - 1)
    w_out_full = None
    cf_pw_full = jnp.concatenate([small_g[0, k, 0:n_pw].reshape(DEPTH, 64, DC) for k in range(4)], axis=1)
    taps_all = [lax.bitcast_convert_type(small_g[0, k, n_pw:n_pw + n_taps].reshape(-1, 2), F32) for k in range(4)]
    n_cfdw = DEPTH * CFW * 64
    cf_dw_full = jnp.concatenate([tk[:n_cfdw].reshape(DEPTH, CFW, 64) for tk in taps_all], axis=-1)
    sc_dw_full = jnp.concatenate([tk[n_cfdw:].reshape(DEPTH, SCW, 64) for tk in taps_all], axis=-1)

    bf_pad = jnp.pad(b_f, ((0, 0), (0, LANES - NH)))

    xs = [x.reshape(t, D)]
    saved = []
    dy = loss_part = None
    for l in range(DEPTH):
        xl = xs[-1]
        gq, gk = q_norm_g[l].reshape(1, DA), k_norm_g[l].reshape(1, DA)
        pcf, psc, pq, pk, vb, pz, pf = _fwd_inproj(xl, norm_g[l][None], w_in_full[l], f"fwd_inproj_{l}")
        b3 = lambda a: a.reshape(bsz, seq, a.shape[-1])
        ycs = _fwd_conv(b3(pcf), b3(psc), cf_dw_full[l], cf_dw_b[l][None], cf_ln_g[l][None], cf_ln_b[l][None],
                        cf_pw_full[l], sc_dw_full[l], f"fwd_conv_{l}")
        qs, kn, cq, crow8 = _fwd_attn_prep(b3(pq), b3(pk), b3(pf), gq, gk, bf_pad[l][None], f"fwd_attn_prep_{l}")
        crow = crow8.reshape(bsz, NH // 2, 2, seq)
        (o, ohp, yatt, lse), landed = _fwd_attn(qs, kn, b3(vb), cq, crow, b3(pz), f"fwd_attn_{l}",
                                                gather_rest if l == 0 else None)
        if l == 0:
            win_rest_g, wout_g = landed
            w_in_full[1:] = [full_w_in(win_rest_g[i]) for i in range(DEPTH - 1)]
            w_out_full = wout_g.reshape(DEPTH, D, D)
        ycs2, yatt2 = ycs.reshape(t, 512), yatt.reshape(t, DA)
        if l + 1 < DEPTH:
            xs.append(_fwd_outproj(xl, ycs2, yatt2, w_out_full[l], f"fwd_outproj_{l}"))
        else:
            dy, loss_part = _fwd_outproj_loss(xl, ycs2, yatt2, w_out_full[l], loss_target.reshape(t, D),
                                              f"fwd_outproj_loss_{l}")
        saved.append((pcf, psc, pq, pk, vb, pz, pf, ycs2, yatt2, o, ohp, qs, kn, cq, crow, lse))

    grads = [None] * DEPTH
    reduced = [None] * DEPTH
    cidx = lax.axis_index("c").astype(jnp.int32).reshape(1)
    big_tiles = [256, 128]

    def big_packs(gl):
        return [jnp.stack([gl["w_in"][:, NSH * k:NSH * (k + 1)] for k in range(4)])[None],
                gl["w_out"].reshape(1, 4, 256, D)]

    def add_pairs(packs, gots, tiles, dtypes, tag):
        return [_rs_add_pair(p, got, cidx, tb, dt, f"rs_add_pair_{tag}_{i}")
                for i, (p, got, tb, dt) in enumerate(zip(packs, gots, tiles, dtypes))]

    def add_chips(gots2, tiles, tag):
        return [_rs_add_chips(got, tb, f"rs_add_chips_{tag}_{i}") for i, (got, tb) in enumerate(zip(gots2, tiles))]

    for l in reversed(range(DEPTH)):
        pcf, psc, pq, pk, vb, pz, pf, ycs2, yatt2, o, ohp, qs, kn, cq, crow, lse = saved[l]
        gq, gk = q_norm_g[l].reshape(1, DA), k_norm_g[l].reshape(1, DA)
        b3 = lambda a: a.reshape(bsz, seq, a.shape[-1])
        f2 = lambda a: a.reshape(t, a.shape[-1])
        packs = big_packs(grads[l + 1]) if l + 1 < DEPTH else None
        (dycs, do, dz, dl, d_wout), gots = _bwd_outproj(
            dy, ycs2, yatt2, f2(o), f2(ohp), pz, w_out_full[l], f"bwd_outproj_{l}",
            _rs_swap_halves(packs) if packs else None)
        pairs = add_pairs(packs, gots, big_tiles, [BF, BF], l + 1) if packs else None
        (dqs, dkn, dv, dcrow), gots2 = _bwd_attn(qs, kn, b3(vb), b3(do), cq, lse, b3(dl), crow, f"bwd_attn_{l}",
                                                 _rs_exchange_chips(pairs) if packs else None)
        totals = add_chips(gots2, big_tiles, l + 1) if packs else None
        (dpq, dpk, dpf, d_gq, d_gk, d_bf), joined = _bwd_attn_post(
            b3(pq), b3(pk), dqs, dkn, b3(pf), dcrow.reshape(bsz, NH, seq), gq, gk, bf_pad[l][None],
            f"bwd_attn_post_{l}", _rs_join_halves(totals) if packs else None)
        if packs:
            reduced[l + 1] = joined
        dcf, dsc, d_cfdw, d_cfb, d_lng, d_lnb, d_pw, d_scdw = _bwd_conv(
            b3(pcf), b3(psc), b3(dycs), cf_dw_full[l], cf_dw_b[l][None], cf_ln_g[l][None], cf_ln_b[l][None],
            cf_pw_full[l], sc_dw_full[l], f"bwd_conv_{l}")
        dy, d_ng, d_win = _bwd_inproj(xs[l], norm_g[l][None], dy, w_in_full[l], f2(dcf), f2(dsc), f2(dpq), f2(dpk),
                                      f2(dv), dz, f2(dpf), f"bwd_inproj_{l}")
        grads[l] = dict(norm_g=d_ng[0], w_in=d_win, b_f=d_bf[0, :NH], cf_dw=d_cfdw, cf_dw_b=d_cfb[0],
                        cf_ln_g=d_lng[0], cf_ln_b=d_lnb[0], cf_pw=d_pw, sc_dw=d_scdw,
                        q_norm_g=d_gq.reshape(NH, HD), k_norm_g=d_gk.reshape(NH, HD), w_out=d_wout)
    grad_x = dy.reshape(bsz, seq, D)
    gw = {n: jnp.stack([grads[l][n] for l in range(DEPTH)]) for n in grads[0] if n not in ("w_in", "w_out")}

    rep_names = ("norm_g", "b_f", "cf_dw_b", "cf_ln_g", "cf_ln_b", "q_norm_g", "k_norm_g")
    rep = jnp.concatenate([jnp.pad(gw[n].reshape(-1), (0, (-gw[n].size) % LANES)) for n in rep_names]
                          + [jnp.pad(loss_part.reshape(-1), (0, LANES - 1))]).reshape(-1, LANES)
    blocks = []
    for k in range(4):
        small = jnp.concatenate([gw["cf_dw"][:, :, 64 * k:64 * (k + 1)].reshape(-1),
                                 gw["sc_dw"][:, :, 64 * k:64 * (k + 1)].reshape(-1)]).reshape(-1, LANES)
        blocks.append(_pad_rows(jnp.concatenate([_rows(gw["cf_pw"][:, 64 * k:64 * (k + 1), :]), small, rep], axis=0), 16))
    g_small = jnp.stack(blocks)[None]
    packs = big_packs(grads[0]) + [g_small]
    tiles = big_tiles + [g_small.shape[2] // 2]
    gots = _run_exchange(_rs_swap_halves(packs), "rs_swap_halves_last")
    pairs = add_pairs(packs, gots, tiles, [BF, BF, F32], 0)
    gots2 = _run_exchange(_rs_exchange_chips(pairs), "rs_exchange_chips_last")
    red_win0, red_wout0, red_small = _run_exchange(_rs_join_halves(add_chips(gots2, tiles, 0)), "rs_join_halves_last")
    reduced[0] = (red_win0, red_wout0)

    n_small = (DEPTH * CFW * 64 + DEPTH * SCW * 64) // LANES
    red = red_small[0]
    pos = [0]

    def take(nrows):
        pos[0] += nrows
        return red[pos[0] - nrows:pos[0]]

    g = {}
    g["w_in"] = jnp.concatenate([reduced[l][0] for l in range(DEPTH)], axis=0)
    g["w_out"] = jnp.concatenate([reduced[l][1] for l in range(DEPTH)], axis=0)
    g["cf_pw"] = take(n_pw).reshape(DEPTH, 64, DC)
    small = take(n_small).reshape(-1)
    g["cf_dw"] = small[:n_cfdw].reshape(DEPTH, CFW, 64)
    g["sc_dw"] = small[n_cfdw:].reshape(DEPTH, SCW, 64)
    shapes = dict(norm_g=norm_g.shape, b_f=b_f.shape, cf_dw_b=cf_dw_b.shape, cf_ln_g=cf_ln_g.shape,
                  cf_ln_b=cf_ln_b.shape, q_norm_g=q_norm_g.shape, k_norm_g=k_norm_g.shape)
    for n in rep_names:
        size = 1
        for s in shapes[n]:
            size *= s
        g[n] = take(-(-size // LANES)).reshape(-1)[:size].reshape(shapes[n])
    loss = take(1)[0, 0]

    order = ("norm_g", "w_in", "b_f", "cf_dw", "cf_dw_b", "cf_ln_g", "cf_ln_b", "cf_pw", "sc_dw", "q_norm_g",
             "k_norm_g", "w_out")
    weights = dict(norm_g=norm_g, w_in=w_in, b_f=b_f, cf_dw=cf_dw, cf_dw_b=cf_dw_b, cf_ln_g=cf_ln_g, cf_ln_b=cf_ln_b,
                   cf_pw=cf_pw, sc_dw=sc_dw, q_norm_g=q_norm_g, k_norm_g=k_norm_g, w_out=w_out)
    ms = dict(norm_g=m_norm_g, w_in=m_w_in, b_f=m_b_f, cf_dw=m_cf_dw, cf_dw_b=m_cf_dw_b, cf_ln_g=m_cf_ln_g,
              cf_ln_b=m_cf_ln_b, cf_pw=m_cf_pw, sc_dw=m_sc_dw, q_norm_g=m_q_norm_g, k_norm_g=m_k_norm_g, w_out=m_w_out)
    vs = dict(norm_g=v_norm_g, w_in=v_w_in, b_f=v_b_f, cf_dw=v_cf_dw, cf_dw_b=v_cf_dw_b, cf_ln_g=v_cf_ln_g,
              cf_ln_b=v_cf_ln_b, cf_pw=v_cf_pw, sc_dw=v_sc_dw, q_norm_g=v_q_norm_g, k_norm_g=v_k_norm_g, w_out=v_w_out)
    delta, new_m, new_v = {}, {}, {}
    two_d = lambda a: a.reshape(-1, a.shape[-1])
    for n in ("w_in", "w_out"):
        outs = _adamw(two_d(weights[n]), two_d(g[n]), two_d(ms[n]), two_d(vs[n]), 256, f"adamw_{n}")
        delta[n], new_m[n], new_v[n] = (a.reshape(weights[n].shape) for a in outs)
    small_names = [n for n in order if n not in ("w_in", "w_out")]
    flat = lambda d: _pad_rows(jnp.pad(jnp.concatenate([d[n].reshape(-1) for n in small_names]),
                                       (0, (-sum(weights[n].size for n in small_names)) % LANES)).reshape(-1, LANES), 8)
    fw = flat(weights)
    outs = _adamw(fw, flat(g), flat(ms), flat(vs), fw.shape[0], "adamw_small")
    off = 0
    for n in small_names:
        size = weights[n].size
        for dst, src in zip((delta, new_m, new_v), outs):
            dst[n] = src.reshape(-1)[off:off + size].reshape(weights[n].shape)
        off += size

    return (loss, grad_x, *[g[n] for n in order], *[delta[n] for n in order], *[new_m[n] for n in order],
            *[new_v[n] for n in order])
```

```python
import jax
import jax.numpy as jnp
from jax import lax
from jax.experimental import pallas as pl
from jax.experimental.pallas import tpu as pltpu

F32 = jnp.float32
BF = jnp.bfloat16
MESH = pl.DeviceIdType.MESH

DEPTH = 2
D = 1024
DC = 256
DA = 512
NH = 8
HD = 64
CFW = 31
SCW = 3
N_IN = 3848
NP = 3968
NSH = N_IN // 4
HALO = 32
ATT_ROWS = 32
FWD_ROWS = 256
SUBLANES = 8
CONV_ROWS = 32
TAP_GROUP = 8
EPS = 1e-6
LANES = 128
VMEM_LIMIT = 56 * 1024 * 1024

C_CF, C_SC, C_Q, C_K, C_V, C_Z, C_F = 0, 768, 1792, 2304, 2816, 3328, 3840

ADAM_LR = 0.001
ADAM_B1 = 0.9
ADAM_B2 = 0.999
ADAM_EPS = 1e-08
ADAM_WD = 0.01
ADAM_STEP = 10


def _cparams(sem=None):
    return pltpu.CompilerParams(dimension_semantics=sem, vmem_limit_bytes=VMEM_LIMIT)


def _dot(a, b):
    return jnp.dot(a, b, preferred_element_type=F32)


def _dot_nt(a, b):
    return lax.dot_general(a, b, (((1,), (1,)), ((), ())), preferred_element_type=F32)


def _dot_tn(a, b):
    return lax.dot_general(a, b, (((0,), (0,)), ((), ())), preferred_element_type=F32)


def _split3(x):
    hi = x.astype(BF)
    r1 = x - hi.astype(F32)
    mid = r1.astype(BF)
    lo = (r1 - mid.astype(F32)).astype(BF)
    return hi, mid, lo


def _dot_exact(a_bf, x):
    hi, mid, lo = _split3(x)
    return _dot(a_bf, hi) + _dot(a_bf, mid) + _dot(a_bf, lo)


def _sigmoid(x):
    return 1.0 / (1.0 + jnp.exp(-x))


def _seg_sum64(x):
    i = lax.broadcasted_iota(jnp.int32, (LANES, LANES), 0)
    j = lax.broadcasted_iota(jnp.int32, (LANES, LANES), 1)
    g = ((i >= HD) == (j >= HD)).astype(BF)
    hi, mid, lo = _split3(x)
    return _dot(hi, g) + _dot(mid, g) + _dot(lo, g)


def _fwd_inproj(x, g, w, name):
    t = x.shape[0]
    tm = 256

    def body(x_ref, g_ref, w_ref, cf_ref, sc_ref, q_ref, k_ref, v_ref, z_ref, f_ref):
        xt = x_ref[...]
        r = lax.rsqrt(jnp.mean(xt * xt, axis=-1, keepdims=True) + EPS)
        h = ((xt * r) * g_ref[...]).astype(BF)
        cf_ref[:, 0:512] = _dot(h, w_ref[:, 0:512])
        cf_ref[:, 512:768] = _dot(h, w_ref[:, 512:768])
        sc_ref[:, 0:512] = _dot(h, w_ref[:, C_SC:C_SC + 512])
        sc_ref[:, 512:1024] = _dot(h, w_ref[:, C_SC + 512:C_Q])
        q_ref[...] = _dot(h, w_ref[:, C_Q:C_K])
        k_ref[...] = _dot(h, w_ref[:, C_K:C_V])
        v_ref[...] = _dot(h, w_ref[:, C_V:C_Z]).astype(BF)
        z_ref[...] = _dot(h, w_ref[:, C_Z:C_F])
        f_ref[...] = _dot(h, w_ref[:, C_F:NP])

    row = lambda n: pl.BlockSpec((tm, n), lambda i: (i, 0))
    return pl.pallas_call(
        body, grid=(t // tm,),
        in_specs=[row(D), pl.BlockSpec((1, D), lambda i: (0, 0)), pl.BlockSpec((D, NP), lambda i: (0, 0))],
        out_specs=[row(768), row(1024), row(DA), row(DA), row(DA), row(DA), row(LANES)],
        out_shape=[jax.ShapeDtypeStruct((t, 768), F32), jax.ShapeDtypeStruct((t, 1024), F32),
                   jax.ShapeDtypeStruct((t, DA), F32), jax.ShapeDtypeStruct((t, DA), F32),
                   jax.ShapeDtypeStruct((t, DA), BF), jax.ShapeDtypeStruct((t, DA), F32),
                   jax.ShapeDtypeStruct((t, LANES), F32)],
        compiler_params=_cparams(("arbitrary",)), name=name)(x, g, w)


def _shift_copies(buf, sbuf, shifts):
    n = buf.shape[0]
    for sh in shifts:
        rows = n if sh == 0 else n - SUBLANES
        sbuf[sh, 0:rows, :] = buf[pl.ds(sh, rows), :]


def _tap_rows(sbuf, off, r0):
    sh = off % SUBLANES
    return sbuf[sh, pl.ds(pl.multiple_of(r0 + (off - sh), SUBLANES), CONV_ROWS), :]


def _tap_conv(sbuf, w_ref, offsets, dst, ts, bias):
    def chunk(c, _):
        r0 = pl.multiple_of(c * CONV_ROWS, CONV_ROWS)
        acc = jnp.zeros((CONV_ROWS, DC), F32) + bias
        for k, off in enumerate(offsets):
            acc = acc + w_ref[k:k + 1, :] * _tap_rows(sbuf, off, r0)
        dst[pl.ds(r0, CONV_ROWS), :] = acc
        return 0

    lax.fori_loop(0, ts // CONV_ROWS, chunk, 0)


def _tap_sums(a_buf, sbuf, offsets, ts):
    outs = []
    for g0 in range(0, len(offsets), TAP_GROUP):
        group = offsets[g0:g0 + TAP_GROUP]

        def chunk(c, accs, group=group):
            r0 = pl.multiple_of(c * CONV_ROWS, CONV_ROWS)
            a = a_buf[pl.ds(r0, CONV_ROWS), :]
            return tuple(acc + (a * _tap_rows(sbuf, off, r0)).reshape(CONV_ROWS // SUBLANES, SUBLANES, DC).sum(axis=0)
                         for acc, off in zip(accs, group))

        accs = lax.fori_loop(0, ts // CONV_ROWS, chunk, tuple(jnp.zeros((SUBLANES, DC), F32) for _ in group))
        outs += [jnp.sum(acc, axis=0, keepdims=True) for acc in accs]
    return outs


CF_TAPS = tuple(HALO - (CFW - 1) + k for k in range(CFW))
SC_TAPS = tuple(HALO - (SCW - 1) + k for k in range(SCW))
CF_TAPS_T = tuple(CFW - 1 - k for k in range(CFW))
SC_TAPS_T = tuple(SCW - 1 - k for k in range(SCW))


def _conformer_fwd(a, g, ha, hg, first, ubuf, usbuf, u1buf, dw_ref, bias, lng, lnb):
    ts = a.shape[0]
    u0 = a * _sigmoid(g)
    ubuf[0:HALO, :] = jnp.where(first, 0.0, ha * _sigmoid(hg))
    ubuf[HALO:HALO + ts, :] = u0
    _shift_copies(ubuf, usbuf, range(SUBLANES))
    _tap_conv(usbuf, dw_ref, CF_TAPS, u1buf, ts, bias)
    u1 = u1buf[...]
    mu = jnp.mean(u1, axis=-1, keepdims=True)
    xc = u1 - mu
    rstd = lax.rsqrt(jnp.mean(xc * xc, axis=-1, keepdims=True) + EPS)
    n = xc * rstd
    u2 = n * lng + lnb
    s2 = _sigmoid(u2)
    u3 = u2 * s2
    return u0, n, rstd, u2, s2, u3


def _shortconv_fwd(c, xs, hc, hx, first, mbuf, msbuf, cvbuf, dw_ref):
    ts = c.shape[0]
    mbuf[0:HALO, :] = jnp.where(first, 0.0, hc * hx)
    mbuf[HALO:HALO + ts, :] = c * xs
    _shift_copies(mbuf, msbuf, sorted({off % SUBLANES for off in SC_TAPS}))
    _tap_conv(msbuf, dw_ref, SC_TAPS, cvbuf, ts, 0.0)
    return cvbuf[...]


def _conv_specs(bsz, seq, ts, order):
    nt = seq // ts
    tile = (lambda i: i) if order > 0 else (lambda i: nt - 1 - i)
    hrow = lambda i: jnp.maximum(tile(i) * (ts // HALO) - 1, 0)
    cur = lambda n: pl.BlockSpec((None, ts, n), lambda b, i: (b, tile(i), 0))
    halo = lambda j: pl.BlockSpec((None, HALO, DC), lambda b, i: (b, hrow(i), j))
    full = lambda r, c: pl.BlockSpec((r, c), lambda b, i: (0, 0))
    return nt, tile, cur, halo, full


def _fwd_conv(pcf, psc, cf_dw, cf_dw_b, ln_g, ln_b, cf_pw, sc_dw, name):
    bsz, seq, _ = pcf.shape
    ts = 512
    nt, tile, cur, halo, full = _conv_specs(bsz, seq, ts, +1)

    def body(cf_ref, ha_ref, hg_ref, sc_ref, hc_ref, hx_ref, dw_ref, b_ref, lg_ref, lb_ref, pw_ref, sdw_ref,
             y_ref, ubuf, sbuf, obuf):
        first = pl.program_id(1) == 0
        _, _, _, _, _, u3 = _conformer_fwd(cf_ref[:, 0:256], cf_ref[:, 256:512], ha_ref[...], hg_ref[...], first,
                                           ubuf, sbuf, obuf, dw_ref, b_ref[...], lg_ref[...], lb_ref[...])
        z = cf_ref[:, 512:768]
        y_ref[:, 0:256] = _dot(u3.astype(BF), pw_ref[...]) * (z * _sigmoid(z))
        cv = _shortconv_fwd(sc_ref[:, 256:512], sc_ref[:, 512:768], hc_ref[...], hx_ref[...], first,
                            ubuf, sbuf, obuf, sdw_ref)
        zs = sc_ref[:, 768:1024]
        y_ref[:, 256:512] = sc_ref[:, 0:256] * cv * (zs * _sigmoid(zs))

    return pl.pallas_call(
        body, grid=(bsz, nt),
        in_specs=[cur(768), halo(0), halo(1), cur(1024), halo(1), halo(2),
                  full(CFW, DC), full(1, DC), full(1, DC), full(1, DC), full(DC, DC), full(SCW, DC)],
        out_specs=pl.BlockSpec((None, ts, 512), lambda b, i: (b, i, 0)),
        out_shape=jax.ShapeDtypeStruct((bsz, seq, 512), F32),
        scratch_shapes=[pltpu.VMEM((HALO + ts, DC), F32), pltpu.VMEM((SUBLANES, HALO + ts, DC), F32),
                        pltpu.VMEM((ts, DC), F32)],
        compiler_params=_cparams(("arbitrary", "arbitrary")), name=name,
    )(pcf, pcf, pcf, psc, psc, psc, cf_dw, cf_dw_b, ln_g, ln_b, cf_pw, sc_dw)


def _head_rms(xb, gb):
    ms = _seg_sum64(xb * xb) * (1.0 / HD)
    r = lax.rsqrt(ms + EPS)
    xhat = xb * r
    return xhat, r, xhat * gb


def _fwd_attn_prep(pq, pk, pf, gq, gk, bf, name):
    bsz, seq, _ = pq.shape
    ts = 512
    nt = seq // ts

    def body(q_ref, k_ref, f_ref, gq_ref, gk_ref, bf_ref, qs_ref, kn_ref, cq_ref, crow_ref, carry):
        @pl.when(pl.program_id(1) == 0)
        def _():
            carry[...] = jnp.zeros_like(carry)

        for jb in range(DA // LANES):
            sl = slice(jb * LANES, (jb + 1) * LANES)
            _, _, qn = _head_rms(q_ref[:, sl], gq_ref[:, sl])
            qs_ref[:, sl] = (qn * (1.0 / 8.0)).astype(BF)
            _, _, kn = _head_rms(k_ref[:, sl], gk_ref[:, sl])
            kn_ref[:, sl] = kn.astype(BF)

        xf = f_ref[...] + bf_ref[...]
        lf = jnp.minimum(xf, 0.0) - jnp.log(1.0 + jnp.exp(-jnp.abs(xf)))
        ti = lax.broadcasted_iota(jnp.int32, (ts, ts), 0)
        si = lax.broadcasted_iota(jnp.int32, (ts, ts), 1)
        c = _dot_exact((si <= ti).astype(BF), lf) + carry[...]
        carry[...] = c[ts - 1:ts, :]
        hj = lax.broadcasted_iota(jnp.int32, (LANES, DA), 0)
        ll = lax.broadcasted_iota(jnp.int32, (LANES, DA), 1)
        dd = ll - hj * HD
        chi, cmid, clo = _split3(c)
        e = ((dd >= 0) & (dd < HD)).astype(BF)
        cq_ref[...] = _dot(chi, e) + _dot(cmid, e) + _dot(clo, e)
        hh = lax.broadcasted_iota(jnp.int32, (16, LANES), 0)
        jj = lax.broadcasted_iota(jnp.int32, (16, LANES), 1)
        sel = (hh == jj).astype(BF)
        cr = _dot_nt(sel, chi) + _dot_nt(sel, cmid) + _dot_nt(sel, clo)
        crow_ref[...] = cr[0:NH, :]

    tile = lambda n: pl.BlockSpec((None, ts, n), lambda b, i: (b, i, 0))
    vec = lambda n: pl.BlockSpec((1, n), lambda b, i: (0, 0))
    return pl.pallas_call(
        body, grid=(bsz, nt),
        in_specs=[tile(DA), tile(DA), tile(LANES), vec(DA), vec(DA), vec(LANES)],
        out_specs=[tile(DA), tile(DA), tile(DA), pl.BlockSpec((None, NH, ts), lambda b, i: (b, 0, i))],
        out_shape=[jax.ShapeDtypeStruct((bsz, seq, DA), BF), jax.ShapeDtypeStruct((bsz, seq, DA), BF),
                   jax.ShapeDtypeStruct((bsz, seq, DA), F32), jax.ShapeDtypeStruct((bsz, NH, seq), F32)],
        scratch_shapes=[pltpu.VMEM((1, LANES), F32)],
        compiler_params=_cparams(("arbitrary", "arbitrary")), name=name)(pq, pk, pf, gq, gk, bf)


def _tile_rows(i, rows):
    return pl.ds(i * rows, rows) if isinstance(i, int) else pl.ds(pl.multiple_of(i * rows, rows), rows)


def _causal_mask(tq, tk):
    r = lax.broadcasted_iota(jnp.int32, (tq, tk), 0)
    c = lax.broadcasted_iota(jnp.int32, (tq, tk), 1)
    return r - c


def _fwd_attn(qs, kn, vb, cq, crow, pz, name, ex=None):
    bsz, seq, _ = qs.shape
    tq, tk = 256, 256
    nq, nk = seq // tq, seq // tk
    npair = NH // 2

    def body(q_ref, k_ref, v_ref, cq_ref, cr_ref, z_ref, o_ref, ohp_ref, y_ref, lse_ref,
             s_buf, p_buf, acc_ref, accl_ref):
        qi = pl.program_id(2)
        head0 = lax.broadcasted_iota(jnp.int32, (1, LANES), 1) < HD
        q = q_ref[...]
        zq = jnp.zeros_like(q)
        qm = (jnp.where(head0, q, zq), jnp.where(head0, zq, q))
        ct = (cq_ref[:, 0:1], cq_ref[:, HD:HD + 1])

        tile = lambda j: _tile_rows(j, tk)

        def scores(j, slot):
            kb = k_ref[tile(j), :]
            for hh in range(2):
                s_buf[slot, hh] = _dot_nt(qm[hh], kb) + (ct[hh] - cr_ref[hh:hh + 1, tile(j)])

        def weighted_values(slot, j, al):
            vb_t = v_ref[tile(j), :]
            acc_ref[...] = al * acc_ref[...] + jnp.where(head0, _dot(p_buf[slot, 0, 0], vb_t),
                                                         _dot(p_buf[slot, 1, 0], vb_t))
            accl_ref[...] = al * accl_ref[...] + jnp.where(head0, _dot(p_buf[slot, 0, 1], vb_t),
                                                           _dot(p_buf[slot, 1, 1], vb_t))

        def softmax(slot, first_visible, m, l):
            m_new, l_new, al_new = ([], []), ([], []), []
            for r in range(tq // FWD_ROWS):
                rows = slice(r * FWD_ROWS, (r + 1) * FWD_ROWS)
                visible = _causal_mask(FWD_ROWS, tk) >= first_visible - r * FWD_ROWS
                alphas = []
                for hh in range(2):
                    s = jnp.where(visible, s_buf[slot, hh, rows, :], -1e30)
                    m_old = m[hh][rows]
                    m2 = jnp.maximum(m_old, jnp.max(s, axis=-1, keepdims=True))
                    p = jnp.exp(s - m2)
                    alpha = jnp.exp(m_old - m2)
                    l_new[hh].append(alpha * l[hh][rows] + jnp.sum(p, axis=-1, keepdims=True))
                    m_new[hh].append(m2)
                    pb = p.astype(BF)
                    p_buf[slot, hh, 0, rows, :] = pb
                    p_buf[slot, hh, 1, rows, :] = (p - pb.astype(F32)).astype(BF)
                    alphas.append(alpha)
                al_new.append(jnp.where(head0, alphas[0], alphas[1]))
            cat = lambda parts: jnp.concatenate(parts, axis=0)
            return (cat(m_new[0]), cat(m_new[1])), (cat(l_new[0]), cat(l_new[1])), cat(al_new)

        clamp = lambda j: jnp.clip(j, 0, nk - 1)

        def step(j, slot, carry):
            m, l, al = carry
            weighted_values(1 - slot, clamp(j - 1), al)
            m, l, al = softmax(slot, j * tk - qi * tq, m, l)
            scores(clamp(j + 1), 1 - slot)
            return m, l, al

        scores(0, 0)
        p_buf[1] = jnp.zeros((2, 2, tq, tk), BF)
        acc_ref[...] = jnp.zeros((tq, LANES), F32)
        accl_ref[...] = jnp.zeros((tq, LANES), F32)
        neg = jnp.full((tq, 1), -1e30, F32)
        zcol = jnp.zeros((tq, 1), F32)
        trips = ((qi + 1) * (tq // tk) + 1) // 2
        m, l, al = lax.fori_loop(0, trips, lambda t, cy: step(2 * t + 1, 1, step(2 * t, 0, cy)),
                                 ((neg, neg), (zcol, zcol), jnp.ones((tq, LANES), F32)))
        weighted_values(1, clamp(2 * trips - 1), al)
        inv = jnp.where(head0, 1.0 / l[0], 1.0 / l[1])
        o = acc_ref[...] * inv
        z = z_ref[...]
        o_ref[...] = o
        ohp_ref[...] = (acc_ref[...] + accl_ref[...]) * inv
        y_ref[...] = o * (z * _sigmoid(z))
        lse_ref[...] = jnp.where(head0, m[0] + jnp.log(l[0]), m[1] + jnp.log(l[1]))

    qblk = pl.BlockSpec((None, tq, LANES), lambda b, h, i: (b, i, h))
    kvblk = pl.BlockSpec((None, seq, LANES), lambda b, h, i: (b, 0, h))
    return _host_call(
        body, (bsz, npair, nq),
        [qblk, kvblk, kvblk, qblk, pl.BlockSpec((None, None, 2, seq), lambda b, h, i: (b, h, 0, 0)), qblk],
        [qblk, qblk, qblk, qblk], [jax.ShapeDtypeStruct((bsz, seq, DA), F32)] * 4,
        [pltpu.VMEM((2, 2, tq, tk), F32), pltpu.VMEM((2, 2, 2, tq, tk), BF), pltpu.VMEM((tq, LANES), F32),
         pltpu.VMEM((tq, LANES), F32)],
        (qs, kn, vb, cq, crow, pz), name, ex)


def _fwd_outproj(x, ycs, yatt, wo, name):
    t = x.shape[0]
    tm = 512

    def body(x_ref, a_ref, b_ref, w_ref, o_ref):
        o_ref[...] = (x_ref[...] + _dot(a_ref[...].astype(BF), w_ref[0:512, :])
                      + _dot(b_ref[...].astype(BF), w_ref[512:1024, :]))

    row = lambda n: pl.BlockSpec((tm, n), lambda i: (i, 0))
    return pl.pallas_call(
        body, grid=(t // tm,),
        in_specs=[row(D), row(512), row(512), pl.BlockSpec((D, D), lambda i: (0, 0))],
        out_specs=row(D), out_shape=jax.ShapeDtypeStruct((t, D), F32),
        compiler_params=_cparams(("arbitrary",)), name=name)(x, ycs, yatt, wo)


def _fwd_outproj_loss(x, ycs, yatt, wo, target, name):
    t = x.shape[0]
    tm = 512

    def body(x_ref, a_ref, b_ref, w_ref, t_ref, dy_ref, loss_ref):
        @pl.when(pl.program_id(0) == 0)
        def _():
            loss_ref[...] = jnp.zeros_like(loss_ref)

        y = (x_ref[...] + _dot(a_ref[...].astype(BF), w_ref[0:512, :])
             + _dot(b_ref[...].astype(BF), w_ref[512:1024, :]))
        err = y - t_ref[...]
        dy_ref[...] = err * (1.0 / D)
        per_tok = jnp.mean(err * err, axis=-1, keepdims=True)
        loss_ref[...] += 0.5 * jnp.sum(per_tok, axis=0, keepdims=True)

    row = lambda n: pl.BlockSpec((tm, n), lambda i: (i, 0))
    return pl.pallas_call(
        body, grid=(t // tm,),
        in_specs=[row(D), row(512), row(512), pl.BlockSpec((D, D), lambda i: (0, 0)), row(D)],
        out_specs=[row(D), pl.BlockSpec((1, 1), lambda i: (0, 0))],
        out_shape=[jax.ShapeDtypeStruct((t, D), F32), jax.ShapeDtypeStruct((1, 1), F32)],
        compiler_params=_cparams(("arbitrary",)), name=name)(x, ycs, yatt, wo, target)


def _dsilu(x, s):
    return s * (1.0 + x * (1.0 - s))


def _bwd_outproj(dy, ycs, yatt, o, ohp, pz, wo, name, ex=None):
    t = dy.shape[0]
    tm = 256

    def body(dy_ref, a_ref, b_ref, o_ref, ohp_ref, z_ref, w_ref, dcs_ref, do_ref, dz_ref, dl_ref, dw_ref):
        @pl.when(pl.program_id(0) == 0)
        def _():
            dw_ref[...] = jnp.zeros_like(dw_ref)

        dyb = dy_ref[...].astype(BF)
        dw_ref[0:512, :] += _dot_tn(a_ref[...].astype(BF), dyb)
        dw_ref[512:1024, :] += _dot_tn(b_ref[...].astype(BF), dyb)
        dcs_ref[...] = _dot_nt(dyb, w_ref[0:512, :])
        dyatt = _dot_nt(dyb, w_ref[512:1024, :])
        z = z_ref[...]
        sz = _sigmoid(z)
        o_t = o_ref[...]
        dob = (dyatt * (z * sz)).astype(BF)
        do_ref[...] = dob
        dz_ref[...] = dyatt * o_t * _dsilu(z, sz)
        prod = dob.astype(F32) * ohp_ref[...]
        for jb in range(DA // LANES):
            sl = slice(jb * LANES, (jb + 1) * LANES)
            dl_ref[:, sl] = _seg_sum64(prod[:, sl])

    row = lambda n: pl.BlockSpec((tm, n), lambda i: (i, 0))
    return _host_call(
        body, (t // tm,),
        [row(D), row(512), row(512), row(DA), row(DA), row(DA), pl.BlockSpec((D, D), lambda i: (0, 0))],
        [row(512), row(DA), row(DA), row(DA), pl.BlockSpec((D, D), lambda i: (0, 0))],
        [jax.ShapeDtypeStruct((t, 512), F32), jax.ShapeDtypeStruct((t, DA), BF),
         jax.ShapeDtypeStruct((t, DA), F32), jax.ShapeDtypeStruct((t, DA), F32),
         jax.ShapeDtypeStruct((D, D), F32)], [],
        (dy, ycs, yatt, o, ohp, pz, wo), name, ex)


def _bwd_attn(qs, kn, vb, do, cq, lse, dl, crow, name, ex=None):
    bsz, seq, _ = qs.shape
    tq = 256
    nq = seq // tq
    npair = NH // 2

    def body(q_ref, k_ref, v_ref, do_ref, cq_ref, lse_ref, dl_ref, cr_ref, dq_ref, dk_ref, dv_ref, dc_ref,
             qm, dom, s_buf, dp_buf, pd_buf):
        rc = _causal_mask(tq, tq)
        head0 = lax.broadcasted_iota(jnp.int32, (1, LANES), 1) < HD
        zb = jnp.zeros((seq, LANES), BF)
        qm[0] = jnp.where(head0, q_ref[...], zb)
        qm[1] = jnp.where(head0, zb, q_ref[...])
        dom[0] = jnp.where(head0, do_ref[...], zb)
        dom[1] = jnp.where(head0, zb, do_ref[...])
        dq_ref[...] = jnp.zeros_like(dq_ref)

        def kloop(kj, _):
            krows = pl.ds(pl.multiple_of(kj * tq, tq), tq)
            kb = k_ref[krows, :]
            vb_t = v_ref[krows, :]
            zk = jnp.zeros_like(kb)
            km = (jnp.where(head0, kb, zk), jnp.where(head0, zk, kb))

            tile = lambda i: _tile_rows(i, tq)

            def scores(i, slot):
                for hh in range(2):
                    c0 = hh * HD
                    s_buf[slot, hh] = (_dot_nt(qm[hh, tile(i), :], kb)
                                       + (cq_ref[tile(i), c0:c0 + 1] - cr_ref[hh:hh + 1, krows]))
                    dp_buf[slot, hh] = _dot_nt(dom[hh, tile(i), :], vb_t)

            def products(slot, i, dk, dv):
                dq = jnp.zeros((tq, LANES), F32)
                for hh in range(2):
                    dsb = pd_buf[slot, hh, 1]
                    dv = dv + _dot_tn(pd_buf[slot, hh, 0], dom[hh, tile(i), :])
                    dk = dk + _dot_tn(dsb, qm[hh, tile(i), :])
                    dq = dq + _dot(dsb, km[hh])
                dq_ref[tile(i), :] += dq
                return dk, dv

            clamp = lambda i: jnp.clip(i, 0, nq - 1)

            def qstep(i, slot, carry):
                dk, dv, cs = carry
                dk, dv = products(1 - slot, clamp(i - 1), dk, dv)
                cs = list(cs)
                ic = clamp(i)
                first_visible = jnp.where(i < nq, (kj - i) * tq, 2 * tq)
                for r in range(tq // ATT_ROWS):
                    rows = slice(r * ATT_ROWS, (r + 1) * ATT_ROWS)
                    qrows = pl.ds(pl.multiple_of(ic * tq + r * ATT_ROWS, ATT_ROWS), ATT_ROWS)
                    visible = _causal_mask(ATT_ROWS, tq) >= first_visible - r * ATT_ROWS
                    for hh in range(2):
                        c0 = hh * HD
                        s = jnp.where(visible, s_buf[slot, hh, rows, :], -1e30)
                        p = jnp.exp(s - lse_ref[qrows, c0:c0 + 1])
                        ds = p * (dp_buf[slot, hh, rows, :] - dl_ref[qrows, c0:c0 + 1])
                        pd_buf[slot, hh, 0, rows, :] = p.astype(BF)
                        pd_buf[slot, hh, 1, rows, :] = ds.astype(BF)
                        cs[hh] = cs[hh] + jnp.sum(ds, axis=0, keepdims=True)
                scores(clamp(i + 1), 1 - slot)
                return dk, dv, tuple(cs)

            scores(kj, 0)
            pd_buf[1] = jnp.zeros((2, 2, tq, tq), BF)
            zero = jnp.zeros((tq, LANES), F32)
            zrow = jnp.zeros((1, tq), F32)
            trips = (nq - kj + 1) // 2
            dk, dv, cs = lax.fori_loop(
                0, trips, lambda t, cy: qstep(kj + 2 * t + 1, 1, qstep(kj + 2 * t, 0, cy)), (zero, zero, (zrow, zrow)))
            dk, dv = products(1, clamp(kj + 2 * trips - 1), dk, dv)
            dk_ref[krows, :] = dk
            dv_ref[krows, :] = dv
            dc_ref[0:1, krows] = -cs[0]
            dc_ref[1:2, krows] = -cs[1]
            return 0

        lax.fori_loop(0, nq, kloop, 0)

    blk = pl.BlockSpec((None, seq, LANES), lambda b, h: (b, 0, h))
    rowblk = pl.BlockSpec((None, None, 2, seq), lambda b, h: (b, h, 0, 0))
    return _host_call(
        body, (bsz, npair), [blk, blk, blk, blk, blk, blk, blk, rowblk], [blk, blk, blk, rowblk],
        [jax.ShapeDtypeStruct((bsz, seq, DA), F32)] * 3 + [jax.ShapeDtypeStruct((bsz, npair, 2, seq), F32)],
        [pltpu.VMEM((2, seq, LANES), BF), pltpu.VMEM((2, seq, LANES), BF), pltpu.VMEM((2, 2, tq, tq), F32),
         pltpu.VMEM((2, 2, tq, tq), F32), pltpu.VMEM((2, 2, 2, tq, tq), BF)],
        (qs, kn, vb, do, cq, lse, dl, crow), name, ex)


def _bwd_attn_post(pq, pk, dqs, dkn, pf, dcrow, gq, gk, bf, name, ex=None):
    bsz, seq, _ = pq.shape
    ts = 512
    nt = seq // ts

    def body(q_ref, k_ref, dq_ref, dk_ref, f_ref, dc_ref, gq_ref, gk_ref, bf_ref,
             dpq_ref, dpk_ref, dpf_ref, dgq_ref, dgk_ref, dbf_ref, carry):
        @pl.when((pl.program_id(0) == 0) & (pl.program_id(1) == 0))
        def _():
            dgq_ref[...] = jnp.zeros_like(dgq_ref)
            dgk_ref[...] = jnp.zeros_like(dgk_ref)
            dbf_ref[...] = jnp.zeros_like(dbf_ref)

        @pl.when(pl.program_id(1) == 0)
        def _():
            carry[...] = jnp.zeros_like(carry)

        for x_ref, dx_ref, g_ref, dp_ref, dg_ref, scale in ((q_ref, dq_ref, gq_ref, dpq_ref, dgq_ref, 1.0 / 8.0),
                                                            (k_ref, dk_ref, gk_ref, dpk_ref, dgk_ref, 1.0)):
            for jb in range(DA // LANES):
                sl = slice(jb * LANES, (jb + 1) * LANES)
                gb = g_ref[:, sl]
                xhat, r, _ = _head_rms(x_ref[:, sl], gb)
                dn = dx_ref[:, sl] * scale
                dg_ref[:, sl] += jnp.sum(dn * xhat, axis=0, keepdims=True)
                dxh = dn * gb
                dp_ref[:, sl] = r * (dxh - xhat * (_seg_sum64(dxh * xhat) * (1.0 / HD)))

        ui = lax.broadcasted_iota(jnp.int32, (ts, ts), 0)
        tj = lax.broadcasted_iota(jnp.int32, (ts, ts), 1)
        tri = (tj >= ui).astype(BF)
        dc = jnp.concatenate([dc_ref[...], jnp.zeros((LANES - NH, ts), F32)], axis=0)
        hi, mid, lo = _split3(dc)
        dlf = _dot_nt(tri, hi) + _dot_nt(tri, mid) + _dot_nt(tri, lo) + carry[...]
        carry[...] = dlf[0:1, :]
        xf = f_ref[...] + bf_ref[...]
        lane = lax.broadcasted_iota(jnp.int32, (ts, LANES), 1)
        dfl = jnp.where(lane < NH, dlf * _sigmoid(-xf), 0.0)
        dpf_ref[...] = dfl
        dbf_ref[...] += jnp.sum(dfl, axis=0, keepdims=True)

    rev = lambda i: nt - 1 - i
    tile = lambda n: pl.BlockSpec((None, ts, n), lambda b, i: (b, rev(i), 0))
    vec = lambda n: pl.BlockSpec((1, n), lambda b, i: (0, 0))
    return _host_call(
        body, (bsz, nt),
        [tile(DA), tile(DA), tile(DA), tile(DA), tile(LANES),
         pl.BlockSpec((None, NH, ts), lambda b, i: (b, 0, rev(i))), vec(DA), vec(DA), vec(LANES)],
        [tile(DA), tile(DA), tile(LANES), vec(DA), vec(DA), vec(LANES)],
        [jax.ShapeDtypeStruct((bsz, seq, DA), F32), jax.ShapeDtypeStruct((bsz, seq, DA), F32),
         jax.ShapeDtypeStruct((bsz, seq, LANES), F32), jax.ShapeDtypeStruct((1, DA), F32),
         jax.ShapeDtypeStruct((1, DA), F32), jax.ShapeDtypeStruct((1, LANES), F32)],
        [pltpu.VMEM((1, LANES), F32)], (pq, pk, dqs, dkn, pf, dcrow, gq, gk, bf), name, ex)


def _bwd_conv(pcf, psc, dycs, cf_dw, cf_dw_b, ln_g, ln_b, cf_pw, sc_dw, name):
    bsz, seq, _ = pcf.shape
    ts = 512
    nt, tile, cur, halo, full = _conv_specs(bsz, seq, ts, -1)

    def body(cf_ref, ha_ref, hg_ref, sc_ref, hc_ref, hx_ref, dy_ref, dw_ref, b_ref, lg_ref, lb_ref, pw_ref, sdw_ref,
             dcf_ref, dsc_ref, ddw_ref, db_ref, dlg_ref, dlb_ref, dpw_ref, dsdw_ref,
             ubuf, usbuf, obuf, dubuf, dsbuf, carry_du, carry_dc):
        step = pl.program_id(1)
        first = tile(step) == 0

        @pl.when((pl.program_id(0) == 0) & (step == 0))
        def _():
            for r in (ddw_ref, db_ref, dlg_ref, dlb_ref, dpw_ref, dsdw_ref):
                r[...] = jnp.zeros_like(r)

        @pl.when(step == 0)
        def _():
            carry_du[...] = jnp.zeros((HALO, DC), F32)
            carry_dc[...] = jnp.zeros((HALO, DC), F32)

        def transposed_conv(d, carry, w_ref, taps_t):
            dubuf[0:ts, :] = d
            dubuf[ts:ts + HALO, :] = carry[...]
            carry[...] = d[0:HALO, :]
            _shift_copies(dubuf, dsbuf, sorted({off % SUBLANES for off in taps_t}))
            _tap_conv(dsbuf, w_ref, taps_t, obuf, ts, 0.0)
            return obuf[...]

        a = cf_ref[:, 0:256]
        g = cf_ref[:, 256:512]
        z = cf_ref[:, 512:768]
        lng = lg_ref[...]
        u0, n, rstd, u2, s2, u3 = _conformer_fwd(a, g, ha_ref[...], hg_ref[...], first, ubuf, usbuf, obuf, dw_ref,
                                                 b_ref[...], lng, lb_ref[...])
        u3b = u3.astype(BF)
        p = _dot(u3b, pw_ref[...])
        sz = _sigmoid(z)
        dy = dy_ref[:, 0:256]
        dcf_ref[:, 512:768] = dy * p * _dsilu(z, sz)
        dpb = (dy * (z * sz)).astype(BF)
        dpw_ref[...] += _dot_tn(u3b, dpb)
        du2 = _dot_nt(dpb, pw_ref[...]) * _dsilu(u2, s2)
        dlg_ref[...] += jnp.sum(du2 * n, axis=0, keepdims=True)
        dlb_ref[...] += jnp.sum(du2, axis=0, keepdims=True)
        dn = du2 * lng
        du1 = rstd * (dn - jnp.mean(dn, axis=-1, keepdims=True) - n * jnp.mean(dn * n, axis=-1, keepdims=True))
        db_ref[...] += jnp.sum(du1, axis=0, keepdims=True)
        du0 = transposed_conv(du1, carry_du, dw_ref, CF_TAPS_T)
        for k, tap_grad in enumerate(_tap_sums(dubuf, usbuf, CF_TAPS, ts)):
            ddw_ref[k:k + 1, :] += tap_grad
        sg = _sigmoid(g)
        dcf_ref[:, 0:256] = du0 * sg
        dcf_ref[:, 256:512] = du0 * a * sg * (1.0 - sg)

        bb = sc_ref[:, 0:256]
        c = sc_ref[:, 256:512]
        xs = sc_ref[:, 512:768]
        zs = sc_ref[:, 768:1024]
        cv = _shortconv_fwd(c, xs, hc_ref[...], hx_ref[...], first, ubuf, usbuf, obuf, sdw_ref)
        szs = _sigmoid(zs)
        dys = dy_ref[:, 256:512]
        gate = zs * szs
        dsc_ref[:, 0:256] = dys * cv * gate
        dsc_ref[:, 768:1024] = dys * bb * cv * _dsilu(zs, szs)
        dm = transposed_conv(dys * bb * gate, carry_dc, sdw_ref, SC_TAPS_T)
        for k, tap_grad in enumerate(_tap_sums(dubuf, usbuf, SC_TAPS, ts)):
            dsdw_ref[k:k + 1, :] += tap_grad
        dsc_ref[:, 256:512] = dm * xs
        dsc_ref[:, 512:768] = dm * c

    outt = lambda n: pl.BlockSpec((None, ts, n), lambda b, i: (b, tile(i), 0))
    return pl.pallas_call(
        body, grid=(bsz, nt),
        in_specs=[cur(768), halo(0), halo(1), cur(1024), halo(1), halo(2), cur(512),
                  full(CFW, DC), full(1, DC), full(1, DC), full(1, DC), full(DC, DC), full(SCW, DC)],
        out_specs=[outt(768), outt(1024), full(CFW, DC), full(1, DC), full(1, DC), full(1, DC), full(DC, DC),
                   full(SCW, DC)],
        out_shape=[jax.ShapeDtypeStruct((bsz, seq, 768), F32), jax.ShapeDtypeStruct((bsz, seq, 1024), F32),
                   jax.ShapeDtypeStruct((CFW, DC), F32), jax.ShapeDtypeStruct((1, DC), F32),
                   jax.ShapeDtypeStruct((1, DC), F32), jax.ShapeDtypeStruct((1, DC), F32),
                   jax.ShapeDtypeStruct((DC, DC), F32), jax.ShapeDtypeStruct((SCW, DC), F32)],
        scratch_shapes=[pltpu.VMEM((HALO + ts, DC), F32), pltpu.VMEM((SUBLANES, HALO + ts, DC), F32),
                        pltpu.VMEM((ts, DC), F32), pltpu.VMEM((ts + HALO, DC), F32),
                        pltpu.VMEM((SUBLANES, ts + HALO, DC), F32), pltpu.VMEM((HALO, DC), F32),
                        pltpu.VMEM((HALO, DC), F32)],
        compiler_params=_cparams(("arbitrary", "arbitrary")), name=name,
    )(pcf, pcf, pcf, psc, psc, psc, dycs, cf_dw, cf_dw_b, ln_g, ln_b, cf_pw, sc_dw)


def _bwd_inproj(x, g, dyres, w, dcf, dsc, dq, dk, dv, dz, df, name):
    t = x.shape[0]
    tm = 256
    pieces = ((C_CF, 768), (C_SC, 1024), (C_Q, DA), (C_K, DA), (C_V, DA), (C_Z, DA), (C_F, LANES))

    def body(x_ref, g_ref, dy_ref, w_hbm, dcf_ref, dsc_ref, dq_ref, dk_ref, dv_ref, dz_ref, df_ref,
             dx_ref, dg_ref, dw_hbm, w_vmem, dw_acc, sem):
        i = pl.program_id(0)

        @pl.when(i == 0)
        def _():
            cp = pltpu.make_async_copy(w_hbm, w_vmem, sem)
            cp.start()
            dw_acc[...] = jnp.zeros_like(dw_acc)
            dg_ref[...] = jnp.zeros_like(dg_ref)
            cp.wait()

        xt = x_ref[...]
        gg = g_ref[...]
        r = lax.rsqrt(jnp.mean(xt * xt, axis=-1, keepdims=True) + EPS)
        xhat = xt * r
        ht = (xhat * gg).astype(BF).T
        dh = jnp.zeros((tm, D), F32)
        for (c0, n), ref in zip(pieces, (dcf_ref, dsc_ref, dq_ref, dk_ref, dv_ref, dz_ref, df_ref)):
            for s0 in range(0, n, 512):
                s1 = min(s0 + 512, n)
                d = ref[:, s0:s1].astype(BF)
                dh = dh + _dot_nt(d, w_vmem[:, c0 + s0:c0 + s1])
                dw_acc[:, c0 + s0:c0 + s1] += _dot(ht, d)
        dg_ref[...] += jnp.sum(dh * xhat, axis=0, keepdims=True)
        dhg = dh * gg
        dx_ref[...] = dy_ref[...] + r * (dhg - xhat * jnp.mean(dhg * xhat, axis=-1, keepdims=True))

        @pl.when(i == pl.num_programs(0) - 1)
        def _():
            out = pltpu.make_async_copy(dw_acc, dw_hbm, sem)
            out.start()
            out.wait()

    row = lambda n: pl.BlockSpec((tm, n), lambda i: (i, 0))
    anyspec = pl.BlockSpec(memory_space=pl.ANY)
    return pl.pallas_call(
        body, grid=(t // tm,),
        in_specs=[row(D), pl.BlockSpec((1, D), lambda i: (0, 0)), row(D), anyspec,
                  row(768), row(1024), row(DA), row(DA), row(DA), row(DA), row(LANES)],
        out_specs=[row(D), pl.BlockSpec((1, D), lambda i: (0, 0)), anyspec],
        out_shape=[jax.ShapeDtypeStruct((t, D), F32), jax.ShapeDtypeStruct((1, D), F32),
                   jax.ShapeDtypeStruct((D, NP), F32)],
        scratch_shapes=[pltpu.VMEM((D, NP), BF), pltpu.VMEM((D, NP), F32), pltpu.SemaphoreType.DMA],
        compiler_params=_cparams(("arbitrary",)), name=name)(x, g, dyres, w, dcf, dsc, dq, dk, dv, dz, df)


def _adamw_update(w_ref, g_ref, m_ref, v_ref, d_ref, m2_ref, v2_ref):
    gt = g_ref[...]
    m2 = ADAM_B1 * m_ref[...] + (1.0 - ADAM_B1) * gt
    v2 = ADAM_B2 * v_ref[...] + (1.0 - ADAM_B2) * (gt * gt)
    m_hat = m2 / (1.0 - ADAM_B1 ** ADAM_STEP)
    v_hat = v2 / (1.0 - ADAM_B2 ** ADAM_STEP)
    d_ref[...] = -ADAM_LR * (m_hat / (jnp.sqrt(v_hat) + ADAM_EPS) + ADAM_WD * w_ref[...])
    m2_ref[...] = m2
    v2_ref[...] = v2


def _adamw_many(ws, gs, ms, vs, name):
    n = len(ws)

    def body(*refs):
        ins, outs = refs[:4 * n], refs[4 * n:]
        for i in range(n):
            _adamw_update(ins[i], ins[n + i], ins[2 * n + i], ins[3 * n + i], *outs[3 * i:3 * i + 3])

    return pl.pallas_call(
        body, out_shape=[jax.ShapeDtypeStruct(w.shape, F32) for w in ws for _ in range(3)],
        name=name)(*ws, *gs, *ms, *vs)


def _adamw(w, g, m, v, tr, name):
    rows, cols = w.shape

    def body(w_ref, g_ref, m_ref, v_ref, d_ref, m2_ref, v2_ref):
        _adamw_update(w_ref, g_ref, m_ref, v_ref, d_ref, m2_ref, v2_ref)

    blk = pl.BlockSpec((tr, cols), lambda i: (i, 0))
    return pl.pallas_call(
        body, grid=(rows // tr,), in_specs=[blk] * 4, out_specs=[blk] * 3,
        out_shape=[jax.ShapeDtypeStruct((rows, cols), F32)] * 3,
        compiler_params=_cparams(("arbitrary",)), name=name)(w, g, m, v)


def _place():
    x, y, c = lax.axis_index("x"), lax.axis_index("y"), lax.axis_index("c")
    chips = [(1 - x, y), (x, 1 - y), (1 - x, 1 - y)]
    return x, y, c, chips


def _halves(c, rows, align):
    rh = rows // 2
    return pl.ds(pl.multiple_of(c * rh, align), rh), pl.ds(pl.multiple_of((1 - c) * rh, align), rh)


ANY = pl.BlockSpec(memory_space=pl.ANY)


class _Exchange:
    def __init__(self, operands, out_shape, sems, start, wait):
        self.operands, self.out_shape, self.sems, self.start, self.wait = operands, out_shape, sems, start, wait


class _StagedCopy:
    def __init__(self, src, dst, stage, sem):
        self.src, self.dst, self.stage, self.sem = src, dst, stage, sem

    def start(self):
        pltpu.make_async_copy(self.src, self.stage, self.sem).start()

    def finish(self):
        pltpu.make_async_copy(self.src, self.stage, self.sem).wait()
        out = pltpu.make_async_copy(self.stage, self.dst, self.sem)
        out.start()
        out.wait()


def _run_exchange(ex, name):
    n_in, n_out = len(ex.operands), len(ex.out_shape)

    def body(*refs):
        ins, outs, sems = refs[:n_in], refs[n_in:n_in + n_out], refs[n_in + n_out:]
        ex.start(ins, outs, sems)
        ex.wait(ins, outs, sems)

    return pl.pallas_call(body, in_specs=[ANY] * n_in, out_specs=[ANY] * n_out, out_shape=ex.out_shape,
                          scratch_shapes=ex.sems, name=name)(*ex.operands)


def _host_call(body, grid, in_specs, out_specs, out_shape, scratch, operands, name, ex=None):
    sem = ("arbitrary",) * len(grid)
    if ex is None:
        outs = pl.pallas_call(body, grid=grid, in_specs=in_specs, out_specs=out_specs, out_shape=out_shape,
                              scratch_shapes=scratch, compiler_params=_cparams(sem), name=name)(*operands)
        return outs, None
    n_in, n_out, n_scr = len(in_specs), len(out_specs), len(scratch)
    xi, xo = len(ex.operands), len(ex.out_shape)

    def hosted(*refs):
        ins, xins = refs[:n_in], refs[n_in:n_in + xi]
        o0 = n_in + xi
        outs, xouts = refs[o0:o0 + n_out], refs[o0 + n_out:o0 + n_out + xo]
        s0 = o0 + n_out + xo
        scr, xsems = refs[s0:s0 + n_scr], refs[s0 + n_scr:]
        first = pl.program_id(0) == 0
        last = pl.program_id(0) == pl.num_programs(0) - 1
        for d in range(1, len(grid)):
            first = first & (pl.program_id(d) == 0)
            last = last & (pl.program_id(d) == pl.num_programs(d) - 1)

        @pl.when(first)
        def _():
            ex.start(xins, xouts, xsems)

        body(*ins, *outs, *scr)

        @pl.when(last)
        def _():
            ex.wait(xins, xouts, xsems)

    res = pl.pallas_call(
        hosted, grid=grid, in_specs=list(in_specs) + [ANY] * xi, out_specs=list(out_specs) + [ANY] * xo,
        out_shape=list(out_shape) + list(ex.out_shape), scratch_shapes=list(scratch) + list(ex.sems),
        compiler_params=_cparams(sem), name=name)(*operands, *ex.operands)
    return res[:n_out], res[n_out:]


def _all_gather_chips(shards):
    n = len(shards)

    def copies(srcs, outs, sems):
        send_sems, recv_sems, local_sems, *stage = sems
        x, y, c, chips = _place()
        me = 2 * x + y
        halves = [_halves(c, s.shape[1], 16) for s in shards]
        local = [_StagedCopy(srcs[i], outs[i].at[:, me], stage[i], local_sems.at[i]) for i in range(n)]

        def ici(i, j, chip, k):
            mine = halves[i][0]
            return pltpu.make_async_remote_copy(
                src_ref=srcs[i].at[:, mine, :], dst_ref=outs[i].at[:, k, mine, :], send_sem=send_sems.at[6 * i + j],
                recv_sem=recv_sems.at[6 * i + j], device_id=(*chip, c), device_id_type=MESH)

        def d2d(i, j, k, half):
            return pltpu.make_async_remote_copy(
                src_ref=outs[i].at[:, k, half, :], dst_ref=outs[i].at[:, k, half, :], send_sem=send_sems.at[6 * i + 3 + j],
                recv_sem=recv_sems.at[6 * i + 3 + j], device_id=(x, y, 1 - c), device_id_type=MESH)

        return me, chips, halves, local, ici, d2d

    def start(srcs, outs, sems):
        me, chips, _, local, ici, _ = copies(srcs, outs, sems)
        for cp in local:
            cp.start()
        for i in range(n):
            for j, chip in enumerate(chips):
                ici(i, j, chip, me).start()

    def wait(srcs, outs, sems):
        me, chips, halves, local, ici, d2d = copies(srcs, outs, sems)
        passed = []
        for i in range(n):
            for j, (px, py) in enumerate(chips):
                k = 2 * px + py
                ici(i, j, (px, py), k).wait_recv()
                fw = d2d(i, j, k, halves[i][0])
                fw.start()
                passed.append(fw)
        for i in range(n):
            for j, (px, py) in enumerate(chips):
                d2d(i, j, 2 * px + py, halves[i][1]).wait_recv()
        for i in range(n):
            for j, chip in enumerate(chips):
                ici(i, j, chip, me).wait_send()
        for cp in passed:
            cp.wait_send()
        for cp in local:
            cp.finish()

    return _Exchange(
        list(shards), [jax.ShapeDtypeStruct((s.shape[0], 4) + s.shape[1:], s.dtype) for s in shards],
        [pltpu.SemaphoreType.DMA((6 * n,)), pltpu.SemaphoreType.DMA((6 * n,)), pltpu.SemaphoreType.DMA((n,))]
        + [pltpu.VMEM(s.shape, s.dtype) for s in shards], start, wait)


def _rs_swap_halves(gs):
    n = len(gs)

    def copies(srcs, gots, sems):
        send_sems, recv_sems = sems
        x, y, c, _ = _place()
        return [pltpu.make_async_remote_copy(
            src_ref=srcs[i].at[:, :, _halves(c, gs[i].shape[2], 8)[1], :], dst_ref=gots[i], send_sem=send_sems.at[i],
            recv_sem=recv_sems.at[i], device_id=(x, y, 1 - c), device_id_type=MESH) for i in range(n)]

    def start(srcs, gots, sems):
        for cp in copies(srcs, gots, sems):
            cp.start()

    def wait(srcs, gots, sems):
        for cp in copies(srcs, gots, sems):
            cp.wait()

    return _Exchange(
        list(gs), [jax.ShapeDtypeStruct(g.shape[:2] + (g.shape[2] // 2, g.shape[3]), g.dtype) for g in gs],
        [pltpu.SemaphoreType.DMA((n,)), pltpu.SemaphoreType.DMA((n,))], start, wait)


def _rs_add_pair(g, got, cidx, tb, out_dtype, name):
    nl, _, rh, cols = got.shape
    nb = rh // tb

    def body(c_ref, g_ref, o_ref, s_ref):
        s_ref[...] = (g_ref[...] + o_ref[...]).astype(out_dtype)

    blk = lambda half: pl.BlockSpec((None, None, tb, cols), (lambda l, k, i, c: (l, k, c[0] * nb + i, 0)) if half
                                    else (lambda l, k, i, c: (l, k, i, 0)))
    return pl.pallas_call(
        body,
        grid_spec=pltpu.PrefetchScalarGridSpec(num_scalar_prefetch=1, grid=(nl, 4, nb),
                                               in_specs=[blk(True), blk(False)], out_specs=blk(False)),
        out_shape=jax.ShapeDtypeStruct(got.shape, out_dtype),
        compiler_params=_cparams(("arbitrary", "arbitrary", "arbitrary")), name=name)(cidx, g, got)


def _rs_exchange_chips(pairs):
    n = len(pairs)

    def copies(srcs, gots, sems):
        send_sems, recv_sems, local_sems, *stage = sems
        x, y, c, chips = _place()
        me = 2 * x + y
        local = [_StagedCopy(srcs[i].at[:, me], gots[i].at[:, me], stage[i], local_sems.at[i]) for i in range(n)]

        def ici(i, j, chip, frm, to):
            return pltpu.make_async_remote_copy(
                src_ref=srcs[i].at[:, to], dst_ref=gots[i].at[:, frm], send_sem=send_sems.at[3 * i + j],
                recv_sem=recv_sems.at[3 * i + j], device_id=(*chip, c), device_id_type=MESH)

        sent = [ici(i, j, (px, py), me, 2 * px + py) for i in range(n) for j, (px, py) in enumerate(chips)]
        recvd = [ici(i, j, (px, py), 2 * px + py, me) for i in range(n) for j, (px, py) in enumerate(chips)]
        return local, sent, recvd

    def start(srcs, gots, sems):
        local, sent, _ = copies(srcs, gots, sems)
        for cp in local + sent:
            cp.start()

    def wait(srcs, gots, sems):
        local, sent, recvd = copies(srcs, gots, sems)
        for cp in recvd:
            cp.wait_recv()
        for cp in sent:
            cp.wait_send()
        for cp in local:
            cp.finish()

    return _Exchange(
        list(pairs), [jax.ShapeDtypeStruct(p.shape, p.dtype) for p in pairs],
        [pltpu.SemaphoreType.DMA((3 * n,)), pltpu.SemaphoreType.DMA((3 * n,)), pltpu.SemaphoreType.DMA((n,))]
        + [pltpu.VMEM((p.shape[0],) + p.shape[2:], p.dtype) for p in pairs], start, wait)


def _rs_add_chips(got, tb, name):
    nl, _, rh, cols = got.shape

    def body(g_ref, s_ref):
        s_ref[...] = ((g_ref[0].astype(F32) + g_ref[1].astype(F32)) + g_ref[2].astype(F32)) + g_ref[3].astype(F32)

    return pl.pallas_call(
        body, grid=(nl, rh // tb),
        in_specs=[pl.BlockSpec((None, 4, tb, cols), lambda l, i: (l, 0, i, 0))],
        out_specs=pl.BlockSpec((None, tb, cols), lambda l, i: (l, i, 0)),
        out_shape=jax.ShapeDtypeStruct((nl, rh, cols), F32),
        compiler_params=_cparams(("arbitrary", "arbitrary")), name=name)(got)


def _rs_join_halves(groups):
    totals = [t for grp in groups for t in grp]
    where = [(gi, li) for gi, grp in enumerate(groups) for li in range(len(grp))]
    n = len(totals)

    def copies(srcs, outs, sems):
        send_sems, recv_sems, local_sems, *stage = sems
        x, y, c, _ = _place()
        local, send, recv = [], [], []
        for i, (gi, li) in enumerate(where):
            mine, other = _halves(c, 2 * totals[i].shape[1], 8)
            local.append(_StagedCopy(srcs[i], outs[gi].at[li:li + 1, mine, :], stage[i], local_sems.at[i]))
            for lst, rows in ((send, mine), (recv, other)):
                lst.append(pltpu.make_async_remote_copy(
                    src_ref=srcs[i], dst_ref=outs[gi].at[li:li + 1, rows, :], send_sem=send_sems.at[i],
                    recv_sem=recv_sems.at[i], device_id=(x, y, 1 - c), device_id_type=MESH))
        return local, send, recv

    def start(srcs, outs, sems):
        local, send, _ = copies(srcs, outs, sems)
        for cp in local + send:
            cp.start()

    def wait(srcs, outs, sems):
        local, send, recv = copies(srcs, outs, sems)
        for cp in recv:
            cp.wait_recv()
        for cp in send:
            cp.wait_send()
        for cp in local:
            cp.finish()

    return _Exchange(
        totals, [jax.ShapeDtypeStruct((len(grp), 2 * grp[0].shape[1], grp[0].shape[2]), F32) for grp in groups],
        [pltpu.SemaphoreType.DMA((n,)), pltpu.SemaphoreType.DMA((n,)), pltpu.SemaphoreType.DMA((n,))]
        + [pltpu.VMEM(t.shape, F32) for t in totals], start, wait)


def _rows(a):
    return a.reshape(-1, LANES)


def _pad_rows(a, mult):
    r = (-a.shape[0]) % mult
    return a if r == 0 else jnp.concatenate([a, jnp.zeros((r, a.shape[1]), a.dtype)], axis=0)


def kernel(x, norm_g, w_in, b_f, cf_dw, cf_dw_b, cf_ln_g, cf_ln_b, cf_pw, sc_dw, q_norm_g, k_norm_g, w_out, loss_target, m_norm_g, m_w_in, m_b_f, m_cf_dw, m_cf_dw_b, m_cf_ln_g, m_cf_ln_b, m_cf_pw, m_sc_dw, m_q_norm_g, m_k_norm_g, m_w_out, v_norm_g, v_w_in, v_b_f, v_cf_dw, v_cf_dw_b, v_cf_ln_g, v_cf_ln_b, v_cf_pw, v_sc_dw, v_q_norm_g, v_k_norm_g, v_w_out):
    bsz, seq, _ = x.shape
    t = bsz * seq

    taps = jnp.concatenate([cf_dw.reshape(-1), sc_dw.reshape(-1)])
    taps_bf = lax.bitcast_convert_type(taps, jnp.bfloat16).reshape(-1, LANES)
    n_pw, n_taps = 2 * 64 * DC // LANES, taps_bf.shape[0]
    small_w = _pad_rows(jnp.concatenate([_rows(cf_pw.astype(BF)), taps_bf], axis=0), 32)[None]
    win_b, wout_b = w_in.astype(BF), w_out.astype(BF)
    win0_g, small_g = _run_exchange(_all_gather_chips([win_b[0:1], small_w]), "all_gather_first")
    gather_rest = _all_gather_chips([win_b[1:DEPTH], wout_b])
    full_w_in = lambda g: jnp.concatenate([g[k] for k in range(4)] + [jnp.zeros((D, NP - N_IN), BF)], axis=-1)
    w_in_full = [full_w_in(win0_g[0])] + [None] * (DEPTH - 1)
    w_out_full = None
    cf_pw_full = jnp.concatenate([small_g[0, k, 0:n_pw].reshape(DEPTH, 64, DC) for k in range(4)], axis=1)
    taps_all = [lax.bitcast_convert_type(small_g[0, k, n_pw:n_pw + n_taps].reshape(-1, 2), F32) for k in range(4)]
    n_cfdw = DEPTH * CFW * 64
    cf_dw_full = jnp.concatenate([tk[:n_cfdw].reshape(DEPTH, CFW, 64) for tk in taps_all], axis=-1)
    sc_dw_full = jnp.concatenate([tk[n_cfdw:].reshape(DEPTH, SCW, 64) for tk in taps_all], axis=-1)

    bf_pad = jnp.pad(b_f, ((0, 0), (0, LANES - NH)))

    xs = [x.reshape(t, D)]
    saved = []
    dy = loss_part = None
    for l in range(DEPTH):
        xl = xs[-1]
        gq, gk = q_norm_g[l].reshape(1, DA), k_norm_g[l].reshape(1, DA)
        pcf, psc, pq, pk, vb, pz, pf = _fwd_inproj(xl, norm_g[l][None], w_in_full[l], f"fwd_inproj_{l}")
        b3 = lambda a: a.reshape(bsz, seq, a.shape[-1])
        ycs = _fwd_conv(b3(pcf), b3(psc), cf_dw_full[l], cf_dw_b[l][None], cf_ln_g[l][None], cf_ln_b[l][None],
                        cf_pw_full[l], sc_dw_full[l], f"fwd_conv_{l}")
        qs, kn, cq, crow8 = _fwd_attn_prep(b3(pq), b3(pk), b3(pf), gq, gk, bf_pad[l][None], f"fwd_attn_prep_{l}")
        crow = crow8.reshape(bsz, NH // 2, 2, seq)
        (o, ohp, yatt, lse), landed = _fwd_attn(qs, kn, b3(vb), cq, crow, b3(pz), f"fwd_attn_{l}",
                                                gather_rest if l == 0 else None)
        if l == 0:
            win_rest_g, wout_g = landed
            w_in_full[1:] = [full_w_in(win_rest_g[i]) for i in range(DEPTH - 1)]
            w_out_full = wout_g.reshape(DEPTH, D, D)
        ycs2, yatt2 = ycs.reshape(t, 512), yatt.reshape(t, DA)
        if l + 1 < DEPTH:
            xs.append(_fwd_outproj(xl, ycs2, yatt2, w_out_full[l], f"fwd_outproj_{l}"))
        else:
            dy, loss_part = _fwd_outproj_loss(xl, ycs2, yatt2, w_out_full[l], loss_target.reshape(t, D),
                                              f"fwd_outproj_loss_{l}")
        saved.append((pcf, psc, pq, pk, vb, pz, pf, ycs2, yatt2, o, ohp, qs, kn, cq, crow, lse))

    grads = [None] * DEPTH
    reduced = [None] * DEPTH
    cidx = lax.axis_index("c").astype(jnp.int32).reshape(1)
    big_tiles = [256, 128]

    def big_packs(gl):
        return [jnp.stack([gl["w_in"][:, NSH * k:NSH * (k + 1)] for k in range(4)])[None],
                gl["w_out"].reshape(1, 4, 256, D)]

    def add_pairs(packs, gots, tiles, dtypes, tag):
        return [_rs_add_pair(p, got, cidx, tb, dt, f"rs_add_pair_{tag}_{i}")
                for i, (p, got, tb, dt) in enumerate(zip(packs, gots, tiles, dtypes))]

    def add_chips(gots2, tiles, tag):
        return [_rs_add_chips(got, tb, f"rs_add_chips_{tag}_{i}") for i, (got, tb) in enumerate(zip(gots2, tiles))]

    for l in reversed(range(DEPTH)):
        pcf, psc, pq, pk, vb, pz, pf, ycs2, yatt2, o, ohp, qs, kn, cq, crow, lse = saved[l]
        gq, gk = q_norm_g[l].reshape(1, DA), k_norm_g[l].reshape(1, DA)
        b3 = lambda a: a.reshape(bsz, seq, a.shape[-1])
        f2 = lambda a: a.reshape(t, a.shape[-1])
        packs = big_packs(grads[l + 1]) if l + 1 < DEPTH else None
        (dycs, do, dz, dl, d_wout), gots = _bwd_outproj(
            dy, ycs2, yatt2, f2(o), f2(ohp), pz, w_out_full[l], f"bwd_outproj_{l}",
            _rs_swap_halves(packs) if packs else None)
        pairs = add_pairs(packs, gots, big_tiles, [BF, BF], l + 1) if packs else None
        (dqs, dkn, dv, dcrow), gots2 = _bwd_attn(qs, kn, b3(vb), b3(do), cq, lse, b3(dl), crow, f"bwd_attn_{l}",
                                                 _rs_exchange_chips(pairs) if packs else None)
        if packs:
            reduced[l + 1] = add_chips(gots2, big_tiles, l + 1)
        (dpq, dpk, dpf, d_gq, d_gk, d_bf), _ = _bwd_attn_post(
            b3(pq), b3(pk), dqs, dkn, b3(pf), dcrow.reshape(bsz, NH, seq), gq, gk, bf_pad[l][None],
            f"bwd_attn_post_{l}")
        dcf, dsc, d_cfdw, d_cfb, d_lng, d_lnb, d_pw, d_scdw = _bwd_conv(
            b3(pcf), b3(psc), b3(dycs), cf_dw_full[l], cf_dw_b[l][None], cf_ln_g[l][None], cf_ln_b[l][None],
            cf_pw_full[l], sc_dw_full[l], f"bwd_conv_{l}")
        dy, d_ng, d_win = _bwd_inproj(xs[l], norm_g[l][None], dy, w_in_full[l], f2(dcf), f2(dsc), f2(dpq), f2(dpk),
                                      f2(dv), dz, f2(dpf), f"bwd_inproj_{l}")
        grads[l] = dict(norm_g=d_ng[0], w_in=d_win, b_f=d_bf[0, :NH], cf_dw=d_cfdw, cf_dw_b=d_cfb[0],
                        cf_ln_g=d_lng[0], cf_ln_b=d_lnb[0], cf_pw=d_pw, sc_dw=d_scdw,
                        q_norm_g=d_gq.reshape(NH, HD), k_norm_g=d_gk.reshape(NH, HD), w_out=d_wout)
    grad_x = dy.reshape(bsz, seq, D)
    gw = {n: jnp.stack([grads[l][n] for l in range(DEPTH)]) for n in grads[0] if n not in ("w_in", "w_out")}

    rep_names = ("norm_g", "b_f", "cf_dw_b", "cf_ln_g", "cf_ln_b", "q_norm_g", "k_norm_g")
    rep = jnp.concatenate([jnp.pad(gw[n].reshape(-1), (0, (-gw[n].size) % LANES)) for n in rep_names]
                          + [jnp.pad(loss_part.reshape(-1), (0, LANES - 1))]).reshape(-1, LANES)
    blocks = []
    for k in range(4):
        small = jnp.concatenate([gw["cf_dw"][:, :, 64 * k:64 * (k + 1)].reshape(-1),
                                 gw["sc_dw"][:, :, 64 * k:64 * (k + 1)].reshape(-1)]).reshape(-1, LANES)
        blocks.append(_pad_rows(jnp.concatenate([_rows(gw["cf_pw"][:, 64 * k:64 * (k + 1), :]), small, rep], axis=0), 16))
    g_small = jnp.stack(blocks)[None]
    packs = big_packs(grads[0]) + [g_small]
    tiles = big_tiles + [g_small.shape[2] // 2]
    gots = _run_exchange(_rs_swap_halves(packs), "rs_swap_halves_last")
    pairs = add_pairs(packs, gots, tiles, [BF, BF, F32], 0)
    gots2 = _run_exchange(_rs_exchange_chips(pairs), "rs_exchange_chips_last")
    tot_win0, tot_wout0, tot_small = add_chips(gots2, tiles, 0)
    reduced[0] = (tot_win0, tot_wout0)
    red_win, red_wout, red_small = _run_exchange(
        _rs_join_halves([[reduced[l][0] for l in range(DEPTH)], [reduced[l][1] for l in range(DEPTH)], [tot_small]]),
        "rs_join_halves")

    n_small = (DEPTH * CFW * 64 + DEPTH * SCW * 64) // LANES
    red = red_small[0]
    pos = [0]

    def take(nrows):
        pos[0] += nrows
        return red[pos[0] - nrows:pos[0]]

    g = {}
    g["w_in"] = red_win
    g["w_out"] = red_wout
    g["cf_pw"] = take(n_pw).reshape(DEPTH, 64, DC)
    small = take(n_small).reshape(-1)
    g["cf_dw"] = small[:n_cfdw].reshape(DEPTH, CFW, 64)
    g["sc_dw"] = small[n_cfdw:].reshape(DEPTH, SCW, 64)
    shapes = dict(norm_g=norm_g.shape, b_f=b_f.shape, cf_dw_b=cf_dw_b.shape, cf_ln_g=cf_ln_g.shape,
                  cf_ln_b=cf_ln_b.shape, q_norm_g=q_norm_g.shape, k_norm_g=k_norm_g.shape)
    for n in rep_names:
        size = 1
        for s in shapes[n]:
            size *= s
        g[n] = take(-(-size // LANES)).reshape(-1)[:size].reshape(shapes[n])
    loss = take(1)[0, 0]

    order = ("norm_g", "w_in", "b_f", "cf_dw", "cf_dw_b", "cf_ln_g", "cf_ln_b", "cf_pw", "sc_dw", "q_norm_g",
             "k_norm_g", "w_out")
    weights = dict(norm_g=norm_g, w_in=w_in, b_f=b_f, cf_dw=cf_dw, cf_dw_b=cf_dw_b, cf_ln_g=cf_ln_g, cf_ln_b=cf_ln_b,
                   cf_pw=cf_pw, sc_dw=sc_dw, q_norm_g=q_norm_g, k_norm_g=k_norm_g, w_out=w_out)
    ms = dict(norm_g=m_norm_g, w_in=m_w_in, b_f=m_b_f, cf_dw=m_cf_dw, cf_dw_b=m_cf_dw_b, cf_ln_g=m_cf_ln_g,
              cf_ln_b=m_cf_ln_b, cf_pw=m_cf_pw, sc_dw=m_sc_dw, q_norm_g=m_q_norm_g, k_norm_g=m_k_norm_g, w_out=m_w_out)
    vs = dict(norm_g=v_norm_g, w_in=v_w_in, b_f=v_b_f, cf_dw=v_cf_dw, cf_dw_b=v_cf_dw_b, cf_ln_g=v_cf_ln_g,
              cf_ln_b=v_cf_ln_b, cf_pw=v_cf_pw, sc_dw=v_sc_dw, q_norm_g=v_q_norm_g, k_norm_g=v_k_norm_g, w_out=v_w_out)
    delta, new_m, new_v = {}, {}, {}
    two_d = lambda a: a.reshape(-1, a.shape[-1])
    for n in ("w_in", "w_out"):
        outs = _adamw(two_d(weights[n]), two_d(g[n]), two_d(ms[n]), two_d(vs[n]), 256, f"adamw_{n}")
        delta[n], new_m[n], new_v[n] = (a.reshape(weights[n].shape) for a in outs)
    small_names = [n for n in order if n not in ("w_in", "w_out")]
    outs = _adamw_many([weights[n] for n in small_names], [g[n] for n in small_names],
                       [ms[n] for n in small_names], [vs[n] for n in small_names], "adamw_small")
    for i, n in enumerate(small_names):
        delta[n], new_m[n], new_v[n] = outs[3 * i:3 * i + 3]

    return (loss, grad_x, *[g[n] for n in order], *[delta[n] for n in order], *[new_m[n] for n in order],
            *[new_v[n] for n in order])
```

```python
import jax
import jax.numpy as jnp
from jax import lax
from jax.experimental import pallas as pl
from jax.experimental.pallas import tpu as pltpu

F32 = jnp.float32
BF = jnp.bfloat16
MESH = pl.DeviceIdType.MESH

DEPTH = 2
D = 1024
DC = 256
DA = 512
NH = 8
HD = 64
CFW = 31
SCW = 3
N_IN = 3848
NP = 3968
NSH = N_IN // 4
HALO = 32
ATT_ROWS = 32
BWD_KEYS = 512
FWD_ROWS = 256
SUBLANES = 8
CONV_ROWS = 32
TAP_GROUP = 8
EPS = 1e-6
LANES = 128
VMEM_LIMIT = 56 * 1024 * 1024

C_CF, C_SC, C_Q, C_K, C_V, C_Z, C_F = 0, 768, 1792, 2304, 2816, 3328, 3840

ADAM_LR = 0.001
ADAM_B1 = 0.9
ADAM_B2 = 0.999
ADAM_EPS = 1e-08
ADAM_WD = 0.01
ADAM_STEP = 10


def _cparams(sem=None):
    return pltpu.CompilerParams(dimension_semantics=sem, vmem_limit_bytes=VMEM_LIMIT)


def _dot(a, b):
    return jnp.dot(a, b, preferred_element_type=F32)


def _dot_nt(a, b):
    return lax.dot_general(a, b, (((1,), (1,)), ((), ())), preferred_element_type=F32)


def _dot_tn(a, b):
    return lax.dot_general(a, b, (((0,), (0,)), ((), ())), preferred_element_type=F32)


def _split3(x):
    hi = x.astype(BF)
    r1 = x - hi.astype(F32)
    mid = r1.astype(BF)
    lo = (r1 - mid.astype(F32)).astype(BF)
    return hi, mid, lo


def _dot_exact(a_bf, x):
    hi, mid, lo = _split3(x)
    return _dot(a_bf, hi) + _dot(a_bf, mid) + _dot(a_bf, lo)


def _sigmoid(x):
    return 1.0 / (1.0 + jnp.exp(-x))


def _seg_sum64(x):
    i = lax.broadcasted_iota(jnp.int32, (LANES, LANES), 0)
    j = lax.broadcasted_iota(jnp.int32, (LANES, LANES), 1)
    g = ((i >= HD) == (j >= HD)).astype(BF)
    hi, mid, lo = _split3(x)
    return _dot(hi, g) + _dot(mid, g) + _dot(lo, g)


def _fwd_inproj(x, g, w, name):
    t = x.shape[0]
    tm = 256

    def body(x_ref, g_ref, w_ref, cf_ref, sc_ref, q_ref, k_ref, v_ref, z_ref, f_ref):
        xt = x_ref[...]
        r = lax.rsqrt(jnp.mean(xt * xt, axis=-1, keepdims=True) + EPS)
        h = ((xt * r) * g_ref[...]).astype(BF)
        cf_ref[:, 0:512] = _dot(h, w_ref[:, 0:512])
        cf_ref[:, 512:768] = _dot(h, w_ref[:, 512:768])
        sc_ref[:, 0:512] = _dot(h, w_ref[:, C_SC:C_SC + 512])
        sc_ref[:, 512:1024] = _dot(h, w_ref[:, C_SC + 512:C_Q])
        q_ref[...] = _dot(h, w_ref[:, C_Q:C_K])
        k_ref[...] = _dot(h, w_ref[:, C_K:C_V])
        v_ref[...] = _dot(h, w_ref[:, C_V:C_Z]).astype(BF)
        z_ref[...] = _dot(h, w_ref[:, C_Z:C_F])
        f_ref[...] = _dot(h, w_ref[:, C_F:NP])

    row = lambda n: pl.BlockSpec((tm, n), lambda i: (i, 0))
    return pl.pallas_call(
        body, grid=(t // tm,),
        in_specs=[row(D), pl.BlockSpec((1, D), lambda i: (0, 0)), pl.BlockSpec((D, NP), lambda i: (0, 0))],
        out_specs=[row(768), row(1024), row(DA), row(DA), row(DA), row(DA), row(LANES)],
        out_shape=[jax.ShapeDtypeStruct((t, 768), F32), jax.ShapeDtypeStruct((t, 1024), F32),
                   jax.ShapeDtypeStruct((t, DA), F32), jax.ShapeDtypeStruct((t, DA), F32),
                   jax.ShapeDtypeStruct((t, DA), BF), jax.ShapeDtypeStruct((t, DA), F32),
                   jax.ShapeDtypeStruct((t, LANES), F32)],
        compiler_params=_cparams(("arbitrary",)), name=name)(x, g, w)


def _shift_copies(buf, sbuf, shifts):
    n = buf.shape[0]
    for sh in shifts:
        rows = n if sh == 0 else n - SUBLANES
        sbuf[sh, 0:rows, :] = buf[pl.ds(sh, rows), :]


def _tap_rows(sbuf, off, r0):
    sh = off % SUBLANES
    return sbuf[sh, pl.ds(pl.multiple_of(r0 + (off - sh), SUBLANES), CONV_ROWS), :]


def _tap_conv(sbuf, w_ref, offsets, dst, ts, bias):
    def chunk(c, _):
        r0 = pl.multiple_of(c * CONV_ROWS, CONV_ROWS)
        acc = jnp.zeros((CONV_ROWS, DC), F32) + bias
        for k, off in enumerate(offsets):
            acc = acc + w_ref[k:k + 1, :] * _tap_rows(sbuf, off, r0)
        dst[pl.ds(r0, CONV_ROWS), :] = acc
        return 0

    lax.fori_loop(0, ts // CONV_ROWS, chunk, 0)


def _tap_sums(a_buf, sbuf, offsets, ts):
    outs = []
    for g0 in range(0, len(offsets), TAP_GROUP):
        group = offsets[g0:g0 + TAP_GROUP]

        def chunk(c, accs, group=group):
            r0 = pl.multiple_of(c * CONV_ROWS, CONV_ROWS)
            a = a_buf[pl.ds(r0, CONV_ROWS), :]
            return tuple(acc + (a * _tap_rows(sbuf, off, r0)).reshape(CONV_ROWS // SUBLANES, SUBLANES, DC).sum(axis=0)
                         for acc, off in zip(accs, group))

        accs = lax.fori_loop(0, ts // CONV_ROWS, chunk, tuple(jnp.zeros((SUBLANES, DC), F32) for _ in group))
        outs += [jnp.sum(acc, axis=0, keepdims=True) for acc in accs]
    return outs


CF_TAPS = tuple(HALO - (CFW - 1) + k for k in range(CFW))
SC_TAPS = tuple(HALO - (SCW - 1) + k for k in range(SCW))
CF_TAPS_T = tuple(CFW - 1 - k for k in range(CFW))
SC_TAPS_T = tuple(SCW - 1 - k for k in range(SCW))


def _conformer_fwd(a, g, ha, hg, first, ubuf, usbuf, u1buf, dw_ref, bias, lng, lnb):
    ts = a.shape[0]
    u0 = a * _sigmoid(g)
    ubuf[0:HALO, :] = jnp.where(first, 0.0, ha * _sigmoid(hg))
    ubuf[HALO:HALO + ts, :] = u0
    _shift_copies(ubuf, usbuf, range(SUBLANES))
    _tap_conv(usbuf, dw_ref, CF_TAPS, u1buf, ts, bias)
    u1 = u1buf[...]
    mu = jnp.mean(u1, axis=-1, keepdims=True)
    xc = u1 - mu
    rstd = lax.rsqrt(jnp.mean(xc * xc, axis=-1, keepdims=True) + EPS)
    n = xc * rstd
    u2 = n * lng + lnb
    s2 = _sigmoid(u2)
    u3 = u2 * s2
    return u0, n, rstd, u2, s2, u3


def _shortconv_fwd(c, xs, hc, hx, first, mbuf, msbuf, cvbuf, dw_ref):
    ts = c.shape[0]
    mbuf[0:HALO, :] = jnp.where(first, 0.0, hc * hx)
    mbuf[HALO:HALO + ts, :] = c * xs
    _shift_copies(mbuf, msbuf, sorted({off % SUBLANES for off in SC_TAPS}))
    _tap_conv(msbuf, dw_ref, SC_TAPS, cvbuf, ts, 0.0)
    return cvbuf[...]


def _conv_specs(bsz, seq, ts, order):
    nt = seq // ts
    tile = (lambda i: i) if order > 0 else (lambda i: nt - 1 - i)
    hrow = lambda i: jnp.maximum(tile(i) * (ts // HALO) - 1, 0)
    cur = lambda n: pl.BlockSpec((None, ts, n), lambda b, i: (b, tile(i), 0))
    halo = lambda j: pl.BlockSpec((None, HALO, DC), lambda b, i: (b, hrow(i), j))
    full = lambda r, c: pl.BlockSpec((r, c), lambda b, i: (0, 0))
    return nt, tile, cur, halo, full


def _fwd_conv(pcf, psc, cf_dw, cf_dw_b, ln_g, ln_b, cf_pw, sc_dw, name):
    bsz, seq, _ = pcf.shape
    ts = 512
    nt, tile, cur, halo, full = _conv_specs(bsz, seq, ts, +1)

    def body(cf_ref, ha_ref, hg_ref, sc_ref, hc_ref, hx_ref, dw_ref, b_ref, lg_ref, lb_ref, pw_ref, sdw_ref,
             y_ref, ubuf, sbuf, obuf):
        first = pl.program_id(1) == 0
        _, _, _, _, _, u3 = _conformer_fwd(cf_ref[:, 0:256], cf_ref[:, 256:512], ha_ref[...], hg_ref[...], first,
                                           ubuf, sbuf, obuf, dw_ref, b_ref[...], lg_ref[...], lb_ref[...])
        z = cf_ref[:, 512:768]
        y_ref[:, 0:256] = _dot(u3.astype(BF), pw_ref[...]) * (z * _sigmoid(z))
        cv = _shortconv_fwd(sc_ref[:, 256:512], sc_ref[:, 512:768], hc_ref[...], hx_ref[...], first,
                            ubuf, sbuf, obuf, sdw_ref)
        zs = sc_ref[:, 768:1024]
        y_ref[:, 256:512] = sc_ref[:, 0:256] * cv * (zs * _sigmoid(zs))

    return pl.pallas_call(
        body, grid=(bsz, nt),
        in_specs=[cur(768), halo(0), halo(1), cur(1024), halo(1), halo(2),
                  full(CFW, DC), full(1, DC), full(1, DC), full(1, DC), full(DC, DC), full(SCW, DC)],
        out_specs=pl.BlockSpec((None, ts, 512), lambda b, i: (b, i, 0)),
        out_shape=jax.ShapeDtypeStruct((bsz, seq, 512), F32),
        scratch_shapes=[pltpu.VMEM((HALO + ts, DC), F32), pltpu.VMEM((SUBLANES, HALO + ts, DC), F32),
                        pltpu.VMEM((ts, DC), F32)],
        compiler_params=_cparams(("arbitrary", "arbitrary")), name=name,
    )(pcf, pcf, pcf, psc, psc, psc, cf_dw, cf_dw_b, ln_g, ln_b, cf_pw, sc_dw)


def _head_rms(xb, gb):
    ms = _seg_sum64(xb * xb) * (1.0 / HD)
    r = lax.rsqrt(ms + EPS)
    xhat = xb * r
    return xhat, r, xhat * gb


def _fwd_attn_prep(pq, pk, pf, gq, gk, bf, name):
    bsz, seq, _ = pq.shape
    ts = 512
    nt = seq // ts

    def body(q_ref, k_ref, f_ref, gq_ref, gk_ref, bf_ref, qs_ref, kn_ref, cq_ref, crow_ref, carry):
        @pl.when(pl.program_id(1) == 0)
        def _():
            carry[...] = jnp.zeros_like(carry)

        for jb in range(DA // LANES):
            sl = slice(jb * LANES, (jb + 1) * LANES)
            _, _, qn = _head_rms(q_ref[:, sl], gq_ref[:, sl])
            qs_ref[:, sl] = (qn * (1.0 / 8.0)).astype(BF)
            _, _, kn = _head_rms(k_ref[:, sl], gk_ref[:, sl])
            kn_ref[:, sl] = kn.astype(BF)

        xf = f_ref[...] + bf_ref[...]
        lf = jnp.minimum(xf, 0.0) - jnp.log(1.0 + jnp.exp(-jnp.abs(xf)))
        ti = lax.broadcasted_iota(jnp.int32, (ts, ts), 0)
        si = lax.broadcasted_iota(jnp.int32, (ts, ts), 1)
        c = _dot_exact((si <= ti).astype(BF), lf) + carry[...]
        carry[...] = c[ts - 1:ts, :]
        hj = lax.broadcasted_iota(jnp.int32, (LANES, DA), 0)
        ll = lax.broadcasted_iota(jnp.int32, (LANES, DA), 1)
        dd = ll - hj * HD
        chi, cmid, clo = _split3(c)
        e = ((dd >= 0) & (dd < HD)).astype(BF)
        cq_ref[...] = _dot(chi, e) + _dot(cmid, e) + _dot(clo, e)
        hh = lax.broadcasted_iota(jnp.int32, (16, LANES), 0)
        jj = lax.broadcasted_iota(jnp.int32, (16, LANES), 1)
        sel = (hh == jj).astype(BF)
        cr = _dot_nt(sel, chi) + _dot_nt(sel, cmid) + _dot_nt(sel, clo)
        crow_ref[...] = cr[0:NH, :]

    tile = lambda n: pl.BlockSpec((None, ts, n), lambda b, i: (b, i, 0))
    vec = lambda n: pl.BlockSpec((1, n), lambda b, i: (0, 0))
    return pl.pallas_call(
        body, grid=(bsz, nt),
        in_specs=[tile(DA), tile(DA), tile(LANES), vec(DA), vec(DA), vec(LANES)],
        out_specs=[tile(DA), tile(DA), tile(DA), pl.BlockSpec((None, NH, ts), lambda b, i: (b, 0, i))],
        out_shape=[jax.ShapeDtypeStruct((bsz, seq, DA), BF), jax.ShapeDtypeStruct((bsz, seq, DA), BF),
                   jax.ShapeDtypeStruct((bsz, seq, DA), F32), jax.ShapeDtypeStruct((bsz, NH, seq), F32)],
        scratch_shapes=[pltpu.VMEM((1, LANES), F32)],
        compiler_params=_cparams(("arbitrary", "arbitrary")), name=name)(pq, pk, pf, gq, gk, bf)


def _tile_rows(i, rows):
    return pl.ds(i * rows, rows) if isinstance(i, int) else pl.ds(pl.multiple_of(i * rows, rows), rows)


def _causal_mask(tq, tk):
    r = lax.broadcasted_iota(jnp.int32, (tq, tk), 0)
    c = lax.broadcasted_iota(jnp.int32, (tq, tk), 1)
    return r - c


def _fwd_attn(qs, kn, vb, cq, crow, pz, name, ex=None):
    bsz, seq, _ = qs.shape
    tq, tk = 256, 256
    nq, nk = seq // tq, seq // tk
    npair = NH // 2

    def body(q_ref, k_ref, v_ref, cq_ref, cr_ref, z_ref, o_ref, ohp_ref, y_ref, lse_ref,
             s_buf, p_buf, acc_ref, accl_ref):
        qi = pl.program_id(2)
        head0 = lax.broadcasted_iota(jnp.int32, (1, LANES), 1) < HD
        q = q_ref[...]
        zq = jnp.zeros_like(q)
        qm = (jnp.where(head0, q, zq), jnp.where(head0, zq, q))
        ct = (cq_ref[:, 0:1], cq_ref[:, HD:HD + 1])

        tile = lambda j: _tile_rows(j, tk)

        def scores(j, slot):
            kb = k_ref[tile(j), :]
            for hh in range(2):
                s_buf[slot, hh] = _dot_nt(qm[hh], kb) + (ct[hh] - cr_ref[hh:hh + 1, tile(j)])

        def weighted_values(slot, j, al):
            vb_t = v_ref[tile(j), :]
            acc_ref[...] = al * acc_ref[...] + jnp.where(head0, _dot(p_buf[slot, 0, 0], vb_t),
                                                         _dot(p_buf[slot, 1, 0], vb_t))
            accl_ref[...] = al * accl_ref[...] + jnp.where(head0, _dot(p_buf[slot, 0, 1], vb_t),
                                                           _dot(p_buf[slot, 1, 1], vb_t))

        def softmax(slot, first_visible, m, l):
            m_new, l_new, al_new = ([], []), ([], []), []
            for r in range(tq // FWD_ROWS):
                rows = slice(r * FWD_ROWS, (r + 1) * FWD_ROWS)
                visible = _causal_mask(FWD_ROWS, tk) >= first_visible - r * FWD_ROWS
                alphas = []
                for hh in range(2):
                    s = jnp.where(visible, s_buf[slot, hh, rows, :], -1e30)
                    m_old = m[hh][rows]
                    m2 = jnp.maximum(m_old, jnp.max(s, axis=-1, keepdims=True))
                    p = jnp.exp(s - m2)
                    alpha = jnp.exp(m_old - m2)
                    l_new[hh].append(alpha * l[hh][rows] + jnp.sum(p, axis=-1, keepdims=True))
                    m_new[hh].append(m2)
                    pb = p.astype(BF)
                    p_buf[slot, hh, 0, rows, :] = pb
                    p_buf[slot, hh, 1, rows, :] = (p - pb.astype(F32)).astype(BF)
                    alphas.append(alpha)
                al_new.append(jnp.where(head0, alphas[0], alphas[1]))
            cat = lambda parts: jnp.concatenate(parts, axis=0)
            return (cat(m_new[0]), cat(m_new[1])), (cat(l_new[0]), cat(l_new[1])), cat(al_new)

        clamp = lambda j: jnp.clip(j, 0, nk - 1)

        def step(j, slot, carry):
            m, l, al = carry
            weighted_values(1 - slot, clamp(j - 1), al)
            m, l, al = softmax(slot, j * tk - qi * tq, m, l)
            scores(clamp(j + 1), 1 - slot)
            return m, l, al

        scores(0, 0)
        p_buf[1] = jnp.zeros((2, 2, tq, tk), BF)
        acc_ref[...] = jnp.zeros((tq, LANES), F32)
        accl_ref[...] = jnp.zeros((tq, LANES), F32)
        neg = jnp.full((tq, 1), -1e30, F32)
        zcol = jnp.zeros((tq, 1), F32)
        trips = ((qi + 1) * (tq // tk) + 1) // 2
        m, l, al = lax.fori_loop(0, trips, lambda t, cy: step(2 * t + 1, 1, step(2 * t, 0, cy)),
                                 ((neg, neg), (zcol, zcol), jnp.ones((tq, LANES), F32)))
        weighted_values(1, clamp(2 * trips - 1), al)
        inv = jnp.where(head0, 1.0 / l[0], 1.0 / l[1])
        o = acc_ref[...] * inv
        z = z_ref[...]
        o_ref[...] = o
        ohp_ref[...] = (acc_ref[...] + accl_ref[...]) * inv
        y_ref[...] = o * (z * _sigmoid(z))
        lse_ref[...] = jnp.where(head0, m[0] + jnp.log(l[0]), m[1] + jnp.log(l[1]))

    qblk = pl.BlockSpec((None, tq, LANES), lambda b, h, i: (b, i, h))
    kvblk = pl.BlockSpec((None, seq, LANES), lambda b, h, i: (b, 0, h))
    return _host_call(
        body, (bsz, npair, nq),
        [qblk, kvblk, kvblk, qblk, pl.BlockSpec((None, None, 2, seq), lambda b, h, i: (b, h, 0, 0)), qblk],
        [qblk, qblk, qblk, qblk], [jax.ShapeDtypeStruct((bsz, seq, DA), F32)] * 4,
        [pltpu.VMEM((2, 2, tq, tk), F32), pltpu.VMEM((2, 2, 2, tq, tk), BF), pltpu.VMEM((tq, LANES), F32),
         pltpu.VMEM((tq, LANES), F32)],
        (qs, kn, vb, cq, crow, pz), name, ex)


def _fwd_outproj(x, ycs, yatt, wo, name):
    t = x.shape[0]
    tm = 512

    def body(x_ref, a_ref, b_ref, w_ref, o_ref):
        o_ref[...] = (x_ref[...] + _dot(a_ref[...].astype(BF), w_ref[0:512, :])
                      + _dot(b_ref[...].astype(BF), w_ref[512:1024, :]))

    row = lambda n: pl.BlockSpec((tm, n), lambda i: (i, 0))
    return pl.pallas_call(
        body, grid=(t // tm,),
        in_specs=[row(D), row(512), row(512), pl.BlockSpec((D, D), lambda i: (0, 0))],
        out_specs=row(D), out_shape=jax.ShapeDtypeStruct((t, D), F32),
        compiler_params=_cparams(("arbitrary",)), name=name)(x, ycs, yatt, wo)


def _fwd_outproj_loss(x, ycs, yatt, wo, target, name):
    t = x.shape[0]
    tm = 512

    def body(x_ref, a_ref, b_ref, w_ref, t_ref, dy_ref, loss_ref):
        @pl.when(pl.program_id(0) == 0)
        def _():
            loss_ref[...] = jnp.zeros_like(loss_ref)

        y = (x_ref[...] + _dot(a_ref[...].astype(BF), w_ref[0:512, :])
             + _dot(b_ref[...].astype(BF), w_ref[512:1024, :]))
        err = y - t_ref[...]
        dy_ref[...] = err * (1.0 / D)
        per_tok = jnp.mean(err * err, axis=-1, keepdims=True)
        loss_ref[...] += 0.5 * jnp.sum(per_tok, axis=0, keepdims=True)

    row = lambda n: pl.BlockSpec((tm, n), lambda i: (i, 0))
    return pl.pallas_call(
        body, grid=(t // tm,),
        in_specs=[row(D), row(512), row(512), pl.BlockSpec((D, D), lambda i: (0, 0)), row(D)],
        out_specs=[row(D), pl.BlockSpec((1, 1), lambda i: (0, 0))],
        out_shape=[jax.ShapeDtypeStruct((t, D), F32), jax.ShapeDtypeStruct((1, 1), F32)],
        compiler_params=_cparams(("arbitrary",)), name=name)(x, ycs, yatt, wo, target)


def _dsilu(x, s):
    return s * (1.0 + x * (1.0 - s))


def _bwd_outproj(dy, ycs, yatt, o, ohp, pz, wo, name, ex=None):
    t = dy.shape[0]
    tm = 256

    def body(dy_ref, a_ref, b_ref, o_ref, ohp_ref, z_ref, w_ref, dcs_ref, do_ref, dz_ref, dl_ref, dw_ref):
        @pl.when(pl.program_id(0) == 0)
        def _():
            dw_ref[...] = jnp.zeros_like(dw_ref)

        dyb = dy_ref[...].astype(BF)
        dw_ref[0:512, :] += _dot_tn(a_ref[...].astype(BF), dyb)
        dw_ref[512:1024, :] += _dot_tn(b_ref[...].astype(BF), dyb)
        dcs_ref[...] = _dot_nt(dyb, w_ref[0:512, :])
        dyatt = _dot_nt(dyb, w_ref[512:1024, :])
        z = z_ref[...]
        sz = _sigmoid(z)
        o_t = o_ref[...]
        dob = (dyatt * (z * sz)).astype(BF)
        do_ref[...] = dob
        dz_ref[...] = dyatt * o_t * _dsilu(z, sz)
        prod = dob.astype(F32) * ohp_ref[...]
        for jb in range(DA // LANES):
            sl = slice(jb * LANES, (jb + 1) * LANES)
            dl_ref[:, sl] = _seg_sum64(prod[:, sl])

    row = lambda n: pl.BlockSpec((tm, n), lambda i: (i, 0))
    return _host_call(
        body, (t // tm,),
        [row(D), row(512), row(512), row(DA), row(DA), row(DA), pl.BlockSpec((D, D), lambda i: (0, 0))],
        [row(512), row(DA), row(DA), row(DA), pl.BlockSpec((D, D), lambda i: (0, 0))],
        [jax.ShapeDtypeStruct((t, 512), F32), jax.ShapeDtypeStruct((t, DA), BF),
         jax.ShapeDtypeStruct((t, DA), F32), jax.ShapeDtypeStruct((t, DA), F32),
         jax.ShapeDtypeStruct((D, D), F32)], [],
        (dy, ycs, yatt, o, ohp, pz, wo), name, ex)


def _bwd_attn(qs, kn, vb, do, cq, lse, dl, crow, name, ex=None):
    bsz, seq, _ = qs.shape
    tq, tk = 256, min(BWD_KEYS, seq)
    nq, nk = seq // tq, seq // tk
    rows_c = ATT_ROWS * tq // tk
    npair = NH // 2

    def body(q_ref, k_ref, v_ref, do_ref, cq_ref, lse_ref, dl_ref, cr_ref, dq_ref, dk_ref, dv_ref, dc_ref,
             qm, dom, s_buf, dp_buf, pd_buf):
        head0 = lax.broadcasted_iota(jnp.int32, (1, LANES), 1) < HD
        zb = jnp.zeros((seq, LANES), BF)
        qm[0] = jnp.where(head0, q_ref[...], zb)
        qm[1] = jnp.where(head0, zb, q_ref[...])
        dom[0] = jnp.where(head0, do_ref[...], zb)
        dom[1] = jnp.where(head0, zb, do_ref[...])
        dq_ref[...] = jnp.zeros_like(dq_ref)

        def kloop(kj, _):
            krows = pl.ds(pl.multiple_of(kj * tk, tk), tk)
            kb = k_ref[krows, :]
            vb_t = v_ref[krows, :]
            zk = jnp.zeros_like(kb)
            km = (jnp.where(head0, kb, zk), jnp.where(head0, zk, kb))

            tile = lambda i: _tile_rows(i, tq)

            def scores(i, slot):
                for hh in range(2):
                    c0 = hh * HD
                    s_buf[slot, hh] = (_dot_nt(qm[hh, tile(i), :], kb)
                                       + (cq_ref[tile(i), c0:c0 + 1] - cr_ref[hh:hh + 1, krows]))
                    dp_buf[slot, hh] = _dot_nt(dom[hh, tile(i), :], vb_t)

            def products(slot, i, dk, dv):
                dq = jnp.zeros((tq, LANES), F32)
                for hh in range(2):
                    dsb = pd_buf[slot, hh, 1]
                    dv = dv + _dot_tn(pd_buf[slot, hh, 0], dom[hh, tile(i), :])
                    dk = dk + _dot_tn(dsb, qm[hh, tile(i), :])
                    dq = dq + _dot(dsb, km[hh])
                dq_ref[tile(i), :] += dq
                return dk, dv

            clamp = lambda i: jnp.clip(i, 0, nq - 1)

            def qstep(i, slot, carry):
                dk, dv, cs = carry
                dk, dv = products(1 - slot, clamp(i - 1), dk, dv)
                cs = list(cs)
                ic = clamp(i)
                first_visible = jnp.where(i < nq, kj * tk - i * tq, 2 * tk)
                for r in range(tq // rows_c):
                    rows = slice(r * rows_c, (r + 1) * rows_c)
                    qrows = pl.ds(pl.multiple_of(ic * tq + r * rows_c, rows_c), rows_c)
                    visible = _causal_mask(rows_c, tk) >= first_visible - r * rows_c
                    for hh in range(2):
                        c0 = hh * HD
                        s = jnp.where(visible, s_buf[slot, hh, rows, :], -1e30)
                        p = jnp.exp(s - lse_ref[qrows, c0:c0 + 1])
                        ds = p * (dp_buf[slot, hh, rows, :] - dl_ref[qrows, c0:c0 + 1])
                        pd_buf[slot, hh, 0, rows, :] = p.astype(BF)
                        pd_buf[slot, hh, 1, rows, :] = ds.astype(BF)
                        cs[hh] = cs[hh] + jnp.sum(ds, axis=0, keepdims=True)
                scores(clamp(i + 1), 1 - slot)
                return dk, dv, tuple(cs)

            i0 = kj * (tk // tq)
            scores(i0, 0)
            pd_buf[1] = jnp.zeros((2, 2, tq, tk), BF)
            zero = jnp.zeros((tk, LANES), F32)
            zrow = jnp.zeros((1, tk), F32)
            trips = (nq - i0 + 1) // 2
            dk, dv, cs = lax.fori_loop(
                0, trips, lambda t, cy: qstep(i0 + 2 * t + 1, 1, qstep(i0 + 2 * t, 0, cy)), (zero, zero, (zrow, zrow)))
            dk, dv = products(1, clamp(i0 + 2 * trips - 1), dk, dv)
            dk_ref[krows, :] = dk
            dv_ref[krows, :] = dv
            dc_ref[0:1, krows] = -cs[0]
            dc_ref[1:2, krows] = -cs[1]
            return 0

        lax.fori_loop(0, nk, kloop, 0)

    blk = pl.BlockSpec((None, seq, LANES), lambda b, h: (b, 0, h))
    rowblk = pl.BlockSpec((None, None, 2, seq), lambda b, h: (b, h, 0, 0))
    return _host_call(
        body, (bsz, npair), [blk, blk, blk, blk, blk, blk, blk, rowblk], [blk, blk, blk, rowblk],
        [jax.ShapeDtypeStruct((bsz, seq, DA), F32)] * 3 + [jax.ShapeDtypeStruct((bsz, npair, 2, seq), F32)],
        [pltpu.VMEM((2, seq, LANES), BF), pltpu.VMEM((2, seq, LANES), BF), pltpu.VMEM((2, 2, tq, tk), F32),
         pltpu.VMEM((2, 2, tq, tk), F32), pltpu.VMEM((2, 2, 2, tq, tk), BF)],
        (qs, kn, vb, do, cq, lse, dl, crow), name, ex)


def _bwd_attn_post(pq, pk, dqs, dkn, pf, dcrow, gq, gk, bf, name, ex=None):
    bsz, seq, _ = pq.shape
    ts = 512
    nt = seq // ts

    def body(q_ref, k_ref, dq_ref, dk_ref, f_ref, dc_ref, gq_ref, gk_ref, bf_ref,
             dpq_ref, dpk_ref, dpf_ref, dgq_ref, dgk_ref, dbf_ref, carry):
        @pl.when((pl.program_id(0) == 0) & (pl.program_id(1) == 0))
        def _():
            dgq_ref[...] = jnp.zeros_like(dgq_ref)
            dgk_ref[...] = jnp.zeros_like(dgk_ref)
            dbf_ref[...] = jnp.zeros_like(dbf_ref)

        @pl.when(pl.program_id(1) == 0)
        def _():
            carry[...] = jnp.zeros_like(carry)

        for x_ref, dx_ref, g_ref, dp_ref, dg_ref, scale in ((q_ref, dq_ref, gq_ref, dpq_ref, dgq_ref, 1.0 / 8.0),
                                                            (k_ref, dk_ref, gk_ref, dpk_ref, dgk_ref, 1.0)):
            for jb in range(DA // LANES):
                sl = slice(jb * LANES, (jb + 1) * LANES)
                gb = g_ref[:, sl]
                xhat, r, _ = _head_rms(x_ref[:, sl], gb)
                dn = dx_ref[:, sl] * scale
                dg_ref[:, sl] += jnp.sum(dn * xhat, axis=0, keepdims=True)
                dxh = dn * gb
                dp_ref[:, sl] = r * (dxh - xhat * (_seg_sum64(dxh * xhat) * (1.0 / HD)))

        ui = lax.broadcasted_iota(jnp.int32, (ts, ts), 0)
        tj = lax.broadcasted_iota(jnp.int32, (ts, ts), 1)
        tri = (tj >= ui).astype(BF)
        dc = jnp.concatenate([dc_ref[...], jnp.zeros((LANES - NH, ts), F32)], axis=0)
        hi, mid, lo = _split3(dc)
        dlf = _dot_nt(tri, hi) + _dot_nt(tri, mid) + _dot_nt(tri, lo) + carry[...]
        carry[...] = dlf[0:1, :]
        xf = f_ref[...] + bf_ref[...]
        lane = lax.broadcasted_iota(jnp.int32, (ts, LANES), 1)
        dfl = jnp.where(lane < NH, dlf * _sigmoid(-xf), 0.0)
        dpf_ref[...] = dfl
        dbf_ref[...] += jnp.sum(dfl, axis=0, keepdims=True)

    rev = lambda i: nt - 1 - i
    tile = lambda n: pl.BlockSpec((None, ts, n), lambda b, i: (b, rev(i), 0))
    vec = lambda n: pl.BlockSpec((1, n), lambda b, i: (0, 0))
    return _host_call(
        body, (bsz, nt),
        [tile(DA), tile(DA), tile(DA), tile(DA), tile(LANES),
         pl.BlockSpec((None, NH, ts), lambda b, i: (b, 0, rev(i))), vec(DA), vec(DA), vec(LANES)],
        [tile(DA), tile(DA), tile(LANES), vec(DA), vec(DA), vec(LANES)],
        [jax.ShapeDtypeStruct((bsz, seq, DA), F32), jax.ShapeDtypeStruct((bsz, seq, DA), F32),
         jax.ShapeDtypeStruct((bsz, seq, LANES), F32), jax.ShapeDtypeStruct((1, DA), F32),
         jax.ShapeDtypeStruct((1, DA), F32), jax.ShapeDtypeStruct((1, LANES), F32)],
        [pltpu.VMEM((1, LANES), F32)], (pq, pk, dqs, dkn, pf, dcrow, gq, gk, bf), name, ex)


def _bwd_conv(pcf, psc, dycs, cf_dw, cf_dw_b, ln_g, ln_b, cf_pw, sc_dw, name):
    bsz, seq, _ = pcf.shape
    ts = 512
    nt, tile, cur, halo, full = _conv_specs(bsz, seq, ts, -1)

    def body(cf_ref, ha_ref, hg_ref, sc_ref, hc_ref, hx_ref, dy_ref, dw_ref, b_ref, lg_ref, lb_ref, pw_ref, sdw_ref,
             dcf_ref, dsc_ref, ddw_ref, db_ref, dlg_ref, dlb_ref, dpw_ref, dsdw_ref,
             ubuf, usbuf, obuf, dubuf, dsbuf, carry_du, carry_dc):
        step = pl.program_id(1)
        first = tile(step) == 0

        @pl.when((pl.program_id(0) == 0) & (step == 0))
        def _():
            for r in (ddw_ref, db_ref, dlg_ref, dlb_ref, dpw_ref, dsdw_ref):
                r[...] = jnp.zeros_like(r)

        @pl.when(step == 0)
        def _():
            carry_du[...] = jnp.zeros((HALO, DC), F32)
            carry_dc[...] = jnp.zeros((HALO, DC), F32)

        def transposed_conv(d, carry, w_ref, taps_t):
            dubuf[0:ts, :] = d
            dubuf[ts:ts + HALO, :] = carry[...]
            carry[...] = d[0:HALO, :]
            _shift_copies(dubuf, dsbuf, sorted({off % SUBLANES for off in taps_t}))
            _tap_conv(dsbuf, w_ref, taps_t, obuf, ts, 0.0)
            return obuf[...]

        a = cf_ref[:, 0:256]
        g = cf_ref[:, 256:512]
        z = cf_ref[:, 512:768]
        lng = lg_ref[...]
        u0, n, rstd, u2, s2, u3 = _conformer_fwd(a, g, ha_ref[...], hg_ref[...], first, ubuf, usbuf, obuf, dw_ref,
                                                 b_ref[...], lng, lb_ref[...])
        u3b = u3.astype(BF)
        p = _dot(u3b, pw_ref[...])
        sz = _sigmoid(z)
        dy = dy_ref[:, 0:256]
        dcf_ref[:, 512:768] = dy * p * _dsilu(z, sz)
        dpb = (dy * (z * sz)).astype(BF)
        dpw_ref[...] += _dot_tn(u3b, dpb)
        du2 = _dot_nt(dpb, pw_ref[...]) * _dsilu(u2, s2)
        dlg_ref[...] += jnp.sum(du2 * n, axis=0, keepdims=True)
        dlb_ref[...] += jnp.sum(du2, axis=0, keepdims=True)
        dn = du2 * lng
        du1 = rstd * (dn - jnp.mean(dn, axis=-1, keepdims=True) - n * jnp.mean(dn * n, axis=-1, keepdims=True))
        db_ref[...] += jnp.sum(du1, axis=0, keepdims=True)
        du0 = transposed_conv(du1, carry_du, dw_ref, CF_TAPS_T)
        for k, tap_grad in enumerate(_tap_sums(dubuf, usbuf, CF_TAPS, ts)):
            ddw_ref[k:k + 1, :] += tap_grad
        sg = _sigmoid(g)
        dcf_ref[:, 0:256] = du0 * sg
        dcf_ref[:, 256:512] = du0 * a * sg * (1.0 - sg)

        bb = sc_ref[:, 0:256]
        c = sc_ref[:, 256:512]
        xs = sc_ref[:, 512:768]
        zs = sc_ref[:, 768:1024]
        cv = _shortconv_fwd(c, xs, hc_ref[...], hx_ref[...], first, ubuf, usbuf, obuf, sdw_ref)
        szs = _sigmoid(zs)
        dys = dy_ref[:, 256:512]
        gate = zs * szs
        dsc_ref[:, 0:256] = dys * cv * gate
        dsc_ref[:, 768:1024] = dys * bb * cv * _dsilu(zs, szs)
        dm = transposed_conv(dys * bb * gate, carry_dc, sdw_ref, SC_TAPS_T)
        for k, tap_grad in enumerate(_tap_sums(dubuf, usbuf, SC_TAPS, ts)):
            dsdw_ref[k:k + 1, :] += tap_grad
        dsc_ref[:, 256:512] = dm * xs
        dsc_ref[:, 512:768] = dm * c

    outt = lambda n: pl.BlockSpec((None, ts, n), lambda b, i: (b, tile(i), 0))
    return pl.pallas_call(
        body, grid=(bsz, nt),
        in_specs=[cur(768), halo(0), halo(1), cur(1024), halo(1), halo(2), cur(512),
                  full(CFW, DC), full(1, DC), full(1, DC), full(1, DC), full(DC, DC), full(SCW, DC)],
        out_specs=[outt(768), outt(1024), full(CFW, DC), full(1, DC), full(1, DC), full(1, DC), full(DC, DC),
                   full(SCW, DC)],
        out_shape=[jax.ShapeDtypeStruct((bsz, seq, 768), F32), jax.ShapeDtypeStruct((bsz, seq, 1024), F32),
                   jax.ShapeDtypeStruct((CFW, DC), F32), jax.ShapeDtypeStruct((1, DC), F32),
                   jax.ShapeDtypeStruct((1, DC), F32), jax.ShapeDtypeStruct((1, DC), F32),
                   jax.ShapeDtypeStruct((DC, DC), F32), jax.ShapeDtypeStruct((SCW, DC), F32)],
        scratch_shapes=[pltpu.VMEM((HALO + ts, DC), F32), pltpu.VMEM((SUBLANES, HALO + ts, DC), F32),
                        pltpu.VMEM((ts, DC), F32), pltpu.VMEM((ts + HALO, DC), F32),
                        pltpu.VMEM((SUBLANES, ts + HALO, DC), F32), pltpu.VMEM((HALO, DC), F32),
                        pltpu.VMEM((HALO, DC), F32)],
        compiler_params=_cparams(("arbitrary", "arbitrary")), name=name,
    )(pcf, pcf, pcf, psc, psc, psc, dycs, cf_dw, cf_dw_b, ln_g, ln_b, cf_pw, sc_dw)


def _bwd_inproj(x, g, dyres, w, dcf, dsc, dq, dk, dv, dz, df, name):
    t = x.shape[0]
    tm = 256
    pieces = ((C_CF, 768), (C_SC, 1024), (C_Q, DA), (C_K, DA), (C_V, DA), (C_Z, DA), (C_F, LANES))

    def body(x_ref, g_ref, dy_ref, w_hbm, dcf_ref, dsc_ref, dq_ref, dk_ref, dv_ref, dz_ref, df_ref,
             dx_ref, dg_ref, dw_hbm, w_vmem, dw_acc, sem):
        i = pl.program_id(0)

        @pl.when(i == 0)
        def _():
            cp = pltpu.make_async_copy(w_hbm, w_vmem, sem)
            cp.start()
            dw_acc[...] = jnp.zeros_like(dw_acc)
            dg_ref[...] = jnp.zeros_like(dg_ref)
            cp.wait()

        xt = x_ref[...]
        gg = g_ref[...]
        r = lax.rsqrt(jnp.mean(xt * xt, axis=-1, keepdims=True) + EPS)
        xhat = xt * r
        ht = (xhat * gg).astype(BF).T
        dh = jnp.zeros((tm, D), F32)
        for (c0, n), ref in zip(pieces, (dcf_ref, dsc_ref, dq_ref, dk_ref, dv_ref, dz_ref, df_ref)):
            for s0 in range(0, n, 512):
                s1 = min(s0 + 512, n)
                d = ref[:, s0:s1].astype(BF)
                dh = dh + _dot_nt(d, w_vmem[:, c0 + s0:c0 + s1])
                dw_acc[:, c0 + s0:c0 + s1] += _dot(ht, d)
        dg_ref[...] += jnp.sum(dh * xhat, axis=0, keepdims=True)
        dhg = dh * gg
        dx_ref[...] = dy_ref[...] + r * (dhg - xhat * jnp.mean(dhg * xhat, axis=-1, keepdims=True))

        @pl.when(i == pl.num_programs(0) - 1)
        def _():
            out = pltpu.make_async_copy(dw_acc, dw_hbm, sem)
            out.start()
            out.wait()

    row = lambda n: pl.BlockSpec((tm, n), lambda i: (i, 0))
    anyspec = pl.BlockSpec(memory_space=pl.ANY)
    return pl.pallas_call(
        body, grid=(t // tm,),
        in_specs=[row(D), pl.BlockSpec((1, D), lambda i: (0, 0)), row(D), anyspec,
                  row(768), row(1024), row(DA), row(DA), row(DA), row(DA), row(LANES)],
        out_specs=[row(D), pl.BlockSpec((1, D), lambda i: (0, 0)), anyspec],
        out_shape=[jax.ShapeDtypeStruct((t, D), F32), jax.ShapeDtypeStruct((1, D), F32),
                   jax.ShapeDtypeStruct((D, NP), F32)],
        scratch_shapes=[pltpu.VMEM((D, NP), BF), pltpu.VMEM((D, NP), F32), pltpu.SemaphoreType.DMA],
        compiler_params=_cparams(("arbitrary",)), name=name)(x, g, dyres, w, dcf, dsc, dq, dk, dv, dz, df)


def _adamw_update(w_ref, g_ref, m_ref, v_ref, d_ref, m2_ref, v2_ref):
    gt = g_ref[...]
    m2 = ADAM_B1 * m_ref[...] + (1.0 - ADAM_B1) * gt
    v2 = ADAM_B2 * v_ref[...] + (1.0 - ADAM_B2) * (gt * gt)
    m_hat = m2 / (1.0 - ADAM_B1 ** ADAM_STEP)
    v_hat = v2 / (1.0 - ADAM_B2 ** ADAM_STEP)
    d_ref[...] = -ADAM_LR * (m_hat / (jnp.sqrt(v_hat) + ADAM_EPS) + ADAM_WD * w_ref[...])
    m2_ref[...] = m2
    v2_ref[...] = v2


def _adamw_many(ws, gs, ms, vs, name):
    n = len(ws)

    def body(*refs):
        ins, outs = refs[:4 * n], refs[4 * n:]
        for i in range(n):
            _adamw_update(ins[i], ins[n + i], ins[2 * n + i], ins[3 * n + i], *outs[3 * i:3 * i + 3])

    return pl.pallas_call(
        body, out_shape=[jax.ShapeDtypeStruct(w.shape, F32) for w in ws for _ in range(3)],
        name=name)(*ws, *gs, *ms, *vs)


def _adamw(w, g, m, v, tr, name):
    rows, cols = w.shape

    def body(w_ref, g_ref, m_ref, v_ref, d_ref, m2_ref, v2_ref, g_out_ref):
        _adamw_update(w_ref, g_ref, m_ref, v_ref, d_ref, m2_ref, v2_ref)
        g_out_ref[...] = g_ref[...]

    blk = pl.BlockSpec((tr, cols), lambda i: (i, 0))
    return pl.pallas_call(
        body, grid=(rows // tr,), in_specs=[blk] * 4, out_specs=[blk] * 4,
        out_shape=[jax.ShapeDtypeStruct((rows, cols), F32)] * 4,
        compiler_params=_cparams(("arbitrary",)), name=name)(w, g, m, v)


def _place():
    x, y, c = lax.axis_index("x"), lax.axis_index("y"), lax.axis_index("c")
    chips = [(1 - x, y), (x, 1 - y), (1 - x, 1 - y)]
    return x, y, c, chips


def _halves(c, rows, align):
    rh = rows // 2
    return pl.ds(pl.multiple_of(c * rh, align), rh), pl.ds(pl.multiple_of((1 - c) * rh, align), rh)


ANY = pl.BlockSpec(memory_space=pl.ANY)


class _Exchange:
    def __init__(self, operands, out_shape, sems, start, wait):
        self.operands, self.out_shape, self.sems, self.start, self.wait = operands, out_shape, sems, start, wait


class _StagedCopy:
    def __init__(self, src, dst, stage, sem):
        self.src, self.dst, self.stage, self.sem = src, dst, stage, sem

    def start(self):
        pltpu.make_async_copy(self.src, self.stage, self.sem).start()

    def relay(self):
        pltpu.make_async_copy(self.src, self.stage, self.sem).wait()
        pltpu.make_async_copy(self.stage, self.dst, self.sem).start()

    def finish(self):
        pltpu.make_async_copy(self.stage, self.dst, self.sem).wait()


def _run_exchange(ex, name):
    n_in, n_out = len(ex.operands), len(ex.out_shape)

    def body(*refs):
        ins, outs, sems = refs[:n_in], refs[n_in:n_in + n_out], refs[n_in + n_out:]
        ex.start(ins, outs, sems)
        ex.wait(ins, outs, sems)

    return pl.pallas_call(body, in_specs=[ANY] * n_in, out_specs=[ANY] * n_out, out_shape=ex.out_shape,
                          scratch_shapes=ex.sems, name=name)(*ex.operands)


def _host_call(body, grid, in_specs, out_specs, out_shape, scratch, operands, name, ex=None):
    sem = ("arbitrary",) * len(grid)
    if ex is None:
        outs = pl.pallas_call(body, grid=grid, in_specs=in_specs, out_specs=out_specs, out_shape=out_shape,
                              scratch_shapes=scratch, compiler_params=_cparams(sem), name=name)(*operands)
        return outs, None
    n_in, n_out, n_scr = len(in_specs), len(out_specs), len(scratch)
    xi, xo = len(ex.operands), len(ex.out_shape)

    def hosted(*refs):
        ins, xins = refs[:n_in], refs[n_in:n_in + xi]
        o0 = n_in + xi
        outs, xouts = refs[o0:o0 + n_out], refs[o0 + n_out:o0 + n_out + xo]
        s0 = o0 + n_out + xo
        scr, xsems = refs[s0:s0 + n_scr], refs[s0 + n_scr:]
        first = pl.program_id(0) == 0
        last = pl.program_id(0) == pl.num_programs(0) - 1
        for d in range(1, len(grid)):
            first = first & (pl.program_id(d) == 0)
            last = last & (pl.program_id(d) == pl.num_programs(d) - 1)

        @pl.when(first)
        def _():
            ex.start(xins, xouts, xsems)

        body(*ins, *outs, *scr)

        @pl.when(last)
        def _():
            ex.wait(xins, xouts, xsems)

    res = pl.pallas_call(
        hosted, grid=grid, in_specs=list(in_specs) + [ANY] * xi, out_specs=list(out_specs) + [ANY] * xo,
        out_shape=list(out_shape) + list(ex.out_shape), scratch_shapes=list(scratch) + list(ex.sems),
        compiler_params=_cparams(sem), name=name)(*operands, *ex.operands)
    return res[:n_out], res[n_out:]


def _all_gather_chips(shards):
    n = len(shards)

    def copies(srcs, outs, sems):
        send_sems, recv_sems, local_sems, *stage = sems
        x, y, c, chips = _place()
        me = 2 * x + y
        halves = [_halves(c, s.shape[1], 16) for s in shards]
        local = [_StagedCopy(srcs[i], outs[i].at[:, me], stage[i], local_sems.at[i]) for i in range(n)]

        def ici(i, j, chip, k):
            mine = halves[i][0]
            return pltpu.make_async_remote_copy(
                src_ref=srcs[i].at[:, mine, :], dst_ref=outs[i].at[:, k, mine, :], send_sem=send_sems.at[6 * i + j],
                recv_sem=recv_sems.at[6 * i + j], device_id=(*chip, c), device_id_type=MESH)

        def d2d(i, j, k, half):
            return pltpu.make_async_remote_copy(
                src_ref=outs[i].at[:, k, half, :], dst_ref=outs[i].at[:, k, half, :], send_sem=send_sems.at[6 * i + 3 + j],
                recv_sem=recv_sems.at[6 * i + 3 + j], device_id=(x, y, 1 - c), device_id_type=MESH)

        return me, chips, halves, local, ici, d2d

    def start(srcs, outs, sems):
        me, chips, _, local, ici, _ = copies(srcs, outs, sems)
        for cp in local:
            cp.start()
        for i in range(n):
            for j, chip in enumerate(chips):
                ici(i, j, chip, me).start()

    def wait(srcs, outs, sems):
        me, chips, halves, local, ici, d2d = copies(srcs, outs, sems)
        for cp in local:
            cp.relay()
        passed = []
        for i in range(n):
            for j, (px, py) in enumerate(chips):
                k = 2 * px + py
                ici(i, j, (px, py), k).wait_recv()
                fw = d2d(i, j, k, halves[i][0])
                fw.start()
                passed.append(fw)
        for i in range(n):
            for j, (px, py) in enumerate(chips):
                d2d(i, j, 2 * px + py, halves[i][1]).wait_recv()
        for i in range(n):
            for j, chip in enumerate(chips):
                ici(i, j, chip, me).wait_send()
        for cp in passed:
            cp.wait_send()
        for cp in local:
            cp.finish()

    return _Exchange(
        list(shards), [jax.ShapeDtypeStruct((s.shape[0], 4) + s.shape[1:], s.dtype) for s in shards],
        [pltpu.SemaphoreType.DMA((6 * n,)), pltpu.SemaphoreType.DMA((6 * n,)), pltpu.SemaphoreType.DMA((n,))]
        + [pltpu.VMEM(s.shape, s.dtype) for s in shards], start, wait)


def _rs_swap_halves(gs):
    n = len(gs)

    def copies(srcs, gots, sems):
        send_sems, recv_sems = sems
        x, y, c, _ = _place()
        return [pltpu.make_async_remote_copy(
            src_ref=srcs[i].at[:, :, _halves(c, gs[i].shape[2], 8)[1], :], dst_ref=gots[i], send_sem=send_sems.at[i],
            recv_sem=recv_sems.at[i], device_id=(x, y, 1 - c), device_id_type=MESH) for i in range(n)]

    def start(srcs, gots, sems):
        for cp in copies(srcs, gots, sems):
            cp.start()

    def wait(srcs, gots, sems):
        for cp in copies(srcs, gots, sems):
            cp.wait()

    return _Exchange(
        list(gs), [jax.ShapeDtypeStruct(g.shape[:2] + (g.shape[2] // 2, g.shape[3]), g.dtype) for g in gs],
        [pltpu.SemaphoreType.DMA((n,)), pltpu.SemaphoreType.DMA((n,))], start, wait)


def _rs_add_pair(g, got, cidx, tb, out_dtype, name):
    nl, _, rh, cols = got.shape
    nb = rh // tb

    def body(c_ref, g_ref, o_ref, s_ref):
        s_ref[...] = (g_ref[...] + o_ref[...]).astype(out_dtype)

    blk = lambda half: pl.BlockSpec((None, None, tb, cols), (lambda l, k, i, c: (l, k, c[0] * nb + i, 0)) if half
                                    else (lambda l, k, i, c: (l, k, i, 0)))
    return pl.pallas_call(
        body,
        grid_spec=pltpu.PrefetchScalarGridSpec(num_scalar_prefetch=1, grid=(nl, 4, nb),
                                               in_specs=[blk(True), blk(False)], out_specs=blk(False)),
        out_shape=jax.ShapeDtypeStruct(got.shape, out_dtype),
        compiler_params=_cparams(("arbitrary", "arbitrary", "arbitrary")), name=name)(cidx, g, got)


def _rs_exchange_chips(pairs):
    n = len(pairs)

    def copies(srcs, gots, sems):
        send_sems, recv_sems, local_sems, *stage = sems
        x, y, c, chips = _place()
        me = 2 * x + y
        local = [_StagedCopy(srcs[i].at[:, me], gots[i].at[:, me], stage[i], local_sems.at[i]) for i in range(n)]

        def ici(i, j, chip, frm, to):
            return pltpu.make_async_remote_copy(
                src_ref=srcs[i].at[:, to], dst_ref=gots[i].at[:, frm], send_sem=send_sems.at[3 * i + j],
                recv_sem=recv_sems.at[3 * i + j], device_id=(*chip, c), device_id_type=MESH)

        sent = [ici(i, j, (px, py), me, 2 * px + py) for i in range(n) for j, (px, py) in enumerate(chips)]
        recvd = [ici(i, j, (px, py), 2 * px + py, me) for i in range(n) for j, (px, py) in enumerate(chips)]
        return local, sent, recvd

    def start(srcs, gots, sems):
        local, sent, _ = copies(srcs, gots, sems)
        for cp in local + sent:
            cp.start()

    def wait(srcs, gots, sems):
        local, sent, recvd = copies(srcs, gots, sems)
        for cp in local:
            cp.relay()
        for cp in recvd:
            cp.wait_recv()
        for cp in sent:
            cp.wait_send()
        for cp in local:
            cp.finish()

    return _Exchange(
        list(pairs), [jax.ShapeDtypeStruct(p.shape, p.dtype) for p in pairs],
        [pltpu.SemaphoreType.DMA((3 * n,)), pltpu.SemaphoreType.DMA((3 * n,)), pltpu.SemaphoreType.DMA((n,))]
        + [pltpu.VMEM((p.shape[0],) + p.shape[2:], p.dtype) for p in pairs], start, wait)


def _rs_add_chips(got, tb, name):
    nl, _, rh, cols = got.shape

    def body(g_ref, s_ref):
        s_ref[...] = ((g_ref[0].astype(F32) + g_ref[1].astype(F32)) + g_ref[2].astype(F32)) + g_ref[3].astype(F32)

    return pl.pallas_call(
        body, grid=(nl, rh // tb),
        in_specs=[pl.BlockSpec((None, 4, tb, cols), lambda l, i: (l, 0, i, 0))],
        out_specs=pl.BlockSpec((None, tb, cols), lambda l, i: (l, i, 0)),
        out_shape=jax.ShapeDtypeStruct((nl, rh, cols), F32),
        compiler_params=_cparams(("arbitrary", "arbitrary")), name=name)(got)


def _rs_join_halves(groups):
    totals = [t for grp in groups for t in grp]
    where = [(gi, li) for gi, grp in enumerate(groups) for li in range(len(grp))]
    n = len(totals)

    def copies(srcs, outs, sems):
        send_sems, recv_sems, local_sems, *stage = sems
        x, y, c, _ = _place()
        local, send, recv = [], [], []
        for i, (gi, li) in enumerate(where):
            mine, other = _halves(c, 2 * totals[i].shape[1], 8)
            local.append(_StagedCopy(srcs[i], outs[gi].at[li:li + 1, mine, :], stage[i], local_sems.at[i]))
            for lst, rows in ((send, mine), (recv, other)):
                lst.append(pltpu.make_async_remote_copy(
                    src_ref=srcs[i], dst_ref=outs[gi].at[li:li + 1, rows, :], send_sem=send_sems.at[i],
                    recv_sem=recv_sems.at[i], device_id=(x, y, 1 - c), device_id_type=MESH))
        return local, send, recv

    def start(srcs, outs, sems):
        local, send, _ = copies(srcs, outs, sems)
        for cp in local + send:
            cp.start()

    def wait(srcs, outs, sems):
        local, send, recv = copies(srcs, outs, sems)
        for cp in local:
            cp.relay()
        for cp in recv:
            cp.wait_recv()
        for cp in send:
            cp.wait_send()
        for cp in local:
            cp.finish()

    return _Exchange(
        totals, [jax.ShapeDtypeStruct((len(grp), 2 * grp[0].shape[1], grp[0].shape[2]), F32) for grp in groups],
        [pltpu.SemaphoreType.DMA((n,)), pltpu.SemaphoreType.DMA((n,)), pltpu.SemaphoreType.DMA((n,))]
        + [pltpu.VMEM(t.shape, F32) for t in totals], start, wait)


def _rows(a):
    return a.reshape(-1, LANES)


def _pad_rows(a, mult):
    r = (-a.shape[0]) % mult
    return a if r == 0 else jnp.concatenate([a, jnp.zeros((r, a.shape[1]), a.dtype)], axis=0)


def kernel(x, norm_g, w_in, b_f, cf_dw, cf_dw_b, cf_ln_g, cf_ln_b, cf_pw, sc_dw, q_norm_g, k_norm_g, w_out, loss_target, m_norm_g, m_w_in, m_b_f, m_cf_dw, m_cf_dw_b, m_cf_ln_g, m_cf_ln_b, m_cf_pw, m_sc_dw, m_q_norm_g, m_k_norm_g, m_w_out, v_norm_g, v_w_in, v_b_f, v_cf_dw, v_cf_dw_b, v_cf_ln_g, v_cf_ln_b, v_cf_pw, v_sc_dw, v_q_norm_g, v_k_norm_g, v_w_out):
    bsz, seq, _ = x.shape
    t = bsz * seq

    taps = jnp.concatenate([cf_dw.reshape(-1), sc_dw.reshape(-1)])
    taps_bf = lax.bitcast_convert_type(taps, jnp.bfloat16).reshape(-1, LANES)
    n_pw, n_taps = 2 * 64 * DC // LANES, taps_bf.shape[0]
    small_w = _pad_rows(jnp.concatenate([_rows(cf_pw.astype(BF)), taps_bf], axis=0), 32)[None]
    win_b, wout_b = w_in.astype(BF), w_out.astype(BF)
    win0_g, small_g = _run_exchange(_all_gather_chips([win_b[0:1], small_w]), "all_gather_first")
    gather_rest = _all_gather_chips([win_b[1:DEPTH], wout_b])
    full_w_in = lambda g: jnp.concatenate([g[k] for k in range(4)] + [jnp.zeros((D, NP - N_IN), BF)], axis=-1)
    w_in_full = [full_w_in(win0_g[0])] + [None] * (DEPTH - 1)
    w_out_full = None
    cf_pw_full = jnp.concatenate([small_g[0, k, 0:n_pw].reshape(DEPTH, 64, DC) for k in range(4)], axis=1)
    taps_all = [lax.bitcast_convert_type(small_g[0, k, n_pw:n_pw + n_taps].reshape(-1, 2), F32) for k in range(4)]
    n_cfdw = DEPTH * CFW * 64
    cf_dw_full = jnp.concatenate([tk[:n_cfdw].reshape(DEPTH, CFW, 64) for tk in taps_all], axis=-1)
    sc_dw_full = jnp.concatenate([tk[n_cfdw:].reshape(DEPTH, SCW, 64) for tk in taps_all], axis=-1)

    bf_pad = jnp.pad(b_f, ((0, 0), (0, LANES - NH)))

    xs = [x.reshape(t, D)]
    saved = []
    dy = loss_part = None
    for l in range(DEPTH):
        xl = xs[-1]
        gq, gk = q_norm_g[l].reshape(1, DA), k_norm_g[l].reshape(1, DA)
        pcf, psc, pq, pk, vb, pz, pf = _fwd_inproj(xl, norm_g[l][None], w_in_full[l], f"fwd_inproj_{l}")
        b3 = lambda a: a.reshape(bsz, seq, a.shape[-1])
        ycs = _fwd_conv(b3(pcf), b3(psc), cf_dw_full[l], cf_dw_b[l][None], cf_ln_g[l][None], cf_ln_b[l][None],
                        cf_pw_full[l], sc_dw_full[l], f"fwd_conv_{l}")
        qs, kn, cq, crow8 = _fwd_attn_prep(b3(pq), b3(pk), b3(pf), gq, gk, bf_pad[l][None], f"fwd_attn_prep_{l}")
        crow = crow8.reshape(bsz, NH // 2, 2, seq)
        (o, ohp, yatt, lse), landed = _fwd_attn(qs, kn, b3(vb), cq, crow, b3(pz), f"fwd_attn_{l}",
                                                gather_rest if l == 0 else None)
        if l == 0:
            win_rest_g, wout_g = landed
            w_in_full[1:] = [full_w_in(win_rest_g[i]) for i in range(DEPTH - 1)]
            w_out_full = wout_g.reshape(DEPTH, D, D)
        ycs2, yatt2 = ycs.reshape(t, 512), yatt.reshape(t, DA)
        if l + 1 < DEPTH:
            xs.append(_fwd_outproj(xl, ycs2, yatt2, w_out_full[l], f"fwd_outproj_{l}"))
        else:
            dy, loss_part = _fwd_outproj_loss(xl, ycs2, yatt2, w_out_full[l], loss_target.reshape(t, D),
                                              f"fwd_outproj_loss_{l}")
        saved.append((pcf, psc, pq, pk, vb, pz, pf, ycs2, yatt2, o, ohp, qs, kn, cq, crow, lse))

    grads = [None] * DEPTH
    reduced = [None] * DEPTH
    cidx = lax.axis_index("c").astype(jnp.int32).reshape(1)
    big_tiles = [256, 128]

    def big_packs(gl):
        return [jnp.stack([gl["w_in"][:, NSH * k:NSH * (k + 1)] for k in range(4)])[None],
                gl["w_out"].reshape(1, 4, 256, D)]

    def add_pairs(packs, gots, tiles, dtypes, tag):
        return [_rs_add_pair(p, got, cidx, tb, dt, f"rs_add_pair_{tag}_{i}")
                for i, (p, got, tb, dt) in enumerate(zip(packs, gots, tiles, dtypes))]

    def add_chips(gots2, tiles, tag):
        return [_rs_add_chips(got, tb, f"rs_add_chips_{tag}_{i}") for i, (got, tb) in enumerate(zip(gots2, tiles))]

    for l in reversed(range(DEPTH)):
        pcf, psc, pq, pk, vb, pz, pf, ycs2, yatt2, o, ohp, qs, kn, cq, crow, lse = saved[l]
        gq, gk = q_norm_g[l].reshape(1, DA), k_norm_g[l].reshape(1, DA)
        b3 = lambda a: a.reshape(bsz, seq, a.shape[-1])
        f2 = lambda a: a.reshape(t, a.shape[-1])
        packs = big_packs(grads[l + 1]) if l + 1 < DEPTH else None
        (dycs, do, dz, dl, d_wout), gots = _bwd_outproj(
            dy, ycs2, yatt2, f2(o), f2(ohp), pz, w_out_full[l], f"bwd_outproj_{l}",
            _rs_swap_halves(packs) if packs else None)
        pairs = add_pairs(packs, gots, big_tiles, [BF, BF], l + 1) if packs else None
        (dqs, dkn, dv, dcrow), gots2 = _bwd_attn(qs, kn, b3(vb), b3(do), cq, lse, b3(dl), crow, f"bwd_attn_{l}",
                                                 _rs_exchange_chips(pairs) if packs else None)
        if packs:
            reduced[l + 1] = add_chips(gots2, big_tiles, l + 1)
        (dpq, dpk, dpf, d_gq, d_gk, d_bf), _ = _bwd_attn_post(
            b3(pq), b3(pk), dqs, dkn, b3(pf), dcrow.reshape(bsz, NH, seq), gq, gk, bf_pad[l][None],
            f"bwd_attn_post_{l}")
        dcf, dsc, d_cfdw, d_cfb, d_lng, d_lnb, d_pw, d_scdw = _bwd_conv(
            b3(pcf), b3(psc), b3(dycs), cf_dw_full[l], cf_dw_b[l][None], cf_ln_g[l][None], cf_ln_b[l][None],
            cf_pw_full[l], sc_dw_full[l], f"bwd_conv_{l}")
        dy, d_ng, d_win = _bwd_inproj(xs[l], norm_g[l][None], dy, w_in_full[l], f2(dcf), f2(dsc), f2(dpq), f2(dpk),
                                      f2(dv), dz, f2(dpf), f"bwd_inproj_{l}")
        grads[l] = dict(norm_g=d_ng[0], w_in=d_win, b_f=d_bf[0, :NH], cf_dw=d_cfdw, cf_dw_b=d_cfb[0],
                        cf_ln_g=d_lng[0], cf_ln_b=d_lnb[0], cf_pw=d_pw, sc_dw=d_scdw,
                        q_norm_g=d_gq.reshape(NH, HD), k_norm_g=d_gk.reshape(NH, HD), w_out=d_wout)
    grad_x = dy.reshape(bsz, seq, D)
    gw = {n: jnp.stack([grads[l][n] for l in range(DEPTH)]) for n in grads[0] if n not in ("w_in", "w_out")}

    rep_names = ("norm_g", "b_f", "cf_dw_b", "cf_ln_g", "cf_ln_b", "q_norm_g", "k_norm_g")
    rep = jnp.concatenate([jnp.pad(gw[n].reshape(-1), (0, (-gw[n].size) % LANES)) for n in rep_names]
                          + [jnp.pad(loss_part.reshape(-1), (0, LANES - 1))]).reshape(-1, LANES)
    blocks = []
    for k in range(4):
        small = jnp.concatenate([gw["cf_dw"][:, :, 64 * k:64 * (k + 1)].reshape(-1),
                                 gw["sc_dw"][:, :, 64 * k:64 * (k + 1)].reshape(-1)]).reshape(-1, LANES)
        blocks.append(_pad_rows(jnp.concatenate([_rows(gw["cf_pw"][:, 64 * k:64 * (k + 1), :]), small, rep], axis=0), 16))
    g_small = jnp.stack(blocks)[None]
    packs = big_packs(grads[0]) + [g_small]
    tiles = big_tiles + [g_small.shape[2] // 2]
    gots = _run_exchange(_rs_swap_halves(packs), "rs_swap_halves_last")
    pairs = add_pairs(packs, gots, tiles, [BF, BF, F32], 0)
    gots2 = _run_exchange(_rs_exchange_chips(pairs), "rs_exchange_chips_last")
    tot_win0, tot_wout0, tot_small = add_chips(gots2, tiles, 0)
    reduced[0] = (tot_win0, tot_wout0)
    red_win, red_wout, red_small = _run_exchange(
        _rs_join_halves([[reduced[l][0] for l in range(DEPTH)], [reduced[l][1] for l in range(DEPTH)], [tot_small]]),
        "rs_join_halves")

    n_small = (DEPTH * CFW * 64 + DEPTH * SCW * 64) // LANES
    red = red_small[0]
    pos = [0]

    def take(nrows):
        pos[0] += nrows
        return red[pos[0] - nrows:pos[0]]

    g = {}
    g["w_in"] = red_win
    g["w_out"] = red_wout
    g["cf_pw"] = take(n_pw).reshape(DEPTH, 64, DC)
    small = take(n_small).reshape(-1)
    g["cf_dw"] = small[:n_cfdw].reshape(DEPTH, CFW, 64)
    g["sc_dw"] = small[n_cfdw:].reshape(DEPTH, SCW, 64)
    shapes = dict(norm_g=norm_g.shape, b_f=b_f.shape, cf_dw_b=cf_dw_b.shape, cf_ln_g=cf_ln_g.shape,
                  cf_ln_b=cf_ln_b.shape, q_norm_g=q_norm_g.shape, k_norm_g=k_norm_g.shape)
    for n in rep_names:
        size = 1
        for s in shapes[n]:
            size *= s
        g[n] = take(-(-size // LANES)).reshape(-1)[:size].reshape(shapes[n])
    loss = take(1)[0, 0]

    order = ("norm_g", "w_in", "b_f", "cf_dw", "cf_dw_b", "cf_ln_g", "cf_ln_b", "cf_pw", "sc_dw", "q_norm_g",
             "k_norm_g", "w_out")
    weights = dict(norm_g=norm_g, w_in=w_in, b_f=b_f, cf_dw=cf_dw, cf_dw_b=cf_dw_b, cf_ln_g=cf_ln_g, cf_ln_b=cf_ln_b,
                   cf_pw=cf_pw, sc_dw=sc_dw, q_norm_g=q_norm_g, k_norm_g=k_norm_g, w_out=w_out)
    ms = dict(norm_g=m_norm_g, w_in=m_w_in, b_f=m_b_f, cf_dw=m_cf_dw, cf_dw_b=m_cf_dw_b, cf_ln_g=m_cf_ln_g,
              cf_ln_b=m_cf_ln_b, cf_pw=m_cf_pw, sc_dw=m_sc_dw, q_norm_g=m_q_norm_g, k_norm_g=m_k_norm_g, w_out=m_w_out)
    vs = dict(norm_g=v_norm_g, w_in=v_w_in, b_f=v_b_f, cf_dw=v_cf_dw, cf_dw_b=v_cf_dw_b, cf_ln_g=v_cf_ln_g,
              cf_ln_b=v_cf_ln_b, cf_pw=v_cf_pw, sc_dw=v_sc_dw, q_norm_g=v_q_norm_g, k_norm_g=v_k_norm_g, w_out=v_w_out)
    delta, new_m, new_v = {}, {}, {}
    two_d = lambda a: a.reshape(-1, a.shape[-1])
    for n in ("w_in", "w_out"):
        outs = _adamw(two_d(weights[n]), two_d(g[n]), two_d(ms[n]), two_d(vs[n]), 256, f"adamw_{n}")
        delta[n], new_m[n], new_v[n], g[n] = (a.reshape(weights[n].shape) for a in outs)
    small_names = [n for n in order if n not in ("w_in", "w_out")]
    outs = _adamw_many([weights[n] for n in small_names], [g[n] for n in small_names],
                       [ms[n] for n in small_names], [vs[n] for n in small_names], "adamw_small")
    for i, n in enumerate(small_names):
        delta[n], new_m[n], new_v[n] = outs[3 * i:3 * i + 3]

    return (loss, grad_x, *[g[n] for n in order], *[delta[n] for n in order], *[new_m[n] for n in order],
            *[new_v[n] for n in order])
```

```python
import jax
import jax.numpy as jnp
from jax import lax
from jax.experimental import pallas as pl
from jax.experimental.pallas import tpu as pltpu

F32 = jnp.float32
BF = jnp.bfloat16
MESH = pl.DeviceIdType.MESH

DEPTH = 2
D = 1024
DC = 256
DA = 512
NH = 8
HD = 64
CFW = 31
SCW = 3
N_IN = 3848
NP = 3968
NSH = N_IN // 4
HALO = 32
ATT_ROWS = 32
BWD_KEYS = 512
FWD_ROWS = 256
SUBLANES = 8
CONV_ROWS = 32
TAP_GROUP = 8
EPS = 1e-6
LANES = 128
VMEM_LIMIT = 56 * 1024 * 1024

C_CF, C_SC, C_Q, C_K, C_V, C_Z, C_F = 0, 768, 1792, 2304, 2816, 3328, 3840

ADAM_LR = 0.001
ADAM_B1 = 0.9
ADAM_B2 = 0.999
ADAM_EPS = 1e-08
ADAM_WD = 0.01
ADAM_STEP = 10


def _cparams(sem=None):
    return pltpu.CompilerParams(dimension_semantics=sem, vmem_limit_bytes=VMEM_LIMIT)


def _dot(a, b):
    return jnp.dot(a, b, preferred_element_type=F32)


def _dot_nt(a, b):
    return lax.dot_general(a, b, (((1,), (1,)), ((), ())), preferred_element_type=F32)


def _dot_tn(a, b):
    return lax.dot_general(a, b, (((0,), (0,)), ((), ())), preferred_element_type=F32)


def _split3(x):
    hi = x.astype(BF)
    r1 = x - hi.astype(F32)
    mid = r1.astype(BF)
    lo = (r1 - mid.astype(F32)).astype(BF)
    return hi, mid, lo


def _dot_exact(a_bf, x):
    hi, mid, lo = _split3(x)
    return _dot(a_bf, hi) + _dot(a_bf, mid) + _dot(a_bf, lo)


def _sigmoid(x):
    return 1.0 / (1.0 + jnp.exp(-x))


def _seg_sum64(x, exact=False):
    i = lax.broadcasted_iota(jnp.int32, (LANES, LANES), 0)
    j = lax.broadcasted_iota(jnp.int32, (LANES, LANES), 1)
    g = ((i >= HD) == (j >= HD)).astype(BF)
    hi, mid, lo = _split3(x)
    two = _dot(hi, g) + _dot(mid, g)
    return two + _dot(lo, g) if exact else two


def _fwd_inproj(x, g, w, name):
    t = x.shape[0]
    tm = 256

    def body(x_ref, g_ref, w_ref, cf_ref, sc_ref, q_ref, k_ref, v_ref, z_ref, f_ref):
        xt = x_ref[...]
        r = lax.rsqrt(jnp.mean(xt * xt, axis=-1, keepdims=True) + EPS)
        h = ((xt * r) * g_ref[...]).astype(BF)
        cf_ref[:, 0:512] = _dot(h, w_ref[:, 0:512])
        cf_ref[:, 512:768] = _dot(h, w_ref[:, 512:768])
        sc_ref[:, 0:512] = _dot(h, w_ref[:, C_SC:C_SC + 512])
        sc_ref[:, 512:1024] = _dot(h, w_ref[:, C_SC + 512:C_Q])
        q_ref[...] = _dot(h, w_ref[:, C_Q:C_K])
        k_ref[...] = _dot(h, w_ref[:, C_K:C_V])
        v_ref[...] = _dot(h, w_ref[:, C_V:C_Z]).astype(BF)
        z_ref[...] = _dot(h, w_ref[:, C_Z:C_F])
        f_ref[...] = _dot(h, w_ref[:, C_F:NP])

    row = lambda n: pl.BlockSpec((tm, n), lambda i: (i, 0))
    return pl.pallas_call(
        body, grid=(t // tm,),
        in_specs=[row(D), pl.BlockSpec((1, D), lambda i: (0, 0)), pl.BlockSpec((D, NP), lambda i: (0, 0))],
        out_specs=[row(768), row(1024), row(DA), row(DA), row(DA), row(DA), row(LANES)],
        out_shape=[jax.ShapeDtypeStruct((t, 768), F32), jax.ShapeDtypeStruct((t, 1024), F32),
                   jax.ShapeDtypeStruct((t, DA), F32), jax.ShapeDtypeStruct((t, DA), F32),
                   jax.ShapeDtypeStruct((t, DA), BF), jax.ShapeDtypeStruct((t, DA), F32),
                   jax.ShapeDtypeStruct((t, LANES), F32)],
        compiler_params=_cparams(("arbitrary",)), name=name)(x, g, w)


def _shift_copies(buf, sbuf, shifts):
    n = buf.shape[0]
    for sh in shifts:
        rows = n if sh == 0 else n - SUBLANES
        sbuf[sh, 0:rows, :] = buf[pl.ds(sh, rows), :]


def _tap_rows(sbuf, off, r0):
    sh = off % SUBLANES
    return sbuf[sh, pl.ds(pl.multiple_of(r0 + (off - sh), SUBLANES), CONV_ROWS), :]


def _tap_conv(sbuf, w_ref, offsets, dst, ts, bias):
    def chunk(c, _):
        r0 = pl.multiple_of(c * CONV_ROWS, CONV_ROWS)
        acc = jnp.zeros((CONV_ROWS, DC), F32) + bias
        for k, off in enumerate(offsets):
            acc = acc + w_ref[k:k + 1, :] * _tap_rows(sbuf, off, r0)
        dst[pl.ds(r0, CONV_ROWS), :] = acc
        return 0

    lax.fori_loop(0, ts // CONV_ROWS, chunk, 0)


def _tap_sums(a_buf, sbuf, offsets, ts):
    outs = []
    for g0 in range(0, len(offsets), TAP_GROUP):
        group = offsets[g0:g0 + TAP_GROUP]

        def chunk(c, accs, group=group):
            r0 = pl.multiple_of(c * CONV_ROWS, CONV_ROWS)
            a = a_buf[pl.ds(r0, CONV_ROWS), :]
            return tuple(acc + (a * _tap_rows(sbuf, off, r0)).reshape(CONV_ROWS // SUBLANES, SUBLANES, DC).sum(axis=0)
                         for acc, off in zip(accs, group))

        accs = lax.fori_loop(0, ts // CONV_ROWS, chunk, tuple(jnp.zeros((SUBLANES, DC), F32) for _ in group))
        outs += [jnp.sum(acc, axis=0, keepdims=True) for acc in accs]
    return outs


CF_TAPS = tuple(HALO - (CFW - 1) + k for k in range(CFW))
SC_TAPS = tuple(HALO - (SCW - 1) + k for k in range(SCW))
CF_TAPS_T = tuple(CFW - 1 - k for k in range(CFW))
SC_TAPS_T = tuple(SCW - 1 - k for k in range(SCW))


def _conformer_fwd(a, g, ha, hg, first, ubuf, usbuf, u1buf, dw_ref, bias, lng, lnb):
    ts = a.shape[0]
    u0 = a * _sigmoid(g)
    ubuf[0:HALO, :] = jnp.where(first, 0.0, ha * _sigmoid(hg))
    ubuf[HALO:HALO + ts, :] = u0
    _shift_copies(ubuf, usbuf, range(SUBLANES))
    _tap_conv(usbuf, dw_ref, CF_TAPS, u1buf, ts, bias)
    u1 = u1buf[...]
    mu = jnp.mean(u1, axis=-1, keepdims=True)
    xc = u1 - mu
    rstd = lax.rsqrt(jnp.mean(xc * xc, axis=-1, keepdims=True) + EPS)
    n = xc * rstd
    u2 = n * lng + lnb
    s2 = _sigmoid(u2)
    u3 = u2 * s2
    return u0, n, rstd, u2, s2, u3


def _shortconv_fwd(c, xs, hc, hx, first, mbuf, msbuf, cvbuf, dw_ref):
    ts = c.shape[0]
    mbuf[0:HALO, :] = jnp.where(first, 0.0, hc * hx)
    mbuf[HALO:HALO + ts, :] = c * xs
    _shift_copies(mbuf, msbuf, sorted({off % SUBLANES for off in SC_TAPS}))
    _tap_conv(msbuf, dw_ref, SC_TAPS, cvbuf, ts, 0.0)
    return cvbuf[...]


def _conv_specs(bsz, seq, ts, order):
    nt = seq // ts
    tile = (lambda i: i) if order > 0 else (lambda i: nt - 1 - i)
    hrow = lambda i: jnp.maximum(tile(i) * (ts // HALO) - 1, 0)
    cur = lambda n: pl.BlockSpec((None, ts, n), lambda b, i: (b, tile(i), 0))
    halo = lambda j: pl.BlockSpec((None, HALO, DC), lambda b, i: (b, hrow(i), j))
    full = lambda r, c: pl.BlockSpec((r, c), lambda b, i: (0, 0))
    return nt, tile, cur, halo, full


def _fwd_conv(pcf, psc, cf_dw, cf_dw_b, ln_g, ln_b, cf_pw, sc_dw, name):
    bsz, seq, _ = pcf.shape
    ts = 512
    nt, tile, cur, halo, full = _conv_specs(bsz, seq, ts, +1)

    def body(cf_ref, ha_ref, hg_ref, sc_ref, hc_ref, hx_ref, dw_ref, b_ref, lg_ref, lb_ref, pw_ref, sdw_ref,
             y_ref, ubuf, sbuf, obuf):
        first = pl.program_id(1) == 0
        _, _, _, _, _, u3 = _conformer_fwd(cf_ref[:, 0:256], cf_ref[:, 256:512], ha_ref[...], hg_ref[...], first,
                                           ubuf, sbuf, obuf, dw_ref, b_ref[...], lg_ref[...], lb_ref[...])
        z = cf_ref[:, 512:768]
        y_ref[:, 0:256] = _dot(u3.astype(BF), pw_ref[...]) * (z * _sigmoid(z))
        cv = _shortconv_fwd(sc_ref[:, 256:512], sc_ref[:, 512:768], hc_ref[...], hx_ref[...], first,
                            ubuf, sbuf, obuf, sdw_ref)
        zs = sc_ref[:, 768:1024]
        y_ref[:, 256:512] = sc_ref[:, 0:256] * cv * (zs * _sigmoid(zs))

    return pl.pallas_call(
        body, grid=(bsz, nt),
        in_specs=[cur(768), halo(0), halo(1), cur(1024), halo(1), halo(2),
                  full(CFW, DC), full(1, DC), full(1, DC), full(1, DC), full(DC, DC), full(SCW, DC)],
        out_specs=pl.BlockSpec((None, ts, 512), lambda b, i: (b, i, 0)),
        out_shape=jax.ShapeDtypeStruct((bsz, seq, 512), F32),
        scratch_shapes=[pltpu.VMEM((HALO + ts, DC), F32), pltpu.VMEM((SUBLANES, HALO + ts, DC), F32),
                        pltpu.VMEM((ts, DC), F32)],
        compiler_params=_cparams(("arbitrary", "arbitrary")), name=name,
    )(pcf, pcf, pcf, psc, psc, psc, cf_dw, cf_dw_b, ln_g, ln_b, cf_pw, sc_dw)


def _head_rms(xb, gb):
    ms = _seg_sum64(xb * xb) * (1.0 / HD)
    r = lax.rsqrt(ms + EPS)
    xhat = xb * r
    return xhat, r, xhat * gb


def _fwd_attn_prep(pq, pk, pf, gq, gk, bf, name):
    bsz, seq, _ = pq.shape
    ts = 512
    nt = seq // ts

    def body(q_ref, k_ref, f_ref, gq_ref, gk_ref, bf_ref, qs_ref, kn_ref, cq_ref, crow_ref, carry):
        @pl.when(pl.program_id(1) == 0)
        def _():
            carry[...] = jnp.zeros_like(carry)

        for jb in range(DA // LANES):
            sl = slice(jb * LANES, (jb + 1) * LANES)
            _, _, qn = _head_rms(q_ref[:, sl], gq_ref[:, sl])
            qs_ref[:, sl] = (qn * (1.0 / 8.0)).astype(BF)
            _, _, kn = _head_rms(k_ref[:, sl], gk_ref[:, sl])
            kn_ref[:, sl] = kn.astype(BF)

        xf = f_ref[...] + bf_ref[...]
        lf = jnp.minimum(xf, 0.0) - jnp.log(1.0 + jnp.exp(-jnp.abs(xf)))
        ti = lax.broadcasted_iota(jnp.int32, (ts, ts), 0)
        si = lax.broadcasted_iota(jnp.int32, (ts, ts), 1)
        c = _dot_exact((si <= ti).astype(BF), lf) + carry[...]
        carry[...] = c[ts - 1:ts, :]
        hj = lax.broadcasted_iota(jnp.int32, (LANES, DA), 0)
        ll = lax.broadcasted_iota(jnp.int32, (LANES, DA), 1)
        dd = ll - hj * HD
        chi, cmid, clo = _split3(c)
        e = ((dd >= 0) & (dd < HD)).astype(BF)
        cq_ref[...] = _dot(chi, e) + _dot(cmid, e) + _dot(clo, e)
        hh = lax.broadcasted_iota(jnp.int32, (16, LANES), 0)
        jj = lax.broadcasted_iota(jnp.int32, (16, LANES), 1)
        sel = (hh == jj).astype(BF)
        cr = _dot_nt(sel, chi) + _dot_nt(sel, cmid) + _dot_nt(sel, clo)
        crow_ref[...] = cr[0:NH, :]

    tile = lambda n: pl.BlockSpec((None, ts, n), lambda b, i: (b, i, 0))
    vec = lambda n: pl.BlockSpec((1, n), lambda b, i: (0, 0))
    return pl.pallas_call(
        body, grid=(bsz, nt),
        in_specs=[tile(DA), tile(DA), tile(LANES), vec(DA), vec(DA), vec(LANES)],
        out_specs=[tile(DA), tile(DA), tile(DA), pl.BlockSpec((None, NH, ts), lambda b, i: (b, 0, i))],
        out_shape=[jax.ShapeDtypeStruct((bsz, seq, DA), BF), jax.ShapeDtypeStruct((bsz, seq, DA), BF),
                   jax.ShapeDtypeStruct((bsz, seq, DA), F32), jax.ShapeDtypeStruct((bsz, NH, seq), F32)],
        scratch_shapes=[pltpu.VMEM((1, LANES), F32)],
        compiler_params=_cparams(("arbitrary", "arbitrary")), name=name)(pq, pk, pf, gq, gk, bf)


def _tile_rows(i, rows):
    return pl.ds(i * rows, rows) if isinstance(i, int) else pl.ds(pl.multiple_of(i * rows, rows), rows)


def _causal_mask(tq, tk):
    r = lax.broadcasted_iota(jnp.int32, (tq, tk), 0)
    c = lax.broadcasted_iota(jnp.int32, (tq, tk), 1)
    return r - c


def _fwd_attn(qs, kn, vb, cq, crow, pz, name, ex=None):
    bsz, seq, _ = qs.shape
    tq, tk = 256, 256
    nq, nk = seq // tq, seq // tk
    npair = NH // 2

    def body(q_ref, k_ref, v_ref, cq_ref, cr_ref, z_ref, o_ref, ohp_ref, y_ref, lse_ref,
             s_buf, p_buf, acc_ref, accl_ref):
        qi = pl.program_id(2)
        head0 = lax.broadcasted_iota(jnp.int32, (1, LANES), 1) < HD
        q = q_ref[...]
        zq = jnp.zeros_like(q)
        qm = (jnp.where(head0, q, zq), jnp.where(head0, zq, q))
        ct = (cq_ref[:, 0:1], cq_ref[:, HD:HD + 1])

        tile = lambda j: _tile_rows(j, tk)

        def scores(j, slot):
            kb = k_ref[tile(j), :]
            for hh in range(2):
                s_buf[slot, hh] = _dot_nt(qm[hh], kb) + (ct[hh] - cr_ref[hh:hh + 1, tile(j)])

        def weighted_values(slot, j, al):
            vb_t = v_ref[tile(j), :]
            acc_ref[...] = al * acc_ref[...] + jnp.where(head0, _dot(p_buf[slot, 0, 0], vb_t),
                                                         _dot(p_buf[slot, 1, 0], vb_t))
            accl_ref[...] = al * accl_ref[...] + jnp.where(head0, _dot(p_buf[slot, 0, 1], vb_t),
                                                           _dot(p_buf[slot, 1, 1], vb_t))

        def softmax(slot, first_visible, m, l):
            m_new, l_new, al_new = ([], []), ([], []), []
            for r in range(tq // FWD_ROWS):
                rows = slice(r * FWD_ROWS, (r + 1) * FWD_ROWS)
                visible = _causal_mask(FWD_ROWS, tk) >= first_visible - r * FWD_ROWS
                alphas = []
                for hh in range(2):
                    s = jnp.where(visible, s_buf[slot, hh, rows, :], -1e30)
                    m_old = m[hh][rows]
                    m2 = jnp.maximum(m_old, jnp.max(s, axis=-1, keepdims=True))
                    p = jnp.exp(s - m2)
                    alpha = jnp.exp(m_old - m2)
                    l_new[hh].append(alpha * l[hh][rows] + jnp.sum(p, axis=-1, keepdims=True))
                    m_new[hh].append(m2)
                    pb = p.astype(BF)
                    p_buf[slot, hh, 0, rows, :] = pb
                    p_buf[slot, hh, 1, rows, :] = (p - pb.astype(F32)).astype(BF)
                    alphas.append(alpha)
                al_new.append(jnp.where(head0, alphas[0], alphas[1]))
            cat = lambda parts: jnp.concatenate(parts, axis=0)
            return (cat(m_new[0]), cat(m_new[1])), (cat(l_new[0]), cat(l_new[1])), cat(al_new)

        clamp = lambda j: jnp.clip(j, 0, nk - 1)

        def step(j, slot, carry):
            m, l, al = carry
            weighted_values(1 - slot, clamp(j - 1), al)
            m, l, al = softmax(slot, j * tk - qi * tq, m, l)
            scores(clamp(j + 1), 1 - slot)
            return m, l, al

        scores(0, 0)
        p_buf[1] = jnp.zeros((2, 2, tq, tk), BF)
        acc_ref[...] = jnp.zeros((tq, LANES), F32)
        accl_ref[...] = jnp.zeros((tq, LANES), F32)
        neg = jnp.full((tq, 1), -1e30, F32)
        zcol = jnp.zeros((tq, 1), F32)
        trips = ((qi + 1) * (tq // tk) + 1) // 2
        m, l, al = lax.fori_loop(0, trips, lambda t, cy: step(2 * t + 1, 1, step(2 * t, 0, cy)),
                                 ((neg, neg), (zcol, zcol), jnp.ones((tq, LANES), F32)))
        weighted_values(1, clamp(2 * trips - 1), al)
        inv = jnp.where(head0, 1.0 / l[0], 1.0 / l[1])
        o = acc_ref[...] * inv
        z = z_ref[...]
        o_ref[...] = o
        ohp_ref[...] = (acc_ref[...] + accl_ref[...]) * inv
        y_ref[...] = o * (z * _sigmoid(z))
        lse_ref[...] = jnp.where(head0, m[0] + jnp.log(l[0]), m[1] + jnp.log(l[1]))

    qblk = pl.BlockSpec((None, tq, LANES), lambda b, h, i: (b, i, h))
    kvblk = pl.BlockSpec((None, seq, LANES), lambda b, h, i: (b, 0, h))
    return _host_call(
        body, (bsz, npair, nq),
        [qblk, kvblk, kvblk, qblk, pl.BlockSpec((None, None, 2, seq), lambda b, h, i: (b, h, 0, 0)), qblk],
        [qblk, qblk, qblk, qblk], [jax.ShapeDtypeStruct((bsz, seq, DA), F32)] * 4,
        [pltpu.VMEM((2, 2, tq, tk), F32), pltpu.VMEM((2, 2, 2, tq, tk), BF), pltpu.VMEM((tq, LANES), F32),
         pltpu.VMEM((tq, LANES), F32)],
        (qs, kn, vb, cq, crow, pz), name, ex)


def _fwd_outproj(x, ycs, yatt, wo, name):
    t = x.shape[0]
    tm = 512

    def body(x_ref, a_ref, b_ref, w_ref, o_ref):
        o_ref[...] = (x_ref[...] + _dot(a_ref[...].astype(BF), w_ref[0:512, :])
                      + _dot(b_ref[...].astype(BF), w_ref[512:1024, :]))

    row = lambda n: pl.BlockSpec((tm, n), lambda i: (i, 0))
    return pl.pallas_call(
        body, grid=(t // tm,),
        in_specs=[row(D), row(512), row(512), pl.BlockSpec((D, D), lambda i: (0, 0))],
        out_specs=row(D), out_shape=jax.ShapeDtypeStruct((t, D), F32),
        compiler_params=_cparams(("arbitrary",)), name=name)(x, ycs, yatt, wo)


def _fwd_outproj_loss(x, ycs, yatt, wo, target, name):
    t = x.shape[0]
    tm = 512

    def body(x_ref, a_ref, b_ref, w_ref, t_ref, dy_ref, loss_ref):
        @pl.when(pl.program_id(0) == 0)
        def _():
            loss_ref[...] = jnp.zeros_like(loss_ref)

        y = (x_ref[...] + _dot(a_ref[...].astype(BF), w_ref[0:512, :])
             + _dot(b_ref[...].astype(BF), w_ref[512:1024, :]))
        err = y - t_ref[...]
        dy_ref[...] = err * (1.0 / D)
        per_tok = jnp.mean(err * err, axis=-1, keepdims=True)
        loss_ref[...] += 0.5 * jnp.sum(per_tok, axis=0, keepdims=True)

    row = lambda n: pl.BlockSpec((tm, n), lambda i: (i, 0))
    return pl.pallas_call(
        body, grid=(t // tm,),
        in_specs=[row(D), row(512), row(512), pl.BlockSpec((D, D), lambda i: (0, 0)), row(D)],
        out_specs=[row(D), pl.BlockSpec((1, 1), lambda i: (0, 0))],
        out_shape=[jax.ShapeDtypeStruct((t, D), F32), jax.ShapeDtypeStruct((1, 1), F32)],
        compiler_params=_cparams(("arbitrary",)), name=name)(x, ycs, yatt, wo, target)


def _dsilu(x, s):
    return s * (1.0 + x * (1.0 - s))


def _bwd_outproj(dy, ycs, yatt, o, ohp, pz, wo, name, ex=None):
    t = dy.shape[0]
    tm = 512

    def body(dy_ref, a_ref, b_ref, o_ref, ohp_ref, z_ref, w_ref, dcs_ref, do_ref, dz_ref, dl_ref, dw_ref):
        @pl.when(pl.program_id(0) == 0)
        def _():
            dw_ref[...] = jnp.zeros_like(dw_ref)

        dyb = dy_ref[...].astype(BF)
        dw_ref[0:512, :] += _dot_tn(a_ref[...].astype(BF), dyb)
        dw_ref[512:1024, :] += _dot_tn(b_ref[...].astype(BF), dyb)
        dcs_ref[...] = _dot_nt(dyb, w_ref[0:512, :])
        dyatt = _dot_nt(dyb, w_ref[512:1024, :])
        z = z_ref[...]
        sz = _sigmoid(z)
        o_t = o_ref[...]
        dob = (dyatt * (z * sz)).astype(BF)
        do_ref[...] = dob
        dz_ref[...] = dyatt * o_t * _dsilu(z, sz)
        prod = dob.astype(F32) * ohp_ref[...]
        for jb in range(DA // LANES):
            sl = slice(jb * LANES, (jb + 1) * LANES)
            dl_ref[:, sl] = _seg_sum64(prod[:, sl], exact=True)

    row = lambda n: pl.BlockSpec((tm, n), lambda i: (i, 0))
    return _host_call(
        body, (t // tm,),
        [row(D), row(512), row(512), row(DA), row(DA), row(DA), pl.BlockSpec((D, D), lambda i: (0, 0))],
        [row(512), row(DA), row(DA), row(DA), pl.BlockSpec((D, D), lambda i: (0, 0))],
        [jax.ShapeDtypeStruct((t, 512), F32), jax.ShapeDtypeStruct((t, DA), BF),
         jax.ShapeDtypeStruct((t, DA), F32), jax.ShapeDtypeStruct((t, DA), F32),
         jax.ShapeDtypeStruct((D, D), F32)], [],
        (dy, ycs, yatt, o, ohp, pz, wo), name, ex)


def _bwd_attn(qs, kn, vb, do, cq, lse, dl, crow, name, ex=None):
    bsz, seq, _ = qs.shape
    tq, tk = 256, min(BWD_KEYS, seq)
    nq, nk = seq // tq, seq // tk
    rows_c = ATT_ROWS * tq // tk
    npair = NH // 2

    def body(q_ref, k_ref, v_ref, do_ref, cq_ref, lse_ref, dl_ref, cr_ref, dq_ref, dk_ref, dv_ref, dc_ref,
             qm, dom, s_buf, dp_buf, pd_buf):
        head0 = lax.broadcasted_iota(jnp.int32, (1, LANES), 1) < HD
        zb = jnp.zeros((seq, LANES), BF)
        qm[0] = jnp.where(head0, q_ref[...], zb)
        qm[1] = jnp.where(head0, zb, q_ref[...])
        dom[0] = jnp.where(head0, do_ref[...], zb)
        dom[1] = jnp.where(head0, zb, do_ref[...])
        dq_ref[...] = jnp.zeros_like(dq_ref)

        def kloop(kj, _):
            krows = pl.ds(pl.multiple_of(kj * tk, tk), tk)
            kb = k_ref[krows, :]
            vb_t = v_ref[krows, :]
            zk = jnp.zeros_like(kb)
            km = (jnp.where(head0, kb, zk), jnp.where(head0, zk, kb))

            tile = lambda i: _tile_rows(i, tq)

            def scores(i, slot):
                for hh in range(2):
                    c0 = hh * HD
                    s_buf[slot, hh] = (_dot_nt(qm[hh, tile(i), :], kb)
                                       + (cq_ref[tile(i), c0:c0 + 1] - cr_ref[hh:hh + 1, krows]))
                    dp_buf[slot, hh] = _dot_nt(dom[hh, tile(i), :], vb_t)

            def products(slot, i, dk, dv):
                dq = jnp.zeros((tq, LANES), F32)
                for hh in range(2):
                    dsb = pd_buf[slot, hh, 1]
                    dv = dv + _dot_tn(pd_buf[slot, hh, 0], dom[hh, tile(i), :])
                    dk = dk + _dot_tn(dsb, qm[hh, tile(i), :])
                    dq = dq + _dot(dsb, km[hh])
                dq_ref[tile(i), :] += dq
                return dk, dv

            clamp = lambda i: jnp.clip(i, 0, nq - 1)

            def qstep(i, slot, carry):
                dk, dv, cs = carry
                dk, dv = products(1 - slot, clamp(i - 1), dk, dv)
                cs = list(cs)
                ic = clamp(i)
                first_visible = jnp.where(i < nq, kj * tk - i * tq, 2 * tk)
                for r in range(tq // rows_c):
                    rows = slice(r * rows_c, (r + 1) * rows_c)
                    qrows = pl.ds(pl.multiple_of(ic * tq + r * rows_c, rows_c), rows_c)
                    visible = _causal_mask(rows_c, tk) >= first_visible - r * rows_c
                    for hh in range(2):
                        c0 = hh * HD
                        s = jnp.where(visible, s_buf[slot, hh, rows, :], -1e30)
                        p = jnp.exp(s - lse_ref[qrows, c0:c0 + 1])
                        ds = p * (dp_buf[slot, hh, rows, :] - dl_ref[qrows, c0:c0 + 1])
                        pd_buf[slot, hh, 0, rows, :] = p.astype(BF)
                        pd_buf[slot, hh, 1, rows, :] = ds.astype(BF)
                        cs[hh] = cs[hh] + jnp.sum(ds, axis=0, keepdims=True)
                scores(clamp(i + 1), 1 - slot)
                return dk, dv, tuple(cs)

            i0 = kj * (tk // tq)
            scores(i0, 0)
            pd_buf[1] = jnp.zeros((2, 2, tq, tk), BF)
            zero = jnp.zeros((tk, LANES), F32)
            zrow = jnp.zeros((1, tk), F32)
            trips = (nq - i0 + 1) // 2
            dk, dv, cs = lax.fori_loop(
                0, trips, lambda t, cy: qstep(i0 + 2 * t + 1, 1, qstep(i0 + 2 * t, 0, cy)), (zero, zero, (zrow, zrow)))
            dk, dv = products(1, clamp(i0 + 2 * trips - 1), dk, dv)
            dk_ref[krows, :] = dk
            dv_ref[krows, :] = dv
            dc_ref[0:1, krows] = -cs[0]
            dc_ref[1:2, krows] = -cs[1]
            return 0

        lax.fori_loop(0, nk, kloop, 0)

    blk = pl.BlockSpec((None, seq, LANES), lambda b, h: (b, 0, h))
    rowblk = pl.BlockSpec((None, None, 2, seq), lambda b, h: (b, h, 0, 0))
    return _host_call(
        body, (bsz, npair), [blk, blk, blk, blk, blk, blk, blk, rowblk], [blk, blk, blk, rowblk],
        [jax.ShapeDtypeStruct((bsz, seq, DA), F32)] * 3 + [jax.ShapeDtypeStruct((bsz, npair, 2, seq), F32)],
        [pltpu.VMEM((2, seq, LANES), BF), pltpu.VMEM((2, seq, LANES), BF), pltpu.VMEM((2, 2, tq, tk), F32),
         pltpu.VMEM((2, 2, tq, tk), F32), pltpu.VMEM((2, 2, 2, tq, tk), BF)],
        (qs, kn, vb, do, cq, lse, dl, crow), name, ex)


def _bwd_attn_post(pq, pk, dqs, dkn, pf, dcrow, gq, gk, bf, name, ex=None):
    bsz, seq, _ = pq.shape
    ts = 512
    nt = seq // ts

    def body(q_ref, k_ref, dq_ref, dk_ref, f_ref, dc_ref, gq_ref, gk_ref, bf_ref,
             dpq_ref, dpk_ref, dpf_ref, dgq_ref, dgk_ref, dbf_ref, carry):
        @pl.when((pl.program_id(0) == 0) & (pl.program_id(1) == 0))
        def _():
            dgq_ref[...] = jnp.zeros_like(dgq_ref)
            dgk_ref[...] = jnp.zeros_like(dgk_ref)
            dbf_ref[...] = jnp.zeros_like(dbf_ref)

        @pl.when(pl.program_id(1) == 0)
        def _():
            carry[...] = jnp.zeros_like(carry)

        for x_ref, dx_ref, g_ref, dp_ref, dg_ref, scale in ((q_ref, dq_ref, gq_ref, dpq_ref, dgq_ref, 1.0 / 8.0),
                                                            (k_ref, dk_ref, gk_ref, dpk_ref, dgk_ref, 1.0)):
            for jb in range(DA // LANES):
                sl = slice(jb * LANES, (jb + 1) * LANES)
                gb = g_ref[:, sl]
                xhat, r, _ = _head_rms(x_ref[:, sl], gb)
                dn = dx_ref[:, sl] * scale
                dg_ref[:, sl] += jnp.sum(dn * xhat, axis=0, keepdims=True)
                dxh = dn * gb
                dp_ref[:, sl] = r * (dxh - xhat * (_seg_sum64(dxh * xhat) * (1.0 / HD)))

        ui = lax.broadcasted_iota(jnp.int32, (ts, ts), 0)
        tj = lax.broadcasted_iota(jnp.int32, (ts, ts), 1)
        tri = (tj >= ui).astype(BF)
        dc = jnp.concatenate([dc_ref[...], jnp.zeros((LANES - NH, ts), F32)], axis=0)
        hi, mid, lo = _split3(dc)
        dlf = _dot_nt(tri, hi) + _dot_nt(tri, mid) + _dot_nt(tri, lo) + carry[...]
        carry[...] = dlf[0:1, :]
        xf = f_ref[...] + bf_ref[...]
        lane = lax.broadcasted_iota(jnp.int32, (ts, LANES), 1)
        dfl = jnp.where(lane < NH, dlf * _sigmoid(-xf), 0.0)
        dpf_ref[...] = dfl
        dbf_ref[...] += jnp.sum(dfl, axis=0, keepdims=True)

    rev = lambda i: nt - 1 - i
    tile = lambda n: pl.BlockSpec((None, ts, n), lambda b, i: (b, rev(i), 0))
    vec = lambda n: pl.BlockSpec((1, n), lambda b, i: (0, 0))
    return _host_call(
        body, (bsz, nt),
        [tile(DA), tile(DA), tile(DA), tile(DA), tile(LANES),
         pl.BlockSpec((None, NH, ts), lambda b, i: (b, 0, rev(i))), vec(DA), vec(DA), vec(LANES)],
        [tile(DA), tile(DA), tile(LANES), vec(DA), vec(DA), vec(LANES)],
        [jax.ShapeDtypeStruct((bsz, seq, DA), F32), jax.ShapeDtypeStruct((bsz, seq, DA), F32),
         jax.ShapeDtypeStruct((bsz, seq, LANES), F32), jax.ShapeDtypeStruct((1, DA), F32),
         jax.ShapeDtypeStruct((1, DA), F32), jax.ShapeDtypeStruct((1, LANES), F32)],
        [pltpu.VMEM((1, LANES), F32)], (pq, pk, dqs, dkn, pf, dcrow, gq, gk, bf), name, ex)


def _bwd_conv(pcf, psc, dycs, cf_dw, cf_dw_b, ln_g, ln_b, cf_pw, sc_dw, name):
    bsz, seq, _ = pcf.shape
    ts = 512
    nt, tile, cur, halo, full = _conv_specs(bsz, seq, ts, -1)

    def body(cf_ref, ha_ref, hg_ref, sc_ref, hc_ref, hx_ref, dy_ref, dw_ref, b_ref, lg_ref, lb_ref, pw_ref, sdw_ref,
             dcf_ref, dsc_ref, ddw_ref, db_ref, dlg_ref, dlb_ref, dpw_ref, dsdw_ref,
             ubuf, usbuf, obuf, dubuf, dsbuf, carry_du, carry_dc):
        step = pl.program_id(1)
        first = tile(step) == 0

        @pl.when((pl.program_id(0) == 0) & (step == 0))
        def _():
            for r in (ddw_ref, db_ref, dlg_ref, dlb_ref, dpw_ref, dsdw_ref):
                r[...] = jnp.zeros_like(r)

        @pl.when(step == 0)
        def _():
            carry_du[...] = jnp.zeros((HALO, DC), F32)
            carry_dc[...] = jnp.zeros((HALO, DC), F32)

        def transposed_conv(d, carry, w_ref, taps_t):
            dubuf[0:ts, :] = d
            dubuf[ts:ts + HALO, :] = carry[...]
            carry[...] = d[0:HALO, :]
            _shift_copies(dubuf, dsbuf, sorted({off % SUBLANES for off in taps_t}))
            _tap_conv(dsbuf, w_ref, taps_t, obuf, ts, 0.0)
            return obuf[...]

        a = cf_ref[:, 0:256]
        g = cf_ref[:, 256:512]
        z = cf_ref[:, 512:768]
        lng = lg_ref[...]
        u0, n, rstd, u2, s2, u3 = _conformer_fwd(a, g, ha_ref[...], hg_ref[...], first, ubuf, usbuf, obuf, dw_ref,
                                                 b_ref[...], lng, lb_ref[...])
        u3b = u3.astype(BF)
        p = _dot(u3b, pw_ref[...])
        sz = _sigmoid(z)
        dy = dy_ref[:, 0:256]
        dcf_ref[:, 512:768] = dy * p * _dsilu(z, sz)
        dpb = (dy * (z * sz)).astype(BF)
        dpw_ref[...] += _dot_tn(u3b, dpb)
        du2 = _dot_nt(dpb, pw_ref[...]) * _dsilu(u2, s2)
        dlg_ref[...] += jnp.sum(du2 * n, axis=0, keepdims=True)
        dlb_ref[...] += jnp.sum(du2, axis=0, keepdims=True)
        dn = du2 * lng
        du1 = rstd * (dn - jnp.mean(dn, axis=-1, keepdims=True) - n * jnp.mean(dn * n, axis=-1, keepdims=True))
        db_ref[...] += jnp.sum(du1, axis=0, keepdims=True)
        du0 = transposed_conv(du1, carry_du, dw_ref, CF_TAPS_T)
        for k, tap_grad in enumerate(_tap_sums(dubuf, usbuf, CF_TAPS, ts)):
            ddw_ref[k:k + 1, :] += tap_grad
        sg = _sigmoid(g)
        dcf_ref[:, 0:256] = du0 * sg
        dcf_ref[:, 256:512] = du0 * a * sg * (1.0 - sg)

        bb = sc_ref[:, 0:256]
        c = sc_ref[:, 256:512]
        xs = sc_ref[:, 512:768]
        zs = sc_ref[:, 768:1024]
        cv = _shortconv_fwd(c, xs, hc_ref[...], hx_ref[...], first, ubuf, usbuf, obuf, sdw_ref)
        szs = _sigmoid(zs)
        dys = dy_ref[:, 256:512]
        gate = zs * szs
        dsc_ref[:, 0:256] = dys * cv * gate
        dsc_ref[:, 768:1024] = dys * bb * cv * _dsilu(zs, szs)
        dm = transposed_conv(dys * bb * gate, carry_dc, sdw_ref, SC_TAPS_T)
        for k, tap_grad in enumerate(_tap_sums(dubuf, usbuf, SC_TAPS, ts)):
            dsdw_ref[k:k + 1, :] += tap_grad
        dsc_ref[:, 256:512] = dm * xs
        dsc_ref[:, 512:768] = dm * c

    outt = lambda n: pl.BlockSpec((None, ts, n), lambda b, i: (b, tile(i), 0))
    return pl.pallas_call(
        body, grid=(bsz, nt),
        in_specs=[cur(768), halo(0), halo(1), cur(1024), halo(1), halo(2), cur(512),
                  full(CFW, DC), full(1, DC), full(1, DC), full(1, DC), full(DC, DC), full(SCW, DC)],
        out_specs=[outt(768), outt(1024), full(CFW, DC), full(1, DC), full(1, DC), full(1, DC), full(DC, DC),
                   full(SCW, DC)],
        out_shape=[jax.ShapeDtypeStruct((bsz, seq, 768), F32), jax.ShapeDtypeStruct((bsz, seq, 1024), F32),
                   jax.ShapeDtypeStruct((CFW, DC), F32), jax.ShapeDtypeStruct((1, DC), F32),
                   jax.ShapeDtypeStruct((1, DC), F32), jax.ShapeDtypeStruct((1, DC), F32),
                   jax.ShapeDtypeStruct((DC, DC), F32), jax.ShapeDtypeStruct((SCW, DC), F32)],
        scratch_shapes=[pltpu.VMEM((HALO + ts, DC), F32), pltpu.VMEM((SUBLANES, HALO + ts, DC), F32),
                        pltpu.VMEM((ts, DC), F32), pltpu.VMEM((ts + HALO, DC), F32),
                        pltpu.VMEM((SUBLANES, ts + HALO, DC), F32), pltpu.VMEM((HALO, DC), F32),
                        pltpu.VMEM((HALO, DC), F32)],
        compiler_params=_cparams(("arbitrary", "arbitrary")), name=name,
    )(pcf, pcf, pcf, psc, psc, psc, dycs, cf_dw, cf_dw_b, ln_g, ln_b, cf_pw, sc_dw)


def _bwd_inproj(x, g, dyres, w, dcf, dsc, dq, dk, dv, dz, df, name):
    t = x.shape[0]
    tm = 256
    pieces = ((C_CF, 768), (C_SC, 1024), (C_Q, DA), (C_K, DA), (C_V, DA), (C_Z, DA), (C_F, LANES))

    def body(x_ref, g_ref, dy_ref, w_hbm, dcf_ref, dsc_ref, dq_ref, dk_ref, dv_ref, dz_ref, df_ref,
             dx_ref, dg_ref, dw_hbm, w_vmem, dw_acc, sem):
        i = pl.program_id(0)

        @pl.when(i == 0)
        def _():
            cp = pltpu.make_async_copy(w_hbm, w_vmem, sem)
            cp.start()
            dw_acc[...] = jnp.zeros_like(dw_acc)
            dg_ref[...] = jnp.zeros_like(dg_ref)
            cp.wait()

        xt = x_ref[...]
        gg = g_ref[...]
        r = lax.rsqrt(jnp.mean(xt * xt, axis=-1, keepdims=True) + EPS)
        xhat = xt * r
        ht = (xhat * gg).astype(BF).T
        dh = jnp.zeros((tm, D), F32)
        for (c0, n), ref in zip(pieces, (dcf_ref, dsc_ref, dq_ref, dk_ref, dv_ref, dz_ref, df_ref)):
            for s0 in range(0, n, 512):
                s1 = min(s0 + 512, n)
                d = ref[:, s0:s1].astype(BF)
                dh = dh + _dot_nt(d, w_vmem[:, c0 + s0:c0 + s1])
                dw_acc[:, c0 + s0:c0 + s1] += _dot(ht, d)
        dg_ref[...] += jnp.sum(dh * xhat, axis=0, keepdims=True)
        dhg = dh * gg
        dx_ref[...] = dy_ref[...] + r * (dhg - xhat * jnp.mean(dhg * xhat, axis=-1, keepdims=True))

        @pl.when(i == pl.num_programs(0) - 1)
        def _():
            out = pltpu.make_async_copy(dw_acc, dw_hbm, sem)
            out.start()
            out.wait()

    row = lambda n: pl.BlockSpec((tm, n), lambda i: (i, 0))
    anyspec = pl.BlockSpec(memory_space=pl.ANY)
    return pl.pallas_call(
        body, grid=(t // tm,),
        in_specs=[row(D), pl.BlockSpec((1, D), lambda i: (0, 0)), row(D), anyspec,
                  row(768), row(1024), row(DA), row(DA), row(DA), row(DA), row(LANES)],
        out_specs=[row(D), pl.BlockSpec((1, D), lambda i: (0, 0)), anyspec],
        out_shape=[jax.ShapeDtypeStruct((t, D), F32), jax.ShapeDtypeStruct((1, D), F32),
                   jax.ShapeDtypeStruct((D, NP), F32)],
        scratch_shapes=[pltpu.VMEM((D, NP), BF), pltpu.VMEM((D, NP), F32), pltpu.SemaphoreType.DMA],
        compiler_params=_cparams(("arbitrary",)), name=name)(x, g, dyres, w, dcf, dsc, dq, dk, dv, dz, df)


def _adamw_update(w_ref, g_ref, m_ref, v_ref, d_ref, m2_ref, v2_ref):
    gt = g_ref[...]
    m2 = ADAM_B1 * m_ref[...] + (1.0 - ADAM_B1) * gt
    v2 = ADAM_B2 * v_ref[...] + (1.0 - ADAM_B2) * (gt * gt)
    m_hat = m2 / (1.0 - ADAM_B1 ** ADAM_STEP)
    v_hat = v2 / (1.0 - ADAM_B2 ** ADAM_STEP)
    d_ref[...] = -ADAM_LR * (m_hat / (jnp.sqrt(v_hat) + ADAM_EPS) + ADAM_WD * w_ref[...])
    m2_ref[...] = m2
    v2_ref[...] = v2


def _adamw_many(ws, gs, ms, vs, name):
    n = len(ws)

    def body(*refs):
        ins, outs = refs[:4 * n], refs[4 * n:]
        for i in range(n):
            _adamw_update(ins[i], ins[n + i], ins[2 * n + i], ins[3 * n + i], *outs[3 * i:3 * i + 3])

    return pl.pallas_call(
        body, out_shape=[jax.ShapeDtypeStruct(w.shape, F32) for w in ws for _ in range(3)],
        name=name)(*ws, *gs, *ms, *vs)


def _adamw(w, g, m, v, tr, name):
    rows, cols = w.shape

    def body(w_ref, g_ref, m_ref, v_ref, d_ref, m2_ref, v2_ref, g_out_ref):
        _adamw_update(w_ref, g_ref, m_ref, v_ref, d_ref, m2_ref, v2_ref)
        g_out_ref[...] = g_ref[...]

    blk = pl.BlockSpec((tr, cols), lambda i: (i, 0))
    return pl.pallas_call(
        body, grid=(rows // tr,), in_specs=[blk] * 4, out_specs=[blk] * 4,
        out_shape=[jax.ShapeDtypeStruct((rows, cols), F32)] * 4,
        compiler_params=_cparams(("arbitrary",)), name=name)(w, g, m, v)


def _place():
    x, y, c = lax.axis_index("x"), lax.axis_index("y"), lax.axis_index("c")
    chips = [(1 - x, y), (x, 1 - y), (1 - x, 1 - y)]
    return x, y, c, chips


def _halves(c, rows, align):
    rh = rows // 2
    return pl.ds(pl.multiple_of(c * rh, align), rh), pl.ds(pl.multiple_of((1 - c) * rh, align), rh)


ANY = pl.BlockSpec(memory_space=pl.ANY)


class _Exchange:
    def __init__(self, operands, out_shape, sems, start, wait, relay=None):
        self.operands, self.out_shape, self.sems, self.start, self.wait = operands, out_shape, sems, start, wait
        self.relay = relay


class _StagedCopy:
    def __init__(self, src, dst, stage, sem):
        self.src, self.dst, self.stage, self.sem = src, dst, stage, sem

    def start(self):
        pltpu.make_async_copy(self.src, self.stage, self.sem).start()

    def relay(self):
        pltpu.make_async_copy(self.src, self.stage, self.sem).wait()
        pltpu.make_async_copy(self.stage, self.dst, self.sem).start()

    def finish(self):
        pltpu.make_async_copy(self.stage, self.dst, self.sem).wait()


def _run_exchange(ex, name):
    n_in, n_out = len(ex.operands), len(ex.out_shape)

    def body(*refs):
        ins, outs, sems = refs[:n_in], refs[n_in:n_in + n_out], refs[n_in + n_out:]
        ex.start(ins, outs, sems)
        if ex.relay:
            ex.relay(ins, outs, sems)
        ex.wait(ins, outs, sems)

    return pl.pallas_call(body, in_specs=[ANY] * n_in, out_specs=[ANY] * n_out, out_shape=ex.out_shape,
                          scratch_shapes=ex.sems, name=name)(*ex.operands)


def _host_call(body, grid, in_specs, out_specs, out_shape, scratch, operands, name, ex=None):
    sem = ("arbitrary",) * len(grid)
    if ex is None:
        outs = pl.pallas_call(body, grid=grid, in_specs=in_specs, out_specs=out_specs, out_shape=out_shape,
                              scratch_shapes=scratch, compiler_params=_cparams(sem), name=name)(*operands)
        return outs, None
    n_in, n_out, n_scr = len(in_specs), len(out_specs), len(scratch)
    xi, xo = len(ex.operands), len(ex.out_shape)

    def hosted(*refs):
        ins, xins = refs[:n_in], refs[n_in:n_in + xi]
        o0 = n_in + xi
        outs, xouts = refs[o0:o0 + n_out], refs[o0 + n_out:o0 + n_out + xo]
        s0 = o0 + n_out + xo
        scr, xsems = refs[s0:s0 + n_scr], refs[s0 + n_scr:]
        first = pl.program_id(0) == 0
        half = pl.program_id(0) == grid[0] // 2
        last = pl.program_id(0) == pl.num_programs(0) - 1
        for d in range(1, len(grid)):
            first = first & (pl.program_id(d) == 0)
            half = half & (pl.program_id(d) == 0)
            last = last & (pl.program_id(d) == pl.num_programs(d) - 1)

        @pl.when(first)
        def _():
            ex.start(xins, xouts, xsems)

        if ex.relay:
            @pl.when(half)
            def _():
                ex.relay(xins, xouts, xsems)

        body(*ins, *outs, *scr)

        @pl.when(last)
        def _():
            ex.wait(xins, xouts, xsems)

    res = pl.pallas_call(
        hosted, grid=grid, in_specs=list(in_specs) + [ANY] * xi, out_specs=list(out_specs) + [ANY] * xo,
        out_shape=list(out_shape) + list(ex.out_shape), scratch_shapes=list(scratch) + list(ex.sems),
        compiler_params=_cparams(sem), name=name)(*operands, *ex.operands)
    return res[:n_out], res[n_out:]


def _all_gather_chips(shards):
    n = len(shards)

    def copies(srcs, outs, sems):
        send_sems, recv_sems, local_sems, *stage = sems
        x, y, c, chips = _place()
        me = 2 * x + y
        halves = [_halves(c, s.shape[1], 16) for s in shards]
        local = [_StagedCopy(srcs[i], outs[i].at[:, me], stage[i], local_sems.at[i]) for i in range(n)]

        def ici(i, j, chip, k):
            mine = halves[i][0]
            return pltpu.make_async_remote_copy(
                src_ref=srcs[i].at[:, mine, :], dst_ref=outs[i].at[:, k, mine, :], send_sem=send_sems.at[6 * i + j],
                recv_sem=recv_sems.at[6 * i + j], device_id=(*chip, c), device_id_type=MESH)

        def d2d(i, j, k, half):
            return pltpu.make_async_remote_copy(
                src_ref=outs[i].at[:, k, half, :], dst_ref=outs[i].at[:, k, half, :], send_sem=send_sems.at[6 * i + 3 + j],
                recv_sem=recv_sems.at[6 * i + 3 + j], device_id=(x, y, 1 - c), device_id_type=MESH)

        return me, chips, halves, local, ici, d2d

    def start(srcs, outs, sems):
        me, chips, _, local, ici, _ = copies(srcs, outs, sems)
        for cp in local:
            cp.start()
        for i in range(n):
            for j, chip in enumerate(chips):
                ici(i, j, chip, me).start()

    def relay(srcs, outs, sems):
        me, chips, halves, local, ici, d2d = copies(srcs, outs, sems)
        for cp in local:
            cp.relay()
        for i in range(n):
            for j, (px, py) in enumerate(chips):
                k = 2 * px + py
                ici(i, j, (px, py), k).wait_recv()
                d2d(i, j, k, halves[i][0]).start()

    def wait(srcs, outs, sems):
        me, chips, halves, local, ici, d2d = copies(srcs, outs, sems)
        for i in range(n):
            for j, (px, py) in enumerate(chips):
                d2d(i, j, 2 * px + py, halves[i][1]).wait_recv()
        for i in range(n):
            for j, (px, py) in enumerate(chips):
                ici(i, j, (px, py), me).wait_send()
                d2d(i, j, 2 * px + py, halves[i][0]).wait_send()
        for cp in local:
            cp.finish()

    return _Exchange(
        list(shards), [jax.ShapeDtypeStruct((s.shape[0], 4) + s.shape[1:], s.dtype) for s in shards],
        [pltpu.SemaphoreType.DMA((6 * n,)), pltpu.SemaphoreType.DMA((6 * n,)), pltpu.SemaphoreType.DMA((n,))]
        + [pltpu.VMEM(s.shape, s.dtype) for s in shards], start, wait, relay)


def _rs_swap_halves(gs):
    n = len(gs)

    def copies(srcs, gots, sems):
        send_sems, recv_sems = sems
        x, y, c, _ = _place()
        return [pltpu.make_async_remote_copy(
            src_ref=srcs[i].at[:, :, _halves(c, gs[i].shape[2], 8)[1], :], dst_ref=gots[i], send_sem=send_sems.at[i],
            recv_sem=recv_sems.at[i], device_id=(x, y, 1 - c), device_id_type=MESH) for i in range(n)]

    def start(srcs, gots, sems):
        for cp in copies(srcs, gots, sems):
            cp.start()

    def wait(srcs, gots, sems):
        for cp in copies(srcs, gots, sems):
            cp.wait()

    return _Exchange(
        list(gs), [jax.ShapeDtypeStruct(g.shape[:2] + (g.shape[2] // 2, g.shape[3]), g.dtype) for g in gs],
        [pltpu.SemaphoreType.DMA((n,)), pltpu.SemaphoreType.DMA((n,))], start, wait)


def _rs_add_pair(g, got, cidx, tb, out_dtype, name):
    nl, _, rh, cols = got.shape
    nb = rh // tb

    def body(c_ref, g_ref, o_ref, s_ref):
        s_ref[...] = (g_ref[...] + o_ref[...]).astype(out_dtype)

    blk = lambda half: pl.BlockSpec((None, None, tb, cols), (lambda l, k, i, c: (l, k, c[0] * nb + i, 0)) if half
                                    else (lambda l, k, i, c: (l, k, i, 0)))
    return pl.pallas_call(
        body,
        grid_spec=pltpu.PrefetchScalarGridSpec(num_scalar_prefetch=1, grid=(nl, 4, nb),
                                               in_specs=[blk(True), blk(False)], out_specs=blk(False)),
        out_shape=jax.ShapeDtypeStruct(got.shape, out_dtype),
        compiler_params=_cparams(("arbitrary", "arbitrary", "arbitrary")), name=name)(cidx, g, got)


def _rs_exchange_chips(pairs):
    n = len(pairs)

    def copies(srcs, gots, sems):
        send_sems, recv_sems, local_sems, *stage = sems
        x, y, c, chips = _place()
        me = 2 * x + y
        local = [_StagedCopy(srcs[i].at[:, me], gots[i].at[:, me], stage[i], local_sems.at[i]) for i in range(n)]

        def ici(i, j, chip, frm, to):
            return pltpu.make_async_remote_copy(
                src_ref=srcs[i].at[:, to], dst_ref=gots[i].at[:, frm], send_sem=send_sems.at[3 * i + j],
                recv_sem=recv_sems.at[3 * i + j], device_id=(*chip, c), device_id_type=MESH)

        sent = [ici(i, j, (px, py), me, 2 * px + py) for i in range(n) for j, (px, py) in enumerate(chips)]
        recvd = [ici(i, j, (px, py), 2 * px + py, me) for i in range(n) for j, (px, py) in enumerate(chips)]
        return local, sent, recvd

    def start(srcs, gots, sems):
        local, sent, _ = copies(srcs, gots, sems)
        for cp in local + sent:
            cp.start()

    def relay(srcs, gots, sems):
        for cp in copies(srcs, gots, sems)[0]:
            cp.relay()

    def wait(srcs, gots, sems):
        local, sent, recvd = copies(srcs, gots, sems)
        for cp in recvd:
            cp.wait_recv()
        for cp in sent:
            cp.wait_send()
        for cp in local:
            cp.finish()

    return _Exchange(
        list(pairs), [jax.ShapeDtypeStruct(p.shape, p.dtype) for p in pairs],
        [pltpu.SemaphoreType.DMA((3 * n,)), pltpu.SemaphoreType.DMA((3 * n,)), pltpu.SemaphoreType.DMA((n,))]
        + [pltpu.VMEM((p.shape[0],) + p.shape[2:], p.dtype) for p in pairs], start, wait, relay)


def _rs_add_chips(got, tb, name):
    nl, _, rh, cols = got.shape

    def body(g_ref, s_ref):
        s_ref[...] = ((g_ref[0].astype(F32) + g_ref[1].astype(F32)) + g_ref[2].astype(F32)) + g_ref[3].astype(F32)

    return pl.pallas_call(
        body, grid=(nl, rh // tb),
        in_specs=[pl.BlockSpec((None, 4, tb, cols), lambda l, i: (l, 0, i, 0))],
        out_specs=pl.BlockSpec((None, tb, cols), lambda l, i: (l, i, 0)),
        out_shape=jax.ShapeDtypeStruct((nl, rh, cols), F32),
        compiler_params=_cparams(("arbitrary", "arbitrary")), name=name)(got)


def _rs_join_halves(groups):
    totals = [t for grp in groups for t in grp]
    where = [(gi, li) for gi, grp in enumerate(groups) for li in range(len(grp))]
    n = len(totals)

    def copies(srcs, outs, sems):
        send_sems, recv_sems, local_sems, *stage = sems
        x, y, c, _ = _place()
        local, send, recv = [], [], []
        for i, (gi, li) in enumerate(where):
            mine, other = _halves(c, 2 * totals[i].shape[1], 8)
            local.append(_StagedCopy(srcs[i], outs[gi].at[li:li + 1, mine, :], stage[i], local_sems.at[i]))
            for lst, rows in ((send, mine), (recv, other)):
                lst.append(pltpu.make_async_remote_copy(
                    src_ref=srcs[i], dst_ref=outs[gi].at[li:li + 1, rows, :], send_sem=send_sems.at[i],
                    recv_sem=recv_sems.at[i], device_id=(x, y, 1 - c), device_id_type=MESH))
        return local, send, recv

    def start(srcs, outs, sems):
        local, send, _ = copies(srcs, outs, sems)
        for cp in local + send:
            cp.start()

    def relay(srcs, outs, sems):
        for cp in copies(srcs, outs, sems)[0]:
            cp.relay()

    def wait(srcs, outs, sems):
        local, send, recv = copies(srcs, outs, sems)
        for cp in recv:
            cp.wait_recv()
        for cp in send:
            cp.wait_send()
        for cp in local:
            cp.finish()

    return _Exchange(
        totals, [jax.ShapeDtypeStruct((len(grp), 2 * grp[0].shape[1], grp[0].shape[2]), F32) for grp in groups],
        [pltpu.SemaphoreType.DMA((n,)), pltpu.SemaphoreType.DMA((n,)), pltpu.SemaphoreType.DMA((n,))]
        + [pltpu.VMEM(t.shape, F32) for t in totals], start, wait, relay)


def _rows(a):
    return a.reshape(-1, LANES)


def _pad_rows(a, mult):
    r = (-a.shape[0]) % mult
    return a if r == 0 else jnp.concatenate([a, jnp.zeros((r, a.shape[1]), a.dtype)], axis=0)


def kernel(x, norm_g, w_in, b_f, cf_dw, cf_dw_b, cf_ln_g, cf_ln_b, cf_pw, sc_dw, q_norm_g, k_norm_g, w_out, loss_target, m_norm_g, m_w_in, m_b_f, m_cf_dw, m_cf_dw_b, m_cf_ln_g, m_cf_ln_b, m_cf_pw, m_sc_dw, m_q_norm_g, m_k_norm_g, m_w_out, v_norm_g, v_w_in, v_b_f, v_cf_dw, v_cf_dw_b, v_cf_ln_g, v_cf_ln_b, v_cf_pw, v_sc_dw, v_q_norm_g, v_k_norm_g, v_w_out):
    bsz, seq, _ = x.shape
    t = bsz * seq

    taps = jnp.concatenate([cf_dw.reshape(-1), sc_dw.reshape(-1)])
    taps_bf = lax.bitcast_convert_type(taps, jnp.bfloat16).reshape(-1, LANES)
    n_pw, n_taps = 2 * 64 * DC // LANES, taps_bf.shape[0]
    small_w = _pad_rows(jnp.concatenate([_rows(cf_pw.astype(BF)), taps_bf], axis=0), 32)[None]
    win_b, wout_b = w_in.astype(BF), w_out.astype(BF)
    win0_g, small_g = _run_exchange(_all_gather_chips([win_b[0:1], small_w]), "all_gather_first")
    gather_rest = _all_gather_chips([win_b[1:DEPTH], wout_b])
    full_w_in = lambda g: jnp.concatenate([g[k] for k in range(4)] + [jnp.zeros((D, NP - N_IN), BF)], axis=-1)
    w_in_full = [full_w_in(win0_g[0])] + [None] * (DEPTH - 1)
    w_out_full = None
    cf_pw_full = jnp.concatenate([small_g[0, k, 0:n_pw].reshape(DEPTH, 64, DC) for k in range(4)], axis=1)
    taps_all = [lax.bitcast_convert_type(small_g[0, k, n_pw:n_pw + n_taps].reshape(-1, 2), F32) for k in range(4)]
    n_cfdw = DEPTH * CFW * 64
    cf_dw_full = jnp.concatenate([tk[:n_cfdw].reshape(DEPTH, CFW, 64) for tk in taps_all], axis=-1)
    sc_dw_full = jnp.concatenate([tk[n_cfdw:].reshape(DEPTH, SCW, 64) for tk in taps_all], axis=-1)

    bf_pad = jnp.pad(b_f, ((0, 0), (0, LANES - NH)))

    xs = [x.reshape(t, D)]
    saved = []
    dy = loss_part = None
    for l in range(DEPTH):
        xl = xs[-1]
        gq, gk = q_norm_g[l].reshape(1, DA), k_norm_g[l].reshape(1, DA)
        pcf, psc, pq, pk, vb, pz, pf = _fwd_inproj(xl, norm_g[l][None], w_in_full[l], f"fwd_inproj_{l}")
        b3 = lambda a: a.reshape(bsz, seq, a.shape[-1])
        ycs = _fwd_conv(b3(pcf), b3(psc), cf_dw_full[l], cf_dw_b[l][None], cf_ln_g[l][None], cf_ln_b[l][None],
                        cf_pw_full[l], sc_dw_full[l], f"fwd_conv_{l}")
        qs, kn, cq, crow8 = _fwd_attn_prep(b3(pq), b3(pk), b3(pf), gq, gk, bf_pad[l][None], f"fwd_attn_prep_{l}")
        crow = crow8.reshape(bsz, NH // 2, 2, seq)
        (o, ohp, yatt, lse), landed = _fwd_attn(qs, kn, b3(vb), cq, crow, b3(pz), f"fwd_attn_{l}",
                                                gather_rest if l == 0 else None)
        if l == 0:
            win_rest_g, wout_g = landed
            w_in_full[1:] = [full_w_in(win_rest_g[i]) for i in range(DEPTH - 1)]
            w_out_full = wout_g.reshape(DEPTH, D, D)
        ycs2, yatt2 = ycs.reshape(t, 512), yatt.reshape(t, DA)
        if l + 1 < DEPTH:
            xs.append(_fwd_outproj(xl, ycs2, yatt2, w_out_full[l], f"fwd_outproj_{l}"))
        else:
            dy, loss_part = _fwd_outproj_loss(xl, ycs2, yatt2, w_out_full[l], loss_target.reshape(t, D),
                                              f"fwd_outproj_loss_{l}")
        saved.append((pcf, psc, pq, pk, vb, pz, pf, ycs2, yatt2, o, ohp, qs, kn, cq, crow, lse))

    grads = [None] * DEPTH
    reduced = [None] * DEPTH
    cidx = lax.axis_index("c").astype(jnp.int32).reshape(1)
    big_tiles = [256, 128]

    def big_packs(gl):
        return [jnp.stack([gl["w_in"][:, NSH * k:NSH * (k + 1)] for k in range(4)])[None],
                gl["w_out"].reshape(1, 4, 256, D)]

    def add_pairs(packs, gots, tiles, dtypes, tag):
        return [_rs_add_pair(p, got, cidx, tb, dt, f"rs_add_pair_{tag}_{i}")
                for i, (p, got, tb, dt) in enumerate(zip(packs, gots, tiles, dtypes))]

    def add_chips(gots2, tiles, tag):
        return [_rs_add_chips(got, tb, f"rs_add_chips_{tag}_{i}") for i, (got, tb) in enumerate(zip(gots2, tiles))]

    for l in reversed(range(DEPTH)):
        pcf, psc, pq, pk, vb, pz, pf, ycs2, yatt2, o, ohp, qs, kn, cq, crow, lse = saved[l]
        gq, gk = q_norm_g[l].reshape(1, DA), k_norm_g[l].reshape(1, DA)
        b3 = lambda a: a.reshape(bsz, seq, a.shape[-1])
        f2 = lambda a: a.reshape(t, a.shape[-1])
        packs = big_packs(grads[l + 1]) if l + 1 < DEPTH else None
        (dycs, do, dz, dl, d_wout), gots = _bwd_outproj(
            dy, ycs2, yatt2, f2(o), f2(ohp), pz, w_out_full[l], f"bwd_outproj_{l}",
            _rs_swap_halves(packs) if packs else None)
        pairs = add_pairs(packs, gots, big_tiles, [BF, BF], l + 1) if packs else None
        (dqs, dkn, dv, dcrow), gots2 = _bwd_attn(qs, kn, b3(vb), b3(do), cq, lse, b3(dl), crow, f"bwd_attn_{l}",
                                                 _rs_exchange_chips(pairs) if packs else None)
        if packs:
            reduced[l + 1] = add_chips(gots2, big_tiles, l + 1)
        (dpq, dpk, dpf, d_gq, d_gk, d_bf), _ = _bwd_attn_post(
            b3(pq), b3(pk), dqs, dkn, b3(pf), dcrow.reshape(bsz, NH, seq), gq, gk, bf_pad[l][None],
            f"bwd_attn_post_{l}")
        dcf, dsc, d_cfdw, d_cfb, d_lng, d_lnb, d_pw, d_scdw = _bwd_conv(
            b3(pcf), b3(psc), b3(dycs), cf_dw_full[l], cf_dw_b[l][None], cf_ln_g[l][None], cf_ln_b[l][None],
            cf_pw_full[l], sc_dw_full[l], f"bwd_conv_{l}")
        dy, d_ng, d_win = _bwd_inproj(xs[l], norm_g[l][None], dy, w_in_full[l], f2(dcf), f2(dsc), f2(dpq), f2(dpk),
                                      f2(dv), dz, f2(dpf), f"bwd_inproj_{l}")
        grads[l] = dict(norm_g=d_ng[0], w_in=d_win, b_f=d_bf[0, :NH], cf_dw=d_cfdw, cf_dw_b=d_cfb[0],
                        cf_ln_g=d_lng[0], cf_ln_b=d_lnb[0], cf_pw=d_pw, sc_dw=d_scdw,
                        q_norm_g=d_gq.reshape(NH, HD), k_norm_g=d_gk.reshape(NH, HD), w_out=d_wout)
    grad_x = dy.reshape(bsz, seq, D)
    gw = {n: jnp.stack([grads[l][n] for l in range(DEPTH)]) for n in grads[0] if n not in ("w_in", "w_out")}

    rep_names = ("norm_g", "b_f", "cf_dw_b", "cf_ln_g", "cf_ln_b", "q_norm_g", "k_norm_g")
    rep = jnp.concatenate([jnp.pad(gw[n].reshape(-1), (0, (-gw[n].size) % LANES)) for n in rep_names]
                          + [jnp.pad(loss_part.reshape(-1), (0, LANES - 1))]).reshape(-1, LANES)
    blocks = []
    for k in range(4):
        small = jnp.concatenate([gw["cf_dw"][:, :, 64 * k:64 * (k + 1)].reshape(-1),
                                 gw["sc_dw"][:, :, 64 * k:64 * (k + 1)].reshape(-1)]).reshape(-1, LANES)
        blocks.append(_pad_rows(jnp.concatenate([_rows(gw["cf_pw"][:, 64 * k:64 * (k + 1), :]), small, rep], axis=0), 16))
    g_small = jnp.stack(blocks)[None]
    packs = big_packs(grads[0]) + [g_small]
    tiles = big_tiles + [g_small.shape[2] // 2]
    gots = _run_exchange(_rs_swap_halves(packs), "rs_swap_halves_last")
    pairs = add_pairs(packs, gots, tiles, [BF, BF, F32], 0)
    gots2 = _run_exchange(_rs_exchange_chips(pairs), "rs_exchange_chips_last")
    tot_win0, tot_wout0, tot_small = add_chips(gots2, tiles, 0)
    reduced[0] = (tot_win0, tot_wout0)
    red_win, red_wout, red_small = _run_exchange(
        _rs_join_halves([[reduced[l][0] for l in range(DEPTH)], [reduced[l][1] for l in range(DEPTH)], [tot_small]]),
        "rs_join_halves")

    n_small = (DEPTH * CFW * 64 + DEPTH * SCW * 64) // LANES
    red = red_small[0]
    pos = [0]

    def take(nrows):
        pos[0] += nrows
        return red[pos[0] - nrows:pos[0]]

    g = {}
    g["w_in"] = red_win
    g["w_out"] = red_wout
    g["cf_pw"] = take(n_pw).reshape(DEPTH, 64, DC)
    small = take(n_small).reshape(-1)
    g["cf_dw"] = small[:n_cfdw].reshape(DEPTH, CFW, 64)
    g["sc_dw"] = small[n_cfdw:].reshape(DEPTH, SCW, 64)
    shapes = dict(norm_g=norm_g.shape, b_f=b_f.shape, cf_dw_b=cf_dw_b.shape, cf_ln_g=cf_ln_g.shape,
                  cf_ln_b=cf_ln_b.shape, q_norm_g=q_norm_g.shape, k_norm_g=k_norm_g.shape)
    for n in rep_names:
        size = 1
        for s in shapes[n]:
            size *= s
        g[n] = take(-(-size // LANES)).reshape(-1)[:size].reshape(shapes[n])
    loss = take(1)[0, 0]

    order = ("norm_g", "w_in", "b_f", "cf_dw", "cf_dw_b", "cf_ln_g", "cf_ln_b", "cf_pw", "sc_dw", "q_norm_g",
             "k_norm_g", "w_out")
    weights = dict(norm_g=norm_g, w_in=w_in, b_f=b_f, cf_dw=cf_dw, cf_dw_b=cf_dw_b, cf_ln_g=cf_ln_g, cf_ln_b=cf_ln_b,
                   cf_pw=cf_pw, sc_dw=sc_dw, q_norm_g=q_norm_g, k_norm_g=k_norm_g, w_out=w_out)
    ms = dict(norm_g=m_norm_g, w_in=m_w_in, b_f=m_b_f, cf_dw=m_cf_dw, cf_dw_b=m_cf_dw_b, cf_ln_g=m_cf_ln_g,
              cf_ln_b=m_cf_ln_b, cf_pw=m_cf_pw, sc_dw=m_sc_dw, q_norm_g=m_q_norm_g, k_norm_g=m_k_norm_g, w_out=m_w_out)
    vs = dict(norm_g=v_norm_g, w_in=v_w_in, b_f=v_b_f, cf_dw=v_cf_dw, cf_dw_b=v_cf_dw_b, cf_ln_g=v_cf_ln_g,
              cf_ln_b=v_cf_ln_b, cf_pw=v_cf_pw, sc_dw=v_sc_dw, q_norm_g=v_q_norm_g, k_norm_g=v_k_norm_g, w_out=v_w_out)
    delta, new_m, new_v = {}, {}, {}
    two_d = lambda a: a.reshape(-1, a.shape[-1])
    for n in ("w_in", "w_out"):
        outs = _adamw(two_d(weights[n]), two_d(g[n]), two_d(ms[n]), two_d(vs[n]), 256, f"adamw_{n}")
        delta[n], new_m[n], new_v[n], g[n] = (a.reshape(weights[n].shape) for a in outs)
    small_names = [n for n in order if n not in ("w_in", "w_out")]
    outs = _adamw_many([weights[n] for n in small_names], [g[n] for n in small_names],
                       [ms[n] for n in small_names], [vs[n] for n in small_names], "adamw_small")
    for i, n in enumerate(small_names):
        delta[n], new_m[n], new_v[n] = outs[3 * i:3 * i + 3]

    return (loss, grad_x, *[g[n] for n in order], *[delta[n] for n in order], *[new_m[n] for n in order],
            *[new_v[n] for n in order])
```

```python
import jax
import jax.numpy as jnp
from jax import lax
from jax.experimental import pallas as pl
from jax.experimental.pallas import tpu as pltpu

F32 = jnp.float32
BF = jnp.bfloat16
MESH = pl.DeviceIdType.MESH

DEPTH = 2
D = 1024
DC = 256
DA = 512
NH = 8
HD = 64
CFW = 31
SCW = 3
N_IN = 3848
NP = 3968
NSH = N_IN // 4
HALO = 32
ATT_ROWS = 32
BWD_KEYS = 512
FWD_ROWS = 256
SUBLANES = 8
CONV_ROWS = 32
TAP_GROUP = 8
EPS = 1e-6
LANES = 128
VMEM_LIMIT = 56 * 1024 * 1024

C_CF, C_SC, C_Q, C_K, C_V, C_Z, C_F = 0, 768, 1792, 2304, 2816, 3328, 3840

ADAM_LR = 0.001
ADAM_B1 = 0.9
ADAM_B2 = 0.999
ADAM_EPS = 1e-08
ADAM_WD = 0.01
ADAM_STEP = 10


def _cparams(sem=None):
    return pltpu.CompilerParams(dimension_semantics=sem, vmem_limit_bytes=VMEM_LIMIT)


def _dot(a, b):
    return jnp.dot(a, b, preferred_element_type=F32)


def _dot_nt(a, b):
    return lax.dot_general(a, b, (((1,), (1,)), ((), ())), preferred_element_type=F32)


def _dot_tn(a, b):
    return lax.dot_general(a, b, (((0,), (0,)), ((), ())), preferred_element_type=F32)


def _split3(x):
    hi = x.astype(BF)
    r1 = x - hi.astype(F32)
    mid = r1.astype(BF)
    lo = (r1 - mid.astype(F32)).astype(BF)
    return hi, mid, lo


def _dot_exact(a_bf, x):
    hi, mid, lo = _split3(x)
    return _dot(a_bf, hi) + _dot(a_bf, mid) + _dot(a_bf, lo)


def _sigmoid(x):
    return 1.0 / (1.0 + jnp.exp(-x))


def _seg_sum64(x, exact=False):
    i = lax.broadcasted_iota(jnp.int32, (LANES, LANES), 0)
    j = lax.broadcasted_iota(jnp.int32, (LANES, LANES), 1)
    g = ((i >= HD) == (j >= HD)).astype(BF)
    hi, mid, lo = _split3(x)
    two = _dot(hi, g) + _dot(mid, g)
    return two + _dot(lo, g) if exact else two


def _assemble_w_in(shards, name):
    tr = 256

    def body(s_ref, o_ref):
        for k in range(4):
            o_ref[:, NSH * k:NSH * (k + 1)] = s_ref[k]
        o_ref[:, N_IN:NP] = jnp.zeros((tr, NP - N_IN), shards.dtype)

    return pl.pallas_call(
        body, grid=(D // tr,), in_specs=[pl.BlockSpec((4, tr, NSH), lambda i: (0, i, 0))],
        out_specs=pl.BlockSpec((tr, NP), lambda i: (i, 0)), out_shape=jax.ShapeDtypeStruct((D, NP), shards.dtype),
        compiler_params=_cparams(("arbitrary",)), name=name)(shards)


def _fwd_inproj(x, g, w, name):
    t = x.shape[0]
    tm = 256

    def body(x_ref, g_ref, w_ref, cf_ref, sc_ref, q_ref, k_ref, v_ref, z_ref, f_ref):
        xt = x_ref[...]
        r = lax.rsqrt(jnp.mean(xt * xt, axis=-1, keepdims=True) + EPS)
        h = ((xt * r) * g_ref[...]).astype(BF)
        cf_ref[:, 0:512] = _dot(h, w_ref[:, 0:512])
        cf_ref[:, 512:768] = _dot(h, w_ref[:, 512:768])
        sc_ref[:, 0:512] = _dot(h, w_ref[:, C_SC:C_SC + 512])
        sc_ref[:, 512:1024] = _dot(h, w_ref[:, C_SC + 512:C_Q])
        q_ref[...] = _dot(h, w_ref[:, C_Q:C_K])
        k_ref[...] = _dot(h, w_ref[:, C_K:C_V])
        v_ref[...] = _dot(h, w_ref[:, C_V:C_Z]).astype(BF)
        z_ref[...] = _dot(h, w_ref[:, C_Z:C_F])
        f_ref[...] = _dot(h, w_ref[:, C_F:NP])

    row = lambda n: pl.BlockSpec((tm, n), lambda i: (i, 0))
    return pl.pallas_call(
        body, grid=(t // tm,),
        in_specs=[row(D), pl.BlockSpec((1, D), lambda i: (0, 0)), pl.BlockSpec((D, NP), lambda i: (0, 0))],
        out_specs=[row(768), row(1024), row(DA), row(DA), row(DA), row(DA), row(LANES)],
        out_shape=[jax.ShapeDtypeStruct((t, 768), F32), jax.ShapeDtypeStruct((t, 1024), F32),
                   jax.ShapeDtypeStruct((t, DA), F32), jax.ShapeDtypeStruct((t, DA), F32),
                   jax.ShapeDtypeStruct((t, DA), BF), jax.ShapeDtypeStruct((t, DA), F32),
                   jax.ShapeDtypeStruct((t, LANES), F32)],
        compiler_params=_cparams(("arbitrary",)), name=name)(x, g, w)


def _shift_copies(buf, sbuf, shifts):
    n = buf.shape[0]
    for sh in shifts:
        rows = n if sh == 0 else n - SUBLANES
        sbuf[sh, 0:rows, :] = buf[pl.ds(sh, rows), :]


def _tap_rows(sbuf, off, r0):
    sh = off % SUBLANES
    return sbuf[sh, pl.ds(pl.multiple_of(r0 + (off - sh), SUBLANES), CONV_ROWS), :]


def _tap_conv(sbuf, w_ref, offsets, dst, ts, bias):
    def chunk(c, _):
        r0 = pl.multiple_of(c * CONV_ROWS, CONV_ROWS)
        acc = jnp.zeros((CONV_ROWS, DC), F32) + bias
        for k, off in enumerate(offsets):
            acc = acc + w_ref[k:k + 1, :] * _tap_rows(sbuf, off, r0)
        dst[pl.ds(r0, CONV_ROWS), :] = acc
        return 0

    lax.fori_loop(0, ts // CONV_ROWS, chunk, 0)


def _tap_sums(a_buf, sbuf, offsets, ts):
    outs = []
    for g0 in range(0, len(offsets), TAP_GROUP):
        group = offsets[g0:g0 + TAP_GROUP]

        def chunk(c, accs, group=group):
            r0 = pl.multiple_of(c * CONV_ROWS, CONV_ROWS)
            a = a_buf[pl.ds(r0, CONV_ROWS), :]
            return tuple(acc + (a * _tap_rows(sbuf, off, r0)).reshape(CONV_ROWS // SUBLANES, SUBLANES, DC).sum(axis=0)
                         for acc, off in zip(accs, group))

        accs = lax.fori_loop(0, ts // CONV_ROWS, chunk, tuple(jnp.zeros((SUBLANES, DC), F32) for _ in group))
        outs += [jnp.sum(acc, axis=0, keepdims=True) for acc in accs]
    return outs


CF_TAPS = tuple(HALO - (CFW - 1) + k for k in range(CFW))
SC_TAPS = tuple(HALO - (SCW - 1) + k for k in range(SCW))
CF_TAPS_T = tuple(CFW - 1 - k for k in range(CFW))
SC_TAPS_T = tuple(SCW - 1 - k for k in range(SCW))


def _conformer_fwd(a, g, ha, hg, first, ubuf, usbuf, u1buf, dw_ref, bias, lng, lnb):
    ts = a.shape[0]
    u0 = a * _sigmoid(g)
    ubuf[0:HALO, :] = jnp.where(first, 0.0, ha * _sigmoid(hg))
    ubuf[HALO:HALO + ts, :] = u0
    _shift_copies(ubuf, usbuf, range(SUBLANES))
    _tap_conv(usbuf, dw_ref, CF_TAPS, u1buf, ts, bias)
    u1 = u1buf[...]
    mu = jnp.mean(u1, axis=-1, keepdims=True)
    xc = u1 - mu
    rstd = lax.rsqrt(jnp.mean(xc * xc, axis=-1, keepdims=True) + EPS)
    n = xc * rstd
    u2 = n * lng + lnb
    s2 = _sigmoid(u2)
    u3 = u2 * s2
    return u0, n, rstd, u2, s2, u3


def _shortconv_fwd(c, xs, hc, hx, first, mbuf, msbuf, cvbuf, dw_ref):
    ts = c.shape[0]
    mbuf[0:HALO, :] = jnp.where(first, 0.0, hc * hx)
    mbuf[HALO:HALO + ts, :] = c * xs
    _shift_copies(mbuf, msbuf, sorted({off % SUBLANES for off in SC_TAPS}))
    _tap_conv(msbuf, dw_ref, SC_TAPS, cvbuf, ts, 0.0)
    return cvbuf[...]


def _conv_specs(bsz, seq, ts, order):
    nt = seq // ts
    tile = (lambda i: i) if order > 0 else (lambda i: nt - 1 - i)
    hrow = lambda i: jnp.maximum(tile(i) * (ts // HALO) - 1, 0)
    cur = lambda n: pl.BlockSpec((None, ts, n), lambda b, i: (b, tile(i), 0))
    halo = lambda j: pl.BlockSpec((None, HALO, DC), lambda b, i: (b, hrow(i), j))
    full = lambda r, c: pl.BlockSpec((r, c), lambda b, i: (0, 0))
    return nt, tile, cur, halo, full


def _fwd_conv(pcf, psc, cf_dw, cf_dw_b, ln_g, ln_b, cf_pw, sc_dw, name):
    bsz, seq, _ = pcf.shape
    ts = 512
    nt, tile, cur, halo, full = _conv_specs(bsz, seq, ts, +1)

    def body(cf_ref, ha_ref, hg_ref, sc_ref, hc_ref, hx_ref, dw_ref, b_ref, lg_ref, lb_ref, pw_ref, sdw_ref,
             y_ref, ubuf, sbuf, obuf):
        first = pl.program_id(1) == 0
        _, _, _, _, _, u3 = _conformer_fwd(cf_ref[:, 0:256], cf_ref[:, 256:512], ha_ref[...], hg_ref[...], first,
                                           ubuf, sbuf, obuf, dw_ref, b_ref[...], lg_ref[...], lb_ref[...])
        z = cf_ref[:, 512:768]
        y_ref[:, 0:256] = _dot(u3.astype(BF), pw_ref[...]) * (z * _sigmoid(z))
        cv = _shortconv_fwd(sc_ref[:, 256:512], sc_ref[:, 512:768], hc_ref[...], hx_ref[...], first,
                            ubuf, sbuf, obuf, sdw_ref)
        zs = sc_ref[:, 768:1024]
        y_ref[:, 256:512] = sc_ref[:, 0:256] * cv * (zs * _sigmoid(zs))

    return pl.pallas_call(
        body, grid=(bsz, nt),
        in_specs=[cur(768), halo(0), halo(1), cur(1024), halo(1), halo(2),
                  full(CFW, DC), full(1, DC), full(1, DC), full(1, DC), full(DC, DC), full(SCW, DC)],
        out_specs=pl.BlockSpec((None, ts, 512), lambda b, i: (b, i, 0)),
        out_shape=jax.ShapeDtypeStruct((bsz, seq, 512), F32),
        scratch_shapes=[pltpu.VMEM((HALO + ts, DC), F32), pltpu.VMEM((SUBLANES, HALO + ts, DC), F32),
                        pltpu.VMEM((ts, DC), F32)],
        compiler_params=_cparams(("arbitrary", "arbitrary")), name=name,
    )(pcf, pcf, pcf, psc, psc, psc, cf_dw, cf_dw_b, ln_g, ln_b, cf_pw, sc_dw)


def _head_rms(xb, gb):
    ms = _seg_sum64(xb * xb) * (1.0 / HD)
    r = lax.rsqrt(ms + EPS)
    xhat = xb * r
    return xhat, r, xhat * gb


def _fwd_attn_prep(pq, pk, pf, gq, gk, bf, name):
    bsz, seq, _ = pq.shape
    ts = 512
    nt = seq // ts

    def body(q_ref, k_ref, f_ref, gq_ref, gk_ref, bf_ref, qs_ref, kn_ref, cq_ref, crow_ref, carry):
        @pl.when(pl.program_id(1) == 0)
        def _():
            carry[...] = jnp.zeros_like(carry)

        for jb in range(DA // LANES):
            sl = slice(jb * LANES, (jb + 1) * LANES)
            _, _, qn = _head_rms(q_ref[:, sl], gq_ref[:, sl])
            qs_ref[:, sl] = (qn * (1.0 / 8.0)).astype(BF)
            _, _, kn = _head_rms(k_ref[:, sl], gk_ref[:, sl])
            kn_ref[:, sl] = kn.astype(BF)

        xf = f_ref[...] + bf_ref[...]
        lf = jnp.minimum(xf, 0.0) - jnp.log(1.0 + jnp.exp(-jnp.abs(xf)))
        ti = lax.broadcasted_iota(jnp.int32, (ts, ts), 0)
        si = lax.broadcasted_iota(jnp.int32, (ts, ts), 1)
        c = _dot_exact((si <= ti).astype(BF), lf) + carry[...]
        carry[...] = c[ts - 1:ts, :]
        hj = lax.broadcasted_iota(jnp.int32, (LANES, DA), 0)
        ll = lax.broadcasted_iota(jnp.int32, (LANES, DA), 1)
        dd = ll - hj * HD
        chi, cmid, clo = _split3(c)
        e = ((dd >= 0) & (dd < HD)).astype(BF)
        cq_ref[...] = _dot(chi, e) + _dot(cmid, e) + _dot(clo, e)
        hh = lax.broadcasted_iota(jnp.int32, (16, LANES), 0)
        jj = lax.broadcasted_iota(jnp.int32, (16, LANES), 1)
        sel = (hh == jj).astype(BF)
        cr = _dot_nt(sel, chi) + _dot_nt(sel, cmid) + _dot_nt(sel, clo)
        crow_ref[...] = cr[0:NH, :]

    tile = lambda n: pl.BlockSpec((None, ts, n), lambda b, i: (b, i, 0))
    vec = lambda n: pl.BlockSpec((1, n), lambda b, i: (0, 0))
    return pl.pallas_call(
        body, grid=(bsz, nt),
        in_specs=[tile(DA), tile(DA), tile(LANES), vec(DA), vec(DA), vec(LANES)],
        out_specs=[tile(DA), tile(DA), tile(DA), pl.BlockSpec((None, NH, ts), lambda b, i: (b, 0, i))],
        out_shape=[jax.ShapeDtypeStruct((bsz, seq, DA), BF), jax.ShapeDtypeStruct((bsz, seq, DA), BF),
                   jax.ShapeDtypeStruct((bsz, seq, DA), F32), jax.ShapeDtypeStruct((bsz, NH, seq), F32)],
        scratch_shapes=[pltpu.VMEM((1, LANES), F32)],
        compiler_params=_cparams(("arbitrary", "arbitrary")), name=name)(pq, pk, pf, gq, gk, bf)


def _tile_rows(i, rows):
    return pl.ds(i * rows, rows) if isinstance(i, int) else pl.ds(pl.multiple_of(i * rows, rows), rows)


def _causal_mask(tq, tk):
    r = lax.broadcasted_iota(jnp.int32, (tq, tk), 0)
    c = lax.broadcasted_iota(jnp.int32, (tq, tk), 1)
    return r - c


def _fwd_attn(qs, kn, vb, cq, crow, pz, name, ex=None):
    bsz, seq, _ = qs.shape
    tq, tk = 256, 256
    nq, nk = seq // tq, seq // tk
    npair = NH // 2

    def body(q_ref, k_ref, v_ref, cq_ref, cr_ref, z_ref, o_ref, ohp_ref, y_ref, lse_ref,
             s_buf, p_buf, acc_ref, accl_ref):
        qi = pl.program_id(2)
        head0 = lax.broadcasted_iota(jnp.int32, (1, LANES), 1) < HD
        q = q_ref[...]
        zq = jnp.zeros_like(q)
        qm = (jnp.where(head0, q, zq), jnp.where(head0, zq, q))
        ct = (cq_ref[:, 0:1], cq_ref[:, HD:HD + 1])

        tile = lambda j: _tile_rows(j, tk)

        def scores(j, slot):
            kb = k_ref[tile(j), :]
            for hh in range(2):
                s_buf[slot, hh] = _dot_nt(qm[hh], kb) + (ct[hh] - cr_ref[hh:hh + 1, tile(j)])

        def weighted_values(slot, j, al):
            vb_t = v_ref[tile(j), :]
            acc_ref[...] = al * acc_ref[...] + jnp.where(head0, _dot(p_buf[slot, 0, 0], vb_t),
                                                         _dot(p_buf[slot, 1, 0], vb_t))
            accl_ref[...] = al * accl_ref[...] + jnp.where(head0, _dot(p_buf[slot, 0, 1], vb_t),
                                                           _dot(p_buf[slot, 1, 1], vb_t))

        def softmax(slot, first_visible, m, l):
            m_new, l_new, al_new = ([], []), ([], []), []
            for r in range(tq // FWD_ROWS):
                rows = slice(r * FWD_ROWS, (r + 1) * FWD_ROWS)
                visible = _causal_mask(FWD_ROWS, tk) >= first_visible - r * FWD_ROWS
                alphas = []
                for hh in range(2):
                    s = jnp.where(visible, s_buf[slot, hh, rows, :], -1e30)
                    m_old = m[hh][rows]
                    m2 = jnp.maximum(m_old, jnp.max(s, axis=-1, keepdims=True))
                    p = jnp.exp(s - m2)
                    alpha = jnp.exp(m_old - m2)
                    l_new[hh].append(alpha * l[hh][rows] + jnp.sum(p, axis=-1, keepdims=True))
                    m_new[hh].append(m2)
                    pb = p.astype(BF)
                    p_buf[slot, hh, 0, rows, :] = pb
                    p_buf[slot, hh, 1, rows, :] = (p - pb.astype(F32)).astype(BF)
                    alphas.append(alpha)
                al_new.append(jnp.where(head0, alphas[0], alphas[1]))
            cat = lambda parts: jnp.concatenate(parts, axis=0)
            return (cat(m_new[0]), cat(m_new[1])), (cat(l_new[0]), cat(l_new[1])), cat(al_new)

        clamp = lambda j: jnp.clip(j, 0, nk - 1)

        def step(j, slot, carry):
            m, l, al = carry
            weighted_values(1 - slot, clamp(j - 1), al)
            m, l, al = softmax(slot, j * tk - qi * tq, m, l)
            scores(clamp(j + 1), 1 - slot)
            return m, l, al

        scores(0, 0)
        p_buf[1] = jnp.zeros((2, 2, tq, tk), BF)
        acc_ref[...] = jnp.zeros((tq, LANES), F32)
        accl_ref[...] = jnp.zeros((tq, LANES), F32)
        neg = jnp.full((tq, 1), -1e30, F32)
        zcol = jnp.zeros((tq, 1), F32)
        trips = ((qi + 1) * (tq // tk) + 1) // 2
        m, l, al = lax.fori_loop(0, trips, lambda t, cy: step(2 * t + 1, 1, step(2 * t, 0, cy)),
                                 ((neg, neg), (zcol, zcol), jnp.ones((tq, LANES), F32)))
        weighted_values(1, clamp(2 * trips - 1), al)
        inv = jnp.where(head0, 1.0 / l[0], 1.0 / l[1])
        o = acc_ref[...] * inv
        z = z_ref[...]
        o_ref[...] = o
        ohp_ref[...] = (acc_ref[...] + accl_ref[...]) * inv
        y_ref[...] = o * (z * _sigmoid(z))
        lse_ref[...] = jnp.where(head0, m[0] + jnp.log(l[0]), m[1] + jnp.log(l[1]))

    qblk = pl.BlockSpec((None, tq, LANES), lambda b, h, i: (b, i, h))
    kvblk = pl.BlockSpec((None, seq, LANES), lambda b, h, i: (b, 0, h))
    return _host_call(
        body, (bsz, npair, nq),
        [qblk, kvblk, kvblk, qblk, pl.BlockSpec((None, None, 2, seq), lambda b, h, i: (b, h, 0, 0)), qblk],
        [qblk, qblk, qblk, qblk], [jax.ShapeDtypeStruct((bsz, seq, DA), F32)] * 4,
        [pltpu.VMEM((2, 2, tq, tk), F32), pltpu.VMEM((2, 2, 2, tq, tk), BF), pltpu.VMEM((tq, LANES), F32),
         pltpu.VMEM((tq, LANES), F32)],
        (qs, kn, vb, cq, crow, pz), name, ex)


def _fwd_outproj(x, ycs, yatt, wo, name):
    t = x.shape[0]
    tm = 512

    def body(x_ref, a_ref, b_ref, w_ref, o_ref):
        o_ref[...] = (x_ref[...] + _dot(a_ref[...].astype(BF), w_ref[0:512, :])
                      + _dot(b_ref[...].astype(BF), w_ref[512:1024, :]))

    row = lambda n: pl.BlockSpec((tm, n), lambda i: (i, 0))
    return pl.pallas_call(
        body, grid=(t // tm,),
        in_specs=[row(D), row(512), row(512), pl.BlockSpec((D, D), lambda i: (0, 0))],
        out_specs=row(D), out_shape=jax.ShapeDtypeStruct((t, D), F32),
        compiler_params=_cparams(("arbitrary",)), name=name)(x, ycs, yatt, wo)


def _fwd_outproj_loss(x, ycs, yatt, wo, target, name):
    t = x.shape[0]
    tm = 512

    def body(x_ref, a_ref, b_ref, w_ref, t_ref, dy_ref, loss_ref):
        @pl.when(pl.program_id(0) == 0)
        def _():
            loss_ref[...] = jnp.zeros_like(loss_ref)

        y = (x_ref[...] + _dot(a_ref[...].astype(BF), w_ref[0:512, :])
             + _dot(b_ref[...].astype(BF), w_ref[512:1024, :]))
        err = y - t_ref[...]
        dy_ref[...] = err * (1.0 / D)
        per_tok = jnp.mean(err * err, axis=-1, keepdims=True)
        loss_ref[...] += 0.5 * jnp.sum(per_tok, axis=0, keepdims=True)

    row = lambda n: pl.BlockSpec((tm, n), lambda i: (i, 0))
    return pl.pallas_call(
        body, grid=(t // tm,),
        in_specs=[row(D), row(512), row(512), pl.BlockSpec((D, D), lambda i: (0, 0)), row(D)],
        out_specs=[row(D), pl.BlockSpec((1, 1), lambda i: (0, 0))],
        out_shape=[jax.ShapeDtypeStruct((t, D), F32), jax.ShapeDtypeStruct((1, 1), F32)],
        compiler_params=_cparams(("arbitrary",)), name=name)(x, ycs, yatt, wo, target)


def _dsilu(x, s):
    return s * (1.0 + x * (1.0 - s))


def _bwd_outproj(dy, ycs, yatt, o, ohp, pz, wo, name, ex=None):
    t = dy.shape[0]
    tm = 512

    def body(dy_ref, a_ref, b_ref, o_ref, ohp_ref, z_ref, w_ref, dcs_ref, do_ref, dz_ref, dl_ref, dw_ref):
        @pl.when(pl.program_id(0) == 0)
        def _():
            dw_ref[...] = jnp.zeros_like(dw_ref)

        dyb = dy_ref[...].astype(BF)
        dw_ref[0:512, :] += _dot_tn(a_ref[...].astype(BF), dyb)
        dw_ref[512:1024, :] += _dot_tn(b_ref[...].astype(BF), dyb)
        dcs_ref[...] = _dot_nt(dyb, w_ref[0:512, :])
        dyatt = _dot_nt(dyb, w_ref[512:1024, :])
        z = z_ref[...]
        sz = _sigmoid(z)
        o_t = o_ref[...]
        dob = (dyatt * (z * sz)).astype(BF)
        do_ref[...] = dob
        dz_ref[...] = dyatt * o_t * _dsilu(z, sz)
        prod = dob.astype(F32) * ohp_ref[...]
        for jb in range(DA // LANES):
            sl = slice(jb * LANES, (jb + 1) * LANES)
            dl_ref[:, sl] = _seg_sum64(prod[:, sl], exact=True)

    row = lambda n: pl.BlockSpec((tm, n), lambda i: (i, 0))
    return _host_call(
        body, (t // tm,),
        [row(D), row(512), row(512), row(DA), row(DA), row(DA), pl.BlockSpec((D, D), lambda i: (0, 0))],
        [row(512), row(DA), row(DA), row(DA), pl.BlockSpec((D, D), lambda i: (0, 0))],
        [jax.ShapeDtypeStruct((t, 512), F32), jax.ShapeDtypeStruct((t, DA), BF),
         jax.ShapeDtypeStruct((t, DA), F32), jax.ShapeDtypeStruct((t, DA), F32),
         jax.ShapeDtypeStruct((D, D), F32)], [],
        (dy, ycs, yatt, o, ohp, pz, wo), name, ex)


def _bwd_attn(qs, kn, vb, do, cq, lse, dl, crow, name, ex=None):
    bsz, seq, _ = qs.shape
    tq, tk = 256, min(BWD_KEYS, seq)
    nq, nk = seq // tq, seq // tk
    rows_c = ATT_ROWS * tq // tk
    npair = NH // 2

    def body(q_ref, k_ref, v_ref, do_ref, cq_ref, lse_ref, dl_ref, cr_ref, dq_ref, dk_ref, dv_ref, dc_ref,
             qm, dom, s_buf, dp_buf, pd_buf):
        head0 = lax.broadcasted_iota(jnp.int32, (1, LANES), 1) < HD
        zb = jnp.zeros((seq, LANES), BF)
        qm[0] = jnp.where(head0, q_ref[...], zb)
        qm[1] = jnp.where(head0, zb, q_ref[...])
        dom[0] = jnp.where(head0, do_ref[...], zb)
        dom[1] = jnp.where(head0, zb, do_ref[...])
        dq_ref[...] = jnp.zeros_like(dq_ref)

        def kloop(kj, _):
            krows = pl.ds(pl.multiple_of(kj * tk, tk), tk)
            kb = k_ref[krows, :]
            vb_t = v_ref[krows, :]
            zk = jnp.zeros_like(kb)
            km = (jnp.where(head0, kb, zk), jnp.where(head0, zk, kb))

            tile = lambda i: _tile_rows(i, tq)

            def scores(i, slot):
                for hh in range(2):
                    c0 = hh * HD
                    s_buf[slot, hh] = (_dot_nt(qm[hh, tile(i), :], kb)
                                       + (cq_ref[tile(i), c0:c0 + 1] - cr_ref[hh:hh + 1, krows]))
                    dp_buf[slot, hh] = _dot_nt(dom[hh, tile(i), :], vb_t)

            def products(slot, i, dk, dv):
                dq = jnp.zeros((tq, LANES), F32)
                for hh in range(2):
                    dsb = pd_buf[slot, hh, 1]
                    dv = dv + _dot_tn(pd_buf[slot, hh, 0], dom[hh, tile(i), :])
                    dk = dk + _dot_tn(dsb, qm[hh, tile(i), :])
                    dq = dq + _dot(dsb, km[hh])
                dq_ref[tile(i), :] += dq
                return dk, dv

            clamp = lambda i: jnp.clip(i, 0, nq - 1)

            def qstep(i, slot, carry):
                dk, dv, cs = carry
                dk, dv = products(1 - slot, clamp(i - 1), dk, dv)
                cs = list(cs)
                ic = clamp(i)
                first_visible = jnp.where(i < nq, kj * tk - i * tq, 2 * tk)
                for r in range(tq // rows_c):
                    rows = slice(r * rows_c, (r + 1) * rows_c)
                    qrows = pl.ds(pl.multiple_of(ic * tq + r * rows_c, rows_c), rows_c)
                    visible = _causal_mask(rows_c, tk) >= first_visible - r * rows_c
                    for hh in range(2):
                        c0 = hh * HD
                        s = jnp.where(visible, s_buf[slot, hh, rows, :], -1e30)
                        p = jnp.exp(s - lse_ref[qrows, c0:c0 + 1])
                        ds = p * (dp_buf[slot, hh, rows, :] - dl_ref[qrows, c0:c0 + 1])
                        pd_buf[slot, hh, 0, rows, :] = p.astype(BF)
                        pd_buf[slot, hh, 1, rows, :] = ds.astype(BF)
                        cs[hh] = cs[hh] + jnp.sum(ds, axis=0, keepdims=True)
                scores(clamp(i + 1), 1 - slot)
                return dk, dv, tuple(cs)

            i0 = kj * (tk // tq)
            scores(i0, 0)
            pd_buf[1] = jnp.zeros((2, 2, tq, tk), BF)
            zero = jnp.zeros((tk, LANES), F32)
            zrow = jnp.zeros((1, tk), F32)
            trips = (nq - i0 + 1) // 2
            dk, dv, cs = lax.fori_loop(
                0, trips, lambda t, cy: qstep(i0 + 2 * t + 1, 1, qstep(i0 + 2 * t, 0, cy)), (zero, zero, (zrow, zrow)))
            dk, dv = products(1, clamp(i0 + 2 * trips - 1), dk, dv)
            dk_ref[krows, :] = dk
            dv_ref[krows, :] = dv
            dc_ref[0:1, krows] = -cs[0]
            dc_ref[1:2, krows] = -cs[1]
            return 0

        lax.fori_loop(0, nk, kloop, 0)

    blk = pl.BlockSpec((None, seq, LANES), lambda b, h: (b, 0, h))
    rowblk = pl.BlockSpec((None, None, 2, seq), lambda b, h: (b, h, 0, 0))
    return _host_call(
        body, (bsz, npair), [blk, blk, blk, blk, blk, blk, blk, rowblk], [blk, blk, blk, rowblk],
        [jax.ShapeDtypeStruct((bsz, seq, DA), F32)] * 3 + [jax.ShapeDtypeStruct((bsz, npair, 2, seq), F32)],
        [pltpu.VMEM((2, seq, LANES), BF), pltpu.VMEM((2, seq, LANES), BF), pltpu.VMEM((2, 2, tq, tk), F32),
         pltpu.VMEM((2, 2, tq, tk), F32), pltpu.VMEM((2, 2, 2, tq, tk), BF)],
        (qs, kn, vb, do, cq, lse, dl, crow), name, ex)


def _bwd_attn_post(pq, pk, dqs, dkn, pf, dcrow, gq, gk, bf, name, ex=None):
    bsz, seq, _ = pq.shape
    ts = 512
    nt = seq // ts

    def body(q_ref, k_ref, dq_ref, dk_ref, f_ref, dc_ref, gq_ref, gk_ref, bf_ref,
             dpq_ref, dpk_ref, dpf_ref, dgq_ref, dgk_ref, dbf_ref, carry):
        @pl.when((pl.program_id(0) == 0) & (pl.program_id(1) == 0))
        def _():
            dgq_ref[...] = jnp.zeros_like(dgq_ref)
            dgk_ref[...] = jnp.zeros_like(dgk_ref)
            dbf_ref[...] = jnp.zeros_like(dbf_ref)

        @pl.when(pl.program_id(1) == 0)
        def _():
            carry[...] = jnp.zeros_like(carry)

        for x_ref, dx_ref, g_ref, dp_ref, dg_ref, scale in ((q_ref, dq_ref, gq_ref, dpq_ref, dgq_ref, 1.0 / 8.0),
                                                            (k_ref, dk_ref, gk_ref, dpk_ref, dgk_ref, 1.0)):
            for jb in range(DA // LANES):
                sl = slice(jb * LANES, (jb + 1) * LANES)
                gb = g_ref[:, sl]
                xhat, r, _ = _head_rms(x_ref[:, sl], gb)
                dn = dx_ref[:, sl] * scale
                dg_ref[:, sl] += jnp.sum(dn * xhat, axis=0, keepdims=True)
                dxh = dn * gb
                dp_ref[:, sl] = r * (dxh - xhat * (_seg_sum64(dxh * xhat) * (1.0 / HD)))

        ui = lax.broadcasted_iota(jnp.int32, (ts, ts), 0)
        tj = lax.broadcasted_iota(jnp.int32, (ts, ts), 1)
        tri = (tj >= ui).astype(BF)
        dc = jnp.concatenate([dc_ref[...], jnp.zeros((LANES - NH, ts), F32)], axis=0)
        hi, mid, lo = _split3(dc)
        dlf = _dot_nt(tri, hi) + _dot_nt(tri, mid) + _dot_nt(tri, lo) + carry[...]
        carry[...] = dlf[0:1, :]
        xf = f_ref[...] + bf_ref[...]
        lane = lax.broadcasted_iota(jnp.int32, (ts, LANES), 1)
        dfl = jnp.where(lane < NH, dlf * _sigmoid(-xf), 0.0)
        dpf_ref[...] = dfl
        dbf_ref[...] += jnp.sum(dfl, axis=0, keepdims=True)

    rev = lambda i: nt - 1 - i
    tile = lambda n: pl.BlockSpec((None, ts, n), lambda b, i: (b, rev(i), 0))
    vec = lambda n: pl.BlockSpec((1, n), lambda b, i: (0, 0))
    return _host_call(
        body, (bsz, nt),
        [tile(DA), tile(DA), tile(DA), tile(DA), tile(LANES),
         pl.BlockSpec((None, NH, ts), lambda b, i: (b, 0, rev(i))), vec(DA), vec(DA), vec(LANES)],
        [tile(DA), tile(DA), tile(LANES), vec(DA), vec(DA), vec(LANES)],
        [jax.ShapeDtypeStruct((bsz, seq, DA), F32), jax.ShapeDtypeStruct((bsz, seq, DA), F32),
         jax.ShapeDtypeStruct((bsz, seq, LANES), F32), jax.ShapeDtypeStruct((1, DA), F32),
         jax.ShapeDtypeStruct((1, DA), F32), jax.ShapeDtypeStruct((1, LANES), F32)],
        [pltpu.VMEM((1, LANES), F32)], (pq, pk, dqs, dkn, pf, dcrow, gq, gk, bf), name, ex)


def _bwd_conv(pcf, psc, dycs, cf_dw, cf_dw_b, ln_g, ln_b, cf_pw, sc_dw, name):
    bsz, seq, _ = pcf.shape
    ts = 512
    nt, tile, cur, halo, full = _conv_specs(bsz, seq, ts, -1)

    def body(cf_ref, ha_ref, hg_ref, sc_ref, hc_ref, hx_ref, dy_ref, dw_ref, b_ref, lg_ref, lb_ref, pw_ref, sdw_ref,
             dcf_ref, dsc_ref, ddw_ref, db_ref, dlg_ref, dlb_ref, dpw_ref, dsdw_ref,
             ubuf, usbuf, obuf, dubuf, dsbuf, carry_du, carry_dc):
        step = pl.program_id(1)
        first = tile(step) == 0

        @pl.when((pl.program_id(0) == 0) & (step == 0))
        def _():
            for r in (ddw_ref, db_ref, dlg_ref, dlb_ref, dpw_ref, dsdw_ref):
                r[...] = jnp.zeros_like(r)

        @pl.when(step == 0)
        def _():
            carry_du[...] = jnp.zeros((HALO, DC), F32)
            carry_dc[...] = jnp.zeros((HALO, DC), F32)

        def transposed_conv(d, carry, w_ref, taps_t):
            dubuf[0:ts, :] = d
            dubuf[ts:ts + HALO, :] = carry[...]
            carry[...] = d[0:HALO, :]
            _shift_copies(dubuf, dsbuf, sorted({off % SUBLANES for off in taps_t}))
            _tap_conv(dsbuf, w_ref, taps_t, obuf, ts, 0.0)
            return obuf[...]

        a = cf_ref[:, 0:256]
        g = cf_ref[:, 256:512]
        z = cf_ref[:, 512:768]
        lng = lg_ref[...]
        u0, n, rstd, u2, s2, u3 = _conformer_fwd(a, g, ha_ref[...], hg_ref[...], first, ubuf, usbuf, obuf, dw_ref,
                                                 b_ref[...], lng, lb_ref[...])
        u3b = u3.astype(BF)
        p = _dot(u3b, pw_ref[...])
        sz = _sigmoid(z)
        dy = dy_ref[:, 0:256]
        dcf_ref[:, 512:768] = dy * p * _dsilu(z, sz)
        dpb = (dy * (z * sz)).astype(BF)
        dpw_ref[...] += _dot_tn(u3b, dpb)
        du2 = _dot_nt(dpb, pw_ref[...]) * _dsilu(u2, s2)
        dlg_ref[...] += jnp.sum(du2 * n, axis=0, keepdims=True)
        dlb_ref[...] += jnp.sum(du2, axis=0, keepdims=True)
        dn = du2 * lng
        du1 = rstd * (dn - jnp.mean(dn, axis=-1, keepdims=True) - n * jnp.mean(dn * n, axis=-1, keepdims=True))
        db_ref[...] += jnp.sum(du1, axis=0, keepdims=True)
        du0 = transposed_conv(du1, carry_du, dw_ref, CF_TAPS_T)
        for k, tap_grad in enumerate(_tap_sums(dubuf, usbuf, CF_TAPS, ts)):
            ddw_ref[k:k + 1, :] += tap_grad
        sg = _sigmoid(g)
        dcf_ref[:, 0:256] = du0 * sg
        dcf_ref[:, 256:512] = du0 * a * sg * (1.0 - sg)

        bb = sc_ref[:, 0:256]
        c = sc_ref[:, 256:512]
        xs = sc_ref[:, 512:768]
        zs = sc_ref[:, 768:1024]
        cv = _shortconv_fwd(c, xs, hc_ref[...], hx_ref[...], first, ubuf, usbuf, obuf, sdw_ref)
        szs = _sigmoid(zs)
        dys = dy_ref[:, 256:512]
        gate = zs * szs
        dsc_ref[:, 0:256] = dys * cv * gate
        dsc_ref[:, 768:1024] = dys * bb * cv * _dsilu(zs, szs)
        dm = transposed_conv(dys * bb * gate, carry_dc, sdw_ref, SC_TAPS_T)
        for k, tap_grad in enumerate(_tap_sums(dubuf, usbuf, SC_TAPS, ts)):
            dsdw_ref[k:k + 1, :] += tap_grad
        dsc_ref[:, 256:512] = dm * xs
        dsc_ref[:, 512:768] = dm * c

    outt = lambda n: pl.BlockSpec((None, ts, n), lambda b, i: (b, tile(i), 0))
    return pl.pallas_call(
        body, grid=(bsz, nt),
        in_specs=[cur(768), halo(0), halo(1), cur(1024), halo(1), halo(2), cur(512),
                  full(CFW, DC), full(1, DC), full(1, DC), full(1, DC), full(DC, DC), full(SCW, DC)],
        out_specs=[outt(768), outt(1024), full(CFW, DC), full(1, DC), full(1, DC), full(1, DC), full(DC, DC),
                   full(SCW, DC)],
        out_shape=[jax.ShapeDtypeStruct((bsz, seq, 768), F32), jax.ShapeDtypeStruct((bsz, seq, 1024), F32),
                   jax.ShapeDtypeStruct((CFW, DC), F32), jax.ShapeDtypeStruct((1, DC), F32),
                   jax.ShapeDtypeStruct((1, DC), F32), jax.ShapeDtypeStruct((1, DC), F32),
                   jax.ShapeDtypeStruct((DC, DC), F32), jax.ShapeDtypeStruct((SCW, DC), F32)],
        scratch_shapes=[pltpu.VMEM((HALO + ts, DC), F32), pltpu.VMEM((SUBLANES, HALO + ts, DC), F32),
                        pltpu.VMEM((ts, DC), F32), pltpu.VMEM((ts + HALO, DC), F32),
                        pltpu.VMEM((SUBLANES, ts + HALO, DC), F32), pltpu.VMEM((HALO, DC), F32),
                        pltpu.VMEM((HALO, DC), F32)],
        compiler_params=_cparams(("arbitrary", "arbitrary")), name=name,
    )(pcf, pcf, pcf, psc, psc, psc, dycs, cf_dw, cf_dw_b, ln_g, ln_b, cf_pw, sc_dw)


def _bwd_inproj(x, g, dyres, w, dcf, dsc, dq, dk, dv, dz, df, name):
    t = x.shape[0]
    tm = 256
    pieces = ((C_CF, 768), (C_SC, 1024), (C_Q, DA), (C_K, DA), (C_V, DA), (C_Z, DA), (C_F, LANES))

    def body(x_ref, g_ref, dy_ref, w_hbm, dcf_ref, dsc_ref, dq_ref, dk_ref, dv_ref, dz_ref, df_ref,
             dx_ref, dg_ref, dw_hbm, w_vmem, dw_acc, sem, stage, out_sems):
        i = pl.program_id(0)

        @pl.when(i == 0)
        def _():
            cp = pltpu.make_async_copy(w_hbm, w_vmem, sem)
            cp.start()
            dw_acc[...] = jnp.zeros_like(dw_acc)
            dg_ref[...] = jnp.zeros_like(dg_ref)
            cp.wait()

        xt = x_ref[...]
        gg = g_ref[...]
        r = lax.rsqrt(jnp.mean(xt * xt, axis=-1, keepdims=True) + EPS)
        xhat = xt * r
        ht = (xhat * gg).astype(BF).T
        dh = jnp.zeros((tm, D), F32)
        for (c0, n), ref in zip(pieces, (dcf_ref, dsc_ref, dq_ref, dk_ref, dv_ref, dz_ref, df_ref)):
            for s0 in range(0, n, 512):
                s1 = min(s0 + 512, n)
                d = ref[:, s0:s1].astype(BF)
                dh = dh + _dot_nt(d, w_vmem[:, c0 + s0:c0 + s1])
                dw_acc[:, c0 + s0:c0 + s1] += _dot(ht, d)
        dg_ref[...] += jnp.sum(dh * xhat, axis=0, keepdims=True)
        dhg = dh * gg
        dx_ref[...] = dy_ref[...] + r * (dhg - xhat * jnp.mean(dhg * xhat, axis=-1, keepdims=True))

        @pl.when(i == pl.num_programs(0) - 1)
        def _():
            rows_p = D // 4
            pieces_out = [(k, q) for k in range(4) for q in range(4)]
            copies = []
            for p, (k, q) in enumerate(pieces_out):
                slot = p % 2
                if p >= 2:
                    copies[p - 2].wait()

                def chunk(c, _, k=k, q=q, slot=slot):
                    r0 = pl.multiple_of(c * 64, 64)
                    stage[slot, pl.ds(r0, 64), :] = dw_acc[pl.ds(q * rows_p + r0, 64), NSH * k:NSH * (k + 1)]
                    return 0

                lax.fori_loop(0, rows_p // 64, chunk, 0)
                out = pltpu.make_async_copy(stage.at[slot], dw_hbm.at[k, q * rows_p:(q + 1) * rows_p, :],
                                            out_sems.at[slot])
                out.start()
                copies.append(out)
            copies[-2].wait()
            copies[-1].wait()

    row = lambda n: pl.BlockSpec((tm, n), lambda i: (i, 0))
    anyspec = pl.BlockSpec(memory_space=pl.ANY)
    return pl.pallas_call(
        body, grid=(t // tm,),
        in_specs=[row(D), pl.BlockSpec((1, D), lambda i: (0, 0)), row(D), anyspec,
                  row(768), row(1024), row(DA), row(DA), row(DA), row(DA), row(LANES)],
        out_specs=[row(D), pl.BlockSpec((1, D), lambda i: (0, 0)), anyspec],
        out_shape=[jax.ShapeDtypeStruct((t, D), F32), jax.ShapeDtypeStruct((1, D), F32),
                   jax.ShapeDtypeStruct((4, D, NSH), F32)],
        scratch_shapes=[pltpu.VMEM((D, NP), BF), pltpu.VMEM((D, NP), F32), pltpu.SemaphoreType.DMA,
                        pltpu.VMEM((2, D // 4, NSH), F32), pltpu.SemaphoreType.DMA((2,))],
        compiler_params=_cparams(("arbitrary",)), name=name)(x, g, dyres, w, dcf, dsc, dq, dk, dv, dz, df)


def _adamw_update(w_ref, g_ref, m_ref, v_ref, d_ref, m2_ref, v2_ref):
    gt = g_ref[...]
    m2 = ADAM_B1 * m_ref[...] + (1.0 - ADAM_B1) * gt
    v2 = ADAM_B2 * v_ref[...] + (1.0 - ADAM_B2) * (gt * gt)
    m_hat = m2 / (1.0 - ADAM_B1 ** ADAM_STEP)
    v_hat = v2 / (1.0 - ADAM_B2 ** ADAM_STEP)
    d_ref[...] = -ADAM_LR * (m_hat / (jnp.sqrt(v_hat) + ADAM_EPS) + ADAM_WD * w_ref[...])
    m2_ref[...] = m2
    v2_ref[...] = v2


def _adamw_many(ws, gs, ms, vs, name):
    n = len(ws)

    def body(*refs):
        ins, outs = refs[:4 * n], refs[4 * n:]
        for i in range(n):
            _adamw_update(ins[i], ins[n + i], ins[2 * n + i], ins[3 * n + i], *outs[3 * i:3 * i + 3])

    return pl.pallas_call(
        body, out_shape=[jax.ShapeDtypeStruct(w.shape, F32) for w in ws for _ in range(3)],
        name=name)(*ws, *gs, *ms, *vs)


def _adamw(w, g, m, v, tr, name):
    rows, cols = w.shape

    def body(w_ref, g_ref, m_ref, v_ref, d_ref, m2_ref, v2_ref, g_out_ref):
        _adamw_update(w_ref, g_ref, m_ref, v_ref, d_ref, m2_ref, v2_ref)
        g_out_ref[...] = g_ref[...]

    blk = pl.BlockSpec((tr, cols), lambda i: (i, 0))
    return pl.pallas_call(
        body, grid=(rows // tr,), in_specs=[blk] * 4, out_specs=[blk] * 4,
        out_shape=[jax.ShapeDtypeStruct((rows, cols), F32)] * 4,
        compiler_params=_cparams(("arbitrary",)), name=name)(w, g, m, v)


def _place():
    x, y, c = lax.axis_index("x"), lax.axis_index("y"), lax.axis_index("c")
    chips = [(1 - x, y), (x, 1 - y), (1 - x, 1 - y)]
    return x, y, c, chips


def _halves(c, rows, align):
    rh = rows // 2
    return pl.ds(pl.multiple_of(c * rh, align), rh), pl.ds(pl.multiple_of((1 - c) * rh, align), rh)


ANY = pl.BlockSpec(memory_space=pl.ANY)


class _Exchange:
    def __init__(self, operands, out_shape, sems, start, wait, relay=None):
        self.operands, self.out_shape, self.sems, self.start, self.wait = operands, out_shape, sems, start, wait
        self.relay = relay


class _StagedCopy:
    def __init__(self, src, dst, stage, sem):
        self.src, self.dst, self.stage, self.sem = src, dst, stage, sem

    def start(self):
        pltpu.make_async_copy(self.src, self.stage, self.sem).start()

    def relay(self):
        pltpu.make_async_copy(self.src, self.stage, self.sem).wait()
        pltpu.make_async_copy(self.stage, self.dst, self.sem).start()

    def finish(self):
        pltpu.make_async_copy(self.stage, self.dst, self.sem).wait()


def _run_exchange(ex, name):
    n_in, n_out = len(ex.operands), len(ex.out_shape)

    def body(*refs):
        ins, outs, sems = refs[:n_in], refs[n_in:n_in + n_out], refs[n_in + n_out:]
        ex.start(ins, outs, sems)
        if ex.relay:
            ex.relay(ins, outs, sems)
        ex.wait(ins, outs, sems)

    return pl.pallas_call(body, in_specs=[ANY] * n_in, out_specs=[ANY] * n_out, out_shape=ex.out_shape,
                          scratch_shapes=ex.sems, name=name)(*ex.operands)


def _host_call(body, grid, in_specs, out_specs, out_shape, scratch, operands, name, ex=None):
    sem = ("arbitrary",) * len(grid)
    if ex is None:
        outs = pl.pallas_call(body, grid=grid, in_specs=in_specs, out_specs=out_specs, out_shape=out_shape,
                              scratch_shapes=scratch, compiler_params=_cparams(sem), name=name)(*operands)
        return outs, None
    n_in, n_out, n_scr = len(in_specs), len(out_specs), len(scratch)
    xi, xo = len(ex.operands), len(ex.out_shape)

    def hosted(*refs):
        ins, xins = refs[:n_in], refs[n_in:n_in + xi]
        o0 = n_in + xi
        outs, xouts = refs[o0:o0 + n_out], refs[o0 + n_out:o0 + n_out + xo]
        s0 = o0 + n_out + xo
        scr, xsems = refs[s0:s0 + n_scr], refs[s0 + n_scr:]
        first = pl.program_id(0) == 0
        half = pl.program_id(0) == grid[0] // 2
        last = pl.program_id(0) == pl.num_programs(0) - 1
        for d in range(1, len(grid)):
            first = first & (pl.program_id(d) == 0)
            half = half & (pl.program_id(d) == 0)
            last = last & (pl.program_id(d) == pl.num_programs(d) - 1)

        @pl.when(first)
        def _():
            ex.start(xins, xouts, xsems)

        if ex.relay:
            @pl.when(half)
            def _():
                ex.relay(xins, xouts, xsems)

        body(*ins, *outs, *scr)

        @pl.when(last)
        def _():
            ex.wait(xins, xouts, xsems)

    res = pl.pallas_call(
        hosted, grid=grid, in_specs=list(in_specs) + [ANY] * xi, out_specs=list(out_specs) + [ANY] * xo,
        out_shape=list(out_shape) + list(ex.out_shape), scratch_shapes=list(scratch) + list(ex.sems),
        compiler_params=_cparams(sem), name=name)(*operands, *ex.operands)
    return res[:n_out], res[n_out:]


def _all_gather_chips(shards):
    n = len(shards)

    def copies(srcs, outs, sems):
        send_sems, recv_sems, local_sems, *stage = sems
        x, y, c, chips = _place()
        me = 2 * x + y
        halves = [_halves(c, s.shape[1], 16) for s in shards]
        local = [_StagedCopy(srcs[i], outs[i].at[:, me], stage[i], local_sems.at[i]) for i in range(n)]

        def ici(i, j, chip, k):
            mine = halves[i][0]
            return pltpu.make_async_remote_copy(
                src_ref=srcs[i].at[:, mine, :], dst_ref=outs[i].at[:, k, mine, :], send_sem=send_sems.at[6 * i + j],
                recv_sem=recv_sems.at[6 * i + j], device_id=(*chip, c), device_id_type=MESH)

        def d2d(i, j, k, half):
            return pltpu.make_async_remote_copy(
                src_ref=outs[i].at[:, k, half, :], dst_ref=outs[i].at[:, k, half, :], send_sem=send_sems.at[6 * i + 3 + j],
                recv_sem=recv_sems.at[6 * i + 3 + j], device_id=(x, y, 1 - c), device_id_type=MESH)

        return me, chips, halves, local, ici, d2d

    def start(srcs, outs, sems):
        me, chips, _, local, ici, _ = copies(srcs, outs, sems)
        for cp in local:
            cp.start()
        for i in range(n):
            for j, chip in enumerate(chips):
                ici(i, j, chip, me).start()

    def relay(srcs, outs, sems):
        me, chips, halves, local, ici, d2d = copies(srcs, outs, sems)
        for cp in local:
            cp.relay()
        for i in range(n):
            for j, (px, py) in enumerate(chips):
                k = 2 * px + py
                ici(i, j, (px, py), k).wait_recv()
                d2d(i, j, k, halves[i][0]).start()

    def wait(srcs, outs, sems):
        me, chips, halves, local, ici, d2d = copies(srcs, outs, sems)
        for i in range(n):
            for j, (px, py) in enumerate(chips):
                d2d(i, j, 2 * px + py, halves[i][1]).wait_recv()
        for i in range(n):
            for j, (px, py) in enumerate(chips):
                ici(i, j, (px, py), me).wait_send()
                d2d(i, j, 2 * px + py, halves[i][0]).wait_send()
        for cp in local:
            cp.finish()

    return _Exchange(
        list(shards), [jax.ShapeDtypeStruct((s.shape[0], 4) + s.shape[1:], s.dtype) for s in shards],
        [pltpu.SemaphoreType.DMA((6 * n,)), pltpu.SemaphoreType.DMA((6 * n,)), pltpu.SemaphoreType.DMA((n,))]
        + [pltpu.VMEM(s.shape, s.dtype) for s in shards], start, wait, relay)


def _rs_swap_halves(gs):
    n = len(gs)

    def copies(srcs, gots, sems):
        send_sems, recv_sems = sems
        x, y, c, _ = _place()
        return [pltpu.make_async_remote_copy(
            src_ref=srcs[i].at[:, :, _halves(c, gs[i].shape[2], 8)[1], :], dst_ref=gots[i], send_sem=send_sems.at[i],
            recv_sem=recv_sems.at[i], device_id=(x, y, 1 - c), device_id_type=MESH) for i in range(n)]

    def start(srcs, gots, sems):
        for cp in copies(srcs, gots, sems):
            cp.start()

    def wait(srcs, gots, sems):
        for cp in copies(srcs, gots, sems):
            cp.wait()

    return _Exchange(
        list(gs), [jax.ShapeDtypeStruct(g.shape[:2] + (g.shape[2] // 2, g.shape[3]), g.dtype) for g in gs],
        [pltpu.SemaphoreType.DMA((n,)), pltpu.SemaphoreType.DMA((n,))], start, wait)


def _rs_add_pair(g, got, cidx, tb, out_dtype, name):
    nl, _, rh, cols = got.shape
    nb = rh // tb

    def body(c_ref, g_ref, o_ref, s_ref):
        s_ref[...] = (g_ref[...] + o_ref[...]).astype(out_dtype)

    blk = lambda half: pl.BlockSpec((None, None, tb, cols), (lambda l, k, i, c: (l, k, c[0] * nb + i, 0)) if half
                                    else (lambda l, k, i, c: (l, k, i, 0)))
    return pl.pallas_call(
        body,
        grid_spec=pltpu.PrefetchScalarGridSpec(num_scalar_prefetch=1, grid=(nl, 4, nb),
                                               in_specs=[blk(True), blk(False)], out_specs=blk(False)),
        out_shape=jax.ShapeDtypeStruct(got.shape, out_dtype),
        compiler_params=_cparams(("arbitrary", "arbitrary", "arbitrary")), name=name)(cidx, g, got)


def _rs_exchange_chips(pairs):
    n = len(pairs)

    def copies(srcs, gots, sems):
        send_sems, recv_sems, local_sems, *stage = sems
        x, y, c, chips = _place()
        me = 2 * x + y
        local = [_StagedCopy(srcs[i].at[:, me], gots[i].at[:, me], stage[i], local_sems.at[i]) for i in range(n)]

        def ici(i, j, chip, frm, to):
            return pltpu.make_async_remote_copy(
                src_ref=srcs[i].at[:, to], dst_ref=gots[i].at[:, frm], send_sem=send_sems.at[3 * i + j],
                recv_sem=recv_sems.at[3 * i + j], device_id=(*chip, c), device_id_type=MESH)

        sent = [ici(i, j, (px, py), me, 2 * px + py) for i in range(n) for j, (px, py) in enumerate(chips)]
        recvd = [ici(i, j, (px, py), 2 * px + py, me) for i in range(n) for j, (px, py) in enumerate(chips)]
        return local, sent, recvd

    def start(srcs, gots, sems):
        local, sent, _ = copies(srcs, gots, sems)
        for cp in local + sent:
            cp.start()

    def relay(srcs, gots, sems):
        for cp in copies(srcs, gots, sems)[0]:
            cp.relay()

    def wait(srcs, gots, sems):
        local, sent, recvd = copies(srcs, gots, sems)
        for cp in recvd:
            cp.wait_recv()
        for cp in sent:
            cp.wait_send()
        for cp in local:
            cp.finish()

    return _Exchange(
        list(pairs), [jax.ShapeDtypeStruct(p.shape, p.dtype) for p in pairs],
        [pltpu.SemaphoreType.DMA((3 * n,)), pltpu.SemaphoreType.DMA((3 * n,)), pltpu.SemaphoreType.DMA((n,))]
        + [pltpu.VMEM((p.shape[0],) + p.shape[2:], p.dtype) for p in pairs], start, wait, relay)


def _rs_add_chips(got, tb, name):
    nl, _, rh, cols = got.shape

    def body(g_ref, s_ref):
        s_ref[...] = ((g_ref[0].astype(F32) + g_ref[1].astype(F32)) + g_ref[2].astype(F32)) + g_ref[3].astype(F32)

    return pl.pallas_call(
        body, grid=(nl, rh // tb),
        in_specs=[pl.BlockSpec((None, 4, tb, cols), lambda l, i: (l, 0, i, 0))],
        out_specs=pl.BlockSpec((None, tb, cols), lambda l, i: (l, i, 0)),
        out_shape=jax.ShapeDtypeStruct((nl, rh, cols), F32),
        compiler_params=_cparams(("arbitrary", "arbitrary")), name=name)(got)


def _rs_join_halves(groups):
    totals = [t for grp in groups for t in grp]
    where = [(gi, li) for gi, grp in enumerate(groups) for li in range(len(grp))]
    n = len(totals)

    def copies(srcs, outs, sems):
        send_sems, recv_sems, local_sems, *stage = sems
        x, y, c, _ = _place()
        local, send, recv = [], [], []
        for i, (gi, li) in enumerate(where):
            mine, other = _halves(c, 2 * totals[i].shape[1], 8)
            local.append(_StagedCopy(srcs[i], outs[gi].at[li:li + 1, mine, :], stage[i], local_sems.at[i]))
            for lst, rows in ((send, mine), (recv, other)):
                lst.append(pltpu.make_async_remote_copy(
                    src_ref=srcs[i], dst_ref=outs[gi].at[li:li + 1, rows, :], send_sem=send_sems.at[i],
                    recv_sem=recv_sems.at[i], device_id=(x, y, 1 - c), device_id_type=MESH))
        return local, send, recv

    def start(srcs, outs, sems):
        local, send, _ = copies(srcs, outs, sems)
        for cp in local + send:
            cp.start()

    def relay(srcs, outs, sems):
        for cp in copies(srcs, outs, sems)[0]:
            cp.relay()

    def wait(srcs, outs, sems):
        local, send, recv = copies(srcs, outs, sems)
        for cp in recv:
            cp.wait_recv()
        for cp in send:
            cp.wait_send()
        for cp in local:
            cp.finish()

    return _Exchange(
        totals, [jax.ShapeDtypeStruct((len(grp), 2 * grp[0].shape[1], grp[0].shape[2]), F32) for grp in groups],
        [pltpu.SemaphoreType.DMA((n,)), pltpu.SemaphoreType.DMA((n,)), pltpu.SemaphoreType.DMA((n,))]
        + [pltpu.VMEM(t.shape, F32) for t in totals], start, wait, relay)


def _rows(a):
    return a.reshape(-1, LANES)


def _pad_rows(a, mult):
    r = (-a.shape[0]) % mult
    return a if r == 0 else jnp.concatenate([a, jnp.zeros((r, a.shape[1]), a.dtype)], axis=0)


def kernel(x, norm_g, w_in, b_f, cf_dw, cf_dw_b, cf_ln_g, cf_ln_b, cf_pw, sc_dw, q_norm_g, k_norm_g, w_out, loss_target, m_norm_g, m_w_in, m_b_f, m_cf_dw, m_cf_dw_b, m_cf_ln_g, m_cf_ln_b, m_cf_pw, m_sc_dw, m_q_norm_g, m_k_norm_g, m_w_out, v_norm_g, v_w_in, v_b_f, v_cf_dw, v_cf_dw_b, v_cf_ln_g, v_cf_ln_b, v_cf_pw, v_sc_dw, v_q_norm_g, v_k_norm_g, v_w_out):
    bsz, seq, _ = x.shape
    t = bsz * seq

    taps = jnp.concatenate([cf_dw.reshape(-1), sc_dw.reshape(-1)])
    taps_bf = lax.bitcast_convert_type(taps, jnp.bfloat16).reshape(-1, LANES)
    n_pw, n_taps = 2 * 64 * DC // LANES, taps_bf.shape[0]
    small_w = _pad_rows(jnp.concatenate([_rows(cf_pw.astype(BF)), taps_bf], axis=0), 32)[None]
    win_b, wout_b = w_in.astype(BF), w_out.astype(BF)
    win0_g, small_g = _run_exchange(_all_gather_chips([win_b[0:1], small_w]), "all_gather_first")
    gather_rest = _all_gather_chips([win_b[1:DEPTH], wout_b])
    w_in_full = [_assemble_w_in(win0_g[0], "assemble_w_in_0")] + [None] * (DEPTH - 1)
    w_out_full = None
    cf_pw_full = jnp.concatenate([small_g[0, k, 0:n_pw].reshape(DEPTH, 64, DC) for k in range(4)], axis=1)
    taps_all = [lax.bitcast_convert_type(small_g[0, k, n_pw:n_pw + n_taps].reshape(-1, 2), F32) for k in range(4)]
    n_cfdw = DEPTH * CFW * 64
    cf_dw_full = jnp.concatenate([tk[:n_cfdw].reshape(DEPTH, CFW, 64) for tk in taps_all], axis=-1)
    sc_dw_full = jnp.concatenate([tk[n_cfdw:].reshape(DEPTH, SCW, 64) for tk in taps_all], axis=-1)

    bf_pad = jnp.pad(b_f, ((0, 0), (0, LANES - NH)))

    xs = [x.reshape(t, D)]
    saved = []
    dy = loss_part = None
    for l in range(DEPTH):
        xl = xs[-1]
        gq, gk = q_norm_g[l].reshape(1, DA), k_norm_g[l].reshape(1, DA)
        pcf, psc, pq, pk, vb, pz, pf = _fwd_inproj(xl, norm_g[l][None], w_in_full[l], f"fwd_inproj_{l}")
        b3 = lambda a: a.reshape(bsz, seq, a.shape[-1])
        ycs = _fwd_conv(b3(pcf), b3(psc), cf_dw_full[l], cf_dw_b[l][None], cf_ln_g[l][None], cf_ln_b[l][None],
                        cf_pw_full[l], sc_dw_full[l], f"fwd_conv_{l}")
        qs, kn, cq, crow8 = _fwd_attn_prep(b3(pq), b3(pk), b3(pf), gq, gk, bf_pad[l][None], f"fwd_attn_prep_{l}")
        crow = crow8.reshape(bsz, NH // 2, 2, seq)
        (o, ohp, yatt, lse), landed = _fwd_attn(qs, kn, b3(vb), cq, crow, b3(pz), f"fwd_attn_{l}",
                                                gather_rest if l == 0 else None)
        if l == 0:
            win_rest_g, wout_g = landed
            w_in_full[1:] = [_assemble_w_in(win_rest_g[i], f"assemble_w_in_{i + 1}") for i in range(DEPTH - 1)]
            w_out_full = wout_g.reshape(DEPTH, D, D)
        ycs2, yatt2 = ycs.reshape(t, 512), yatt.reshape(t, DA)
        if l + 1 < DEPTH:
            xs.append(_fwd_outproj(xl, ycs2, yatt2, w_out_full[l], f"fwd_outproj_{l}"))
        else:
            dy, loss_part = _fwd_outproj_loss(xl, ycs2, yatt2, w_out_full[l], loss_target.reshape(t, D),
                                              f"fwd_outproj_loss_{l}")
        saved.append((pcf, psc, pq, pk, vb, pz, pf, ycs2, yatt2, o, ohp, qs, kn, cq, crow, lse))

    grads = [None] * DEPTH
    reduced = [None] * DEPTH
    cidx = lax.axis_index("c").astype(jnp.int32).reshape(1)
    big_tiles = [256, 128]

    def big_packs(gl):
        return [gl["w_in"][None], gl["w_out"].reshape(1, 4, 256, D)]

    def add_pairs(packs, gots, tiles, dtypes, tag):
        return [_rs_add_pair(p, got, cidx, tb, dt, f"rs_add_pair_{tag}_{i}")
                for i, (p, got, tb, dt) in enumerate(zip(packs, gots, tiles, dtypes))]

    def add_chips(gots2, tiles, tag):
        return [_rs_add_chips(got, tb, f"rs_add_chips_{tag}_{i}") for i, (got, tb) in enumerate(zip(gots2, tiles))]

    for l in reversed(range(DEPTH)):
        pcf, psc, pq, pk, vb, pz, pf, ycs2, yatt2, o, ohp, qs, kn, cq, crow, lse = saved[l]
        gq, gk = q_norm_g[l].reshape(1, DA), k_norm_g[l].reshape(1, DA)
        b3 = lambda a: a.reshape(bsz, seq, a.shape[-1])
        f2 = lambda a: a.reshape(t, a.shape[-1])
        packs = big_packs(grads[l + 1]) if l + 1 < DEPTH else None
        (dycs, do, dz, dl, d_wout), gots = _bwd_outproj(
            dy, ycs2, yatt2, f2(o), f2(ohp), pz, w_out_full[l], f"bwd_outproj_{l}",
            _rs_swap_halves(packs) if packs else None)
        pairs = add_pairs(packs, gots, big_tiles, [BF, BF], l + 1) if packs else None
        (dqs, dkn, dv, dcrow), gots2 = _bwd_attn(qs, kn, b3(vb), b3(do), cq, lse, b3(dl), crow, f"bwd_attn_{l}",
                                                 _rs_exchange_chips(pairs) if packs else None)
        if packs:
            reduced[l + 1] = add_chips(gots2, big_tiles, l + 1)
        (dpq, dpk, dpf, d_gq, d_gk, d_bf), _ = _bwd_attn_post(
            b3(pq), b3(pk), dqs, dkn, b3(pf), dcrow.reshape(bsz, NH, seq), gq, gk, bf_pad[l][None],
            f"bwd_attn_post_{l}")
        dcf, dsc, d_cfdw, d_cfb, d_lng, d_lnb, d_pw, d_scdw = _bwd_conv(
            b3(pcf), b3(psc), b3(dycs), cf_dw_full[l], cf_dw_b[l][None], cf_ln_g[l][None], cf_ln_b[l][None],
            cf_pw_full[l], sc_dw_full[l], f"bwd_conv_{l}")
        dy, d_ng, d_win = _bwd_inproj(xs[l], norm_g[l][None], dy, w_in_full[l], f2(dcf), f2(dsc), f2(dpq), f2(dpk),
                                      f2(dv), dz, f2(dpf), f"bwd_inproj_{l}")
        grads[l] = dict(norm_g=d_ng[0], w_in=d_win, b_f=d_bf[0, :NH], cf_dw=d_cfdw, cf_dw_b=d_cfb[0],
                        cf_ln_g=d_lng[0], cf_ln_b=d_lnb[0], cf_pw=d_pw, sc_dw=d_scdw,
                        q_norm_g=d_gq.reshape(NH, HD), k_norm_g=d_gk.reshape(NH, HD), w_out=d_wout)
    grad_x = dy.reshape(bsz, seq, D)
    gw = {n: jnp.stack([grads[l][n] for l in range(DEPTH)]) for n in grads[0] if n not in ("w_in", "w_out")}

    rep_names = ("norm_g", "b_f", "cf_dw_b", "cf_ln_g", "cf_ln_b", "q_norm_g", "k_norm_g")
    rep = jnp.concatenate([jnp.pad(gw[n].reshape(-1), (0, (-gw[n].size) % LANES)) for n in rep_names]
                          + [jnp.pad(loss_part.reshape(-1), (0, LANES - 1))]).reshape(-1, LANES)
    blocks = []
    for k in range(4):
        small = jnp.concatenate([gw["cf_dw"][:, :, 64 * k:64 * (k + 1)].reshape(-1),
                                 gw["sc_dw"][:, :, 64 * k:64 * (k + 1)].reshape(-1)]).reshape(-1, LANES)
        blocks.append(_pad_rows(jnp.concatenate([_rows(gw["cf_pw"][:, 64 * k:64 * (k + 1), :]), small, rep], axis=0), 16))
    g_small = jnp.stack(blocks)[None]
    packs = big_packs(grads[0]) + [g_small]
    tiles = big_tiles + [g_small.shape[2] // 2]
    gots = _run_exchange(_rs_swap_halves(packs), "rs_swap_halves_last")
    pairs = add_pairs(packs, gots, tiles, [BF, BF, F32], 0)
    gots2 = _run_exchange(_rs_exchange_chips(pairs), "rs_exchange_chips_last")
    tot_win0, tot_wout0, tot_small = add_chips(gots2, tiles, 0)
    reduced[0] = (tot_win0, tot_wout0)
    red_win, red_wout, red_small = _run_exchange(
        _rs_join_halves([[reduced[l][0] for l in range(DEPTH)], [reduced[l][1] for l in range(DEPTH)], [tot_small]]),
        "rs_join_halves")

    n_small = (DEPTH * CFW * 64 + DEPTH * SCW * 64) // LANES
    red = red_small[0]
    pos = [0]

    def take(nrows):
        pos[0] += nrows
        return red[pos[0] - nrows:pos[0]]

    g = {}
    g["w_in"] = red_win
    g["w_out"] = red_wout
    g["cf_pw"] = take(n_pw).reshape(DEPTH, 64, DC)
    small = take(n_small).reshape(-1)
    g["cf_dw"] = small[:n_cfdw].reshape(DEPTH, CFW, 64)
    g["sc_dw"] = small[n_cfdw:].reshape(DEPTH, SCW, 64)
    shapes = dict(norm_g=norm_g.shape, b_f=b_f.shape, cf_dw_b=cf_dw_b.shape, cf_ln_g=cf_ln_g.shape,
                  cf_ln_b=cf_ln_b.shape, q_norm_g=q_norm_g.shape, k_norm_g=k_norm_g.shape)
    for n in rep_names:
        size = 1
        for s in shapes[n]:
            size *= s
        g[n] = take(-(-size // LANES)).reshape(-1)[:size].reshape(shapes[n])
    loss = take(1)[0, 0]

    order = ("norm_g", "w_in", "b_f", "cf_dw", "cf_dw_b", "cf_ln_g", "cf_ln_b", "cf_pw", "sc_dw", "q_norm_g",
             "k_norm_g", "w_out")
    weights = dict(norm_g=norm_g, w_in=w_in, b_f=b_f, cf_dw=cf_dw, cf_dw_b=cf_dw_b, cf_ln_g=cf_ln_g, cf_ln_b=cf_ln_b,
                   cf_pw=cf_pw, sc_dw=sc_dw, q_norm_g=q_norm_g, k_norm_g=k_norm_g, w_out=w_out)
    ms = dict(norm_g=m_norm_g, w_in=m_w_in, b_f=m_b_f, cf_dw=m_cf_dw, cf_dw_b=m_cf_dw_b, cf_ln_g=m_cf_ln_g,
              cf_ln_b=m_cf_ln_b, cf_pw=m_cf_pw, sc_dw=m_sc_dw, q_norm_g=m_q_norm_g, k_norm_g=m_k_norm_g, w_out=m_w_out)
    vs = dict(norm_g=v_norm_g, w_in=v_w_in, b_f=v_b_f, cf_dw=v_cf_dw, cf_dw_b=v_cf_dw_b, cf_ln_g=v_cf_ln_g,
              cf_ln_b=v_cf_ln_b, cf_pw=v_cf_pw, sc_dw=v_sc_dw, q_norm_g=v_q_norm_g, k_norm_g=v_k_norm_g, w_out=v_w_out)
    delta, new_m, new_v = {}, {}, {}
    two_d = lambda a: a.reshape(-1, a.shape[-1])
    for n in ("w_in", "w_out"):
        outs = _adamw(two_d(weights[n]), two_d(g[n]), two_d(ms[n]), two_d(vs[n]), 256, f"adamw_{n}")
        delta[n], new_m[n], new_v[n], g[n] = (a.reshape(weights[n].shape) for a in outs)
    small_names = [n for n in order if n not in ("w_in", "w_out")]
    outs = _adamw_many([weights[n] for n in small_names], [g[n] for n in small_names],
                       [ms[n] for n in small_names], [vs[n] for n in small_names], "adamw_small")
    for i, n in enumerate(small_names):
        delta[n], new_m[n], new_v[n] = outs[3 * i:3 * i + 3]

    return (loss, grad_x, *[g[n] for n in order], *[delta[n] for n in order], *[new_m[n] for n in order],
            *[new_v[n] for n in order])
```

```python
import jax
import jax.numpy as jnp
from jax import lax
from jax.experimental import pallas as pl
from jax.experimental.pallas import tpu as pltpu

F32 = jnp.float32
BF = jnp.bfloat16
MESH = pl.DeviceIdType.MESH

DEPTH = 2
D = 1024
DC = 256
DA = 512
NH = 8
HD = 64
CFW = 31
SCW = 3
N_IN = 3848
NP = 3968
NSH = N_IN // 4
HALO = 32
ATT_ROWS = 32
BWD_KEYS = 512
FWD_ROWS = 256
SUBLANES = 8
CONV_ROWS = 64
TAP_GROUP = 16
EPS = 1e-6
LANES = 128
VMEM_LIMIT = 56 * 1024 * 1024

C_CF, C_SC, C_Q, C_K, C_V, C_Z, C_F = 0, 768, 1792, 2304, 2816, 3328, 3840

ADAM_LR = 0.001
ADAM_B1 = 0.9
ADAM_B2 = 0.999
ADAM_EPS = 1e-08
ADAM_WD = 0.01
ADAM_STEP = 10


def _cparams(sem=None):
    return pltpu.CompilerParams(dimension_semantics=sem, vmem_limit_bytes=VMEM_LIMIT)


def _dot(a, b):
    return jnp.dot(a, b, preferred_element_type=F32)


def _dot_nt(a, b):
    return lax.dot_general(a, b, (((1,), (1,)), ((), ())), preferred_element_type=F32)


def _dot_tn(a, b):
    return lax.dot_general(a, b, (((0,), (0,)), ((), ())), preferred_element_type=F32)


def _split3(x):
    hi = x.astype(BF)
    r1 = x - hi.astype(F32)
    mid = r1.astype(BF)
    lo = (r1 - mid.astype(F32)).astype(BF)
    return hi, mid, lo


def _dot_exact(a_bf, x):
    hi, mid, lo = _split3(x)
    return _dot(a_bf, hi) + _dot(a_bf, mid) + _dot(a_bf, lo)


def _sigmoid(x):
    return 1.0 / (1.0 + jnp.exp(-x))


def _seg_sum64(x, exact=False):
    i = lax.broadcasted_iota(jnp.int32, (LANES, LANES), 0)
    j = lax.broadcasted_iota(jnp.int32, (LANES, LANES), 1)
    g = ((i >= HD) == (j >= HD)).astype(BF)
    hi, mid, lo = _split3(x)
    two = _dot(hi, g) + _dot(mid, g)
    return two + _dot(lo, g) if exact else two


def _assemble_w_in(shards, name):
    tr = 256

    def body(s_ref, o_ref):
        for k in range(4):
            o_ref[:, NSH * k:NSH * (k + 1)] = s_ref[k]
        o_ref[:, N_IN:NP] = jnp.zeros((tr, NP - N_IN), shards.dtype)

    return pl.pallas_call(
        body, grid=(D // tr,), in_specs=[pl.BlockSpec((4, tr, NSH), lambda i: (0, i, 0))],
        out_specs=pl.BlockSpec((tr, NP), lambda i: (i, 0)), out_shape=jax.ShapeDtypeStruct((D, NP), shards.dtype),
        compiler_params=_cparams(("arbitrary",)), name=name)(shards)


def _fwd_inproj(x, g, w, name):
    t = x.shape[0]
    tm = 512

    def body(x_ref, g_ref, w_ref, cf_ref, sc_ref, q_ref, k_ref, v_ref, z_ref, f_ref):
        xt = x_ref[...]
        r = lax.rsqrt(jnp.mean(xt * xt, axis=-1, keepdims=True) + EPS)
        h = ((xt * r) * g_ref[...]).astype(BF)
        cf_ref[:, 0:512] = _dot(h, w_ref[:, 0:512])
        cf_ref[:, 512:768] = _dot(h, w_ref[:, 512:768])
        sc_ref[:, 0:512] = _dot(h, w_ref[:, C_SC:C_SC + 512])
        sc_ref[:, 512:1024] = _dot(h, w_ref[:, C_SC + 512:C_Q])
        q_ref[...] = _dot(h, w_ref[:, C_Q:C_K])
        k_ref[...] = _dot(h, w_ref[:, C_K:C_V])
        v_ref[...] = _dot(h, w_ref[:, C_V:C_Z]).astype(BF)
        z_ref[...] = _dot(h, w_ref[:, C_Z:C_F])
        f_ref[...] = _dot(h, w_ref[:, C_F:NP])

    row = lambda n: pl.BlockSpec((tm, n), lambda i: (i, 0))
    return pl.pallas_call(
        body, grid=(t // tm,),
        in_specs=[row(D), pl.BlockSpec((1, D), lambda i: (0, 0)), pl.BlockSpec((D, NP), lambda i: (0, 0))],
        out_specs=[row(768), row(1024), row(DA), row(DA), row(DA), row(DA), row(LANES)],
        out_shape=[jax.ShapeDtypeStruct((t, 768), F32), jax.ShapeDtypeStruct((t, 1024), F32),
                   jax.ShapeDtypeStruct((t, DA), F32), jax.ShapeDtypeStruct((t, DA), F32),
                   jax.ShapeDtypeStruct((t, DA), BF), jax.ShapeDtypeStruct((t, DA), F32),
                   jax.ShapeDtypeStruct((t, LANES), F32)],
        compiler_params=_cparams(("arbitrary",)), name=name)(x, g, w)


def _shift_copies(buf, sbuf, shifts):
    n = buf.shape[0]
    for sh in shifts:
        rows = n if sh == 0 else n - SUBLANES
        sbuf[sh, 0:rows, :] = buf[pl.ds(sh, rows), :]


def _tap_rows(sbuf, off, r0):
    sh = off % SUBLANES
    return sbuf[sh, pl.ds(pl.multiple_of(r0 + (off - sh), SUBLANES), CONV_ROWS), :]


def _tap_conv(sbuf, w_ref, offsets, dst, ts, bias):
    def chunk(c, _):
        r0 = pl.multiple_of(c * CONV_ROWS, CONV_ROWS)
        acc = jnp.zeros((CONV_ROWS, DC), F32) + bias
        for k, off in enumerate(offsets):
            acc = acc + w_ref[k:k + 1, :] * _tap_rows(sbuf, off, r0)
        dst[pl.ds(r0, CONV_ROWS), :] = acc
        return 0

    lax.fori_loop(0, ts // CONV_ROWS, chunk, 0)


def _tap_sums(a_buf, sbuf, offsets, ts):
    outs = []
    for g0 in range(0, len(offsets), TAP_GROUP):
        group = offsets[g0:g0 + TAP_GROUP]

        def chunk(c, accs, group=group):
            r0 = pl.multiple_of(c * CONV_ROWS, CONV_ROWS)
            a = a_buf[pl.ds(r0, CONV_ROWS), :]
            return tuple(acc + (a * _tap_rows(sbuf, off, r0)).reshape(CONV_ROWS // SUBLANES, SUBLANES, DC).sum(axis=0)
                         for acc, off in zip(accs, group))

        accs = lax.fori_loop(0, ts // CONV_ROWS, chunk, tuple(jnp.zeros((SUBLANES, DC), F32) for _ in group))
        outs += [jnp.sum(acc, axis=0, keepdims=True) for acc in accs]
    return outs


CF_TAPS = tuple(HALO - (CFW - 1) + k for k in range(CFW))
SC_TAPS = tuple(HALO - (SCW - 1) + k for k in range(SCW))
CF_TAPS_T = tuple(CFW - 1 - k for k in range(CFW))
SC_TAPS_T = tuple(SCW - 1 - k for k in range(SCW))


def _conformer_fwd(a, g, ha, hg, first, ubuf, usbuf, u1buf, dw_ref, bias, lng, lnb):
    ts = a.shape[0]
    u0 = a * _sigmoid(g)
    ubuf[0:HALO, :] = jnp.where(first, 0.0, ha * _sigmoid(hg))
    ubuf[HALO:HALO + ts, :] = u0
    _shift_copies(ubuf, usbuf, range(SUBLANES))
    _tap_conv(usbuf, dw_ref, CF_TAPS, u1buf, ts, bias)
    u1 = u1buf[...]
    mu = jnp.mean(u1, axis=-1, keepdims=True)
    xc = u1 - mu
    rstd = lax.rsqrt(jnp.mean(xc * xc, axis=-1, keepdims=True) + EPS)
    n = xc * rstd
    u2 = n * lng + lnb
    s2 = _sigmoid(u2)
    u3 = u2 * s2
    return u0, n, rstd, u2, s2, u3


def _shortconv_fwd(c, xs, hc, hx, first, mbuf, msbuf, cvbuf, dw_ref):
    ts = c.shape[0]
    mbuf[0:HALO, :] = jnp.where(first, 0.0, hc * hx)
    mbuf[HALO:HALO + ts, :] = c * xs
    _shift_copies(mbuf, msbuf, sorted({off % SUBLANES for off in SC_TAPS}))
    _tap_conv(msbuf, dw_ref, SC_TAPS, cvbuf, ts, 0.0)
    return cvbuf[...]


def _conv_specs(bsz, seq, ts, order):
    nt = seq // ts
    tile = (lambda i: i) if order > 0 else (lambda i: nt - 1 - i)
    hrow = lambda i: jnp.maximum(tile(i) * (ts // HALO) - 1, 0)
    cur = lambda n: pl.BlockSpec((None, ts, n), lambda b, i: (b, tile(i), 0))
    halo = lambda j: pl.BlockSpec((None, HALO, DC), lambda b, i: (b, hrow(i), j))
    full = lambda r, c: pl.BlockSpec((r, c), lambda b, i: (0, 0))
    return nt, tile, cur, halo, full


def _fwd_conv(pcf, psc, cf_dw, cf_dw_b, ln_g, ln_b, cf_pw, sc_dw, name):
    bsz, seq, _ = pcf.shape
    ts = 512
    nt, tile, cur, halo, full = _conv_specs(bsz, seq, ts, +1)

    def body(cf_ref, ha_ref, hg_ref, sc_ref, hc_ref, hx_ref, dw_ref, b_ref, lg_ref, lb_ref, pw_ref, sdw_ref,
             y_ref, ubuf, sbuf, obuf):
        first = pl.program_id(1) == 0
        _, _, _, _, _, u3 = _conformer_fwd(cf_ref[:, 0:256], cf_ref[:, 256:512], ha_ref[...], hg_ref[...], first,
                                           ubuf, sbuf, obuf, dw_ref, b_ref[...], lg_ref[...], lb_ref[...])
        z = cf_ref[:, 512:768]
        y_ref[:, 0:256] = _dot(u3.astype(BF), pw_ref[...]) * (z * _sigmoid(z))
        cv = _shortconv_fwd(sc_ref[:, 256:512], sc_ref[:, 512:768], hc_ref[...], hx_ref[...], first,
                            ubuf, sbuf, obuf, sdw_ref)
        zs = sc_ref[:, 768:1024]
        y_ref[:, 256:512] = sc_ref[:, 0:256] * cv * (zs * _sigmoid(zs))

    return pl.pallas_call(
        body, grid=(bsz, nt),
        in_specs=[cur(768), halo(0), halo(1), cur(1024), halo(1), halo(2),
                  full(CFW, DC), full(1, DC), full(1, DC), full(1, DC), full(DC, DC), full(SCW, DC)],
        out_specs=pl.BlockSpec((None, ts, 512), lambda b, i: (b, i, 0)),
        out_shape=jax.ShapeDtypeStruct((bsz, seq, 512), F32),
        scratch_shapes=[pltpu.VMEM((HALO + ts, DC), F32), pltpu.VMEM((SUBLANES, HALO + ts, DC), F32),
                        pltpu.VMEM((ts, DC), F32)],
        compiler_params=_cparams(("arbitrary", "arbitrary")), name=name,
    )(pcf, pcf, pcf, psc, psc, psc, cf_dw, cf_dw_b, ln_g, ln_b, cf_pw, sc_dw)


def _head_rms(xb, gb):
    ms = _seg_sum64(xb * xb) * (1.0 / HD)
    r = lax.rsqrt(ms + EPS)
    xhat = xb * r
    return xhat, r, xhat * gb


def _fwd_attn_prep(pq, pk, pf, gq, gk, bf, name):
    bsz, seq, _ = pq.shape
    ts = 512
    nt = seq // ts

    def body(q_ref, k_ref, f_ref, gq_ref, gk_ref, bf_ref, qs_ref, kn_ref, cq_ref, crow_ref, carry):
        @pl.when(pl.program_id(1) == 0)
        def _():
            carry[...] = jnp.zeros_like(carry)

        for jb in range(DA // LANES):
            sl = slice(jb * LANES, (jb + 1) * LANES)
            _, _, qn = _head_rms(q_ref[:, sl], gq_ref[:, sl])
            qs_ref[:, sl] = (qn * (1.0 / 8.0)).astype(BF)
            _, _, kn = _head_rms(k_ref[:, sl], gk_ref[:, sl])
            kn_ref[:, sl] = kn.astype(BF)

        xf = f_ref[...] + bf_ref[...]
        lf = jnp.minimum(xf, 0.0) - jnp.log(1.0 + jnp.exp(-jnp.abs(xf)))
        ti = lax.broadcasted_iota(jnp.int32, (ts, ts), 0)
        si = lax.broadcasted_iota(jnp.int32, (ts, ts), 1)
        c = _dot_exact((si <= ti).astype(BF), lf) + carry[...]
        carry[...] = c[ts - 1:ts, :]
        hj = lax.broadcasted_iota(jnp.int32, (LANES, DA), 0)
        ll = lax.broadcasted_iota(jnp.int32, (LANES, DA), 1)
        dd = ll - hj * HD
        chi, cmid, clo = _split3(c)
        e = ((dd >= 0) & (dd < HD)).astype(BF)
        cq_ref[...] = _dot(chi, e) + _dot(cmid, e) + _dot(clo, e)
        hh = lax.broadcasted_iota(jnp.int32, (16, LANES), 0)
        jj = lax.broadcasted_iota(jnp.int32, (16, LANES), 1)
        sel = (hh == jj).astype(BF)
        cr = _dot_nt(sel, chi) + _dot_nt(sel, cmid) + _dot_nt(sel, clo)
        crow_ref[...] = cr[0:NH, :]

    tile = lambda n: pl.BlockSpec((None, ts, n), lambda b, i: (b, i, 0))
    vec = lambda n: pl.BlockSpec((1, n), lambda b, i: (0, 0))
    return pl.pallas_call(
        body, grid=(bsz, nt),
        in_specs=[tile(DA), tile(DA), tile(LANES), vec(DA), vec(DA), vec(LANES)],
        out_specs=[tile(DA), tile(DA), tile(DA), pl.BlockSpec((None, NH, ts), lambda b, i: (b, 0, i))],
        out_shape=[jax.ShapeDtypeStruct((bsz, seq, DA), BF), jax.ShapeDtypeStruct((bsz, seq, DA), BF),
                   jax.ShapeDtypeStruct((bsz, seq, DA), F32), jax.ShapeDtypeStruct((bsz, NH, seq), F32)],
        scratch_shapes=[pltpu.VMEM((1, LANES), F32)],
        compiler_params=_cparams(("arbitrary", "arbitrary")), name=name)(pq, pk, pf, gq, gk, bf)


def _tile_rows(i, rows):
    return pl.ds(i * rows, rows) if isinstance(i, int) else pl.ds(pl.multiple_of(i * rows, rows), rows)


def _causal_mask(tq, tk):
    r = lax.broadcasted_iota(jnp.int32, (tq, tk), 0)
    c = lax.broadcasted_iota(jnp.int32, (tq, tk), 1)
    return r - c


def _fwd_attn(qs, kn, vb, cq, crow, pz, name, ex=None):
    bsz, seq, _ = qs.shape
    tq, tk = 256, 256
    nq, nk = seq // tq, seq // tk
    npair = NH // 2

    def body(q_ref, k_ref, v_ref, cq_ref, cr_ref, z_ref, o_ref, ohp_ref, y_ref, lse_ref,
             s_buf, p_buf, acc_ref, accl_ref):
        qi = pl.program_id(2)
        head0 = lax.broadcasted_iota(jnp.int32, (1, LANES), 1) < HD
        q = q_ref[...]
        zq = jnp.zeros_like(q)
        qm = (jnp.where(head0, q, zq), jnp.where(head0, zq, q))
        ct = (cq_ref[:, 0:1], cq_ref[:, HD:HD + 1])

        tile = lambda j: _tile_rows(j, tk)

        def scores(j, slot):
            kb = k_ref[tile(j), :]
            for hh in range(2):
                s_buf[slot, hh] = _dot_nt(qm[hh], kb) + (ct[hh] - cr_ref[hh:hh + 1, tile(j)])

        def weighted_values(slot, j, al):
            vb_t = v_ref[tile(j), :]
            acc_ref[...] = al * acc_ref[...] + jnp.where(head0, _dot(p_buf[slot, 0, 0], vb_t),
                                                         _dot(p_buf[slot, 1, 0], vb_t))
            accl_ref[...] = al * accl_ref[...] + jnp.where(head0, _dot(p_buf[slot, 0, 1], vb_t),
                                                           _dot(p_buf[slot, 1, 1], vb_t))

        def softmax(slot, first_visible, m, l, masked):
            m_new, l_new, al_new = ([], []), ([], []), []
            for r in range(tq // FWD_ROWS):
                rows = slice(r * FWD_ROWS, (r + 1) * FWD_ROWS)
                visible = _causal_mask(FWD_ROWS, tk) >= first_visible - r * FWD_ROWS
                alphas = []
                for hh in range(2):
                    s = s_buf[slot, hh, rows, :]
                    if masked:
                        s = jnp.where(visible, s, -1e30)
                    m_old = m[hh][rows]
                    m2 = jnp.maximum(m_old, jnp.max(s, axis=-1, keepdims=True))
                    p = jnp.exp(s - m2)
                    alpha = jnp.exp(m_old - m2)
                    l_new[hh].append(alpha * l[hh][rows] + jnp.sum(p, axis=-1, keepdims=True))
                    m_new[hh].append(m2)
                    pb = p.astype(BF)
                    p_buf[slot, hh, 0, rows, :] = pb
                    p_buf[slot, hh, 1, rows, :] = (p - pb.astype(F32)).astype(BF)
                    alphas.append(alpha)
                al_new.append(jnp.where(head0, alphas[0], alphas[1]))
            cat = lambda parts: jnp.concatenate(parts, axis=0)
            return (cat(m_new[0]), cat(m_new[1])), (cat(l_new[0]), cat(l_new[1])), cat(al_new)

        clamp = lambda j: jnp.clip(j, 0, nk - 1)

        def step(j, slot, carry, masked):
            m, l, al = carry
            weighted_values(1 - slot, clamp(j - 1), al)
            m, l, al = softmax(slot, j * tk - qi * tq, m, l, masked)
            scores(clamp(j + 1), 1 - slot)
            return m, l, al

        scores(0, 0)
        p_buf[1] = jnp.zeros((2, 2, tq, tk), BF)
        acc_ref[...] = jnp.zeros((tq, LANES), F32)
        accl_ref[...] = jnp.zeros((tq, LANES), F32)
        neg = jnp.full((tq, 1), -1e30, F32)
        zcol = jnp.zeros((tq, 1), F32)
        full = (qi * (tq // tk)) // 2
        carry = lax.fori_loop(0, full, lambda t, cy: step(2 * t + 1, 1, step(2 * t, 0, cy, False), False),
                              ((neg, neg), (zcol, zcol), jnp.ones((tq, LANES), F32)))
        m, l, al = step(2 * full + 1, 1, step(2 * full, 0, carry, True), True)
        weighted_values(1, clamp(2 * full + 1), al)
        inv = jnp.where(head0, 1.0 / l[0], 1.0 / l[1])
        o = acc_ref[...] * inv
        z = z_ref[...]
        o_ref[...] = o
        ohp_ref[...] = (acc_ref[...] + accl_ref[...]) * inv
        y_ref[...] = o * (z * _sigmoid(z))
        lse_ref[...] = jnp.where(head0, m[0] + jnp.log(l[0]), m[1] + jnp.log(l[1]))

    qblk = pl.BlockSpec((None, tq, LANES), lambda b, h, i: (b, i, h))
    kvblk = pl.BlockSpec((None, seq, LANES), lambda b, h, i: (b, 0, h))
    return _host_call(
        body, (bsz, npair, nq),
        [qblk, kvblk, kvblk, qblk, pl.BlockSpec((None, None, 2, seq), lambda b, h, i: (b, h, 0, 0)), qblk],
        [qblk, qblk, qblk, qblk], [jax.ShapeDtypeStruct((bsz, seq, DA), F32)] * 4,
        [pltpu.VMEM((2, 2, tq, tk), F32), pltpu.VMEM((2, 2, 2, tq, tk), BF), pltpu.VMEM((tq, LANES), F32),
         pltpu.VMEM((tq, LANES), F32)],
        (qs, kn, vb, cq, crow, pz), name, ex)


def _fwd_outproj(x, ycs, yatt, wo, name):
    t = x.shape[0]
    tm = 512

    def body(x_ref, a_ref, b_ref, w_ref, o_ref):
        o_ref[...] = (x_ref[...] + _dot(a_ref[...].astype(BF), w_ref[0:512, :])
                      + _dot(b_ref[...].astype(BF), w_ref[512:1024, :]))

    row = lambda n: pl.BlockSpec((tm, n), lambda i: (i, 0))
    return pl.pallas_call(
        body, grid=(t // tm,),
        in_specs=[row(D), row(512), row(512), pl.BlockSpec((D, D), lambda i: (0, 0))],
        out_specs=row(D), out_shape=jax.ShapeDtypeStruct((t, D), F32),
        compiler_params=_cparams(("arbitrary",)), name=name)(x, ycs, yatt, wo)


def _fwd_outproj_loss(x, ycs, yatt, wo, target, name):
    t = x.shape[0]
    tm = 512

    def body(x_ref, a_ref, b_ref, w_ref, t_ref, dy_ref, loss_ref):
        @pl.when(pl.program_id(0) == 0)
        def _():
            loss_ref[...] = jnp.zeros_like(loss_ref)

        y = (x_ref[...] + _dot(a_ref[...].astype(BF), w_ref[0:512, :])
             + _dot(b_ref[...].astype(BF), w_ref[512:1024, :]))
        err = y - t_ref[...]
        dy_ref[...] = err * (1.0 / D)
        per_tok = jnp.mean(err * err, axis=-1, keepdims=True)
        loss_ref[...] += 0.5 * jnp.sum(per_tok, axis=0, keepdims=True)

    row = lambda n: pl.BlockSpec((tm, n), lambda i: (i, 0))
    return pl.pallas_call(
        body, grid=(t // tm,),
        in_specs=[row(D), row(512), row(512), pl.BlockSpec((D, D), lambda i: (0, 0)), row(D)],
        out_specs=[row(D), pl.BlockSpec((1, 1), lambda i: (0, 0))],
        out_shape=[jax.ShapeDtypeStruct((t, D), F32), jax.ShapeDtypeStruct((1, 1), F32)],
        compiler_params=_cparams(("arbitrary",)), name=name)(x, ycs, yatt, wo, target)


def _dsilu(x, s):
    return s * (1.0 + x * (1.0 - s))


def _bwd_outproj(dy, ycs, yatt, o, ohp, pz, wo, name, ex=None):
    t = dy.shape[0]
    tm = 512

    def body(dy_ref, a_ref, b_ref, o_ref, ohp_ref, z_ref, w_ref, dcs_ref, do_ref, dz_ref, dl_ref, dw_ref):
        @pl.when(pl.program_id(0) == 0)
        def _():
            dw_ref[...] = jnp.zeros_like(dw_ref)

        dyb = dy_ref[...].astype(BF)
        dw_ref[0:512, :] += _dot_tn(a_ref[...].astype(BF), dyb)
        dw_ref[512:1024, :] += _dot_tn(b_ref[...].astype(BF), dyb)
        dcs_ref[...] = _dot_nt(dyb, w_ref[0:512, :])
        dyatt = _dot_nt(dyb, w_ref[512:1024, :])
        z = z_ref[...]
        sz = _sigmoid(z)
        o_t = o_ref[...]
        dob = (dyatt * (z * sz)).astype(BF)
        do_ref[...] = dob
        dz_ref[...] = dyatt * o_t * _dsilu(z, sz)
        prod = dob.astype(F32) * ohp_ref[...]
        for jb in range(DA // LANES):
            sl = slice(jb * LANES, (jb + 1) * LANES)
            dl_ref[:, sl] = _seg_sum64(prod[:, sl], exact=True)

    row = lambda n: pl.BlockSpec((tm, n), lambda i: (i, 0))
    return _host_call(
        body, (t // tm,),
        [row(D), row(512), row(512), row(DA), row(DA), row(DA), pl.BlockSpec((D, D), lambda i: (0, 0))],
        [row(512), row(DA), row(DA), row(DA), pl.BlockSpec((D, D), lambda i: (0, 0))],
        [jax.ShapeDtypeStruct((t, 512), F32), jax.ShapeDtypeStruct((t, DA), BF),
         jax.ShapeDtypeStruct((t, DA), F32), jax.ShapeDtypeStruct((t, DA), F32),
         jax.ShapeDtypeStruct((D, D), F32)], [],
        (dy, ycs, yatt, o, ohp, pz, wo), name, ex)


def _bwd_attn(qs, kn, vb, do, cq, lse, dl, crow, name, ex=None):
    bsz, seq, _ = qs.shape
    tq, tk = 256, min(BWD_KEYS, seq)
    nq, nk = seq // tq, seq // tk
    rows_c = ATT_ROWS * tq // tk
    peel = tk == 2 * tq and nq % 2 == 0
    npair = NH // 2

    def body(q_ref, k_ref, v_ref, do_ref, cq_ref, lse_ref, dl_ref, cr_ref, dq_ref, dk_ref, dv_ref, dc_ref,
             qm, dom, s_buf, dp_buf, pd_buf):
        head0 = lax.broadcasted_iota(jnp.int32, (1, LANES), 1) < HD
        zb = jnp.zeros((seq, LANES), BF)
        qm[0] = jnp.where(head0, q_ref[...], zb)
        qm[1] = jnp.where(head0, zb, q_ref[...])
        dom[0] = jnp.where(head0, do_ref[...], zb)
        dom[1] = jnp.where(head0, zb, do_ref[...])
        dq_ref[...] = jnp.zeros_like(dq_ref)

        def kloop(kj, _):
            krows = pl.ds(pl.multiple_of(kj * tk, tk), tk)
            kb = k_ref[krows, :]
            vb_t = v_ref[krows, :]
            zk = jnp.zeros_like(kb)
            km = (jnp.where(head0, kb, zk), jnp.where(head0, zk, kb))

            tile = lambda i: _tile_rows(i, tq)

            def scores(i, slot):
                for hh in range(2):
                    c0 = hh * HD
                    s_buf[slot, hh] = (_dot_nt(qm[hh, tile(i), :], kb)
                                       + (cq_ref[tile(i), c0:c0 + 1] - cr_ref[hh:hh + 1, krows]))
                    dp_buf[slot, hh] = _dot_nt(dom[hh, tile(i), :], vb_t)

            def products(slot, i, dk, dv):
                dq = jnp.zeros((tq, LANES), F32)
                for hh in range(2):
                    dsb = pd_buf[slot, hh, 1]
                    dv = dv + _dot_tn(pd_buf[slot, hh, 0], dom[hh, tile(i), :])
                    dk = dk + _dot_tn(dsb, qm[hh, tile(i), :])
                    dq = dq + _dot(dsb, km[hh])
                dq_ref[tile(i), :] += dq
                return dk, dv

            clamp = lambda i: jnp.clip(i, 0, nq - 1)

            def qstep(i, slot, carry, masked=True):
                dk, dv, cs = carry
                dk, dv = products(1 - slot, clamp(i - 1), dk, dv)
                cs = list(cs)
                ic = clamp(i)
                first_visible = jnp.where(i < nq, kj * tk - i * tq, 2 * tk)
                for r in range(tq // rows_c):
                    rows = slice(r * rows_c, (r + 1) * rows_c)
                    qrows = pl.ds(pl.multiple_of(ic * tq + r * rows_c, rows_c), rows_c)
                    visible = _causal_mask(rows_c, tk) >= first_visible - r * rows_c
                    for hh in range(2):
                        c0 = hh * HD
                        s = s_buf[slot, hh, rows, :]
                        if masked:
                            s = jnp.where(visible, s, -1e30)
                        p = jnp.exp(s - lse_ref[qrows, c0:c0 + 1])
                        ds = p * (dp_buf[slot, hh, rows, :] - dl_ref[qrows, c0:c0 + 1])
                        pd_buf[slot, hh, 0, rows, :] = p.astype(BF)
                        pd_buf[slot, hh, 1, rows, :] = ds.astype(BF)
                        cs[hh] = cs[hh] + jnp.sum(ds, axis=0, keepdims=True)
                scores(clamp(i + 1), 1 - slot)
                return dk, dv, tuple(cs)

            i0 = kj * (tk // tq)
            scores(i0, 0)
            pd_buf[1] = jnp.zeros((2, 2, tq, tk), BF)
            zero = jnp.zeros((tk, LANES), F32)
            zrow = jnp.zeros((1, tk), F32)
            trips = (nq - i0 + 1) // 2
            carry = (zero, zero, (zrow, zrow))
            if peel:
                carry = qstep(i0 + 1, 1, qstep(i0, 0, carry))
                dk, dv, cs = lax.fori_loop(
                    1, trips, lambda t, cy: qstep(i0 + 2 * t + 1, 1, qstep(i0 + 2 * t, 0, cy, False), False), carry)
            else:
                dk, dv, cs = lax.fori_loop(
                    0, trips, lambda t, cy: qstep(i0 + 2 * t + 1, 1, qstep(i0 + 2 * t, 0, cy)), carry)
            dk, dv = products(1, clamp(i0 + 2 * trips - 1), dk, dv)
            dk_ref[krows, :] = dk
            dv_ref[krows, :] = dv
            dc_ref[0:1, krows] = -cs[0]
            dc_ref[1:2, krows] = -cs[1]
            return 0

        lax.fori_loop(0, nk, kloop, 0)

    blk = pl.BlockSpec((None, seq, LANES), lambda b, h: (b, 0, h))
    rowblk = pl.BlockSpec((None, None, 2, seq), lambda b, h: (b, h, 0, 0))
    return _host_call(
        body, (bsz, npair), [blk, blk, blk, blk, blk, blk, blk, rowblk], [blk, blk, blk, rowblk],
        [jax.ShapeDtypeStruct((bsz, seq, DA), F32)] * 3 + [jax.ShapeDtypeStruct((bsz, npair, 2, seq), F32)],
        [pltpu.VMEM((2, seq, LANES), BF), pltpu.VMEM((2, seq, LANES), BF), pltpu.VMEM((2, 2, tq, tk), F32),
         pltpu.VMEM((2, 2, tq, tk), F32), pltpu.VMEM((2, 2, 2, tq, tk), BF)],
        (qs, kn, vb, do, cq, lse, dl, crow), name, ex)


def _bwd_attn_post(pq, pk, dqs, dkn, pf, dcrow, gq, gk, bf, name, ex=None):
    bsz, seq, _ = pq.shape
    ts = 512
    nt = seq // ts

    def body(q_ref, k_ref, dq_ref, dk_ref, f_ref, dc_ref, gq_ref, gk_ref, bf_ref,
             dpq_ref, dpk_ref, dpf_ref, dgq_ref, dgk_ref, dbf_ref, carry):
        @pl.when((pl.program_id(0) == 0) & (pl.program_id(1) == 0))
        def _():
            dgq_ref[...] = jnp.zeros_like(dgq_ref)
            dgk_ref[...] = jnp.zeros_like(dgk_ref)
            dbf_ref[...] = jnp.zeros_like(dbf_ref)

        @pl.when(pl.program_id(1) == 0)
        def _():
            carry[...] = jnp.zeros_like(carry)

        for x_ref, dx_ref, g_ref, dp_ref, dg_ref, scale in ((q_ref, dq_ref, gq_ref, dpq_ref, dgq_ref, 1.0 / 8.0),
                                                            (k_ref, dk_ref, gk_ref, dpk_ref, dgk_ref, 1.0)):
            for jb in range(DA // LANES):
                sl = slice(jb * LANES, (jb + 1) * LANES)
                gb = g_ref[:, sl]
                xhat, r, _ = _head_rms(x_ref[:, sl], gb)
                dn = dx_ref[:, sl] * scale
                dg_ref[:, sl] += jnp.sum(dn * xhat, axis=0, keepdims=True)
                dxh = dn * gb
                dp_ref[:, sl] = r * (dxh - xhat * (_seg_sum64(dxh * xhat) * (1.0 / HD)))

        ui = lax.broadcasted_iota(jnp.int32, (ts, ts), 0)
        tj = lax.broadcasted_iota(jnp.int32, (ts, ts), 1)
        tri = (tj >= ui).astype(BF)
        dc = jnp.concatenate([dc_ref[...], jnp.zeros((LANES - NH, ts), F32)], axis=0)
        hi, mid, lo = _split3(dc)
        dlf = _dot_nt(tri, hi) + _dot_nt(tri, mid) + _dot_nt(tri, lo) + carry[...]
        carry[...] = dlf[0:1, :]
        xf = f_ref[...] + bf_ref[...]
        lane = lax.broadcasted_iota(jnp.int32, (ts, LANES), 1)
        dfl = jnp.where(lane < NH, dlf * _sigmoid(-xf), 0.0)
        dpf_ref[...] = dfl
        dbf_ref[...] += jnp.sum(dfl, axis=0, keepdims=True)

    rev = lambda i: nt - 1 - i
    tile = lambda n: pl.BlockSpec((None, ts, n), lambda b, i: (b, rev(i), 0))
    vec = lambda n: pl.BlockSpec((1, n), lambda b, i: (0, 0))
    return _host_call(
        body, (bsz, nt),
        [tile(DA), tile(DA), tile(DA), tile(DA), tile(LANES),
         pl.BlockSpec((None, NH, ts), lambda b, i: (b, 0, rev(i))), vec(DA), vec(DA), vec(LANES)],
        [tile(DA), tile(DA), tile(LANES), vec(DA), vec(DA), vec(LANES)],
        [jax.ShapeDtypeStruct((bsz, seq, DA), F32), jax.ShapeDtypeStruct((bsz, seq, DA), F32),
         jax.ShapeDtypeStruct((bsz, seq, LANES), F32), jax.ShapeDtypeStruct((1, DA), F32),
         jax.ShapeDtypeStruct((1, DA), F32), jax.ShapeDtypeStruct((1, LANES), F32)],
        [pltpu.VMEM((1, LANES), F32)], (pq, pk, dqs, dkn, pf, dcrow, gq, gk, bf), name, ex)


def _bwd_conv(pcf, psc, dycs, cf_dw, cf_dw_b, ln_g, ln_b, cf_pw, sc_dw, name):
    bsz, seq, _ = pcf.shape
    ts = 512
    nt, tile, cur, halo, full = _conv_specs(bsz, seq, ts, -1)

    def body(cf_ref, ha_ref, hg_ref, sc_ref, hc_ref, hx_ref, dy_ref, dw_ref, b_ref, lg_ref, lb_ref, pw_ref, sdw_ref,
             dcf_ref, dsc_ref, ddw_ref, db_ref, dlg_ref, dlb_ref, dpw_ref, dsdw_ref,
             ubuf, usbuf, obuf, dubuf, dsbuf, carry_du, carry_dc):
        step = pl.program_id(1)
        first = tile(step) == 0

        @pl.when((pl.program_id(0) == 0) & (step == 0))
        def _():
            for r in (ddw_ref, db_ref, dlg_ref, dlb_ref, dpw_ref, dsdw_ref):
                r[...] = jnp.zeros_like(r)

        @pl.when(step == 0)
        def _():
            carry_du[...] = jnp.zeros((HALO, DC), F32)
            carry_dc[...] = jnp.zeros((HALO, DC), F32)

        def transposed_conv(d, carry, w_ref, taps_t):
            dubuf[0:ts, :] = d
            dubuf[ts:ts + HALO, :] = carry[...]
            carry[...] = d[0:HALO, :]
            _shift_copies(dubuf, dsbuf, sorted({off % SUBLANES for off in taps_t}))
            _tap_conv(dsbuf, w_ref, taps_t, obuf, ts, 0.0)
            return obuf[...]

        a = cf_ref[:, 0:256]
        g = cf_ref[:, 256:512]
        z = cf_ref[:, 512:768]
        lng = lg_ref[...]
        u0, n, rstd, u2, s2, u3 = _conformer_fwd(a, g, ha_ref[...], hg_ref[...], first, ubuf, usbuf, obuf, dw_ref,
                                                 b_ref[...], lng, lb_ref[...])
        u3b = u3.astype(BF)
        p = _dot(u3b, pw_ref[...])
        sz = _sigmoid(z)
        dy = dy_ref[:, 0:256]
        dcf_ref[:, 512:768] = dy * p * _dsilu(z, sz)
        dpb = (dy * (z * sz)).astype(BF)
        dpw_ref[...] += _dot_tn(u3b, dpb)
        du2 = _dot_nt(dpb, pw_ref[...]) * _dsilu(u2, s2)
        dlg_ref[...] += jnp.sum(du2 * n, axis=0, keepdims=True)
        dlb_ref[...] += jnp.sum(du2, axis=0, keepdims=True)
        dn = du2 * lng
        du1 = rstd * (dn - jnp.mean(dn, axis=-1, keepdims=True) - n * jnp.mean(dn * n, axis=-1, keepdims=True))
        db_ref[...] += jnp.sum(du1, axis=0, keepdims=True)
        du0 = transposed_conv(du1, carry_du, dw_ref, CF_TAPS_T)
        for k, tap_grad in enumerate(_tap_sums(dubuf, usbuf, CF_TAPS, ts)):
            ddw_ref[k:k + 1, :] += tap_grad
        sg = _sigmoid(g)
        dcf_ref[:, 0:256] = du0 * sg
        dcf_ref[:, 256:512] = du0 * a * sg * (1.0 - sg)

        bb = sc_ref[:, 0:256]
        c = sc_ref[:, 256:512]
        xs = sc_ref[:, 512:768]
        zs = sc_ref[:, 768:1024]
        cv = _shortconv_fwd(c, xs, hc_ref[...], hx_ref[...], first, ubuf, usbuf, obuf, sdw_ref)
        szs = _sigmoid(zs)
        dys = dy_ref[:, 256:512]
        gate = zs * szs
        dsc_ref[:, 0:256] = dys * cv * gate
        dsc_ref[:, 768:1024] = dys * bb * cv * _dsilu(zs, szs)
        dm = transposed_conv(dys * bb * gate, carry_dc, sdw_ref, SC_TAPS_T)
        for k, tap_grad in enumerate(_tap_sums(dubuf, usbuf, SC_TAPS, ts)):
            dsdw_ref[k:k + 1, :] += tap_grad
        dsc_ref[:, 256:512] = dm * xs
        dsc_ref[:, 512:768] = dm * c

    outt = lambda n: pl.BlockSpec((None, ts, n), lambda b, i: (b, tile(i), 0))
    return pl.pallas_call(
        body, grid=(bsz, nt),
        in_specs=[cur(768), halo(0), halo(1), cur(1024), halo(1), halo(2), cur(512),
                  full(CFW, DC), full(1, DC), full(1, DC), full(1, DC), full(DC, DC), full(SCW, DC)],
        out_specs=[outt(768), outt(1024), full(CFW, DC), full(1, DC), full(1, DC), full(1, DC), full(DC, DC),
                   full(SCW, DC)],
        out_shape=[jax.ShapeDtypeStruct((bsz, seq, 768), F32), jax.ShapeDtypeStruct((bsz, seq, 1024), F32),
                   jax.ShapeDtypeStruct((CFW, DC), F32), jax.ShapeDtypeStruct((1, DC), F32),
                   jax.ShapeDtypeStruct((1, DC), F32), jax.ShapeDtypeStruct((1, DC), F32),
                   jax.ShapeDtypeStruct((DC, DC), F32), jax.ShapeDtypeStruct((SCW, DC), F32)],
        scratch_shapes=[pltpu.VMEM((HALO + ts, DC), F32), pltpu.VMEM((SUBLANES, HALO + ts, DC), F32),
                        pltpu.VMEM((ts, DC), F32), pltpu.VMEM((ts + HALO, DC), F32),
                        pltpu.VMEM((SUBLANES, ts + HALO, DC), F32), pltpu.VMEM((HALO, DC), F32),
                        pltpu.VMEM((HALO, DC), F32)],
        compiler_params=_cparams(("arbitrary", "arbitrary")), name=name,
    )(pcf, pcf, pcf, psc, psc, psc, dycs, cf_dw, cf_dw_b, ln_g, ln_b, cf_pw, sc_dw)


def _bwd_inproj(x, g, dyres, w, dcf, dsc, dq, dk, dv, dz, df, name):
    t = x.shape[0]
    tm = 256
    pieces = ((C_CF, 768), (C_SC, 1024), (C_Q, DA), (C_K, DA), (C_V, DA), (C_Z, DA), (C_F, LANES))

    def body(x_ref, g_ref, dy_ref, w_hbm, dcf_ref, dsc_ref, dq_ref, dk_ref, dv_ref, dz_ref, df_ref,
             dx_ref, dg_ref, dw_hbm, w_vmem, dw_acc, sem, stage, out_sems):
        i = pl.program_id(0)

        @pl.when(i == 0)
        def _():
            cp = pltpu.make_async_copy(w_hbm, w_vmem, sem)
            cp.start()
            dw_acc[...] = jnp.zeros_like(dw_acc)
            dg_ref[...] = jnp.zeros_like(dg_ref)
            cp.wait()

        xt = x_ref[...]
        gg = g_ref[...]
        r = lax.rsqrt(jnp.mean(xt * xt, axis=-1, keepdims=True) + EPS)
        xhat = xt * r
        ht = (xhat * gg).astype(BF).T
        dh = jnp.zeros((tm, D), F32)
        for (c0, n), ref in zip(pieces, (dcf_ref, dsc_ref, dq_ref, dk_ref, dv_ref, dz_ref, df_ref)):
            for s0 in range(0, n, 512):
                s1 = min(s0 + 512, n)
                d = ref[:, s0:s1].astype(BF)
                dh = dh + _dot_nt(d, w_vmem[:, c0 + s0:c0 + s1])
                dw_acc[:, c0 + s0:c0 + s1] += _dot(ht, d)
        dg_ref[...] += jnp.sum(dh * xhat, axis=0, keepdims=True)
        dhg = dh * gg
        dx_ref[...] = dy_ref[...] + r * (dhg - xhat * jnp.mean(dhg * xhat, axis=-1, keepdims=True))

        @pl.when(i == pl.num_programs(0) - 1)
        def _():
            rows_p = D // 4
            pieces_out = [(k, q) for k in range(4) for q in range(4)]
            copies = []
            for p, (k, q) in enumerate(pieces_out):
                slot = p % 2
                if p >= 2:
                    copies[p - 2].wait()

                def chunk(c, _, k=k, q=q, slot=slot):
                    r0 = pl.multiple_of(c * 64, 64)
                    stage[slot, pl.ds(r0, 64), :] = dw_acc[pl.ds(q * rows_p + r0, 64), NSH * k:NSH * (k + 1)]
                    return 0

                lax.fori_loop(0, rows_p // 64, chunk, 0)
                out = pltpu.make_async_copy(stage.at[slot], dw_hbm.at[k, q * rows_p:(q + 1) * rows_p, :],
                                            out_sems.at[slot])
                out.start()
                copies.append(out)
            copies[-2].wait()
            copies[-1].wait()

    row = lambda n: pl.BlockSpec((tm, n), lambda i: (i, 0))
    anyspec = pl.BlockSpec(memory_space=pl.ANY)
    return pl.pallas_call(
        body, grid=(t // tm,),
        in_specs=[row(D), pl.BlockSpec((1, D), lambda i: (0, 0)), row(D), anyspec,
                  row(768), row(1024), row(DA), row(DA), row(DA), row(DA), row(LANES)],
        out_specs=[row(D), pl.BlockSpec((1, D), lambda i: (0, 0)), anyspec],
        out_shape=[jax.ShapeDtypeStruct((t, D), F32), jax.ShapeDtypeStruct((1, D), F32),
                   jax.ShapeDtypeStruct((4, D, NSH), F32)],
        scratch_shapes=[pltpu.VMEM((D, NP), BF), pltpu.VMEM((D, NP), F32), pltpu.SemaphoreType.DMA,
                        pltpu.VMEM((2, D // 4, NSH), F32), pltpu.SemaphoreType.DMA((2,))],
        compiler_params=_cparams(("arbitrary",)), name=name)(x, g, dyres, w, dcf, dsc, dq, dk, dv, dz, df)


def _adamw_update(w_ref, g_ref, m_ref, v_ref, d_ref, m2_ref, v2_ref):
    gt = g_ref[...]
    m2 = ADAM_B1 * m_ref[...] + (1.0 - ADAM_B1) * gt
    v2 = ADAM_B2 * v_ref[...] + (1.0 - ADAM_B2) * (gt * gt)
    m_hat = m2 / (1.0 - ADAM_B1 ** ADAM_STEP)
    v_hat = v2 / (1.0 - ADAM_B2 ** ADAM_STEP)
    d_ref[...] = -ADAM_LR * (m_hat / (jnp.sqrt(v_hat) + ADAM_EPS) + ADAM_WD * w_ref[...])
    m2_ref[...] = m2
    v2_ref[...] = v2


def _adamw_many(ws, gs, ms, vs, name):
    n = len(ws)

    def body(*refs):
        ins, outs = refs[:4 * n], refs[4 * n:]
        for i in range(n):
            _adamw_update(ins[i], ins[n + i], ins[2 * n + i], ins[3 * n + i], *outs[3 * i:3 * i + 3])

    return pl.pallas_call(
        body, out_shape=[jax.ShapeDtypeStruct(w.shape, F32) for w in ws for _ in range(3)],
        name=name)(*ws, *gs, *ms, *vs)


def _adamw(w, g, m, v, tr, name):
    rows, cols = w.shape

    def body(w_ref, g_ref, m_ref, v_ref, d_ref, m2_ref, v2_ref, g_out_ref):
        _adamw_update(w_ref, g_ref, m_ref, v_ref, d_ref, m2_ref, v2_ref)
        g_out_ref[...] = g_ref[...]

    blk = pl.BlockSpec((tr, cols), lambda i: (i, 0))
    return pl.pallas_call(
        body, grid=(rows // tr,), in_specs=[blk] * 4, out_specs=[blk] * 4,
        out_shape=[jax.ShapeDtypeStruct((rows, cols), F32)] * 4,
        compiler_params=_cparams(("arbitrary",)), name=name)(w, g, m, v)


def _place():
    x, y, c = lax.axis_index("x"), lax.axis_index("y"), lax.axis_index("c")
    chips = [(1 - x, y), (x, 1 - y), (1 - x, 1 - y)]
    return x, y, c, chips


def _halves(c, rows, align):
    rh = rows // 2
    return pl.ds(pl.multiple_of(c * rh, align), rh), pl.ds(pl.multiple_of((1 - c) * rh, align), rh)


ANY = pl.BlockSpec(memory_space=pl.ANY)


class _Exchange:
    def __init__(self, operands, out_shape, sems, start, wait, relay=None):
        self.operands, self.out_shape, self.sems, self.start, self.wait = operands, out_shape, sems, start, wait
        self.relay = relay


class _StagedCopy:
    def __init__(self, src, dst, stage, sem):
        self.src, self.dst, self.stage, self.sem = src, dst, stage, sem

    def start(self):
        pltpu.make_async_copy(self.src, self.stage, self.sem).start()

    def relay(self):
        pltpu.make_async_copy(self.src, self.stage, self.sem).wait()
        pltpu.make_async_copy(self.stage, self.dst, self.sem).start()

    def finish(self):
        pltpu.make_async_copy(self.stage, self.dst, self.sem).wait()


def _run_exchange(ex, name):
    n_in, n_out = len(ex.operands), len(ex.out_shape)

    def body(*refs):
        ins, outs, sems = refs[:n_in], refs[n_in:n_in + n_out], refs[n_in + n_out:]
        ex.start(ins, outs, sems)
        if ex.relay:
            ex.relay(ins, outs, sems)
        ex.wait(ins, outs, sems)

    return pl.pallas_call(body, in_specs=[ANY] * n_in, out_specs=[ANY] * n_out, out_shape=ex.out_shape,
                          scratch_shapes=ex.sems, name=name)(*ex.operands)


def _host_call(body, grid, in_specs, out_specs, out_shape, scratch, operands, name, ex=None):
    sem = ("arbitrary",) * len(grid)
    if ex is None:
        outs = pl.pallas_call(body, grid=grid, in_specs=in_specs, out_specs=out_specs, out_shape=out_shape,
                              scratch_shapes=scratch, compiler_params=_cparams(sem), name=name)(*operands)
        return outs, None
    n_in, n_out, n_scr = len(in_specs), len(out_specs), len(scratch)
    xi, xo = len(ex.operands), len(ex.out_shape)

    def hosted(*refs):
        ins, xins = refs[:n_in], refs[n_in:n_in + xi]
        o0 = n_in + xi
        outs, xouts = refs[o0:o0 + n_out], refs[o0 + n_out:o0 + n_out + xo]
        s0 = o0 + n_out + xo
        scr, xsems = refs[s0:s0 + n_scr], refs[s0 + n_scr:]
        first = pl.program_id(0) == 0
        half = pl.program_id(0) == grid[0] // 2
        last = pl.program_id(0) == pl.num_programs(0) - 1
        for d in range(1, len(grid)):
            first = first & (pl.program_id(d) == 0)
            half = half & (pl.program_id(d) == 0)
            last = last & (pl.program_id(d) == pl.num_programs(d) - 1)

        @pl.when(first)
        def _():
            ex.start(xins, xouts, xsems)

        if ex.relay:
            @pl.when(half)
            def _():
                ex.relay(xins, xouts, xsems)

        body(*ins, *outs, *scr)

        @pl.when(last)
        def _():
            ex.wait(xins, xouts, xsems)

    res = pl.pallas_call(
        hosted, grid=grid, in_specs=list(in_specs) + [ANY] * xi, out_specs=list(out_specs) + [ANY] * xo,
        out_shape=list(out_shape) + list(ex.out_shape), scratch_shapes=list(scratch) + list(ex.sems),
        compiler_params=_cparams(sem), name=name)(*operands, *ex.operands)
    return res[:n_out], res[n_out:]


def _all_gather_chips(shards):
    n = len(shards)

    def copies(srcs, outs, sems):
        send_sems, recv_sems, local_sems, *stage = sems
        x, y, c, chips = _place()
        me = 2 * x + y
        halves = [_halves(c, s.shape[1], 16) for s in shards]
        local = [_StagedCopy(srcs[i], outs[i].at[:, me], stage[i], local_sems.at[i]) for i in range(n)]

        def ici(i, j, chip, k):
            mine = halves[i][0]
            return pltpu.make_async_remote_copy(
                src_ref=srcs[i].at[:, mine, :], dst_ref=outs[i].at[:, k, mine, :], send_sem=send_sems.at[6 * i + j],
                recv_sem=recv_sems.at[6 * i + j], device_id=(*chip, c), device_id_type=MESH)

        def d2d(i, j, k, half):
            return pltpu.make_async_remote_copy(
                src_ref=outs[i].at[:, k, half, :], dst_ref=outs[i].at[:, k, half, :], send_sem=send_sems.at[6 * i + 3 + j],
                recv_sem=recv_sems.at[6 * i + 3 + j], device_id=(x, y, 1 - c), device_id_type=MESH)

        return me, chips, halves, local, ici, d2d

    def start(srcs, outs, sems):
        me, chips, _, local, ici, _ = copies(srcs, outs, sems)
        for cp in local:
            cp.start()
        for i in range(n):
            for j, chip in enumerate(chips):
                ici(i, j, chip, me).start()

    def relay(srcs, outs, sems):
        me, chips, halves, local, ici, d2d = copies(srcs, outs, sems)
        for cp in local:
            cp.relay()
        for i in range(n):
            for j, (px, py) in enumerate(chips):
                k = 2 * px + py
                ici(i, j, (px, py), k).wait_recv()
                d2d(i, j, k, halves[i][0]).start()

    def wait(srcs, outs, sems):
        me, chips, halves, local, ici, d2d = copies(srcs, outs, sems)
        for i in range(n):
            for j, (px, py) in enumerate(chips):
                d2d(i, j, 2 * px + py, halves[i][1]).wait_recv()
        for i in range(n):
            for j, (px, py) in enumerate(chips):
                ici(i, j, (px, py), me).wait_send()
                d2d(i, j, 2 * px + py, halves[i][0]).wait_send()
        for cp in local:
            cp.finish()

    return _Exchange(
        list(shards), [jax.ShapeDtypeStruct((s.shape[0], 4) + s.shape[1:], s.dtype) for s in shards],
        [pltpu.SemaphoreType.DMA((6 * n,)), pltpu.SemaphoreType.DMA((6 * n,)), pltpu.SemaphoreType.DMA((n,))]
        + [pltpu.VMEM(s.shape, s.dtype) for s in shards], start, wait, relay)


def _rs_swap_halves(gs):
    n = len(gs)

    def copies(srcs, gots, sems):
        send_sems, recv_sems = sems
        x, y, c, _ = _place()
        return [pltpu.make_async_remote_copy(
            src_ref=srcs[i].at[:, :, _halves(c, gs[i].shape[2], 8)[1], :], dst_ref=gots[i], send_sem=send_sems.at[i],
            recv_sem=recv_sems.at[i], device_id=(x, y, 1 - c), device_id_type=MESH) for i in range(n)]

    def start(srcs, gots, sems):
        for cp in copies(srcs, gots, sems):
            cp.start()

    def wait(srcs, gots, sems):
        for cp in copies(srcs, gots, sems):
            cp.wait()

    return _Exchange(
        list(gs), [jax.ShapeDtypeStruct(g.shape[:2] + (g.shape[2] // 2, g.shape[3]), g.dtype) for g in gs],
        [pltpu.SemaphoreType.DMA((n,)), pltpu.SemaphoreType.DMA((n,))], start, wait)


def _rs_add_pair(g, got, cidx, tb, out_dtype, name):
    nl, _, rh, cols = got.shape
    nb = rh // tb

    def body(c_ref, g_ref, o_ref, s_ref):
        s_ref[...] = (g_ref[...] + o_ref[...]).astype(out_dtype)

    blk = lambda half: pl.BlockSpec((None, None, tb, cols), (lambda l, k, i, c: (l, k, c[0] * nb + i, 0)) if half
                                    else (lambda l, k, i, c: (l, k, i, 0)))
    return pl.pallas_call(
        body,
        grid_spec=pltpu.PrefetchScalarGridSpec(num_scalar_prefetch=1, grid=(nl, 4, nb),
                                               in_specs=[blk(True), blk(False)], out_specs=blk(False)),
        out_shape=jax.ShapeDtypeStruct(got.shape, out_dtype),
        compiler_params=_cparams(("arbitrary", "arbitrary", "arbitrary")), name=name)(cidx, g, got)


def _rs_exchange_chips(pairs):
    n = len(pairs)

    def copies(srcs, gots, sems):
        send_sems, recv_sems, local_sems, *stage = sems
        x, y, c, chips = _place()
        me = 2 * x + y
        local = [_StagedCopy(srcs[i].at[:, me], gots[i].at[:, me], stage[i], local_sems.at[i]) for i in range(n)]

        def ici(i, j, chip, frm, to):
            return pltpu.make_async_remote_copy(
                src_ref=srcs[i].at[:, to], dst_ref=gots[i].at[:, frm], send_sem=send_sems.at[3 * i + j],
                recv_sem=recv_sems.at[3 * i + j], device_id=(*chip, c), device_id_type=MESH)

        sent = [ici(i, j, (px, py), me, 2 * px + py) for i in range(n) for j, (px, py) in enumerate(chips)]
        recvd = [ici(i, j, (px, py), 2 * px + py, me) for i in range(n) for j, (px, py) in enumerate(chips)]
        return local, sent, recvd

    def start(srcs, gots, sems):
        local, sent, _ = copies(srcs, gots, sems)
        for cp in local + sent:
            cp.start()

    def relay(srcs, gots, sems):
        for cp in copies(srcs, gots, sems)[0]:
            cp.relay()

    def wait(srcs, gots, sems):
        local, sent, recvd = copies(srcs, gots, sems)
        for cp in recvd:
            cp.wait_recv()
        for cp in sent:
            cp.wait_send()
        for cp in local:
            cp.finish()

    return _Exchange(
        list(pairs), [jax.ShapeDtypeStruct(p.shape, p.dtype) for p in pairs],
        [pltpu.SemaphoreType.DMA((3 * n,)), pltpu.SemaphoreType.DMA((3 * n,)), pltpu.SemaphoreType.DMA((n,))]
        + [pltpu.VMEM((p.shape[0],) + p.shape[2:], p.dtype) for p in pairs], start, wait, relay)


def _rs_add_chips(got, tb, name):
    nl, _, rh, cols = got.shape

    def body(g_ref, s_ref):
        s_ref[...] = ((g_ref[0].astype(F32) + g_ref[1].astype(F32)) + g_ref[2].astype(F32)) + g_ref[3].astype(F32)

    return pl.pallas_call(
        body, grid=(nl, rh // tb),
        in_specs=[pl.BlockSpec((None, 4, tb, cols), lambda l, i: (l, 0, i, 0))],
        out_specs=pl.BlockSpec((None, tb, cols), lambda l, i: (l, i, 0)),
        out_shape=jax.ShapeDtypeStruct((nl, rh, cols), F32),
        compiler_params=_cparams(("arbitrary", "arbitrary")), name=name)(got)


def _rs_join_halves(groups):
    totals = [t for grp in groups for t in grp]
    where = [(gi, li) for gi, grp in enumerate(groups) for li in range(len(grp))]
    n = len(totals)

    def copies(srcs, outs, sems):
        send_sems, recv_sems, local_sems, *stage = sems
        x, y, c, _ = _place()
        local, send, recv = [], [], []
        for i, (gi, li) in enumerate(where):
            mine, other = _halves(c, 2 * totals[i].shape[1], 8)
            local.append(_StagedCopy(srcs[i], outs[gi].at[li:li + 1, mine, :], stage[i], local_sems.at[i]))
            for lst, rows in ((send, mine), (recv, other)):
                lst.append(pltpu.make_async_remote_copy(
                    src_ref=srcs[i], dst_ref=outs[gi].at[li:li + 1, rows, :], send_sem=send_sems.at[i],
                    recv_sem=recv_sems.at[i], device_id=(x, y, 1 - c), device_id_type=MESH))
        return local, send, recv

    def start(srcs, outs, sems):
        local, send, _ = copies(srcs, outs, sems)
        for cp in local + send:
            cp.start()

    def relay(srcs, outs, sems):
        for cp in copies(srcs, outs, sems)[0]:
            cp.relay()

    def wait(srcs, outs, sems):
        local, send, recv = copies(srcs, outs, sems)
        for cp in recv:
            cp.wait_recv()
        for cp in send:
            cp.wait_send()
        for cp in local:
            cp.finish()

    return _Exchange(
        totals, [jax.ShapeDtypeStruct((len(grp), 2 * grp[0].shape[1], grp[0].shape[2]), F32) for grp in groups],
        [pltpu.SemaphoreType.DMA((n,)), pltpu.SemaphoreType.DMA((n,)), pltpu.SemaphoreType.DMA((n,))]
        + [pltpu.VMEM(t.shape, F32) for t in totals], start, wait, relay)


def _rows(a):
    return a.reshape(-1, LANES)


def _pad_rows(a, mult):
    r = (-a.shape[0]) % mult
    return a if r == 0 else jnp.concatenate([a, jnp.zeros((r, a.shape[1]), a.dtype)], axis=0)


def kernel(x, norm_g, w_in, b_f, cf_dw, cf_dw_b, cf_ln_g, cf_ln_b, cf_pw, sc_dw, q_norm_g, k_norm_g, w_out, loss_target, m_norm_g, m_w_in, m_b_f, m_cf_dw, m_cf_dw_b, m_cf_ln_g, m_cf_ln_b, m_cf_pw, m_sc_dw, m_q_norm_g, m_k_norm_g, m_w_out, v_norm_g, v_w_in, v_b_f, v_cf_dw, v_cf_dw_b, v_cf_ln_g, v_cf_ln_b, v_cf_pw, v_sc_dw, v_q_norm_g, v_k_norm_g, v_w_out):
    bsz, seq, _ = x.shape
    t = bsz * seq

    taps = jnp.concatenate([cf_dw.reshape(-1), sc_dw.reshape(-1)])
    taps_bf = lax.bitcast_convert_type(taps, jnp.bfloat16).reshape(-1, LANES)
    n_pw, n_taps = 2 * 64 * DC // LANES, taps_bf.shape[0]
    small_w = _pad_rows(jnp.concatenate([_rows(cf_pw.astype(BF)), taps_bf], axis=0), 32)[None]
    win_b, wout_b = w_in.astype(BF), w_out.astype(BF)
    win0_g, small_g = _run_exchange(_all_gather_chips([win_b[0:1], small_w]), "all_gather_first")
    gather_rest = _all_gather_chips([win_b[1:DEPTH], wout_b])
    w_in_full = [_assemble_w_in(win0_g[0], "assemble_w_in_0")] + [None] * (DEPTH - 1)
    w_out_full = None
    cf_pw_full = jnp.concatenate([small_g[0, k, 0:n_pw].reshape(DEPTH, 64, DC) for k in range(4)], axis=1)
    taps_all = [lax.bitcast_convert_type(small_g[0, k, n_pw:n_pw + n_taps].reshape(-1, 2), F32) for k in range(4)]
    n_cfdw = DEPTH * CFW * 64
    cf_dw_full = jnp.concatenate([tk[:n_cfdw].reshape(DEPTH, CFW, 64) for tk in taps_all], axis=-1)
    sc_dw_full = jnp.concatenate([tk[n_cfdw:].reshape(DEPTH, SCW, 64) for tk in taps_all], axis=-1)

    bf_pad = jnp.pad(b_f, ((0, 0), (0, LANES - NH)))

    xs = [x.reshape(t, D)]
    saved = []
    dy = loss_part = None
    for l in range(DEPTH):
        xl = xs[-1]
        gq, gk = q_norm_g[l].reshape(1, DA), k_norm_g[l].reshape(1, DA)
        pcf, psc, pq, pk, vb, pz, pf = _fwd_inproj(xl, norm_g[l][None], w_in_full[l], f"fwd_inproj_{l}")
        b3 = lambda a: a.reshape(bsz, seq, a.shape[-1])
        ycs = _fwd_conv(b3(pcf), b3(psc), cf_dw_full[l], cf_dw_b[l][None], cf_ln_g[l][None], cf_ln_b[l][None],
                        cf_pw_full[l], sc_dw_full[l], f"fwd_conv_{l}")
        qs, kn, cq, crow8 = _fwd_attn_prep(b3(pq), b3(pk), b3(pf), gq, gk, bf_pad[l][None], f"fwd_attn_prep_{l}")
        crow = crow8.reshape(bsz, NH // 2, 2, seq)
        (o, ohp, yatt, lse), landed = _fwd_attn(qs, kn, b3(vb), cq, crow, b3(pz), f"fwd_attn_{l}",
                                                gather_rest if l == 0 else None)
        if l == 0:
            win_rest_g, wout_g = landed
            w_in_full[1:] = [_assemble_w_in(win_rest_g[i], f"assemble_w_in_{i + 1}") for i in range(DEPTH - 1)]
            w_out_full = wout_g.reshape(DEPTH, D, D)
        ycs2, yatt2 = ycs.reshape(t, 512), yatt.reshape(t, DA)
        if l + 1 < DEPTH:
            xs.append(_fwd_outproj(xl, ycs2, yatt2, w_out_full[l], f"fwd_outproj_{l}"))
        else:
            dy, loss_part = _fwd_outproj_loss(xl, ycs2, yatt2, w_out_full[l], loss_target.reshape(t, D),
                                              f"fwd_outproj_loss_{l}")
        saved.append((pcf, psc, pq, pk, vb, pz, pf, ycs2, yatt2, o, ohp, qs, kn, cq, crow, lse))

    grads = [None] * DEPTH
    reduced = [None] * DEPTH
    cidx = lax.axis_index("c").astype(jnp.int32).reshape(1)
    big_tiles = [256, 128]

    def big_packs(gl):
        return [gl["w_in"][None], gl["w_out"].reshape(1, 4, 256, D)]

    def add_pairs(packs, gots, tiles, dtypes, tag):
        return [_rs_add_pair(p, got, cidx, tb, dt, f"rs_add_pair_{tag}_{i}")
                for i, (p, got, tb, dt) in enumerate(zip(packs, gots, tiles, dtypes))]

    def add_chips(gots2, tiles, tag):
        return [_rs_add_chips(got, tb, f"rs_add_chips_{tag}_{i}") for i, (got, tb) in enumerate(zip(gots2, tiles))]

    for l in reversed(range(DEPTH)):
        pcf, psc, pq, pk, vb, pz, pf, ycs2, yatt2, o, ohp, qs, kn, cq, crow, lse = saved[l]
        gq, gk = q_norm_g[l].reshape(1, DA), k_norm_g[l].reshape(1, DA)
        b3 = lambda a: a.reshape(bsz, seq, a.shape[-1])
        f2 = lambda a: a.reshape(t, a.shape[-1])
        packs = big_packs(grads[l + 1]) if l + 1 < DEPTH else None
        (dycs, do, dz, dl, d_wout), gots = _bwd_outproj(
            dy, ycs2, yatt2, f2(o), f2(ohp), pz, w_out_full[l], f"bwd_outproj_{l}",
            _rs_swap_halves(packs) if packs else None)
        pairs = add_pairs(packs, gots, big_tiles, [BF, BF], l + 1) if packs else None
        (dqs, dkn, dv, dcrow), gots2 = _bwd_attn(qs, kn, b3(vb), b3(do), cq, lse, b3(dl), crow, f"bwd_attn_{l}",
                                                 _rs_exchange_chips(pairs) if packs else None)
        if packs:
            reduced[l + 1] = add_chips(gots2, big_tiles, l + 1)
        (dpq, dpk, dpf, d_gq, d_gk, d_bf), _ = _bwd_attn_post(
            b3(pq), b3(pk), dqs, dkn, b3(pf), dcrow.reshape(bsz, NH, seq), gq, gk, bf_pad[l][None],
            f"bwd_attn_post_{l}")
        dcf, dsc, d_cfdw, d_cfb, d_lng, d_lnb, d_pw, d_scdw = _bwd_conv(
            b3(pcf), b3(psc), b3(dycs), cf_dw_full[l], cf_dw_b[l][None], cf_ln_g[l][None], cf_ln_b[l][None],
            cf_pw_full[l], sc_dw_full[l], f"bwd_conv_{l}")
        dy, d_ng, d_win = _bwd_inproj(xs[l], norm_g[l][None], dy, w_in_full[l], f2(dcf), f2(dsc), f2(dpq), f2(dpk),
                                      f2(dv), dz, f2(dpf), f"bwd_inproj_{l}")
        grads[l] = dict(norm_g=d_ng[0], w_in=d_win, b_f=d_bf[0, :NH], cf_dw=d_cfdw, cf_dw_b=d_cfb[0],
                        cf_ln_g=d_lng[0], cf_ln_b=d_lnb[0], cf_pw=d_pw, sc_dw=d_scdw,
                        q_norm_g=d_gq.reshape(NH, HD), k_norm_g=d_gk.reshape(NH, HD), w_out=d_wout)
    grad_x = dy.reshape(bsz, seq, D)
    gw = {n: jnp.stack([grads[l][n] for l in range(DEPTH)]) for n in grads[0] if n not in ("w_in", "w_out")}

    rep_names = ("norm_g", "b_f", "cf_dw_b", "cf_ln_g", "cf_ln_b", "q_norm_g", "k_norm_g")
    rep = jnp.concatenate([jnp.pad(gw[n].reshape(-1), (0, (-gw[n].size) % LANES)) for n in rep_names]
                          + [jnp.pad(loss_part.reshape(-1), (0, LANES - 1))]).reshape(-1, LANES)
    blocks = []
    for k in range(4):
        small = jnp.concatenate([gw["cf_dw"][:, :, 64 * k:64 * (k + 1)].reshape(-1),
                                 gw["sc_dw"][:, :, 64 * k:64 * (k + 1)].reshape(-1)]).reshape(-1, LANES)
        blocks.append(_pad_rows(jnp.concatenate([_rows(gw["cf_pw"][:, 64 * k:64 * (k + 1), :]), small, rep], axis=0), 16))
    g_small = jnp.stack(blocks)[None]
    packs = big_packs(grads[0]) + [g_small]
    tiles = big_tiles + [g_small.shape[2] // 2]
    gots = _run_exchange(_rs_swap_halves(packs), "rs_swap_halves_last")
    pairs = add_pairs(packs, gots, tiles, [BF, BF, F32], 0)
    gots2 = _run_exchange(_rs_exchange_chips(pairs), "rs_exchange_chips_last")
    tot_win0, tot_wout0, tot_small = add_chips(gots2, tiles, 0)
    reduced[0] = (tot_win0, tot_wout0)
    red_win, red_wout, red_small = _run_exchange(
        _rs_join_halves([[reduced[l][0] for l in range(DEPTH)], [reduced[l][1] for l in range(DEPTH)], [tot_small]]),
        "rs_join_halves")

    n_small = (DEPTH * CFW * 64 + DEPTH * SCW * 64) // LANES
    red = red_small[0]
    pos = [0]

    def take(nrows):
        pos[0] += nrows
        return red[pos[0] - nrows:pos[0]]

    g = {}
    g["w_in"] = red_win
    g["w_out"] = red_wout
    g["cf_pw"] = take(n_pw).reshape(DEPTH, 64, DC)
    small = take(n_small).reshape(-1)
    g["cf_dw"] = small[:n_cfdw].reshape(DEPTH, CFW, 64)
    g["sc_dw"] = small[n_cfdw:].reshape(DEPTH, SCW, 64)
    shapes = dict(norm_g=norm_g.shape, b_f=b_f.shape, cf_dw_b=cf_dw_b.shape, cf_ln_g=cf_ln_g.shape,
                  cf_ln_b=cf_ln_b.shape, q_norm_g=q_norm_g.shape, k_norm_g=k_norm_g.shape)
    for n in rep_names:
        size = 1
        for s in shapes[n]:
            size *= s
        g[n] = take(-(-size // LANES)).reshape(-1)[:size].reshape(shapes[n])
    loss = take(1)[0, 0]

    order = ("norm_g", "w_in", "b_f", "cf_dw", "cf_dw_b", "cf_ln_g", "cf_ln_b", "cf_pw", "sc_dw", "q_norm_g",
             "k_norm_g", "w_out")
    weights = dict(norm_g=norm_g, w_in=w_in, b_f=b_f, cf_dw=cf_dw, cf_dw_b=cf_dw_b, cf_ln_g=cf_ln_g, cf_ln_b=cf_ln_b,
                   cf_pw=cf_pw, sc_dw=sc_dw, q_norm_g=q_norm_g, k_norm_g=k_norm_g, w_out=w_out)
    ms = dict(norm_g=m_norm_g, w_in=m_w_in, b_f=m_b_f, cf_dw=m_cf_dw, cf_dw_b=m_cf_dw_b, cf_ln_g=m_cf_ln_g,
              cf_ln_b=m_cf_ln_b, cf_pw=m_cf_pw, sc_dw=m_sc_dw, q_norm_g=m_q_norm_g, k_norm_g=m_k_norm_g, w_out=m_w_out)
    vs = dict(norm_g=v_norm_g, w_in=v_w_in, b_f=v_b_f, cf_dw=v_cf_dw, cf_dw_b=v_cf_dw_b, cf_ln_g=v_cf_ln_g,
              cf_ln_b=v_cf_ln_b, cf_pw=v_cf_pw, sc_dw=v_sc_dw, q_norm_g=v_q_norm_g, k_norm_g=v_k_norm_g, w_out=v_w_out)
    delta, new_m, new_v = {}, {}, {}
    two_d = lambda a: a.reshape(-1, a.shape[-1])
    for n in ("w_in", "w_out"):
        outs = _adamw(two_d(weights[n]), two_d(g[n]), two_d(ms[n]), two_d(vs[n]), 256, f"adamw_{n}")
        delta[n], new_m[n], new_v[n], g[n] = (a.reshape(weights[n].shape) for a in outs)
    small_names = [n for n in order if n not in ("w_in", "w_out")]
    outs = _adamw_many([weights[n] for n in small_names], [g[n] for n in small_names],
                       [ms[n] for n in small_names], [vs[n] for n in small_names], "adamw_small")
    for i, n in enumerate(small_names):
        delta[n], new_m[n], new_v[n] = outs[3 * i:3 * i + 3]

    return (loss, grad_x, *[g[n] for n in order], *[delta[n] for n in order], *[new_m[n] for n in order],
            *[new_v[n] for n in order])
```

```python
import jax
import jax.numpy as jnp
from jax import lax
from jax.experimental import pallas as pl
from jax.experimental.pallas import tpu as pltpu

F32 = jnp.float32
BF = jnp.bfloat16
MESH = pl.DeviceIdType.MESH

DEPTH = 2
D = 1024
DC = 256
DA = 512
NH = 8
HD = 64
CFW = 31
SCW = 3
N_IN = 3848
NP = 3968
NSH = N_IN // 4
HALO = 32
ATT_ROWS = 128
BWD_KEYS = 512
FWD_ROWS = 256
SUBLANES = 8
CONV_ROWS = 64
TAP_GROUP = 16
EPS = 1e-6
LANES = 128
VMEM_LIMIT = 56 * 1024 * 1024

C_CF, C_SC, C_Q, C_K, C_V, C_Z, C_F = 0, 768, 1792, 2304, 2816, 3328, 3840

ADAM_LR = 0.001
ADAM_B1 = 0.9
ADAM_B2 = 0.999
ADAM_EPS = 1e-08
ADAM_WD = 0.01
ADAM_STEP = 10


def _cparams(sem=None):
    return pltpu.CompilerParams(dimension_semantics=sem, vmem_limit_bytes=VMEM_LIMIT)


def _dot(a, b):
    return jnp.dot(a, b, preferred_element_type=F32)


def _dot_nt(a, b):
    return lax.dot_general(a, b, (((1,), (1,)), ((), ())), preferred_element_type=F32)


def _dot_tn(a, b):
    return lax.dot_general(a, b, (((0,), (0,)), ((), ())), preferred_element_type=F32)


def _split3(x):
    hi = x.astype(BF)
    r1 = x - hi.astype(F32)
    mid = r1.astype(BF)
    lo = (r1 - mid.astype(F32)).astype(BF)
    return hi, mid, lo


def _dot_exact(a_bf, x):
    hi, mid, lo = _split3(x)
    return _dot(a_bf, hi) + _dot(a_bf, mid) + _dot(a_bf, lo)


def _sigmoid(x):
    return 1.0 / (1.0 + jnp.exp(-x))


def _seg_sum64(x, exact=False):
    i = lax.broadcasted_iota(jnp.int32, (LANES, LANES), 0)
    j = lax.broadcasted_iota(jnp.int32, (LANES, LANES), 1)
    g = ((i >= HD) == (j >= HD)).astype(BF)
    hi, mid, lo = _split3(x)
    two = _dot(hi, g) + _dot(mid, g)
    return two + _dot(lo, g) if exact else two


def _assemble_w_in(shards, name):
    tr = 256

    def body(s_ref, o_ref):
        for k in range(4):
            o_ref[:, NSH * k:NSH * (k + 1)] = s_ref[k]
        o_ref[:, N_IN:NP] = jnp.zeros((tr, NP - N_IN), shards.dtype)

    return pl.pallas_call(
        body, grid=(D // tr,), in_specs=[pl.BlockSpec((4, tr, NSH), lambda i: (0, i, 0))],
        out_specs=pl.BlockSpec((tr, NP), lambda i: (i, 0)), out_shape=jax.ShapeDtypeStruct((D, NP), shards.dtype),
        compiler_params=_cparams(("arbitrary",)), name=name)(shards)


def _fwd_inproj(x, g, w, name):
    t = x.shape[0]
    tm = 512

    def body(x_ref, g_ref, w_ref, cf_ref, sc_ref, q_ref, k_ref, v_ref, z_ref, f_ref):
        xt = x_ref[...]
        r = lax.rsqrt(jnp.mean(xt * xt, axis=-1, keepdims=True) + EPS)
        h = ((xt * r) * g_ref[...]).astype(BF)
        cf_ref[:, 0:512] = _dot(h, w_ref[:, 0:512])
        cf_ref[:, 512:768] = _dot(h, w_ref[:, 512:768])
        sc_ref[:, 0:512] = _dot(h, w_ref[:, C_SC:C_SC + 512])
        sc_ref[:, 512:1024] = _dot(h, w_ref[:, C_SC + 512:C_Q])
        q_ref[...] = _dot(h, w_ref[:, C_Q:C_K])
        k_ref[...] = _dot(h, w_ref[:, C_K:C_V])
        v_ref[...] = _dot(h, w_ref[:, C_V:C_Z]).astype(BF)
        z_ref[...] = _dot(h, w_ref[:, C_Z:C_F])
        f_ref[...] = _dot(h, w_ref[:, C_F:NP])

    row = lambda n: pl.BlockSpec((tm, n), lambda i: (i, 0))
    return pl.pallas_call(
        body, grid=(t // tm,),
        in_specs=[row(D), pl.BlockSpec((1, D), lambda i: (0, 0)), pl.BlockSpec((D, NP), lambda i: (0, 0))],
        out_specs=[row(768), row(1024), row(DA), row(DA), row(DA), row(DA), row(LANES)],
        out_shape=[jax.ShapeDtypeStruct((t, 768), F32), jax.ShapeDtypeStruct((t, 1024), F32),
                   jax.ShapeDtypeStruct((t, DA), F32), jax.ShapeDtypeStruct((t, DA), F32),
                   jax.ShapeDtypeStruct((t, DA), BF), jax.ShapeDtypeStruct((t, DA), F32),
                   jax.ShapeDtypeStruct((t, LANES), F32)],
        compiler_params=_cparams(("arbitrary",)), name=name)(x, g, w)


def _shift_copies(buf, sbuf, shifts):
    n = buf.shape[0]
    for sh in shifts:
        rows = n if sh == 0 else n - SUBLANES
        sbuf[sh, 0:rows, :] = buf[pl.ds(sh, rows), :]


def _tap_rows(sbuf, off, r0):
    sh = off % SUBLANES
    return sbuf[sh, pl.ds(pl.multiple_of(r0 + (off - sh), SUBLANES), CONV_ROWS), :]


def _tap_conv(sbuf, w_ref, offsets, dst, ts, bias):
    def chunk(c, _):
        r0 = pl.multiple_of(c * CONV_ROWS, CONV_ROWS)
        acc = jnp.zeros((CONV_ROWS, DC), F32) + bias
        for k, off in enumerate(offsets):
            acc = acc + w_ref[k:k + 1, :] * _tap_rows(sbuf, off, r0)
        dst[pl.ds(r0, CONV_ROWS), :] = acc
        return 0

    lax.fori_loop(0, ts // CONV_ROWS, chunk, 0)


def _tap_sums(a_buf, sbuf, offsets, ts):
    outs = []
    for g0 in range(0, len(offsets), TAP_GROUP):
        group = offsets[g0:g0 + TAP_GROUP]

        def chunk(c, accs, group=group):
            r0 = pl.multiple_of(c * CONV_ROWS, CONV_ROWS)
            a = a_buf[pl.ds(r0, CONV_ROWS), :]
            return tuple(acc + (a * _tap_rows(sbuf, off, r0)).reshape(CONV_ROWS // SUBLANES, SUBLANES, DC).sum(axis=0)
                         for acc, off in zip(accs, group))

        accs = lax.fori_loop(0, ts // CONV_ROWS, chunk, tuple(jnp.zeros((SUBLANES, DC), F32) for _ in group))
        outs += [jnp.sum(acc, axis=0, keepdims=True) for acc in accs]
    return outs


CF_TAPS = tuple(HALO - (CFW - 1) + k for k in range(CFW))
SC_TAPS = tuple(HALO - (SCW - 1) + k for k in range(SCW))
CF_TAPS_T = tuple(CFW - 1 - k for k in range(CFW))
SC_TAPS_T = tuple(SCW - 1 - k for k in range(SCW))


def _conformer_fwd(a, g, ha, hg, first, ubuf, usbuf, u1buf, dw_ref, bias, lng, lnb):
    ts = a.shape[0]
    u0 = a * _sigmoid(g)
    ubuf[0:HALO, :] = jnp.where(first, 0.0, ha * _sigmoid(hg))
    ubuf[HALO:HALO + ts, :] = u0
    _shift_copies(ubuf, usbuf, range(SUBLANES))
    _tap_conv(usbuf, dw_ref, CF_TAPS, u1buf, ts, bias)
    u1 = u1buf[...]
    mu = jnp.mean(u1, axis=-1, keepdims=True)
    xc = u1 - mu
    rstd = lax.rsqrt(jnp.mean(xc * xc, axis=-1, keepdims=True) + EPS)
    n = xc * rstd
    u2 = n * lng + lnb
    s2 = _sigmoid(u2)
    u3 = u2 * s2
    return u0, n, rstd, u2, s2, u3


def _shortconv_fwd(c, xs, hc, hx, first, mbuf, msbuf, cvbuf, dw_ref):
    ts = c.shape[0]
    mbuf[0:HALO, :] = jnp.where(first, 0.0, hc * hx)
    mbuf[HALO:HALO + ts, :] = c * xs
    _shift_copies(mbuf, msbuf, sorted({off % SUBLANES for off in SC_TAPS}))
    _tap_conv(msbuf, dw_ref, SC_TAPS, cvbuf, ts, 0.0)
    return cvbuf[...]


def _conv_specs(bsz, seq, ts, order):
    nt = seq // ts
    tile = (lambda i: i) if order > 0 else (lambda i: nt - 1 - i)
    hrow = lambda i: jnp.maximum(tile(i) * (ts // HALO) - 1, 0)
    cur = lambda n: pl.BlockSpec((None, ts, n), lambda b, i: (b, tile(i), 0))
    halo = lambda j: pl.BlockSpec((None, HALO, DC), lambda b, i: (b, hrow(i), j))
    full = lambda r, c: pl.BlockSpec((r, c), lambda b, i: (0, 0))
    return nt, tile, cur, halo, full


def _fwd_conv(pcf, psc, cf_dw, cf_dw_b, ln_g, ln_b, cf_pw, sc_dw, name):
    bsz, seq, _ = pcf.shape
    ts = 512
    nt, tile, cur, halo, full = _conv_specs(bsz, seq, ts, +1)

    def body(cf_ref, ha_ref, hg_ref, sc_ref, hc_ref, hx_ref, dw_ref, b_ref, lg_ref, lb_ref, pw_ref, sdw_ref,
             y_ref, ubuf, sbuf, obuf):
        first = pl.program_id(1) == 0
        _, _, _, _, _, u3 = _conformer_fwd(cf_ref[:, 0:256], cf_ref[:, 256:512], ha_ref[...], hg_ref[...], first,
                                           ubuf, sbuf, obuf, dw_ref, b_ref[...], lg_ref[...], lb_ref[...])
        z = cf_ref[:, 512:768]
        y_ref[:, 0:256] = _dot(u3.astype(BF), pw_ref[...]) * (z * _sigmoid(z))
        cv = _shortconv_fwd(sc_ref[:, 256:512], sc_ref[:, 512:768], hc_ref[...], hx_ref[...], first,
                            ubuf, sbuf, obuf, sdw_ref)
        zs = sc_ref[:, 768:1024]
        y_ref[:, 256:512] = sc_ref[:, 0:256] * cv * (zs * _sigmoid(zs))

    return pl.pallas_call(
        body, grid=(bsz, nt),
        in_specs=[cur(768), halo(0), halo(1), cur(1024), halo(1), halo(2),
                  full(CFW, DC), full(1, DC), full(1, DC), full(1, DC), full(DC, DC), full(SCW, DC)],
        out_specs=pl.BlockSpec((None, ts, 512), lambda b, i: (b, i, 0)),
        out_shape=jax.ShapeDtypeStruct((bsz, seq, 512), F32),
        scratch_shapes=[pltpu.VMEM((HALO + ts, DC), F32), pltpu.VMEM((SUBLANES, HALO + ts, DC), F32),
                        pltpu.VMEM((ts, DC), F32)],
        compiler_params=_cparams(("arbitrary", "arbitrary")), name=name,
    )(pcf, pcf, pcf, psc, psc, psc, cf_dw, cf_dw_b, ln_g, ln_b, cf_pw, sc_dw)


def _head_rms(xb, gb):
    ms = _seg_sum64(xb * xb) * (1.0 / HD)
    r = lax.rsqrt(ms + EPS)
    xhat = xb * r
    return xhat, r, xhat * gb


def _fwd_attn_prep(pq, pk, pf, gq, gk, bf, name):
    bsz, seq, _ = pq.shape
    ts = 512
    nt = seq // ts

    def body(q_ref, k_ref, f_ref, gq_ref, gk_ref, bf_ref, qs_ref, kn_ref, cq_ref, crow_ref, carry):
        @pl.when(pl.program_id(1) == 0)
        def _():
            carry[...] = jnp.zeros_like(carry)

        for jb in range(DA // LANES):
            sl = slice(jb * LANES, (jb + 1) * LANES)
            _, _, qn = _head_rms(q_ref[:, sl], gq_ref[:, sl])
            qs_ref[:, sl] = (qn * (1.0 / 8.0)).astype(BF)
            _, _, kn = _head_rms(k_ref[:, sl], gk_ref[:, sl])
            kn_ref[:, sl] = kn.astype(BF)

        xf = f_ref[...] + bf_ref[...]
        lf = jnp.minimum(xf, 0.0) - jnp.log(1.0 + jnp.exp(-jnp.abs(xf)))
        ti = lax.broadcasted_iota(jnp.int32, (ts, ts), 0)
        si = lax.broadcasted_iota(jnp.int32, (ts, ts), 1)
        c = _dot_exact((si <= ti).astype(BF), lf) + carry[...]
        carry[...] = c[ts - 1:ts, :]
        hj = lax.broadcasted_iota(jnp.int32, (LANES, DA), 0)
        ll = lax.broadcasted_iota(jnp.int32, (LANES, DA), 1)
        dd = ll - hj * HD
        chi, cmid, clo = _split3(c)
        e = ((dd >= 0) & (dd < HD)).astype(BF)
        cq_ref[...] = _dot(chi, e) + _dot(cmid, e) + _dot(clo, e)
        hh = lax.broadcasted_iota(jnp.int32, (16, LANES), 0)
        jj = lax.broadcasted_iota(jnp.int32, (16, LANES), 1)
        sel = (hh == jj).astype(BF)
        cr = _dot_nt(sel, chi) + _dot_nt(sel, cmid) + _dot_nt(sel, clo)
        crow_ref[...] = cr[0:NH, :]

    tile = lambda n: pl.BlockSpec((None, ts, n), lambda b, i: (b, i, 0))
    vec = lambda n: pl.BlockSpec((1, n), lambda b, i: (0, 0))
    return pl.pallas_call(
        body, grid=(bsz, nt),
        in_specs=[tile(DA), tile(DA), tile(LANES), vec(DA), vec(DA), vec(LANES)],
        out_specs=[tile(DA), tile(DA), tile(DA), pl.BlockSpec((None, NH, ts), lambda b, i: (b, 0, i))],
        out_shape=[jax.ShapeDtypeStruct((bsz, seq, DA), BF), jax.ShapeDtypeStruct((bsz, seq, DA), BF),
                   jax.ShapeDtypeStruct((bsz, seq, DA), F32), jax.ShapeDtypeStruct((bsz, NH, seq), F32)],
        scratch_shapes=[pltpu.VMEM((1, LANES), F32)],
        compiler_params=_cparams(("arbitrary", "arbitrary")), name=name)(pq, pk, pf, gq, gk, bf)


def _tile_rows(i, rows):
    return pl.ds(i * rows, rows) if isinstance(i, int) else pl.ds(pl.multiple_of(i * rows, rows), rows)


def _causal_mask(tq, tk):
    r = lax.broadcasted_iota(jnp.int32, (tq, tk), 0)
    c = lax.broadcasted_iota(jnp.int32, (tq, tk), 1)
    return r - c


def _fwd_attn(qs, kn, vb, cq, crow, pz, name, ex=None):
    bsz, seq, _ = qs.shape
    tq, tk = 256, 256
    nq, nk = seq // tq, seq // tk
    npair = NH // 2

    def body(q_ref, k_ref, v_ref, cq_ref, cr_ref, z_ref, o_ref, ohp_ref, y_ref, lse_ref,
             s_buf, p_buf, acc_ref, accl_ref):
        qi = pl.program_id(2)
        head0 = lax.broadcasted_iota(jnp.int32, (1, LANES), 1) < HD
        q = q_ref[...]
        zq = jnp.zeros_like(q)
        qm = (jnp.where(head0, q, zq), jnp.where(head0, zq, q))
        ct = (cq_ref[:, 0:1], cq_ref[:, HD:HD + 1])

        tile = lambda j: _tile_rows(j, tk)

        def scores(j, slot):
            kb = k_ref[tile(j), :]
            for hh in range(2):
                s_buf[slot, hh] = _dot_nt(qm[hh], kb) + (ct[hh] - cr_ref[hh:hh + 1, tile(j)])

        def weighted_values(slot, j, al):
            vb_t = v_ref[tile(j), :]
            acc_ref[...] = al * acc_ref[...] + jnp.where(head0, _dot(p_buf[slot, 0, 0], vb_t),
                                                         _dot(p_buf[slot, 1, 0], vb_t))
            accl_ref[...] = al * accl_ref[...] + jnp.where(head0, _dot(p_buf[slot, 0, 1], vb_t),
                                                           _dot(p_buf[slot, 1, 1], vb_t))

        def softmax(slot, first_visible, m, l, masked):
            m_new, l_new, al_new = ([], []), ([], []), []
            for r in range(tq // FWD_ROWS):
                rows = slice(r * FWD_ROWS, (r + 1) * FWD_ROWS)
                visible = _causal_mask(FWD_ROWS, tk) >= first_visible - r * FWD_ROWS
                alphas = []
                for hh in range(2):
                    s = s_buf[slot, hh, rows, :]
                    if masked:
                        s = jnp.where(visible, s, -1e30)
                    m_old = m[hh][rows]
                    m2 = jnp.maximum(m_old, jnp.max(s, axis=-1, keepdims=True))
                    p = jnp.exp(s - m2)
                    alpha = jnp.exp(m_old - m2)
                    l_new[hh].append(alpha * l[hh][rows] + jnp.sum(p, axis=-1, keepdims=True))
                    m_new[hh].append(m2)
                    pb = p.astype(BF)
                    p_buf[slot, hh, 0, rows, :] = pb
                    p_buf[slot, hh, 1, rows, :] = (p - pb.astype(F32)).astype(BF)
                    alphas.append(alpha)
                al_new.append(jnp.where(head0, alphas[0], alphas[1]))
            cat = lambda parts: jnp.concatenate(parts, axis=0)
            return (cat(m_new[0]), cat(m_new[1])), (cat(l_new[0]), cat(l_new[1])), cat(al_new)

        clamp = lambda j: jnp.clip(j, 0, nk - 1)

        def step(j, slot, carry, masked):
            m, l, al = carry
            weighted_values(1 - slot, clamp(j - 1), al)
            m, l, al = softmax(slot, j * tk - qi * tq, m, l, masked)
            scores(clamp(j + 1), 1 - slot)
            return m, l, al

        scores(0, 0)
        p_buf[1] = jnp.zeros((2, 2, tq, tk), BF)
        acc_ref[...] = jnp.zeros((tq, LANES), F32)
        accl_ref[...] = jnp.zeros((tq, LANES), F32)
        neg = jnp.full((tq, 1), -1e30, F32)
        zcol = jnp.zeros((tq, 1), F32)
        full = (qi * (tq // tk)) // 2
        carry = lax.fori_loop(0, full, lambda t, cy: step(2 * t + 1, 1, step(2 * t, 0, cy, False), False),
                              ((neg, neg), (zcol, zcol), jnp.ones((tq, LANES), F32)))
        m, l, al = step(2 * full + 1, 1, step(2 * full, 0, carry, True), True)
        weighted_values(1, clamp(2 * full + 1), al)
        inv = jnp.where(head0, 1.0 / l[0], 1.0 / l[1])
        o = acc_ref[...] * inv
        z = z_ref[...]
        o_ref[...] = o
        ohp_ref[...] = (acc_ref[...] + accl_ref[...]) * inv
        y_ref[...] = o * (z * _sigmoid(z))
        lse_ref[...] = jnp.where(head0, m[0] + jnp.log(l[0]), m[1] + jnp.log(l[1]))

    qblk = pl.BlockSpec((None, tq, LANES), lambda b, h, i: (b, i, h))
    kvblk = pl.BlockSpec((None, seq, LANES), lambda b, h, i: (b, 0, h))
    return _host_call(
        body, (bsz, npair, nq),
        [qblk, kvblk, kvblk, qblk, pl.BlockSpec((None, None, 2, seq), lambda b, h, i: (b, h, 0, 0)), qblk],
        [qblk, qblk, qblk, qblk], [jax.ShapeDtypeStruct((bsz, seq, DA), F32)] * 4,
        [pltpu.VMEM((2, 2, tq, tk), F32), pltpu.VMEM((2, 2, 2, tq, tk), BF), pltpu.VMEM((tq, LANES), F32),
         pltpu.VMEM((tq, LANES), F32)],
        (qs, kn, vb, cq, crow, pz), name, ex)


def _fwd_outproj(x, ycs, yatt, wo, name):
    t = x.shape[0]
    tm = 512

    def body(x_ref, a_ref, b_ref, w_ref, o_ref):
        o_ref[...] = (x_ref[...] + _dot(a_ref[...].astype(BF), w_ref[0:512, :])
                      + _dot(b_ref[...].astype(BF), w_ref[512:1024, :]))

    row = lambda n: pl.BlockSpec((tm, n), lambda i: (i, 0))
    return pl.pallas_call(
        body, grid=(t // tm,),
        in_specs=[row(D), row(512), row(512), pl.BlockSpec((D, D), lambda i: (0, 0))],
        out_specs=row(D), out_shape=jax.ShapeDtypeStruct((t, D), F32),
        compiler_params=_cparams(("arbitrary",)), name=name)(x, ycs, yatt, wo)


def _fwd_outproj_loss(x, ycs, yatt, wo, target, name):
    t = x.shape[0]
    tm = 512

    def body(x_ref, a_ref, b_ref, w_ref, t_ref, dy_ref, loss_ref):
        @pl.when(pl.program_id(0) == 0)
        def _():
            loss_ref[...] = jnp.zeros_like(loss_ref)

        y = (x_ref[...] + _dot(a_ref[...].astype(BF), w_ref[0:512, :])
             + _dot(b_ref[...].astype(BF), w_ref[512:1024, :]))
        err = y - t_ref[...]
        dy_ref[...] = err * (1.0 / D)
        per_tok = jnp.mean(err * err, axis=-1, keepdims=True)
        loss_ref[...] += 0.5 * jnp.sum(per_tok, axis=0, keepdims=True)

    row = lambda n: pl.BlockSpec((tm, n), lambda i: (i, 0))
    return pl.pallas_call(
        body, grid=(t // tm,),
        in_specs=[row(D), row(512), row(512), pl.BlockSpec((D, D), lambda i: (0, 0)), row(D)],
        out_specs=[row(D), pl.BlockSpec((1, 1), lambda i: (0, 0))],
        out_shape=[jax.ShapeDtypeStruct((t, D), F32), jax.ShapeDtypeStruct((1, 1), F32)],
        compiler_params=_cparams(("arbitrary",)), name=name)(x, ycs, yatt, wo, target)


def _dsilu(x, s):
    return s * (1.0 + x * (1.0 - s))


def _bwd_outproj(dy, ycs, yatt, o, ohp, pz, wo, name, ex=None):
    t = dy.shape[0]
    tm = 512

    def body(dy_ref, a_ref, b_ref, o_ref, ohp_ref, z_ref, w_ref, dcs_ref, do_ref, dz_ref, dl_ref, dw_ref):
        @pl.when(pl.program_id(0) == 0)
        def _():
            dw_ref[...] = jnp.zeros_like(dw_ref)

        dyb = dy_ref[...].astype(BF)
        dw_ref[0:512, :] += _dot_tn(a_ref[...].astype(BF), dyb)
        dw_ref[512:1024, :] += _dot_tn(b_ref[...].astype(BF), dyb)
        dcs_ref[...] = _dot_nt(dyb, w_ref[0:512, :])
        dyatt = _dot_nt(dyb, w_ref[512:1024, :])
        z = z_ref[...]
        sz = _sigmoid(z)
        o_t = o_ref[...]
        dob = (dyatt * (z * sz)).astype(BF)
        do_ref[...] = dob
        dz_ref[...] = dyatt * o_t * _dsilu(z, sz)
        prod = dob.astype(F32) * ohp_ref[...]
        for jb in range(DA // LANES):
            sl = slice(jb * LANES, (jb + 1) * LANES)
            dl_ref[:, sl] = _seg_sum64(prod[:, sl], exact=True)

    row = lambda n: pl.BlockSpec((tm, n), lambda i: (i, 0))
    return _host_call(
        body, (t // tm,),
        [row(D), row(512), row(512), row(DA), row(DA), row(DA), pl.BlockSpec((D, D), lambda i: (0, 0))],
        [row(512), row(DA), row(DA), row(DA), pl.BlockSpec((D, D), lambda i: (0, 0))],
        [jax.ShapeDtypeStruct((t, 512), F32), jax.ShapeDtypeStruct((t, DA), BF),
         jax.ShapeDtypeStruct((t, DA), F32), jax.ShapeDtypeStruct((t, DA), F32),
         jax.ShapeDtypeStruct((D, D), F32)], [],
        (dy, ycs, yatt, o, ohp, pz, wo), name, ex)


def _bwd_attn(qs, kn, vb, do, cq, lse, dl, crow, name, ex=None):
    bsz, seq, _ = qs.shape
    tq, tk = 256, min(BWD_KEYS, seq)
    nq, nk = seq // tq, seq // tk
    rows_c = ATT_ROWS * tq // tk
    peel = tk == 2 * tq and nq % 2 == 0
    npair = NH // 2

    def body(q_ref, k_ref, v_ref, do_ref, cq_ref, lse_ref, dl_ref, cr_ref, dq_ref, dk_ref, dv_ref, dc_ref,
             qm, dom, s_buf, dp_buf, pd_buf):
        head0 = lax.broadcasted_iota(jnp.int32, (1, LANES), 1) < HD
        zb = jnp.zeros((seq, LANES), BF)
        qm[0] = jnp.where(head0, q_ref[...], zb)
        qm[1] = jnp.where(head0, zb, q_ref[...])
        dom[0] = jnp.where(head0, do_ref[...], zb)
        dom[1] = jnp.where(head0, zb, do_ref[...])
        dq_ref[...] = jnp.zeros_like(dq_ref)

        def kloop(kj, _):
            krows = pl.ds(pl.multiple_of(kj * tk, tk), tk)
            kb = k_ref[krows, :]
            vb_t = v_ref[krows, :]
            zk = jnp.zeros_like(kb)
            km = (jnp.where(head0, kb, zk), jnp.where(head0, zk, kb))

            tile = lambda i: _tile_rows(i, tq)

            def scores(i, slot):
                for hh in range(2):
                    c0 = hh * HD
                    s_buf[slot, hh] = (_dot_nt(qm[hh, tile(i), :], kb)
                                       + (cq_ref[tile(i), c0:c0 + 1] - cr_ref[hh:hh + 1, krows]))
                    dp_buf[slot, hh] = _dot_nt(dom[hh, tile(i), :], vb_t)

            def products(slot, i, dk, dv):
                dq = jnp.zeros((tq, LANES), F32)
                for hh in range(2):
                    dsb = pd_buf[slot, hh, 1]
                    dv = dv + _dot_tn(pd_buf[slot, hh, 0], dom[hh, tile(i), :])
                    dk = dk + _dot_tn(dsb, qm[hh, tile(i), :])
                    dq = dq + _dot(dsb, km[hh])
                dq_ref[tile(i), :] += dq
                return dk, dv

            clamp = lambda i: jnp.clip(i, 0, nq - 1)

            def qstep(i, slot, carry, masked=True):
                dk, dv, cs = carry
                dk, dv = products(1 - slot, clamp(i - 1), dk, dv)
                cs = list(cs)
                ic = clamp(i)
                first_visible = jnp.where(i < nq, kj * tk - i * tq, 2 * tk)
                for r in range(tq // rows_c):
                    rows = slice(r * rows_c, (r + 1) * rows_c)
                    qrows = pl.ds(pl.multiple_of(ic * tq + r * rows_c, rows_c), rows_c)
                    visible = _causal_mask(rows_c, tk) >= first_visible - r * rows_c
                    for hh in range(2):
                        c0 = hh * HD
                        s = s_buf[slot, hh, rows, :]
                        if masked:
                            s = jnp.where(visible, s, -1e30)
                        p = jnp.exp(s - lse_ref[qrows, c0:c0 + 1])
                        ds = p * (dp_buf[slot, hh, rows, :] - dl_ref[qrows, c0:c0 + 1])
                        pd_buf[slot, hh, 0, rows, :] = p.astype(BF)
                        pd_buf[slot, hh, 1, rows, :] = ds.astype(BF)
                        cs[hh] = cs[hh] + jnp.sum(ds, axis=0, keepdims=True)
                scores(clamp(i + 1), 1 - slot)
                return dk, dv, tuple(cs)

            i0 = kj * (tk // tq)
            scores(i0, 0)
            pd_buf[1] = jnp.zeros((2, 2, tq, tk), BF)
            zero = jnp.zeros((tk, LANES), F32)
            zrow = jnp.zeros((1, tk), F32)
            trips = (nq - i0 + 1) // 2
            carry = (zero, zero, (zrow, zrow))
            if peel:
                carry = qstep(i0 + 1, 1, qstep(i0, 0, carry))
                dk, dv, cs = lax.fori_loop(
                    1, trips, lambda t, cy: qstep(i0 + 2 * t + 1, 1, qstep(i0 + 2 * t, 0, cy, False), False), carry)
            else:
                dk, dv, cs = lax.fori_loop(
                    0, trips, lambda t, cy: qstep(i0 + 2 * t + 1, 1, qstep(i0 + 2 * t, 0, cy)), carry)
            dk, dv = products(1, clamp(i0 + 2 * trips - 1), dk, dv)
            dk_ref[krows, :] = dk
            dv_ref[krows, :] = dv
            dc_ref[0:1, krows] = -cs[0]
            dc_ref[1:2, krows] = -cs[1]
            return 0

        lax.fori_loop(0, nk, kloop, 0)

    blk = pl.BlockSpec((None, seq, LANES), lambda b, h: (b, 0, h))
    rowblk = pl.BlockSpec((None, None, 2, seq), lambda b, h: (b, h, 0, 0))
    return _host_call(
        body, (bsz, npair), [blk, blk, blk, blk, blk, blk, blk, rowblk], [blk, blk, blk, rowblk],
        [jax.ShapeDtypeStruct((bsz, seq, DA), F32)] * 3 + [jax.ShapeDtypeStruct((bsz, npair, 2, seq), F32)],
        [pltpu.VMEM((2, seq, LANES), BF), pltpu.VMEM((2, seq, LANES), BF), pltpu.VMEM((2, 2, tq, tk), F32),
         pltpu.VMEM((2, 2, tq, tk), F32), pltpu.VMEM((2, 2, 2, tq, tk), BF)],
        (qs, kn, vb, do, cq, lse, dl, crow), name, ex)


def _bwd_attn_post(pq, pk, dqs, dkn, pf, dcrow, gq, gk, bf, name, ex=None):
    bsz, seq, _ = pq.shape
    ts = 512
    nt = seq // ts

    def body(q_ref, k_ref, dq_ref, dk_ref, f_ref, dc_ref, gq_ref, gk_ref, bf_ref,
             dpq_ref, dpk_ref, dpf_ref, dgq_ref, dgk_ref, dbf_ref, carry):
        @pl.when((pl.program_id(0) == 0) & (pl.program_id(1) == 0))
        def _():
            dgq_ref[...] = jnp.zeros_like(dgq_ref)
            dgk_ref[...] = jnp.zeros_like(dgk_ref)
            dbf_ref[...] = jnp.zeros_like(dbf_ref)

        @pl.when(pl.program_id(1) == 0)
        def _():
            carry[...] = jnp.zeros_like(carry)

        for x_ref, dx_ref, g_ref, dp_ref, dg_ref, scale in ((q_ref, dq_ref, gq_ref, dpq_ref, dgq_ref, 1.0 / 8.0),
                                                            (k_ref, dk_ref, gk_ref, dpk_ref, dgk_ref, 1.0)):
            for jb in range(DA // LANES):
                sl = slice(jb * LANES, (jb + 1) * LANES)
                gb = g_ref[:, sl]
                xhat, r, _ = _head_rms(x_ref[:, sl], gb)
                dn = dx_ref[:, sl] * scale
                dg_ref[:, sl] += jnp.sum(dn * xhat, axis=0, keepdims=True)
                dxh = dn * gb
                dp_ref[:, sl] = r * (dxh - xhat * (_seg_sum64(dxh * xhat) * (1.0 / HD)))

        ui = lax.broadcasted_iota(jnp.int32, (ts, ts), 0)
        tj = lax.broadcasted_iota(jnp.int32, (ts, ts), 1)
        tri = (tj >= ui).astype(BF)
        dc = jnp.concatenate([dc_ref[...], jnp.zeros((LANES - NH, ts), F32)], axis=0)
        hi, mid, lo = _split3(dc)
        dlf = _dot_nt(tri, hi) + _dot_nt(tri, mid) + _dot_nt(tri, lo) + carry[...]
        carry[...] = dlf[0:1, :]
        xf = f_ref[...] + bf_ref[...]
        lane = lax.broadcasted_iota(jnp.int32, (ts, LANES), 1)
        dfl = jnp.where(lane < NH, dlf * _sigmoid(-xf), 0.0)
        dpf_ref[...] = dfl
        dbf_ref[...] += jnp.sum(dfl, axis=0, keepdims=True)

    rev = lambda i: nt - 1 - i
    tile = lambda n: pl.BlockSpec((None, ts, n), lambda b, i: (b, rev(i), 0))
    vec = lambda n: pl.BlockSpec((1, n), lambda b, i: (0, 0))
    return _host_call(
        body, (bsz, nt),
        [tile(DA), tile(DA), tile(DA), tile(DA), tile(LANES),
         pl.BlockSpec((None, NH, ts), lambda b, i: (b, 0, rev(i))), vec(DA), vec(DA), vec(LANES)],
        [tile(DA), tile(DA), tile(LANES), vec(DA), vec(DA), vec(LANES)],
        [jax.ShapeDtypeStruct((bsz, seq, DA), F32), jax.ShapeDtypeStruct((bsz, seq, DA), F32),
         jax.ShapeDtypeStruct((bsz, seq, LANES), F32), jax.ShapeDtypeStruct((1, DA), F32),
         jax.ShapeDtypeStruct((1, DA), F32), jax.ShapeDtypeStruct((1, LANES), F32)],
        [pltpu.VMEM((1, LANES), F32)], (pq, pk, dqs, dkn, pf, dcrow, gq, gk, bf), name, ex)


def _bwd_conv(pcf, psc, dycs, cf_dw, cf_dw_b, ln_g, ln_b, cf_pw, sc_dw, name):
    bsz, seq, _ = pcf.shape
    ts = 512
    nt, tile, cur, halo, full = _conv_specs(bsz, seq, ts, -1)

    def body(cf_ref, ha_ref, hg_ref, sc_ref, hc_ref, hx_ref, dy_ref, dw_ref, b_ref, lg_ref, lb_ref, pw_ref, sdw_ref,
             dcf_ref, dsc_ref, ddw_ref, db_ref, dlg_ref, dlb_ref, dpw_ref, dsdw_ref,
             ubuf, usbuf, obuf, dubuf, dsbuf, carry_du, carry_dc):
        step = pl.program_id(1)
        first = tile(step) == 0

        @pl.when((pl.program_id(0) == 0) & (step == 0))
        def _():
            for r in (ddw_ref, db_ref, dlg_ref, dlb_ref, dpw_ref, dsdw_ref):
                r[...] = jnp.zeros_like(r)

        @pl.when(step == 0)
        def _():
            carry_du[...] = jnp.zeros((HALO, DC), F32)
            carry_dc[...] = jnp.zeros((HALO, DC), F32)

        def transposed_conv(d, carry, w_ref, taps_t):
            dubuf[0:ts, :] = d
            dubuf[ts:ts + HALO, :] = carry[...]
            carry[...] = d[0:HALO, :]
            _shift_copies(dubuf, dsbuf, sorted({off % SUBLANES for off in taps_t}))
            _tap_conv(dsbuf, w_ref, taps_t, obuf, ts, 0.0)
            return obuf[...]

        a = cf_ref[:, 0:256]
        g = cf_ref[:, 256:512]
        z = cf_ref[:, 512:768]
        lng = lg_ref[...]
        u0, n, rstd, u2, s2, u3 = _conformer_fwd(a, g, ha_ref[...], hg_ref[...], first, ubuf, usbuf, obuf, dw_ref,
                                                 b_ref[...], lng, lb_ref[...])
        u3b = u3.astype(BF)
        p = _dot(u3b, pw_ref[...])
        sz = _sigmoid(z)
        dy = dy_ref[:, 0:256]
        dcf_ref[:, 512:768] = dy * p * _dsilu(z, sz)
        dpb = (dy * (z * sz)).astype(BF)
        dpw_ref[...] += _dot_tn(u3b, dpb)
        du2 = _dot_nt(dpb, pw_ref[...]) * _dsilu(u2, s2)
        dlg_ref[...] += jnp.sum(du2 * n, axis=0, keepdims=True)
        dlb_ref[...] += jnp.sum(du2, axis=0, keepdims=True)
        dn = du2 * lng
        du1 = rstd * (dn - jnp.mean(dn, axis=-1, keepdims=True) - n * jnp.mean(dn * n, axis=-1, keepdims=True))
        db_ref[...] += jnp.sum(du1, axis=0, keepdims=True)
        du0 = transposed_conv(du1, carry_du, dw_ref, CF_TAPS_T)
        for k, tap_grad in enumerate(_tap_sums(dubuf, usbuf, CF_TAPS, ts)):
            ddw_ref[k:k + 1, :] += tap_grad
        sg = _sigmoid(g)
        dcf_ref[:, 0:256] = du0 * sg
        dcf_ref[:, 256:512] = du0 * a * sg * (1.0 - sg)

        bb = sc_ref[:, 0:256]
        c = sc_ref[:, 256:512]
        xs = sc_ref[:, 512:768]
        zs = sc_ref[:, 768:1024]
        cv = _shortconv_fwd(c, xs, hc_ref[...], hx_ref[...], first, ubuf, usbuf, obuf, sdw_ref)
        szs = _sigmoid(zs)
        dys = dy_ref[:, 256:512]
        gate = zs * szs
        dsc_ref[:, 0:256] = dys * cv * gate
        dsc_ref[:, 768:1024] = dys * bb * cv * _dsilu(zs, szs)
        dm = transposed_conv(dys * bb * gate, carry_dc, sdw_ref, SC_TAPS_T)
        for k, tap_grad in enumerate(_tap_sums(dubuf, usbuf, SC_TAPS, ts)):
            dsdw_ref[k:k + 1, :] += tap_grad
        dsc_ref[:, 256:512] = dm * xs
        dsc_ref[:, 512:768] = dm * c

    outt = lambda n: pl.BlockSpec((None, ts, n), lambda b, i: (b, tile(i), 0))
    return pl.pallas_call(
        body, grid=(bsz, nt),
        in_specs=[cur(768), halo(0), halo(1), cur(1024), halo(1), halo(2), cur(512),
                  full(CFW, DC), full(1, DC), full(1, DC), full(1, DC), full(DC, DC), full(SCW, DC)],
        out_specs=[outt(768), outt(1024), full(CFW, DC), full(1, DC), full(1, DC), full(1, DC), full(DC, DC),
                   full(SCW, DC)],
        out_shape=[jax.ShapeDtypeStruct((bsz, seq, 768), F32), jax.ShapeDtypeStruct((bsz, seq, 1024), F32),
                   jax.ShapeDtypeStruct((CFW, DC), F32), jax.ShapeDtypeStruct((1, DC), F32),
                   jax.ShapeDtypeStruct((1, DC), F32), jax.ShapeDtypeStruct((1, DC), F32),
                   jax.ShapeDtypeStruct((DC, DC), F32), jax.ShapeDtypeStruct((SCW, DC), F32)],
        scratch_shapes=[pltpu.VMEM((HALO + ts, DC), F32), pltpu.VMEM((SUBLANES, HALO + ts, DC), F32),
                        pltpu.VMEM((ts, DC), F32), pltpu.VMEM((ts + HALO, DC), F32),
                        pltpu.VMEM((SUBLANES, ts + HALO, DC), F32), pltpu.VMEM((HALO, DC), F32),
                        pltpu.VMEM((HALO, DC), F32)],
        compiler_params=_cparams(("arbitrary", "arbitrary")), name=name,
    )(pcf, pcf, pcf, psc, psc, psc, dycs, cf_dw, cf_dw_b, ln_g, ln_b, cf_pw, sc_dw)


def _bwd_inproj(x, g, dyres, w, dcf, dsc, dq, dk, dv, dz, df, name):
    t = x.shape[0]
    tm = 256
    pieces = ((C_CF, 768), (C_SC, 1024), (C_Q, DA), (C_K, DA), (C_V, DA), (C_Z, DA), (C_F, LANES))

    def body(x_ref, g_ref, dy_ref, w_hbm, dcf_ref, dsc_ref, dq_ref, dk_ref, dv_ref, dz_ref, df_ref,
             dx_ref, dg_ref, dw_hbm, w_vmem, dw_acc, sem, stage, out_sems):
        i = pl.program_id(0)

        @pl.when(i == 0)
        def _():
            cp = pltpu.make_async_copy(w_hbm, w_vmem, sem)
            cp.start()
            dw_acc[...] = jnp.zeros_like(dw_acc)
            dg_ref[...] = jnp.zeros_like(dg_ref)
            cp.wait()

        xt = x_ref[...]
        gg = g_ref[...]
        r = lax.rsqrt(jnp.mean(xt * xt, axis=-1, keepdims=True) + EPS)
        xhat = xt * r
        ht = (xhat * gg).astype(BF).T
        dh = jnp.zeros((tm, D), F32)
        for (c0, n), ref in zip(pieces, (dcf_ref, dsc_ref, dq_ref, dk_ref, dv_ref, dz_ref, df_ref)):
            for s0 in range(0, n, 512):
                s1 = min(s0 + 512, n)
                d = ref[:, s0:s1].astype(BF)
                dh = dh + _dot_nt(d, w_vmem[:, c0 + s0:c0 + s1])
                dw_acc[:, c0 + s0:c0 + s1] += _dot(ht, d)
        dg_ref[...] += jnp.sum(dh * xhat, axis=0, keepdims=True)
        dhg = dh * gg
        dx_ref[...] = dy_ref[...] + r * (dhg - xhat * jnp.mean(dhg * xhat, axis=-1, keepdims=True))

        @pl.when(i == pl.num_programs(0) - 1)
        def _():
            rows_p = D // 4
            pieces_out = [(k, q) for k in range(4) for q in range(4)]
            copies = []
            for p, (k, q) in enumerate(pieces_out):
                slot = p % 2
                if p >= 2:
                    copies[p - 2].wait()

                def chunk(c, _, k=k, q=q, slot=slot):
                    r0 = pl.multiple_of(c * 64, 64)
                    stage[slot, pl.ds(r0, 64), :] = dw_acc[pl.ds(q * rows_p + r0, 64), NSH * k:NSH * (k + 1)]
                    return 0

                lax.fori_loop(0, rows_p // 64, chunk, 0)
                out = pltpu.make_async_copy(stage.at[slot], dw_hbm.at[k, q * rows_p:(q + 1) * rows_p, :],
                                            out_sems.at[slot])
                out.start()
                copies.append(out)
            copies[-2].wait()
            copies[-1].wait()

    row = lambda n: pl.BlockSpec((tm, n), lambda i: (i, 0))
    anyspec = pl.BlockSpec(memory_space=pl.ANY)
    return pl.pallas_call(
        body, grid=(t // tm,),
        in_specs=[row(D), pl.BlockSpec((1, D), lambda i: (0, 0)), row(D), anyspec,
                  row(768), row(1024), row(DA), row(DA), row(DA), row(DA), row(LANES)],
        out_specs=[row(D), pl.BlockSpec((1, D), lambda i: (0, 0)), anyspec],
        out_shape=[jax.ShapeDtypeStruct((t, D), F32), jax.ShapeDtypeStruct((1, D), F32),
                   jax.ShapeDtypeStruct((4, D, NSH), F32)],
        scratch_shapes=[pltpu.VMEM((D, NP), BF), pltpu.VMEM((D, NP), F32), pltpu.SemaphoreType.DMA,
                        pltpu.VMEM((2, D // 4, NSH), F32), pltpu.SemaphoreType.DMA((2,))],
        compiler_params=_cparams(("arbitrary",)), name=name)(x, g, dyres, w, dcf, dsc, dq, dk, dv, dz, df)


def _adamw_update(w_ref, g_ref, m_ref, v_ref, d_ref, m2_ref, v2_ref):
    gt = g_ref[...]
    m2 = ADAM_B1 * m_ref[...] + (1.0 - ADAM_B1) * gt
    v2 = ADAM_B2 * v_ref[...] + (1.0 - ADAM_B2) * (gt * gt)
    m_hat = m2 / (1.0 - ADAM_B1 ** ADAM_STEP)
    v_hat = v2 / (1.0 - ADAM_B2 ** ADAM_STEP)
    d_ref[...] = -ADAM_LR * (m_hat / (jnp.sqrt(v_hat) + ADAM_EPS) + ADAM_WD * w_ref[...])
    m2_ref[...] = m2
    v2_ref[...] = v2


def _adamw_many(ws, gs, ms, vs, name):
    n = len(ws)

    def body(*refs):
        ins, outs = refs[:4 * n], refs[4 * n:]
        for i in range(n):
            _adamw_update(ins[i], ins[n + i], ins[2 * n + i], ins[3 * n + i], *outs[3 * i:3 * i + 3])

    return pl.pallas_call(
        body, out_shape=[jax.ShapeDtypeStruct(w.shape, F32) for w in ws for _ in range(3)],
        name=name)(*ws, *gs, *ms, *vs)


def _adamw(w, g, m, v, tr, name):
    rows, cols = w.shape

    def body(w_ref, g_ref, m_ref, v_ref, d_ref, m2_ref, v2_ref, g_out_ref):
        _adamw_update(w_ref, g_ref, m_ref, v_ref, d_ref, m2_ref, v2_ref)
        g_out_ref[...] = g_ref[...]

    blk = pl.BlockSpec((tr, cols), lambda i: (i, 0))
    return pl.pallas_call(
        body, grid=(rows // tr,), in_specs=[blk] * 4, out_specs=[blk] * 4,
        out_shape=[jax.ShapeDtypeStruct((rows, cols), F32)] * 4,
        compiler_params=_cparams(("arbitrary",)), name=name)(w, g, m, v)


def _place():
    x, y, c = lax.axis_index("x"), lax.axis_index("y"), lax.axis_index("c")
    chips = [(1 - x, y), (x, 1 - y), (1 - x, 1 - y)]
    return x, y, c, chips


def _halves(c, rows, align):
    rh = rows // 2
    return pl.ds(pl.multiple_of(c * rh, align), rh), pl.ds(pl.multiple_of((1 - c) * rh, align), rh)


ANY = pl.BlockSpec(memory_space=pl.ANY)


class _Exchange:
    def __init__(self, operands, out_shape, sems, start, wait, relay=None):
        self.operands, self.out_shape, self.sems, self.start, self.wait = operands, out_shape, sems, start, wait
        self.relay = relay


class _StagedCopy:
    def __init__(self, src, dst, stage, sem):
        self.src, self.dst, self.stage, self.sem = src, dst, stage, sem

    def start(self):
        pltpu.make_async_copy(self.src, self.stage, self.sem).start()

    def relay(self):
        pltpu.make_async_copy(self.src, self.stage, self.sem).wait()
        pltpu.make_async_copy(self.stage, self.dst, self.sem).start()

    def finish(self):
        pltpu.make_async_copy(self.stage, self.dst, self.sem).wait()


def _run_exchange(ex, name):
    n_in, n_out = len(ex.operands), len(ex.out_shape)

    def body(*refs):
        ins, outs, sems = refs[:n_in], refs[n_in:n_in + n_out], refs[n_in + n_out:]
        ex.start(ins, outs, sems)
        if ex.relay:
            ex.relay(ins, outs, sems)
        ex.wait(ins, outs, sems)

    return pl.pallas_call(body, in_specs=[ANY] * n_in, out_specs=[ANY] * n_out, out_shape=ex.out_shape,
                          scratch_shapes=ex.sems, name=name)(*ex.operands)


def _host_call(body, grid, in_specs, out_specs, out_shape, scratch, operands, name, ex=None):
    sem = ("arbitrary",) * len(grid)
    if ex is None:
        outs = pl.pallas_call(body, grid=grid, in_specs=in_specs, out_specs=out_specs, out_shape=out_shape,
                              scratch_shapes=scratch, compiler_params=_cparams(sem), name=name)(*operands)
        return outs, None
    n_in, n_out, n_scr = len(in_specs), len(out_specs), len(scratch)
    xi, xo = len(ex.operands), len(ex.out_shape)

    def hosted(*refs):
        ins, xins = refs[:n_in], refs[n_in:n_in + xi]
        o0 = n_in + xi
        outs, xouts = refs[o0:o0 + n_out], refs[o0 + n_out:o0 + n_out + xo]
        s0 = o0 + n_out + xo
        scr, xsems = refs[s0:s0 + n_scr], refs[s0 + n_scr:]
        first = pl.program_id(0) == 0
        half = pl.program_id(0) == grid[0] // 2
        last = pl.program_id(0) == pl.num_programs(0) - 1
        for d in range(1, len(grid)):
            first = first & (pl.program_id(d) == 0)
            half = half & (pl.program_id(d) == 0)
            last = last & (pl.program_id(d) == pl.num_programs(d) - 1)

        @pl.when(first)
        def _():
            ex.start(xins, xouts, xsems)

        if ex.relay:
            @pl.when(half)
            def _():
                ex.relay(xins, xouts, xsems)

        body(*ins, *outs, *scr)

        @pl.when(last)
        def _():
            ex.wait(xins, xouts, xsems)

    res = pl.pallas_call(
        hosted, grid=grid, in_specs=list(in_specs) + [ANY] * xi, out_specs=list(out_specs) + [ANY] * xo,
        out_shape=list(out_shape) + list(ex.out_shape), scratch_shapes=list(scratch) + list(ex.sems),
        compiler_params=_cparams(sem), name=name)(*operands, *ex.operands)
    return res[:n_out], res[n_out:]


def _all_gather_chips(shards):
    n = len(shards)

    def copies(srcs, outs, sems):
        send_sems, recv_sems, local_sems, *stage = sems
        x, y, c, chips = _place()
        me = 2 * x + y
        halves = [_halves(c, s.shape[1], 16) for s in shards]
        local = [_StagedCopy(srcs[i], outs[i].at[:, me], stage[i], local_sems.at[i]) for i in range(n)]

        def ici(i, j, chip, k):
            mine = halves[i][0]
            return pltpu.make_async_remote_copy(
                src_ref=srcs[i].at[:, mine, :], dst_ref=outs[i].at[:, k, mine, :], send_sem=send_sems.at[6 * i + j],
                recv_sem=recv_sems.at[6 * i + j], device_id=(*chip, c), device_id_type=MESH)

        def d2d(i, j, k, half):
            return pltpu.make_async_remote_copy(
                src_ref=outs[i].at[:, k, half, :], dst_ref=outs[i].at[:, k, half, :], send_sem=send_sems.at[6 * i + 3 + j],
                recv_sem=recv_sems.at[6 * i + 3 + j], device_id=(x, y, 1 - c), device_id_type=MESH)

        return me, chips, halves, local, ici, d2d

    def start(srcs, outs, sems):
        me, chips, _, local, ici, _ = copies(srcs, outs, sems)
        for cp in local:
            cp.start()
        for i in range(n):
            for j, chip in enumerate(chips):
                ici(i, j, chip, me).start()

    def relay(srcs, outs, sems):
        me, chips, halves, local, ici, d2d = copies(srcs, outs, sems)
        for cp in local:
            cp.relay()
        for i in range(n):
            for j, (px, py) in enumerate(chips):
                k = 2 * px + py
                ici(i, j, (px, py), k).wait_recv()
                d2d(i, j, k, halves[i][0]).start()

    def wait(srcs, outs, sems):
        me, chips, halves, local, ici, d2d = copies(srcs, outs, sems)
        for i in range(n):
            for j, (px, py) in enumerate(chips):
                d2d(i, j, 2 * px + py, halves[i][1]).wait_recv()
        for i in range(n):
            for j, (px, py) in enumerate(chips):
                ici(i, j, (px, py), me).wait_send()
                d2d(i, j, 2 * px + py, halves[i][0]).wait_send()
        for cp in local:
            cp.finish()

    return _Exchange(
        list(shards), [jax.ShapeDtypeStruct((s.shape[0], 4) + s.shape[1:], s.dtype) for s in shards],
        [pltpu.SemaphoreType.DMA((6 * n,)), pltpu.SemaphoreType.DMA((6 * n,)), pltpu.SemaphoreType.DMA((n,))]
        + [pltpu.VMEM(s.shape, s.dtype) for s in shards], start, wait, relay)


def _rs_swap_halves(gs):
    n = len(gs)

    def copies(srcs, gots, sems):
        send_sems, recv_sems = sems
        x, y, c, _ = _place()
        return [pltpu.make_async_remote_copy(
            src_ref=srcs[i].at[:, :, _halves(c, gs[i].shape[2], 8)[1], :], dst_ref=gots[i], send_sem=send_sems.at[i],
            recv_sem=recv_sems.at[i], device_id=(x, y, 1 - c), device_id_type=MESH) for i in range(n)]

    def start(srcs, gots, sems):
        for cp in copies(srcs, gots, sems):
            cp.start()

    def wait(srcs, gots, sems):
        for cp in copies(srcs, gots, sems):
            cp.wait()

    return _Exchange(
        list(gs), [jax.ShapeDtypeStruct(g.shape[:2] + (g.shape[2] // 2, g.shape[3]), g.dtype) for g in gs],
        [pltpu.SemaphoreType.DMA((n,)), pltpu.SemaphoreType.DMA((n,))], start, wait)


def _rs_add_pair(g, got, cidx, tb, out_dtype, name):
    nl, _, rh, cols = got.shape
    nb = rh // tb

    def body(c_ref, g_ref, o_ref, s_ref):
        s_ref[...] = (g_ref[...] + o_ref[...]).astype(out_dtype)

    blk = lambda half: pl.BlockSpec((None, None, tb, cols), (lambda l, k, i, c: (l, k, c[0] * nb + i, 0)) if half
                                    else (lambda l, k, i, c: (l, k, i, 0)))
    return pl.pallas_call(
        body,
        grid_spec=pltpu.PrefetchScalarGridSpec(num_scalar_prefetch=1, grid=(nl, 4, nb),
                                               in_specs=[blk(True), blk(False)], out_specs=blk(False)),
        out_shape=jax.ShapeDtypeStruct(got.shape, out_dtype),
        compiler_params=_cparams(("arbitrary", "arbitrary", "arbitrary")), name=name)(cidx, g, got)


def _rs_exchange_chips(pairs):
    n = len(pairs)

    def copies(srcs, gots, sems):
        send_sems, recv_sems, local_sems, *stage = sems
        x, y, c, chips = _place()
        me = 2 * x + y
        local = [_StagedCopy(srcs[i].at[:, me], gots[i].at[:, me], stage[i], local_sems.at[i]) for i in range(n)]

        def ici(i, j, chip, frm, to):
            return pltpu.make_async_remote_copy(
                src_ref=srcs[i].at[:, to], dst_ref=gots[i].at[:, frm], send_sem=send_sems.at[3 * i + j],
                recv_sem=recv_sems.at[3 * i + j], device_id=(*chip, c), device_id_type=MESH)

        sent = [ici(i, j, (px, py), me, 2 * px + py) for i in range(n) for j, (px, py) in enumerate(chips)]
        recvd = [ici(i, j, (px, py), 2 * px + py, me) for i in range(n) for j, (px, py) in enumerate(chips)]
        return local, sent, recvd

    def start(srcs, gots, sems):
        local, sent, _ = copies(srcs, gots, sems)
        for cp in local + sent:
            cp.start()

    def relay(srcs, gots, sems):
        for cp in copies(srcs, gots, sems)[0]:
            cp.relay()

    def wait(srcs, gots, sems):
        local, sent, recvd = copies(srcs, gots, sems)
        for cp in recvd:
            cp.wait_recv()
        for cp in sent:
            cp.wait_send()
        for cp in local:
            cp.finish()

    return _Exchange(
        list(pairs), [jax.ShapeDtypeStruct(p.shape, p.dtype) for p in pairs],
        [pltpu.SemaphoreType.DMA((3 * n,)), pltpu.SemaphoreType.DMA((3 * n,)), pltpu.SemaphoreType.DMA((n,))]
        + [pltpu.VMEM((p.shape[0],) + p.shape[2:], p.dtype) for p in pairs], start, wait, relay)


def _rs_add_chips(got, tb, name):
    nl, _, rh, cols = got.shape

    def body(g_ref, s_ref):
        s_ref[...] = ((g_ref[0].astype(F32) + g_ref[1].astype(F32)) + g_ref[2].astype(F32)) + g_ref[3].astype(F32)

    return pl.pallas_call(
        body, grid=(nl, rh // tb),
        in_specs=[pl.BlockSpec((None, 4, tb, cols), lambda l, i: (l, 0, i, 0))],
        out_specs=pl.BlockSpec((None, tb, cols), lambda l, i: (l, i, 0)),
        out_shape=jax.ShapeDtypeStruct((nl, rh, cols), F32),
        compiler_params=_cparams(("arbitrary", "arbitrary")), name=name)(got)


def _rs_join_halves(groups):
    totals = [t for grp in groups for t in grp]
    where = [(gi, li) for gi, grp in enumerate(groups) for li in range(len(grp))]
    n = len(totals)

    def copies(srcs, outs, sems):
        send_sems, recv_sems, local_sems, *stage = sems
        x, y, c, _ = _place()
        local, send, recv = [], [], []
        for i, (gi, li) in enumerate(where):
            mine, other = _halves(c, 2 * totals[i].shape[1], 8)
            local.append(_StagedCopy(srcs[i], outs[gi].at[li:li + 1, mine, :], stage[i], local_sems.at[i]))
            for lst, rows in ((send, mine), (recv, other)):
                lst.append(pltpu.make_async_remote_copy(
                    src_ref=srcs[i], dst_ref=outs[gi].at[li:li + 1, rows, :], send_sem=send_sems.at[i],
                    recv_sem=recv_sems.at[i], device_id=(x, y, 1 - c), device_id_type=MESH))
        return local, send, recv

    def start(srcs, outs, sems):
        local, send, _ = copies(srcs, outs, sems)
        for cp in local + send:
            cp.start()

    def relay(srcs, outs, sems):
        for cp in copies(srcs, outs, sems)[0]:
            cp.relay()

    def wait(srcs, outs, sems):
        local, send, recv = copies(srcs, outs, sems)
        for cp in recv:
            cp.wait_recv()
        for cp in send:
            cp.wait_send()
        for cp in local:
            cp.finish()

    return _Exchange(
        totals, [jax.ShapeDtypeStruct((len(grp), 2 * grp[0].shape[1], grp[0].shape[2]), F32) for grp in groups],
        [pltpu.SemaphoreType.DMA((n,)), pltpu.SemaphoreType.DMA((n,)), pltpu.SemaphoreType.DMA((n,))]
        + [pltpu.VMEM(t.shape, F32) for t in totals], start, wait, relay)


def _rows(a):
    return a.reshape(-1, LANES)


def _pad_rows(a, mult):
    r = (-a.shape[0]) % mult
    return a if r == 0 else jnp.concatenate([a, jnp.zeros((r, a.shape[1]), a.dtype)], axis=0)


def kernel(x, norm_g, w_in, b_f, cf_dw, cf_dw_b, cf_ln_g, cf_ln_b, cf_pw, sc_dw, q_norm_g, k_norm_g, w_out, loss_target, m_norm_g, m_w_in, m_b_f, m_cf_dw, m_cf_dw_b, m_cf_ln_g, m_cf_ln_b, m_cf_pw, m_sc_dw, m_q_norm_g, m_k_norm_g, m_w_out, v_norm_g, v_w_in, v_b_f, v_cf_dw, v_cf_dw_b, v_cf_ln_g, v_cf_ln_b, v_cf_pw, v_sc_dw, v_q_norm_g, v_k_norm_g, v_w_out):
    bsz, seq, _ = x.shape
    t = bsz * seq

    taps = jnp.concatenate([cf_dw.reshape(-1), sc_dw.reshape(-1)])
    taps_bf = lax.bitcast_convert_type(taps, jnp.bfloat16).reshape(-1, LANES)
    n_pw, n_taps = 2 * 64 * DC // LANES, taps_bf.shape[0]
    small_w = _pad_rows(jnp.concatenate([_rows(cf_pw.astype(BF)), taps_bf], axis=0), 32)[None]
    win_b, wout_b = w_in.astype(BF), w_out.astype(BF)
    win0_g, small_g = _run_exchange(_all_gather_chips([win_b[0:1], small_w]), "all_gather_first")
    gather_rest = _all_gather_chips([win_b[1:DEPTH], wout_b])
    w_in_full = [_assemble_w_in(win0_g[0], "assemble_w_in_0")] + [None] * (DEPTH - 1)
    w_out_full = None
    cf_pw_full = jnp.concatenate([small_g[0, k, 0:n_pw].reshape(DEPTH, 64, DC) for k in range(4)], axis=1)
    taps_all = [lax.bitcast_convert_type(small_g[0, k, n_pw:n_pw + n_taps].reshape(-1, 2), F32) for k in range(4)]
    n_cfdw = DEPTH * CFW * 64
    cf_dw_full = jnp.concatenate([tk[:n_cfdw].reshape(DEPTH, CFW, 64) for tk in taps_all], axis=-1)
    sc_dw_full = jnp.concatenate([tk[n_cfdw:].reshape(DEPTH, SCW, 64) for tk in taps_all], axis=-1)

    bf_pad = jnp.pad(b_f, ((0, 0), (0, LANES - NH)))

    xs = [x.reshape(t, D)]
    saved = []
    dy = loss_part = None
    for l in range(DEPTH):
        xl = xs[-1]
        gq, gk = q_norm_g[l].reshape(1, DA), k_norm_g[l].reshape(1, DA)
        pcf, psc, pq, pk, vb, pz, pf = _fwd_inproj(xl, norm_g[l][None], w_in_full[l], f"fwd_inproj_{l}")
        b3 = lambda a: a.reshape(bsz, seq, a.shape[-1])
        ycs = _fwd_conv(b3(pcf), b3(psc), cf_dw_full[l], cf_dw_b[l][None], cf_ln_g[l][None], cf_ln_b[l][None],
                        cf_pw_full[l], sc_dw_full[l], f"fwd_conv_{l}")
        qs, kn, cq, crow8 = _fwd_attn_prep(b3(pq), b3(pk), b3(pf), gq, gk, bf_pad[l][None], f"fwd_attn_prep_{l}")
        crow = crow8.reshape(bsz, NH // 2, 2, seq)
        (o, ohp, yatt, lse), landed = _fwd_attn(qs, kn, b3(vb), cq, crow, b3(pz), f"fwd_attn_{l}",
                                                gather_rest if l == 0 else None)
        if l == 0:
            win_rest_g, wout_g = landed
            w_in_full[1:] = [_assemble_w_in(win_rest_g[i], f"assemble_w_in_{i + 1}") for i in range(DEPTH - 1)]
            w_out_full = wout_g.reshape(DEPTH, D, D)
        ycs2, yatt2 = ycs.reshape(t, 512), yatt.reshape(t, DA)
        if l + 1 < DEPTH:
            xs.append(_fwd_outproj(xl, ycs2, yatt2, w_out_full[l], f"fwd_outproj_{l}"))
        else:
            dy, loss_part = _fwd_outproj_loss(xl, ycs2, yatt2, w_out_full[l], loss_target.reshape(t, D),
                                              f"fwd_outproj_loss_{l}")
        saved.append((pcf, psc, pq, pk, vb, pz, pf, ycs2, yatt2, o, ohp, qs, kn, cq, crow, lse))

    grads = [None] * DEPTH
    reduced = [None] * DEPTH
    cidx = lax.axis_index("c").astype(jnp.int32).reshape(1)
    big_tiles = [256, 128]

    def big_packs(gl):
        return [gl["w_in"][None], gl["w_out"].reshape(1, 4, 256, D)]

    def add_pairs(packs, gots, tiles, dtypes, tag):
        return [_rs_add_pair(p, got, cidx, tb, dt, f"rs_add_pair_{tag}_{i}")
                for i, (p, got, tb, dt) in enumerate(zip(packs, gots, tiles, dtypes))]

    def add_chips(gots2, tiles, tag):
        return [_rs_add_chips(got, tb, f"rs_add_chips_{tag}_{i}") for i, (got, tb) in enumerate(zip(gots2, tiles))]

    for l in reversed(range(DEPTH)):
        pcf, psc, pq, pk, vb, pz, pf, ycs2, yatt2, o, ohp, qs, kn, cq, crow, lse = saved[l]
        gq, gk = q_norm_g[l].reshape(1, DA), k_norm_g[l].reshape(1, DA)
        b3 = lambda a: a.reshape(bsz, seq, a.shape[-1])
        f2 = lambda a: a.reshape(t, a.shape[-1])
        packs = big_packs(grads[l + 1]) if l + 1 < DEPTH else None
        (dycs, do, dz, dl, d_wout), gots = _bwd_outproj(
            dy, ycs2, yatt2, f2(o), f2(ohp), pz, w_out_full[l], f"bwd_outproj_{l}",
            _rs_swap_halves(packs) if packs else None)
        pairs = add_pairs(packs, gots, big_tiles, [BF, BF], l + 1) if packs else None
        (dqs, dkn, dv, dcrow), gots2 = _bwd_attn(qs, kn, b3(vb), b3(do), cq, lse, b3(dl), crow, f"bwd_attn_{l}",
                                                 _rs_exchange_chips(pairs) if packs else None)
        if packs:
            reduced[l + 1] = add_chips(gots2, big_tiles, l + 1)
        (dpq, dpk, dpf, d_gq, d_gk, d_bf), _ = _bwd_attn_post(
            b3(pq), b3(pk), dqs, dkn, b3(pf), dcrow.reshape(bsz, NH, seq), gq, gk, bf_pad[l][None],
            f"bwd_attn_post_{l}")
        dcf, dsc, d_cfdw, d_cfb, d_lng, d_lnb, d_pw, d_scdw = _bwd_conv(
            b3(pcf), b3(psc), b3(dycs), cf_dw_full[l], cf_dw_b[l][None], cf_ln_g[l][None], cf_ln_b[l][None],
            cf_pw_full[l], sc_dw_full[l], f"bwd_conv_{l}")
        dy, d_ng, d_win = _bwd_inproj(xs[l], norm_g[l][None], dy, w_in_full[l], f2(dcf), f2(dsc), f2(dpq), f2(dpk),
                                      f2(dv), dz, f2(dpf), f"bwd_inproj_{l}")
        grads[l] = dict(norm_g=d_ng[0], w_in=d_win, b_f=d_bf[0, :NH], cf_dw=d_cfdw, cf_dw_b=d_cfb[0],
                        cf_ln_g=d_lng[0], cf_ln_b=d_lnb[0], cf_pw=d_pw, sc_dw=d_scdw,
                        q_norm_g=d_gq.reshape(NH, HD), k_norm_g=d_gk.reshape(NH, HD), w_out=d_wout)
    grad_x = dy.reshape(bsz, seq, D)
    gw = {n: jnp.stack([grads[l][n] for l in range(DEPTH)]) for n in grads[0] if n not in ("w_in", "w_out")}

    rep_names = ("norm_g", "b_f", "cf_dw_b", "cf_ln_g", "cf_ln_b", "q_norm_g", "k_norm_g")
    rep = jnp.concatenate([jnp.pad(gw[n].reshape(-1), (0, (-gw[n].size) % LANES)) for n in rep_names]
                          + [jnp.pad(loss_part.reshape(-1), (0, LANES - 1))]).reshape(-1, LANES)
    blocks = []
    for k in range(4):
        small = jnp.concatenate([gw["cf_dw"][:, :, 64 * k:64 * (k + 1)].reshape(-1),
                                 gw["sc_dw"][:, :, 64 * k:64 * (k + 1)].reshape(-1)]).reshape(-1, LANES)
        blocks.append(_pad_rows(jnp.concatenate([_rows(gw["cf_pw"][:, 64 * k:64 * (k + 1), :]), small, rep], axis=0), 16))
    g_small = jnp.stack(blocks)[None]
    packs = big_packs(grads[0]) + [g_small]
    tiles = big_tiles + [g_small.shape[2] // 2]
    gots = _run_exchange(_rs_swap_halves(packs), "rs_swap_halves_last")
    pairs = add_pairs(packs, gots, tiles, [BF, BF, F32], 0)
    gots2 = _run_exchange(_rs_exchange_chips(pairs), "rs_exchange_chips_last")
    tot_win0, tot_wout0, tot_small = add_chips(gots2, tiles, 0)
    reduced[0] = (tot_win0, tot_wout0)
    red_win, red_wout, red_small = _run_exchange(
        _rs_join_halves([[reduced[l][0] for l in range(DEPTH)], [reduced[l][1] for l in range(DEPTH)], [tot_small]]),
        "rs_join_halves")

    n_small = (DEPTH * CFW * 64 + DEPTH * SCW * 64) // LANES
    red = red_small[0]
    pos = [0]

    def take(nrows):
        pos[0] += nrows
        return red[pos[0] - nrows:pos[0]]

    g = {}
    g["w_in"] = red_win
    g["w_out"] = red_wout
    g["cf_pw"] = take(n_pw).reshape(DEPTH, 64, DC)
    small = take(n_small).reshape(-1)
    g["cf_dw"] = small[:n_cfdw].reshape(DEPTH, CFW, 64)
    g["sc_dw"] = small[n_cfdw:].reshape(DEPTH, SCW, 64)
    shapes = dict(norm_g=norm_g.shape, b_f=b_f.shape, cf_dw_b=cf_dw_b.shape, cf_ln_g=cf_ln_g.shape,
                  cf_ln_b=cf_ln_b.shape, q_norm_g=q_norm_g.shape, k_norm_g=k_norm_g.shape)
    for n in rep_names:
        size = 1
        for s in shapes[n]:
            size *= s
        g[n] = take(-(-size // LANES)).reshape(-1)[:size].reshape(shapes[n])
    loss = take(1)[0, 0]

    order = ("norm_g", "w_in", "b_f", "cf_dw", "cf_dw_b", "cf_ln_g", "cf_ln_b", "cf_pw", "sc_dw", "q_norm_g",
             "k_norm_g", "w_out")
    weights = dict(norm_g=norm_g, w_in=w_in, b_f=b_f, cf_dw=cf_dw, cf_dw_b=cf_dw_b, cf_ln_g=cf_ln_g, cf_ln_b=cf_ln_b,
                   cf_pw=cf_pw, sc_dw=sc_dw, q_norm_g=q_norm_g, k_norm_g=k_norm_g, w_out=w_out)
    ms = dict(norm_g=m_norm_g, w_in=m_w_in, b_f=m_b_f, cf_dw=m_cf_dw, cf_dw_b=m_cf_dw_b, cf_ln_g=m_cf_ln_g,
              cf_ln_b=m_cf_ln_b, cf_pw=m_cf_pw, sc_dw=m_sc_dw, q_norm_g=m_q_norm_g, k_norm_g=m_k_norm_g, w_out=m_w_out)
    vs = dict(norm_g=v_norm_g, w_in=v_w_in, b_f=v_b_f, cf_dw=v_cf_dw, cf_dw_b=v_cf_dw_b, cf_ln_g=v_cf_ln_g,
              cf_ln_b=v_cf_ln_b, cf_pw=v_cf_pw, sc_dw=v_sc_dw, q_norm_g=v_q_norm_g, k_norm_g=v_k_norm_g, w_out=v_w_out)
    delta, new_m, new_v = {}, {}, {}
    two_d = lambda a: a.reshape(-1, a.shape[-1])
    for n in ("w_in", "w_out"):
        outs = _adamw(two_d(weights[n]), two_d(g[n]), two_d(ms[n]), two_d(vs[n]), 256, f"adamw_{n}")
        delta[n], new_m[n], new_v[n], g[n] = (a.reshape(weights[n].shape) for a in outs)
    small_names = [n for n in order if n not in ("w_in", "w_out")]
    outs = _adamw_many([weights[n] for n in small_names], [g[n] for n in small_names],
                       [ms[n] for n in small_names], [vs[n] for n in small_names], "adamw_small")
    for i, n in enumerate(small_names):
        delta[n], new_m[n], new_v[n] = outs[3 * i:3 * i + 3]

    return (loss, grad_x, *[g[n] for n in order], *[delta[n] for n in order], *[new_m[n] for n in order],
            *[new_v[n] for n in order])
```

```python
import jax
import jax.numpy as jnp
from jax import lax
from jax.experimental import pallas as pl
from jax.experimental.pallas import tpu as pltpu

F32 = jnp.float32
BF = jnp.bfloat16
MESH = pl.DeviceIdType.MESH

DEPTH = 2
D = 1024
DC = 256
DA = 512
NH = 8
HD = 64
CFW = 31
SCW = 3
N_IN = 3848
NP = 3968
NSH = N_IN // 4
HALO = 32
ATT_ROWS = 128
BWD_KEYS = 512
FWD_ROWS = 256
SUBLANES = 8
CONV_ROWS = 64
TAP_GROUP = 16
EPS = 1e-6
LANES = 128
VMEM_LIMIT = 56 * 1024 * 1024

C_CF, C_SC, C_Q, C_K, C_V, C_Z, C_F = 0, 768, 1792, 2304, 2816, 3328, 3840

ADAM_LR = 0.001
ADAM_B1 = 0.9
ADAM_B2 = 0.999
ADAM_EPS = 1e-08
ADAM_WD = 0.01
ADAM_STEP = 10


def _cparams(sem=None):
    return pltpu.CompilerParams(dimension_semantics=sem, vmem_limit_bytes=VMEM_LIMIT)


def _dot(a, b):
    return jnp.dot(a, b, preferred_element_type=F32)


def _dot_nt(a, b):
    return lax.dot_general(a, b, (((1,), (1,)), ((), ())), preferred_element_type=F32)


def _dot_tn(a, b):
    return lax.dot_general(a, b, (((0,), (0,)), ((), ())), preferred_element_type=F32)


def _split3(x):
    hi = x.astype(BF)
    r1 = x - hi.astype(F32)
    mid = r1.astype(BF)
    lo = (r1 - mid.astype(F32)).astype(BF)
    return hi, mid, lo


def _dot_exact(a_bf, x):
    hi, mid, lo = _split3(x)
    return _dot(a_bf, hi) + _dot(a_bf, mid) + _dot(a_bf, lo)


def _sigmoid(x):
    return 1.0 / (1.0 + jnp.exp(-x))


def _seg_sum64(x, exact=False):
    i = lax.broadcasted_iota(jnp.int32, (LANES, LANES), 0)
    j = lax.broadcasted_iota(jnp.int32, (LANES, LANES), 1)
    g = ((i >= HD) == (j >= HD)).astype(BF)
    hi, mid, lo = _split3(x)
    two = _dot(hi, g) + _dot(mid, g)
    return two + _dot(lo, g) if exact else two


def _assemble_w_in(shards, name):
    tr = 256

    def body(s_ref, o_ref):
        for k in range(4):
            o_ref[:, NSH * k:NSH * (k + 1)] = s_ref[k]
        o_ref[:, N_IN:NP] = jnp.zeros((tr, NP - N_IN), shards.dtype)

    return pl.pallas_call(
        body, grid=(D // tr,), in_specs=[pl.BlockSpec((4, tr, NSH), lambda i: (0, i, 0))],
        out_specs=pl.BlockSpec((tr, NP), lambda i: (i, 0)), out_shape=jax.ShapeDtypeStruct((D, NP), shards.dtype),
        compiler_params=_cparams(("arbitrary",)), name=name)(shards)


def _fwd_inproj(x, g, w, name):
    t = x.shape[0]
    tm = 512

    def body(x_ref, g_ref, w_ref, cf_ref, sc_ref, q_ref, k_ref, v_ref, z_ref, f_ref):
        xt = x_ref[...]
        r = lax.rsqrt(jnp.mean(xt * xt, axis=-1, keepdims=True) + EPS)
        h = ((xt * r) * g_ref[...]).astype(BF)
        cf_ref[:, 0:512] = _dot(h, w_ref[:, 0:512])
        cf_ref[:, 512:768] = _dot(h, w_ref[:, 512:768])
        sc_ref[:, 0:512] = _dot(h, w_ref[:, C_SC:C_SC + 512])
        sc_ref[:, 512:1024] = _dot(h, w_ref[:, C_SC + 512:C_Q])
        q_ref[...] = _dot(h, w_ref[:, C_Q:C_K])
        k_ref[...] = _dot(h, w_ref[:, C_K:C_V])
        v_ref[...] = _dot(h, w_ref[:, C_V:C_Z]).astype(BF)
        z_ref[...] = _dot(h, w_ref[:, C_Z:C_F])
        f_ref[...] = _dot(h, w_ref[:, C_F:NP])

    row = lambda n: pl.BlockSpec((tm, n), lambda i: (i, 0))
    return pl.pallas_call(
        body, grid=(t // tm,),
        in_specs=[row(D), pl.BlockSpec((1, D), lambda i: (0, 0)), pl.BlockSpec((D, NP), lambda i: (0, 0))],
        out_specs=[row(768), row(1024), row(DA), row(DA), row(DA), row(DA), row(LANES)],
        out_shape=[jax.ShapeDtypeStruct((t, 768), F32), jax.ShapeDtypeStruct((t, 1024), F32),
                   jax.ShapeDtypeStruct((t, DA), F32), jax.ShapeDtypeStruct((t, DA), F32),
                   jax.ShapeDtypeStruct((t, DA), BF), jax.ShapeDtypeStruct((t, DA), F32),
                   jax.ShapeDtypeStruct((t, LANES), F32)],
        compiler_params=_cparams(("arbitrary",)), name=name)(x, g, w)


def _shift_copies(buf, sbuf, shifts):
    n = buf.shape[0]
    for sh in shifts:
        rows = n if sh == 0 else n - SUBLANES
        sbuf[sh, 0:rows, :] = buf[pl.ds(sh, rows), :]


def _tap_rows(sbuf, off, r0):
    sh = off % SUBLANES
    return sbuf[sh, pl.ds(pl.multiple_of(r0 + (off - sh), SUBLANES), CONV_ROWS), :]


def _tap_conv(sbuf, w_ref, offsets, dst, ts, bias):
    def chunk(c, _):
        r0 = pl.multiple_of(c * CONV_ROWS, CONV_ROWS)
        acc = jnp.zeros((CONV_ROWS, DC), F32) + bias
        for k, off in enumerate(offsets):
            acc = acc + w_ref[k:k + 1, :] * _tap_rows(sbuf, off, r0)
        dst[pl.ds(r0, CONV_ROWS), :] = acc
        return 0

    lax.fori_loop(0, ts // CONV_ROWS, chunk, 0)


def _tap_sums(a_buf, sbuf, offsets, ts):
    outs = []
    for g0 in range(0, len(offsets), TAP_GROUP):
        group = offsets[g0:g0 + TAP_GROUP]

        def chunk(c, accs, group=group):
            r0 = pl.multiple_of(c * CONV_ROWS, CONV_ROWS)
            a = a_buf[pl.ds(r0, CONV_ROWS), :]
            return tuple(acc + (a * _tap_rows(sbuf, off, r0)).reshape(CONV_ROWS // SUBLANES, SUBLANES, DC).sum(axis=0)
                         for acc, off in zip(accs, group))

        accs = lax.fori_loop(0, ts // CONV_ROWS, chunk, tuple(jnp.zeros((SUBLANES, DC), F32) for _ in group))
        outs += [jnp.sum(acc, axis=0, keepdims=True) for acc in accs]
    return outs


CF_TAPS = tuple(HALO - (CFW - 1) + k for k in range(CFW))
SC_TAPS = tuple(HALO - (SCW - 1) + k for k in range(SCW))
CF_TAPS_T = tuple(CFW - 1 - k for k in range(CFW))
SC_TAPS_T = tuple(SCW - 1 - k for k in range(SCW))


def _conformer_conv(a, g, ha, hg, first, ubuf, usbuf, u1buf, dw_ref, bias):
    ts = a.shape[0]
    ubuf[0:HALO, :] = jnp.where(first, 0.0, ha * _sigmoid(hg))
    ubuf[HALO:HALO + ts, :] = a * _sigmoid(g)
    _shift_copies(ubuf, usbuf, range(SUBLANES))
    _tap_conv(usbuf, dw_ref, CF_TAPS, u1buf, ts, bias)


def _layernorm_swish(u1, lng, lnb):
    mu = jnp.mean(u1, axis=-1, keepdims=True)
    xc = u1 - mu
    rstd = lax.rsqrt(jnp.mean(xc * xc, axis=-1, keepdims=True) + EPS)
    n = xc * rstd
    u2 = n * lng + lnb
    s2 = _sigmoid(u2)
    return n, rstd, u2, s2, u2 * s2


def _conformer_fwd(a, g, ha, hg, first, ubuf, usbuf, u1buf, dw_ref, bias, lng, lnb):
    _conformer_conv(a, g, ha, hg, first, ubuf, usbuf, u1buf, dw_ref, bias)
    return _layernorm_swish(u1buf[...], lng, lnb)[-1]


def _shortconv_fwd(c, xs, hc, hx, first, mbuf, msbuf, cvbuf, dw_ref):
    ts = c.shape[0]
    mbuf[0:HALO, :] = jnp.where(first, 0.0, hc * hx)
    mbuf[HALO:HALO + ts, :] = c * xs
    _shift_copies(mbuf, msbuf, sorted({off % SUBLANES for off in SC_TAPS}))
    _tap_conv(msbuf, dw_ref, SC_TAPS, cvbuf, ts, 0.0)
    return cvbuf[...]


def _conv_specs(bsz, seq, ts, order):
    nt = seq // ts
    tile = (lambda i: i) if order > 0 else (lambda i: nt - 1 - i)
    hrow = lambda i: jnp.maximum(tile(i) * (ts // HALO) - 1, 0)
    cur = lambda n: pl.BlockSpec((None, ts, n), lambda b, i: (b, tile(i), 0))
    halo = lambda j: pl.BlockSpec((None, HALO, DC), lambda b, i: (b, hrow(i), j))
    full = lambda r, c: pl.BlockSpec((r, c), lambda b, i: (0, 0))
    return nt, tile, cur, halo, full


def _fwd_conv(pcf, psc, cf_dw, cf_dw_b, ln_g, ln_b, cf_pw, sc_dw, name):
    bsz, seq, _ = pcf.shape
    ts = 512
    nt, tile, cur, halo, full = _conv_specs(bsz, seq, ts, +1)

    def body(cf_ref, ha_ref, hg_ref, sc_ref, hc_ref, hx_ref, dw_ref, b_ref, lg_ref, lb_ref, pw_ref, sdw_ref,
             y_ref, ubuf, sbuf, obuf):
        first = pl.program_id(1) == 0
        u3 = _conformer_fwd(cf_ref[:, 0:256], cf_ref[:, 256:512], ha_ref[...], hg_ref[...], first,
                            ubuf, sbuf, obuf, dw_ref, b_ref[...], lg_ref[...], lb_ref[...])
        z = cf_ref[:, 512:768]
        y_ref[:, 0:256] = _dot(u3.astype(BF), pw_ref[...]) * (z * _sigmoid(z))
        cv = _shortconv_fwd(sc_ref[:, 256:512], sc_ref[:, 512:768], hc_ref[...], hx_ref[...], first,
                            ubuf, sbuf, obuf, sdw_ref)
        zs = sc_ref[:, 768:1024]
        y_ref[:, 256:512] = sc_ref[:, 0:256] * cv * (zs * _sigmoid(zs))

    return pl.pallas_call(
        body, grid=(bsz, nt),
        in_specs=[cur(768), halo(0), halo(1), cur(1024), halo(1), halo(2),
                  full(CFW, DC), full(1, DC), full(1, DC), full(1, DC), full(DC, DC), full(SCW, DC)],
        out_specs=pl.BlockSpec((None, ts, 512), lambda b, i: (b, i, 0)),
        out_shape=jax.ShapeDtypeStruct((bsz, seq, 512), F32),
        scratch_shapes=[pltpu.VMEM((HALO + ts, DC), F32), pltpu.VMEM((SUBLANES, HALO + ts, DC), F32),
                        pltpu.VMEM((ts, DC), F32)],
        compiler_params=_cparams(("arbitrary", "arbitrary")), name=name,
    )(pcf, pcf, pcf, psc, psc, psc, cf_dw, cf_dw_b, ln_g, ln_b, cf_pw, sc_dw)


def _head_rms(xb, gb):
    ms = _seg_sum64(xb * xb) * (1.0 / HD)
    r = lax.rsqrt(ms + EPS)
    xhat = xb * r
    return xhat, r, xhat * gb


def _fwd_attn_prep(pq, pk, pf, gq, gk, bf, name):
    bsz, seq, _ = pq.shape
    ts = 512
    nt = seq // ts

    def body(q_ref, k_ref, f_ref, gq_ref, gk_ref, bf_ref, qs_ref, kn_ref, cq_ref, crow_ref, carry):
        @pl.when(pl.program_id(1) == 0)
        def _():
            carry[...] = jnp.zeros_like(carry)

        for jb in range(DA // LANES):
            sl = slice(jb * LANES, (jb + 1) * LANES)
            _, _, qn = _head_rms(q_ref[:, sl], gq_ref[:, sl])
            qs_ref[:, sl] = (qn * (1.0 / 8.0)).astype(BF)
            _, _, kn = _head_rms(k_ref[:, sl], gk_ref[:, sl])
            kn_ref[:, sl] = kn.astype(BF)

        xf = f_ref[...] + bf_ref[...]
        lf = jnp.minimum(xf, 0.0) - jnp.log(1.0 + jnp.exp(-jnp.abs(xf)))
        ti = lax.broadcasted_iota(jnp.int32, (ts, ts), 0)
        si = lax.broadcasted_iota(jnp.int32, (ts, ts), 1)
        c = _dot_exact((si <= ti).astype(BF), lf) + carry[...]
        carry[...] = c[ts - 1:ts, :]
        hj = lax.broadcasted_iota(jnp.int32, (LANES, DA), 0)
        ll = lax.broadcasted_iota(jnp.int32, (LANES, DA), 1)
        dd = ll - hj * HD
        chi, cmid, clo = _split3(c)
        e = ((dd >= 0) & (dd < HD)).astype(BF)
        cq_ref[...] = _dot(chi, e) + _dot(cmid, e) + _dot(clo, e)
        hh = lax.broadcasted_iota(jnp.int32, (16, LANES), 0)
        jj = lax.broadcasted_iota(jnp.int32, (16, LANES), 1)
        sel = (hh == jj).astype(BF)
        cr = _dot_nt(sel, chi) + _dot_nt(sel, cmid) + _dot_nt(sel, clo)
        crow_ref[...] = cr[0:NH, :]

    tile = lambda n: pl.BlockSpec((None, ts, n), lambda b, i: (b, i, 0))
    vec = lambda n: pl.BlockSpec((1, n), lambda b, i: (0, 0))
    return pl.pallas_call(
        body, grid=(bsz, nt),
        in_specs=[tile(DA), tile(DA), tile(LANES), vec(DA), vec(DA), vec(LANES)],
        out_specs=[tile(DA), tile(DA), tile(DA), pl.BlockSpec((None, NH, ts), lambda b, i: (b, 0, i))],
        out_shape=[jax.ShapeDtypeStruct((bsz, seq, DA), BF), jax.ShapeDtypeStruct((bsz, seq, DA), BF),
                   jax.ShapeDtypeStruct((bsz, seq, DA), F32), jax.ShapeDtypeStruct((bsz, NH, seq), F32)],
        scratch_shapes=[pltpu.VMEM((1, LANES), F32)],
        compiler_params=_cparams(("arbitrary", "arbitrary")), name=name)(pq, pk, pf, gq, gk, bf)


def _tile_rows(i, rows):
    return pl.ds(i * rows, rows) if isinstance(i, int) else pl.ds(pl.multiple_of(i * rows, rows), rows)


def _causal_mask(tq, tk):
    r = lax.broadcasted_iota(jnp.int32, (tq, tk), 0)
    c = lax.broadcasted_iota(jnp.int32, (tq, tk), 1)
    return r - c


def _fwd_attn(qs, kn, vb, cq, crow, pz, name, ex=None):
    bsz, seq, _ = qs.shape
    tq, tk = 256, 256
    nq, nk = seq // tq, seq // tk
    npair = NH // 2

    def body(q_ref, k_ref, v_ref, cq_ref, cr_ref, z_ref, o_ref, ohp_ref, y_ref, lse_ref,
             s_buf, p_buf, acc_ref, accl_ref):
        qi = pl.program_id(2)
        head0 = lax.broadcasted_iota(jnp.int32, (1, LANES), 1) < HD
        q = q_ref[...]
        zq = jnp.zeros_like(q)
        qm = (jnp.where(head0, q, zq), jnp.where(head0, zq, q))
        ct = (cq_ref[:, 0:1], cq_ref[:, HD:HD + 1])

        tile = lambda j: _tile_rows(j, tk)

        def scores(j, slot):
            kb = k_ref[tile(j), :]
            for hh in range(2):
                s_buf[slot, hh] = _dot_nt(qm[hh], kb) + (ct[hh] - cr_ref[hh:hh + 1, tile(j)])

        def weighted_values(slot, j, al):
            vb_t = v_ref[tile(j), :]
            acc_ref[...] = al * acc_ref[...] + jnp.where(head0, _dot(p_buf[slot, 0, 0], vb_t),
                                                         _dot(p_buf[slot, 1, 0], vb_t))
            accl_ref[...] = al * accl_ref[...] + jnp.where(head0, _dot(p_buf[slot, 0, 1], vb_t),
                                                           _dot(p_buf[slot, 1, 1], vb_t))

        def softmax(slot, first_visible, m, l, masked):
            m_new, l_new, al_new = ([], []), ([], []), []
            for r in range(tq // FWD_ROWS):
                rows = slice(r * FWD_ROWS, (r + 1) * FWD_ROWS)
                visible = _causal_mask(FWD_ROWS, tk) >= first_visible - r * FWD_ROWS
                alphas = []
                for hh in range(2):
                    s = s_buf[slot, hh, rows, :]
                    if masked:
                        s = jnp.where(visible, s, -1e30)
                    m_old = m[hh][rows]
                    m2 = jnp.maximum(m_old, jnp.max(s, axis=-1, keepdims=True))
                    p = jnp.exp(s - m2)
                    alpha = jnp.exp(m_old - m2)
                    l_new[hh].append(alpha * l[hh][rows] + jnp.sum(p, axis=-1, keepdims=True))
                    m_new[hh].append(m2)
                    pb = p.astype(BF)
                    p_buf[slot, hh, 0, rows, :] = pb
                    p_buf[slot, hh, 1, rows, :] = (p - pb.astype(F32)).astype(BF)
                    alphas.append(alpha)
                al_new.append(jnp.where(head0, alphas[0], alphas[1]))
            cat = lambda parts: jnp.concatenate(parts, axis=0)
            return (cat(m_new[0]), cat(m_new[1])), (cat(l_new[0]), cat(l_new[1])), cat(al_new)

        clamp = lambda j: jnp.clip(j, 0, nk - 1)

        def step(j, slot, carry, masked):
            m, l, al = carry
            weighted_values(1 - slot, clamp(j - 1), al)
            m, l, al = softmax(slot, j * tk - qi * tq, m, l, masked)
            scores(clamp(j + 1), 1 - slot)
            return m, l, al

        scores(0, 0)
        p_buf[1] = jnp.zeros((2, 2, tq, tk), BF)
        acc_ref[...] = jnp.zeros((tq, LANES), F32)
        accl_ref[...] = jnp.zeros((tq, LANES), F32)
        neg = jnp.full((tq, 1), -1e30, F32)
        zcol = jnp.zeros((tq, 1), F32)
        full = (qi * (tq // tk)) // 2
        carry = lax.fori_loop(0, full, lambda t, cy: step(2 * t + 1, 1, step(2 * t, 0, cy, False), False),
                              ((neg, neg), (zcol, zcol), jnp.ones((tq, LANES), F32)))
        m, l, al = step(2 * full + 1, 1, step(2 * full, 0, carry, True), True)
        weighted_values(1, clamp(2 * full + 1), al)
        inv = jnp.where(head0, 1.0 / l[0], 1.0 / l[1])
        o = acc_ref[...] * inv
        z = z_ref[...]
        o_ref[...] = o
        ohp_ref[...] = (acc_ref[...] + accl_ref[...]) * inv
        y_ref[...] = o * (z * _sigmoid(z))
        lse_ref[...] = jnp.where(head0, m[0] + jnp.log(l[0]), m[1] + jnp.log(l[1]))

    qblk = pl.BlockSpec((None, tq, LANES), lambda b, h, i: (b, i, h))
    kvblk = pl.BlockSpec((None, seq, LANES), lambda b, h, i: (b, 0, h))
    return _host_call(
        body, (bsz, npair, nq),
        [qblk, kvblk, kvblk, qblk, pl.BlockSpec((None, None, 2, seq), lambda b, h, i: (b, h, 0, 0)), qblk],
        [qblk, qblk, qblk, qblk], [jax.ShapeDtypeStruct((bsz, seq, DA), F32)] * 4,
        [pltpu.VMEM((2, 2, tq, tk), F32), pltpu.VMEM((2, 2, 2, tq, tk), BF), pltpu.VMEM((tq, LANES), F32),
         pltpu.VMEM((tq, LANES), F32)],
        (qs, kn, vb, cq, crow, pz), name, ex)


def _fwd_outproj(x, ycs, yatt, wo, name):
    t = x.shape[0]
    tm = 512

    def body(x_ref, a_ref, b_ref, w_ref, o_ref):
        o_ref[...] = (x_ref[...] + _dot(a_ref[...].astype(BF), w_ref[0:512, :])
                      + _dot(b_ref[...].astype(BF), w_ref[512:1024, :]))

    row = lambda n: pl.BlockSpec((tm, n), lambda i: (i, 0))
    return pl.pallas_call(
        body, grid=(t // tm,),
        in_specs=[row(D), row(512), row(512), pl.BlockSpec((D, D), lambda i: (0, 0))],
        out_specs=row(D), out_shape=jax.ShapeDtypeStruct((t, D), F32),
        compiler_params=_cparams(("arbitrary",)), name=name)(x, ycs, yatt, wo)


def _fwd_outproj_loss(x, ycs, yatt, wo, target, name):
    t = x.shape[0]
    tm = 512

    def body(x_ref, a_ref, b_ref, w_ref, t_ref, dy_ref, loss_ref):
        @pl.when(pl.program_id(0) == 0)
        def _():
            loss_ref[...] = jnp.zeros_like(loss_ref)

        y = (x_ref[...] + _dot(a_ref[...].astype(BF), w_ref[0:512, :])
             + _dot(b_ref[...].astype(BF), w_ref[512:1024, :]))
        err = y - t_ref[...]
        dy_ref[...] = err * (1.0 / D)
        per_tok = jnp.mean(err * err, axis=-1, keepdims=True)
        loss_ref[...] += 0.5 * jnp.sum(per_tok, axis=0, keepdims=True)

    row = lambda n: pl.BlockSpec((tm, n), lambda i: (i, 0))
    return pl.pallas_call(
        body, grid=(t // tm,),
        in_specs=[row(D), row(512), row(512), pl.BlockSpec((D, D), lambda i: (0, 0)), row(D)],
        out_specs=[row(D), pl.BlockSpec((1, 1), lambda i: (0, 0))],
        out_shape=[jax.ShapeDtypeStruct((t, D), F32), jax.ShapeDtypeStruct((1, 1), F32)],
        compiler_params=_cparams(("arbitrary",)), name=name)(x, ycs, yatt, wo, target)


def _dsilu(x, s):
    return s * (1.0 + x * (1.0 - s))


def _bwd_outproj(dy, ycs, yatt, o, ohp, pz, wo, name, ex=None):
    t = dy.shape[0]
    tm = 512

    def body(dy_ref, a_ref, b_ref, o_ref, ohp_ref, z_ref, w_ref, dcs_ref, do_ref, dz_ref, dl_ref, dw_ref):
        @pl.when(pl.program_id(0) == 0)
        def _():
            dw_ref[...] = jnp.zeros_like(dw_ref)

        dyb = dy_ref[...].astype(BF)
        dw_ref[0:512, :] += _dot_tn(a_ref[...].astype(BF), dyb)
        dw_ref[512:1024, :] += _dot_tn(b_ref[...].astype(BF), dyb)
        dcs_ref[...] = _dot_nt(dyb, w_ref[0:512, :])
        dyatt = _dot_nt(dyb, w_ref[512:1024, :])
        z = z_ref[...]
        sz = _sigmoid(z)
        o_t = o_ref[...]
        dob = (dyatt * (z * sz)).astype(BF)
        do_ref[...] = dob
        dz_ref[...] = dyatt * o_t * _dsilu(z, sz)
        prod = dob.astype(F32) * ohp_ref[...]
        for jb in range(DA // LANES):
            sl = slice(jb * LANES, (jb + 1) * LANES)
            dl_ref[:, sl] = _seg_sum64(prod[:, sl], exact=True)

    row = lambda n: pl.BlockSpec((tm, n), lambda i: (i, 0))
    return _host_call(
        body, (t // tm,),
        [row(D), row(512), row(512), row(DA), row(DA), row(DA), pl.BlockSpec((D, D), lambda i: (0, 0))],
        [row(512), row(DA), row(DA), row(DA), pl.BlockSpec((D, D), lambda i: (0, 0))],
        [jax.ShapeDtypeStruct((t, 512), F32), jax.ShapeDtypeStruct((t, DA), BF),
         jax.ShapeDtypeStruct((t, DA), F32), jax.ShapeDtypeStruct((t, DA), F32),
         jax.ShapeDtypeStruct((D, D), F32)], [],
        (dy, ycs, yatt, o, ohp, pz, wo), name, ex)


def _bwd_attn(qs, kn, vb, do, cq, lse, dl, crow, name, ex=None):
    bsz, seq, _ = qs.shape
    tq, tk = 256, min(BWD_KEYS, seq)
    nq, nk = seq // tq, seq // tk
    rows_c = ATT_ROWS * tq // tk
    peel = tk == 2 * tq and nq % 2 == 0
    npair = NH // 2

    def body(q_ref, k_ref, v_ref, do_ref, cq_ref, lse_ref, dl_ref, cr_ref, dq_ref, dk_ref, dv_ref, dc_ref,
             qm, dom, s_buf, dp_buf, pd_buf):
        head0 = lax.broadcasted_iota(jnp.int32, (1, LANES), 1) < HD
        zb = jnp.zeros((seq, LANES), BF)
        qm[0] = jnp.where(head0, q_ref[...], zb)
        qm[1] = jnp.where(head0, zb, q_ref[...])
        dom[0] = jnp.where(head0, do_ref[...], zb)
        dom[1] = jnp.where(head0, zb, do_ref[...])
        dq_ref[...] = jnp.zeros_like(dq_ref)

        def kloop(kj, _):
            krows = pl.ds(pl.multiple_of(kj * tk, tk), tk)
            kb = k_ref[krows, :]
            vb_t = v_ref[krows, :]
            zk = jnp.zeros_like(kb)
            km = (jnp.where(head0, kb, zk), jnp.where(head0, zk, kb))

            tile = lambda i: _tile_rows(i, tq)

            def scores(i, slot):
                for hh in range(2):
                    c0 = hh * HD
                    s_buf[slot, hh] = (_dot_nt(qm[hh, tile(i), :], kb)
                                       + (cq_ref[tile(i), c0:c0 + 1] - cr_ref[hh:hh + 1, krows]))
                    dp_buf[slot, hh] = _dot_nt(dom[hh, tile(i), :], vb_t)

            def products(slot, i, dk, dv):
                dq = jnp.zeros((tq, LANES), F32)
                for hh in range(2):
                    dsb = pd_buf[slot, hh, 1]
                    dv = dv + _dot_tn(pd_buf[slot, hh, 0], dom[hh, tile(i), :])
                    dk = dk + _dot_tn(dsb, qm[hh, tile(i), :])
                    dq = dq + _dot(dsb, km[hh])
                dq_ref[tile(i), :] += dq
                return dk, dv

            clamp = lambda i: jnp.clip(i, 0, nq - 1)

            def qstep(i, slot, carry, masked=True):
                dk, dv, cs = carry
                dk, dv = products(1 - slot, clamp(i - 1), dk, dv)
                cs = list(cs)
                ic = clamp(i)
                first_visible = jnp.where(i < nq, kj * tk - i * tq, 2 * tk)
                for r in range(tq // rows_c):
                    rows = slice(r * rows_c, (r + 1) * rows_c)
                    qrows = pl.ds(pl.multiple_of(ic * tq + r * rows_c, rows_c), rows_c)
                    visible = _causal_mask(rows_c, tk) >= first_visible - r * rows_c
                    for hh in range(2):
                        c0 = hh * HD
                        s = s_buf[slot, hh, rows, :]
                        if masked:
                            s = jnp.where(visible, s, -1e30)
                        p = jnp.exp(s - lse_ref[qrows, c0:c0 + 1])
                        ds = p * (dp_buf[slot, hh, rows, :] - dl_ref[qrows, c0:c0 + 1])
                        pd_buf[slot, hh, 0, rows, :] = p.astype(BF)
                        pd_buf[slot, hh, 1, rows, :] = ds.astype(BF)
                        cs[hh] = cs[hh] + jnp.sum(ds, axis=0, keepdims=True)
                scores(clamp(i + 1), 1 - slot)
                return dk, dv, tuple(cs)

            i0 = kj * (tk // tq)
            scores(i0, 0)
            pd_buf[1] = jnp.zeros((2, 2, tq, tk), BF)
            zero = jnp.zeros((tk, LANES), F32)
            zrow = jnp.zeros((1, tk), F32)
            trips = (nq - i0 + 1) // 2
            carry = (zero, zero, (zrow, zrow))
            if peel:
                carry = qstep(i0 + 1, 1, qstep(i0, 0, carry))
                dk, dv, cs = lax.fori_loop(
                    1, trips, lambda t, cy: qstep(i0 + 2 * t + 1, 1, qstep(i0 + 2 * t, 0, cy, False), False), carry)
            else:
                dk, dv, cs = lax.fori_loop(
                    0, trips, lambda t, cy: qstep(i0 + 2 * t + 1, 1, qstep(i0 + 2 * t, 0, cy)), carry)
            dk, dv = products(1, clamp(i0 + 2 * trips - 1), dk, dv)
            dk_ref[krows, :] = dk
            dv_ref[krows, :] = dv
            dc_ref[0:1, krows] = -cs[0]
            dc_ref[1:2, krows] = -cs[1]
            return 0

        lax.fori_loop(0, nk, kloop, 0)

    blk = pl.BlockSpec((None, seq, LANES), lambda b, h: (b, 0, h))
    rowblk = pl.BlockSpec((None, None, 2, seq), lambda b, h: (b, h, 0, 0))
    return _host_call(
        body, (bsz, npair), [blk, blk, blk, blk, blk, blk, blk, rowblk], [blk, blk, blk, rowblk],
        [jax.ShapeDtypeStruct((bsz, seq, DA), F32)] * 3 + [jax.ShapeDtypeStruct((bsz, npair, 2, seq), F32)],
        [pltpu.VMEM((2, seq, LANES), BF), pltpu.VMEM((2, seq, LANES), BF), pltpu.VMEM((2, 2, tq, tk), F32),
         pltpu.VMEM((2, 2, tq, tk), F32), pltpu.VMEM((2, 2, 2, tq, tk), BF)],
        (qs, kn, vb, do, cq, lse, dl, crow), name, ex)


def _bwd_attn_post(pq, pk, dqs, dkn, pf, dcrow, gq, gk, bf, name, ex=None):
    bsz, seq, _ = pq.shape
    ts = 512
    nt = seq // ts

    def body(q_ref, k_ref, dq_ref, dk_ref, f_ref, dc_ref, gq_ref, gk_ref, bf_ref,
             dpq_ref, dpk_ref, dpf_ref, dgq_ref, dgk_ref, dbf_ref, carry):
        @pl.when((pl.program_id(0) == 0) & (pl.program_id(1) == 0))
        def _():
            dgq_ref[...] = jnp.zeros_like(dgq_ref)
            dgk_ref[...] = jnp.zeros_like(dgk_ref)
            dbf_ref[...] = jnp.zeros_like(dbf_ref)

        @pl.when(pl.program_id(1) == 0)
        def _():
            carry[...] = jnp.zeros_like(carry)

        for x_ref, dx_ref, g_ref, dp_ref, dg_ref, scale in ((q_ref, dq_ref, gq_ref, dpq_ref, dgq_ref, 1.0 / 8.0),
                                                            (k_ref, dk_ref, gk_ref, dpk_ref, dgk_ref, 1.0)):
            for jb in range(DA // LANES):
                sl = slice(jb * LANES, (jb + 1) * LANES)
                gb = g_ref[:, sl]
                xhat, r, _ = _head_rms(x_ref[:, sl], gb)
                dn = dx_ref[:, sl] * scale
                dg_ref[:, sl] += jnp.sum(dn * xhat, axis=0, keepdims=True)
                dxh = dn * gb
                dp_ref[:, sl] = r * (dxh - xhat * (_seg_sum64(dxh * xhat) * (1.0 / HD)))

        ui = lax.broadcasted_iota(jnp.int32, (ts, ts), 0)
        tj = lax.broadcasted_iota(jnp.int32, (ts, ts), 1)
        tri = (tj >= ui).astype(BF)
        dc = jnp.concatenate([dc_ref[...], jnp.zeros((LANES - NH, ts), F32)], axis=0)
        hi, mid, lo = _split3(dc)
        dlf = _dot_nt(tri, hi) + _dot_nt(tri, mid) + _dot_nt(tri, lo) + carry[...]
        carry[...] = dlf[0:1, :]
        xf = f_ref[...] + bf_ref[...]
        lane = lax.broadcasted_iota(jnp.int32, (ts, LANES), 1)
        dfl = jnp.where(lane < NH, dlf * _sigmoid(-xf), 0.0)
        dpf_ref[...] = dfl
        dbf_ref[...] += jnp.sum(dfl, axis=0, keepdims=True)

    rev = lambda i: nt - 1 - i
    tile = lambda n: pl.BlockSpec((None, ts, n), lambda b, i: (b, rev(i), 0))
    vec = lambda n: pl.BlockSpec((1, n), lambda b, i: (0, 0))
    return _host_call(
        body, (bsz, nt),
        [tile(DA), tile(DA), tile(DA), tile(DA), tile(LANES),
         pl.BlockSpec((None, NH, ts), lambda b, i: (b, 0, rev(i))), vec(DA), vec(DA), vec(LANES)],
        [tile(DA), tile(DA), tile(LANES), vec(DA), vec(DA), vec(LANES)],
        [jax.ShapeDtypeStruct((bsz, seq, DA), F32), jax.ShapeDtypeStruct((bsz, seq, DA), F32),
         jax.ShapeDtypeStruct((bsz, seq, LANES), F32), jax.ShapeDtypeStruct((1, DA), F32),
         jax.ShapeDtypeStruct((1, DA), F32), jax.ShapeDtypeStruct((1, LANES), F32)],
        [pltpu.VMEM((1, LANES), F32)], (pq, pk, dqs, dkn, pf, dcrow, gq, gk, bf), name, ex)


def _bwd_conv(pcf, psc, dycs, cf_dw, cf_dw_b, ln_g, ln_b, cf_pw, sc_dw, name):
    bsz, seq, _ = pcf.shape
    ts = 512
    nt, tile, cur, halo, full = _conv_specs(bsz, seq, ts, -1)

    def body(cf_ref, ha_ref, hg_ref, sc_ref, hc_ref, hx_ref, dy_ref, dw_ref, b_ref, lg_ref, lb_ref, pw_ref, sdw_ref,
             dcf_ref, dsc_ref, ddw_ref, db_ref, dlg_ref, dlb_ref, dpw_ref, dsdw_ref,
             ubuf, usbuf, obuf, dubuf, dsbuf, carry_du, carry_dc):
        step = pl.program_id(1)
        first = tile(step) == 0

        @pl.when((pl.program_id(0) == 0) & (step == 0))
        def _():
            for r in (ddw_ref, db_ref, dlg_ref, dlb_ref, dpw_ref, dsdw_ref):
                r[...] = jnp.zeros_like(r)

        @pl.when(step == 0)
        def _():
            carry_du[...] = jnp.zeros((HALO, DC), F32)
            carry_dc[...] = jnp.zeros((HALO, DC), F32)

        def transposed_conv(d, carry, w_ref, taps_t):
            if d is not None:
                dubuf[0:ts, :] = d
            dubuf[ts:ts + HALO, :] = carry[...]
            carry[...] = dubuf[0:HALO, :]
            _shift_copies(dubuf, dsbuf, sorted({off % SUBLANES for off in taps_t}))
            _tap_conv(dsbuf, w_ref, taps_t, obuf, ts, 0.0)
            return obuf[...]

        _conformer_conv(cf_ref[:, 0:256], cf_ref[:, 256:512], ha_ref[...], hg_ref[...], first, ubuf, usbuf, obuf,
                        dw_ref, b_ref[...])
        lng = lg_ref[...]

        def between_convs(c, sums):
            rows = pl.ds(pl.multiple_of(c * CONV_ROWS, CONV_ROWS), CONV_ROWS)
            n, rstd, u2, s2, u3 = _layernorm_swish(obuf[rows, :], lng, lb_ref[...])
            z = cf_ref[rows, 512:768]
            dy = dy_ref[rows, 0:256]
            u3b = u3.astype(BF)
            sz = _sigmoid(z)
            dcf_ref[rows, 512:768] = dy * _dot(u3b, pw_ref[...]) * _dsilu(z, sz)
            dpb = (dy * (z * sz)).astype(BF)
            dpw_ref[...] += _dot_tn(u3b, dpb)
            du2 = _dot_nt(dpb, pw_ref[...]) * _dsilu(u2, s2)
            dn = du2 * lng
            du1 = rstd * (dn - jnp.mean(dn, axis=-1, keepdims=True) - n * jnp.mean(dn * n, axis=-1, keepdims=True))
            dubuf[rows, :] = du1
            col = lambda v: jnp.sum(v, axis=0, keepdims=True)
            return sums[0] + col(du2 * n), sums[1] + col(du2), sums[2] + col(du1)

        zrow = jnp.zeros((1, DC), F32)
        d_lg, d_lb, d_b = lax.fori_loop(0, ts // CONV_ROWS, between_convs, (zrow, zrow, zrow))
        dlg_ref[...] += d_lg
        dlb_ref[...] += d_lb
        db_ref[...] += d_b
        du0_done = transposed_conv(None, carry_du, dw_ref, CF_TAPS_T)
        for k, tap_grad in enumerate(_tap_sums(dubuf, usbuf, CF_TAPS, ts)):
            ddw_ref[k:k + 1, :] += tap_grad
        a = cf_ref[:, 0:256]
        sg = _sigmoid(cf_ref[:, 256:512])
        dcf_ref[:, 0:256] = du0_done * sg
        dcf_ref[:, 256:512] = du0_done * a * sg * (1.0 - sg)

        bb = sc_ref[:, 0:256]
        c = sc_ref[:, 256:512]
        xs = sc_ref[:, 512:768]
        zs = sc_ref[:, 768:1024]
        cv = _shortconv_fwd(c, xs, hc_ref[...], hx_ref[...], first, ubuf, usbuf, obuf, sdw_ref)
        szs = _sigmoid(zs)
        dys = dy_ref[:, 256:512]
        gate = zs * szs
        dsc_ref[:, 0:256] = dys * cv * gate
        dsc_ref[:, 768:1024] = dys * bb * cv * _dsilu(zs, szs)
        dm = transposed_conv(dys * bb * gate, carry_dc, sdw_ref, SC_TAPS_T)
        for k, tap_grad in enumerate(_tap_sums(dubuf, usbuf, SC_TAPS, ts)):
            dsdw_ref[k:k + 1, :] += tap_grad
        dsc_ref[:, 256:512] = dm * xs
        dsc_ref[:, 512:768] = dm * c

    outt = lambda n: pl.BlockSpec((None, ts, n), lambda b, i: (b, tile(i), 0))
    return pl.pallas_call(
        body, grid=(bsz, nt),
        in_specs=[cur(768), halo(0), halo(1), cur(1024), halo(1), halo(2), cur(512),
                  full(CFW, DC), full(1, DC), full(1, DC), full(1, DC), full(DC, DC), full(SCW, DC)],
        out_specs=[outt(768), outt(1024), full(CFW, DC), full(1, DC), full(1, DC), full(1, DC), full(DC, DC),
                   full(SCW, DC)],
        out_shape=[jax.ShapeDtypeStruct((bsz, seq, 768), F32), jax.ShapeDtypeStruct((bsz, seq, 1024), F32),
                   jax.ShapeDtypeStruct((CFW, DC), F32), jax.ShapeDtypeStruct((1, DC), F32),
                   jax.ShapeDtypeStruct((1, DC), F32), jax.ShapeDtypeStruct((1, DC), F32),
                   jax.ShapeDtypeStruct((DC, DC), F32), jax.ShapeDtypeStruct((SCW, DC), F32)],
        scratch_shapes=[pltpu.VMEM((HALO + ts, DC), F32), pltpu.VMEM((SUBLANES, HALO + ts, DC), F32),
                        pltpu.VMEM((ts, DC), F32), pltpu.VMEM((ts + HALO, DC), F32),
                        pltpu.VMEM((SUBLANES, ts + HALO, DC), F32), pltpu.VMEM((HALO, DC), F32),
                        pltpu.VMEM((HALO, DC), F32)],
        compiler_params=_cparams(("arbitrary", "arbitrary")), name=name,
    )(pcf, pcf, pcf, psc, psc, psc, dycs, cf_dw, cf_dw_b, ln_g, ln_b, cf_pw, sc_dw)


def _bwd_inproj(x, g, dyres, w, dcf, dsc, dq, dk, dv, dz, df, name):
    t = x.shape[0]
    tm = 256
    pieces = ((C_CF, 768), (C_SC, 1024), (C_Q, DA), (C_K, DA), (C_V, DA), (C_Z, DA), (C_F, LANES))

    def body(x_ref, g_ref, dy_ref, w_hbm, dcf_ref, dsc_ref, dq_ref, dk_ref, dv_ref, dz_ref, df_ref,
             dx_ref, dg_ref, dw_hbm, w_vmem, dw_acc, sem, stage, out_sems):
        i = pl.program_id(0)

        @pl.when(i == 0)
        def _():
            cp = pltpu.make_async_copy(w_hbm, w_vmem, sem)
            cp.start()
            dw_acc[...] = jnp.zeros_like(dw_acc)
            dg_ref[...] = jnp.zeros_like(dg_ref)
            cp.wait()

        xt = x_ref[...]
        gg = g_ref[...]
        r = lax.rsqrt(jnp.mean(xt * xt, axis=-1, keepdims=True) + EPS)
        xhat = xt * r
        ht = (xhat * gg).astype(BF).T
        dh = jnp.zeros((tm, D), F32)
        for (c0, n), ref in zip(pieces, (dcf_ref, dsc_ref, dq_ref, dk_ref, dv_ref, dz_ref, df_ref)):
            for s0 in range(0, n, 512):
                s1 = min(s0 + 512, n)
                d = ref[:, s0:s1].astype(BF)
                dh = dh + _dot_nt(d, w_vmem[:, c0 + s0:c0 + s1])
                dw_acc[:, c0 + s0:c0 + s1] += _dot(ht, d)
        dg_ref[...] += jnp.sum(dh * xhat, axis=0, keepdims=True)
        dhg = dh * gg
        dx_ref[...] = dy_ref[...] + r * (dhg - xhat * jnp.mean(dhg * xhat, axis=-1, keepdims=True))

        @pl.when(i == pl.num_programs(0) - 1)
        def _():
            rows_p = D // 4
            pieces_out = [(k, q) for k in range(4) for q in range(4)]
            copies = []
            for p, (k, q) in enumerate(pieces_out):
                slot = p % 2
                if p >= 2:
                    copies[p - 2].wait()

                def chunk(c, _, k=k, q=q, slot=slot):
                    r0 = pl.multiple_of(c * 64, 64)
                    stage[slot, pl.ds(r0, 64), :] = dw_acc[pl.ds(q * rows_p + r0, 64), NSH * k:NSH * (k + 1)]
                    return 0

                lax.fori_loop(0, rows_p // 64, chunk, 0)
                out = pltpu.make_async_copy(stage.at[slot], dw_hbm.at[k, q * rows_p:(q + 1) * rows_p, :],
                                            out_sems.at[slot])
                out.start()
                copies.append(out)
            copies[-2].wait()
            copies[-1].wait()

    row = lambda n: pl.BlockSpec((tm, n), lambda i: (i, 0))
    anyspec = pl.BlockSpec(memory_space=pl.ANY)
    return pl.pallas_call(
        body, grid=(t // tm,),
        in_specs=[row(D), pl.BlockSpec((1, D), lambda i: (0, 0)), row(D), anyspec,
                  row(768), row(1024), row(DA), row(DA), row(DA), row(DA), row(LANES)],
        out_specs=[row(D), pl.BlockSpec((1, D), lambda i: (0, 0)), anyspec],
        out_shape=[jax.ShapeDtypeStruct((t, D), F32), jax.ShapeDtypeStruct((1, D), F32),
                   jax.ShapeDtypeStruct((4, D, NSH), F32)],
        scratch_shapes=[pltpu.VMEM((D, NP), BF), pltpu.VMEM((D, NP), F32), pltpu.SemaphoreType.DMA,
                        pltpu.VMEM((2, D // 4, NSH), F32), pltpu.SemaphoreType.DMA((2,))],
        compiler_params=_cparams(("arbitrary",)), name=name)(x, g, dyres, w, dcf, dsc, dq, dk, dv, dz, df)


def _adamw_update(w_ref, g_ref, m_ref, v_ref, d_ref, m2_ref, v2_ref):
    gt = g_ref[...]
    m2 = ADAM_B1 * m_ref[...] + (1.0 - ADAM_B1) * gt
    v2 = ADAM_B2 * v_ref[...] + (1.0 - ADAM_B2) * (gt * gt)
    m_hat = m2 / (1.0 - ADAM_B1 ** ADAM_STEP)
    v_hat = v2 / (1.0 - ADAM_B2 ** ADAM_STEP)
    d_ref[...] = -ADAM_LR * (m_hat / (jnp.sqrt(v_hat) + ADAM_EPS) + ADAM_WD * w_ref[...])
    m2_ref[...] = m2
    v2_ref[...] = v2


def _adamw_many(ws, gs, ms, vs, name):
    n = len(ws)

    def body(*refs):
        ins, outs = refs[:4 * n], refs[4 * n:]
        for i in range(n):
            _adamw_update(ins[i], ins[n + i], ins[2 * n + i], ins[3 * n + i], *outs[3 * i:3 * i + 3])

    return pl.pallas_call(
        body, out_shape=[jax.ShapeDtypeStruct(w.shape, F32) for w in ws for _ in range(3)],
        name=name)(*ws, *gs, *ms, *vs)


def _adamw(w, g, m, v, tr, name):
    rows, cols = w.shape

    def body(w_ref, g_ref, m_ref, v_ref, d_ref, m2_ref, v2_ref, g_out_ref):
        _adamw_update(w_ref, g_ref, m_ref, v_ref, d_ref, m2_ref, v2_ref)
        g_out_ref[...] = g_ref[...]

    blk = pl.BlockSpec((tr, cols), lambda i: (i, 0))
    return pl.pallas_call(
        body, grid=(rows // tr,), in_specs=[blk] * 4, out_specs=[blk] * 4,
        out_shape=[jax.ShapeDtypeStruct((rows, cols), F32)] * 4,
        compiler_params=_cparams(("arbitrary",)), name=name)(w, g, m, v)


def _place():
    x, y, c = lax.axis_index("x"), lax.axis_index("y"), lax.axis_index("c")
    chips = [(1 - x, y), (x, 1 - y), (1 - x, 1 - y)]
    return x, y, c, chips


def _halves(c, rows, align):
    rh = rows // 2
    return pl.ds(pl.multiple_of(c * rh, align), rh), pl.ds(pl.multiple_of((1 - c) * rh, align), rh)


ANY = pl.BlockSpec(memory_space=pl.ANY)


class _Exchange:
    def __init__(self, operands, out_shape, sems, start, wait, relay=None):
        self.operands, self.out_shape, self.sems, self.start, self.wait = operands, out_shape, sems, start, wait
        self.relay = relay


class _StagedCopy:
    def __init__(self, src, dst, stage, sem):
        self.src, self.dst, self.stage, self.sem = src, dst, stage, sem

    def start(self):
        pltpu.make_async_copy(self.src, self.stage, self.sem).start()

    def relay(self):
        pltpu.make_async_copy(self.src, self.stage, self.sem).wait()
        pltpu.make_async_copy(self.stage, self.dst, self.sem).start()

    def finish(self):
        pltpu.make_async_copy(self.stage, self.dst, self.sem).wait()


def _run_exchange(ex, name):
    n_in, n_out = len(ex.operands), len(ex.out_shape)

    def body(*refs):
        ins, outs, sems = refs[:n_in], refs[n_in:n_in + n_out], refs[n_in + n_out:]
        ex.start(ins, outs, sems)
        if ex.relay:
            ex.relay(ins, outs, sems)
        ex.wait(ins, outs, sems)

    return pl.pallas_call(body, in_specs=[ANY] * n_in, out_specs=[ANY] * n_out, out_shape=ex.out_shape,
                          scratch_shapes=ex.sems, name=name)(*ex.operands)


def _host_call(body, grid, in_specs, out_specs, out_shape, scratch, operands, name, ex=None):
    sem = ("arbitrary",) * len(grid)
    if ex is None:
        outs = pl.pallas_call(body, grid=grid, in_specs=in_specs, out_specs=out_specs, out_shape=out_shape,
                              scratch_shapes=scratch, compiler_params=_cparams(sem), name=name)(*operands)
        return outs, None
    n_in, n_out, n_scr = len(in_specs), len(out_specs), len(scratch)
    xi, xo = len(ex.operands), len(ex.out_shape)

    def hosted(*refs):
        ins, xins = refs[:n_in], refs[n_in:n_in + xi]
        o0 = n_in + xi
        outs, xouts = refs[o0:o0 + n_out], refs[o0 + n_out:o0 + n_out + xo]
        s0 = o0 + n_out + xo
        scr, xsems = refs[s0:s0 + n_scr], refs[s0 + n_scr:]
        first = pl.program_id(0) == 0
        half = pl.program_id(0) == grid[0] // 2
        last = pl.program_id(0) == pl.num_programs(0) - 1
        for d in range(1, len(grid)):
            first = first & (pl.program_id(d) == 0)
            half = half & (pl.program_id(d) == 0)
            last = last & (pl.program_id(d) == pl.num_programs(d) - 1)

        @pl.when(first)
        def _():
            ex.start(xins, xouts, xsems)

        if ex.relay:
            @pl.when(half)
            def _():
                ex.relay(xins, xouts, xsems)

        body(*ins, *outs, *scr)

        @pl.when(last)
        def _():
            ex.wait(xins, xouts, xsems)

    res = pl.pallas_call(
        hosted, grid=grid, in_specs=list(in_specs) + [ANY] * xi, out_specs=list(out_specs) + [ANY] * xo,
        out_shape=list(out_shape) + list(ex.out_shape), scratch_shapes=list(scratch) + list(ex.sems),
        compiler_params=_cparams(sem), name=name)(*operands, *ex.operands)
    return res[:n_out], res[n_out:]


def _all_gather_chips(shards):
    n = len(shards)

    def copies(srcs, outs, sems):
        send_sems, recv_sems, local_sems, *stage = sems
        x, y, c, chips = _place()
        me = 2 * x + y
        halves = [_halves(c, s.shape[1], 16) for s in shards]
        local = [_StagedCopy(srcs[i], outs[i].at[:, me], stage[i], local_sems.at[i]) for i in range(n)]

        def ici(i, j, chip, k):
            mine = halves[i][0]
            return pltpu.make_async_remote_copy(
                src_ref=srcs[i].at[:, mine, :], dst_ref=outs[i].at[:, k, mine, :], send_sem=send_sems.at[6 * i + j],
                recv_sem=recv_sems.at[6 * i + j], device_id=(*chip, c), device_id_type=MESH)

        def d2d(i, j, k, half):
            return pltpu.make_async_remote_copy(
                src_ref=outs[i].at[:, k, half, :], dst_ref=outs[i].at[:, k, half, :], send_sem=send_sems.at[6 * i + 3 + j],
                recv_sem=recv_sems.at[6 * i + 3 + j], device_id=(x, y, 1 - c), device_id_type=MESH)

        return me, chips, halves, local, ici, d2d

    def start(srcs, outs, sems):
        me, chips, _, local, ici, _ = copies(srcs, outs, sems)
        for cp in local:
            cp.start()
        for i in range(n):
            for j, chip in enumerate(chips):
                ici(i, j, chip, me).start()

    def relay(srcs, outs, sems):
        me, chips, halves, local, ici, d2d = copies(srcs, outs, sems)
        for cp in local:
            cp.relay()
        for i in range(n):
            for j, (px, py) in enumerate(chips):
                k = 2 * px + py
                ici(i, j, (px, py), k).wait_recv()
                d2d(i, j, k, halves[i][0]).start()

    def wait(srcs, outs, sems):
        me, chips, halves, local, ici, d2d = copies(srcs, outs, sems)
        for i in range(n):
            for j, (px, py) in enumerate(chips):
                d2d(i, j, 2 * px + py, halves[i][1]).wait_recv()
        for i in range(n):
            for j, (px, py) in enumerate(chips):
                ici(i, j, (px, py), me).wait_send()
                d2d(i, j, 2 * px + py, halves[i][0]).wait_send()
        for cp in local:
            cp.finish()

    return _Exchange(
        list(shards), [jax.ShapeDtypeStruct((s.shape[0], 4) + s.shape[1:], s.dtype) for s in shards],
        [pltpu.SemaphoreType.DMA((6 * n,)), pltpu.SemaphoreType.DMA((6 * n,)), pltpu.SemaphoreType.DMA((n,))]
        + [pltpu.VMEM(s.shape, s.dtype) for s in shards], start, wait, relay)


def _rs_swap_halves(gs):
    n = len(gs)

    def copies(srcs, gots, sems):
        send_sems, recv_sems = sems
        x, y, c, _ = _place()
        return [pltpu.make_async_remote_copy(
            src_ref=srcs[i].at[:, :, _halves(c, gs[i].shape[2], 8)[1], :], dst_ref=gots[i], send_sem=send_sems.at[i],
            recv_sem=recv_sems.at[i], device_id=(x, y, 1 - c), device_id_type=MESH) for i in range(n)]

    def start(srcs, gots, sems):
        for cp in copies(srcs, gots, sems):
            cp.start()

    def wait(srcs, gots, sems):
        for cp in copies(srcs, gots, sems):
            cp.wait()

    return _Exchange(
        list(gs), [jax.ShapeDtypeStruct(g.shape[:2] + (g.shape[2] // 2, g.shape[3]), g.dtype) for g in gs],
        [pltpu.SemaphoreType.DMA((n,)), pltpu.SemaphoreType.DMA((n,))], start, wait)


def _rs_add_pair(g, got, cidx, tb, out_dtype, name):
    nl, _, rh, cols = got.shape
    nb = rh // tb

    def body(c_ref, g_ref, o_ref, s_ref):
        s_ref[...] = (g_ref[...] + o_ref[...]).astype(out_dtype)

    blk = lambda half: pl.BlockSpec((None, None, tb, cols), (lambda l, k, i, c: (l, k, c[0] * nb + i, 0)) if half
                                    else (lambda l, k, i, c: (l, k, i, 0)))
    return pl.pallas_call(
        body,
        grid_spec=pltpu.PrefetchScalarGridSpec(num_scalar_prefetch=1, grid=(nl, 4, nb),
                                               in_specs=[blk(True), blk(False)], out_specs=blk(False)),
        out_shape=jax.ShapeDtypeStruct(got.shape, out_dtype),
        compiler_params=_cparams(("arbitrary", "arbitrary", "arbitrary")), name=name)(cidx, g, got)


def _rs_exchange_chips(pairs):
    n = len(pairs)

    def copies(srcs, gots, sems):
        send_sems, recv_sems, local_sems, *stage = sems
        x, y, c, chips = _place()
        me = 2 * x + y
        local = [_StagedCopy(srcs[i].at[:, me], gots[i].at[:, me], stage[i], local_sems.at[i]) for i in range(n)]

        def ici(i, j, chip, frm, to):
            return pltpu.make_async_remote_copy(
                src_ref=srcs[i].at[:, to], dst_ref=gots[i].at[:, frm], send_sem=send_sems.at[3 * i + j],
                recv_sem=recv_sems.at[3 * i + j], device_id=(*chip, c), device_id_type=MESH)

        sent = [ici(i, j, (px, py), me, 2 * px + py) for i in range(n) for j, (px, py) in enumerate(chips)]
        recvd = [ici(i, j, (px, py), 2 * px + py, me) for i in range(n) for j, (px, py) in enumerate(chips)]
        return local, sent, recvd

    def start(srcs, gots, sems):
        local, sent, _ = copies(srcs, gots, sems)
        for cp in local + sent:
            cp.start()

    def relay(srcs, gots, sems):
        for cp in copies(srcs, gots, sems)[0]:
            cp.relay()

    def wait(srcs, gots, sems):
        local, sent, recvd = copies(srcs, gots, sems)
        for cp in recvd:
            cp.wait_recv()
        for cp in sent:
            cp.wait_send()
        for cp in local:
            cp.finish()

    return _Exchange(
        list(pairs), [jax.ShapeDtypeStruct(p.shape, p.dtype) for p in pairs],
        [pltpu.SemaphoreType.DMA((3 * n,)), pltpu.SemaphoreType.DMA((3 * n,)), pltpu.SemaphoreType.DMA((n,))]
        + [pltpu.VMEM((p.shape[0],) + p.shape[2:], p.dtype) for p in pairs], start, wait, relay)


def _rs_add_chips(got, tb, name):
    nl, _, rh, cols = got.shape

    def body(g_ref, s_ref):
        s_ref[...] = ((g_ref[0].astype(F32) + g_ref[1].astype(F32)) + g_ref[2].astype(F32)) + g_ref[3].astype(F32)

    return pl.pallas_call(
        body, grid=(nl, rh // tb),
        in_specs=[pl.BlockSpec((None, 4, tb, cols), lambda l, i: (l, 0, i, 0))],
        out_specs=pl.BlockSpec((None, tb, cols), lambda l, i: (l, i, 0)),
        out_shape=jax.ShapeDtypeStruct((nl, rh, cols), F32),
        compiler_params=_cparams(("arbitrary", "arbitrary")), name=name)(got)


def _rs_join_halves(groups):
    totals = [t for grp in groups for t in grp]
    where = [(gi, li) for gi, grp in enumerate(groups) for li in range(len(grp))]
    n = len(totals)

    def copies(srcs, outs, sems):
        send_sems, recv_sems, local_sems, *stage = sems
        x, y, c, _ = _place()
        local, send, recv = [], [], []
        for i, (gi, li) in enumerate(where):
            mine, other = _halves(c, 2 * totals[i].shape[1], 8)
            local.append(_StagedCopy(srcs[i], outs[gi].at[li:li + 1, mine, :], stage[i], local_sems.at[i]))
            for lst, rows in ((send, mine), (recv, other)):
                lst.append(pltpu.make_async_remote_copy(
                    src_ref=srcs[i], dst_ref=outs[gi].at[li:li + 1, rows, :], send_sem=send_sems.at[i],
                    recv_sem=recv_sems.at[i], device_id=(x, y, 1 - c), device_id_type=MESH))
        return local, send, recv

    def start(srcs, outs, sems):
        local, send, _ = copies(srcs, outs, sems)
        for cp in local + send:
            cp.start()

    def relay(srcs, outs, sems):
        for cp in copies(srcs, outs, sems)[0]:
            cp.relay()

    def wait(srcs, outs, sems):
        local, send, recv = copies(srcs, outs, sems)
        for cp in recv:
            cp.wait_recv()
        for cp in send:
            cp.wait_send()
        for cp in local:
            cp.finish()

    return _Exchange(
        totals, [jax.ShapeDtypeStruct((len(grp), 2 * grp[0].shape[1], grp[0].shape[2]), F32) for grp in groups],
        [pltpu.SemaphoreType.DMA((n,)), pltpu.SemaphoreType.DMA((n,)), pltpu.SemaphoreType.DMA((n,))]
        + [pltpu.VMEM(t.shape, F32) for t in totals], start, wait, relay)


def _rows(a):
    return a.reshape(-1, LANES)


def _pad_rows(a, mult):
    r = (-a.shape[0]) % mult
    return a if r == 0 else jnp.concatenate([a, jnp.zeros((r, a.shape[1]), a.dtype)], axis=0)


def kernel(x, norm_g, w_in, b_f, cf_dw, cf_dw_b, cf_ln_g, cf_ln_b, cf_pw, sc_dw, q_norm_g, k_norm_g, w_out, loss_target, m_norm_g, m_w_in, m_b_f, m_cf_dw, m_cf_dw_b, m_cf_ln_g, m_cf_ln_b, m_cf_pw, m_sc_dw, m_q_norm_g, m_k_norm_g, m_w_out, v_norm_g, v_w_in, v_b_f, v_cf_dw, v_cf_dw_b, v_cf_ln_g, v_cf_ln_b, v_cf_pw, v_sc_dw, v_q_norm_g, v_k_norm_g, v_w_out):
    bsz, seq, _ = x.shape
    t = bsz * seq

    taps = jnp.concatenate([cf_dw.reshape(-1), sc_dw.reshape(-1)])
    taps_bf = lax.bitcast_convert_type(taps, jnp.bfloat16).reshape(-1, LANES)
    n_pw, n_taps = 2 * 64 * DC // LANES, taps_bf.shape[0]
    small_w = _pad_rows(jnp.concatenate([_rows(cf_pw.astype(BF)), taps_bf], axis=0), 32)[None]
    win_b, wout_b = w_in.astype(BF), w_out.astype(BF)
    win0_g, small_g = _run_exchange(_all_gather_chips([win_b[0:1], small_w]), "all_gather_first")
    gather_rest = _all_gather_chips([win_b[1:DEPTH], wout_b])
    w_in_full = [_assemble_w_in(win0_g[0], "assemble_w_in_0")] + [None] * (DEPTH - 1)
    w_out_full = None
    cf_pw_full = jnp.concatenate([small_g[0, k, 0:n_pw].reshape(DEPTH, 64, DC) for k in range(4)], axis=1)
    taps_all = [lax.bitcast_convert_type(small_g[0, k, n_pw:n_pw + n_taps].reshape(-1, 2), F32) for k in range(4)]
    n_cfdw = DEPTH * CFW * 64
    cf_dw_full = jnp.concatenate([tk[:n_cfdw].reshape(DEPTH, CFW, 64) for tk in taps_all], axis=-1)
    sc_dw_full = jnp.concatenate([tk[n_cfdw:].reshape(DEPTH, SCW, 64) for tk in taps_all], axis=-1)

    bf_pad = jnp.pad(b_f, ((0, 0), (0, LANES - NH)))

    xs = [x.reshape(t, D)]
    saved = []
    dy = loss_part = None
    for l in range(DEPTH):
        xl = xs[-1]
        gq, gk = q_norm_g[l].reshape(1, DA), k_norm_g[l].reshape(1, DA)
        pcf, psc, pq, pk, vb, pz, pf = _fwd_inproj(xl, norm_g[l][None], w_in_full[l], f"fwd_inproj_{l}")
        b3 = lambda a: a.reshape(bsz, seq, a.shape[-1])
        ycs = _fwd_conv(b3(pcf), b3(psc), cf_dw_full[l], cf_dw_b[l][None], cf_ln_g[l][None], cf_ln_b[l][None],
                        cf_pw_full[l], sc_dw_full[l], f"fwd_conv_{l}")
        qs, kn, cq, crow8 = _fwd_attn_prep(b3(pq), b3(pk), b3(pf), gq, gk, bf_pad[l][None], f"fwd_attn_prep_{l}")
        crow = crow8.reshape(bsz, NH // 2, 2, seq)
        (o, ohp, yatt, lse), landed = _fwd_attn(qs, kn, b3(vb), cq, crow, b3(pz), f"fwd_attn_{l}",
                                                gather_rest if l == 0 else None)
        if l == 0:
            win_rest_g, wout_g = landed
            w_in_full[1:] = [_assemble_w_in(win_rest_g[i], f"assemble_w_in_{i + 1}") for i in range(DEPTH - 1)]
            w_out_full = wout_g.reshape(DEPTH, D, D)
        ycs2, yatt2 = ycs.reshape(t, 512), yatt.reshape(t, DA)
        if l + 1 < DEPTH:
            xs.append(_fwd_outproj(xl, ycs2, yatt2, w_out_full[l], f"fwd_outproj_{l}"))
        else:
            dy, loss_part = _fwd_outproj_loss(xl, ycs2, yatt2, w_out_full[l], loss_target.reshape(t, D),
                                              f"fwd_outproj_loss_{l}")
        saved.append((pcf, psc, pq, pk, vb, pz, pf, ycs2, yatt2, o, ohp, qs, kn, cq, crow, lse))

    grads = [None] * DEPTH
    reduced = [None] * DEPTH
    cidx = lax.axis_index("c").astype(jnp.int32).reshape(1)
    big_tiles = [256, 128]

    def big_packs(gl):
        return [gl["w_in"][None], gl["w_out"].reshape(1, 4, 256, D)]

    def add_pairs(packs, gots, tiles, dtypes, tag):
        return [_rs_add_pair(p, got, cidx, tb, dt, f"rs_add_pair_{tag}_{i}")
                for i, (p, got, tb, dt) in enumerate(zip(packs, gots, tiles, dtypes))]

    def add_chips(gots2, tiles, tag):
        return [_rs_add_chips(got, tb, f"rs_add_chips_{tag}_{i}") for i, (got, tb) in enumerate(zip(gots2, tiles))]

    for l in reversed(range(DEPTH)):
        pcf, psc, pq, pk, vb, pz, pf, ycs2, yatt2, o, ohp, qs, kn, cq, crow, lse = saved[l]
        gq, gk = q_norm_g[l].reshape(1, DA), k_norm_g[l].reshape(1, DA)
        b3 = lambda a: a.reshape(bsz, seq, a.shape[-1])
        f2 = lambda a: a.reshape(t, a.shape[-1])
        packs = big_packs(grads[l + 1]) if l + 1 < DEPTH else None
        (dycs, do, dz, dl, d_wout), gots = _bwd_outproj(
            dy, ycs2, yatt2, f2(o), f2(ohp), pz, w_out_full[l], f"bwd_outproj_{l}",
            _rs_swap_halves(packs) if packs else None)
        pairs = add_pairs(packs, gots, big_tiles, [BF, BF], l + 1) if packs else None
        (dqs, dkn, dv, dcrow), gots2 = _bwd_attn(qs, kn, b3(vb), b3(do), cq, lse, b3(dl), crow, f"bwd_attn_{l}",
                                                 _rs_exchange_chips(pairs) if packs else None)
        if packs:
            reduced[l + 1] = add_chips(gots2, big_tiles, l + 1)
        (dpq, dpk, dpf, d_gq, d_gk, d_bf), _ = _bwd_attn_post(
            b3(pq), b3(pk), dqs, dkn, b3(pf), dcrow.reshape(bsz, NH, seq), gq, gk, bf_pad[l][None],
            f"bwd_attn_post_{l}")
        dcf, dsc, d_cfdw, d_cfb, d_lng, d_lnb, d_pw, d_scdw = _bwd_conv(
            b3(pcf), b3(psc), b3(dycs), cf_dw_full[l], cf_dw_b[l][None], cf_ln_g[l][None], cf_ln_b[l][None],
            cf_pw_full[l], sc_dw_full[l], f"bwd_conv_{l}")
        dy, d_ng, d_win = _bwd_inproj(xs[l], norm_g[l][None], dy, w_in_full[l], f2(dcf), f2(dsc), f2(dpq), f2(dpk),
                                      f2(dv), dz, f2(dpf), f"bwd_inproj_{l}")
        grads[l] = dict(norm_g=d_ng[0], w_in=d_win, b_f=d_bf[0, :NH], cf_dw=d_cfdw, cf_dw_b=d_cfb[0],
                        cf_ln_g=d_lng[0], cf_ln_b=d_lnb[0], cf_pw=d_pw, sc_dw=d_scdw,
                        q_norm_g=d_gq.reshape(NH, HD), k_norm_g=d_gk.reshape(NH, HD), w_out=d_wout)
    grad_x = dy.reshape(bsz, seq, D)
    gw = {n: jnp.stack([grads[l][n] for l in range(DEPTH)]) for n in grads[0] if n not in ("w_in", "w_out")}

    rep_names = ("norm_g", "b_f", "cf_dw_b", "cf_ln_g", "cf_ln_b", "q_norm_g", "k_norm_g")
    rep = jnp.concatenate([jnp.pad(gw[n].reshape(-1), (0, (-gw[n].size) % LANES)) for n in rep_names]
                          + [jnp.pad(loss_part.reshape(-1), (0, LANES - 1))]).reshape(-1, LANES)
    blocks = []
    for k in range(4):
        small = jnp.concatenate([gw["cf_dw"][:, :, 64 * k:64 * (k + 1)].reshape(-1),
                                 gw["sc_dw"][:, :, 64 * k:64 * (k + 1)].reshape(-1)]).reshape(-1, LANES)
        blocks.append(_pad_rows(jnp.concatenate([_rows(gw["cf_pw"][:, 64 * k:64 * (k + 1), :]), small, rep], axis=0), 16))
    g_small = jnp.stack(blocks)[None]
    packs = big_packs(grads[0]) + [g_small]
    tiles = big_tiles + [g_small.shape[2] // 2]
    gots = _run_exchange(_rs_swap_halves(packs), "rs_swap_halves_last")
    pairs = add_pairs(packs, gots, tiles, [BF, BF, F32], 0)
    gots2 = _run_exchange(_rs_exchange_chips(pairs), "rs_exchange_chips_last")
    tot_win0, tot_wout0, tot_small = add_chips(gots2, tiles, 0)
    reduced[0] = (tot_win0, tot_wout0)
    red_win, red_wout, red_small = _run_exchange(
        _rs_join_halves([[reduced[l][0] for l in range(DEPTH)], [reduced[l][1] for l in range(DEPTH)], [tot_small]]),
        "rs_join_halves")

    n_small = (DEPTH * CFW * 64 + DEPTH * SCW * 64) // LANES
    red = red_small[0]
    pos = [0]

    def take(nrows):
        pos[0] += nrows
        return red[pos[0] - nrows:pos[0]]

    g = {}
    g["w_in"] = red_win
    g["w_out"] = red_wout
    g["cf_pw"] = take(n_pw).reshape(DEPTH, 64, DC)
    small = take(n_small).reshape(-1)
    g["cf_dw"] = small[:n_cfdw].reshape(DEPTH, CFW, 64)
    g["sc_dw"] = small[n_cfdw:].reshape(DEPTH, SCW, 64)
    shapes = dict(norm_g=norm_g.shape, b_f=b_f.shape, cf_dw_b=cf_dw_b.shape, cf_ln_g=cf_ln_g.shape,
                  cf_ln_b=cf_ln_b.shape, q_norm_g=q_norm_g.shape, k_norm_g=k_norm_g.shape)
    for n in rep_names:
        size = 1
        for s in shapes[n]:
            size *= s
        g[n] = take(-(-size // LANES)).reshape(-1)[:size].reshape(shapes[n])
    loss = take(1)[0, 0]

    order = ("norm_g", "w_in", "b_f", "cf_dw", "cf_dw_b", "cf_ln_g", "cf_ln_b", "cf_pw", "sc_dw", "q_norm_g",
             "k_norm_g", "w_out")
    weights = dict(norm_g=norm_g, w_in=w_in, b_f=b_f, cf_dw=cf_dw, cf_dw_b=cf_dw_b, cf_ln_g=cf_ln_g, cf_ln_b=cf_ln_b,
                   cf_pw=cf_pw, sc_dw=sc_dw, q_norm_g=q_norm_g, k_norm_g=k_norm_g, w_out=w_out)
    ms = dict(norm_g=m_norm_g, w_in=m_w_in, b_f=m_b_f, cf_dw=m_cf_dw, cf_dw_b=m_cf_dw_b, cf_ln_g=m_cf_ln_g,
              cf_ln_b=m_cf_ln_b, cf_pw=m_cf_pw, sc_dw=m_sc_dw, q_norm_g=m_q_norm_g, k_norm_g=m_k_norm_g, w_out=m_w_out)
    vs = dict(norm_g=v_norm_g, w_in=v_w_in, b_f=v_b_f, cf_dw=v_cf_dw, cf_dw_b=v_cf_dw_b, cf_ln_g=v_cf_ln_g,
              cf_ln_b=v_cf_ln_b, cf_pw=v_cf_pw, sc_dw=v_sc_dw, q_norm_g=v_q_norm_g, k_norm_g=v_k_norm_g, w_out=v_w_out)
    delta, new_m, new_v = {}, {}, {}
    two_d = lambda a: a.reshape(-1, a.shape[-1])
    for n in ("w_in", "w_out"):
        outs = _adamw(two_d(weights[n]), two_d(g[n]), two_d(ms[n]), two_d(vs[n]), 256, f"adamw_{n}")
        delta[n], new_m[n], new_v[n], g[n] = (a.reshape(weights[n].shape) for a in outs)
    small_names = [n for n in order if n not in ("w_in", "w_out")]
    outs = _adamw_many([weights[n] for n in small_names], [g[n] for n in small_names],
                       [ms[n] for n in small_names], [vs[n] for n in small_names], "adamw_small")
    for i, n in enumerate(small_names):
        delta[n], new_m[n], new_v[n] = outs[3 * i:3 * i + 3]

    return (loss, grad_x, *[g[n] for n in order], *[delta[n] for n in order], *[new_m[n] for n in order],
            *[new_v[n] for n in order])
```
